```python
import math
import jax
import jax.numpy as jnp
from jax import lax
import numpy as np

D_MODEL = 1024
BATCH = 8
SEQ = 2048
DEPTH = 2

EPS = 1e-6
GRID_W = 64

SSD_HEADS = 16
SSD_HEAD_DIM = 64
SSD_D_INNER = SSD_HEADS * SSD_HEAD_DIM
SSD_GROUPS = 2
SSD_STATE = 128
SSD_CONV = 5
SSD_CONV_DIM = SSD_D_INNER + 2 * SSD_GROUPS * SSD_STATE
SSD_CHUNK = 128

GLA_HEADS = 4
GLA_DK = 128
GLA_DV = 256
GLA_KEY_W = GLA_HEADS * GLA_DK
GLA_VAL_W = GLA_HEADS * GLA_DV
GLA_GATE_RANK = 16
GLA_GATE_NORM = 16.0
GLA_CHUNK = 64

NA_HEADS = 16
NA_HEAD_DIM = 64
NA_W = NA_HEADS * NA_HEAD_DIM
NA_WIN_H = 8
NA_WIN_W = 16
NA_QB = 16
NA_KW = NA_QB + NA_WIN_W

N_BRANCH = 3
D_FF = 4 * D_MODEL

IN_SIZES = (SSD_D_INNER, SSD_CONV_DIM, SSD_HEADS, SSD_HEADS,
            GLA_KEY_W, GLA_KEY_W, GLA_VAL_W, GLA_VAL_W, GLA_GATE_RANK, GLA_GATE_RANK,
            NA_W, NA_W, NA_W, N_BRANCH * D_MODEL)
N_IN = sum(IN_SIZES)

kernel_name = 'hybrid_ssd_gla_na_encoder'


def _rms(x):
    xf = x.astype(jnp.float32)
    return (xf * lax.rsqrt(jnp.mean(jnp.square(xf), axis=-1, keepdims=True) + EPS)).astype(x.dtype)


def _rev(a):
    return jnp.flip(a, axis=1)


def dwconv_centred(x, w, b):
    pad = w.shape[0] // 2
    y = lax.conv_general_dilated(x, w[:, None, :], window_strides=(1,), padding=[(pad, pad)],
                                 dimension_numbers=('NWC', 'WIO', 'NWC'),
                                 feature_group_count=x.shape[-1])
    return y + b


def ssd_chunked(x, dt, A, Bm, Cm):
    Bsz, S, H, P = x.shape
    G, N = Bm.shape[-2:]
    HG = H // G
    L = SSD_CHUNK
    nc = S // L
    x = x.reshape(Bsz, nc, L, G, HG, P)
    dt = dt.reshape(Bsz, nc, L, G, HG)
    Bm = Bm.reshape(Bsz, nc, L, G, N)
    Cm = Cm.reshape(Bsz, nc, L, G, N)
    a_cum = jnp.cumsum(dt * A.reshape(G, HG), axis=2)
    a_last = a_cum[:, :, -1]
    xdt = x * dt[..., None]
    ac = jnp.moveaxis(a_cum, 2, -1)
    seg = ac[..., :, None] - ac[..., None, :]
    tril = jnp.tril(jnp.ones((L, L), dtype=bool))
    decay = jnp.exp(jnp.where(tril, seg, -jnp.inf))
    cb = jnp.einsum('bclgn,bcsgn->bcgls', Cm, Bm)
    y_diag = jnp.einsum('bcghls,bcsghp->bclghp', cb[:, :, :, None] * decay, xdt)
    w_state = jnp.exp(a_last[:, :, None] - a_cum)
    states = jnp.einsum('bclgn,bclghp->bcghpn', Bm, xdt * w_state[..., None])

    def step(s, inp):
        st, dec = inp
        return s * dec[..., None, None] + st, s

    s0 = jnp.zeros((Bsz, G, HG, P, N), dtype=states.dtype)
    _, s_prev = lax.scan(step, s0, (jnp.moveaxis(states, 1, 0), jnp.moveaxis(jnp.exp(a_last), 1, 0)))
    s_prev = jnp.moveaxis(s_prev, 0, 1)
    y_off = jnp.einsum('bclgn,bcghpn->bclghp', Cm, s_prev) * jnp.exp(a_cum)[..., None]
    return (y_diag + y_off).reshape(Bsz, S, H, P)


def gla_chunked(q, k, v, g):
    Bsz, S, H, Kd = q.shape
    Vd = v.shape[-1]
    L = GLA_CHUNK
    nc = S // L
    q = q.reshape(Bsz, nc, L, H, Kd) * (Kd ** -0.5)
    k = k.reshape(Bsz, nc, L, H, Kd)
    v = v.reshape(Bsz, nc, L, H, Vd)
    b = jnp.cumsum(g.reshape(Bsz, nc, L, H, Kd), axis=2)
    b_last = b[:, :, -1]
    q_dec = q * jnp.exp(b)
    att = jnp.einsum('bclhk,bcshk->bchls', q_dec, k * jnp.exp(-b))
    tril = jnp.tril(jnp.ones((L, L), dtype=bool))
    att = jnp.where(tril, att, 0.0)
    o_intra = jnp.einsum('bchls,bcshv->bclhv', att, v)
    states = jnp.einsum('bclhk,bclhv->bchkv', k * jnp.exp(b_last[:, :, None] - b), v)

    def step(s, inp):
        st, dec = inp
        return s * dec[..., None] + st, s

    s0 = jnp.zeros((Bsz, H, Kd, Vd), dtype=states.dtype)
    _, s_prev = lax.scan(step, s0, (jnp.moveaxis(states, 1, 0), jnp.moveaxis(jnp.exp(b_last), 1, 0)))
    s_prev = jnp.moveaxis(s_prev, 0, 1)
    o_inter = jnp.einsum('bclhk,bchkv->bclhv', q_dec, s_prev)
    return (o_intra + o_inter).reshape(Bsz, S, H, Vd)


def _na_col_layout():
    ncb = GRID_W // NA_QB
    cb = np.arange(ncb)
    k_start = np.clip(cb * NA_QB - NA_WIN_W // 2, 0, GRID_W - NA_KW)
    kcol = k_start[:, None] + np.arange(NA_KW)
    qcol = cb[:, None] * NA_QB + np.arange(NA_QB)
    w_start = np.clip(qcol - NA_WIN_W // 2, 0, GRID_W - NA_WIN_W)
    rel = kcol[:, None, :] - w_start[:, :, None]
    mask = (rel >= 0) & (rel < NA_WIN_W)
    off = np.clip(kcol[:, None, :] - qcol[:, :, None], -(NA_WIN_W - 1), NA_WIN_W - 1) + (NA_WIN_W - 1)
    return kcol, mask, off


def na_2d(q, k, v, rpb):
    Bsz, S, H, Dh = q.shape
    rows = S // GRID_W
    win_h = min(NA_WIN_H, rows)
    ncb = GRID_W // NA_QB
    kcol, col_mask, col_off = _na_col_layout()
    qg = q.reshape(Bsz, rows, GRID_W, H, Dh) * (Dh ** -0.5)
    kg = k.reshape(Bsz, rows, GRID_W, H, Dh)
    vg = v.reshape(Bsz, rows, GRID_W, H, Dh)
    q_rows = jnp.moveaxis(qg, 1, 0).reshape(rows, Bsz, ncb, NA_QB, H, Dh)

    def one_row(args):
        q_r, r = args
        r0 = jnp.clip(r - win_h // 2, 0, rows - win_h)
        k_blk = lax.dynamic_slice_in_dim(kg, r0, win_h, axis=1)[:, :, kcol]
        v_blk = lax.dynamic_slice_in_dim(vg, r0, win_h, axis=1)[:, :, kcol]
        s = jnp.einsum('bcqhd,bwckhd->bhcqwk', q_r, k_blk).astype(jnp.float32)
        row_off = r0 + jnp.arange(win_h) - r + (NA_WIN_H - 1)
        bias = rpb[:, row_off][:, :, col_off]
        s = s + jnp.transpose(bias, (0, 2, 3, 1, 4)).astype(jnp.float32)
        s = jnp.where(col_mask[:, :, None, :], s, -jnp.inf)
        p = jax.nn.softmax(s.reshape(s.shape[:4] + (-1,)), axis=-1).reshape(s.shape).astype(v.dtype)
        o = jnp.einsum('bhcqwk,bwckhd->bcqhd', p, v_blk)
        return o.reshape(Bsz, GRID_W, H, Dh)

    out = lax.map(one_row, (q_rows, jnp.arange(rows)))
    return jnp.moveaxis(out, 0, 1).reshape(Bsz, S, H, Dh)


def setup_inputs(seed: int = 0) -> dict:
    key = jax.random.key(seed)
    ks = jax.random.split(key, 32)
    L = DEPTH
    D = D_MODEL

    def nrm(k, shape, scale):
        return jax.random.normal(k, shape, jnp.float32) * scale

    def gain(k, shape):
        return 1.0 + 0.02 * jax.random.normal(k, shape, jnp.float32)

    def dt_bias(k):
        dt = jnp.exp(jax.random.uniform(k, (L, SSD_HEADS), jnp.float32,
                                        minval=math.log(1e-3), maxval=math.log(1e-1)))
        return dt + jnp.log(-jnp.expm1(-dt))

    def a_log(k):
        return jnp.log(jax.random.uniform(k, (L, SSD_HEADS), jnp.float32, minval=1.0, maxval=16.0))

    return {
        'x': nrm(ks[0], (BATCH, SEQ, D), 1.0),
        'norm_mix_w': gain(ks[1], (L, D)),
        'w_in': nrm(ks[2], (L, D, N_IN), D ** -0.5),
        'ssd_conv_w': nrm(ks[3], (L, SSD_CONV, SSD_CONV_DIM), SSD_CONV ** -0.5),
        'ssd_conv_b': nrm(ks[4], (L, SSD_CONV_DIM), 0.02),
        'ssd_dt_bias_f': dt_bias(ks[5]),
        'ssd_dt_bias_b': dt_bias(ks[6]),
        'ssd_a_log_f': a_log(ks[7]),
        'ssd_a_log_b': a_log(ks[8]),
        'ssd_d': gain(ks[9], (L, SSD_HEADS)),
        'ssd_norm_w': gain(ks[10], (L, SSD_D_INNER)),
        'gla_a2_f': nrm(ks[11], (L, GLA_GATE_RANK, GLA_KEY_W), GLA_GATE_RANK ** -0.5),
        'gla_a2_bias_f': nrm(ks[12], (L, GLA_KEY_W), 0.1),
        'gla_a2_b': nrm(ks[13], (L, GLA_GATE_RANK, GLA_KEY_W), GLA_GATE_RANK ** -0.5),
        'gla_a2_bias_b': nrm(ks[14], (L, GLA_KEY_W), 0.1),
        'gla_norm_w': gain(ks[15], (L, GLA_DV)),
        'na_q_norm_w': gain(ks[16], (L, NA_HEAD_DIM)),
        'na_k_norm_w': gain(ks[17], (L, NA_HEAD_DIM)),
        'na_rpb': nrm(ks[18], (L, NA_HEADS, 2 * NA_WIN_H - 1, 2 * NA_WIN_W - 1), 0.02),
        'w_branch_ssd': nrm(ks[19], (L, SSD_D_INNER, D), SSD_D_INNER ** -0.5),
        'w_branch_gla': nrm(ks[20], (L, GLA_VAL_W, D), GLA_VAL_W ** -0.5),
        'w_branch_na': nrm(ks[21], (L, NA_W, D), NA_W ** -0.5),
        'w_out': nrm(ks[22], (L, D, D), D ** -0.5),
        'norm_mlp_w': gain(ks[23], (L, D)),
        'w_ff1': nrm(ks[24], (L, D, D_FF), D ** -0.5),
        'w_ff2': nrm(ks[25], (L, D_FF, D), D_FF ** -0.5),
    }


def reference(x, norm_mix_w, w_in, ssd_conv_w, ssd_conv_b, ssd_dt_bias_f, ssd_dt_bias_b,
              ssd_a_log_f, ssd_a_log_b, ssd_d, ssd_norm_w, gla_a2_f, gla_a2_bias_f, gla_a2_b,
              gla_a2_bias_b, gla_norm_w, na_q_norm_w, na_k_norm_w, na_rpb, w_branch_ssd,
              w_branch_gla, w_branch_na, w_out, norm_mlp_w, w_ff1, w_ff2):
    Bsz, S, D = x.shape
    splits = [int(i) for i in np.cumsum(IN_SIZES)[:-1]]
    for l in range(DEPTH):
        h = _rms(x) * norm_mix_w[l]
        u = h @ w_in[l]
        (z, xbc, dt_f, dt_b, gq, gk, gv, gg, ga_f, ga_b,
         nq, nk, nv, gate_logits) = jnp.split(u, splits, axis=-1)

        xbc = jax.nn.silu(dwconv_centred(xbc, ssd_conv_w[l], ssd_conv_b[l]))
        xs, Bm, Cm = jnp.split(xbc, [SSD_D_INNER, SSD_D_INNER + SSD_GROUPS * SSD_STATE], axis=-1)
        xs = xs.reshape(Bsz, S, SSD_HEADS, SSD_HEAD_DIM)
        Bm = Bm.reshape(Bsz, S, SSD_GROUPS, SSD_STATE)
        Cm = Cm.reshape(Bsz, S, SSD_GROUPS, SSD_STATE)
        dtf = jax.nn.softplus(dt_f + ssd_dt_bias_f[l])
        dtb = jax.nn.softplus(dt_b + ssd_dt_bias_b[l])
        y_f = ssd_chunked(xs, dtf, -jnp.exp(ssd_a_log_f[l]), Bm, Cm)
        y_b = _rev(ssd_chunked(_rev(xs), _rev(dtb), -jnp.exp(ssd_a_log_b[l]), _rev(Bm), _rev(Cm)))
        y = (y_f + y_b + xs * ssd_d[l][:, None]).reshape(Bsz, S, SSD_D_INNER) * jax.nn.silu(z)
        y = _rms(y.reshape(Bsz, S, SSD_GROUPS, -1)).reshape(Bsz, S, SSD_D_INNER) * ssd_norm_w[l]
        br_ssd = y @ w_branch_ssd[l]

        q = gq.reshape(Bsz, S, GLA_HEADS, GLA_DK)
        k = gk.reshape(Bsz, S, GLA_HEADS, GLA_DK)
        v = gv.reshape(Bsz, S, GLA_HEADS, GLA_DV)
        g_f = (jax.nn.log_sigmoid(ga_f @ gla_a2_f[l] + gla_a2_bias_f[l]) / GLA_GATE_NORM
               ).reshape(Bsz, S, GLA_HEADS, GLA_DK)
        g_b = (jax.nn.log_sigmoid(ga_b @ gla_a2_b[l] + gla_a2_bias_b[l]) / GLA_GATE_NORM
               ).reshape(Bsz, S, GLA_HEADS, GLA_DK)
        o = gla_chunked(q, k, v, g_f) + _rev(gla_chunked(_rev(q), _rev(k), _rev(v), _rev(g_b)))
        o = _rms(o) * gla_norm_w[l] * jax.nn.silu(gg).reshape(Bsz, S, GLA_HEADS, GLA_DV)
        br_gla = o.reshape(Bsz, S, GLA_VAL_W) @ w_branch_gla[l]

        qn = _rms(nq.reshape(Bsz, S, NA_HEADS, NA_HEAD_DIM)) * na_q_norm_w[l]
        kn = _rms(nk.reshape(Bsz, S, NA_HEADS, NA_HEAD_DIM)) * na_k_norm_w[l]
        vn = nv.reshape(Bsz, S, NA_HEADS, NA_HEAD_DIM)
        br_na = na_2d(qn, kn, vn, na_rpb[l]).reshape(Bsz, S, NA_W) @ w_branch_na[l]

        gates = jax.nn.sigmoid(gate_logits).reshape(Bsz, S, N_BRANCH, D)
        mixed = gates[:, :, 0] * br_ssd + gates[:, :, 1] * br_gla + gates[:, :, 2] * br_na
        x = x + mixed @ w_out[l]

        hm = _rms(x) * norm_mlp_w[l]
        x = x + jnp.square(jax.nn.relu(hm @ w_ff1[l])) @ w_ff2[l]
    return x
```

```python
import functools

import jax
import jax.numpy as jnp
import numpy as np
from jax import lax
from jax.experimental import pallas as pl
from jax.experimental.pallas import tpu as pltpu

F32 = jnp.float32
BF16 = jnp.bfloat16

EPS = 1e-6
D_MODEL = 1024
GRID_W = 64

SSD_HEADS = 16
SSD_HEAD_DIM = 64
SSD_D_INNER = 1024
SSD_GROUPS = 2
SSD_STATE = 128
SSD_CONV = 5
SSD_CONV_DIM = 1536
SSD_CHUNK = 128
SSD_GROUP_W = SSD_D_INNER // SSD_GROUPS
SSD_HG = SSD_HEADS // SSD_GROUPS

GLA_HEADS = 4
GLA_DK = 128
GLA_DV = 256
GLA_KEY_W = 512
GLA_VAL_W = 1024
GLA_GATE_RANK = 16
GLA_GATE_NORM = 16.0
GLA_CHUNK = 64

NA_HEADS = 16
NA_HEAD_DIM = 64
NA_W = 1024
NA_WIN_H = 8
NA_WIN_W = 16

N_BRANCH = 3
D_FF = 4096

IN_SIZES = (SSD_D_INNER, SSD_CONV_DIM, SSD_HEADS, SSD_HEADS,
            GLA_KEY_W, GLA_KEY_W, GLA_VAL_W, GLA_VAL_W, GLA_GATE_RANK, GLA_GATE_RANK,
            NA_W, NA_W, NA_W, N_BRANCH * D_MODEL)
_IN_OFF = np.concatenate([[0], np.cumsum(IN_SIZES)])
(_O_Z, _O_XBC, _O_DTF, _O_DTB, _O_GQ, _O_GK, _O_GV, _O_GG, _O_GAF, _O_GAB,
 _O_NQ, _O_NK, _O_NV, _O_GATE) = [int(v) for v in _IN_OFF[:-1]]

U_GATE = 0
U_NQ = 3072
U_NK = 4096
U_NV = 5120
U_Z = 6144
U_GV = 7168
U_GG = 8192
U_XBC = 9216
U_GQ = 10752
U_GK = 11264
U_WIDTH = 11776
LANE = 128
VMEM_LIMIT = 56 * 1024 * 1024

SMALL_W = SSD_GROUPS * LANE
SM_DTF, SM_DTB, SM_GAF, SM_GAB = 0, 8, 16, 32

_BIG_PERM = np.concatenate([
    np.arange(_O_GATE, _O_GATE + 3072),
    np.arange(_O_NQ, _O_NQ + 3072),
    np.arange(_O_Z, _O_Z + 1024),
    np.arange(_O_GV, _O_GV + 2048),
    np.arange(_O_XBC, _O_XBC + 1536),
    np.arange(_O_GQ, _O_GQ + 1024),
])
def _small_layout():
    src = np.full((SMALL_W,), -1, np.int64)
    for g in range(SSD_GROUPS):
        base = g * LANE
        heads = np.arange(SSD_HG) + g * SSD_HG
        src[base + SM_DTF:base + SM_DTF + SSD_HG] = _O_DTF + heads
        src[base + SM_DTB:base + SM_DTB + SSD_HG] = _O_DTB + heads
    src[SM_GAF:SM_GAF + GLA_GATE_RANK] = _O_GAF + np.arange(GLA_GATE_RANK)
    src[SM_GAB:SM_GAB + GLA_GATE_RANK] = _O_GAB + np.arange(GLA_GATE_RANK)
    return src


_SMALL_SRC = _small_layout()


def _cparams(sem, vmem=VMEM_LIMIT):
    return pltpu.CompilerParams(dimension_semantics=sem, vmem_limit_bytes=vmem)


def _sigmoid(x):
    return 1.0 / (1.0 + jnp.exp(-x))


def _silu(x):
    return x * _sigmoid(x)


def _softplus(x):
    return jnp.maximum(x, 0.0) + jnp.log1p(jnp.exp(-jnp.abs(x)))


def _log_sigmoid(x):
    return -_softplus(-x)


def _nt(a, b):
    return lax.dot_general(a, b, (((1,), (1,)), ((), ())), preferred_element_type=F32)


def _tn(a, b):
    return lax.dot_general(a, b, (((0,), (0,)), ((), ())), preferred_element_type=F32)


def _dot(a, b):
    return jnp.dot(a, b, preferred_element_type=F32)


def _dot_exact(a, b):
    return jnp.dot(a, b, preferred_element_type=F32, precision=lax.Precision.HIGHEST)


def _iota2(shape, dim):
    return lax.broadcasted_iota(jnp.int32, shape, dim)


def _inproj_kernel(x_ref, nw_ref, w_ref, ws_ref, wst_ref, u_ref, us_ref, ust_ref, h_ref):
    @pl.when(pl.program_id(1) == 0)
    def _():
        x = x_ref[...]
        ms = jnp.mean(x * x, axis=-1, keepdims=True)
        h = (x * lax.rsqrt(ms + EPS) * nw_ref[...]).astype(BF16)
        h_ref[...] = h
        us_ref[...] = _dot(h, ws_ref[...])
        ust_ref[...] = _nt(wst_ref[...], h)

    u_ref[...] = _dot(h_ref[...], w_ref[...]).astype(BF16)


def _inproj(x2, nw, w_big, w_small, w_small_t, tm=1024, tn=512):
    T = x2.shape[0]
    return pl.pallas_call(
        _inproj_kernel,
        grid=(T // tm, U_WIDTH // tn),
        in_specs=[
            pl.BlockSpec((tm, D_MODEL), lambda i, j: (i, 0)),
            pl.BlockSpec((1, D_MODEL), lambda i, j: (0, 0)),
            pl.BlockSpec((D_MODEL, tn), lambda i, j: (0, j)),
            pl.BlockSpec((D_MODEL, SMALL_W), lambda i, j: (0, 0)),
            pl.BlockSpec((SMALL_W, D_MODEL), lambda i, j: (0, 0)),
        ],
        out_specs=[
            pl.BlockSpec((tm, tn), lambda i, j: (i, j)),
            pl.BlockSpec((tm, SMALL_W), lambda i, j: (i, 0)),
            pl.BlockSpec((SMALL_W, tm), lambda i, j: (0, i)),
        ],
        out_shape=[
            jax.ShapeDtypeStruct((T, U_WIDTH), BF16),
            jax.ShapeDtypeStruct((T, SMALL_W), F32),
            jax.ShapeDtypeStruct((SMALL_W, T), F32),
        ],
        scratch_shapes=[pltpu.VMEM((tm, D_MODEL), BF16)],
        compiler_params=_cparams(("parallel", "arbitrary")),
        name="inproj",
    )(x2, nw, w_big, w_small, w_small_t)


def _conv_kernel(u_ref, w_ref, b_ref, o_ref):
    x = u_ref[0].astype(F32)
    S = x.shape[0]
    row = _iota2(x.shape, 0)
    acc = jnp.zeros_like(x) + b_ref[...]
    pad = SSD_CONV // 2
    for k in range(SSD_CONV):
        d = k - pad
        if d == 0:
            xs = x
        else:
            xs = pltpu.roll(x, (-d) % S, axis=0)
            valid = (row + d >= 0) & (row + d < S)
            xs = jnp.where(valid, xs, 0.0)
        acc = acc + w_ref[k:k + 1, :] * xs
    o_ref[0] = _silu(acc).astype(BF16)


def _conv(u3, conv_w, conv_b, tc=256):
    B, S, _ = u3.shape
    nblk = SSD_CONV_DIM // tc
    off = U_XBC // tc
    return pl.pallas_call(
        _conv_kernel,
        grid=(B, nblk),
        in_specs=[
            pl.BlockSpec((1, S, tc), lambda b, c: (b, 0, off + c)),
            pl.BlockSpec((SSD_CONV, tc), lambda b, c: (0, c)),
            pl.BlockSpec((1, tc), lambda b, c: (0, c)),
        ],
        out_specs=pl.BlockSpec((1, S, tc), lambda b, c: (b, 0, c)),
        out_shape=jax.ShapeDtypeStruct((B, S, SSD_CONV_DIM), BF16),
        compiler_params=_cparams(("parallel", "parallel")),
        name="ssd_conv",
    )(u3, conv_w, conv_b)


def _split_hi_lo(v):
    hi = v.astype(BF16)
    lo = (v - hi.astype(F32)).astype(BF16)
    return jnp.concatenate([hi, lo], axis=1)


def _ssd_kernel(x_ref, b_ref, c_ref, z_ref, us_ref, ust_ref, prow_ref, pcol_ref, drow_ref, nw_ref,
                o_ref, acc_ref, st_ref):
    L = SSD_CHUNK
    S = x_ref.shape[1]
    nc = S // L
    W = SSD_GROUP_W

    ii = _iota2((L, L), 0)
    jj = _iota2((L, L), 1)
    tril = ii >= jj
    triu = jj >= ii
    tril_f = tril.astype(F32)
    triu_f = triu.astype(F32)

    er = _iota2((LANE, W), 0)
    ec = _iota2((LANE, W), 1) // SSD_HEAD_DIM
    e_f = (er == ec + SM_DTF).astype(BF16)
    e_b = (er == ec + SM_DTB).astype(BF16)
    e2_f = jnp.concatenate([e_f, e_f], axis=0)
    e2_b = jnp.concatenate([e_b, e_b], axis=0)

    bias_row = prow_ref[0:1, :]
    a_row = prow_ref[1:2, :]
    bias_col = pcol_ref[0, :, 0:1]
    a_col = pcol_ref[0, :, 1:2]

    lane_half = _iota2((L, LANE), 1) < SSD_HEAD_DIM
    row0 = _iota2((16, LANE), 0) == 0

    def chunk_terms(c):
        r0 = pl.multiple_of(c * L, L)
        dt_c = _softplus(us_ref[0, pl.ds(r0, L), :] + bias_row)
        a_c = dt_c * a_row
        p_c = _dot_exact(tril_f, a_c)
        tot_c = p_c[L - 1:L, :]
        dt_r = _softplus(ust_ref[:, pl.ds(r0, L)] + bias_col)
        a_r = dt_r * a_col
        p_r = _dot_exact(a_r, triu_f)
        tot_r = p_r[:, L - 1:L]
        return r0, dt_c, a_c, p_c, tot_c, dt_r, a_r, p_r, tot_r

    st_ref[...] = jnp.zeros_like(st_ref)

    def fwd_body(c, carry):
        r0, dt_c, a_c, p_c, tot_c, dt_r, a_r, p_r, tot_r = chunk_terms(c)
        suf_r = tot_r - p_r + a_r
        suf_c = tot_c - p_c + a_c
        x_c = x_ref[0, pl.ds(r0, L), :]
        b_c = b_ref[0, pl.ds(r0, L), :]
        c_c = c_ref[0, pl.ds(r0, L), :]
        cb = _nt(c_c, b_c)
        for hp in range(SSD_HG // 2):
            ms = []
            for hh in range(2):
                hf = SM_DTF + 2 * hp + hh
                hb = SM_DTB + 2 * hp + hh
                colf = p_c[:, hf:hf + 1]
                colb = suf_c[:, hb:hb + 1]
                rowf = p_r[hf:hf + 1, :]
                rowb = suf_r[hb:hb + 1, :]
                dtf = dt_r[hf:hf + 1, :]
                dtb = dt_r[hb:hb + 1, :]
                decf = jnp.exp(jnp.where(tril, colf - rowf, -jnp.inf))
                decb = jnp.exp(jnp.where(triu, colb - rowb, -jnp.inf))
                ms.append((cb * (decf * dtf + decb * dtb)).astype(BF16))
            m2 = jnp.concatenate(ms, axis=1)
            xp = x_c[:, hp * LANE:(hp + 1) * LANE]
            xz = jnp.zeros_like(xp)
            x2 = jnp.concatenate([jnp.where(lane_half, xp, xz), jnp.where(lane_half, xz, xp)], axis=0)
            acc_ref[pl.ds(r0, L), hp * LANE:(hp + 1) * LANE] = _dot(m2, x2)
        s_prev = st_ref[0]
        y_off = _dot(c_c, s_prev.astype(BF16)) * _dot(_split_hi_lo(jnp.exp(p_c)), e2_f)
        acc_ref[pl.ds(r0, L), :] += y_off
        wdt = jnp.exp(tot_c - p_c) * dt_c
        xw = (x_c.astype(F32) * _dot(_split_hi_lo(wdt), e2_f)).astype(BF16)
        dec = _dot(_split_hi_lo(jnp.where(row0, jnp.exp(tot_c), 0.0)), e2_f)[0:1, :]
        b_t = b_c.astype(F32).T.astype(BF16)
        st_ref[0] = s_prev * dec + _dot(b_t, xw)
        return carry

    lax.fori_loop(0, nc, fwd_body, 0)

    def bwd_body(i, carry):
        c = nc - 1 - i
        r0, dt_c, a_c, p_c, tot_c, dt_r, a_r, p_r, tot_r = chunk_terms(c)
        suf_c = tot_c - p_c + a_c
        x_c = x_ref[0, pl.ds(r0, L), :]
        b_c = b_ref[0, pl.ds(r0, L), :]
        c_c = c_ref[0, pl.ds(r0, L), :]
        s_prev = st_ref[1]
        y_off = _dot(c_c, s_prev.astype(BF16)) * _dot(_split_hi_lo(jnp.exp(suf_c)), e2_b)
        wdt = jnp.exp(tot_c - suf_c) * dt_c
        xf = x_c.astype(F32)
        xw = (xf * _dot(_split_hi_lo(wdt), e2_b)).astype(BF16)
        dec = _dot(_split_hi_lo(jnp.where(row0, jnp.exp(tot_c), 0.0)), e2_b)[0:1, :]
        b_t = b_c.astype(F32).T.astype(BF16)
        st_ref[1] = s_prev * dec + _dot(b_t, xw)

        y = acc_ref[pl.ds(r0, L), :] + y_off + xf * drow_ref[...]
        y = y * _silu(z_ref[0, pl.ds(r0, L), :].astype(F32))
        ms = jnp.mean(y * y, axis=-1, keepdims=True)
        o_ref[0, pl.ds(r0, L), :] = (y * lax.rsqrt(ms + EPS) * nw_ref[...]).astype(BF16)
        return carry

    lax.fori_loop(0, nc, bwd_body, 0)


def _ssd(xbc, u3, us3, ust, prow, pcol, drow, nw):
    B, S, _ = xbc.shape
    W = SSD_GROUP_W
    N = SSD_STATE
    return pl.pallas_call(
        _ssd_kernel,
        grid=(B, SSD_GROUPS),
        in_specs=[
            pl.BlockSpec((1, S, W), lambda b, g: (b, 0, g)),
            pl.BlockSpec((1, S, N), lambda b, g: (b, 0, SSD_D_INNER // N + g)),
            pl.BlockSpec((1, S, N), lambda b, g: (b, 0, SSD_D_INNER // N + SSD_GROUPS + g)),
            pl.BlockSpec((1, S, W), lambda b, g: (b, 0, U_Z // W + g)),
            pl.BlockSpec((1, S, LANE), lambda b, g: (b, 0, g)),
            pl.BlockSpec((LANE, S), lambda b, g: (g, b)),
            pl.BlockSpec((8, LANE), lambda b, g: (g, 0)),
            pl.BlockSpec((1, LANE, 8), lambda b, g: (g, 0, 0)),
            pl.BlockSpec((1, W), lambda b, g: (0, g)),
            pl.BlockSpec((1, W), lambda b, g: (0, g)),
        ],
        out_specs=pl.BlockSpec((1, S, W), lambda b, g: (b, 0, g)),
        out_shape=jax.ShapeDtypeStruct((B, S, SSD_D_INNER), BF16),
        scratch_shapes=[pltpu.VMEM((S, W), F32), pltpu.VMEM((2, N, W), F32)],
        compiler_params=_cparams(("parallel", "parallel")),
        name="ssd_scan",
    )(xbc, xbc, xbc, u3, us3, ust, prow, pcol, drow, nw)


def _gla_kernel(q_ref, k_ref, v_ref, gg_ref, us_ref, a2f_ref, a2b_ref, bf_ref, bb_ref, nw_ref,
                o_ref, acc_ref, gf_ref, gb_ref, st_ref):
    L = GLA_CHUNK
    S = q_ref.shape[1]
    nc = S // L
    scale = GLA_DK ** -0.5

    ga = us_ref[0].astype(BF16)
    gf_ref[...] = _log_sigmoid(_dot(ga, a2f_ref[...]) + bf_ref[...]) / GLA_GATE_NORM
    gb_ref[...] = _log_sigmoid(_dot(ga, a2b_ref[...]) + bb_ref[...]) / GLA_GATE_NORM

    ii = _iota2((L, L), 0)
    jj = _iota2((L, L), 1)
    tril = ii >= jj
    triu = jj >= ii
    tril_f = tril.astype(F32)

    def chunk(c, g_ref, mask, reverse):
        r0 = pl.multiple_of(c * L, L)
        g_c = g_ref[pl.ds(r0, L), :]
        p = _dot_exact(tril_f, g_c)
        tot = p[L - 1:L, :]
        b = (tot - p + g_c) if reverse else p
        q_c = q_ref[0, pl.ds(r0, L), :].astype(F32) * scale
        k_c = k_ref[0, pl.ds(r0, L), :].astype(F32)
        v_c = v_ref[0, pl.ds(r0, L), :]
        qd = (q_c * jnp.exp(b)).astype(BF16)
        kd = (k_c * jnp.exp(-b)).astype(BF16)
        att = jnp.where(mask, _nt(qd, kd), 0.0).astype(BF16)
        idx = 1 if reverse else 0
        s_prev = st_ref[idx]
        o = _dot(att, v_c) + _nt(qd, s_prev.astype(BF16))
        kdec = (k_c * jnp.exp(tot - b)).astype(BF16)
        st_ref[idx] = s_prev * jnp.exp(tot) + _tn(v_c, kdec)
        return r0, o

    st_ref[...] = jnp.zeros_like(st_ref)

    def fwd_body(c, carry):
        r0, o = chunk(c, gf_ref, tril, False)
        acc_ref[pl.ds(r0, L), :] = o
        return carry

    lax.fori_loop(0, nc, fwd_body, 0)

    def bwd_body(i, carry):
        r0, o = chunk(nc - 1 - i, gb_ref, triu, True)
        o = acc_ref[pl.ds(r0, L), :] + o
        ms = jnp.mean(o * o, axis=-1, keepdims=True)
        o = o * lax.rsqrt(ms + EPS) * nw_ref[...]
        o_ref[0, pl.ds(r0, L), :] = (o * _silu(gg_ref[0, pl.ds(r0, L), :].astype(F32))).astype(BF16)
        return carry

    lax.fori_loop(0, nc, bwd_body, 0)


def _gla(u3, us3, a2f, a2b, bias_f, bias_b, nw):
    B, S, _ = u3.shape
    return pl.pallas_call(
        _gla_kernel,
        grid=(B, GLA_HEADS),
        in_specs=[
            pl.BlockSpec((1, S, GLA_DK), lambda b, h: (b, 0, U_GQ // GLA_DK + h)),
            pl.BlockSpec((1, S, GLA_DK), lambda b, h: (b, 0, U_GK // GLA_DK + h)),
            pl.BlockSpec((1, S, GLA_DV), lambda b, h: (b, 0, U_GV // GLA_DV + h)),
            pl.BlockSpec((1, S, GLA_DV), lambda b, h: (b, 0, U_GG // GLA_DV + h)),
            pl.BlockSpec((1, S, LANE), lambda b, h: (b, 0, 0)),
            pl.BlockSpec((LANE, GLA_DK), lambda b, h: (0, h)),
            pl.BlockSpec((LANE, GLA_DK), lambda b, h: (0, h)),
            pl.BlockSpec((1, GLA_DK), lambda b, h: (0, h)),
            pl.BlockSpec((1, GLA_DK), lambda b, h: (0, h)),
            pl.BlockSpec((1, GLA_DV), lambda b, h: (0, 0)),
        ],
        out_specs=pl.BlockSpec((1, S, GLA_DV), lambda b, h: (b, 0, h)),
        out_shape=jax.ShapeDtypeStruct((B, S, GLA_VAL_W), BF16),
        scratch_shapes=[
            pltpu.VMEM((S, GLA_DV), F32),
            pltpu.VMEM((S, GLA_DK), F32),
            pltpu.VMEM((S, GLA_DK), F32),
            pltpu.VMEM((2, GLA_DV, GLA_DK), F32),
        ],
        compiler_params=_cparams(("parallel", "parallel")),
        name="gla_scan",
    )(u3, u3, u3, u3, us3, a2f, a2b, bias_f, bias_b, nw)


def _na_col_tables():
    qcol = np.arange(GRID_W)[:, None]
    kcol = np.arange(GRID_W)[None, :]
    w_start = np.clip(qcol - NA_WIN_W // 2, 0, GRID_W - NA_WIN_W)
    mask = (kcol >= w_start) & (kcol < w_start + NA_WIN_W)
    off = np.clip(kcol - qcol, -(NA_WIN_W - 1), NA_WIN_W - 1) + (NA_WIN_W - 1)
    return mask, off


def _na_bias_table(rpb, rows):
    win_h = min(NA_WIN_H, rows)
    mask, off = _na_col_tables()
    delta = np.arange(win_h)[:, None]
    w = np.arange(win_h)[None, :]
    row_off = w - delta + (NA_WIN_H - 1)
    t = rpb[:, row_off]
    t = t[:, :, :, off]
    t = jnp.where(mask[None, None, None], t, -jnp.inf)
    t = jnp.transpose(t, (0, 1, 3, 2, 4))
    return t.reshape(rpb.shape[0], win_h, GRID_W, win_h * GRID_W).astype(F32)


def _na_kernel(q_ref, k_ref, v_ref, tab_ref, qw_ref, kw_ref, o_ref, q0_ref, q1_ref, kn_ref):
    S = q_ref.shape[1]
    rows = S // GRID_W
    win_h = tab_ref.shape[1]
    nk = win_h * GRID_W
    scale = NA_HEAD_DIM ** -0.5

    er = _iota2((LANE, LANE), 0) // NA_HEAD_DIM
    ec = _iota2((LANE, LANE), 1) // NA_HEAD_DIM
    e_blk = (er == ec).astype(BF16)
    first = _iota2((S, LANE), 1) < NA_HEAD_DIM

    q = q_ref[0].astype(F32)
    ms = _dot(_split_hi_lo(q * q), jnp.concatenate([e_blk, e_blk], axis=0)) * (1.0 / NA_HEAD_DIM)
    qn = (q * lax.rsqrt(ms + EPS) * qw_ref[...] * scale).astype(BF16)
    zero = jnp.zeros_like(qn)
    q0_ref[...] = jnp.where(first, qn, zero)
    q1_ref[...] = jnp.where(first, zero, qn)
    k = k_ref[0].astype(F32)
    ms = _dot(_split_hi_lo(k * k), jnp.concatenate([e_blk, e_blk], axis=0)) * (1.0 / NA_HEAD_DIM)
    kn_ref[...] = (k * lax.rsqrt(ms + EPS) * kw_ref[...]).astype(BF16)

    first_q = _iota2((GRID_W, LANE), 1) < NA_HEAD_DIM

    def row_body(r, carry):
        r0 = jnp.clip(r - win_h // 2, 0, rows - win_h)
        delta = r - r0
        k0 = pl.multiple_of(r0 * GRID_W, GRID_W)
        q0 = pl.multiple_of(r * GRID_W, GRID_W)
        kw = kn_ref[pl.ds(k0, nk), :]
        vw = v_ref[0, pl.ds(k0, nk), :]
        outs = []
        for hl, qr in enumerate((q0_ref, q1_ref)):
            s = _nt(qr[pl.ds(q0, GRID_W), :], kw) + tab_ref[hl, delta]
            m = jnp.max(s, axis=-1, keepdims=True)
            p = jnp.exp(s - m)
            den = jnp.sum(p, axis=-1, keepdims=True)
            outs.append(_dot(p.astype(BF16), vw) / den)
        o_ref[0, pl.ds(q0, GRID_W), :] = jnp.where(first_q, outs[0], outs[1]).astype(BF16)
        return carry

    lax.fori_loop(0, rows, row_body, 0)


def _na(u3, table, qw, kw):
    B, S, _ = u3.shape
    win_h = table.shape[1]
    return pl.pallas_call(
        _na_kernel,
        grid=(B, NA_HEADS // 2),
        in_specs=[
            pl.BlockSpec((1, S, LANE), lambda b, h: (b, 0, U_NQ // LANE + h)),
            pl.BlockSpec((1, S, LANE), lambda b, h: (b, 0, U_NK // LANE + h)),
            pl.BlockSpec((1, S, LANE), lambda b, h: (b, 0, U_NV // LANE + h)),
            pl.BlockSpec((2, win_h, GRID_W, win_h * GRID_W), lambda b, h: (h, 0, 0, 0)),
            pl.BlockSpec((1, LANE), lambda b, h: (0, 0)),
            pl.BlockSpec((1, LANE), lambda b, h: (0, 0)),
        ],
        out_specs=pl.BlockSpec((1, S, LANE), lambda b, h: (b, 0, h)),
        out_shape=jax.ShapeDtypeStruct((B, S, NA_W), BF16),
        scratch_shapes=[pltpu.VMEM((S, LANE), BF16)] * 3,
        compiler_params=_cparams(("parallel", "parallel")),
        name="na_attn",
    )(u3, u3, u3, table, qw, kw)


def _merge_kernel(x_ref, ys_ref, yg_ref, yn_ref, gate_ref, ws_ref, wg_ref, wn_ref, wo_ref, o_ref):
    D = D_MODEL
    mixed = _sigmoid(gate_ref[:, 0:D].astype(F32)) * _dot(ys_ref[...], ws_ref[...])
    mixed += _sigmoid(gate_ref[:, D:2 * D].astype(F32)) * _dot(yg_ref[...], wg_ref[...])
    mixed += _sigmoid(gate_ref[:, 2 * D:3 * D].astype(F32)) * _dot(yn_ref[...], wn_ref[...])
    o_ref[...] = x_ref[...] + _dot(mixed.astype(BF16), wo_ref[...])


def _merge(x2, ys, yg, yn, u2, ws, wg, wn, wo, tm=512):
    T = x2.shape[0]
    D = D_MODEL
    row = lambda i: (i, 0)
    fixed = lambda i: (0, 0)
    return pl.pallas_call(
        _merge_kernel,
        grid=(T // tm,),
        in_specs=[
            pl.BlockSpec((tm, D), row),
            pl.BlockSpec((tm, D), row),
            pl.BlockSpec((tm, D), row),
            pl.BlockSpec((tm, D), row),
            pl.BlockSpec((tm, N_BRANCH * D), row),
            pl.BlockSpec((D, D), fixed),
            pl.BlockSpec((D, D), fixed),
            pl.BlockSpec((D, D), fixed),
            pl.BlockSpec((D, D), fixed),
        ],
        out_specs=pl.BlockSpec((tm, D), row),
        out_shape=jax.ShapeDtypeStruct((T, D), F32),
        compiler_params=_cparams(("parallel",)),
        name="merge",
    )(x2, ys, yg, yn, u2, ws, wg, wn, wo)


def _mlp_kernel(x_ref, nw_ref, w1_ref, w2_ref, o_ref, *, tf):
    x = x_ref[...]
    ms = jnp.mean(x * x, axis=-1, keepdims=True)
    h = (x * lax.rsqrt(ms + EPS) * nw_ref[...]).astype(BF16)
    acc = x
    for f in range(D_FF // tf):
        a = jnp.maximum(_dot(h, w1_ref[:, f * tf:(f + 1) * tf]), 0.0)
        acc = acc + _dot((a * a).astype(BF16), w2_ref[f * tf:(f + 1) * tf, :])
    o_ref[...] = acc


def _mlp(x2, nw, w1, w2, tm=512, tf=1024):
    T = x2.shape[0]
    D = D_MODEL
    return pl.pallas_call(
        functools.partial(_mlp_kernel, tf=tf),
        grid=(T // tm,),
        in_specs=[
            pl.BlockSpec((tm, D), lambda i: (i, 0)),
            pl.BlockSpec((1, D), lambda i: (0, 0)),
            pl.BlockSpec((D, D_FF), lambda i: (0, 0)),
            pl.BlockSpec((D_FF, D), lambda i: (0, 0)),
        ],
        out_specs=pl.BlockSpec((tm, D), lambda i: (i, 0)),
        out_shape=jax.ShapeDtypeStruct((T, D), F32),
        compiler_params=_cparams(("parallel",)),
        name="mlp",
    )(x2, nw, w1, w2)


def _pad_rows(w, start, total):
    return jnp.zeros((total, w.shape[1]), w.dtype).at[start:start + w.shape[0]].set(w)


def kernel(x, norm_mix_w, w_in, ssd_conv_w, ssd_conv_b, ssd_dt_bias_f, ssd_dt_bias_b, ssd_a_log_f,
           ssd_a_log_b, ssd_d, ssd_norm_w, gla_a2_f, gla_a2_bias_f, gla_a2_b, gla_a2_bias_b,
           gla_norm_w, na_q_norm_w, na_k_norm_w, na_rpb, w_branch_ssd, w_branch_gla, w_branch_na,
           w_out, norm_mlp_w, w_ff1, w_ff2):
    B, S, D = x.shape
    T = B * S
    depth = w_in.shape[0]
    rows = S // GRID_W
    x2 = x.reshape(T, D)
    for l in range(depth):
        w_big = w_in[l][:, _BIG_PERM].astype(BF16)
        w_small = jnp.where(_SMALL_SRC[None, :] >= 0, w_in[l][:, np.maximum(_SMALL_SRC, 0)], 0.0).astype(BF16)
        hg = (SSD_GROUPS, SSD_HG)
        zeros_r = jnp.zeros((SSD_GROUPS, LANE - 2 * SSD_HG), F32)
        dt_bias = jnp.concatenate([ssd_dt_bias_f[l].reshape(hg), ssd_dt_bias_b[l].reshape(hg), zeros_r], axis=1)
        a_neg = jnp.concatenate([-jnp.exp(ssd_a_log_f[l]).reshape(hg), -jnp.exp(ssd_a_log_b[l]).reshape(hg),
                                 zeros_r], axis=1)
        prow3 = jnp.zeros((SSD_GROUPS, 8, LANE), F32).at[:, 0].set(dt_bias).at[:, 1].set(a_neg)
        prow = prow3.reshape(SSD_GROUPS * 8, LANE)
        pcol = jnp.transpose(prow3, (0, 2, 1))
        drow = jnp.repeat(ssd_d[l], SSD_HEAD_DIM)[None, :]
        a2f = _pad_rows(gla_a2_f[l], SM_GAF, LANE).astype(BF16)
        a2b = _pad_rows(gla_a2_b[l], SM_GAB, LANE).astype(BF16)
        table = _na_bias_table(na_rpb[l], rows)
        qw = jnp.tile(na_q_norm_w[l], 2)[None, :]
        kw = jnp.tile(na_k_norm_w[l], 2)[None, :]

        u2, us2, ust = _inproj(x2, norm_mix_w[l][None, :], w_big, w_small, w_small.T)
        u3 = u2.reshape(B, S, U_WIDTH)
        us3 = us2.reshape(B, S, SMALL_W)
        xbc = _conv(u3, ssd_conv_w[l], ssd_conv_b[l][None, :])
        y_ssd = _ssd(xbc, u3, us3, ust, prow, pcol, drow, ssd_norm_w[l][None, :])
        y_gla = _gla(u3, us3, a2f, a2b, gla_a2_bias_f[l][None, :], gla_a2_bias_b[l][None, :],
                     gla_norm_w[l][None, :])
        y_na = _na(u3, table, qw, kw)
        x2 = _merge(x2, y_ssd.reshape(T, -1), y_gla.reshape(T, -1), y_na.reshape(T, -1), u2,
                    w_branch_ssd[l].astype(BF16), w_branch_gla[l].astype(BF16),
                    w_branch_na[l].astype(BF16), w_out[l].astype(BF16))
        x2 = _mlp(x2, norm_mlp_w[l][None, :], w_ff1[l].astype(BF16), w_ff2[l].astype(BF16))
    return x2.reshape(B, S, D)
```

```python
import functools

import jax
import jax.numpy as jnp
import numpy as np
from jax import lax
from jax.experimental import pallas as pl
from jax.experimental.pallas import tpu as pltpu

F32 = jnp.float32
BF16 = jnp.bfloat16

EPS = 1e-6
D_MODEL = 1024
GRID_W = 64

SSD_HEADS = 16
SSD_HEAD_DIM = 64
SSD_D_INNER = 1024
SSD_GROUPS = 2
SSD_STATE = 128
SSD_CONV = 5
SSD_CONV_DIM = 1536
SSD_CHUNK = 128
SSD_GROUP_W = SSD_D_INNER // SSD_GROUPS
SSD_HG = SSD_HEADS // SSD_GROUPS

GLA_HEADS = 4
GLA_DK = 128
GLA_DV = 256
GLA_KEY_W = 512
GLA_VAL_W = 1024
GLA_GATE_RANK = 16
GLA_GATE_NORM = 16.0
GLA_CHUNK = 64

NA_HEADS = 16
NA_HEAD_DIM = 64
NA_W = 1024
NA_WIN_H = 8
NA_WIN_W = 16
NA_ROWS_PER_STEP = 2

N_BRANCH = 3
D_FF = 4096

IN_SIZES = (SSD_D_INNER, SSD_CONV_DIM, SSD_HEADS, SSD_HEADS,
            GLA_KEY_W, GLA_KEY_W, GLA_VAL_W, GLA_VAL_W, GLA_GATE_RANK, GLA_GATE_RANK,
            NA_W, NA_W, NA_W, N_BRANCH * D_MODEL)
_IN_OFF = np.concatenate([[0], np.cumsum(IN_SIZES)])
(_O_Z, _O_XBC, _O_DTF, _O_DTB, _O_GQ, _O_GK, _O_GV, _O_GG, _O_GAF, _O_GAB,
 _O_NQ, _O_NK, _O_NV, _O_GATE) = [int(v) for v in _IN_OFF[:-1]]

U_GATE = 0
U_NQ = 3072
U_NK = 4096
U_NV = 5120
U_Z = 6144
U_GV = 7168
U_GG = 8192
U_XBC = 9216
U_GQ = 10752
U_GK = 11264
U_WIDTH = 11776
INPROJ_COL_CHUNK = 1024
LANE = 128
VMEM_LIMIT = 56 * 1024 * 1024

SMALL_W = SSD_GROUPS * LANE
SM_DTF, SM_DTB, SM_GAF, SM_GAB = 0, 8, 16, 32

_BIG_SEGS = ((_O_GATE, 3072), (_O_NQ, 3072), (_O_Z, 1024), (_O_GV, 2048), (_O_XBC, 1536), (_O_GQ, 1024))


def _small_weight(w):
    blocks = []
    for g in range(SSD_GROUPS):
        h0 = g * SSD_HG
        cols = [w[:, _O_DTF + h0:_O_DTF + h0 + SSD_HG], w[:, _O_DTB + h0:_O_DTB + h0 + SSD_HG]]
        used = 2 * SSD_HG
        if g == 0:
            cols += [w[:, _O_GAF:_O_GAF + GLA_GATE_RANK], w[:, _O_GAB:_O_GAB + GLA_GATE_RANK]]
            used += 2 * GLA_GATE_RANK
        cols.append(jnp.zeros((w.shape[0], LANE - used), w.dtype))
        blocks += cols
    return jnp.concatenate(blocks, axis=1)


def _cparams(sem, vmem=VMEM_LIMIT):
    return pltpu.CompilerParams(dimension_semantics=sem, vmem_limit_bytes=vmem)


def _sigmoid(x):
    return 1.0 / (1.0 + jnp.exp(-x))


def _silu(x):
    return x * _sigmoid(x)


def _softplus(x):
    return jnp.maximum(x, 0.0) + jnp.log1p(jnp.exp(-jnp.abs(x)))


def _log_sigmoid(x):
    return -_softplus(-x)


def _nt(a, b):
    return lax.dot_general(a, b, (((1,), (1,)), ((), ())), preferred_element_type=F32)


def _tn(a, b):
    return lax.dot_general(a, b, (((0,), (0,)), ((), ())), preferred_element_type=F32)


def _dot(a, b):
    return jnp.dot(a, b, preferred_element_type=F32)


def _dot_exact(a, b):
    return jnp.dot(a, b, preferred_element_type=F32, precision=lax.Precision.HIGHEST)


def _iota2(shape, dim):
    return lax.broadcasted_iota(jnp.int32, shape, dim)


def _inproj_kernel(x_ref, nw_ref, w_ref, ws_ref, wst_ref, u_ref, us_ref, ust_ref, h_ref):
    @pl.when(pl.program_id(1) == 0)
    def _():
        x = x_ref[...]
        ms = jnp.mean(x * x, axis=-1, keepdims=True)
        h = (x * lax.rsqrt(ms + EPS) * nw_ref[...]).astype(BF16)
        h_ref[...] = h
        us_ref[...] = _dot(h, ws_ref[...])
        ust_ref[...] = _nt(wst_ref[...], h)

    tn = u_ref.shape[1]
    for a in range(0, tn, INPROJ_COL_CHUNK):
        b = min(a + INPROJ_COL_CHUNK, tn)
        u_ref[:, a:b] = _dot(h_ref[...], w_ref[:, a:b]).astype(BF16)


def _inproj(x2, nw, w_big, w_small, w_small_t, tm=1024, tn=U_WIDTH // 4):
    T = x2.shape[0]
    return pl.pallas_call(
        _inproj_kernel,
        grid=(T // tm, U_WIDTH // tn),
        in_specs=[
            pl.BlockSpec((tm, D_MODEL), lambda i, j: (i, 0)),
            pl.BlockSpec((1, D_MODEL), lambda i, j: (0, 0)),
            pl.BlockSpec((D_MODEL, tn), lambda i, j: (0, j)),
            pl.BlockSpec((D_MODEL, SMALL_W), lambda i, j: (0, 0)),
            pl.BlockSpec((SMALL_W, D_MODEL), lambda i, j: (0, 0)),
        ],
        out_specs=[
            pl.BlockSpec((tm, tn), lambda i, j: (i, j)),
            pl.BlockSpec((tm, SMALL_W), lambda i, j: (i, 0)),
            pl.BlockSpec((SMALL_W, tm), lambda i, j: (0, i)),
        ],
        out_shape=[
            jax.ShapeDtypeStruct((T, U_WIDTH), BF16),
            jax.ShapeDtypeStruct((T, SMALL_W), F32),
            jax.ShapeDtypeStruct((SMALL_W, T), F32),
        ],
        scratch_shapes=[pltpu.VMEM((tm, D_MODEL), BF16)],
        compiler_params=_cparams(("parallel", "arbitrary")),
        name="inproj",
    )(x2, nw, w_big, w_small, w_small_t)


def _conv_kernel(u_ref, w_ref, b_ref, o_ref):
    x = u_ref[0].astype(F32)
    S = x.shape[0]
    row = _iota2(x.shape, 0)
    acc = jnp.zeros_like(x) + b_ref[...]
    pad = SSD_CONV // 2
    for k in range(SSD_CONV):
        d = k - pad
        if d == 0:
            xs = x
        else:
            xs = pltpu.roll(x, (-d) % S, axis=0)
            valid = (row + d >= 0) & (row + d < S)
            xs = jnp.where(valid, xs, 0.0)
        acc = acc + w_ref[k:k + 1, :] * xs
    o_ref[0] = _silu(acc).astype(BF16)


def _conv(u3, conv_w, conv_b, tc=256):
    B, S, _ = u3.shape
    nblk = SSD_CONV_DIM // tc
    off = U_XBC // tc
    return pl.pallas_call(
        _conv_kernel,
        grid=(B, nblk),
        in_specs=[
            pl.BlockSpec((1, S, tc), lambda b, c: (b, 0, off + c)),
            pl.BlockSpec((SSD_CONV, tc), lambda b, c: (0, c)),
            pl.BlockSpec((1, tc), lambda b, c: (0, c)),
        ],
        out_specs=pl.BlockSpec((1, S, tc), lambda b, c: (b, 0, c)),
        out_shape=jax.ShapeDtypeStruct((B, S, SSD_CONV_DIM), BF16),
        compiler_params=_cparams(("parallel", "parallel")),
        name="ssd_conv",
    )(u3, conv_w, conv_b)


def _split_hi_lo(v):
    hi = v.astype(BF16)
    lo = (v - hi.astype(F32)).astype(BF16)
    return jnp.concatenate([hi, lo], axis=1)


def _ssd_kernel(x_ref, b_ref, c_ref, z_ref, us_ref, ust_ref, prow_ref, pcol_ref, drow_ref, nw_ref,
                o_ref, acc_ref, st_ref):
    L = SSD_CHUNK
    S = x_ref.shape[1]
    nc = S // L
    W = SSD_GROUP_W

    ii = _iota2((L, L), 0)
    jj = _iota2((L, L), 1)
    tril = ii >= jj
    triu = jj >= ii
    tril_f = tril.astype(F32)
    triu_f = triu.astype(F32)

    er = _iota2((LANE, W), 0)
    ec = _iota2((LANE, W), 1) // SSD_HEAD_DIM
    e_f = (er == ec + SM_DTF).astype(BF16)
    e_b = (er == ec + SM_DTB).astype(BF16)
    e2_f = jnp.concatenate([e_f, e_f], axis=0)
    e2_b = jnp.concatenate([e_b, e_b], axis=0)

    bias_row = prow_ref[0:1, :]
    a_row = prow_ref[1:2, :]
    bias_col = pcol_ref[0, :, 0:1]
    a_col = pcol_ref[0, :, 1:2]

    lane_half = _iota2((L, LANE), 1) < SSD_HEAD_DIM
    row0 = _iota2((16, LANE), 0) == 0

    def chunk_terms(c):
        r0 = pl.multiple_of(c * L, L)
        dt_c = _softplus(us_ref[0, pl.ds(r0, L), :] + bias_row)
        a_c = dt_c * a_row
        p_c = _dot_exact(tril_f, a_c)
        tot_c = p_c[L - 1:L, :]
        dt_r = _softplus(ust_ref[:, pl.ds(r0, L)] + bias_col)
        a_r = dt_r * a_col
        p_r = _dot_exact(a_r, triu_f)
        tot_r = p_r[:, L - 1:L]
        return r0, dt_c, a_c, p_c, tot_c, dt_r, a_r, p_r, tot_r

    st_ref[...] = jnp.zeros_like(st_ref)

    def fwd_body(c, carry):
        r0, dt_c, a_c, p_c, tot_c, dt_r, a_r, p_r, tot_r = chunk_terms(c)
        suf_r = tot_r - p_r + a_r
        suf_c = tot_c - p_c + a_c
        x_c = x_ref[0, pl.ds(r0, L), :]
        b_c = b_ref[0, pl.ds(r0, L), :]
        c_c = c_ref[0, pl.ds(r0, L), :]
        cb = _nt(c_c, b_c)
        for hp in range(SSD_HG // 2):
            ms = []
            for hh in range(2):
                hf = SM_DTF + 2 * hp + hh
                hb = SM_DTB + 2 * hp + hh
                colf = p_c[:, hf:hf + 1]
                colb = suf_c[:, hb:hb + 1]
                rowf = p_r[hf:hf + 1, :]
                rowb = suf_r[hb:hb + 1, :]
                dtf = dt_r[hf:hf + 1, :]
                dtb = dt_r[hb:hb + 1, :]
                decf = jnp.exp(jnp.where(tril, colf - rowf, -jnp.inf))
                decb = jnp.exp(jnp.where(triu, colb - rowb, -jnp.inf))
                ms.append((cb * (decf * dtf + decb * dtb)).astype(BF16))
            m2 = jnp.concatenate(ms, axis=1)
            xp = x_c[:, hp * LANE:(hp + 1) * LANE]
            xz = jnp.zeros_like(xp)
            x2 = jnp.concatenate([jnp.where(lane_half, xp, xz), jnp.where(lane_half, xz, xp)], axis=0)
            acc_ref[pl.ds(r0, L), hp * LANE:(hp + 1) * LANE] = _dot(m2, x2)
        s_prev = st_ref[0]
        y_off = _dot(c_c, s_prev.astype(BF16)) * _dot(_split_hi_lo(jnp.exp(p_c)), e2_f)
        acc_ref[pl.ds(r0, L), :] += y_off
        wdt = jnp.exp(tot_c - p_c) * dt_c
        xw = (x_c.astype(F32) * _dot(_split_hi_lo(wdt), e2_f)).astype(BF16)
        dec = _dot(_split_hi_lo(jnp.where(row0, jnp.exp(tot_c), 0.0)), e2_f)[0:1, :]
        b_t = b_c.astype(F32).T.astype(BF16)
        st_ref[0] = s_prev * dec + _dot(b_t, xw)
        return carry

    lax.fori_loop(0, nc, fwd_body, 0)

    def bwd_body(i, carry):
        c = nc - 1 - i
        r0, dt_c, a_c, p_c, tot_c, dt_r, a_r, p_r, tot_r = chunk_terms(c)
        suf_c = tot_c - p_c + a_c
        x_c = x_ref[0, pl.ds(r0, L), :]
        b_c = b_ref[0, pl.ds(r0, L), :]
        c_c = c_ref[0, pl.ds(r0, L), :]
        s_prev = st_ref[1]
        y_off = _dot(c_c, s_prev.astype(BF16)) * _dot(_split_hi_lo(jnp.exp(suf_c)), e2_b)
        wdt = jnp.exp(tot_c - suf_c) * dt_c
        xf = x_c.astype(F32)
        xw = (xf * _dot(_split_hi_lo(wdt), e2_b)).astype(BF16)
        dec = _dot(_split_hi_lo(jnp.where(row0, jnp.exp(tot_c), 0.0)), e2_b)[0:1, :]
        b_t = b_c.astype(F32).T.astype(BF16)
        st_ref[1] = s_prev * dec + _dot(b_t, xw)

        y = acc_ref[pl.ds(r0, L), :] + y_off + xf * drow_ref[...]
        y = y * _silu(z_ref[0, pl.ds(r0, L), :].astype(F32))
        ms = jnp.mean(y * y, axis=-1, keepdims=True)
        o_ref[0, pl.ds(r0, L), :] = (y * lax.rsqrt(ms + EPS) * nw_ref[...]).astype(BF16)
        return carry

    lax.fori_loop(0, nc, bwd_body, 0)


def _ssd(xbc, u3, us3, ust, prow, pcol, drow, nw):
    B, S, _ = xbc.shape
    W = SSD_GROUP_W
    N = SSD_STATE
    return pl.pallas_call(
        _ssd_kernel,
        grid=(B, SSD_GROUPS),
        in_specs=[
            pl.BlockSpec((1, S, W), lambda b, g: (b, 0, g)),
            pl.BlockSpec((1, S, N), lambda b, g: (b, 0, SSD_D_INNER // N + g)),
            pl.BlockSpec((1, S, N), lambda b, g: (b, 0, SSD_D_INNER // N + SSD_GROUPS + g)),
            pl.BlockSpec((1, S, W), lambda b, g: (b, 0, U_Z // W + g)),
            pl.BlockSpec((1, S, LANE), lambda b, g: (b, 0, g)),
            pl.BlockSpec((LANE, S), lambda b, g: (g, b)),
            pl.BlockSpec((8, LANE), lambda b, g: (g, 0)),
            pl.BlockSpec((1, LANE, 8), lambda b, g: (g, 0, 0)),
            pl.BlockSpec((1, W), lambda b, g: (0, g)),
            pl.BlockSpec((1, W), lambda b, g: (0, g)),
        ],
        out_specs=pl.BlockSpec((1, S, W), lambda b, g: (b, 0, g)),
        out_shape=jax.ShapeDtypeStruct((B, S, SSD_D_INNER), BF16),
        scratch_shapes=[pltpu.VMEM((S, W), F32), pltpu.VMEM((2, N, W), F32)],
        compiler_params=_cparams(("parallel", "parallel")),
        name="ssd_scan",
    )(xbc, xbc, xbc, u3, us3, ust, prow, pcol, drow, nw)


def _gla_kernel(q_ref, k_ref, v_ref, gg_ref, us_ref, a2_ref, bias_ref, nw_ref,
                o_ref, acc_ref, g_ref, qd_ref, kd_ref, kdp_ref, qcat_ref, x_ref, dec_ref, sst_ref, st_ref):
    L = GLA_CHUNK
    BL = 2 * L
    DK = GLA_DK
    S = q_ref.shape[1]
    nb = S // BL
    scale = DK ** -0.5

    ii = _iota2((BL, BL), 0)
    jj = _iota2((BL, BL), 1)
    same = (ii // L) == (jj // L)
    masks = (same & (ii >= jj), same & (jj >= ii))
    tri2 = masks[0].astype(BF16)
    par_row = _iota2((BL, DK), 0) // L

    ga = us_ref[0].astype(BF16)
    g_ref[...] = _log_sigmoid(_dot(ga, a2_ref[...]) + bias_ref[...]) / GLA_GATE_NORM

    def decay_body(i, carry):
        rs = pl.ds(pl.multiple_of(i * BL, BL), BL)
        g = g_ref[rs, :]
        hi = g.astype(BF16)
        r1 = g - hi.astype(F32)
        mid = r1.astype(BF16)
        lo = (r1 - mid.astype(F32)).astype(BF16)
        pp = _dot(tri2, jnp.concatenate([hi, mid, lo], axis=1))
        p = pp[:, 0:2 * DK] + pp[:, 2 * DK:4 * DK] + pp[:, 4 * DK:6 * DK]
        q_c = q_ref[0, rs, :].astype(F32) * scale
        k_c = k_ref[0, rs, :].astype(F32)
        zero = jnp.zeros((BL, DK), BF16)
        for d in range(2):
            p_d = p[:, d * DK:(d + 1) * DK]
            tot = jnp.where(par_row == 0, p_d[L - 1:L, :], p_d[BL - 1:BL, :])
            b = p_d if d == 0 else tot - p_d + g[:, DK:]
            qd = (q_c * jnp.exp(b)).astype(BF16)
            kdec = (k_c * jnp.exp(tot - b)).astype(BF16)
            qd_ref[d, rs, :] = qd
            kd_ref[d, rs, :] = (k_c * jnp.exp(-b)).astype(BF16)
            for par in range(2):
                sel = par_row == par
                kdp_ref[d, rs, par * DK:(par + 1) * DK] = jnp.where(sel, kdec, zero)
                qcat_ref[rs, (2 * d + par) * DK:(2 * d + par + 1) * DK] = jnp.where(sel, qd, zero)
                last = (par + 1) * L - 1
                dec_ref[d, 2 * i + par] = jnp.broadcast_to(jnp.exp(p_d[last:last + 1, :]), (8, DK))
        return carry

    lax.fori_loop(0, nb, decay_body, 0, unroll=2)

    def intra_body(i, carry):
        blks = (2 * i, 2 * i + 1)
        rss = [pl.ds(pl.multiple_of(b * BL, BL), BL) for b in blks]
        vs = [v_ref[0, rs, :] for rs in rss]
        atts = [[_nt(qd_ref[d, rs, :], kd_ref[d, rs, :]) for d in range(2)] for rs in rss]
        for j, b in enumerate(blks):
            for d in range(2):
                x_ref[d, b] = _tn(vs[j], kdp_ref[d, rss[j], :])
        for j in range(2):
            att = jnp.where(masks[0], atts[j][0], 0.0) + jnp.where(masks[1], atts[j][1], 0.0)
            acc_ref[rss[j], :] = _dot(att.astype(BF16), vs[j])
        return carry

    lax.fori_loop(0, nb // 2, intra_body, 0)

    st_ref[...] = jnp.zeros_like(st_ref)

    def state_body(i, carry):
        for d, b in enumerate((i, nb - 1 - i)):
            for par in ((0, 1) if d == 0 else (1, 0)):
                s_prev = st_ref[d]
                lane0 = (2 * d + par) * DK
                sst_ref[b, :, lane0:lane0 + DK] = s_prev.astype(BF16)
                st_ref[d] = s_prev * dec_ref[d, 2 * b + par, 0:1, :] + x_ref[d, b, :, par * DK:(par + 1) * DK]
        return carry

    lax.fori_loop(0, nb, state_body, 0)

    def out_body(b, carry):
        rs = pl.ds(pl.multiple_of(b * BL, BL), BL)
        o = acc_ref[rs, :] + _nt(qcat_ref[rs, :], sst_ref[b])
        ms = jnp.mean(o * o, axis=-1, keepdims=True)
        o = o * lax.rsqrt(ms + EPS) * nw_ref[...]
        o_ref[0, rs, :] = (o * _silu(gg_ref[0, rs, :].astype(F32))).astype(BF16)
        return carry

    lax.fori_loop(0, nb, out_body, 0, unroll=2)


def _gla(u3, us3, a2, bias, nw):
    B, S, _ = u3.shape
    DK, DV = GLA_DK, GLA_DV
    nc = S // GLA_CHUNK
    nb = nc // 2
    return pl.pallas_call(
        _gla_kernel,
        grid=(B, GLA_HEADS),
        in_specs=[
            pl.BlockSpec((1, S, DK), lambda b, h: (b, 0, U_GQ // DK + h)),
            pl.BlockSpec((1, S, DK), lambda b, h: (b, 0, U_GK // DK + h)),
            pl.BlockSpec((1, S, DV), lambda b, h: (b, 0, U_GV // DV + h)),
            pl.BlockSpec((1, S, DV), lambda b, h: (b, 0, U_GG // DV + h)),
            pl.BlockSpec((1, S, LANE), lambda b, h: (b, 0, 0)),
            pl.BlockSpec((LANE, 2 * DK), lambda b, h: (0, h)),
            pl.BlockSpec((1, 2 * DK), lambda b, h: (0, h)),
            pl.BlockSpec((1, DV), lambda b, h: (0, 0)),
        ],
        out_specs=pl.BlockSpec((1, S, DV), lambda b, h: (b, 0, h)),
        out_shape=jax.ShapeDtypeStruct((B, S, GLA_VAL_W), BF16),
        scratch_shapes=[
            pltpu.VMEM((S, DV), F32),
            pltpu.VMEM((S, 2 * DK), F32),
            pltpu.VMEM((2, S, DK), BF16),
            pltpu.VMEM((2, S, DK), BF16),
            pltpu.VMEM((2, S, 2 * DK), BF16),
            pltpu.VMEM((S, 4 * DK), BF16),
            pltpu.VMEM((2, nb, DV, 2 * DK), F32),
            pltpu.VMEM((2, nc, 8, DK), F32),
            pltpu.VMEM((nb, DV, 4 * DK), BF16),
            pltpu.VMEM((2, DV, DK), F32),
        ],
        compiler_params=_cparams(("parallel", "parallel")),
        name="gla_scan",
    )(u3, u3, u3, u3, us3, a2, bias, nw)


def _na_col_tables():
    qcol = np.arange(GRID_W)[:, None]
    kcol = np.arange(GRID_W)[None, :]
    w_start = np.clip(qcol - NA_WIN_W // 2, 0, GRID_W - NA_WIN_W)
    mask = (kcol >= w_start) & (kcol < w_start + NA_WIN_W)
    off = np.clip(kcol - qcol, -(NA_WIN_W - 1), NA_WIN_W - 1) + (NA_WIN_W - 1)
    return mask, off


def _na_bias_table(rpb, rows):
    win_h = min(NA_WIN_H, rows)
    mask, off = _na_col_tables()
    delta = np.arange(win_h)[:, None]
    w = np.arange(win_h)[None, :]
    row_off = w - delta + (NA_WIN_H - 1)
    t = rpb[:, row_off]
    t = t[:, :, :, off]
    t = jnp.where(mask[None, None, None], t, -jnp.inf)
    t = jnp.transpose(t, (0, 1, 3, 2, 4))
    H = rpb.shape[0]
    nk = win_h * GRID_W
    t = t.reshape(H // 2, 2, win_h, GRID_W, nk)
    t = jnp.transpose(t, (0, 2, 1, 3, 4))
    return t.reshape(H // 2, win_h, 2 * GRID_W, nk).astype(F32)


def _na_kernel(q_ref, k_ref, v_ref, tab_ref, qw_ref, kw_ref, o_ref, q0_ref, q1_ref, kn_ref):
    S = q_ref.shape[1]
    rows = S // GRID_W
    win_h = tab_ref.shape[1]
    nk = win_h * GRID_W
    scale = NA_HEAD_DIM ** -0.5
    RB = 128

    er = _iota2((LANE, LANE), 0) // NA_HEAD_DIM
    ec = _iota2((LANE, LANE), 1) // NA_HEAD_DIM
    e_blk = (er == ec).astype(BF16)
    e2 = jnp.concatenate([e_blk, e_blk], axis=0)
    first = _iota2((RB, LANE), 1) < NA_HEAD_DIM

    def norm_body(i, carry):
        rs = pl.ds(pl.multiple_of(i * RB, RB), RB)
        q = q_ref[0, rs, :].astype(F32)
        ms = _dot(_split_hi_lo(q * q), e2) * (1.0 / NA_HEAD_DIM)
        qn = (q * lax.rsqrt(ms + EPS) * qw_ref[...] * scale).astype(BF16)
        zero = jnp.zeros_like(qn)
        q0_ref[rs, :] = jnp.where(first, qn, zero)
        q1_ref[rs, :] = jnp.where(first, zero, qn)
        k = k_ref[0, rs, :].astype(F32)
        ms = _dot(_split_hi_lo(k * k), e2) * (1.0 / NA_HEAD_DIM)
        kn_ref[rs, :] = (k * lax.rsqrt(ms + EPS) * kw_ref[...]).astype(BF16)
        return carry

    lax.fori_loop(0, S // RB, norm_body, 0)

    first_q = _iota2((GRID_W, LANE), 1) < NA_HEAD_DIM

    def key_start(r):
        r0 = jnp.clip(r - win_h // 2, 0, rows - win_h)
        return r0, pl.multiple_of(r0 * GRID_W, GRID_W)

    def scores(r):
        r0, k0 = key_start(r)
        q0 = pl.multiple_of(r * GRID_W, GRID_W)
        qs = jnp.concatenate([q0_ref[pl.ds(q0, GRID_W), :], q1_ref[pl.ds(q0, GRID_W), :]], axis=0)
        return _nt(qs, kn_ref[pl.ds(k0, nk), :]) + tab_ref[0, r - r0]

    def softmax(s):
        m = jnp.max(s, axis=-1, keepdims=True)
        p = jnp.exp(s - m)
        return p.astype(BF16), jnp.sum(p, axis=-1, keepdims=True)

    def attend(r, p, den):
        _, k0 = key_start(r)
        o = _dot(p, v_ref[0, pl.ds(k0, nk), :]) / den
        q0 = pl.multiple_of(r * GRID_W, GRID_W)
        o_ref[0, pl.ds(q0, GRID_W), :] = jnp.where(first_q, o[:GRID_W], o[GRID_W:]).astype(BF16)

    U = NA_ROWS_PER_STEP

    def row_body(i, carry):
        s_cur, pd_prev = carry
        r = i * U
        for j in range(U):
            attend(r - U + j, *pd_prev[j])
        pd = tuple(softmax(s) for s in s_cur)
        s_next = tuple(scores(jnp.minimum(r + U + j, rows - 1)) for j in range(U))
        return s_next, pd

    pd0 = tuple(softmax(scores(j)) for j in range(U))
    s1 = tuple(scores(U + j) for j in range(U))
    _, pd_last = lax.fori_loop(1, rows // U, row_body, (s1, pd0))
    for j in range(U):
        attend(rows - U + j, *pd_last[j])


def _na(u3, table, qw, kw):
    B, S, _ = u3.shape
    win_h = table.shape[1]
    return pl.pallas_call(
        _na_kernel,
        grid=(NA_HEADS // 2, B),
        in_specs=[
            pl.BlockSpec((1, S, LANE), lambda h, b: (b, 0, U_NQ // LANE + h)),
            pl.BlockSpec((1, S, LANE), lambda h, b: (b, 0, U_NK // LANE + h)),
            pl.BlockSpec((1, S, LANE), lambda h, b: (b, 0, U_NV // LANE + h)),
            pl.BlockSpec((1, win_h, 2 * GRID_W, win_h * GRID_W), lambda h, b: (h, 0, 0, 0)),
            pl.BlockSpec((1, LANE), lambda h, b: (0, 0)),
            pl.BlockSpec((1, LANE), lambda h, b: (0, 0)),
        ],
        out_specs=pl.BlockSpec((1, S, LANE), lambda h, b: (b, 0, h)),
        out_shape=jax.ShapeDtypeStruct((B, S, NA_W), BF16),
        scratch_shapes=[pltpu.VMEM((S, LANE), BF16)] * 3,
        compiler_params=_cparams(("parallel", "parallel")),
        name="na_attn",
    )(u3, u3, u3, table, qw, kw)


def _merge_kernel(x_ref, ys_ref, yg_ref, yn_ref, gate_ref, ws_ref, wg_ref, wn_ref, wo_ref, o_ref):
    D = D_MODEL
    mixed = _sigmoid(gate_ref[:, 0:D].astype(F32)) * _dot(ys_ref[...], ws_ref[...])
    mixed += _sigmoid(gate_ref[:, D:2 * D].astype(F32)) * _dot(yg_ref[...], wg_ref[...])
    mixed += _sigmoid(gate_ref[:, 2 * D:3 * D].astype(F32)) * _dot(yn_ref[...], wn_ref[...])
    o_ref[...] = x_ref[...] + _dot(mixed.astype(BF16), wo_ref[...])


def _merge(x2, ys, yg, yn, u2, ws, wg, wn, wo, tm=512):
    T = x2.shape[0]
    D = D_MODEL
    row = lambda i: (i, 0)
    fixed = lambda i: (0, 0)
    return pl.pallas_call(
        _merge_kernel,
        grid=(T // tm,),
        in_specs=[
            pl.BlockSpec((tm, D), row),
            pl.BlockSpec((tm, D), row),
            pl.BlockSpec((tm, D), row),
            pl.BlockSpec((tm, D), row),
            pl.BlockSpec((tm, N_BRANCH * D), row),
            pl.BlockSpec((D, D), fixed),
            pl.BlockSpec((D, D), fixed),
            pl.BlockSpec((D, D), fixed),
            pl.BlockSpec((D, D), fixed),
        ],
        out_specs=pl.BlockSpec((tm, D), row),
        out_shape=jax.ShapeDtypeStruct((T, D), F32),
        compiler_params=_cparams(("parallel",)),
        name="merge",
    )(x2, ys, yg, yn, u2, ws, wg, wn, wo)


def _mlp_kernel(x_ref, nw_ref, w1_ref, w2_ref, o_ref, *, tf):
    x = x_ref[...]
    ms = jnp.mean(x * x, axis=-1, keepdims=True)
    h = (x * lax.rsqrt(ms + EPS) * nw_ref[...]).astype(BF16)
    acc = x
    for f in range(D_FF // tf):
        a = jnp.maximum(_dot(h, w1_ref[:, f * tf:(f + 1) * tf]), 0.0)
        acc = acc + _dot((a * a).astype(BF16), w2_ref[f * tf:(f + 1) * tf, :])
    o_ref[...] = acc


def _mlp(x2, nw, w1, w2, tm=512, tf=1024):
    T = x2.shape[0]
    D = D_MODEL
    return pl.pallas_call(
        functools.partial(_mlp_kernel, tf=tf),
        grid=(T // tm,),
        in_specs=[
            pl.BlockSpec((tm, D), lambda i: (i, 0)),
            pl.BlockSpec((1, D), lambda i: (0, 0)),
            pl.BlockSpec((D, D_FF), lambda i: (0, 0)),
            pl.BlockSpec((D_FF, D), lambda i: (0, 0)),
        ],
        out_specs=pl.BlockSpec((tm, D), lambda i: (i, 0)),
        out_shape=jax.ShapeDtypeStruct((T, D), F32),
        compiler_params=_cparams(("parallel",)),
        name="mlp",
    )(x2, nw, w1, w2)


def _pad_rows(w, start, total):
    return jnp.zeros((total, w.shape[1]), w.dtype).at[start:start + w.shape[0]].set(w)


def kernel(x, norm_mix_w, w_in, ssd_conv_w, ssd_conv_b, ssd_dt_bias_f, ssd_dt_bias_b, ssd_a_log_f,
           ssd_a_log_b, ssd_d, ssd_norm_w, gla_a2_f, gla_a2_bias_f, gla_a2_b, gla_a2_bias_b,
           gla_norm_w, na_q_norm_w, na_k_norm_w, na_rpb, w_branch_ssd, w_branch_gla, w_branch_na,
           w_out, norm_mlp_w, w_ff1, w_ff2):
    B, S, D = x.shape
    T = B * S
    depth = w_in.shape[0]
    rows = S // GRID_W
    x2 = x.reshape(T, D)
    for l in range(depth):
        w_big = jnp.concatenate([w_in[l][:, a:a + n] for a, n in _BIG_SEGS], axis=1).astype(BF16)
        w_small = _small_weight(w_in[l]).astype(BF16)
        hg = (SSD_GROUPS, SSD_HG)
        zeros_r = jnp.zeros((SSD_GROUPS, LANE - 2 * SSD_HG), F32)
        dt_bias = jnp.concatenate([ssd_dt_bias_f[l].reshape(hg), ssd_dt_bias_b[l].reshape(hg), zeros_r], axis=1)
        a_neg = jnp.concatenate([-jnp.exp(ssd_a_log_f[l]).reshape(hg), -jnp.exp(ssd_a_log_b[l]).reshape(hg),
                                 zeros_r], axis=1)
        prow3 = jnp.zeros((SSD_GROUPS, 8, LANE), F32).at[:, 0].set(dt_bias).at[:, 1].set(a_neg)
        prow = prow3.reshape(SSD_GROUPS * 8, LANE)
        pcol = jnp.transpose(prow3, (0, 2, 1))
        drow = jnp.repeat(ssd_d[l], SSD_HEAD_DIM)[None, :]
        hk = (GLA_HEADS, 1, GLA_DK)
        a2 = jnp.concatenate([_pad_rows(gla_a2_f[l], SM_GAF, LANE).reshape((LANE,) + hk),
                              _pad_rows(gla_a2_b[l], SM_GAB, LANE).reshape((LANE,) + hk)],
                             axis=2).reshape(LANE, -1).astype(BF16)
        a2_bias = jnp.concatenate([gla_a2_bias_f[l].reshape(hk), gla_a2_bias_b[l].reshape(hk)],
                                  axis=1).reshape(1, -1)
        table = _na_bias_table(na_rpb[l], rows)
        qw = jnp.tile(na_q_norm_w[l], 2)[None, :]
        kw = jnp.tile(na_k_norm_w[l], 2)[None, :]

        u2, us2, ust = _inproj(x2, norm_mix_w[l][None, :], w_big, w_small, w_small.T)
        u3 = u2.reshape(B, S, U_WIDTH)
        us3 = us2.reshape(B, S, SMALL_W)
        xbc = _conv(u3, ssd_conv_w[l], ssd_conv_b[l][None, :])
        y_ssd = _ssd(xbc, u3, us3, ust, prow, pcol, drow, ssd_norm_w[l][None, :])
        y_gla = _gla(u3, us3, a2, a2_bias, gla_norm_w[l][None, :])
        y_na = _na(u3, table, qw, kw)
        x2 = _merge(x2, y_ssd.reshape(T, -1), y_gla.reshape(T, -1), y_na.reshape(T, -1), u2,
                    w_branch_ssd[l].astype(BF16), w_branch_gla[l].astype(BF16),
                    w_branch_na[l].astype(BF16), w_out[l].astype(BF16))
        x2 = _mlp(x2, norm_mlp_w[l][None, :], w_ff1[l].astype(BF16), w_ff2[l].astype(BF16))
    return x2.reshape(B, S, D)
```

```python
import functools

import jax
import jax.numpy as jnp
import numpy as np
from jax import lax
from jax.experimental import pallas as pl
from jax.experimental.pallas import tpu as pltpu

F32 = jnp.float32
BF16 = jnp.bfloat16

EPS = 1e-6
D_MODEL = 1024
GRID_W = 64

SSD_HEADS = 16
SSD_HEAD_DIM = 64
SSD_D_INNER = 1024
SSD_GROUPS = 2
SSD_STATE = 128
SSD_CONV = 5
SSD_CONV_DIM = 1536
SSD_CHUNK = 128
SSD_GROUP_W = SSD_D_INNER // SSD_GROUPS
SSD_HG = SSD_HEADS // SSD_GROUPS

GLA_HEADS = 4
GLA_DK = 128
GLA_DV = 256
GLA_KEY_W = 512
GLA_VAL_W = 1024
GLA_GATE_RANK = 16
GLA_GATE_NORM = 16.0
GLA_CHUNK = 64

NA_HEADS = 16
NA_HEAD_DIM = 64
NA_W = 1024
NA_WIN_H = 8
NA_WIN_W = 16
NA_ROWS_PER_STEP = 2

N_BRANCH = 3
D_FF = 4096

IN_SIZES = (SSD_D_INNER, SSD_CONV_DIM, SSD_HEADS, SSD_HEADS,
            GLA_KEY_W, GLA_KEY_W, GLA_VAL_W, GLA_VAL_W, GLA_GATE_RANK, GLA_GATE_RANK,
            NA_W, NA_W, NA_W, N_BRANCH * D_MODEL)
_IN_OFF = np.concatenate([[0], np.cumsum(IN_SIZES)])
(_O_Z, _O_XBC, _O_DTF, _O_DTB, _O_GQ, _O_GK, _O_GV, _O_GG, _O_GAF, _O_GAB,
 _O_NQ, _O_NK, _O_NV, _O_GATE) = [int(v) for v in _IN_OFF[:-1]]

U_GATE = 0
U_NQ = 3072
U_NK = 4096
U_NV = 5120
U_Z = 6144
U_GV = 7168
U_GG = 8192
U_XBC = 9216
U_GQ = 10752
U_GK = 11264
U_WIDTH = 11776
INPROJ_COL_CHUNK = 1024
LANE = 128
VMEM_LIMIT = 56 * 1024 * 1024

SMALL_W = SSD_GROUPS * LANE
SM_DTF, SM_DTB, SM_GAF, SM_GAB = 0, 8, 16, 32
SMALL_T_ROWS = 2 * SSD_HG

_BIG_SEGS = ((_O_GATE, 3072), (_O_NQ, 3072), (_O_Z, 1024), (_O_GV, 2048), (_O_XBC, 1536), (_O_GQ, 1024))


def _small_weight(w):
    blocks = []
    for g in range(SSD_GROUPS):
        h0 = g * SSD_HG
        cols = [w[:, _O_DTF + h0:_O_DTF + h0 + SSD_HG], w[:, _O_DTB + h0:_O_DTB + h0 + SSD_HG]]
        used = 2 * SSD_HG
        if g == 0:
            cols += [w[:, _O_GAF:_O_GAF + GLA_GATE_RANK], w[:, _O_GAB:_O_GAB + GLA_GATE_RANK]]
            used += 2 * GLA_GATE_RANK
        cols.append(jnp.zeros((w.shape[0], LANE - used), w.dtype))
        blocks += cols
    return jnp.concatenate(blocks, axis=1)


def _cparams(sem, vmem=VMEM_LIMIT):
    return pltpu.CompilerParams(dimension_semantics=sem, vmem_limit_bytes=vmem)


def _sigmoid(x):
    return 1.0 / (1.0 + jnp.exp(-x))


def _silu(x):
    return x * _sigmoid(x)


def _softplus(x):
    return jnp.maximum(x, 0.0) + jnp.log1p(jnp.exp(-jnp.abs(x)))


def _log_sigmoid(x):
    return jnp.minimum(x, 0.0) - jnp.log(1.0 + jnp.exp(-jnp.abs(x)))


def _nt(a, b):
    return lax.dot_general(a, b, (((1,), (1,)), ((), ())), preferred_element_type=F32)


def _tn(a, b):
    return lax.dot_general(a, b, (((0,), (0,)), ((), ())), preferred_element_type=F32)


def _dot(a, b):
    return jnp.dot(a, b, preferred_element_type=F32)


def _dot_exact(a, b):
    return jnp.dot(a, b, preferred_element_type=F32, precision=lax.Precision.HIGHEST)


def _iota2(shape, dim):
    return lax.broadcasted_iota(jnp.int32, shape, dim)


def _inproj_kernel(x_ref, nw_ref, w_ref, ws_ref, wst_ref, u_ref, us_ref, ust_ref, h_ref):
    @pl.when(pl.program_id(1) == 0)
    def _():
        x = x_ref[...]
        ms = jnp.mean(x * x, axis=-1, keepdims=True)
        h = (x * lax.rsqrt(ms + EPS) * nw_ref[...]).astype(BF16)
        h_ref[...] = h
        us_ref[...] = _dot(h, ws_ref[...])
        ust_ref[...] = _nt(wst_ref[...], h)

    tn = u_ref.shape[1]
    for a in range(0, tn, INPROJ_COL_CHUNK):
        b = min(a + INPROJ_COL_CHUNK, tn)
        u_ref[:, a:b] = _dot(h_ref[...], w_ref[:, a:b]).astype(BF16)


def _inproj(x2, nw, w_big, w_small, w_small_t, tm=1024, tn=U_WIDTH // 4):
    T = x2.shape[0]
    return pl.pallas_call(
        _inproj_kernel,
        grid=(T // tm, U_WIDTH // tn),
        in_specs=[
            pl.BlockSpec((tm, D_MODEL), lambda i, j: (i, 0)),
            pl.BlockSpec((1, D_MODEL), lambda i, j: (0, 0)),
            pl.BlockSpec((D_MODEL, tn), lambda i, j: (0, j)),
            pl.BlockSpec((D_MODEL, SMALL_W), lambda i, j: (0, 0)),
            pl.BlockSpec((SSD_GROUPS * SMALL_T_ROWS, D_MODEL), lambda i, j: (0, 0)),
        ],
        out_specs=[
            pl.BlockSpec((tm, tn), lambda i, j: (i, j)),
            pl.BlockSpec((tm, SMALL_W), lambda i, j: (i, 0)),
            pl.BlockSpec((SSD_GROUPS * SMALL_T_ROWS, tm), lambda i, j: (0, i)),
        ],
        out_shape=[
            jax.ShapeDtypeStruct((T, U_WIDTH), BF16),
            jax.ShapeDtypeStruct((T, SMALL_W), F32),
            jax.ShapeDtypeStruct((SSD_GROUPS * SMALL_T_ROWS, T), F32),
        ],
        scratch_shapes=[pltpu.VMEM((tm, D_MODEL), BF16)],
        compiler_params=_cparams(("parallel", "arbitrary")),
        name="inproj",
    )(x2, nw, w_big, w_small, w_small_t)


def _conv_kernel(u_ref, w_ref, b_ref, o_ref):
    x = u_ref[0].astype(F32)
    S = x.shape[0]
    row = _iota2(x.shape, 0)
    acc = jnp.zeros_like(x) + b_ref[...]
    pad = SSD_CONV // 2
    for k in range(SSD_CONV):
        d = k - pad
        if d == 0:
            xs = x
        else:
            xs = pltpu.roll(x, (-d) % S, axis=0)
            valid = (row + d >= 0) & (row + d < S)
            xs = jnp.where(valid, xs, 0.0)
        acc = acc + w_ref[k:k + 1, :] * xs
    o_ref[0] = _silu(acc).astype(BF16)


def _conv(u3, conv_w, conv_b, tc=256):
    B, S, _ = u3.shape
    nblk = SSD_CONV_DIM // tc
    off = U_XBC // tc
    return pl.pallas_call(
        _conv_kernel,
        grid=(B, nblk),
        in_specs=[
            pl.BlockSpec((1, S, tc), lambda b, c: (b, 0, off + c)),
            pl.BlockSpec((SSD_CONV, tc), lambda b, c: (0, c)),
            pl.BlockSpec((1, tc), lambda b, c: (0, c)),
        ],
        out_specs=pl.BlockSpec((1, S, tc), lambda b, c: (b, 0, c)),
        out_shape=jax.ShapeDtypeStruct((B, S, SSD_CONV_DIM), BF16),
        compiler_params=_cparams(("parallel", "parallel")),
        name="ssd_conv",
    )(u3, conv_w, conv_b)


def _split_hi_lo(v):
    hi = v.astype(BF16)
    lo = (v - hi.astype(F32)).astype(BF16)
    return jnp.concatenate([hi, lo], axis=1)


def _split3(v, axis):
    hi = v.astype(BF16)
    r1 = v - hi.astype(F32)
    mid = r1.astype(BF16)
    lo = (r1 - mid.astype(F32)).astype(BF16)
    return jnp.concatenate([hi, mid, lo], axis=axis)


def _ssd_kernel(x_ref, b_ref, c_ref, z_ref, us_ref, ust_ref, prow_ref, pcol_ref, drow_ref, nw_ref,
                o_ref, acc_ref, cumc_ref, cumr_ref, wdt_ref, dec_ref, xs_ref, decx_ref, sst_ref, st_ref):
    L = SSD_CHUNK
    S = x_ref.shape[1]
    nc = S // L
    W = SSD_GROUP_W
    HG = SSD_HG
    R = 2 * HG

    ii = _iota2((L, L), 0)
    jj = _iota2((L, L), 1)
    tril = ii >= jj
    triu = jj >= ii
    tril_b = tril.astype(BF16)
    triu_b = triu.astype(BF16)

    er = _iota2((LANE, 2 * W), 0)
    ec = _iota2((LANE, 2 * W), 1)
    e = (er == jnp.where(ec < W, ec // SSD_HEAD_DIM, (ec - W) // SSD_HEAD_DIM + HG)).astype(BF16)
    e2 = jnp.concatenate([e, e], axis=0)

    bias_row = prow_ref[0:1, :]
    a_row = prow_ref[1:2, :]
    bias_col = pcol_ref[0, :, 0:1]
    a_col = pcol_ref[0, :, 1:2]

    lane_fwd = _iota2((L, LANE), 1) < HG
    row_fwd = _iota2((R, L), 0) < HG
    lane_half = _iota2((L, LANE), 1) < SSD_HEAD_DIM

    def decay_body(c, carry):
        rs = pl.ds(pl.multiple_of(c * L, L), L)
        dt_c = _softplus(us_ref[0, rs, :] + bias_row)
        a_c = dt_c * a_row
        pp = _dot(tril_b, _split3(a_c, 1))
        p_c = pp[:, 0:LANE] + pp[:, LANE:2 * LANE] + pp[:, 2 * LANE:3 * LANE]
        tot_c = p_c[L - 1:L, :]
        cum_c = jnp.where(lane_fwd, p_c, tot_c - p_c + a_c)
        cumc_ref[rs, :] = cum_c
        wdt_ref[rs, :] = jnp.exp(tot_c - cum_c) * dt_c
        dec_ref[c] = jnp.broadcast_to(jnp.exp(tot_c), (16, LANE))
        dt_r = _softplus(ust_ref[:, rs] + bias_col)
        a_r = dt_r * a_col
        pr = _dot(_split3(a_r, 0), triu_b)
        p_r = pr[0:R] + pr[R:2 * R] + pr[2 * R:3 * R]
        tot_r = p_r[:, L - 1:L]
        cum_r = jnp.where(row_fwd, p_r, tot_r - p_r + a_r)
        cumr_ref[:, rs] = cum_r - jnp.log(dt_r)
        return carry

    lax.fori_loop(0, nc, decay_body, 0, unroll=2)

    def intra_body(c, carry):
        rs = pl.ds(pl.multiple_of(c * L, L), L)
        x_c = x_ref[0, rs, :]
        b_c = b_ref[0, rs, :]
        cb = _nt(c_ref[0, rs, :], b_c)
        ex = _dot(_split_hi_lo(jnp.concatenate([wdt_ref[rs, :], dec_ref[c]], axis=0)), e2)
        decx_ref[c] = ex[L:L + 8]
        xf = x_c.astype(F32)
        xw = jnp.concatenate([xf * ex[0:L, 0:W], xf * ex[0:L, W:2 * W]], axis=1).astype(BF16)
        b_t = b_c.astype(F32).T.astype(BF16)
        xs_ref[c] = _dot(b_t, xw)
        cum_c = cumc_ref[rs, :]
        cum_r = cumr_ref[:, rs]
        for hp in range(HG // 2):
            ms = []
            for hh in range(2):
                hf = 2 * hp + hh
                hb = HG + 2 * hp + hh
                decf = jnp.exp(jnp.where(tril, cum_c[:, hf:hf + 1] - cum_r[hf:hf + 1, :], -jnp.inf))
                decb = jnp.exp(jnp.where(triu, cum_c[:, hb:hb + 1] - cum_r[hb:hb + 1, :], -jnp.inf))
                ms.append((cb * (decf + decb)).astype(BF16))
            m2 = jnp.concatenate(ms, axis=1)
            xp = x_c[:, hp * LANE:(hp + 1) * LANE]
            xz = jnp.zeros_like(xp)
            x2 = jnp.concatenate([jnp.where(lane_half, xp, xz), jnp.where(lane_half, xz, xp)], axis=0)
            acc_ref[rs, hp * LANE:(hp + 1) * LANE] = _dot(m2, x2)
        return carry

    lax.fori_loop(0, nc, intra_body, 0, unroll=2)

    st_ref[...] = jnp.zeros_like(st_ref)

    def state_body(i, carry):
        for d, c in enumerate((i, nc - 1 - i)):
            ls = slice(d * W, (d + 1) * W)
            s_prev = st_ref[:, ls]
            sst_ref[c, :, ls] = s_prev.astype(BF16)
            st_ref[:, ls] = s_prev * decx_ref[c, 0:1, ls] + xs_ref[c, :, ls]
        return carry

    lax.fori_loop(0, nc, state_body, 0)

    def out_body(c, carry):
        rs = pl.ds(pl.multiple_of(c * L, L), L)
        yo = _dot(c_ref[0, rs, :], sst_ref[c])
        sc = _dot(_split_hi_lo(jnp.exp(cumc_ref[rs, :])), e2)
        xf = x_ref[0, rs, :].astype(F32)
        y = acc_ref[rs, :] + yo[:, 0:W] * sc[:, 0:W] + yo[:, W:2 * W] * sc[:, W:2 * W] + xf * drow_ref[...]
        y = y * _silu(z_ref[0, rs, :].astype(F32))
        ms = jnp.mean(y * y, axis=-1, keepdims=True)
        o_ref[0, rs, :] = (y * lax.rsqrt(ms + EPS) * nw_ref[...]).astype(BF16)
        return carry

    lax.fori_loop(0, nc, out_body, 0, unroll=2)


def _ssd(xbc, u3, us3, ust, prow, pcol, drow, nw):
    B, S, _ = xbc.shape
    W = SSD_GROUP_W
    N = SSD_STATE
    nc = S // SSD_CHUNK
    return pl.pallas_call(
        _ssd_kernel,
        grid=(B, SSD_GROUPS),
        in_specs=[
            pl.BlockSpec((1, S, W), lambda b, g: (b, 0, g)),
            pl.BlockSpec((1, S, N), lambda b, g: (b, 0, SSD_D_INNER // N + g)),
            pl.BlockSpec((1, S, N), lambda b, g: (b, 0, SSD_D_INNER // N + SSD_GROUPS + g)),
            pl.BlockSpec((1, S, W), lambda b, g: (b, 0, U_Z // W + g)),
            pl.BlockSpec((1, S, LANE), lambda b, g: (b, 0, g)),
            pl.BlockSpec((SMALL_T_ROWS, S), lambda b, g: (g, b)),
            pl.BlockSpec((8, LANE), lambda b, g: (g, 0)),
            pl.BlockSpec((1, SMALL_T_ROWS, 8), lambda b, g: (g, 0, 0)),
            pl.BlockSpec((1, W), lambda b, g: (0, g)),
            pl.BlockSpec((1, W), lambda b, g: (0, g)),
        ],
        out_specs=pl.BlockSpec((1, S, W), lambda b, g: (b, 0, g)),
        out_shape=jax.ShapeDtypeStruct((B, S, SSD_D_INNER), BF16),
        scratch_shapes=[
            pltpu.VMEM((S, W), F32),
            pltpu.VMEM((S, LANE), F32),
            pltpu.VMEM((SMALL_T_ROWS, S), F32),
            pltpu.VMEM((S, LANE), F32),
            pltpu.VMEM((nc, 16, LANE), F32),
            pltpu.VMEM((nc, N, 2 * W), F32),
            pltpu.VMEM((nc, 8, 2 * W), F32),
            pltpu.VMEM((nc, N, 2 * W), BF16),
            pltpu.VMEM((N, 2 * W), F32),
        ],
        compiler_params=_cparams(("parallel", "parallel")),
        name="ssd_scan",
    )(xbc, xbc, xbc, u3, us3, ust, prow, pcol, drow, nw)


def _gla_kernel(q_ref, k_ref, v_ref, gg_ref, us_ref, a2_ref, bias_ref, nw_ref,
                o_ref, acc_ref, g_ref, qd_ref, kd_ref, kdp_ref, qcat_ref, x_ref, dec_ref, sst_ref, st_ref):
    L = GLA_CHUNK
    BL = 2 * L
    DK = GLA_DK
    S = q_ref.shape[1]
    nb = S // BL
    scale = DK ** -0.5

    ii = _iota2((BL, BL), 0)
    jj = _iota2((BL, BL), 1)
    same = (ii // L) == (jj // L)
    masks = (same & (ii >= jj), same & (jj >= ii))
    tri2 = masks[0].astype(BF16)
    par_row = _iota2((BL, DK), 0) // L

    ga = us_ref[0].astype(BF16)
    g_ref[...] = _log_sigmoid(_dot(ga, a2_ref[...]) + bias_ref[...]) / GLA_GATE_NORM

    def decay_body(i, carry):
        rs = pl.ds(pl.multiple_of(i * BL, BL), BL)
        g = g_ref[rs, :]
        hi = g.astype(BF16)
        r1 = g - hi.astype(F32)
        mid = r1.astype(BF16)
        lo = (r1 - mid.astype(F32)).astype(BF16)
        pp = _dot(tri2, jnp.concatenate([hi, mid, lo], axis=1))
        p = pp[:, 0:2 * DK] + pp[:, 2 * DK:4 * DK] + pp[:, 4 * DK:6 * DK]
        q_c = q_ref[0, rs, :].astype(F32) * scale
        k_c = k_ref[0, rs, :].astype(F32)
        zero = jnp.zeros((BL, DK), BF16)
        for d in range(2):
            p_d = p[:, d * DK:(d + 1) * DK]
            tot = jnp.where(par_row == 0, p_d[L - 1:L, :], p_d[BL - 1:BL, :])
            b = p_d if d == 0 else tot - p_d + g[:, DK:]
            qd = (q_c * jnp.exp(b)).astype(BF16)
            kdec = (k_c * jnp.exp(tot - b)).astype(BF16)
            qd_ref[d, rs, :] = qd
            kd_ref[d, rs, :] = (k_c * jnp.exp(-b)).astype(BF16)
            for par in range(2):
                sel = par_row == par
                kdp_ref[d, rs, par * DK:(par + 1) * DK] = jnp.where(sel, kdec, zero)
                qcat_ref[rs, (2 * d + par) * DK:(2 * d + par + 1) * DK] = jnp.where(sel, qd, zero)
                last = (par + 1) * L - 1
                dec_ref[d, 2 * i + par] = jnp.broadcast_to(jnp.exp(p_d[last:last + 1, :]), (8, DK))
        return carry

    lax.fori_loop(0, nb, decay_body, 0, unroll=2)

    def intra_body(i, carry):
        blks = (2 * i, 2 * i + 1)
        rss = [pl.ds(pl.multiple_of(b * BL, BL), BL) for b in blks]
        vs = [v_ref[0, rs, :] for rs in rss]
        atts = [[_nt(qd_ref[d, rs, :], kd_ref[d, rs, :]) for d in range(2)] for rs in rss]
        for j, b in enumerate(blks):
            for d in range(2):
                x_ref[d, b] = _tn(vs[j], kdp_ref[d, rss[j], :])
        for j in range(2):
            att = jnp.where(masks[0], atts[j][0], 0.0) + jnp.where(masks[1], atts[j][1], 0.0)
            acc_ref[rss[j], :] = _dot(att.astype(BF16), vs[j])
        return carry

    lax.fori_loop(0, nb // 2, intra_body, 0)

    st_ref[...] = jnp.zeros_like(st_ref)

    def state_body(i, carry):
        for d, b in enumerate((i, nb - 1 - i)):
            for par in ((0, 1) if d == 0 else (1, 0)):
                s_prev = st_ref[d]
                lane0 = (2 * d + par) * DK
                sst_ref[b, :, lane0:lane0 + DK] = s_prev.astype(BF16)
                st_ref[d] = s_prev * dec_ref[d, 2 * b + par, 0:1, :] + x_ref[d, b, :, par * DK:(par + 1) * DK]
        return carry

    lax.fori_loop(0, nb, state_body, 0)

    def out_body(b, carry):
        rs = pl.ds(pl.multiple_of(b * BL, BL), BL)
        o = acc_ref[rs, :] + _nt(qcat_ref[rs, :], sst_ref[b])
        ms = jnp.mean(o * o, axis=-1, keepdims=True)
        o = o * lax.rsqrt(ms + EPS) * nw_ref[...]
        o_ref[0, rs, :] = (o * _silu(gg_ref[0, rs, :].astype(F32))).astype(BF16)
        return carry

    lax.fori_loop(0, nb, out_body, 0, unroll=2)


def _gla(u3, us3, a2, bias, nw):
    B, S, _ = u3.shape
    DK, DV = GLA_DK, GLA_DV
    nc = S // GLA_CHUNK
    nb = nc // 2
    return pl.pallas_call(
        _gla_kernel,
        grid=(B, GLA_HEADS),
        in_specs=[
            pl.BlockSpec((1, S, DK), lambda b, h: (b, 0, U_GQ // DK + h)),
            pl.BlockSpec((1, S, DK), lambda b, h: (b, 0, U_GK // DK + h)),
            pl.BlockSpec((1, S, DV), lambda b, h: (b, 0, U_GV // DV + h)),
            pl.BlockSpec((1, S, DV), lambda b, h: (b, 0, U_GG // DV + h)),
            pl.BlockSpec((1, S, LANE), lambda b, h: (b, 0, 0)),
            pl.BlockSpec((LANE, 2 * DK), lambda b, h: (0, h)),
            pl.BlockSpec((1, 2 * DK), lambda b, h: (0, h)),
            pl.BlockSpec((1, DV), lambda b, h: (0, 0)),
        ],
        out_specs=pl.BlockSpec((1, S, DV), lambda b, h: (b, 0, h)),
        out_shape=jax.ShapeDtypeStruct((B, S, GLA_VAL_W), BF16),
        scratch_shapes=[
            pltpu.VMEM((S, DV), F32),
            pltpu.VMEM((S, 2 * DK), F32),
            pltpu.VMEM((2, S, DK), BF16),
            pltpu.VMEM((2, S, DK), BF16),
            pltpu.VMEM((2, S, 2 * DK), BF16),
            pltpu.VMEM((S, 4 * DK), BF16),
            pltpu.VMEM((2, nb, DV, 2 * DK), F32),
            pltpu.VMEM((2, nc, 8, DK), F32),
            pltpu.VMEM((nb, DV, 4 * DK), BF16),
            pltpu.VMEM((2, DV, DK), F32),
        ],
        compiler_params=_cparams(("parallel", "parallel")),
        name="gla_scan",
    )(u3, u3, u3, u3, us3, a2, bias, nw)


def _na_col_tables():
    qcol = np.arange(GRID_W)[:, None]
    kcol = np.arange(GRID_W)[None, :]
    w_start = np.clip(qcol - NA_WIN_W // 2, 0, GRID_W - NA_WIN_W)
    mask = (kcol >= w_start) & (kcol < w_start + NA_WIN_W)
    off = np.clip(kcol - qcol, -(NA_WIN_W - 1), NA_WIN_W - 1) + (NA_WIN_W - 1)
    return mask, off


def _na_bias_table(rpb, rows):
    win_h = min(NA_WIN_H, rows)
    mask, off = _na_col_tables()
    H = rpb.shape[0]
    a = jnp.where(mask[None, None], rpb[:, :, off], -jnp.inf)
    a = a.reshape(H // 2, 2, 2 * NA_WIN_H - 1, GRID_W, GRID_W)
    per_delta = []
    for delta in range(win_h):
        lo = NA_WIN_H - 1 - delta
        per_delta.append(jnp.concatenate([a[:, :, lo + w] for w in range(win_h)], axis=-1))
    t = jnp.stack(per_delta, axis=1)
    return t.reshape(H // 2, win_h, 2 * GRID_W, win_h * GRID_W).astype(F32)


def _na_kernel(q_ref, k_ref, v_ref, tab_ref, qw_ref, kw_ref, o_ref, q0_ref, q1_ref, kn_ref):
    S = q_ref.shape[1]
    rows = S // GRID_W
    win_h = tab_ref.shape[1]
    nk = win_h * GRID_W
    scale = NA_HEAD_DIM ** -0.5
    RB = 128

    er = _iota2((LANE, LANE), 0) // NA_HEAD_DIM
    ec = _iota2((LANE, LANE), 1) // NA_HEAD_DIM
    e_blk = (er == ec).astype(BF16)
    e2 = jnp.concatenate([e_blk, e_blk], axis=0)
    first = _iota2((RB, LANE), 1) < NA_HEAD_DIM

    def norm_body(i, carry):
        rs = pl.ds(pl.multiple_of(i * RB, RB), RB)
        q = q_ref[0, rs, :].astype(F32)
        ms = _dot(_split_hi_lo(q * q), e2) * (1.0 / NA_HEAD_DIM)
        qn = (q * lax.rsqrt(ms + EPS) * qw_ref[...] * scale).astype(BF16)
        zero = jnp.zeros_like(qn)
        q0_ref[rs, :] = jnp.where(first, qn, zero)
        q1_ref[rs, :] = jnp.where(first, zero, qn)
        k = k_ref[0, rs, :].astype(F32)
        ms = _dot(_split_hi_lo(k * k), e2) * (1.0 / NA_HEAD_DIM)
        kn_ref[rs, :] = (k * lax.rsqrt(ms + EPS) * kw_ref[...]).astype(BF16)
        return carry

    lax.fori_loop(0, S // RB, norm_body, 0)

    first_q = _iota2((GRID_W, LANE), 1) < NA_HEAD_DIM

    def key_start(r):
        r0 = jnp.clip(r - win_h // 2, 0, rows - win_h)
        return r0, pl.multiple_of(r0 * GRID_W, GRID_W)

    def scores(r):
        r0, k0 = key_start(r)
        q0 = pl.multiple_of(r * GRID_W, GRID_W)
        qs = jnp.concatenate([q0_ref[pl.ds(q0, GRID_W), :], q1_ref[pl.ds(q0, GRID_W), :]], axis=0)
        return _nt(qs, kn_ref[pl.ds(k0, nk), :]) + tab_ref[0, r - r0]

    def softmax(s):
        m = jnp.max(s, axis=-1, keepdims=True)
        p = jnp.exp(s - m)
        return p.astype(BF16), jnp.sum(p, axis=-1, keepdims=True)

    def attend(r, p, den):
        _, k0 = key_start(r)
        o = _dot(p, v_ref[0, pl.ds(k0, nk), :]) / den
        q0 = pl.multiple_of(r * GRID_W, GRID_W)
        o_ref[0, pl.ds(q0, GRID_W), :] = jnp.where(first_q, o[:GRID_W], o[GRID_W:]).astype(BF16)

    U = NA_ROWS_PER_STEP

    def row_body(i, carry):
        s_cur, pd_prev = carry
        r = i * U
        for j in range(U):
            attend(r - U + j, *pd_prev[j])
        pd = tuple(softmax(s) for s in s_cur)
        s_next = tuple(scores(jnp.minimum(r + U + j, rows - 1)) for j in range(U))
        return s_next, pd

    pd0 = tuple(softmax(scores(j)) for j in range(U))
    s1 = tuple(scores(U + j) for j in range(U))
    _, pd_last = lax.fori_loop(1, rows // U, row_body, (s1, pd0))
    for j in range(U):
        attend(rows - U + j, *pd_last[j])


def _na(u3, table, qw, kw):
    B, S, _ = u3.shape
    win_h = table.shape[1]
    return pl.pallas_call(
        _na_kernel,
        grid=(NA_HEADS // 2, B),
        in_specs=[
            pl.BlockSpec((1, S, LANE), lambda h, b: (b, 0, U_NQ // LANE + h)),
            pl.BlockSpec((1, S, LANE), lambda h, b: (b, 0, U_NK // LANE + h)),
            pl.BlockSpec((1, S, LANE), lambda h, b: (b, 0, U_NV // LANE + h)),
            pl.BlockSpec((1, win_h, 2 * GRID_W, win_h * GRID_W), lambda h, b: (h, 0, 0, 0)),
            pl.BlockSpec((1, LANE), lambda h, b: (0, 0)),
            pl.BlockSpec((1, LANE), lambda h, b: (0, 0)),
        ],
        out_specs=pl.BlockSpec((1, S, LANE), lambda h, b: (b, 0, h)),
        out_shape=jax.ShapeDtypeStruct((B, S, NA_W), BF16),
        scratch_shapes=[pltpu.VMEM((S, LANE), BF16)] * 3,
        compiler_params=_cparams(("parallel", "parallel")),
        name="na_attn",
    )(u3, u3, u3, table, qw, kw)


def _merge_kernel(x_ref, ys_ref, yg_ref, yn_ref, gate_ref, ws_ref, wg_ref, wn_ref, wo_ref, o_ref):
    D = D_MODEL
    mixed = _sigmoid(gate_ref[:, 0:D].astype(F32)) * _dot(ys_ref[...], ws_ref[...])
    mixed += _sigmoid(gate_ref[:, D:2 * D].astype(F32)) * _dot(yg_ref[...], wg_ref[...])
    mixed += _sigmoid(gate_ref[:, 2 * D:3 * D].astype(F32)) * _dot(yn_ref[...], wn_ref[...])
    o_ref[...] = x_ref[...] + _dot(mixed.astype(BF16), wo_ref[...])


def _merge(x2, ys, yg, yn, u2, ws, wg, wn, wo, tm=512):
    T = x2.shape[0]
    D = D_MODEL
    row = lambda i: (i, 0)
    fixed = lambda i: (0, 0)
    return pl.pallas_call(
        _merge_kernel,
        grid=(T // tm,),
        in_specs=[
            pl.BlockSpec((tm, D), row),
            pl.BlockSpec((tm, D), row),
            pl.BlockSpec((tm, D), row),
            pl.BlockSpec((tm, D), row),
            pl.BlockSpec((tm, N_BRANCH * D), row),
            pl.BlockSpec((D, D), fixed),
            pl.BlockSpec((D, D), fixed),
            pl.BlockSpec((D, D), fixed),
            pl.BlockSpec((D, D), fixed),
        ],
        out_specs=pl.BlockSpec((tm, D), row),
        out_shape=jax.ShapeDtypeStruct((T, D), F32),
        compiler_params=_cparams(("parallel",)),
        name="merge",
    )(x2, ys, yg, yn, u2, ws, wg, wn, wo)


def _mlp_kernel(x_ref, nw_ref, w1_ref, w2_ref, o_ref, *, tf):
    x = x_ref[...]
    ms = jnp.mean(x * x, axis=-1, keepdims=True)
    h = (x * lax.rsqrt(ms + EPS) * nw_ref[...]).astype(BF16)
    acc = x
    for f in range(D_FF // tf):
        a = jnp.maximum(_dot(h, w1_ref[:, f * tf:(f + 1) * tf]), 0.0)
        acc = acc + _dot((a * a).astype(BF16), w2_ref[f * tf:(f + 1) * tf, :])
    o_ref[...] = acc


def _mlp(x2, nw, w1, w2, tm=512, tf=1024):
    T = x2.shape[0]
    D = D_MODEL
    return pl.pallas_call(
        functools.partial(_mlp_kernel, tf=tf),
        grid=(T // tm,),
        in_specs=[
            pl.BlockSpec((tm, D), lambda i: (i, 0)),
            pl.BlockSpec((1, D), lambda i: (0, 0)),
            pl.BlockSpec((D, D_FF), lambda i: (0, 0)),
            pl.BlockSpec((D_FF, D), lambda i: (0, 0)),
        ],
        out_specs=pl.BlockSpec((tm, D), lambda i: (i, 0)),
        out_shape=jax.ShapeDtypeStruct((T, D), F32),
        compiler_params=_cparams(("parallel",)),
        name="mlp",
    )(x2, nw, w1, w2)


def _pad_rows(w, start, total):
    return jnp.zeros((total, w.shape[1]), w.dtype).at[start:start + w.shape[0]].set(w)


def kernel(x, norm_mix_w, w_in, ssd_conv_w, ssd_conv_b, ssd_dt_bias_f, ssd_dt_bias_b, ssd_a_log_f,
           ssd_a_log_b, ssd_d, ssd_norm_w, gla_a2_f, gla_a2_bias_f, gla_a2_b, gla_a2_bias_b,
           gla_norm_w, na_q_norm_w, na_k_norm_w, na_rpb, w_branch_ssd, w_branch_gla, w_branch_na,
           w_out, norm_mlp_w, w_ff1, w_ff2):
    B, S, D = x.shape
    T = B * S
    depth = w_in.shape[0]
    rows = S // GRID_W
    x2 = x.reshape(T, D)
    for l in range(depth):
        w_big = jnp.concatenate([w_in[l][:, a:a + n] for a, n in _BIG_SEGS], axis=1).astype(BF16)
        w_small = _small_weight(w_in[l]).astype(BF16)
        hg = (SSD_GROUPS, SSD_HG)
        zeros_r = jnp.zeros((SSD_GROUPS, LANE - 2 * SSD_HG), F32)
        dt_bias = jnp.concatenate([ssd_dt_bias_f[l].reshape(hg), ssd_dt_bias_b[l].reshape(hg), zeros_r], axis=1)
        a_neg = jnp.concatenate([-jnp.exp(ssd_a_log_f[l]).reshape(hg), -jnp.exp(ssd_a_log_b[l]).reshape(hg),
                                 zeros_r], axis=1)
        prow3 = jnp.zeros((SSD_GROUPS, 8, LANE), F32).at[:, 0].set(dt_bias).at[:, 1].set(a_neg)
        prow = prow3.reshape(SSD_GROUPS * 8, LANE)
        pcol = jnp.transpose(prow3[:, :, :SMALL_T_ROWS], (0, 2, 1))
        w_small_t = w_small.T.reshape(SSD_GROUPS, LANE, D)[:, :SMALL_T_ROWS].reshape(-1, D)
        drow = jnp.repeat(ssd_d[l], SSD_HEAD_DIM)[None, :]
        hk = (GLA_HEADS, 1, GLA_DK)
        a2 = jnp.concatenate([_pad_rows(gla_a2_f[l], SM_GAF, LANE).reshape((LANE,) + hk),
                              _pad_rows(gla_a2_b[l], SM_GAB, LANE).reshape((LANE,) + hk)],
                             axis=2).reshape(LANE, -1).astype(BF16)
        a2_bias = jnp.concatenate([gla_a2_bias_f[l].reshape(hk), gla_a2_bias_b[l].reshape(hk)],
                                  axis=1).reshape(1, -1)
        table = _na_bias_table(na_rpb[l], rows)
        qw = jnp.tile(na_q_norm_w[l], 2)[None, :]
        kw = jnp.tile(na_k_norm_w[l], 2)[None, :]

        u2, us2, ust = _inproj(x2, norm_mix_w[l][None, :], w_big, w_small, w_small_t)
        u3 = u2.reshape(B, S, U_WIDTH)
        us3 = us2.reshape(B, S, SMALL_W)
        xbc = _conv(u3, ssd_conv_w[l], ssd_conv_b[l][None, :])
        y_ssd = _ssd(xbc, u3, us3, ust, prow, pcol, drow, ssd_norm_w[l][None, :])
        y_gla = _gla(u3, us3, a2, a2_bias, gla_norm_w[l][None, :])
        y_na = _na(u3, table, qw, kw)
        x2 = _merge(x2, y_ssd.reshape(T, -1), y_gla.reshape(T, -1), y_na.reshape(T, -1), u2,
                    w_branch_ssd[l].astype(BF16), w_branch_gla[l].astype(BF16),
                    w_branch_na[l].astype(BF16), w_out[l].astype(BF16))
        x2 = _mlp(x2, norm_mlp_w[l][None, :], w_ff1[l].astype(BF16), w_ff2[l].astype(BF16))
    return x2.reshape(B, S, D)
```

```python
import functools

import jax
import jax.numpy as jnp
import numpy as np
from jax import lax
from jax.experimental import pallas as pl
from jax.experimental.pallas import tpu as pltpu

F32 = jnp.float32
BF16 = jnp.bfloat16

EPS = 1e-6
D_MODEL = 1024
GRID_W = 64

SSD_HEADS = 16
SSD_HEAD_DIM = 64
SSD_D_INNER = 1024
SSD_GROUPS = 2
SSD_STATE = 128
SSD_CONV = 5
SSD_CONV_DIM = 1536
SSD_CHUNK = 128
SSD_GROUP_W = SSD_D_INNER // SSD_GROUPS
SSD_HG = SSD_HEADS // SSD_GROUPS

GLA_HEADS = 4
GLA_DK = 128
GLA_DV = 256
GLA_KEY_W = 512
GLA_VAL_W = 1024
GLA_GATE_RANK = 16
GLA_GATE_NORM = 16.0
GLA_CHUNK = 64

NA_HEADS = 16
NA_HEAD_DIM = 64
NA_W = 1024
NA_WIN_H = 8
NA_WIN_W = 16
NA_ROWS_PER_STEP = 8
NA_ROWS_PER_STEP_EXACT = 2
LOG2E = 1.4426950408889634
NA_BOUND_SLACK = 0.02
NA_MAX_BOUND_GAP = 90.0

N_BRANCH = 3
D_FF = 4096

IN_SIZES = (SSD_D_INNER, SSD_CONV_DIM, SSD_HEADS, SSD_HEADS,
            GLA_KEY_W, GLA_KEY_W, GLA_VAL_W, GLA_VAL_W, GLA_GATE_RANK, GLA_GATE_RANK,
            NA_W, NA_W, NA_W, N_BRANCH * D_MODEL)
_IN_OFF = np.concatenate([[0], np.cumsum(IN_SIZES)])
(_O_Z, _O_XBC, _O_DTF, _O_DTB, _O_GQ, _O_GK, _O_GV, _O_GG, _O_GAF, _O_GAB,
 _O_NQ, _O_NK, _O_NV, _O_GATE) = [int(v) for v in _IN_OFF[:-1]]

U_GATE = 0
U_NQ = 3072
U_NK = 4096
U_NV = 5120
U_Z = 6144
U_GV = 7168
U_GG = 8192
U_XBC = 9216
U_GQ = 10752
U_GK = 11264
U_WIDTH = 11776
INPROJ_COL_CHUNK = 1024
LANE = 128
VMEM_LIMIT = 56 * 1024 * 1024

SMALL_W = SSD_GROUPS * LANE
SM_DTF, SM_DTB, SM_GAF, SM_GAB = 0, 8, 16, 32
SMALL_T_ROWS = 2 * SSD_HG

_BIG_SEGS = ((_O_GATE, 3072), (_O_NQ, 3072), (_O_Z, 1024), (_O_GV, 2048), (_O_XBC, 1536), (_O_GQ, 1024))


def _small_weight(w):
    blocks = []
    for g in range(SSD_GROUPS):
        h0 = g * SSD_HG
        cols = [w[:, _O_DTF + h0:_O_DTF + h0 + SSD_HG], w[:, _O_DTB + h0:_O_DTB + h0 + SSD_HG]]
        used = 2 * SSD_HG
        if g == 0:
            cols += [w[:, _O_GAF:_O_GAF + GLA_GATE_RANK], w[:, _O_GAB:_O_GAB + GLA_GATE_RANK]]
            used += 2 * GLA_GATE_RANK
        cols.append(jnp.zeros((w.shape[0], LANE - used), w.dtype))
        blocks += cols
    return jnp.concatenate(blocks, axis=1)


def _cparams(sem, vmem=VMEM_LIMIT):
    return pltpu.CompilerParams(dimension_semantics=sem, vmem_limit_bytes=vmem)


def _sigmoid(x):
    return 1.0 / (1.0 + jnp.exp(-x))


def _silu(x):
    return x * _sigmoid(x)


def _softplus(x):
    return jnp.maximum(x, 0.0) + jnp.log1p(jnp.exp(-jnp.abs(x)))


def _log_sigmoid(x):
    return jnp.minimum(x, 0.0) - jnp.log(1.0 + jnp.exp(-jnp.abs(x)))


def _nt(a, b):
    return lax.dot_general(a, b, (((1,), (1,)), ((), ())), preferred_element_type=F32)


def _tn(a, b):
    return lax.dot_general(a, b, (((0,), (0,)), ((), ())), preferred_element_type=F32)


def _dot(a, b):
    return jnp.dot(a, b, preferred_element_type=F32)


def _dot_exact(a, b):
    return jnp.dot(a, b, preferred_element_type=F32, precision=lax.Precision.HIGHEST)


def _iota2(shape, dim):
    return lax.broadcasted_iota(jnp.int32, shape, dim)


def _inproj_kernel(x_ref, nw_ref, w_ref, ws_ref, wst_ref, u_ref, us_ref, ust_ref, h_ref):
    @pl.when(pl.program_id(1) == 0)
    def _():
        x = x_ref[...]
        ms = jnp.mean(x * x, axis=-1, keepdims=True)
        h = (x * lax.rsqrt(ms + EPS) * nw_ref[...]).astype(BF16)
        h_ref[...] = h
        us_ref[...] = _dot(h, ws_ref[...])
        ust_ref[...] = _nt(wst_ref[...], h)

    tn = u_ref.shape[1]
    for a in range(0, tn, INPROJ_COL_CHUNK):
        b = min(a + INPROJ_COL_CHUNK, tn)
        u_ref[:, a:b] = _dot(h_ref[...], w_ref[:, a:b]).astype(BF16)


def _inproj(x2, nw, w_big, w_small, w_small_t, tm=1024, tn=U_WIDTH // 4):
    T = x2.shape[0]
    return pl.pallas_call(
        _inproj_kernel,
        grid=(T // tm, U_WIDTH // tn),
        in_specs=[
            pl.BlockSpec((tm, D_MODEL), lambda i, j: (i, 0)),
            pl.BlockSpec((1, D_MODEL), lambda i, j: (0, 0)),
            pl.BlockSpec((D_MODEL, tn), lambda i, j: (0, j)),
            pl.BlockSpec((D_MODEL, SMALL_W), lambda i, j: (0, 0)),
            pl.BlockSpec((SSD_GROUPS * SMALL_T_ROWS, D_MODEL), lambda i, j: (0, 0)),
        ],
        out_specs=[
            pl.BlockSpec((tm, tn), lambda i, j: (i, j)),
            pl.BlockSpec((tm, SMALL_W), lambda i, j: (i, 0)),
            pl.BlockSpec((SSD_GROUPS * SMALL_T_ROWS, tm), lambda i, j: (0, i)),
        ],
        out_shape=[
            jax.ShapeDtypeStruct((T, U_WIDTH), BF16),
            jax.ShapeDtypeStruct((T, SMALL_W), F32),
            jax.ShapeDtypeStruct((SSD_GROUPS * SMALL_T_ROWS, T), F32),
        ],
        scratch_shapes=[pltpu.VMEM((tm, D_MODEL), BF16)],
        compiler_params=_cparams(("parallel", "arbitrary")),
        name="inproj",
    )(x2, nw, w_big, w_small, w_small_t)


def _conv_kernel(u_ref, w_ref, b_ref, o_ref):
    x = u_ref[0].astype(F32)
    S = x.shape[0]
    row = _iota2(x.shape, 0)
    acc = jnp.zeros_like(x) + b_ref[...]
    pad = SSD_CONV // 2
    for k in range(SSD_CONV):
        d = k - pad
        if d == 0:
            xs = x
        else:
            xs = pltpu.roll(x, (-d) % S, axis=0)
            valid = (row + d >= 0) & (row + d < S)
            xs = jnp.where(valid, xs, 0.0)
        acc = acc + w_ref[k:k + 1, :] * xs
    o_ref[0] = _silu(acc).astype(BF16)


def _conv(u3, conv_w, conv_b, tc=256):
    B, S, _ = u3.shape
    nblk = SSD_CONV_DIM // tc
    off = U_XBC // tc
    return pl.pallas_call(
        _conv_kernel,
        grid=(B, nblk),
        in_specs=[
            pl.BlockSpec((1, S, tc), lambda b, c: (b, 0, off + c)),
            pl.BlockSpec((SSD_CONV, tc), lambda b, c: (0, c)),
            pl.BlockSpec((1, tc), lambda b, c: (0, c)),
        ],
        out_specs=pl.BlockSpec((1, S, tc), lambda b, c: (b, 0, c)),
        out_shape=jax.ShapeDtypeStruct((B, S, SSD_CONV_DIM), BF16),
        compiler_params=_cparams(("parallel", "parallel")),
        name="ssd_conv",
    )(u3, conv_w, conv_b)


def _split_hi_lo(v):
    hi = v.astype(BF16)
    lo = (v - hi.astype(F32)).astype(BF16)
    return jnp.concatenate([hi, lo], axis=1)


def _split3(v, axis):
    hi = v.astype(BF16)
    r1 = v - hi.astype(F32)
    mid = r1.astype(BF16)
    lo = (r1 - mid.astype(F32)).astype(BF16)
    return jnp.concatenate([hi, mid, lo], axis=axis)


def _ssd_kernel(x_ref, b_ref, c_ref, z_ref, us_ref, ust_ref, prow_ref, pcol_ref, drow_ref, nw_ref,
                o_ref, acc_ref, cumc_ref, cumr_ref, wdt_ref, dec_ref, xs_ref, decx_ref, sst_ref, st_ref):
    L = SSD_CHUNK
    S = x_ref.shape[1]
    nc = S // L
    W = SSD_GROUP_W
    HG = SSD_HG
    R = 2 * HG

    ii = _iota2((L, L), 0)
    jj = _iota2((L, L), 1)
    tril = ii >= jj
    triu = jj >= ii
    tril_b = tril.astype(BF16)
    triu_b = triu.astype(BF16)

    er = _iota2((LANE, 2 * W), 0)
    ec = _iota2((LANE, 2 * W), 1)
    e = (er == jnp.where(ec < W, ec // SSD_HEAD_DIM, (ec - W) // SSD_HEAD_DIM + HG)).astype(BF16)
    e2 = jnp.concatenate([e, e], axis=0)

    bias_row = prow_ref[0:1, :]
    a_row = prow_ref[1:2, :]
    bias_col = pcol_ref[0, :, 0:1]
    a_col = pcol_ref[0, :, 1:2]

    lane_fwd = _iota2((L, LANE), 1) < HG
    row_fwd = _iota2((R, L), 0) < HG
    lane_half = _iota2((L, LANE), 1) < SSD_HEAD_DIM

    def decay_body(c, carry):
        rs = pl.ds(pl.multiple_of(c * L, L), L)
        dt_c = _softplus(us_ref[0, rs, :] + bias_row)
        a_c = dt_c * a_row
        pp = _dot(tril_b, _split3(a_c, 1))
        p_c = pp[:, 0:LANE] + pp[:, LANE:2 * LANE] + pp[:, 2 * LANE:3 * LANE]
        tot_c = p_c[L - 1:L, :]
        cum_c = jnp.where(lane_fwd, p_c, tot_c - p_c + a_c)
        cumc_ref[rs, :] = cum_c
        wdt_ref[rs, :] = jnp.exp(tot_c - cum_c) * dt_c
        dec_ref[c] = jnp.broadcast_to(jnp.exp(tot_c), (16, LANE))
        dt_r = _softplus(ust_ref[:, rs] + bias_col)
        a_r = dt_r * a_col
        pr = _dot(_split3(a_r, 0), triu_b)
        p_r = pr[0:R] + pr[R:2 * R] + pr[2 * R:3 * R]
        tot_r = p_r[:, L - 1:L]
        cum_r = jnp.where(row_fwd, p_r, tot_r - p_r + a_r)
        cumr_ref[:, rs] = cum_r - jnp.log(dt_r)
        return carry

    lax.fori_loop(0, nc, decay_body, 0, unroll=2)

    def intra_body(c, carry):
        rs = pl.ds(pl.multiple_of(c * L, L), L)
        x_c = x_ref[0, rs, :]
        b_c = b_ref[0, rs, :]
        cb = _nt(c_ref[0, rs, :], b_c)
        ex = _dot(_split_hi_lo(jnp.concatenate([wdt_ref[rs, :], dec_ref[c]], axis=0)), e2)
        decx_ref[c] = ex[L:L + 8]
        xf = x_c.astype(F32)
        xw = jnp.concatenate([xf * ex[0:L, 0:W], xf * ex[0:L, W:2 * W]], axis=1).astype(BF16)
        b_t = b_c.astype(F32).T.astype(BF16)
        xs_ref[c] = _dot(b_t, xw)
        cum_c = cumc_ref[rs, :]
        cum_r = cumr_ref[:, rs]
        for hp in range(HG // 2):
            ms = []
            for hh in range(2):
                hf = 2 * hp + hh
                hb = HG + 2 * hp + hh
                decf = jnp.exp(jnp.where(tril, cum_c[:, hf:hf + 1] - cum_r[hf:hf + 1, :], -jnp.inf))
                decb = jnp.exp(jnp.where(triu, cum_c[:, hb:hb + 1] - cum_r[hb:hb + 1, :], -jnp.inf))
                ms.append((cb * (decf + decb)).astype(BF16))
            m2 = jnp.concatenate(ms, axis=1)
            xp = x_c[:, hp * LANE:(hp + 1) * LANE]
            xz = jnp.zeros_like(xp)
            x2 = jnp.concatenate([jnp.where(lane_half, xp, xz), jnp.where(lane_half, xz, xp)], axis=0)
            acc_ref[rs, hp * LANE:(hp + 1) * LANE] = _dot(m2, x2)
        return carry

    lax.fori_loop(0, nc, intra_body, 0, unroll=2)

    st_ref[...] = jnp.zeros_like(st_ref)

    def state_body(i, carry):
        for d, c in enumerate((i, nc - 1 - i)):
            ls = slice(d * W, (d + 1) * W)
            s_prev = st_ref[:, ls]
            sst_ref[c, :, ls] = s_prev.astype(BF16)
            st_ref[:, ls] = s_prev * decx_ref[c, 0:1, ls] + xs_ref[c, :, ls]
        return carry

    lax.fori_loop(0, nc, state_body, 0)

    def out_body(c, carry):
        rs = pl.ds(pl.multiple_of(c * L, L), L)
        yo = _dot(c_ref[0, rs, :], sst_ref[c])
        sc = _dot(_split_hi_lo(jnp.exp(cumc_ref[rs, :])), e2)
        xf = x_ref[0, rs, :].astype(F32)
        y = acc_ref[rs, :] + yo[:, 0:W] * sc[:, 0:W] + yo[:, W:2 * W] * sc[:, W:2 * W] + xf * drow_ref[...]
        y = y * _silu(z_ref[0, rs, :].astype(F32))
        ms = jnp.mean(y * y, axis=-1, keepdims=True)
        o_ref[0, rs, :] = (y * lax.rsqrt(ms + EPS) * nw_ref[...]).astype(BF16)
        return carry

    lax.fori_loop(0, nc, out_body, 0, unroll=2)


def _ssd(xbc, u3, us3, ust, prow, pcol, drow, nw):
    B, S, _ = xbc.shape
    W = SSD_GROUP_W
    N = SSD_STATE
    nc = S // SSD_CHUNK
    return pl.pallas_call(
        _ssd_kernel,
        grid=(B, SSD_GROUPS),
        in_specs=[
            pl.BlockSpec((1, S, W), lambda b, g: (b, 0, g)),
            pl.BlockSpec((1, S, N), lambda b, g: (b, 0, SSD_D_INNER // N + g)),
            pl.BlockSpec((1, S, N), lambda b, g: (b, 0, SSD_D_INNER // N + SSD_GROUPS + g)),
            pl.BlockSpec((1, S, W), lambda b, g: (b, 0, U_Z // W + g)),
            pl.BlockSpec((1, S, LANE), lambda b, g: (b, 0, g)),
            pl.BlockSpec((SMALL_T_ROWS, S), lambda b, g: (g, b)),
            pl.BlockSpec((8, LANE), lambda b, g: (g, 0)),
            pl.BlockSpec((1, SMALL_T_ROWS, 8), lambda b, g: (g, 0, 0)),
            pl.BlockSpec((1, W), lambda b, g: (0, g)),
            pl.BlockSpec((1, W), lambda b, g: (0, g)),
        ],
        out_specs=pl.BlockSpec((1, S, W), lambda b, g: (b, 0, g)),
        out_shape=jax.ShapeDtypeStruct((B, S, SSD_D_INNER), BF16),
        scratch_shapes=[
            pltpu.VMEM((S, W), F32),
            pltpu.VMEM((S, LANE), F32),
            pltpu.VMEM((SMALL_T_ROWS, S), F32),
            pltpu.VMEM((S, LANE), F32),
            pltpu.VMEM((nc, 16, LANE), F32),
            pltpu.VMEM((nc, N, 2 * W), F32),
            pltpu.VMEM((nc, 8, 2 * W), F32),
            pltpu.VMEM((nc, N, 2 * W), BF16),
            pltpu.VMEM((N, 2 * W), F32),
        ],
        compiler_params=_cparams(("parallel", "parallel")),
        name="ssd_scan",
    )(xbc, xbc, xbc, u3, us3, ust, prow, pcol, drow, nw)


def _gla_kernel(q_ref, k_ref, v_ref, gg_ref, us_ref, a2_ref, bias_ref, nw_ref,
                o_ref, acc_ref, g_ref, qd_ref, kd_ref, kdp_ref, qcat_ref, x_ref, dec_ref, sst_ref, st_ref):
    L = GLA_CHUNK
    BL = 2 * L
    DK = GLA_DK
    S = q_ref.shape[1]
    nb = S // BL
    scale = DK ** -0.5

    ii = _iota2((BL, BL), 0)
    jj = _iota2((BL, BL), 1)
    same = (ii // L) == (jj // L)
    masks = (same & (ii >= jj), same & (jj >= ii))
    tri2 = masks[0].astype(BF16)
    par_row = _iota2((BL, DK), 0) // L

    ga = us_ref[0].astype(BF16)
    g_ref[...] = _log_sigmoid(_dot(ga, a2_ref[...]) + bias_ref[...]) / GLA_GATE_NORM

    def decay_body(i, carry):
        rs = pl.ds(pl.multiple_of(i * BL, BL), BL)
        g = g_ref[rs, :]
        hi = g.astype(BF16)
        r1 = g - hi.astype(F32)
        mid = r1.astype(BF16)
        lo = (r1 - mid.astype(F32)).astype(BF16)
        pp = _dot(tri2, jnp.concatenate([hi, mid, lo], axis=1))
        p = pp[:, 0:2 * DK] + pp[:, 2 * DK:4 * DK] + pp[:, 4 * DK:6 * DK]
        q_c = q_ref[0, rs, :].astype(F32) * scale
        k_c = k_ref[0, rs, :].astype(F32)
        zero = jnp.zeros((BL, DK), BF16)
        for d in range(2):
            p_d = p[:, d * DK:(d + 1) * DK]
            tot = jnp.where(par_row == 0, p_d[L - 1:L, :], p_d[BL - 1:BL, :])
            b = p_d if d == 0 else tot - p_d + g[:, DK:]
            qd = (q_c * jnp.exp(b)).astype(BF16)
            kdec = (k_c * jnp.exp(tot - b)).astype(BF16)
            qd_ref[d, rs, :] = qd
            kd_ref[d, rs, :] = (k_c * jnp.exp(-b)).astype(BF16)
            for par in range(2):
                sel = par_row == par
                kdp_ref[d, rs, par * DK:(par + 1) * DK] = jnp.where(sel, kdec, zero)
                qcat_ref[rs, (2 * d + par) * DK:(2 * d + par + 1) * DK] = jnp.where(sel, qd, zero)
                last = (par + 1) * L - 1
                dec_ref[d, 2 * i + par] = jnp.broadcast_to(jnp.exp(p_d[last:last + 1, :]), (8, DK))
        return carry

    lax.fori_loop(0, nb, decay_body, 0, unroll=2)

    def intra_body(i, carry):
        blks = (2 * i, 2 * i + 1)
        rss = [pl.ds(pl.multiple_of(b * BL, BL), BL) for b in blks]
        vs = [v_ref[0, rs, :] for rs in rss]
        atts = [[_nt(qd_ref[d, rs, :], kd_ref[d, rs, :]) for d in range(2)] for rs in rss]
        for j, b in enumerate(blks):
            for d in range(2):
                x_ref[d, b] = _tn(vs[j], kdp_ref[d, rss[j], :])
        for j in range(2):
            att = jnp.where(masks[0], atts[j][0], 0.0) + jnp.where(masks[1], atts[j][1], 0.0)
            acc_ref[rss[j], :] = _dot(att.astype(BF16), vs[j])
        return carry

    lax.fori_loop(0, nb // 2, intra_body, 0)

    st_ref[...] = jnp.zeros_like(st_ref)

    def state_body(i, carry):
        for d, b in enumerate((i, nb - 1 - i)):
            for par in ((0, 1) if d == 0 else (1, 0)):
                s_prev = st_ref[d]
                lane0 = (2 * d + par) * DK
                sst_ref[b, :, lane0:lane0 + DK] = s_prev.astype(BF16)
                st_ref[d] = s_prev * dec_ref[d, 2 * b + par, 0:1, :] + x_ref[d, b, :, par * DK:(par + 1) * DK]
        return carry

    lax.fori_loop(0, nb, state_body, 0)

    def out_body(b, carry):
        rs = pl.ds(pl.multiple_of(b * BL, BL), BL)
        o = acc_ref[rs, :] + _nt(qcat_ref[rs, :], sst_ref[b])
        ms = jnp.mean(o * o, axis=-1, keepdims=True)
        o = o * lax.rsqrt(ms + EPS) * nw_ref[...]
        o_ref[0, rs, :] = (o * _silu(gg_ref[0, rs, :].astype(F32))).astype(BF16)
        return carry

    lax.fori_loop(0, nb, out_body, 0, unroll=2)


def _gla(u3, us3, a2, bias, nw):
    B, S, _ = u3.shape
    DK, DV = GLA_DK, GLA_DV
    nc = S // GLA_CHUNK
    nb = nc // 2
    return pl.pallas_call(
        _gla_kernel,
        grid=(B, GLA_HEADS),
        in_specs=[
            pl.BlockSpec((1, S, DK), lambda b, h: (b, 0, U_GQ // DK + h)),
            pl.BlockSpec((1, S, DK), lambda b, h: (b, 0, U_GK // DK + h)),
            pl.BlockSpec((1, S, DV), lambda b, h: (b, 0, U_GV // DV + h)),
            pl.BlockSpec((1, S, DV), lambda b, h: (b, 0, U_GG // DV + h)),
            pl.BlockSpec((1, S, LANE), lambda b, h: (b, 0, 0)),
            pl.BlockSpec((LANE, 2 * DK), lambda b, h: (0, h)),
            pl.BlockSpec((1, 2 * DK), lambda b, h: (0, h)),
            pl.BlockSpec((1, DV), lambda b, h: (0, 0)),
        ],
        out_specs=pl.BlockSpec((1, S, DV), lambda b, h: (b, 0, h)),
        out_shape=jax.ShapeDtypeStruct((B, S, GLA_VAL_W), BF16),
        scratch_shapes=[
            pltpu.VMEM((S, DV), F32),
            pltpu.VMEM((S, 2 * DK), F32),
            pltpu.VMEM((2, S, DK), BF16),
            pltpu.VMEM((2, S, DK), BF16),
            pltpu.VMEM((2, S, 2 * DK), BF16),
            pltpu.VMEM((S, 4 * DK), BF16),
            pltpu.VMEM((2, nb, DV, 2 * DK), F32),
            pltpu.VMEM((2, nc, 8, DK), F32),
            pltpu.VMEM((nb, DV, 4 * DK), BF16),
            pltpu.VMEM((2, DV, DK), F32),
        ],
        compiler_params=_cparams(("parallel", "parallel")),
        name="gla_scan",
    )(u3, u3, u3, u3, us3, a2, bias, nw)


def _na_col_tables():
    qcol = np.arange(GRID_W)[:, None]
    kcol = np.arange(GRID_W)[None, :]
    w_start = np.clip(qcol - NA_WIN_W // 2, 0, GRID_W - NA_WIN_W)
    mask = (kcol >= w_start) & (kcol < w_start + NA_WIN_W)
    off = np.clip(kcol - qcol, -(NA_WIN_W - 1), NA_WIN_W - 1) + (NA_WIN_W - 1)
    return mask, off


def _na_bias_table(rpb, rows):
    win_h = min(NA_WIN_H, rows)
    mask, off = _na_col_tables()
    H = rpb.shape[0]
    padw = GRID_W - NA_WIN_W
    ext = jnp.pad(rpb, ((0, 0), (0, 0), (padw, padw)), mode="edge")
    a = jnp.stack([ext[:, :, GRID_W - 1 - q:2 * GRID_W - 1 - q] for q in range(GRID_W)], axis=2)
    a = jnp.where(mask[None, None], a, -jnp.inf)
    a = a.reshape(H // 2, 2, 2 * NA_WIN_H - 1, GRID_W, GRID_W)
    per_delta = []
    for delta in range(win_h):
        lo = NA_WIN_H - 1 - delta
        per_delta.append(jnp.concatenate([a[:, :, lo + w] for w in range(win_h)], axis=-1))
    t = jnp.stack(per_delta, axis=1)
    return t.reshape(H // 2, win_h, 2 * GRID_W, win_h * GRID_W).astype(F32)


def _na_kernel(flag_ref, q_ref, k_ref, v_ref, tab_ref, qw_ref, kw_ref, o_ref,
               q0_ref, q1_ref, kn_ref, vx_ref):
    S = q_ref.shape[1]
    rows = S // GRID_W
    win_h = tab_ref.shape[1]
    nk = win_h * GRID_W
    scale = NA_HEAD_DIM ** -0.5 * LOG2E
    RB = 128

    er = _iota2((LANE, LANE), 0) // NA_HEAD_DIM
    ec = _iota2((LANE, LANE), 1) // NA_HEAD_DIM
    e_blk = (er == ec).astype(BF16)
    e2 = jnp.concatenate([e_blk, e_blk], axis=0)
    first = _iota2((RB, LANE), 1) < NA_HEAD_DIM

    def norm_body(i, carry):
        rs = pl.ds(pl.multiple_of(i * RB, RB), RB)
        q = q_ref[0, rs, :].astype(F32)
        k = k_ref[0, rs, :].astype(F32)
        ms = _dot(_split_hi_lo(jnp.concatenate([q * q, k * k], axis=0)), e2) * (1.0 / NA_HEAD_DIM)
        qn = (q * lax.rsqrt(ms[0:RB] + EPS) * (qw_ref[...] * scale)).astype(BF16)
        zero = jnp.zeros_like(qn)
        q0_ref[rs, :] = jnp.where(first, qn, zero)
        q1_ref[rs, :] = jnp.where(first, zero, qn)
        kn_ref[rs, :] = (k * lax.rsqrt(ms[RB:2 * RB] + EPS) * kw_ref[...]).astype(BF16)
        vx_ref[rs, 0:LANE] = v_ref[0, rs, :]
        vx_ref[rs, LANE:2 * LANE] = jnp.ones((RB, LANE), BF16)
        return carry

    lax.fori_loop(0, S // RB, norm_body, 0, unroll=2)
    bound_ok = flag_ref[0] != 0

    first_q = _iota2((GRID_W, LANE), 1) < NA_HEAD_DIM

    def key_start(r):
        r0 = jnp.clip(r - win_h // 2, 0, rows - win_h)
        return r0, pl.multiple_of(r0 * GRID_W, GRID_W)

    def scores(r):
        r0, k0 = key_start(r)
        q0 = pl.multiple_of(r * GRID_W, GRID_W)
        qs = jnp.concatenate([q0_ref[pl.ds(q0, GRID_W), :], q1_ref[pl.ds(q0, GRID_W), :]], axis=0)
        return _nt(qs, kn_ref[pl.ds(k0, nk), :]) + tab_ref[0, r - r0]

    def probs_exact(s):
        return jnp.exp2(s - jnp.max(s, axis=-1, keepdims=True)).astype(BF16)

    def probs_bounded(r):
        return jnp.exp2(scores(r)).astype(BF16)

    def attend(r, p):
        _, k0 = key_start(r)
        ox = _dot(p, vx_ref[pl.ds(k0, nk), :])
        o = ox[:, 0:LANE] / ox[:, LANE:2 * LANE]
        q0 = pl.multiple_of(r * GRID_W, GRID_W)
        o_ref[0, pl.ds(q0, GRID_W), :] = jnp.where(first_q, o[:GRID_W], o[GRID_W:]).astype(BF16)

    @pl.when(bound_ok)
    def _():
        U = NA_ROWS_PER_STEP

        def row_body(i, p_prev):
            r = i * U
            for j in range(U):
                attend(r - U + j, p_prev[j])
            return tuple(probs_bounded(r + j) for j in range(U))

        p_last = lax.fori_loop(1, rows // U, row_body, tuple(probs_bounded(j) for j in range(U)))
        for j in range(U):
            attend(rows - U + j, p_last[j])

    @pl.when(jnp.logical_not(bound_ok))
    def _():
        U = NA_ROWS_PER_STEP_EXACT

        def row_body(i, carry):
            s_cur, p_prev = carry
            r = i * U
            for j in range(U):
                attend(r - U + j, p_prev[j])
            p = tuple(probs_exact(s) for s in s_cur)
            s_next = tuple(scores(jnp.minimum(r + U + j, rows - 1)) for j in range(U))
            return s_next, p

        p0 = tuple(probs_exact(scores(j)) for j in range(U))
        s1 = tuple(scores(U + j) for j in range(U))
        _, p_last = lax.fori_loop(1, rows // U, row_body, (s1, p0))
        for j in range(U):
            attend(rows - U + j, p_last[j])


def _na_score_bound(rpb, q_norm_w, k_norm_w):
    H = rpb.shape[0]
    qk = (NA_HEAD_DIM ** 0.5 * LOG2E * (1.0 + NA_BOUND_SLACK)) * jnp.max(jnp.abs(q_norm_w)) * jnp.max(jnp.abs(k_norm_w))
    b_max = jnp.max(rpb.reshape(H, -1), axis=1) * LOG2E
    b_self = rpb[:, NA_WIN_H - 1, NA_WIN_W - 1] * LOG2E
    bound = qk + b_max
    flag = jnp.all(bound - (b_self - qk) <= NA_MAX_BOUND_GAP)
    return bound, flag.astype(jnp.int32).reshape(1)


def _na(u3, table, flag, qw, kw):
    B, S, _ = u3.shape
    win_h = table.shape[1]
    return pl.pallas_call(
        _na_kernel,
        grid=(NA_HEADS // 2, B),
        in_specs=[
            pl.BlockSpec(memory_space=pltpu.SMEM),
            pl.BlockSpec((1, S, LANE), lambda h, b: (b, 0, U_NQ // LANE + h)),
            pl.BlockSpec((1, S, LANE), lambda h, b: (b, 0, U_NK // LANE + h)),
            pl.BlockSpec((1, S, LANE), lambda h, b: (b, 0, U_NV // LANE + h)),
            pl.BlockSpec((1, win_h, 2 * GRID_W, win_h * GRID_W), lambda h, b: (h, 0, 0, 0)),
            pl.BlockSpec((1, LANE), lambda h, b: (0, 0)),
            pl.BlockSpec((1, LANE), lambda h, b: (0, 0)),
        ],
        out_specs=pl.BlockSpec((1, S, LANE), lambda h, b: (b, 0, h)),
        out_shape=jax.ShapeDtypeStruct((B, S, NA_W), BF16),
        scratch_shapes=[
            pltpu.VMEM((S, LANE), BF16),
            pltpu.VMEM((S, LANE), BF16),
            pltpu.VMEM((S, LANE), BF16),
            pltpu.VMEM((S, 2 * LANE), BF16),
        ],
        compiler_params=_cparams(("parallel", "parallel")),
        name="na_attn",
    )(flag, u3, u3, u3, table, qw, kw)


def _merge_kernel(x_ref, ys_ref, yg_ref, yn_ref, gate_ref, ws_ref, wg_ref, wn_ref, wo_ref, o_ref):
    D = D_MODEL
    mixed = _sigmoid(gate_ref[:, 0:D].astype(F32)) * _dot(ys_ref[...], ws_ref[...])
    mixed += _sigmoid(gate_ref[:, D:2 * D].astype(F32)) * _dot(yg_ref[...], wg_ref[...])
    mixed += _sigmoid(gate_ref[:, 2 * D:3 * D].astype(F32)) * _dot(yn_ref[...], wn_ref[...])
    o_ref[...] = x_ref[...] + _dot(mixed.astype(BF16), wo_ref[...])


def _merge(x2, ys, yg, yn, u2, ws, wg, wn, wo, tm=512):
    T = x2.shape[0]
    D = D_MODEL
    row = lambda i: (i, 0)
    fixed = lambda i: (0, 0)
    return pl.pallas_call(
        _merge_kernel,
        grid=(T // tm,),
        in_specs=[
            pl.BlockSpec((tm, D), row),
            pl.BlockSpec((tm, D), row),
            pl.BlockSpec((tm, D), row),
            pl.BlockSpec((tm, D), row),
            pl.BlockSpec((tm, N_BRANCH * D), row),
            pl.BlockSpec((D, D), fixed),
            pl.BlockSpec((D, D), fixed),
            pl.BlockSpec((D, D), fixed),
            pl.BlockSpec((D, D), fixed),
        ],
        out_specs=pl.BlockSpec((tm, D), row),
        out_shape=jax.ShapeDtypeStruct((T, D), F32),
        compiler_params=_cparams(("parallel",)),
        name="merge",
    )(x2, ys, yg, yn, u2, ws, wg, wn, wo)


def _mlp_kernel(x_ref, nw_ref, w1_ref, w2_ref, o_ref, *, tf):
    x = x_ref[...]
    ms = jnp.mean(x * x, axis=-1, keepdims=True)
    h = (x * lax.rsqrt(ms + EPS) * nw_ref[...]).astype(BF16)
    acc = x
    for f in range(D_FF // tf):
        a = jnp.maximum(_dot(h, w1_ref[:, f * tf:(f + 1) * tf]), 0.0)
        acc = acc + _dot((a * a).astype(BF16), w2_ref[f * tf:(f + 1) * tf, :])
    o_ref[...] = acc


def _mlp(x2, nw, w1, w2, tm=512, tf=1024):
    T = x2.shape[0]
    D = D_MODEL
    return pl.pallas_call(
        functools.partial(_mlp_kernel, tf=tf),
        grid=(T // tm,),
        in_specs=[
            pl.BlockSpec((tm, D), lambda i: (i, 0)),
            pl.BlockSpec((1, D), lambda i: (0, 0)),
            pl.BlockSpec((D, D_FF), lambda i: (0, 0)),
            pl.BlockSpec((D_FF, D), lambda i: (0, 0)),
        ],
        out_specs=pl.BlockSpec((tm, D), lambda i: (i, 0)),
        out_shape=jax.ShapeDtypeStruct((T, D), F32),
        compiler_params=_cparams(("parallel",)),
        name="mlp",
    )(x2, nw, w1, w2)


def _pad_rows(w, start, total):
    return jnp.zeros((total, w.shape[1]), w.dtype).at[start:start + w.shape[0]].set(w)


def kernel(x, norm_mix_w, w_in, ssd_conv_w, ssd_conv_b, ssd_dt_bias_f, ssd_dt_bias_b, ssd_a_log_f,
           ssd_a_log_b, ssd_d, ssd_norm_w, gla_a2_f, gla_a2_bias_f, gla_a2_b, gla_a2_bias_b,
           gla_norm_w, na_q_norm_w, na_k_norm_w, na_rpb, w_branch_ssd, w_branch_gla, w_branch_na,
           w_out, norm_mlp_w, w_ff1, w_ff2):
    B, S, D = x.shape
    T = B * S
    depth = w_in.shape[0]
    rows = S // GRID_W
    x2 = x.reshape(T, D)
    for l in range(depth):
        w_big = jnp.concatenate([w_in[l][:, a:a + n] for a, n in _BIG_SEGS], axis=1).astype(BF16)
        w_small = _small_weight(w_in[l]).astype(BF16)
        hg = (SSD_GROUPS, SSD_HG)
        zeros_r = jnp.zeros((SSD_GROUPS, LANE - 2 * SSD_HG), F32)
        dt_bias = jnp.concatenate([ssd_dt_bias_f[l].reshape(hg), ssd_dt_bias_b[l].reshape(hg), zeros_r], axis=1)
        a_neg = jnp.concatenate([-jnp.exp(ssd_a_log_f[l]).reshape(hg), -jnp.exp(ssd_a_log_b[l]).reshape(hg),
                                 zeros_r], axis=1)
        prow3 = jnp.zeros((SSD_GROUPS, 8, LANE), F32).at[:, 0].set(dt_bias).at[:, 1].set(a_neg)
        prow = prow3.reshape(SSD_GROUPS * 8, LANE)
        pcol = jnp.transpose(prow3[:, :, :SMALL_T_ROWS], (0, 2, 1))
        w_small_t = w_small.T.reshape(SSD_GROUPS, LANE, D)[:, :SMALL_T_ROWS].reshape(-1, D)
        drow = jnp.repeat(ssd_d[l], SSD_HEAD_DIM)[None, :]
        hk = (GLA_HEADS, 1, GLA_DK)
        a2 = jnp.concatenate([_pad_rows(gla_a2_f[l], SM_GAF, LANE).reshape((LANE,) + hk),
                              _pad_rows(gla_a2_b[l], SM_GAB, LANE).reshape((LANE,) + hk)],
                             axis=2).reshape(LANE, -1).astype(BF16)
        a2_bias = jnp.concatenate([gla_a2_bias_f[l].reshape(hk), gla_a2_bias_b[l].reshape(hk)],
                                  axis=1).reshape(1, -1)
        na_bound, na_flag = _na_score_bound(na_rpb[l], na_q_norm_w[l], na_k_norm_w[l])
        table = _na_bias_table(na_rpb[l] * LOG2E - na_bound[:, None, None], rows)
        qw = jnp.tile(na_q_norm_w[l], 2)[None, :]
        kw = jnp.tile(na_k_norm_w[l], 2)[None, :]

        u2, us2, ust = _inproj(x2, norm_mix_w[l][None, :], w_big, w_small, w_small_t)
        u3 = u2.reshape(B, S, U_WIDTH)
        us3 = us2.reshape(B, S, SMALL_W)
        xbc = _conv(u3, ssd_conv_w[l], ssd_conv_b[l][None, :])
        y_ssd = _ssd(xbc, u3, us3, ust, prow, pcol, drow, ssd_norm_w[l][None, :])
        y_gla = _gla(u3, us3, a2, a2_bias, gla_norm_w[l][None, :])
        y_na = _na(u3, table, na_flag, qw, kw)
        x2 = _merge(x2, y_ssd.reshape(T, -1), y_gla.reshape(T, -1), y_na.reshape(T, -1), u2,
                    w_branch_ssd[l].astype(BF16), w_branch_gla[l].astype(BF16),
                    w_branch_na[l].astype(BF16), w_out[l].astype(BF16))
        x2 = _mlp(x2, norm_mlp_w[l][None, :], w_ff1[l].astype(BF16), w_ff2[l].astype(BF16))
    return x2.reshape(B, S, D)
```

```python
import functools

import jax
import jax.numpy as jnp
import numpy as np
from jax import lax
from jax.experimental import pallas as pl
from jax.experimental.pallas import tpu as pltpu

F32 = jnp.float32
BF16 = jnp.bfloat16

EPS = 1e-6
D_MODEL = 1024
GRID_W = 64

SSD_HEADS = 16
SSD_HEAD_DIM = 64
SSD_D_INNER = 1024
SSD_GROUPS = 2
SSD_STATE = 128
SSD_CONV = 5
SSD_CONV_DIM = 1536
SSD_CHUNK = 128
SSD_GROUP_W = SSD_D_INNER // SSD_GROUPS
SSD_HG = SSD_HEADS // SSD_GROUPS

GLA_HEADS = 4
GLA_DK = 128
GLA_DV = 256
GLA_KEY_W = 512
GLA_VAL_W = 1024
GLA_GATE_RANK = 16
GLA_GATE_NORM = 16.0
GLA_CHUNK = 64

NA_HEADS = 16
NA_HEAD_DIM = 64
NA_W = 1024
NA_WIN_H = 8
NA_WIN_W = 16
NA_ROWS_PER_STEP = 8
NA_ROWS_PER_STEP_EXACT = 2
LOG2E = 1.4426950408889634
LN2 = 0.6931471805599453
NA_BOUND_SLACK = 0.02
NA_MAX_BOUND_GAP = 90.0

N_BRANCH = 3
D_FF = 4096

IN_SIZES = (SSD_D_INNER, SSD_CONV_DIM, SSD_HEADS, SSD_HEADS,
            GLA_KEY_W, GLA_KEY_W, GLA_VAL_W, GLA_VAL_W, GLA_GATE_RANK, GLA_GATE_RANK,
            NA_W, NA_W, NA_W, N_BRANCH * D_MODEL)
_IN_OFF = np.concatenate([[0], np.cumsum(IN_SIZES)])
(_O_Z, _O_XBC, _O_DTF, _O_DTB, _O_GQ, _O_GK, _O_GV, _O_GG, _O_GAF, _O_GAB,
 _O_NQ, _O_NK, _O_NV, _O_GATE) = [int(v) for v in _IN_OFF[:-1]]

U_NQ = 0
U_NK = 1024
U_GQ = 2048
U_GK = 2560
U_GATE = 3072
U_NV = 6144
U_Z = 7168
U_GV = 8192
U_GG = 9216
U_XBC = 10240
U_WIDTH = 11776
INPROJ_COL_CHUNK = 1024
LANE = 128
VMEM_LIMIT = 56 * 1024 * 1024

SMALL_W = SSD_GROUPS * LANE
SM_DTF, SM_DTB, SM_GAF, SM_GAB = 0, 8, 16, 32
SMALL_T_ROWS = 2 * SSD_HG

_BIG_SEGS = ((_O_NQ, 2048), (_O_GQ, 1024), (_O_GATE, 3072), (_O_NV, 1024), (_O_Z, 1024), (_O_GV, 2048),
             (_O_XBC, 1536))


def _small_weight(w):
    blocks = []
    for g in range(SSD_GROUPS):
        h0 = g * SSD_HG
        cols = [w[:, _O_DTF + h0:_O_DTF + h0 + SSD_HG], w[:, _O_DTB + h0:_O_DTB + h0 + SSD_HG]]
        used = 2 * SSD_HG
        if g == 0:
            cols += [w[:, _O_GAF:_O_GAF + GLA_GATE_RANK], w[:, _O_GAB:_O_GAB + GLA_GATE_RANK]]
            used += 2 * GLA_GATE_RANK
        cols.append(jnp.zeros((w.shape[0], LANE - used), w.dtype))
        blocks += cols
    return jnp.concatenate(blocks, axis=1)


def _cparams(sem, vmem=VMEM_LIMIT):
    return pltpu.CompilerParams(dimension_semantics=sem, vmem_limit_bytes=vmem)


def _sigmoid(x):
    return 1.0 / (1.0 + jnp.exp(-x))


def _silu(x):
    return x * _sigmoid(x)


def _softplus(x):
    return jnp.maximum(x, 0.0) + jnp.log1p(jnp.exp(-jnp.abs(x)))


def _log_sigmoid(x):
    t = x * (-LOG2E)
    return (-LN2) * (jnp.maximum(t, 0.0) + jnp.log2(1.0 + jnp.exp2(-jnp.abs(t))))


def _nt(a, b):
    return lax.dot_general(a, b, (((1,), (1,)), ((), ())), preferred_element_type=F32)


def _tn(a, b):
    return lax.dot_general(a, b, (((0,), (0,)), ((), ())), preferred_element_type=F32)


def _dot(a, b):
    return jnp.dot(a, b, preferred_element_type=F32)


def _dot_exact(a, b):
    return jnp.dot(a, b, preferred_element_type=F32, precision=lax.Precision.HIGHEST)


def _iota2(shape, dim):
    return lax.broadcasted_iota(jnp.int32, shape, dim)


def _head_rms(r, w_row):
    er = _iota2((LANE, LANE), 0) // NA_HEAD_DIM
    ec = _iota2((LANE, LANE), 1) // NA_HEAD_DIM
    e_blk = (er == ec).astype(BF16)
    outs = []
    for a in range(0, r.shape[1], LANE):
        x = r[:, a:a + LANE]
        ms = _dot((x * x).astype(BF16), e_blk) * (1.0 / NA_HEAD_DIM)
        outs.append(x * lax.rsqrt(ms + EPS))
    return jnp.concatenate(outs, axis=1) * w_row


def _inproj_kernel(x_ref, nw_ref, w_ref, ws_ref, wst_ref, qkw_ref, u_ref, us_ref, ust_ref, h_ref):
    tn = u_ref.shape[1]
    chunks = [(a, min(a + INPROJ_COL_CHUNK, tn)) for a in range(0, tn, INPROJ_COL_CHUNK)]

    @pl.when(pl.program_id(1) == 0)
    def _():
        x = x_ref[...]
        ms = jnp.mean(x * x, axis=-1, keepdims=True)
        h = (x * lax.rsqrt(ms + EPS) * nw_ref[...]).astype(BF16)
        h_ref[...] = h
        us_ref[...] = _dot(h, ws_ref[...])
        ust_ref[...] = _nt(wst_ref[...], h)
        for c, (a, b) in enumerate(chunks):
            r = _dot(h, w_ref[:, a:b])
            if c < 2:
                r = _head_rms(r, qkw_ref[c:c + 1, :])
            u_ref[:, a:b] = r.astype(BF16)

    @pl.when(pl.program_id(1) != 0)
    def _():
        for a, b in chunks:
            u_ref[:, a:b] = _dot(h_ref[...], w_ref[:, a:b]).astype(BF16)


def _inproj(x2, nw, w_big, w_small, w_small_t, qkw, tm=1024, tn=U_WIDTH // 4):
    assert (U_NQ, U_NK) == (0, INPROJ_COL_CHUNK) and NA_W == INPROJ_COL_CHUNK
    T = x2.shape[0]
    return pl.pallas_call(
        _inproj_kernel,
        grid=(T // tm, U_WIDTH // tn),
        in_specs=[
            pl.BlockSpec((tm, D_MODEL), lambda i, j: (i, 0)),
            pl.BlockSpec((1, D_MODEL), lambda i, j: (0, 0)),
            pl.BlockSpec((D_MODEL, tn), lambda i, j: (0, j)),
            pl.BlockSpec((D_MODEL, SMALL_W), lambda i, j: (0, 0)),
            pl.BlockSpec((SSD_GROUPS * SMALL_T_ROWS, D_MODEL), lambda i, j: (0, 0)),
            pl.BlockSpec((8, NA_W), lambda i, j: (0, 0)),
        ],
        out_specs=[
            pl.BlockSpec((tm, tn), lambda i, j: (i, j)),
            pl.BlockSpec((tm, SMALL_W), lambda i, j: (i, 0)),
            pl.BlockSpec((SSD_GROUPS * SMALL_T_ROWS, tm), lambda i, j: (0, i)),
        ],
        out_shape=[
            jax.ShapeDtypeStruct((T, U_WIDTH), BF16),
            jax.ShapeDtypeStruct((T, SMALL_W), F32),
            jax.ShapeDtypeStruct((SSD_GROUPS * SMALL_T_ROWS, T), F32),
        ],
        scratch_shapes=[pltpu.VMEM((tm, D_MODEL), BF16)],
        compiler_params=_cparams(("parallel", "arbitrary")),
        name="inproj",
    )(x2, nw, w_big, w_small, w_small_t, qkw)


def _conv_kernel(u_ref, w_ref, b_ref, o_ref):
    x = u_ref[0].astype(F32)
    S = x.shape[0]
    row = _iota2(x.shape, 0)
    acc = jnp.zeros_like(x) + b_ref[...]
    pad = SSD_CONV // 2
    for k in range(SSD_CONV):
        d = k - pad
        if d == 0:
            xs = x
        else:
            xs = pltpu.roll(x, (-d) % S, axis=0)
            valid = (row + d >= 0) & (row + d < S)
            xs = jnp.where(valid, xs, 0.0)
        acc = acc + w_ref[k:k + 1, :] * xs
    o_ref[0] = _silu(acc).astype(BF16)


def _conv(u3, conv_w, conv_b, tc=256):
    B, S, _ = u3.shape
    nblk = SSD_CONV_DIM // tc
    off = U_XBC // tc
    return pl.pallas_call(
        _conv_kernel,
        grid=(B, nblk),
        in_specs=[
            pl.BlockSpec((1, S, tc), lambda b, c: (b, 0, off + c)),
            pl.BlockSpec((SSD_CONV, tc), lambda b, c: (0, c)),
            pl.BlockSpec((1, tc), lambda b, c: (0, c)),
        ],
        out_specs=pl.BlockSpec((1, S, tc), lambda b, c: (b, 0, c)),
        out_shape=jax.ShapeDtypeStruct((B, S, SSD_CONV_DIM), BF16),
        compiler_params=_cparams(("parallel", "parallel")),
        name="ssd_conv",
    )(u3, conv_w, conv_b)


def _split_hi_lo(v):
    hi = v.astype(BF16)
    lo = (v - hi.astype(F32)).astype(BF16)
    return jnp.concatenate([hi, lo], axis=1)


def _split3(v, axis):
    hi = v.astype(BF16)
    r1 = v - hi.astype(F32)
    mid = r1.astype(BF16)
    lo = (r1 - mid.astype(F32)).astype(BF16)
    return jnp.concatenate([hi, mid, lo], axis=axis)


def _ssd_kernel(x_ref, b_ref, c_ref, z_ref, us_ref, ust_ref, prow_ref, pcol_ref, drow_ref, nw_ref,
                o_ref, acc_ref, cumc_ref, cumr_ref, wdt_ref, dec_ref, xs_ref, decx_ref, sst_ref, st_ref):
    L = SSD_CHUNK
    S = x_ref.shape[1]
    nc = S // L
    W = SSD_GROUP_W
    HG = SSD_HG
    R = 2 * HG

    ii = _iota2((L, L), 0)
    jj = _iota2((L, L), 1)
    tril = ii >= jj
    triu = jj >= ii
    tril_b = tril.astype(BF16)
    triu_b = triu.astype(BF16)

    er = _iota2((LANE, 2 * W), 0)
    ec = _iota2((LANE, 2 * W), 1)
    e = (er == jnp.where(ec < W, ec // SSD_HEAD_DIM, (ec - W) // SSD_HEAD_DIM + HG)).astype(BF16)
    e2 = jnp.concatenate([e, e], axis=0)

    bias_row = prow_ref[0:1, :]
    a_row = prow_ref[1:2, :]
    bias_col = pcol_ref[0, :, 0:1]
    a_col = pcol_ref[0, :, 1:2]

    lane_fwd = _iota2((L, LANE), 1) < HG
    row_fwd = _iota2((R, L), 0) < HG
    lane_half = _iota2((L, LANE), 1) < SSD_HEAD_DIM

    def decay_body(c, carry):
        rs = pl.ds(pl.multiple_of(c * L, L), L)
        dt_c = _softplus(us_ref[0, rs, :] + bias_row)
        a_c = dt_c * a_row
        pp = _dot(tril_b, _split3(a_c, 1))
        p_c = pp[:, 0:LANE] + pp[:, LANE:2 * LANE] + pp[:, 2 * LANE:3 * LANE]
        tot_c = p_c[L - 1:L, :]
        cum_c = jnp.where(lane_fwd, p_c, tot_c - p_c + a_c)
        cumc_ref[rs, :] = cum_c * LOG2E
        wdt_ref[rs, :] = jnp.exp(tot_c - cum_c) * dt_c
        dec_ref[c] = jnp.broadcast_to(jnp.exp(tot_c), (16, LANE))
        dt_r = _softplus(ust_ref[:, rs] + bias_col)
        a_r = dt_r * a_col
        pr = _dot(_split3(a_r, 0), triu_b)
        p_r = pr[0:R] + pr[R:2 * R] + pr[2 * R:3 * R]
        tot_r = p_r[:, L - 1:L]
        cum_r = jnp.where(row_fwd, p_r, tot_r - p_r + a_r)
        cumr_ref[:, rs] = (cum_r - jnp.log(dt_r)) * LOG2E
        return carry

    lax.fori_loop(0, nc, decay_body, 0, unroll=2)

    def intra_body(c, carry):
        rs = pl.ds(pl.multiple_of(c * L, L), L)
        x_c = x_ref[0, rs, :]
        b_c = b_ref[0, rs, :]
        cb = _nt(c_ref[0, rs, :], b_c)
        ex = _dot(_split_hi_lo(jnp.concatenate([wdt_ref[rs, :], dec_ref[c]], axis=0)), e2)
        decx_ref[c] = ex[L:L + 8]
        xf = x_c.astype(F32)
        xw = jnp.concatenate([xf * ex[0:L, 0:W], xf * ex[0:L, W:2 * W]], axis=1).astype(BF16)
        b_t = b_c.astype(F32).T.astype(BF16)
        xs_ref[c] = _dot(b_t, xw)
        cum_c = cumc_ref[rs, :]
        cum_r = cumr_ref[:, rs]
        for hp in range(HG // 2):
            ms = []
            for hh in range(2):
                hf = 2 * hp + hh
                hb = HG + 2 * hp + hh
                decf = jnp.exp2(jnp.where(tril, cum_c[:, hf:hf + 1] - cum_r[hf:hf + 1, :], -jnp.inf))
                decb = jnp.exp2(jnp.where(triu, cum_c[:, hb:hb + 1] - cum_r[hb:hb + 1, :], -jnp.inf))
                ms.append((cb * (decf + decb)).astype(BF16))
            m2 = jnp.concatenate(ms, axis=1)
            xp = x_c[:, hp * LANE:(hp + 1) * LANE]
            xz = jnp.zeros_like(xp)
            x2 = jnp.concatenate([jnp.where(lane_half, xp, xz), jnp.where(lane_half, xz, xp)], axis=0)
            acc_ref[rs, hp * LANE:(hp + 1) * LANE] = _dot(m2, x2)
        return carry

    lax.fori_loop(0, nc, intra_body, 0, unroll=2)

    st_ref[...] = jnp.zeros_like(st_ref)

    def state_body(i, carry):
        for d, c in enumerate((i, nc - 1 - i)):
            ls = slice(d * W, (d + 1) * W)
            s_prev = st_ref[:, ls]
            sst_ref[c, :, ls] = s_prev.astype(BF16)
            st_ref[:, ls] = s_prev * decx_ref[c, 0:1, ls] + xs_ref[c, :, ls]
        return carry

    lax.fori_loop(0, nc, state_body, 0)

    def out_body(c, carry):
        rs = pl.ds(pl.multiple_of(c * L, L), L)
        yo = _dot(c_ref[0, rs, :], sst_ref[c])
        sc = _dot(_split_hi_lo(jnp.exp2(cumc_ref[rs, :])), e2)
        xf = x_ref[0, rs, :].astype(F32)
        y = acc_ref[rs, :] + yo[:, 0:W] * sc[:, 0:W] + yo[:, W:2 * W] * sc[:, W:2 * W] + xf * drow_ref[...]
        y = y * _silu(z_ref[0, rs, :].astype(F32))
        ms = jnp.mean(y * y, axis=-1, keepdims=True)
        o_ref[0, rs, :] = (y * lax.rsqrt(ms + EPS) * nw_ref[...]).astype(BF16)
        return carry

    lax.fori_loop(0, nc, out_body, 0, unroll=2)


def _ssd(xbc, u3, us3, ust, prow, pcol, drow, nw):
    B, S, _ = xbc.shape
    W = SSD_GROUP_W
    N = SSD_STATE
    nc = S // SSD_CHUNK
    return pl.pallas_call(
        _ssd_kernel,
        grid=(B, SSD_GROUPS),
        in_specs=[
            pl.BlockSpec((1, S, W), lambda b, g: (b, 0, g)),
            pl.BlockSpec((1, S, N), lambda b, g: (b, 0, SSD_D_INNER // N + g)),
            pl.BlockSpec((1, S, N), lambda b, g: (b, 0, SSD_D_INNER // N + SSD_GROUPS + g)),
            pl.BlockSpec((1, S, W), lambda b, g: (b, 0, U_Z // W + g)),
            pl.BlockSpec((1, S, LANE), lambda b, g: (b, 0, g)),
            pl.BlockSpec((SMALL_T_ROWS, S), lambda b, g: (g, b)),
            pl.BlockSpec((8, LANE), lambda b, g: (g, 0)),
            pl.BlockSpec((1, SMALL_T_ROWS, 8), lambda b, g: (g, 0, 0)),
            pl.BlockSpec((1, W), lambda b, g: (0, g)),
            pl.BlockSpec((1, W), lambda b, g: (0, g)),
        ],
        out_specs=pl.BlockSpec((1, S, W), lambda b, g: (b, 0, g)),
        out_shape=jax.ShapeDtypeStruct((B, S, SSD_D_INNER), BF16),
        scratch_shapes=[
            pltpu.VMEM((S, W), F32),
            pltpu.VMEM((S, LANE), F32),
            pltpu.VMEM((SMALL_T_ROWS, S), F32),
            pltpu.VMEM((S, LANE), F32),
            pltpu.VMEM((nc, 16, LANE), F32),
            pltpu.VMEM((nc, N, 2 * W), F32),
            pltpu.VMEM((nc, 8, 2 * W), F32),
            pltpu.VMEM((nc, N, 2 * W), BF16),
            pltpu.VMEM((N, 2 * W), F32),
        ],
        compiler_params=_cparams(("parallel", "parallel")),
        name="ssd_scan",
    )(xbc, xbc, xbc, u3, us3, ust, prow, pcol, drow, nw)


def _gla_kernel(q_ref, k_ref, v_ref, gg_ref, us_ref, a2_ref, bias_ref, nw_ref,
                o_ref, acc_ref, g_ref, qd_ref, kd_ref, kdp_ref, qcat_ref, x_ref, dec_ref, sst_ref, st_ref):
    L = GLA_CHUNK
    BL = 2 * L
    DK = GLA_DK
    S = q_ref.shape[1]
    nb = S // BL
    scale = DK ** -0.5

    ii = _iota2((BL, BL), 0)
    jj = _iota2((BL, BL), 1)
    same = (ii // L) == (jj // L)
    masks = (same & (ii >= jj), same & (jj >= ii))
    tri2 = masks[0].astype(BF16)
    par_row = _iota2((BL, DK), 0) // L

    ga = us_ref[0].astype(BF16)
    g_ref[...] = _log_sigmoid(_dot(ga, a2_ref[...]) + bias_ref[...]) / GLA_GATE_NORM

    def decay_body(i, carry):
        rs = pl.ds(pl.multiple_of(i * BL, BL), BL)
        g = g_ref[rs, :]
        hi = g.astype(BF16)
        r1 = g - hi.astype(F32)
        mid = r1.astype(BF16)
        lo = (r1 - mid.astype(F32)).astype(BF16)
        pp = _dot(tri2, jnp.concatenate([hi, mid, lo], axis=1))
        p = pp[:, 0:2 * DK] + pp[:, 2 * DK:4 * DK] + pp[:, 4 * DK:6 * DK]
        q_c = q_ref[0, rs, :].astype(F32) * scale
        k_c = k_ref[0, rs, :].astype(F32)
        zero = jnp.zeros((BL, DK), BF16)
        for d in range(2):
            p_d = p[:, d * DK:(d + 1) * DK]
            tot = jnp.where(par_row == 0, p_d[L - 1:L, :], p_d[BL - 1:BL, :])
            b = p_d if d == 0 else tot - p_d + g[:, DK:]
            qd = (q_c * jnp.exp(b)).astype(BF16)
            kdec = (k_c * jnp.exp(tot - b)).astype(BF16)
            qd_ref[d, rs, :] = qd
            kd_ref[d, rs, :] = (k_c * jnp.exp(-b)).astype(BF16)
            for par in range(2):
                sel = par_row == par
                kdp_ref[d, rs, par * DK:(par + 1) * DK] = jnp.where(sel, kdec, zero)
                qcat_ref[rs, (2 * d + par) * DK:(2 * d + par + 1) * DK] = jnp.where(sel, qd, zero)
                last = (par + 1) * L - 1
                dec_ref[d, 2 * i + par] = jnp.broadcast_to(jnp.exp(p_d[last:last + 1, :]), (8, DK))
        return carry

    lax.fori_loop(0, nb, decay_body, 0, unroll=2)

    def intra_body(i, carry):
        blks = (2 * i, 2 * i + 1)
        rss = [pl.ds(pl.multiple_of(b * BL, BL), BL) for b in blks]
        vs = [v_ref[0, rs, :] for rs in rss]
        atts = [[_nt(qd_ref[d, rs, :], kd_ref[d, rs, :]) for d in range(2)] for rs in rss]
        for j, b in enumerate(blks):
            for d in range(2):
                x_ref[d, b] = _tn(vs[j], kdp_ref[d, rss[j], :])
        for j in range(2):
            att = jnp.where(masks[0], atts[j][0], 0.0) + jnp.where(masks[1], atts[j][1], 0.0)
            acc_ref[rss[j], :] = _dot(att.astype(BF16), vs[j])
        return carry

    lax.fori_loop(0, nb // 2, intra_body, 0)

    st_ref[...] = jnp.zeros_like(st_ref)

    def state_body(i, carry):
        for d, b in enumerate((i, nb - 1 - i)):
            for par in ((0, 1) if d == 0 else (1, 0)):
                s_prev = st_ref[d]
                lane0 = (2 * d + par) * DK
                sst_ref[b, :, lane0:lane0 + DK] = s_prev.astype(BF16)
                st_ref[d] = s_prev * dec_ref[d, 2 * b + par, 0:1, :] + x_ref[d, b, :, par * DK:(par + 1) * DK]
        return carry

    lax.fori_loop(0, nb, state_body, 0)

    def out_body(b, carry):
        rs = pl.ds(pl.multiple_of(b * BL, BL), BL)
        o = acc_ref[rs, :] + _nt(qcat_ref[rs, :], sst_ref[b])
        ms = jnp.mean(o * o, axis=-1, keepdims=True)
        o = o * lax.rsqrt(ms + EPS) * nw_ref[...]
        o_ref[0, rs, :] = (o * _silu(gg_ref[0, rs, :].astype(F32))).astype(BF16)
        return carry

    lax.fori_loop(0, nb, out_body, 0, unroll=2)


def _gla(u3, us3, a2, bias, nw):
    B, S, _ = u3.shape
    DK, DV = GLA_DK, GLA_DV
    nc = S // GLA_CHUNK
    nb = nc // 2
    return pl.pallas_call(
        _gla_kernel,
        grid=(B, GLA_HEADS),
        in_specs=[
            pl.BlockSpec((1, S, DK), lambda b, h: (b, 0, U_GQ // DK + h)),
            pl.BlockSpec((1, S, DK), lambda b, h: (b, 0, U_GK // DK + h)),
            pl.BlockSpec((1, S, DV), lambda b, h: (b, 0, U_GV // DV + h)),
            pl.BlockSpec((1, S, DV), lambda b, h: (b, 0, U_GG // DV + h)),
            pl.BlockSpec((1, S, LANE), lambda b, h: (b, 0, 0)),
            pl.BlockSpec((LANE, 2 * DK), lambda b, h: (0, h)),
            pl.BlockSpec((1, 2 * DK), lambda b, h: (0, h)),
            pl.BlockSpec((1, DV), lambda b, h: (0, 0)),
        ],
        out_specs=pl.BlockSpec((1, S, DV), lambda b, h: (b, 0, h)),
        out_shape=jax.ShapeDtypeStruct((B, S, GLA_VAL_W), BF16),
        scratch_shapes=[
            pltpu.VMEM((S, DV), F32),
            pltpu.VMEM((S, 2 * DK), F32),
            pltpu.VMEM((2, S, DK), BF16),
            pltpu.VMEM((2, S, DK), BF16),
            pltpu.VMEM((2, S, 2 * DK), BF16),
            pltpu.VMEM((S, 4 * DK), BF16),
            pltpu.VMEM((2, nb, DV, 2 * DK), F32),
            pltpu.VMEM((2, nc, 8, DK), F32),
            pltpu.VMEM((nb, DV, 4 * DK), BF16),
            pltpu.VMEM((2, DV, DK), F32),
        ],
        compiler_params=_cparams(("parallel", "parallel")),
        name="gla_scan",
    )(u3, u3, u3, u3, us3, a2, bias, nw)


def _na_col_tables():
    qcol = np.arange(GRID_W)[:, None]
    kcol = np.arange(GRID_W)[None, :]
    w_start = np.clip(qcol - NA_WIN_W // 2, 0, GRID_W - NA_WIN_W)
    mask = (kcol >= w_start) & (kcol < w_start + NA_WIN_W)
    off = np.clip(kcol - qcol, -(NA_WIN_W - 1), NA_WIN_W - 1) + (NA_WIN_W - 1)
    return mask, off


def _na_bias_table(rpb, rows):
    win_h = min(NA_WIN_H, rows)
    mask, off = _na_col_tables()
    H = rpb.shape[0]
    padw = GRID_W - NA_WIN_W
    ext = jnp.pad(rpb, ((0, 0), (0, 0), (padw, padw)), mode="edge")
    R = rpb.shape[1]
    n2 = 2 * GRID_W
    tiled = jnp.broadcast_to(jnp.pad(ext, ((0, 0), (0, 0), (0, 1)))[:, :, None, :], (H, R, GRID_W, n2))
    view = tiled.reshape(H, R, GRID_W * n2)[:, :, :GRID_W * (n2 - 1)].reshape(H, R, GRID_W, n2 - 1)
    a = view[:, :, :, GRID_W - 1:]
    a = jnp.where(mask[None, None], a, -jnp.inf)
    a = a.reshape(H // 2, 2, 2 * NA_WIN_H - 1, GRID_W, GRID_W)
    per_delta = []
    for delta in range(win_h):
        lo = NA_WIN_H - 1 - delta
        per_delta.append(jnp.concatenate([a[:, :, lo + w] for w in range(win_h)], axis=-1))
    t = jnp.stack(per_delta, axis=1)
    return t.reshape(H // 2, win_h, 2 * GRID_W, win_h * GRID_W).astype(F32)


def _na_kernel(flag_ref, q_ref, k_ref, v_ref, tab_ref, o_ref, vx_ref):
    S = q_ref.shape[1]
    rows = S // GRID_W
    win_h = tab_ref.shape[1]
    nk = win_h * GRID_W

    vx_ref[:, 0:LANE] = v_ref[0]
    vx_ref[:, LANE:2 * LANE] = jnp.ones((S, LANE), BF16)
    bound_ok = flag_ref[0] != 0

    first_q = _iota2((GRID_W, LANE), 1) < NA_HEAD_DIM

    def key_start(r):
        r0 = jnp.clip(r - win_h // 2, 0, rows - win_h)
        return r0, pl.multiple_of(r0 * GRID_W, GRID_W)

    def scores(r):
        r0, k0 = key_start(r)
        q = q_ref[0, pl.ds(pl.multiple_of(r * GRID_W, GRID_W), GRID_W), :]
        zero = jnp.zeros_like(q)
        qs = jnp.concatenate([jnp.where(first_q, q, zero), jnp.where(first_q, zero, q)], axis=0)
        return _nt(qs, k_ref[0, pl.ds(k0, nk), :]) + tab_ref[0, r - r0]

    def probs_exact(s):
        return jnp.exp2(s - jnp.max(s, axis=-1, keepdims=True)).astype(BF16)

    def probs_bounded(r):
        return jnp.exp2(scores(r)).astype(BF16)

    def attend(r, p):
        _, k0 = key_start(r)
        ox = _dot(p, vx_ref[pl.ds(k0, nk), :])
        o = ox[:, 0:LANE] / ox[:, LANE:2 * LANE]
        q0 = pl.multiple_of(r * GRID_W, GRID_W)
        o_ref[0, pl.ds(q0, GRID_W), :] = jnp.where(first_q, o[:GRID_W], o[GRID_W:]).astype(BF16)

    @pl.when(bound_ok)
    def _():
        U = NA_ROWS_PER_STEP

        def row_body(i, p_prev):
            r = i * U
            for j in range(U):
                attend(r - U + j, p_prev[j])
            return tuple(probs_bounded(r + j) for j in range(U))

        p_last = lax.fori_loop(1, rows // U, row_body, tuple(probs_bounded(j) for j in range(U)))
        for j in range(U):
            attend(rows - U + j, p_last[j])

    @pl.when(jnp.logical_not(bound_ok))
    def _():
        U = NA_ROWS_PER_STEP_EXACT

        def row_body(i, carry):
            s_cur, p_prev = carry
            r = i * U
            for j in range(U):
                attend(r - U + j, p_prev[j])
            p = tuple(probs_exact(s) for s in s_cur)
            s_next = tuple(scores(jnp.minimum(r + U + j, rows - 1)) for j in range(U))
            return s_next, p

        p0 = tuple(probs_exact(scores(j)) for j in range(U))
        s1 = tuple(scores(U + j) for j in range(U))
        _, p_last = lax.fori_loop(1, rows // U, row_body, (s1, p0))
        for j in range(U):
            attend(rows - U + j, p_last[j])


def _na_score_bound(rpb, q_norm_w, k_norm_w):
    H = rpb.shape[0]
    qk = (NA_HEAD_DIM ** 0.5 * LOG2E * (1.0 + NA_BOUND_SLACK)) * jnp.max(jnp.abs(q_norm_w)) * jnp.max(jnp.abs(k_norm_w))
    b_max = jnp.max(rpb.reshape(H, -1), axis=1) * LOG2E
    b_self = rpb[:, NA_WIN_H - 1, NA_WIN_W - 1] * LOG2E
    bound = qk + b_max
    flag = jnp.all(bound - (b_self - qk) <= NA_MAX_BOUND_GAP)
    return bound, flag.astype(jnp.int32).reshape(1)


def _na(u3, table, flag):
    B, S, _ = u3.shape
    win_h = table.shape[1]
    return pl.pallas_call(
        _na_kernel,
        grid=(NA_HEADS // 2, B),
        in_specs=[
            pl.BlockSpec(memory_space=pltpu.SMEM),
            pl.BlockSpec((1, S, LANE), lambda h, b: (b, 0, U_NQ // LANE + h)),
            pl.BlockSpec((1, S, LANE), lambda h, b: (b, 0, U_NK // LANE + h)),
            pl.BlockSpec((1, S, LANE), lambda h, b: (b, 0, U_NV // LANE + h)),
            pl.BlockSpec((1, win_h, 2 * GRID_W, win_h * GRID_W), lambda h, b: (h, 0, 0, 0)),
        ],
        out_specs=pl.BlockSpec((1, S, LANE), lambda h, b: (b, 0, h)),
        out_shape=jax.ShapeDtypeStruct((B, S, NA_W), BF16),
        scratch_shapes=[pltpu.VMEM((S, 2 * LANE), BF16)],
        compiler_params=_cparams(("parallel", "parallel")),
        name="na_attn",
    )(flag, u3, u3, u3, table)


def _merge_kernel(x_ref, ys_ref, yg_ref, yn_ref, gate_ref, ws_ref, wg_ref, wn_ref, wo_ref, o_ref):
    D = D_MODEL
    mixed = _sigmoid(gate_ref[:, 0:D].astype(F32)) * _dot(ys_ref[...], ws_ref[...])
    mixed += _sigmoid(gate_ref[:, D:2 * D].astype(F32)) * _dot(yg_ref[...], wg_ref[...])
    mixed += _sigmoid(gate_ref[:, 2 * D:3 * D].astype(F32)) * _dot(yn_ref[...], wn_ref[...])
    o_ref[...] = x_ref[...] + _dot(mixed.astype(BF16), wo_ref[...])


def _merge(x2, ys, yg, yn, u2, ws, wg, wn, wo, tm=512):
    T = x2.shape[0]
    D = D_MODEL
    row = lambda i: (i, 0)
    fixed = lambda i: (0, 0)
    return pl.pallas_call(
        _merge_kernel,
        grid=(T // tm,),
        in_specs=[
            pl.BlockSpec((tm, D), row),
            pl.BlockSpec((tm, D), row),
            pl.BlockSpec((tm, D), row),
            pl.BlockSpec((tm, D), row),
            pl.BlockSpec((tm, N_BRANCH * D), lambda i: (i, U_GATE // (N_BRANCH * D))),
            pl.BlockSpec((D, D), fixed),
            pl.BlockSpec((D, D), fixed),
            pl.BlockSpec((D, D), fixed),
            pl.BlockSpec((D, D), fixed),
        ],
        out_specs=pl.BlockSpec((tm, D), row),
        out_shape=jax.ShapeDtypeStruct((T, D), F32),
        compiler_params=_cparams(("parallel",)),
        name="merge",
    )(x2, ys, yg, yn, u2, ws, wg, wn, wo)


def _mlp_kernel(x_ref, nw_ref, w1_ref, w2_ref, o_ref, *, tf):
    x = x_ref[...]
    ms = jnp.mean(x * x, axis=-1, keepdims=True)
    h = (x * lax.rsqrt(ms + EPS) * nw_ref[...]).astype(BF16)
    acc = x
    for f in range(D_FF // tf):
        a = jnp.maximum(_dot(h, w1_ref[:, f * tf:(f + 1) * tf]), 0.0)
        acc = acc + _dot((a * a).astype(BF16), w2_ref[f * tf:(f + 1) * tf, :])
    o_ref[...] = acc


def _mlp(x2, nw, w1, w2, tm=512, tf=1024):
    T = x2.shape[0]
    D = D_MODEL
    return pl.pallas_call(
        functools.partial(_mlp_kernel, tf=tf),
        grid=(T // tm,),
        in_specs=[
            pl.BlockSpec((tm, D), lambda i: (i, 0)),
            pl.BlockSpec((1, D), lambda i: (0, 0)),
            pl.BlockSpec((D, D_FF), lambda i: (0, 0)),
            pl.BlockSpec((D_FF, D), lambda i: (0, 0)),
        ],
        out_specs=pl.BlockSpec((tm, D), lambda i: (i, 0)),
        out_shape=jax.ShapeDtypeStruct((T, D), F32),
        compiler_params=_cparams(("parallel",)),
        name="mlp",
    )(x2, nw, w1, w2)


def _pad_rows(w, start, total):
    return jnp.zeros((total, w.shape[1]), w.dtype).at[start:start + w.shape[0]].set(w)


def kernel(x, norm_mix_w, w_in, ssd_conv_w, ssd_conv_b, ssd_dt_bias_f, ssd_dt_bias_b, ssd_a_log_f,
           ssd_a_log_b, ssd_d, ssd_norm_w, gla_a2_f, gla_a2_bias_f, gla_a2_b, gla_a2_bias_b,
           gla_norm_w, na_q_norm_w, na_k_norm_w, na_rpb, w_branch_ssd, w_branch_gla, w_branch_na,
           w_out, norm_mlp_w, w_ff1, w_ff2):
    B, S, D = x.shape
    T = B * S
    depth = w_in.shape[0]
    rows = S // GRID_W
    x2 = x.reshape(T, D)
    for l in range(depth):
        w_big = jnp.concatenate([w_in[l][:, a:a + n] for a, n in _BIG_SEGS], axis=1).astype(BF16)
        w_small = _small_weight(w_in[l]).astype(BF16)
        hg = (SSD_GROUPS, SSD_HG)
        zeros_r = jnp.zeros((SSD_GROUPS, LANE - 2 * SSD_HG), F32)
        dt_bias = jnp.concatenate([ssd_dt_bias_f[l].reshape(hg), ssd_dt_bias_b[l].reshape(hg), zeros_r], axis=1)
        a_neg = jnp.concatenate([-jnp.exp(ssd_a_log_f[l]).reshape(hg), -jnp.exp(ssd_a_log_b[l]).reshape(hg),
                                 zeros_r], axis=1)
        prow3 = jnp.zeros((SSD_GROUPS, 8, LANE), F32).at[:, 0].set(dt_bias).at[:, 1].set(a_neg)
        prow = prow3.reshape(SSD_GROUPS * 8, LANE)
        pcol = jnp.transpose(prow3[:, :, :SMALL_T_ROWS], (0, 2, 1))
        w_small_t = w_small.T.reshape(SSD_GROUPS, LANE, D)[:, :SMALL_T_ROWS].reshape(-1, D)
        drow = jnp.repeat(ssd_d[l], SSD_HEAD_DIM)[None, :]
        hk = (GLA_HEADS, 1, GLA_DK)
        a2 = jnp.concatenate([_pad_rows(gla_a2_f[l], SM_GAF, LANE).reshape((LANE,) + hk),
                              _pad_rows(gla_a2_b[l], SM_GAB, LANE).reshape((LANE,) + hk)],
                             axis=2).reshape(LANE, -1).astype(BF16)
        a2_bias = jnp.concatenate([gla_a2_bias_f[l].reshape(hk), gla_a2_bias_b[l].reshape(hk)],
                                  axis=1).reshape(1, -1)
        na_bound, na_flag = _na_score_bound(na_rpb[l], na_q_norm_w[l], na_k_norm_w[l])
        table = _na_bias_table(na_rpb[l] * LOG2E - na_bound[:, None, None], rows)
        qkw = jnp.zeros((8, NA_W), F32)
        qkw = qkw.at[0].set(jnp.tile(na_q_norm_w[l] * (NA_HEAD_DIM ** -0.5 * LOG2E), NA_HEADS))
        qkw = qkw.at[1].set(jnp.tile(na_k_norm_w[l], NA_HEADS))

        u2, us2, ust = _inproj(x2, norm_mix_w[l][None, :], w_big, w_small, w_small_t, qkw)
        u3 = u2.reshape(B, S, U_WIDTH)
        us3 = us2.reshape(B, S, SMALL_W)
        xbc = _conv(u3, ssd_conv_w[l], ssd_conv_b[l][None, :])
        y_ssd = _ssd(xbc, u3, us3, ust, prow, pcol, drow, ssd_norm_w[l][None, :])
        y_gla = _gla(u3, us3, a2, a2_bias, gla_norm_w[l][None, :])
        y_na = _na(u3, table, na_flag)
        x2 = _merge(x2, y_ssd.reshape(T, -1), y_gla.reshape(T, -1), y_na.reshape(T, -1), u2,
                    w_branch_ssd[l].astype(BF16), w_branch_gla[l].astype(BF16),
                    w_branch_na[l].astype(BF16), w_out[l].astype(BF16))
        x2 = _mlp(x2, norm_mlp_w[l][None, :], w_ff1[l].astype(BF16), w_ff2[l].astype(BF16))
    return x2.reshape(B, S, D)
```

```python
import functools

import jax
import jax.numpy as jnp
import numpy as np
from jax import lax
from jax.experimental import pallas as pl
from jax.experimental.pallas import tpu as pltpu

F32 = jnp.float32
BF16 = jnp.bfloat16

EPS = 1e-6
D_MODEL = 1024
GRID_W = 64

SSD_HEADS = 16
SSD_HEAD_DIM = 64
SSD_D_INNER = 1024
SSD_GROUPS = 2
SSD_STATE = 128
SSD_CONV = 5
SSD_CONV_DIM = 1536
SSD_CHUNK = 128
SSD_GROUP_W = SSD_D_INNER // SSD_GROUPS
SSD_HG = SSD_HEADS // SSD_GROUPS

GLA_HEADS = 4
GLA_DK = 128
GLA_DV = 256
GLA_KEY_W = 512
GLA_VAL_W = 1024
GLA_GATE_RANK = 16
GLA_GATE_NORM = 16.0
GLA_CHUNK = 64
GLA_BLOCKS_PER_STEP = 4

NA_HEADS = 16
NA_HEAD_DIM = 64
NA_W = 1024
NA_WIN_H = 8
NA_WIN_W = 16
NA_ROWS_PER_STEP = 8
NA_ROWS_PER_STEP_EXACT = 2
LOG2E = 1.4426950408889634
NA_BOUND_SLACK = 0.02
NA_MAX_BOUND_GAP = 90.0

N_BRANCH = 3
D_FF = 4096

IN_SIZES = (SSD_D_INNER, SSD_CONV_DIM, SSD_HEADS, SSD_HEADS,
            GLA_KEY_W, GLA_KEY_W, GLA_VAL_W, GLA_VAL_W, GLA_GATE_RANK, GLA_GATE_RANK,
            NA_W, NA_W, NA_W, N_BRANCH * D_MODEL)
_IN_OFF = np.concatenate([[0], np.cumsum(IN_SIZES)])
(_O_Z, _O_XBC, _O_DTF, _O_DTB, _O_GQ, _O_GK, _O_GV, _O_GG, _O_GAF, _O_GAB,
 _O_NQ, _O_NK, _O_NV, _O_GATE) = [int(v) for v in _IN_OFF[:-1]]

U_NQ = 0
U_NK = 1024
U_GQ = 2048
U_GK = 2560
U_GATE = 3072
U_NV = 6144
U_Z = 7168
U_GV = 8192
U_GG = 9216
U_XBC = 10240
U_WIDTH = 11776
INPROJ_COL_CHUNK = 1024
LANE = 128
VMEM_LIMIT = 56 * 1024 * 1024

SMALL_W = SSD_GROUPS * LANE
SM_DTF, SM_DTB, SM_GAF, SM_GAB = 0, 8, 16, 32
SMALL_T_ROWS = 2 * SSD_HG

_BIG_SEGS = ((_O_NQ, 2048), (_O_GQ, 1024), (_O_GATE, 3072), (_O_NV, 1024), (_O_Z, 1024), (_O_GV, 2048),
             (_O_XBC, 1536))


def _small_weight(w):
    blocks = []
    for g in range(SSD_GROUPS):
        h0 = g * SSD_HG
        cols = [w[:, _O_DTF + h0:_O_DTF + h0 + SSD_HG], w[:, _O_DTB + h0:_O_DTB + h0 + SSD_HG]]
        used = 2 * SSD_HG
        if g == 0:
            cols += [w[:, _O_GAF:_O_GAF + GLA_GATE_RANK], w[:, _O_GAB:_O_GAB + GLA_GATE_RANK]]
            used += 2 * GLA_GATE_RANK
        cols.append(jnp.zeros((w.shape[0], LANE - used), w.dtype))
        blocks += cols
    return jnp.concatenate(blocks, axis=1)


def _cparams(sem, vmem=VMEM_LIMIT):
    return pltpu.CompilerParams(dimension_semantics=sem, vmem_limit_bytes=vmem)


def _sigmoid(x):
    return 1.0 / (1.0 + jnp.exp(-x))


def _silu(x):
    return x * _sigmoid(x)


def _softplus(x):
    return jnp.maximum(x, 0.0) + jnp.log1p(jnp.exp(-jnp.abs(x)))


def _log2_sigmoid(x):
    t = x * (-LOG2E)
    return -(jnp.maximum(t, 0.0) + jnp.log2(1.0 + jnp.exp2(-jnp.abs(t))))


def _nt(a, b):
    return lax.dot_general(a, b, (((1,), (1,)), ((), ())), preferred_element_type=F32)


def _tn(a, b):
    return lax.dot_general(a, b, (((0,), (0,)), ((), ())), preferred_element_type=F32)


def _dot(a, b):
    return jnp.dot(a, b, preferred_element_type=F32)


def _dot_exact(a, b):
    return jnp.dot(a, b, preferred_element_type=F32, precision=lax.Precision.HIGHEST)


def _iota2(shape, dim):
    return lax.broadcasted_iota(jnp.int32, shape, dim)


def _head_rms(r, w_row):
    er = _iota2((LANE, LANE), 0) // NA_HEAD_DIM
    ec = _iota2((LANE, LANE), 1) // NA_HEAD_DIM
    e_blk = (er == ec).astype(BF16)
    outs = []
    for a in range(0, r.shape[1], LANE):
        x = r[:, a:a + LANE]
        ms = _dot((x * x).astype(BF16), e_blk) * (1.0 / NA_HEAD_DIM)
        outs.append(x * lax.rsqrt(ms + EPS))
    return jnp.concatenate(outs, axis=1) * w_row


def _inproj_kernel(x_ref, nw_ref, w_ref, ws_ref, wst_ref, qkw_ref, u_ref, us_ref, ust_ref, h_ref):
    tn = u_ref.shape[1]
    chunks = [(a, min(a + INPROJ_COL_CHUNK, tn)) for a in range(0, tn, INPROJ_COL_CHUNK)]

    @pl.when(pl.program_id(1) == 0)
    def _():
        x = x_ref[...]
        ms = jnp.mean(x * x, axis=-1, keepdims=True)
        h = (x * lax.rsqrt(ms + EPS) * nw_ref[...]).astype(BF16)
        h_ref[...] = h
        us_ref[...] = _dot(h, ws_ref[...])
        ust_ref[...] = _nt(wst_ref[...], h)
        for c, (a, b) in enumerate(chunks):
            r = _dot(h, w_ref[:, a:b])
            if c < 2:
                r = _head_rms(r, qkw_ref[c:c + 1, :])
            u_ref[:, a:b] = r.astype(BF16)

    @pl.when(pl.program_id(1) != 0)
    def _():
        for a, b in chunks:
            u_ref[:, a:b] = _dot(h_ref[...], w_ref[:, a:b]).astype(BF16)


def _inproj(x2, nw, w_big, w_small, w_small_t, qkw, tm=1024, tn=U_WIDTH // 4):
    assert (U_NQ, U_NK) == (0, INPROJ_COL_CHUNK) and NA_W == INPROJ_COL_CHUNK
    T = x2.shape[0]
    return pl.pallas_call(
        _inproj_kernel,
        grid=(T // tm, U_WIDTH // tn),
        in_specs=[
            pl.BlockSpec((tm, D_MODEL), lambda i, j: (i, 0)),
            pl.BlockSpec((1, D_MODEL), lambda i, j: (0, 0)),
            pl.BlockSpec((D_MODEL, tn), lambda i, j: (0, j)),
            pl.BlockSpec((D_MODEL, SMALL_W), lambda i, j: (0, 0)),
            pl.BlockSpec((SSD_GROUPS * SMALL_T_ROWS, D_MODEL), lambda i, j: (0, 0)),
            pl.BlockSpec((8, NA_W), lambda i, j: (0, 0)),
        ],
        out_specs=[
            pl.BlockSpec((tm, tn), lambda i, j: (i, j)),
            pl.BlockSpec((tm, SMALL_W), lambda i, j: (i, 0)),
            pl.BlockSpec((SSD_GROUPS * SMALL_T_ROWS, tm), lambda i, j: (0, i)),
        ],
        out_shape=[
            jax.ShapeDtypeStruct((T, U_WIDTH), BF16),
            jax.ShapeDtypeStruct((T, SMALL_W), F32),
            jax.ShapeDtypeStruct((SSD_GROUPS * SMALL_T_ROWS, T), F32),
        ],
        scratch_shapes=[pltpu.VMEM((tm, D_MODEL), BF16)],
        compiler_params=_cparams(("parallel", "arbitrary")),
        name="inproj",
    )(x2, nw, w_big, w_small, w_small_t, qkw)


def _conv_kernel(u_ref, w_ref, b_ref, o_ref):
    x = u_ref[0].astype(F32)
    S = x.shape[0]
    row = _iota2(x.shape, 0)
    acc = jnp.zeros_like(x) + b_ref[...]
    pad = SSD_CONV // 2
    for k in range(SSD_CONV):
        d = k - pad
        if d == 0:
            xs = x
        else:
            xs = pltpu.roll(x, (-d) % S, axis=0)
            valid = (row + d >= 0) & (row + d < S)
            xs = jnp.where(valid, xs, 0.0)
        acc = acc + w_ref[k:k + 1, :] * xs
    o_ref[0] = _silu(acc).astype(BF16)


def _conv(u3, conv_w, conv_b, tc=256):
    B, S, _ = u3.shape
    nblk = SSD_CONV_DIM // tc
    off = U_XBC // tc
    return pl.pallas_call(
        _conv_kernel,
        grid=(B, nblk),
        in_specs=[
            pl.BlockSpec((1, S, tc), lambda b, c: (b, 0, off + c)),
            pl.BlockSpec((SSD_CONV, tc), lambda b, c: (0, c)),
            pl.BlockSpec((1, tc), lambda b, c: (0, c)),
        ],
        out_specs=pl.BlockSpec((1, S, tc), lambda b, c: (b, 0, c)),
        out_shape=jax.ShapeDtypeStruct((B, S, SSD_CONV_DIM), BF16),
        compiler_params=_cparams(("parallel", "parallel")),
        name="ssd_conv",
    )(u3, conv_w, conv_b)


def _split_hi_lo(v):
    hi = v.astype(BF16)
    lo = (v - hi.astype(F32)).astype(BF16)
    return jnp.concatenate([hi, lo], axis=1)


def _split3(v, axis):
    hi = v.astype(BF16)
    r1 = v - hi.astype(F32)
    mid = r1.astype(BF16)
    lo = (r1 - mid.astype(F32)).astype(BF16)
    return jnp.concatenate([hi, mid, lo], axis=axis)


def _ssd_kernel(x_ref, b_ref, c_ref, z_ref, us_ref, ust_ref, prow_ref, pcol_ref, drow_ref, nw_ref,
                o_ref, acc_ref, cumc_ref, cumr_ref, wdt_ref, dec_ref, xs_ref, decx_ref, sst_ref, st_ref):
    L = SSD_CHUNK
    S = x_ref.shape[1]
    nc = S // L
    W = SSD_GROUP_W
    HG = SSD_HG
    R = 2 * HG

    ii = _iota2((L, L), 0)
    jj = _iota2((L, L), 1)
    tril = ii >= jj
    triu = jj >= ii
    tril_b = tril.astype(BF16)
    triu_b = triu.astype(BF16)

    er = _iota2((LANE, 2 * W), 0)
    ec = _iota2((LANE, 2 * W), 1)
    e = (er == jnp.where(ec < W, ec // SSD_HEAD_DIM, (ec - W) // SSD_HEAD_DIM + HG)).astype(BF16)
    e2 = jnp.concatenate([e, e], axis=0)

    bias_row = prow_ref[0:1, :]
    a_row = prow_ref[1:2, :]
    bias_col = pcol_ref[0, :, 0:1]
    a_col = pcol_ref[0, :, 1:2]

    lane_fwd = _iota2((L, LANE), 1) < HG
    row_fwd = _iota2((R, L), 0) < HG
    lane_half = _iota2((L, LANE), 1) < SSD_HEAD_DIM

    def decay_body(c, carry):
        rs = pl.ds(pl.multiple_of(c * L, L), L)
        dt_c = _softplus(us_ref[0, rs, :] + bias_row)
        a_c = dt_c * a_row
        pp = _dot(tril_b, _split3(a_c, 1))
        p_c = pp[:, 0:LANE] + pp[:, LANE:2 * LANE] + pp[:, 2 * LANE:3 * LANE]
        tot_c = p_c[L - 1:L, :]
        cum_c = jnp.where(lane_fwd, p_c, tot_c - p_c + a_c)
        cumc_ref[rs, :] = cum_c * LOG2E
        wdt_ref[rs, :] = jnp.exp(tot_c - cum_c) * dt_c
        dec_ref[c] = jnp.broadcast_to(jnp.exp(tot_c), (16, LANE))
        dt_r = _softplus(ust_ref[:, rs] + bias_col)
        a_r = dt_r * a_col
        pr = _dot(_split3(a_r, 0), triu_b)
        p_r = pr[0:R] + pr[R:2 * R] + pr[2 * R:3 * R]
        tot_r = p_r[:, L - 1:L]
        cum_r = jnp.where(row_fwd, p_r, tot_r - p_r + a_r)
        cumr_ref[:, rs] = (cum_r - jnp.log(dt_r)) * LOG2E
        return carry

    lax.fori_loop(0, nc, decay_body, 0, unroll=2)

    def intra_body(c, carry):
        rs = pl.ds(pl.multiple_of(c * L, L), L)
        x_c = x_ref[0, rs, :]
        b_c = b_ref[0, rs, :]
        cb = _nt(c_ref[0, rs, :], b_c)
        ex = _dot(_split_hi_lo(jnp.concatenate([wdt_ref[rs, :], dec_ref[c]], axis=0)), e2)
        decx_ref[c] = ex[L:L + 8]
        xf = x_c.astype(F32)
        xw = jnp.concatenate([xf * ex[0:L, 0:W], xf * ex[0:L, W:2 * W]], axis=1).astype(BF16)
        b_t = b_c.astype(F32).T.astype(BF16)
        xs_ref[c] = _dot(b_t, xw)
        cum_c = cumc_ref[rs, :]
        cum_r = cumr_ref[:, rs]
        for hp in range(HG // 2):
            ms = []
            for hh in range(2):
                hf = 2 * hp + hh
                hb = HG + 2 * hp + hh
                decf = jnp.exp2(jnp.where(tril, cum_c[:, hf:hf + 1] - cum_r[hf:hf + 1, :], -jnp.inf))
                decb = jnp.exp2(jnp.where(triu, cum_c[:, hb:hb + 1] - cum_r[hb:hb + 1, :], -jnp.inf))
                ms.append((cb * (decf + decb)).astype(BF16))
            m2 = jnp.concatenate(ms, axis=1)
            xp = x_c[:, hp * LANE:(hp + 1) * LANE]
            xz = jnp.zeros_like(xp)
            x2 = jnp.concatenate([jnp.where(lane_half, xp, xz), jnp.where(lane_half, xz, xp)], axis=0)
            acc_ref[rs, hp * LANE:(hp + 1) * LANE] = _dot(m2, x2)
        return carry

    lax.fori_loop(0, nc, intra_body, 0, unroll=2)

    st_ref[...] = jnp.zeros_like(st_ref)

    def state_body(i, carry):
        for d, c in enumerate((i, nc - 1 - i)):
            ls = slice(d * W, (d + 1) * W)
            s_prev = st_ref[:, ls]
            sst_ref[c, :, ls] = s_prev.astype(BF16)
            st_ref[:, ls] = s_prev * decx_ref[c, 0:1, ls] + xs_ref[c, :, ls]
        return carry

    lax.fori_loop(0, nc, state_body, 0)

    def out_body(c, carry):
        rs = pl.ds(pl.multiple_of(c * L, L), L)
        yo = _dot(c_ref[0, rs, :], sst_ref[c])
        sc = _dot(_split_hi_lo(jnp.exp2(cumc_ref[rs, :])), e2)
        xf = x_ref[0, rs, :].astype(F32)
        y = acc_ref[rs, :] + yo[:, 0:W] * sc[:, 0:W] + yo[:, W:2 * W] * sc[:, W:2 * W] + xf * drow_ref[...]
        y = y * _silu(z_ref[0, rs, :].astype(F32))
        ms = jnp.mean(y * y, axis=-1, keepdims=True)
        o_ref[0, rs, :] = (y * lax.rsqrt(ms + EPS) * nw_ref[...]).astype(BF16)
        return carry

    lax.fori_loop(0, nc, out_body, 0, unroll=2)


def _ssd(xbc, u3, us3, ust, prow, pcol, drow, nw):
    B, S, _ = xbc.shape
    W = SSD_GROUP_W
    N = SSD_STATE
    nc = S // SSD_CHUNK
    return pl.pallas_call(
        _ssd_kernel,
        grid=(B, SSD_GROUPS),
        in_specs=[
            pl.BlockSpec((1, S, W), lambda b, g: (b, 0, g)),
            pl.BlockSpec((1, S, N), lambda b, g: (b, 0, SSD_D_INNER // N + g)),
            pl.BlockSpec((1, S, N), lambda b, g: (b, 0, SSD_D_INNER // N + SSD_GROUPS + g)),
            pl.BlockSpec((1, S, W), lambda b, g: (b, 0, U_Z // W + g)),
            pl.BlockSpec((1, S, LANE), lambda b, g: (b, 0, g)),
            pl.BlockSpec((SMALL_T_ROWS, S), lambda b, g: (g, b)),
            pl.BlockSpec((8, LANE), lambda b, g: (g, 0)),
            pl.BlockSpec((1, SMALL_T_ROWS, 8), lambda b, g: (g, 0, 0)),
            pl.BlockSpec((1, W), lambda b, g: (0, g)),
            pl.BlockSpec((1, W), lambda b, g: (0, g)),
        ],
        out_specs=pl.BlockSpec((1, S, W), lambda b, g: (b, 0, g)),
        out_shape=jax.ShapeDtypeStruct((B, S, SSD_D_INNER), BF16),
        scratch_shapes=[
            pltpu.VMEM((S, W), F32),
            pltpu.VMEM((S, LANE), F32),
            pltpu.VMEM((SMALL_T_ROWS, S), F32),
            pltpu.VMEM((S, LANE), F32),
            pltpu.VMEM((nc, 16, LANE), F32),
            pltpu.VMEM((nc, N, 2 * W), F32),
            pltpu.VMEM((nc, 8, 2 * W), F32),
            pltpu.VMEM((nc, N, 2 * W), BF16),
            pltpu.VMEM((N, 2 * W), F32),
        ],
        compiler_params=_cparams(("parallel", "parallel")),
        name="ssd_scan",
    )(xbc, xbc, xbc, u3, us3, ust, prow, pcol, drow, nw)


def _gla_kernel(q_ref, k_ref, v_ref, gg_ref, us_ref, a2_ref, bias_ref, nw_ref,
                o_ref, acc_ref, g_ref, qd_ref, kd_ref, kdp_ref, qcat_ref, x_ref, dec_ref, sst_ref, st_ref):
    L = GLA_CHUNK
    BL = 2 * L
    DK = GLA_DK
    S = q_ref.shape[1]
    nb = S // BL
    scale = DK ** -0.5

    ii = _iota2((BL, BL), 0)
    jj = _iota2((BL, BL), 1)
    same = (ii // L) == (jj // L)
    masks = (same & (ii >= jj), same & (jj >= ii))
    tri2 = masks[0].astype(BF16)
    par_row = _iota2((BL, DK), 0) // L

    ga = us_ref[0].astype(BF16)
    g_ref[...] = _log2_sigmoid(_dot(ga, a2_ref[...]) + bias_ref[...]) * (1.0 / GLA_GATE_NORM)

    def decay_body(i, carry):
        rs = pl.ds(pl.multiple_of(i * BL, BL), BL)
        g = g_ref[rs, :]
        hi = g.astype(BF16)
        r1 = g - hi.astype(F32)
        mid = r1.astype(BF16)
        lo = (r1 - mid.astype(F32)).astype(BF16)
        pp = _dot(tri2, jnp.concatenate([hi, mid, lo], axis=1))
        p = pp[:, 0:2 * DK] + pp[:, 2 * DK:4 * DK] + pp[:, 4 * DK:6 * DK]
        q_c = q_ref[0, rs, :].astype(F32) * scale
        k_c = k_ref[0, rs, :].astype(F32)
        zero = jnp.zeros((BL, DK), BF16)
        for d in range(2):
            p_d = p[:, d * DK:(d + 1) * DK]
            tot = jnp.where(par_row == 0, p_d[L - 1:L, :], p_d[BL - 1:BL, :])
            b = p_d if d == 0 else tot - p_d + g[:, DK:]
            qd = (q_c * jnp.exp2(b)).astype(BF16)
            kdec = (k_c * jnp.exp2(tot - b)).astype(BF16)
            qd_ref[d, rs, :] = qd
            kd_ref[d, rs, :] = (k_c * jnp.exp2(-b)).astype(BF16)
            for par in range(2):
                sel = par_row == par
                kdp_ref[d, rs, par * DK:(par + 1) * DK] = jnp.where(sel, kdec, zero)
                qcat_ref[rs, (2 * d + par) * DK:(2 * d + par + 1) * DK] = jnp.where(sel, qd, zero)
                last = (par + 1) * L - 1
                dec_ref[d, 2 * i + par] = jnp.broadcast_to(jnp.exp2(p_d[last:last + 1, :]), (8, DK))
        return carry

    lax.fori_loop(0, nb, decay_body, 0, unroll=2)

    G = GLA_BLOCKS_PER_STEP

    def intra_body(i, carry):
        blks = [G * i + j for j in range(G)]
        rss = [pl.ds(pl.multiple_of(b * BL, BL), BL) for b in blks]
        vs = [v_ref[0, rs, :] for rs in rss]
        atts = [[_nt(qd_ref[d, rs, :], kd_ref[d, rs, :]) for d in range(2)] for rs in rss]
        for j, b in enumerate(blks):
            for d in range(2):
                x_ref[d, b] = _tn(vs[j], kdp_ref[d, rss[j], :])
        for j in range(G):
            att = jnp.where(masks[0], atts[j][0], 0.0) + jnp.where(masks[1], atts[j][1], 0.0)
            acc_ref[rss[j], :] = _dot(att.astype(BF16), vs[j])
        return carry

    lax.fori_loop(0, nb // G, intra_body, 0)

    st_ref[...] = jnp.zeros_like(st_ref)

    def state_body(i, carry):
        for d, b in enumerate((i, nb - 1 - i)):
            for par in ((0, 1) if d == 0 else (1, 0)):
                s_prev = st_ref[d]
                lane0 = (2 * d + par) * DK
                sst_ref[b, :, lane0:lane0 + DK] = s_prev.astype(BF16)
                st_ref[d] = s_prev * dec_ref[d, 2 * b + par, 0:1, :] + x_ref[d, b, :, par * DK:(par + 1) * DK]
        return carry

    lax.fori_loop(0, nb, state_body, 0)

    def out_body(b, carry):
        rs = pl.ds(pl.multiple_of(b * BL, BL), BL)
        o = acc_ref[rs, :] + _nt(qcat_ref[rs, :], sst_ref[b])
        ms = jnp.mean(o * o, axis=-1, keepdims=True)
        o = o * lax.rsqrt(ms + EPS) * nw_ref[...]
        o_ref[0, rs, :] = (o * _silu(gg_ref[0, rs, :].astype(F32))).astype(BF16)
        return carry

    lax.fori_loop(0, nb, out_body, 0, unroll=GLA_BLOCKS_PER_STEP)


def _gla(u3, us3, a2, bias, nw):
    B, S, _ = u3.shape
    DK, DV = GLA_DK, GLA_DV
    nc = S // GLA_CHUNK
    nb = nc // 2
    return pl.pallas_call(
        _gla_kernel,
        grid=(B, GLA_HEADS),
        in_specs=[
            pl.BlockSpec((1, S, DK), lambda b, h: (b, 0, U_GQ // DK + h)),
            pl.BlockSpec((1, S, DK), lambda b, h: (b, 0, U_GK // DK + h)),
            pl.BlockSpec((1, S, DV), lambda b, h: (b, 0, U_GV // DV + h)),
            pl.BlockSpec((1, S, DV), lambda b, h: (b, 0, U_GG // DV + h)),
            pl.BlockSpec((1, S, LANE), lambda b, h: (b, 0, 0)),
            pl.BlockSpec((LANE, 2 * DK), lambda b, h: (0, h)),
            pl.BlockSpec((1, 2 * DK), lambda b, h: (0, h)),
            pl.BlockSpec((1, DV), lambda b, h: (0, 0)),
        ],
        out_specs=pl.BlockSpec((1, S, DV), lambda b, h: (b, 0, h)),
        out_shape=jax.ShapeDtypeStruct((B, S, GLA_VAL_W), BF16),
        scratch_shapes=[
            pltpu.VMEM((S, DV), F32),
            pltpu.VMEM((S, 2 * DK), F32),
            pltpu.VMEM((2, S, DK), BF16),
            pltpu.VMEM((2, S, DK), BF16),
            pltpu.VMEM((2, S, 2 * DK), BF16),
            pltpu.VMEM((S, 4 * DK), BF16),
            pltpu.VMEM((2, nb, DV, 2 * DK), F32),
            pltpu.VMEM((2, nc, 8, DK), F32),
            pltpu.VMEM((nb, DV, 4 * DK), BF16),
            pltpu.VMEM((2, DV, DK), F32),
        ],
        compiler_params=_cparams(("parallel", "parallel")),
        name="gla_scan",
    )(u3, u3, u3, u3, us3, a2, bias, nw)


def _na_col_tables():
    qcol = np.arange(GRID_W)[:, None]
    kcol = np.arange(GRID_W)[None, :]
    w_start = np.clip(qcol - NA_WIN_W // 2, 0, GRID_W - NA_WIN_W)
    mask = (kcol >= w_start) & (kcol < w_start + NA_WIN_W)
    off = np.clip(kcol - qcol, -(NA_WIN_W - 1), NA_WIN_W - 1) + (NA_WIN_W - 1)
    return mask, off


def _na_bias_table(rpb, rows):
    win_h = min(NA_WIN_H, rows)
    mask, off = _na_col_tables()
    H = rpb.shape[0]
    padw = GRID_W - NA_WIN_W
    ext = jnp.pad(rpb, ((0, 0), (0, 0), (padw, padw)), mode="edge")
    R = rpb.shape[1]
    n2 = 2 * GRID_W
    tiled = jnp.broadcast_to(jnp.pad(ext, ((0, 0), (0, 0), (0, 1)))[:, :, None, :], (H, R, GRID_W, n2))
    view = tiled.reshape(H, R, GRID_W * n2)[:, :, :GRID_W * (n2 - 1)].reshape(H, R, GRID_W, n2 - 1)
    a = view[:, :, :, GRID_W - 1:]
    a = jnp.where(mask[None, None], a, -jnp.inf)
    a = a.reshape(H // 2, 2, 2 * NA_WIN_H - 1, GRID_W, GRID_W)
    per_delta = []
    for delta in range(win_h):
        lo = NA_WIN_H - 1 - delta
        per_delta.append(jnp.concatenate([a[:, :, lo + w] for w in range(win_h)], axis=-1))
    t = jnp.stack(per_delta, axis=1)
    return t.reshape(H // 2, win_h, 2 * GRID_W, win_h * GRID_W).astype(F32)


def _na_kernel(flag_ref, q_ref, k_ref, v_ref, tab_ref, o_ref, vx_ref):
    S = q_ref.shape[1]
    rows = S // GRID_W
    win_h = tab_ref.shape[1]
    nk = win_h * GRID_W

    vx_ref[:, 0:LANE] = v_ref[0]
    vx_ref[:, LANE:2 * LANE] = jnp.ones((S, LANE), BF16)
    bound_ok = flag_ref[0] != 0

    first_q = _iota2((GRID_W, LANE), 1) < NA_HEAD_DIM

    def key_start(r):
        r0 = jnp.clip(r - win_h // 2, 0, rows - win_h)
        return r0, pl.multiple_of(r0 * GRID_W, GRID_W)

    def scores(r):
        r0, k0 = key_start(r)
        q = q_ref[0, pl.ds(pl.multiple_of(r * GRID_W, GRID_W), GRID_W), :]
        zero = jnp.zeros_like(q)
        qs = jnp.concatenate([jnp.where(first_q, q, zero), jnp.where(first_q, zero, q)], axis=0)
        return _nt(qs, k_ref[0, pl.ds(k0, nk), :]) + tab_ref[0, r - r0]

    def probs_exact(s):
        return jnp.exp2(s - jnp.max(s, axis=-1, keepdims=True)).astype(BF16)

    def probs_bounded(r):
        return jnp.exp2(scores(r)).astype(BF16)

    def attend(r, p):
        _, k0 = key_start(r)
        ox = _dot(p, vx_ref[pl.ds(k0, nk), :])
        o = ox[:, 0:LANE] / ox[:, LANE:2 * LANE]
        q0 = pl.multiple_of(r * GRID_W, GRID_W)
        o_ref[0, pl.ds(q0, GRID_W), :] = jnp.where(first_q, o[:GRID_W], o[GRID_W:]).astype(BF16)

    @pl.when(bound_ok)
    def _():
        U = NA_ROWS_PER_STEP

        def row_body(i, p_prev):
            r = i * U
            for j in range(U):
                attend(r - U + j, p_prev[j])
            return tuple(probs_bounded(r + j) for j in range(U))

        p_last = lax.fori_loop(1, rows // U, row_body, tuple(probs_bounded(j) for j in range(U)))
        for j in range(U):
            attend(rows - U + j, p_last[j])

    @pl.when(jnp.logical_not(bound_ok))
    def _():
        U = NA_ROWS_PER_STEP_EXACT

        def row_body(i, carry):
            s_cur, p_prev = carry
            r = i * U
            for j in range(U):
                attend(r - U + j, p_prev[j])
            p = tuple(probs_exact(s) for s in s_cur)
            s_next = tuple(scores(jnp.minimum(r + U + j, rows - 1)) for j in range(U))
            return s_next, p

        p0 = tuple(probs_exact(scores(j)) for j in range(U))
        s1 = tuple(scores(U + j) for j in range(U))
        _, p_last = lax.fori_loop(1, rows // U, row_body, (s1, p0))
        for j in range(U):
            attend(rows - U + j, p_last[j])


def _na_score_bound(rpb, q_norm_w, k_norm_w):
    H = rpb.shape[0]
    qk = (NA_HEAD_DIM ** 0.5 * LOG2E * (1.0 + NA_BOUND_SLACK)) * jnp.max(jnp.abs(q_norm_w)) * jnp.max(jnp.abs(k_norm_w))
    b_max = jnp.max(rpb.reshape(H, -1), axis=1) * LOG2E
    b_self = rpb[:, NA_WIN_H - 1, NA_WIN_W - 1] * LOG2E
    bound = qk + b_max
    flag = jnp.all(bound - (b_self - qk) <= NA_MAX_BOUND_GAP)
    return bound, flag.astype(jnp.int32).reshape(1)


def _na(u3, table, flag):
    B, S, _ = u3.shape
    win_h = table.shape[1]
    return pl.pallas_call(
        _na_kernel,
        grid=(NA_HEADS // 2, B),
        in_specs=[
            pl.BlockSpec(memory_space=pltpu.SMEM),
            pl.BlockSpec((1, S, LANE), lambda h, b: (b, 0, U_NQ // LANE + h)),
            pl.BlockSpec((1, S, LANE), lambda h, b: (b, 0, U_NK // LANE + h)),
            pl.BlockSpec((1, S, LANE), lambda h, b: (b, 0, U_NV // LANE + h)),
            pl.BlockSpec((1, win_h, 2 * GRID_W, win_h * GRID_W), lambda h, b: (h, 0, 0, 0)),
        ],
        out_specs=pl.BlockSpec((1, S, LANE), lambda h, b: (b, 0, h)),
        out_shape=jax.ShapeDtypeStruct((B, S, NA_W), BF16),
        scratch_shapes=[pltpu.VMEM((S, 2 * LANE), BF16)],
        compiler_params=_cparams(("parallel", "parallel")),
        name="na_attn",
    )(flag, u3, u3, u3, table)


def _merge_kernel(x_ref, ys_ref, yg_ref, yn_ref, gate_ref, ws_ref, wg_ref, wn_ref, wo_ref, o_ref):
    D = D_MODEL
    mixed = _sigmoid(gate_ref[:, 0:D].astype(F32)) * _dot(ys_ref[...], ws_ref[...])
    mixed += _sigmoid(gate_ref[:, D:2 * D].astype(F32)) * _dot(yg_ref[...], wg_ref[...])
    mixed += _sigmoid(gate_ref[:, 2 * D:3 * D].astype(F32)) * _dot(yn_ref[...], wn_ref[...])
    o_ref[...] = x_ref[...] + _dot(mixed.astype(BF16), wo_ref[...])


def _merge(x2, ys, yg, yn, u2, ws, wg, wn, wo, tm=512):
    T = x2.shape[0]
    D = D_MODEL
    row = lambda i: (i, 0)
    fixed = lambda i: (0, 0)
    return pl.pallas_call(
        _merge_kernel,
        grid=(T // tm,),
        in_specs=[
            pl.BlockSpec((tm, D), row),
            pl.BlockSpec((tm, D), row),
            pl.BlockSpec((tm, D), row),
            pl.BlockSpec((tm, D), row),
            pl.BlockSpec((tm, N_BRANCH * D), lambda i: (i, U_GATE // (N_BRANCH * D))),
            pl.BlockSpec((D, D), fixed),
            pl.BlockSpec((D, D), fixed),
            pl.BlockSpec((D, D), fixed),
            pl.BlockSpec((D, D), fixed),
        ],
        out_specs=pl.BlockSpec((tm, D), row),
        out_shape=jax.ShapeDtypeStruct((T, D), F32),
        compiler_params=_cparams(("parallel",)),
        name="merge",
    )(x2, ys, yg, yn, u2, ws, wg, wn, wo)


def _mlp_kernel(x_ref, nw_ref, w1_ref, w2_ref, o_ref, *, tf):
    x = x_ref[...]
    ms = jnp.mean(x * x, axis=-1, keepdims=True)
    h = (x * lax.rsqrt(ms + EPS) * nw_ref[...]).astype(BF16)
    acc = x
    for f in range(D_FF // tf):
        a = jnp.maximum(_dot(h, w1_ref[:, f * tf:(f + 1) * tf]), 0.0)
        acc = acc + _dot((a * a).astype(BF16), w2_ref[f * tf:(f + 1) * tf, :])
    o_ref[...] = acc


def _mlp(x2, nw, w1, w2, tm=512, tf=1024):
    T = x2.shape[0]
    D = D_MODEL
    return pl.pallas_call(
        functools.partial(_mlp_kernel, tf=tf),
        grid=(T // tm,),
        in_specs=[
            pl.BlockSpec((tm, D), lambda i: (i, 0)),
            pl.BlockSpec((1, D), lambda i: (0, 0)),
            pl.BlockSpec((D, D_FF), lambda i: (0, 0)),
            pl.BlockSpec((D_FF, D), lambda i: (0, 0)),
        ],
        out_specs=pl.BlockSpec((tm, D), lambda i: (i, 0)),
        out_shape=jax.ShapeDtypeStruct((T, D), F32),
        compiler_params=_cparams(("parallel",)),
        name="mlp",
    )(x2, nw, w1, w2)


def _pad_rows(w, start, total):
    return jnp.zeros((total, w.shape[1]), w.dtype).at[start:start + w.shape[0]].set(w)


def kernel(x, norm_mix_w, w_in, ssd_conv_w, ssd_conv_b, ssd_dt_bias_f, ssd_dt_bias_b, ssd_a_log_f,
           ssd_a_log_b, ssd_d, ssd_norm_w, gla_a2_f, gla_a2_bias_f, gla_a2_b, gla_a2_bias_b,
           gla_norm_w, na_q_norm_w, na_k_norm_w, na_rpb, w_branch_ssd, w_branch_gla, w_branch_na,
           w_out, norm_mlp_w, w_ff1, w_ff2):
    B, S, D = x.shape
    T = B * S
    depth = w_in.shape[0]
    rows = S // GRID_W
    x2 = x.reshape(T, D)

    def small_params(w_in_l, dt_bias_f, dt_bias_b, a_log_f, a_log_b, ssd_d_l, a2_f, a2_bias_f, a2_b, a2_bias_b,
                     q_norm_w, k_norm_w, rpb):
        w_small = _small_weight(w_in_l).astype(BF16)
        hg = (SSD_GROUPS, SSD_HG)
        zeros_r = jnp.zeros((SSD_GROUPS, LANE - 2 * SSD_HG), F32)
        dt_bias = jnp.concatenate([dt_bias_f.reshape(hg), dt_bias_b.reshape(hg), zeros_r], axis=1)
        a_neg = jnp.concatenate([-jnp.exp(a_log_f).reshape(hg), -jnp.exp(a_log_b).reshape(hg), zeros_r], axis=1)
        prow3 = jnp.concatenate([dt_bias[:, None], a_neg[:, None], jnp.zeros((SSD_GROUPS, 6, LANE), F32)], axis=1)
        hk = (GLA_HEADS, 1, GLA_DK)
        a2 = jnp.concatenate([_pad_rows(a2_f, SM_GAF, LANE).reshape((LANE,) + hk),
                              _pad_rows(a2_b, SM_GAB, LANE).reshape((LANE,) + hk)], axis=2)
        na_bound, na_flag = _na_score_bound(rpb, q_norm_w, k_norm_w)
        q_row = jnp.tile(q_norm_w * (NA_HEAD_DIM ** -0.5 * LOG2E), NA_HEADS)
        k_row = jnp.tile(k_norm_w, NA_HEADS)
        return dict(
            w_small=w_small,
            w_small_t=w_small.T.reshape(SSD_GROUPS, LANE, D)[:, :SMALL_T_ROWS].reshape(-1, D),
            prow=prow3.reshape(SSD_GROUPS * 8, LANE),
            pcol=jnp.transpose(prow3[:, :, :SMALL_T_ROWS], (0, 2, 1)),
            drow=jnp.repeat(ssd_d_l, SSD_HEAD_DIM)[None, :],
            a2=a2.reshape(LANE, -1).astype(BF16),
            a2_bias=jnp.concatenate([a2_bias_f.reshape(hk), a2_bias_b.reshape(hk)], axis=1).reshape(1, -1),
            table=_na_bias_table(rpb * LOG2E - na_bound[:, None, None], rows),
            na_flag=na_flag,
            qkw=jnp.concatenate([q_row[None], k_row[None], jnp.zeros((6, NA_W), F32)], axis=0),
        )

    small = jax.vmap(small_params)(w_in, ssd_dt_bias_f, ssd_dt_bias_b, ssd_a_log_f, ssd_a_log_b, ssd_d,
                                   gla_a2_f, gla_a2_bias_f, gla_a2_b, gla_a2_bias_b, na_q_norm_w, na_k_norm_w,
                                   na_rpb)
    w_big_all = jnp.concatenate([w_in[:, :, a:a + n] for a, n in _BIG_SEGS], axis=2).astype(BF16)

    for l in range(depth):
        p = {name: v[l] for name, v in small.items()}
        w_big, w_small, w_small_t = w_big_all[l], p["w_small"], p["w_small_t"]
        prow, pcol, drow = p["prow"], p["pcol"], p["drow"]
        a2, a2_bias, table, na_flag, qkw = p["a2"], p["a2_bias"], p["table"], p["na_flag"], p["qkw"]

        u2, us2, ust = _inproj(x2, norm_mix_w[l][None, :], w_big, w_small, w_small_t, qkw)
        u3 = u2.reshape(B, S, U_WIDTH)
        us3 = us2.reshape(B, S, SMALL_W)
        xbc = _conv(u3, ssd_conv_w[l], ssd_conv_b[l][None, :])
        y_ssd = _ssd(xbc, u3, us3, ust, prow, pcol, drow, ssd_norm_w[l][None, :])
        y_gla = _gla(u3, us3, a2, a2_bias, gla_norm_w[l][None, :])
        y_na = _na(u3, table, na_flag)
        x2 = _merge(x2, y_ssd.reshape(T, -1), y_gla.reshape(T, -1), y_na.reshape(T, -1), u2,
                    w_branch_ssd[l].astype(BF16), w_branch_gla[l].astype(BF16),
                    w_branch_na[l].astype(BF16), w_out[l].astype(BF16))
        x2 = _mlp(x2, norm_mlp_w[l][None, :], w_ff1[l].astype(BF16), w_ff2[l].astype(BF16))
    return x2.reshape(B, S, D)
```

```python
import functools

import jax
import jax.numpy as jnp
import numpy as np
from jax import lax
from jax.experimental import pallas as pl
from jax.experimental.pallas import tpu as pltpu

F32 = jnp.float32
BF16 = jnp.bfloat16

EPS = 1e-6
D_MODEL = 1024
GRID_W = 64

SSD_HEADS = 16
SSD_HEAD_DIM = 64
SSD_D_INNER = 1024
SSD_GROUPS = 2
SSD_STATE = 128
SSD_CONV = 5
SSD_CONV_DIM = 1536
SSD_CHUNK = 128
SSD_GROUP_W = SSD_D_INNER // SSD_GROUPS
SSD_HG = SSD_HEADS // SSD_GROUPS

GLA_HEADS = 4
GLA_DK = 128
GLA_DV = 256
GLA_KEY_W = 512
GLA_VAL_W = 1024
GLA_GATE_RANK = 16
GLA_GATE_NORM = 16.0
GLA_CHUNK = 64
GLA_BLOCKS_PER_STEP = 4

NA_HEADS = 16
NA_HEAD_DIM = 64
NA_W = 1024
NA_WIN_H = 8
NA_WIN_W = 16
NA_ROWS_PER_STEP = 8
NA_ROWS_PER_STEP_EXACT = 2
LOG2E = 1.4426950408889634
NA_BOUND_SLACK = 0.02
NA_MAX_BOUND_GAP = 90.0

N_BRANCH = 3
D_FF = 4096

IN_SIZES = (SSD_D_INNER, SSD_CONV_DIM, SSD_HEADS, SSD_HEADS,
            GLA_KEY_W, GLA_KEY_W, GLA_VAL_W, GLA_VAL_W, GLA_GATE_RANK, GLA_GATE_RANK,
            NA_W, NA_W, NA_W, N_BRANCH * D_MODEL)
_IN_OFF = np.concatenate([[0], np.cumsum(IN_SIZES)])
(_O_Z, _O_XBC, _O_DTF, _O_DTB, _O_GQ, _O_GK, _O_GV, _O_GG, _O_GAF, _O_GAB,
 _O_NQ, _O_NK, _O_NV, _O_GATE) = [int(v) for v in _IN_OFF[:-1]]

U_NQ = 0
U_NK = 1024
U_GQ = 2048
U_GK = 2560
U_GATE = 3072
U_NV = 6144
U_Z = 7168
U_GV = 8192
U_GG = 9216
U_XBC = 10240
U_WIDTH = 11776
INPROJ_COL_CHUNK = 1024
LANE = 128
VMEM_LIMIT = 56 * 1024 * 1024

SMALL_W = SSD_GROUPS * LANE
SM_DTF, SM_DTB, SM_GAF, SM_GAB = 0, 8, 16, 32
SMALL_T_ROWS = 2 * SSD_HG

_BIG_SEGS = ((_O_NQ, 2048), (_O_GQ, 1024), (_O_GATE, 3072), (_O_NV, 1024), (_O_Z, 1024), (_O_GV, 2048),
             (_O_XBC, 1536))


def _small_weight(w):
    blocks = []
    for g in range(SSD_GROUPS):
        h0 = g * SSD_HG
        cols = [w[:, _O_DTF + h0:_O_DTF + h0 + SSD_HG], w[:, _O_DTB + h0:_O_DTB + h0 + SSD_HG]]
        used = 2 * SSD_HG
        if g == 0:
            cols += [w[:, _O_GAF:_O_GAF + GLA_GATE_RANK], w[:, _O_GAB:_O_GAB + GLA_GATE_RANK]]
            used += 2 * GLA_GATE_RANK
        cols.append(jnp.zeros((w.shape[0], LANE - used), w.dtype))
        blocks += cols
    return jnp.concatenate(blocks, axis=1)


def _cparams(sem, vmem=VMEM_LIMIT):
    return pltpu.CompilerParams(dimension_semantics=sem, vmem_limit_bytes=vmem)


def _sigmoid(x):
    return 1.0 / (1.0 + jnp.exp(-x))


def _silu(x):
    return x * _sigmoid(x)


def _softplus(x):
    return jnp.maximum(x, 0.0) + jnp.log1p(jnp.exp(-jnp.abs(x)))


def _log2_sigmoid(x):
    t = x * (-LOG2E)
    return -(jnp.maximum(t, 0.0) + jnp.log2(1.0 + jnp.exp2(-jnp.abs(t))))


def _nt(a, b):
    return lax.dot_general(a, b, (((1,), (1,)), ((), ())), preferred_element_type=F32)


def _tn(a, b):
    return lax.dot_general(a, b, (((0,), (0,)), ((), ())), preferred_element_type=F32)


def _dot(a, b):
    return jnp.dot(a, b, preferred_element_type=F32)


def _dot_exact(a, b):
    return jnp.dot(a, b, preferred_element_type=F32, precision=lax.Precision.HIGHEST)


def _iota2(shape, dim):
    return lax.broadcasted_iota(jnp.int32, shape, dim)


def _head_rms(r, w_row):
    er = _iota2((LANE, LANE), 0) // NA_HEAD_DIM
    ec = _iota2((LANE, LANE), 1) // NA_HEAD_DIM
    e_blk = (er == ec).astype(BF16)
    outs = []
    for a in range(0, r.shape[1], LANE):
        x = r[:, a:a + LANE]
        ms = _dot((x * x).astype(BF16), e_blk) * (1.0 / NA_HEAD_DIM)
        outs.append(x * lax.rsqrt(ms + EPS))
    return jnp.concatenate(outs, axis=1) * w_row


def _inproj_kernel(x_ref, nw_ref, w_ref, ws_ref, wst_ref, qkw_ref, u_ref, us_ref, ust_ref, h_ref):
    tn = u_ref.shape[1]
    chunks = [(a, min(a + INPROJ_COL_CHUNK, tn)) for a in range(0, tn, INPROJ_COL_CHUNK)]

    @pl.when(pl.program_id(1) == 0)
    def _():
        x = x_ref[...]
        ms = jnp.mean(x * x, axis=-1, keepdims=True)
        h = (x * lax.rsqrt(ms + EPS) * nw_ref[...]).astype(BF16)
        h_ref[...] = h
        us_ref[...] = _dot(h, ws_ref[...])
        ust_ref[...] = _nt(wst_ref[...], h)
        for c, (a, b) in enumerate(chunks):
            r = _dot(h, w_ref[:, a:b])
            if c < 2:
                r = _head_rms(r, qkw_ref[c:c + 1, :])
            u_ref[:, a:b] = r.astype(BF16)

    @pl.when(pl.program_id(1) != 0)
    def _():
        for a, b in chunks:
            u_ref[:, a:b] = _dot(h_ref[...], w_ref[:, a:b]).astype(BF16)


def _inproj(x2, nw, w_big, w_small, w_small_t, qkw, tm=1024, tn=U_WIDTH // 4):
    assert (U_NQ, U_NK) == (0, INPROJ_COL_CHUNK) and NA_W == INPROJ_COL_CHUNK
    T = x2.shape[0]
    return pl.pallas_call(
        _inproj_kernel,
        grid=(T // tm, U_WIDTH // tn),
        in_specs=[
            pl.BlockSpec((tm, D_MODEL), lambda i, j: (i, 0)),
            pl.BlockSpec((1, D_MODEL), lambda i, j: (0, 0)),
            pl.BlockSpec((D_MODEL, tn), lambda i, j: (0, j)),
            pl.BlockSpec((D_MODEL, SMALL_W), lambda i, j: (0, 0)),
            pl.BlockSpec((SSD_GROUPS * SMALL_T_ROWS, D_MODEL), lambda i, j: (0, 0)),
            pl.BlockSpec((8, NA_W), lambda i, j: (0, 0)),
        ],
        out_specs=[
            pl.BlockSpec((tm, tn), lambda i, j: (i, j)),
            pl.BlockSpec((tm, SMALL_W), lambda i, j: (i, 0)),
            pl.BlockSpec((SSD_GROUPS * SMALL_T_ROWS, tm), lambda i, j: (0, i)),
        ],
        out_shape=[
            jax.ShapeDtypeStruct((T, U_WIDTH), BF16),
            jax.ShapeDtypeStruct((T, SMALL_W), F32),
            jax.ShapeDtypeStruct((SSD_GROUPS * SMALL_T_ROWS, T), F32),
        ],
        scratch_shapes=[pltpu.VMEM((tm, D_MODEL), BF16)],
        compiler_params=_cparams(("parallel", "arbitrary")),
        name="inproj",
    )(x2, nw, w_big, w_small, w_small_t, qkw)


def _conv_kernel(u_ref, w_ref, b_ref, o_ref):
    x = u_ref[0].astype(F32)
    S = x.shape[0]
    E = 8
    pad = SSD_CONV // 2
    taps = [x if k == pad else pltpu.roll(x, (pad - k) % S, axis=0) for k in range(SSD_CONV)]
    acc = b_ref[...] + w_ref[0:1, :] * taps[0]
    for k in range(1, SSD_CONV):
        acc = acc + w_ref[k:k + 1, :] * taps[k]
    o_ref[0] = _silu(acc).astype(BF16)
    row = _iota2((E, x.shape[1]), 0)
    for lo in (0, S - E):
        edge = jnp.zeros((E, x.shape[1]), F32) + b_ref[...]
        for k in range(SSD_CONV):
            src = row + (lo + k - pad)
            edge = edge + w_ref[k:k + 1, :] * jnp.where((src >= 0) & (src < S), taps[k][lo:lo + E], 0.0)
        o_ref[0, lo:lo + E, :] = _silu(edge).astype(BF16)


def _conv(u3, conv_w, conv_b, tc=256):
    B, S, _ = u3.shape
    nblk = SSD_CONV_DIM // tc
    off = U_XBC // tc
    return pl.pallas_call(
        _conv_kernel,
        grid=(B, nblk),
        in_specs=[
            pl.BlockSpec((1, S, tc), lambda b, c: (b, 0, off + c)),
            pl.BlockSpec((SSD_CONV, tc), lambda b, c: (0, c)),
            pl.BlockSpec((1, tc), lambda b, c: (0, c)),
        ],
        out_specs=pl.BlockSpec((1, S, tc), lambda b, c: (b, 0, c)),
        out_shape=jax.ShapeDtypeStruct((B, S, SSD_CONV_DIM), BF16),
        compiler_params=_cparams(("parallel", "parallel")),
        name="ssd_conv",
    )(u3, conv_w, conv_b)


def _split_hi_lo(v):
    hi = v.astype(BF16)
    lo = (v - hi.astype(F32)).astype(BF16)
    return jnp.concatenate([hi, lo], axis=1)


def _split3(v, axis):
    hi = v.astype(BF16)
    r1 = v - hi.astype(F32)
    mid = r1.astype(BF16)
    lo = (r1 - mid.astype(F32)).astype(BF16)
    return jnp.concatenate([hi, mid, lo], axis=axis)


def _ssd_kernel(x_ref, b_ref, c_ref, z_ref, us_ref, ust_ref, prow_ref, pcol_ref, drow_ref, nw_ref,
                o_ref, acc_ref, cumc_ref, cumr_ref, wdt_ref, dec_ref, xs_ref, decx_ref, sst_ref, st_ref):
    L = SSD_CHUNK
    S = x_ref.shape[1]
    nc = S // L
    W = SSD_GROUP_W
    HG = SSD_HG
    R = 2 * HG

    ii = _iota2((L, L), 0)
    jj = _iota2((L, L), 1)
    tril = ii >= jj
    triu = jj >= ii
    tril_b = tril.astype(BF16)
    triu_b = triu.astype(BF16)

    er = _iota2((LANE, 2 * W), 0)
    ec = _iota2((LANE, 2 * W), 1)
    e = (er == jnp.where(ec < W, ec // SSD_HEAD_DIM, (ec - W) // SSD_HEAD_DIM + HG)).astype(BF16)
    e2 = jnp.concatenate([e, e], axis=0)

    bias_row = prow_ref[0:1, :]
    a_row = prow_ref[1:2, :]
    bias_col = pcol_ref[0, :, 0:1]
    a_col = pcol_ref[0, :, 1:2]

    lane_fwd = _iota2((L, LANE), 1) < HG
    row_fwd = _iota2((R, L), 0) < HG
    lane_half = _iota2((L, LANE), 1) < SSD_HEAD_DIM

    def decay_body(c, carry):
        rs = pl.ds(pl.multiple_of(c * L, L), L)
        dt_c = _softplus(us_ref[0, rs, :] + bias_row)
        a_c = dt_c * a_row
        pp = _dot(tril_b, _split3(a_c, 1))
        p_c = pp[:, 0:LANE] + pp[:, LANE:2 * LANE] + pp[:, 2 * LANE:3 * LANE]
        tot_c = p_c[L - 1:L, :]
        cum_c = jnp.where(lane_fwd, p_c, tot_c - p_c + a_c)
        cumc_ref[rs, :] = cum_c * LOG2E
        wdt_ref[rs, :] = jnp.exp(tot_c - cum_c) * dt_c
        dec_ref[c] = jnp.broadcast_to(jnp.exp(tot_c), (16, LANE))
        dt_r = _softplus(ust_ref[:, rs] + bias_col)
        a_r = dt_r * a_col
        pr = _dot(_split3(a_r, 0), triu_b)
        p_r = pr[0:R] + pr[R:2 * R] + pr[2 * R:3 * R]
        tot_r = p_r[:, L - 1:L]
        cum_r = jnp.where(row_fwd, p_r, tot_r - p_r + a_r)
        cumr_ref[:, rs] = (cum_r - jnp.log(dt_r)) * LOG2E
        return carry

    lax.fori_loop(0, nc, decay_body, 0, unroll=2)

    def intra_body(c, carry):
        rs = pl.ds(pl.multiple_of(c * L, L), L)
        x_c = x_ref[0, rs, :]
        b_c = b_ref[0, rs, :]
        cb = _nt(c_ref[0, rs, :], b_c)
        ex = _dot(_split_hi_lo(jnp.concatenate([wdt_ref[rs, :], dec_ref[c]], axis=0)), e2)
        decx_ref[c] = ex[L:L + 8]
        xf = x_c.astype(F32)
        xw = jnp.concatenate([xf * ex[0:L, 0:W], xf * ex[0:L, W:2 * W]], axis=1).astype(BF16)
        b_t = b_c.astype(F32).T.astype(BF16)
        xs_ref[c] = _dot(b_t, xw)
        cum_c = cumc_ref[rs, :]
        cum_r = cumr_ref[:, rs]
        for hp in range(HG // 2):
            ms = []
            for hh in range(2):
                hf = 2 * hp + hh
                hb = HG + 2 * hp + hh
                decf = jnp.exp2(jnp.where(tril, cum_c[:, hf:hf + 1] - cum_r[hf:hf + 1, :], -jnp.inf))
                decb = jnp.exp2(jnp.where(triu, cum_c[:, hb:hb + 1] - cum_r[hb:hb + 1, :], -jnp.inf))
                ms.append((cb * (decf + decb)).astype(BF16))
            m2 = jnp.concatenate(ms, axis=1)
            xp = x_c[:, hp * LANE:(hp + 1) * LANE]
            xz = jnp.zeros_like(xp)
            x2 = jnp.concatenate([jnp.where(lane_half, xp, xz), jnp.where(lane_half, xz, xp)], axis=0)
            acc_ref[rs, hp * LANE:(hp + 1) * LANE] = _dot(m2, x2)
        return carry

    lax.fori_loop(0, nc, intra_body, 0, unroll=2)

    st_ref[...] = jnp.zeros_like(st_ref)

    def state_body(i, carry):
        for d, c in enumerate((i, nc - 1 - i)):
            ls = slice(d * W, (d + 1) * W)
            s_prev = st_ref[:, ls]
            sst_ref[c, :, ls] = s_prev.astype(BF16)
            st_ref[:, ls] = s_prev * decx_ref[c, 0:1, ls] + xs_ref[c, :, ls]
        return carry

    lax.fori_loop(0, nc, state_body, 0)

    def out_body(c, carry):
        rs = pl.ds(pl.multiple_of(c * L, L), L)
        yo = _dot(c_ref[0, rs, :], sst_ref[c])
        sc = _dot(_split_hi_lo(jnp.exp2(cumc_ref[rs, :])), e2)
        xf = x_ref[0, rs, :].astype(F32)
        y = acc_ref[rs, :] + yo[:, 0:W] * sc[:, 0:W] + yo[:, W:2 * W] * sc[:, W:2 * W] + xf * drow_ref[...]
        y = y * _silu(z_ref[0, rs, :].astype(F32))
        ms = jnp.mean(y * y, axis=-1, keepdims=True)
        o_ref[0, rs, :] = (y * lax.rsqrt(ms + EPS) * nw_ref[...]).astype(BF16)
        return carry

    lax.fori_loop(0, nc, out_body, 0, unroll=2)


def _ssd(xbc, u3, us3, ust, prow, pcol, drow, nw):
    B, S, _ = xbc.shape
    W = SSD_GROUP_W
    N = SSD_STATE
    nc = S // SSD_CHUNK
    return pl.pallas_call(
        _ssd_kernel,
        grid=(B, SSD_GROUPS),
        in_specs=[
            pl.BlockSpec((1, S, W), lambda b, g: (b, 0, g)),
            pl.BlockSpec((1, S, N), lambda b, g: (b, 0, SSD_D_INNER // N + g)),
            pl.BlockSpec((1, S, N), lambda b, g: (b, 0, SSD_D_INNER // N + SSD_GROUPS + g)),
            pl.BlockSpec((1, S, W), lambda b, g: (b, 0, U_Z // W + g)),
            pl.BlockSpec((1, S, LANE), lambda b, g: (b, 0, g)),
            pl.BlockSpec((SMALL_T_ROWS, S), lambda b, g: (g, b)),
            pl.BlockSpec((8, LANE), lambda b, g: (g, 0)),
            pl.BlockSpec((1, SMALL_T_ROWS, 8), lambda b, g: (g, 0, 0)),
            pl.BlockSpec((1, W), lambda b, g: (0, g)),
            pl.BlockSpec((1, W), lambda b, g: (0, g)),
        ],
        out_specs=pl.BlockSpec((1, S, W), lambda b, g: (b, 0, g)),
        out_shape=jax.ShapeDtypeStruct((B, S, SSD_D_INNER), BF16),
        scratch_shapes=[
            pltpu.VMEM((S, W), F32),
            pltpu.VMEM((S, LANE), F32),
            pltpu.VMEM((SMALL_T_ROWS, S), F32),
            pltpu.VMEM((S, LANE), F32),
            pltpu.VMEM((nc, 16, LANE), F32),
            pltpu.VMEM((nc, N, 2 * W), F32),
            pltpu.VMEM((nc, 8, 2 * W), F32),
            pltpu.VMEM((nc, N, 2 * W), BF16),
            pltpu.VMEM((N, 2 * W), F32),
        ],
        compiler_params=_cparams(("parallel", "parallel")),
        name="ssd_scan",
    )(xbc, xbc, xbc, u3, us3, ust, prow, pcol, drow, nw)


def _gla_kernel(q_ref, k_ref, v_ref, gg_ref, us_ref, a2_ref, bias_ref, nw_ref,
                o_ref, acc_ref, g_ref, qd_ref, kd_ref, kdp_ref, qcat_ref, x_ref, dec_ref, sst_ref, st_ref):
    L = GLA_CHUNK
    BL = 2 * L
    DK = GLA_DK
    S = q_ref.shape[1]
    nb = S // BL
    scale = DK ** -0.5

    ii = _iota2((BL, BL), 0)
    jj = _iota2((BL, BL), 1)
    same = (ii // L) == (jj // L)
    masks = (same & (ii >= jj), same & (jj >= ii))
    tri2 = masks[0].astype(BF16)
    par_row = _iota2((BL, DK), 0) // L

    ga = us_ref[0].astype(BF16)
    g_ref[...] = _log2_sigmoid(_dot(ga, a2_ref[...]) + bias_ref[...]) * (1.0 / GLA_GATE_NORM)

    def decay_body(i, carry):
        rs = pl.ds(pl.multiple_of(i * BL, BL), BL)
        g = g_ref[rs, :]
        hi = g.astype(BF16)
        r1 = g - hi.astype(F32)
        mid = r1.astype(BF16)
        lo = (r1 - mid.astype(F32)).astype(BF16)
        pp = _dot(tri2, jnp.concatenate([hi, mid, lo], axis=1))
        p = pp[:, 0:2 * DK] + pp[:, 2 * DK:4 * DK] + pp[:, 4 * DK:6 * DK]
        q_c = q_ref[0, rs, :].astype(F32) * scale
        k_c = k_ref[0, rs, :].astype(F32)
        zero = jnp.zeros((BL, DK), BF16)
        for d in range(2):
            p_d = p[:, d * DK:(d + 1) * DK]
            tot = jnp.where(par_row == 0, p_d[L - 1:L, :], p_d[BL - 1:BL, :])
            b = p_d if d == 0 else tot - p_d + g[:, DK:]
            qd = (q_c * jnp.exp2(b)).astype(BF16)
            kdec = (k_c * jnp.exp2(tot - b)).astype(BF16)
            qd_ref[d, rs, :] = qd
            kd_ref[d, rs, :] = (k_c * jnp.exp2(-b)).astype(BF16)
            for par in range(2):
                sel = par_row == par
                kdp_ref[d, rs, par * DK:(par + 1) * DK] = jnp.where(sel, kdec, zero)
                qcat_ref[rs, (2 * d + par) * DK:(2 * d + par + 1) * DK] = jnp.where(sel, qd, zero)
                last = (par + 1) * L - 1
                dec_ref[d, 2 * i + par] = jnp.broadcast_to(jnp.exp2(p_d[last:last + 1, :]), (8, DK))
        return carry

    lax.fori_loop(0, nb, decay_body, 0, unroll=2)

    G = GLA_BLOCKS_PER_STEP

    def intra_body(i, carry):
        blks = [G * i + j for j in range(G)]
        rss = [pl.ds(pl.multiple_of(b * BL, BL), BL) for b in blks]
        vs = [v_ref[0, rs, :] for rs in rss]
        atts = [[_nt(qd_ref[d, rs, :], kd_ref[d, rs, :]) for d in range(2)] for rs in rss]
        for j, b in enumerate(blks):
            for d in range(2):
                x_ref[d, b] = _tn(vs[j], kdp_ref[d, rss[j], :])
        for j in range(G):
            att = jnp.where(masks[0], atts[j][0], 0.0) + jnp.where(masks[1], atts[j][1], 0.0)
            acc_ref[rss[j], :] = _dot(att.astype(BF16), vs[j])
        return carry

    lax.fori_loop(0, nb // G, intra_body, 0)

    st_ref[...] = jnp.zeros_like(st_ref)

    def state_body(i, carry):
        for d, b in enumerate((i, nb - 1 - i)):
            for par in ((0, 1) if d == 0 else (1, 0)):
                s_prev = st_ref[d]
                lane0 = (2 * d + par) * DK
                sst_ref[b, :, lane0:lane0 + DK] = s_prev.astype(BF16)
                st_ref[d] = s_prev * dec_ref[d, 2 * b + par, 0:1, :] + x_ref[d, b, :, par * DK:(par + 1) * DK]
        return carry

    lax.fori_loop(0, nb, state_body, 0)

    def out_body(b, carry):
        rs = pl.ds(pl.multiple_of(b * BL, BL), BL)
        o = acc_ref[rs, :] + _nt(qcat_ref[rs, :], sst_ref[b])
        ms = jnp.mean(o * o, axis=-1, keepdims=True)
        o = o * lax.rsqrt(ms + EPS) * nw_ref[...]
        o_ref[0, rs, :] = (o * _silu(gg_ref[0, rs, :].astype(F32))).astype(BF16)
        return carry

    lax.fori_loop(0, nb, out_body, 0, unroll=GLA_BLOCKS_PER_STEP)


def _gla(u3, us3, a2, bias, nw):
    B, S, _ = u3.shape
    DK, DV = GLA_DK, GLA_DV
    nc = S // GLA_CHUNK
    nb = nc // 2
    return pl.pallas_call(
        _gla_kernel,
        grid=(B, GLA_HEADS),
        in_specs=[
            pl.BlockSpec((1, S, DK), lambda b, h: (b, 0, U_GQ // DK + h)),
            pl.BlockSpec((1, S, DK), lambda b, h: (b, 0, U_GK // DK + h)),
            pl.BlockSpec((1, S, DV), lambda b, h: (b, 0, U_GV // DV + h)),
            pl.BlockSpec((1, S, DV), lambda b, h: (b, 0, U_GG // DV + h)),
            pl.BlockSpec((1, S, LANE), lambda b, h: (b, 0, 0)),
            pl.BlockSpec((LANE, 2 * DK), lambda b, h: (0, h)),
            pl.BlockSpec((1, 2 * DK), lambda b, h: (0, h)),
            pl.BlockSpec((1, DV), lambda b, h: (0, 0)),
        ],
        out_specs=pl.BlockSpec((1, S, DV), lambda b, h: (b, 0, h)),
        out_shape=jax.ShapeDtypeStruct((B, S, GLA_VAL_W), BF16),
        scratch_shapes=[
            pltpu.VMEM((S, DV), F32),
            pltpu.VMEM((S, 2 * DK), F32),
            pltpu.VMEM((2, S, DK), BF16),
            pltpu.VMEM((2, S, DK), BF16),
            pltpu.VMEM((2, S, 2 * DK), BF16),
            pltpu.VMEM((S, 4 * DK), BF16),
            pltpu.VMEM((2, nb, DV, 2 * DK), F32),
            pltpu.VMEM((2, nc, 8, DK), F32),
            pltpu.VMEM((nb, DV, 4 * DK), BF16),
            pltpu.VMEM((2, DV, DK), F32),
        ],
        compiler_params=_cparams(("parallel", "parallel")),
        name="gla_scan",
    )(u3, u3, u3, u3, us3, a2, bias, nw)


def _na_col_tables():
    qcol = np.arange(GRID_W)[:, None]
    kcol = np.arange(GRID_W)[None, :]
    w_start = np.clip(qcol - NA_WIN_W // 2, 0, GRID_W - NA_WIN_W)
    mask = (kcol >= w_start) & (kcol < w_start + NA_WIN_W)
    off = np.clip(kcol - qcol, -(NA_WIN_W - 1), NA_WIN_W - 1) + (NA_WIN_W - 1)
    return mask, off


def _na_bias_table(rpb):
    mask, off = _na_col_tables()
    H = rpb.shape[0]
    padw = GRID_W - NA_WIN_W
    ext = jnp.pad(rpb, ((0, 0), (0, 0), (padw, padw)), mode="edge")
    R = rpb.shape[1]
    n2 = 2 * GRID_W
    tiled = jnp.broadcast_to(jnp.pad(ext, ((0, 0), (0, 0), (0, 1)))[:, :, None, :], (H, R, GRID_W, n2))
    view = tiled.reshape(H, R, GRID_W * n2)[:, :, :GRID_W * (n2 - 1)].reshape(H, R, GRID_W, n2 - 1)
    a = view[:, :, :, GRID_W - 1:]
    a = jnp.where(mask[None, None], a, -jnp.inf)
    a = jnp.transpose(a.reshape(H // 2, 2, R, GRID_W, GRID_W), (0, 2, 1, 3, 4)).reshape(H // 2, R, 2 * GRID_W, GRID_W)
    return jnp.concatenate([a[:, :-1], a[:, 1:]], axis=-1).astype(F32)


def _na_kernel(flag_ref, q_ref, k_ref, v_ref, tab_ref, o_ref, vx_ref):
    S = q_ref.shape[1]
    rows = S // GRID_W
    win_h = NA_WIN_H
    nk = win_h * GRID_W

    vx_ref[:, 0:LANE] = v_ref[0]
    vx_ref[:, LANE:2 * LANE] = jnp.ones((S, LANE), BF16)
    bound_ok = flag_ref[0] != 0

    first_q = _iota2((GRID_W, LANE), 1) < NA_HEAD_DIM

    def key_start(r):
        r0 = jnp.clip(r - win_h // 2, 0, rows - win_h)
        return r0, pl.multiple_of(r0 * GRID_W, GRID_W)

    def scores(r):
        r0, k0 = key_start(r)
        q = q_ref[0, pl.ds(pl.multiple_of(r * GRID_W, GRID_W), GRID_W), :]
        zero = jnp.zeros_like(q)
        qs = jnp.concatenate([jnp.where(first_q, q, zero), jnp.where(first_q, zero, q)], axis=0)
        rr0 = (win_h - 1) - (r - r0)
        bias = jnp.concatenate([tab_ref[0, rr0 + w] for w in range(0, win_h, 2)], axis=1)
        return _nt(qs, k_ref[0, pl.ds(k0, nk), :]) + bias

    def probs_exact(s):
        return jnp.exp2(s - jnp.max(s, axis=-1, keepdims=True)).astype(BF16)

    def probs_bounded(r):
        return jnp.exp2(scores(r)).astype(BF16)

    def attend(r, p):
        _, k0 = key_start(r)
        ox = _dot(p, vx_ref[pl.ds(k0, nk), :])
        o = ox[:, 0:LANE] / ox[:, LANE:2 * LANE]
        q0 = pl.multiple_of(r * GRID_W, GRID_W)
        o_ref[0, pl.ds(q0, GRID_W), :] = jnp.where(first_q, o[:GRID_W], o[GRID_W:]).astype(BF16)

    @pl.when(bound_ok)
    def _():
        U = NA_ROWS_PER_STEP

        def row_body(i, p_prev):
            r = i * U
            for j in range(U):
                attend(r - U + j, p_prev[j])
            return tuple(probs_bounded(r + j) for j in range(U))

        p_last = lax.fori_loop(1, rows // U, row_body, tuple(probs_bounded(j) for j in range(U)))
        for j in range(U):
            attend(rows - U + j, p_last[j])

    @pl.when(jnp.logical_not(bound_ok))
    def _():
        U = NA_ROWS_PER_STEP_EXACT

        def row_body(i, carry):
            s_cur, p_prev = carry
            r = i * U
            for j in range(U):
                attend(r - U + j, p_prev[j])
            p = tuple(probs_exact(s) for s in s_cur)
            s_next = tuple(scores(jnp.minimum(r + U + j, rows - 1)) for j in range(U))
            return s_next, p

        p0 = tuple(probs_exact(scores(j)) for j in range(U))
        s1 = tuple(scores(U + j) for j in range(U))
        _, p_last = lax.fori_loop(1, rows // U, row_body, (s1, p0))
        for j in range(U):
            attend(rows - U + j, p_last[j])


def _na_score_bound(rpb, q_norm_w, k_norm_w):
    H = rpb.shape[0]
    qk = (NA_HEAD_DIM ** 0.5 * LOG2E * (1.0 + NA_BOUND_SLACK)) * jnp.max(jnp.abs(q_norm_w)) * jnp.max(jnp.abs(k_norm_w))
    b_max = jnp.max(rpb.reshape(H, -1), axis=1) * LOG2E
    b_self = rpb[:, NA_WIN_H - 1, NA_WIN_W - 1] * LOG2E
    bound = qk + b_max
    flag = jnp.all(bound - (b_self - qk) <= NA_MAX_BOUND_GAP)
    return bound, flag.astype(jnp.int32).reshape(1)


def _na(u3, table, flag):
    B, S, _ = u3.shape
    assert S // GRID_W >= NA_WIN_H and NA_WIN_H % 2 == 0
    return pl.pallas_call(
        _na_kernel,
        grid=(NA_HEADS // 2, B),
        in_specs=[
            pl.BlockSpec(memory_space=pltpu.SMEM),
            pl.BlockSpec((1, S, LANE), lambda h, b: (b, 0, U_NQ // LANE + h)),
            pl.BlockSpec((1, S, LANE), lambda h, b: (b, 0, U_NK // LANE + h)),
            pl.BlockSpec((1, S, LANE), lambda h, b: (b, 0, U_NV // LANE + h)),
            pl.BlockSpec((1, 2 * NA_WIN_H - 2, 2 * GRID_W, 2 * GRID_W), lambda h, b: (h, 0, 0, 0)),
        ],
        out_specs=pl.BlockSpec((1, S, LANE), lambda h, b: (b, 0, h)),
        out_shape=jax.ShapeDtypeStruct((B, S, NA_W), BF16),
        scratch_shapes=[pltpu.VMEM((S, 2 * LANE), BF16)],
        compiler_params=_cparams(("parallel", "parallel")),
        name="na_attn",
    )(flag, u3, u3, u3, table)


def _merge_kernel(x_ref, ys_ref, yg_ref, yn_ref, gate_ref, ws_ref, wg_ref, wn_ref, wo_ref, o_ref):
    D = D_MODEL
    mixed = _sigmoid(gate_ref[:, 0:D].astype(F32)) * _dot(ys_ref[...], ws_ref[...])
    mixed += _sigmoid(gate_ref[:, D:2 * D].astype(F32)) * _dot(yg_ref[...], wg_ref[...])
    mixed += _sigmoid(gate_ref[:, 2 * D:3 * D].astype(F32)) * _dot(yn_ref[...], wn_ref[...])
    o_ref[...] = x_ref[...] + _dot(mixed.astype(BF16), wo_ref[...])


def _merge(x2, ys, yg, yn, u2, ws, wg, wn, wo, tm=512):
    T = x2.shape[0]
    D = D_MODEL
    row = lambda i: (i, 0)
    fixed = lambda i: (0, 0)
    return pl.pallas_call(
        _merge_kernel,
        grid=(T // tm,),
        in_specs=[
            pl.BlockSpec((tm, D), row),
            pl.BlockSpec((tm, D), row),
            pl.BlockSpec((tm, D), row),
            pl.BlockSpec((tm, D), row),
            pl.BlockSpec((tm, N_BRANCH * D), lambda i: (i, U_GATE // (N_BRANCH * D))),
            pl.BlockSpec((D, D), fixed),
            pl.BlockSpec((D, D), fixed),
            pl.BlockSpec((D, D), fixed),
            pl.BlockSpec((D, D), fixed),
        ],
        out_specs=pl.BlockSpec((tm, D), row),
        out_shape=jax.ShapeDtypeStruct((T, D), F32),
        compiler_params=_cparams(("parallel",)),
        name="merge",
    )(x2, ys, yg, yn, u2, ws, wg, wn, wo)


def _mlp_kernel(x_ref, nw_ref, w1_ref, w2_ref, o_ref, *, tf):
    x = x_ref[...]
    ms = jnp.mean(x * x, axis=-1, keepdims=True)
    h = (x * lax.rsqrt(ms + EPS) * nw_ref[...]).astype(BF16)
    acc = x
    for f in range(D_FF // tf):
        a = jnp.maximum(_dot(h, w1_ref[:, f * tf:(f + 1) * tf]), 0.0)
        acc = acc + _dot((a * a).astype(BF16), w2_ref[f * tf:(f + 1) * tf, :])
    o_ref[...] = acc


def _mlp(x2, nw, w1, w2, tm=512, tf=1024):
    T = x2.shape[0]
    D = D_MODEL
    return pl.pallas_call(
        functools.partial(_mlp_kernel, tf=tf),
        grid=(T // tm,),
        in_specs=[
            pl.BlockSpec((tm, D), lambda i: (i, 0)),
            pl.BlockSpec((1, D), lambda i: (0, 0)),
            pl.BlockSpec((D, D_FF), lambda i: (0, 0)),
            pl.BlockSpec((D_FF, D), lambda i: (0, 0)),
        ],
        out_specs=pl.BlockSpec((tm, D), lambda i: (i, 0)),
        out_shape=jax.ShapeDtypeStruct((T, D), F32),
        compiler_params=_cparams(("parallel",)),
        name="mlp",
    )(x2, nw, w1, w2)


def _pad_rows(w, start, total):
    return jnp.zeros((total, w.shape[1]), w.dtype).at[start:start + w.shape[0]].set(w)


def kernel(x, norm_mix_w, w_in, ssd_conv_w, ssd_conv_b, ssd_dt_bias_f, ssd_dt_bias_b, ssd_a_log_f,
           ssd_a_log_b, ssd_d, ssd_norm_w, gla_a2_f, gla_a2_bias_f, gla_a2_b, gla_a2_bias_b,
           gla_norm_w, na_q_norm_w, na_k_norm_w, na_rpb, w_branch_ssd, w_branch_gla, w_branch_na,
           w_out, norm_mlp_w, w_ff1, w_ff2):
    B, S, D = x.shape
    T = B * S
    depth = w_in.shape[0]
    rows = S // GRID_W
    x2 = x.reshape(T, D)
    for l in range(depth):
        w_big = jnp.concatenate([w_in[l][:, a:a + n] for a, n in _BIG_SEGS], axis=1).astype(BF16)
        w_small = _small_weight(w_in[l]).astype(BF16)
        hg = (SSD_GROUPS, SSD_HG)
        zeros_r = jnp.zeros((SSD_GROUPS, LANE - 2 * SSD_HG), F32)
        dt_bias = jnp.concatenate([ssd_dt_bias_f[l].reshape(hg), ssd_dt_bias_b[l].reshape(hg), zeros_r], axis=1)
        a_neg = jnp.concatenate([-jnp.exp(ssd_a_log_f[l]).reshape(hg), -jnp.exp(ssd_a_log_b[l]).reshape(hg),
                                 zeros_r], axis=1)
        prow3 = jnp.concatenate([dt_bias[:, None], a_neg[:, None], jnp.zeros((SSD_GROUPS, 6, LANE), F32)], axis=1)
        prow = prow3.reshape(SSD_GROUPS * 8, LANE)
        pcol = jnp.transpose(prow3[:, :, :SMALL_T_ROWS], (0, 2, 1))
        w_small_t = w_small.T.reshape(SSD_GROUPS, LANE, D)[:, :SMALL_T_ROWS].reshape(-1, D)
        drow = jnp.repeat(ssd_d[l], SSD_HEAD_DIM)[None, :]
        hk = (GLA_HEADS, 1, GLA_DK)
        a2 = jnp.concatenate([_pad_rows(gla_a2_f[l], SM_GAF, LANE).reshape((LANE,) + hk),
                              _pad_rows(gla_a2_b[l], SM_GAB, LANE).reshape((LANE,) + hk)],
                             axis=2).reshape(LANE, -1).astype(BF16)
        a2_bias = jnp.concatenate([gla_a2_bias_f[l].reshape(hk), gla_a2_bias_b[l].reshape(hk)],
                                  axis=1).reshape(1, -1)
        na_bound, na_flag = _na_score_bound(na_rpb[l], na_q_norm_w[l], na_k_norm_w[l])
        table = _na_bias_table(na_rpb[l] * LOG2E - na_bound[:, None, None])
        q_row = jnp.tile(na_q_norm_w[l] * (NA_HEAD_DIM ** -0.5 * LOG2E), NA_HEADS)
        k_row = jnp.tile(na_k_norm_w[l], NA_HEADS)
        qkw = jnp.concatenate([q_row[None], k_row[None], jnp.zeros((6, NA_W), F32)], axis=0)

        u2, us2, ust = _inproj(x2, norm_mix_w[l][None, :], w_big, w_small, w_small_t, qkw)
        u3 = u2.reshape(B, S, U_WIDTH)
        us3 = us2.reshape(B, S, SMALL_W)
        xbc = _conv(u3, ssd_conv_w[l], ssd_conv_b[l][None, :])
        y_ssd = _ssd(xbc, u3, us3, ust, prow, pcol, drow, ssd_norm_w[l][None, :])
        y_gla = _gla(u3, us3, a2, a2_bias, gla_norm_w[l][None, :])
        y_na = _na(u3, table, na_flag)
        x2 = _merge(x2, y_ssd.reshape(T, -1), y_gla.reshape(T, -1), y_na.reshape(T, -1), u2,
                    w_branch_ssd[l].astype(BF16), w_branch_gla[l].astype(BF16),
                    w_branch_na[l].astype(BF16), w_out[l].astype(BF16))
        x2 = _mlp(x2, norm_mlp_w[l][None, :], w_ff1[l].astype(BF16), w_ff2[l].astype(BF16))
    return x2.reshape(B, S, D)
```

```python
import functools

import jax
import jax.numpy as jnp
import numpy as np
from jax import lax
from jax.experimental import pallas as pl
from jax.experimental.pallas import tpu as pltpu

F32 = jnp.float32
BF16 = jnp.bfloat16

EPS = 1e-6
D_MODEL = 1024
GRID_W = 64

SSD_HEADS = 16
SSD_HEAD_DIM = 64
SSD_D_INNER = 1024
SSD_GROUPS = 2
SSD_STATE = 128
SSD_CONV = 5
SSD_CONV_DIM = 1536
SSD_CHUNK = 128
SSD_GROUP_W = SSD_D_INNER // SSD_GROUPS
SSD_HG = SSD_HEADS // SSD_GROUPS

GLA_HEADS = 4
GLA_DK = 128
GLA_DV = 256
GLA_KEY_W = 512
GLA_VAL_W = 1024
GLA_GATE_RANK = 16
GLA_GATE_NORM = 16.0
GLA_CHUNK = 64
GLA_BLOCKS_PER_STEP = 4

NA_HEADS = 16
NA_HEAD_DIM = 64
NA_W = 1024
NA_WIN_H = 8
NA_WIN_W = 16
NA_ROWS_PER_STEP = 8
NA_ROWS_PER_STEP_EXACT = 2
LOG2E = 1.4426950408889634
NA_BOUND_SLACK = 0.02
NA_MAX_BOUND_GAP = 90.0

N_BRANCH = 3
D_FF = 4096

IN_SIZES = (SSD_D_INNER, SSD_CONV_DIM, SSD_HEADS, SSD_HEADS,
            GLA_KEY_W, GLA_KEY_W, GLA_VAL_W, GLA_VAL_W, GLA_GATE_RANK, GLA_GATE_RANK,
            NA_W, NA_W, NA_W, N_BRANCH * D_MODEL)
_IN_OFF = np.concatenate([[0], np.cumsum(IN_SIZES)])
(_O_Z, _O_XBC, _O_DTF, _O_DTB, _O_GQ, _O_GK, _O_GV, _O_GG, _O_GAF, _O_GAB,
 _O_NQ, _O_NK, _O_NV, _O_GATE) = [int(v) for v in _IN_OFF[:-1]]

U_NQ = 0
U_NK = 1024
U_NV = 2048
U_GATE = 3072
U_Z = 6144
U_XBC = 7168
U_GQ = 8704
U_GK = 9216
U_GV = 9728
U_GG = 10752
U_WIDTH = 11776
INPROJ_COL_CHUNK = 1024
LANE = 128
VMEM_LIMIT = 56 * 1024 * 1024

SMALL_W = SSD_GROUPS * LANE
SM_DTF, SM_DTB, SM_GAF, SM_GAB = 0, 8, 16, 32
SMALL_T_ROWS = 2 * SSD_HG

_BIG_SEGS = ((_O_NQ, 6144), (_O_Z, 2560), (_O_GQ, 3072))


def _small_weight(w):
    blocks = []
    for g in range(SSD_GROUPS):
        h0 = g * SSD_HG
        cols = [w[:, _O_DTF + h0:_O_DTF + h0 + SSD_HG], w[:, _O_DTB + h0:_O_DTB + h0 + SSD_HG]]
        used = 2 * SSD_HG
        if g == 0:
            cols += [w[:, _O_GAF:_O_GAF + GLA_GATE_RANK], w[:, _O_GAB:_O_GAB + GLA_GATE_RANK]]
            used += 2 * GLA_GATE_RANK
        cols.append(jnp.zeros((w.shape[0], LANE - used), w.dtype))
        blocks += cols
    return jnp.concatenate(blocks, axis=1)


def _cparams(sem, vmem=VMEM_LIMIT):
    return pltpu.CompilerParams(dimension_semantics=sem, vmem_limit_bytes=vmem)


def _sigmoid(x):
    return 1.0 / (1.0 + jnp.exp(-x))


def _silu(x):
    return x * _sigmoid(x)


def _softplus(x):
    return jnp.maximum(x, 0.0) + jnp.log1p(jnp.exp(-jnp.abs(x)))


def _log2_sigmoid(x):
    t = x * (-LOG2E)
    return -(jnp.maximum(t, 0.0) + jnp.log2(1.0 + jnp.exp2(-jnp.abs(t))))


def _nt(a, b):
    return lax.dot_general(a, b, (((1,), (1,)), ((), ())), preferred_element_type=F32)


def _tn(a, b):
    return lax.dot_general(a, b, (((0,), (0,)), ((), ())), preferred_element_type=F32)


def _dot(a, b):
    return jnp.dot(a, b, preferred_element_type=F32)


def _dot_exact(a, b):
    return jnp.dot(a, b, preferred_element_type=F32, precision=lax.Precision.HIGHEST)


def _iota2(shape, dim):
    return lax.broadcasted_iota(jnp.int32, shape, dim)


def _head_rms(r, w_row):
    er = _iota2((LANE, LANE), 0) // NA_HEAD_DIM
    ec = _iota2((LANE, LANE), 1) // NA_HEAD_DIM
    e_blk = (er == ec).astype(BF16)
    outs = []
    for a in range(0, r.shape[1], LANE):
        x = r[:, a:a + LANE]
        ms = _dot((x * x).astype(BF16), e_blk) * (1.0 / NA_HEAD_DIM)
        outs.append(x * lax.rsqrt(ms + EPS))
    return jnp.concatenate(outs, axis=1) * w_row


def _inproj_kernel(x_ref, nw_ref, w_ref, ws_ref, wst_ref, qkw_ref, u_ref, us_ref, ust_ref, h_ref):
    tn = u_ref.shape[1]
    chunks = [(a, min(a + INPROJ_COL_CHUNK, tn)) for a in range(0, tn, INPROJ_COL_CHUNK)]

    @pl.when(pl.program_id(1) == 0)
    def _():
        x = x_ref[...]
        ms = jnp.mean(x * x, axis=-1, keepdims=True)
        h = (x * lax.rsqrt(ms + EPS) * nw_ref[...]).astype(BF16)
        h_ref[...] = h
        us_ref[...] = _dot(h, ws_ref[...])
        ust_ref[...] = _nt(wst_ref[...], h)
        for c, (a, b) in enumerate(chunks):
            r = _dot(h, w_ref[:, a:b])
            if c < 2:
                r = _head_rms(r, qkw_ref[c:c + 1, :])
            u_ref[:, a:b] = r.astype(BF16)

    @pl.when(pl.program_id(1) != 0)
    def _():
        for a, b in chunks:
            u_ref[:, a:b] = _dot(h_ref[...], w_ref[:, a:b]).astype(BF16)


def _inproj(x2, nw, w_big, w_small, w_small_t, qkw, tm=1024, tn=U_WIDTH // 4):
    assert (U_NQ, U_NK) == (0, INPROJ_COL_CHUNK) and NA_W == INPROJ_COL_CHUNK
    T = x2.shape[0]
    return pl.pallas_call(
        _inproj_kernel,
        grid=(T // tm, U_WIDTH // tn),
        in_specs=[
            pl.BlockSpec((tm, D_MODEL), lambda i, j: (i, 0)),
            pl.BlockSpec((1, D_MODEL), lambda i, j: (0, 0)),
            pl.BlockSpec((D_MODEL, tn), lambda i, j: (0, j)),
            pl.BlockSpec((D_MODEL, SMALL_W), lambda i, j: (0, 0)),
            pl.BlockSpec((SSD_GROUPS * SMALL_T_ROWS, D_MODEL), lambda i, j: (0, 0)),
            pl.BlockSpec((8, NA_W), lambda i, j: (0, 0)),
        ],
        out_specs=[
            pl.BlockSpec((tm, tn), lambda i, j: (i, j)),
            pl.BlockSpec((tm, SMALL_W), lambda i, j: (i, 0)),
            pl.BlockSpec((SSD_GROUPS * SMALL_T_ROWS, tm), lambda i, j: (0, i)),
        ],
        out_shape=[
            jax.ShapeDtypeStruct((T, U_WIDTH), BF16),
            jax.ShapeDtypeStruct((T, SMALL_W), F32),
            jax.ShapeDtypeStruct((SSD_GROUPS * SMALL_T_ROWS, T), F32),
        ],
        scratch_shapes=[pltpu.VMEM((tm, D_MODEL), BF16)],
        compiler_params=_cparams(("parallel", "arbitrary")),
        name="inproj",
    )(x2, nw, w_big, w_small, w_small_t, qkw)


def _conv_kernel(u_ref, w_ref, b_ref, o_ref):
    x = u_ref[0].astype(F32)
    S = x.shape[0]
    row = _iota2(x.shape, 0)
    acc = jnp.zeros_like(x) + b_ref[...]
    pad = SSD_CONV // 2
    for k in range(SSD_CONV):
        d = k - pad
        if d == 0:
            xs = x
        else:
            xs = pltpu.roll(x, (-d) % S, axis=0)
            valid = (row + d >= 0) & (row + d < S)
            xs = jnp.where(valid, xs, 0.0)
        acc = acc + w_ref[k:k + 1, :] * xs
    o_ref[0] = _silu(acc).astype(BF16)


def _conv(u3, conv_w, conv_b, tc=256):
    B, S, _ = u3.shape
    nblk = SSD_CONV_DIM // tc
    off = U_XBC // tc
    return pl.pallas_call(
        _conv_kernel,
        grid=(B, nblk),
        in_specs=[
            pl.BlockSpec((1, S, tc), lambda b, c: (b, 0, off + c)),
            pl.BlockSpec((SSD_CONV, tc), lambda b, c: (0, c)),
            pl.BlockSpec((1, tc), lambda b, c: (0, c)),
        ],
        out_specs=pl.BlockSpec((1, S, tc), lambda b, c: (b, 0, c)),
        out_shape=jax.ShapeDtypeStruct((B, S, SSD_CONV_DIM), BF16),
        compiler_params=_cparams(("parallel", "parallel")),
        name="ssd_conv",
    )(u3, conv_w, conv_b)


def _split_hi_lo(v):
    hi = v.astype(BF16)
    lo = (v - hi.astype(F32)).astype(BF16)
    return jnp.concatenate([hi, lo], axis=1)


def _split3(v, axis):
    hi = v.astype(BF16)
    r1 = v - hi.astype(F32)
    mid = r1.astype(BF16)
    lo = (r1 - mid.astype(F32)).astype(BF16)
    return jnp.concatenate([hi, mid, lo], axis=axis)


def _ssd_kernel(x_ref, b_ref, c_ref, z_ref, us_ref, ust_ref, prow_ref, pcol_ref, drow_ref, nw_ref,
                o_ref, acc_ref, cumc_ref, cumr_ref, wdt_ref, dec_ref, xs_ref, decx_ref, sst_ref, st_ref):
    L = SSD_CHUNK
    S = x_ref.shape[1]
    nc = S // L
    W = SSD_GROUP_W
    HG = SSD_HG
    R = 2 * HG

    ii = _iota2((L, L), 0)
    jj = _iota2((L, L), 1)
    tril = ii >= jj
    triu = jj >= ii
    tril_b = tril.astype(BF16)
    triu_b = triu.astype(BF16)

    er = _iota2((LANE, 2 * W), 0)
    ec = _iota2((LANE, 2 * W), 1)
    e = (er == jnp.where(ec < W, ec // SSD_HEAD_DIM, (ec - W) // SSD_HEAD_DIM + HG)).astype(BF16)
    e2 = jnp.concatenate([e, e], axis=0)

    bias_row = prow_ref[0:1, :]
    a_row = prow_ref[1:2, :]
    bias_col = pcol_ref[0, :, 0:1]
    a_col = pcol_ref[0, :, 1:2]

    lane_fwd = _iota2((L, LANE), 1) < HG
    row_fwd = _iota2((R, L), 0) < HG
    lane_half = _iota2((L, LANE), 1) < SSD_HEAD_DIM

    def decay_body(c, carry):
        rs = pl.ds(pl.multiple_of(c * L, L), L)
        dt_c = _softplus(us_ref[0, rs, :] + bias_row)
        a_c = dt_c * a_row
        pp = _dot(tril_b, _split3(a_c, 1))
        p_c = pp[:, 0:LANE] + pp[:, LANE:2 * LANE] + pp[:, 2 * LANE:3 * LANE]
        tot_c = p_c[L - 1:L, :]
        cum_c = jnp.where(lane_fwd, p_c, tot_c - p_c + a_c)
        cumc_ref[rs, :] = cum_c * LOG2E
        wdt_ref[rs, :] = jnp.exp(tot_c - cum_c) * dt_c
        dec_ref[c] = jnp.broadcast_to(jnp.exp(tot_c), (16, LANE))
        dt_r = _softplus(ust_ref[:, rs] + bias_col)
        a_r = dt_r * a_col
        pr = _dot(_split3(a_r, 0), triu_b)
        p_r = pr[0:R] + pr[R:2 * R] + pr[2 * R:3 * R]
        tot_r = p_r[:, L - 1:L]
        cum_r = jnp.where(row_fwd, p_r, tot_r - p_r + a_r)
        cumr_ref[:, rs] = (cum_r - jnp.log(dt_r)) * LOG2E
        return carry

    lax.fori_loop(0, nc, decay_body, 0, unroll=2)

    def intra_body(c, carry):
        rs = pl.ds(pl.multiple_of(c * L, L), L)
        x_c = x_ref[0, rs, :]
        b_c = b_ref[0, rs, :]
        cb = _nt(c_ref[0, rs, :], b_c)
        ex = _dot(_split_hi_lo(jnp.concatenate([wdt_ref[rs, :], dec_ref[c]], axis=0)), e2)
        decx_ref[c] = ex[L:L + 8]
        xf = x_c.astype(F32)
        xw = jnp.concatenate([xf * ex[0:L, 0:W], xf * ex[0:L, W:2 * W]], axis=1).astype(BF16)
        b_t = b_c.astype(F32).T.astype(BF16)
        xs_ref[c] = _dot(b_t, xw)
        cum_c = cumc_ref[rs, :]
        cum_r = cumr_ref[:, rs]
        for hp in range(HG // 2):
            ms = []
            for hh in range(2):
                hf = 2 * hp + hh
                hb = HG + 2 * hp + hh
                decf = jnp.exp2(jnp.where(tril, cum_c[:, hf:hf + 1] - cum_r[hf:hf + 1, :], -jnp.inf))
                decb = jnp.exp2(jnp.where(triu, cum_c[:, hb:hb + 1] - cum_r[hb:hb + 1, :], -jnp.inf))
                ms.append((cb * (decf + decb)).astype(BF16))
            m2 = jnp.concatenate(ms, axis=1)
            xp = x_c[:, hp * LANE:(hp + 1) * LANE]
            xz = jnp.zeros_like(xp)
            x2 = jnp.concatenate([jnp.where(lane_half, xp, xz), jnp.where(lane_half, xz, xp)], axis=0)
            acc_ref[rs, hp * LANE:(hp + 1) * LANE] = _dot(m2, x2)
        return carry

    lax.fori_loop(0, nc, intra_body, 0, unroll=2)

    st_ref[...] = jnp.zeros_like(st_ref)

    def state_body(i, carry):
        for d, c in enumerate((i, nc - 1 - i)):
            ls = slice(d * W, (d + 1) * W)
            s_prev = st_ref[:, ls]
            sst_ref[c, :, ls] = s_prev.astype(BF16)
            st_ref[:, ls] = s_prev * decx_ref[c, 0:1, ls] + xs_ref[c, :, ls]
        return carry

    lax.fori_loop(0, nc, state_body, 0)

    def out_body(c, carry):
        rs = pl.ds(pl.multiple_of(c * L, L), L)
        yo = _dot(c_ref[0, rs, :], sst_ref[c])
        sc = _dot(_split_hi_lo(jnp.exp2(cumc_ref[rs, :])), e2)
        xf = x_ref[0, rs, :].astype(F32)
        y = acc_ref[rs, :] + yo[:, 0:W] * sc[:, 0:W] + yo[:, W:2 * W] * sc[:, W:2 * W] + xf * drow_ref[...]
        y = y * _silu(z_ref[0, rs, :].astype(F32))
        ms = jnp.mean(y * y, axis=-1, keepdims=True)
        o_ref[0, rs, :] = (y * lax.rsqrt(ms + EPS) * nw_ref[...]).astype(BF16)
        return carry

    lax.fori_loop(0, nc, out_body, 0, unroll=2)


def _ssd(xbc, u3, us3, ust, prow, pcol, drow, nw):
    B, S, _ = xbc.shape
    W = SSD_GROUP_W
    N = SSD_STATE
    nc = S // SSD_CHUNK
    return pl.pallas_call(
        _ssd_kernel,
        grid=(B, SSD_GROUPS),
        in_specs=[
            pl.BlockSpec((1, S, W), lambda b, g: (b, 0, g)),
            pl.BlockSpec((1, S, N), lambda b, g: (b, 0, SSD_D_INNER // N + g)),
            pl.BlockSpec((1, S, N), lambda b, g: (b, 0, SSD_D_INNER // N + SSD_GROUPS + g)),
            pl.BlockSpec((1, S, W), lambda b, g: (b, 0, U_Z // W + g)),
            pl.BlockSpec((1, S, LANE), lambda b, g: (b, 0, g)),
            pl.BlockSpec((SMALL_T_ROWS, S), lambda b, g: (g, b)),
            pl.BlockSpec((8, LANE), lambda b, g: (g, 0)),
            pl.BlockSpec((1, SMALL_T_ROWS, 8), lambda b, g: (g, 0, 0)),
            pl.BlockSpec((1, W), lambda b, g: (0, g)),
            pl.BlockSpec((1, W), lambda b, g: (0, g)),
        ],
        out_specs=pl.BlockSpec((1, S, W), lambda b, g: (b, 0, g)),
        out_shape=jax.ShapeDtypeStruct((B, S, SSD_D_INNER), BF16),
        scratch_shapes=[
            pltpu.VMEM((S, W), F32),
            pltpu.VMEM((S, LANE), F32),
            pltpu.VMEM((SMALL_T_ROWS, S), F32),
            pltpu.VMEM((S, LANE), F32),
            pltpu.VMEM((nc, 16, LANE), F32),
            pltpu.VMEM((nc, N, 2 * W), F32),
            pltpu.VMEM((nc, 8, 2 * W), F32),
            pltpu.VMEM((nc, N, 2 * W), BF16),
            pltpu.VMEM((N, 2 * W), F32),
        ],
        compiler_params=_cparams(("parallel", "parallel")),
        name="ssd_scan",
    )(xbc, xbc, xbc, u3, us3, ust, prow, pcol, drow, nw)


def _gla_kernel(q_ref, k_ref, v_ref, gg_ref, us_ref, a2_ref, bias_ref, nw_ref,
                o_ref, acc_ref, g_ref, qd_ref, kd_ref, kdp_ref, qcat_ref, x_ref, dec_ref, sst_ref, st_ref):
    L = GLA_CHUNK
    BL = 2 * L
    DK = GLA_DK
    S = q_ref.shape[1]
    nb = S // BL
    scale = DK ** -0.5

    ii = _iota2((BL, BL), 0)
    jj = _iota2((BL, BL), 1)
    same = (ii // L) == (jj // L)
    masks = (same & (ii >= jj), same & (jj >= ii))
    tri2 = masks[0].astype(BF16)
    par_row = _iota2((BL, DK), 0) // L

    ga = us_ref[0].astype(BF16)
    g_ref[...] = _log2_sigmoid(_dot(ga, a2_ref[...]) + bias_ref[...]) * (1.0 / GLA_GATE_NORM)

    def decay_body(i, carry):
        rs = pl.ds(pl.multiple_of(i * BL, BL), BL)
        g = g_ref[rs, :]
        hi = g.astype(BF16)
        r1 = g - hi.astype(F32)
        mid = r1.astype(BF16)
        lo = (r1 - mid.astype(F32)).astype(BF16)
        pp = _dot(tri2, jnp.concatenate([hi, mid, lo], axis=1))
        p = pp[:, 0:2 * DK] + pp[:, 2 * DK:4 * DK] + pp[:, 4 * DK:6 * DK]
        q_c = q_ref[0, rs, :].astype(F32) * scale
        k_c = k_ref[0, rs, :].astype(F32)
        zero = jnp.zeros((BL, DK), BF16)
        for d in range(2):
            p_d = p[:, d * DK:(d + 1) * DK]
            tot = jnp.where(par_row == 0, p_d[L - 1:L, :], p_d[BL - 1:BL, :])
            b = p_d if d == 0 else tot - p_d + g[:, DK:]
            qd = (q_c * jnp.exp2(b)).astype(BF16)
            kdec = (k_c * jnp.exp2(tot - b)).astype(BF16)
            qd_ref[d, rs, :] = qd
            kd_ref[d, rs, :] = (k_c * jnp.exp2(-b)).astype(BF16)
            for par in range(2):
                sel = par_row == par
                kdp_ref[d, rs, par * DK:(par + 1) * DK] = jnp.where(sel, kdec, zero)
                qcat_ref[rs, (2 * d + par) * DK:(2 * d + par + 1) * DK] = jnp.where(sel, qd, zero)
                last = (par + 1) * L - 1
                dec_ref[d, 2 * i + par] = jnp.broadcast_to(jnp.exp2(p_d[last:last + 1, :]), (8, DK))
        return carry

    lax.fori_loop(0, nb, decay_body, 0, unroll=2)

    G = GLA_BLOCKS_PER_STEP

    def intra_body(i, carry):
        blks = [G * i + j for j in range(G)]
        rss = [pl.ds(pl.multiple_of(b * BL, BL), BL) for b in blks]
        vs = [v_ref[0, rs, :] for rs in rss]
        atts = [[_nt(qd_ref[d, rs, :], kd_ref[d, rs, :]) for d in range(2)] for rs in rss]
        for j, b in enumerate(blks):
            for d in range(2):
                x_ref[d, b] = _tn(vs[j], kdp_ref[d, rss[j], :])
        for j in range(G):
            att = jnp.where(masks[0], atts[j][0], 0.0) + jnp.where(masks[1], atts[j][1], 0.0)
            acc_ref[rss[j], :] = _dot(att.astype(BF16), vs[j])
        return carry

    lax.fori_loop(0, nb // G, intra_body, 0)

    st_ref[...] = jnp.zeros_like(st_ref)

    def state_body(i, carry):
        for d, b in enumerate((i, nb - 1 - i)):
            for par in ((0, 1) if d == 0 else (1, 0)):
                s_prev = st_ref[d]
                lane0 = (2 * d + par) * DK
                sst_ref[b, :, lane0:lane0 + DK] = s_prev.astype(BF16)
                st_ref[d] = s_prev * dec_ref[d, 2 * b + par, 0:1, :] + x_ref[d, b, :, par * DK:(par + 1) * DK]
        return carry

    lax.fori_loop(0, nb, state_body, 0)

    def out_body(b, carry):
        rs = pl.ds(pl.multiple_of(b * BL, BL), BL)
        o = acc_ref[rs, :] + _nt(qcat_ref[rs, :], sst_ref[b])
        ms = jnp.mean(o * o, axis=-1, keepdims=True)
        o = o * lax.rsqrt(ms + EPS) * nw_ref[...]
        o_ref[0, rs, :] = (o * _silu(gg_ref[0, rs, :].astype(F32))).astype(BF16)
        return carry

    lax.fori_loop(0, nb, out_body, 0, unroll=GLA_BLOCKS_PER_STEP)


def _gla(u3, us3, a2, bias, nw):
    B, S, _ = u3.shape
    DK, DV = GLA_DK, GLA_DV
    nc = S // GLA_CHUNK
    nb = nc // 2
    return pl.pallas_call(
        _gla_kernel,
        grid=(B, GLA_HEADS),
        in_specs=[
            pl.BlockSpec((1, S, DK), lambda b, h: (b, 0, U_GQ // DK + h)),
            pl.BlockSpec((1, S, DK), lambda b, h: (b, 0, U_GK // DK + h)),
            pl.BlockSpec((1, S, DV), lambda b, h: (b, 0, U_GV // DV + h)),
            pl.BlockSpec((1, S, DV), lambda b, h: (b, 0, U_GG // DV + h)),
            pl.BlockSpec((1, S, LANE), lambda b, h: (b, 0, 0)),
            pl.BlockSpec((LANE, 2 * DK), lambda b, h: (0, h)),
            pl.BlockSpec((1, 2 * DK), lambda b, h: (0, h)),
            pl.BlockSpec((1, DV), lambda b, h: (0, 0)),
        ],
        out_specs=pl.BlockSpec((1, S, DV), lambda b, h: (b, 0, h)),
        out_shape=jax.ShapeDtypeStruct((B, S, GLA_VAL_W), BF16),
        scratch_shapes=[
            pltpu.VMEM((S, DV), F32),
            pltpu.VMEM((S, 2 * DK), F32),
            pltpu.VMEM((2, S, DK), BF16),
            pltpu.VMEM((2, S, DK), BF16),
            pltpu.VMEM((2, S, 2 * DK), BF16),
            pltpu.VMEM((S, 4 * DK), BF16),
            pltpu.VMEM((2, nb, DV, 2 * DK), F32),
            pltpu.VMEM((2, nc, 8, DK), F32),
            pltpu.VMEM((nb, DV, 4 * DK), BF16),
            pltpu.VMEM((2, DV, DK), F32),
        ],
        compiler_params=_cparams(("parallel", "parallel")),
        name="gla_scan",
    )(u3, u3, u3, u3, us3, a2, bias, nw)


def _na_bias_rows(rpb):
    H, R, C = rpb.shape
    n_pos = GRID_W - NA_WIN_W
    n_neg = GRID_W - NA_WIN_W + 2
    ext = jnp.concatenate([rpb[:, :, NA_WIN_W - 1:], jnp.repeat(rpb[:, :, C - 1:], n_pos, axis=2),
                           jnp.repeat(rpb[:, :, 0:1], n_neg, axis=2), rpb[:, :, 1:NA_WIN_W - 1]], axis=2)
    return jnp.pad(ext, ((0, 0), (0, 1), (0, 0))).reshape(H // 2, 2, R + 1, 2 * GRID_W)


def _na_kernel(flag_ref, q_ref, k_ref, v_ref, ext_ref, o_ref, vx_ref, tab_ref):
    S = q_ref.shape[1]
    rows = S // GRID_W
    win_h = NA_WIN_H
    nk = win_h * GRID_W

    vx_ref[:, 0:LANE] = v_ref[0]
    vx_ref[:, LANE:2 * LANE] = jnp.ones((S, LANE), BF16)
    bound_ok = flag_ref[0] != 0

    first_q = _iota2((GRID_W, LANE), 1) < NA_HEAD_DIM

    @pl.when(pl.program_id(1) == 0)
    def _():
        q_col = _iota2((GRID_W, 2 * GRID_W), 0)
        k_col = _iota2((GRID_W, 2 * GRID_W), 1) % GRID_W
        w_start = jnp.clip(q_col - NA_WIN_W // 2, 0, GRID_W - NA_WIN_W)
        in_window = (k_col >= w_start) & (k_col < w_start + NA_WIN_W)
        low = _iota2((GRID_W, 2 * GRID_W), 1) < GRID_W
        for hd in range(2):
            def skewed(rr, shift):
                row = jnp.broadcast_to(ext_ref[0, hd, rr:rr + 1, :], (GRID_W, 2 * GRID_W))
                return pltpu.roll(row, shift, axis=1, stride=1, stride_axis=0)

            for e in range(2 * NA_WIN_H - 2):
                t = jnp.where(low, skewed(e, 0), skewed(e + 1, GRID_W))
                tab_ref[e, hd * GRID_W:(hd + 1) * GRID_W, :] = jnp.where(in_window, t, -jnp.inf)

    def key_start(r):
        r0 = jnp.clip(r - win_h // 2, 0, rows - win_h)
        return r0, pl.multiple_of(r0 * GRID_W, GRID_W)

    def scores(r):
        r0, k0 = key_start(r)
        q = q_ref[0, pl.ds(pl.multiple_of(r * GRID_W, GRID_W), GRID_W), :]
        zero = jnp.zeros_like(q)
        qs = jnp.concatenate([jnp.where(first_q, q, zero), jnp.where(first_q, zero, q)], axis=0)
        rr0 = (win_h - 1) - (r - r0)
        bias = jnp.concatenate([tab_ref[rr0 + w] for w in range(0, win_h, 2)], axis=1)
        return _nt(qs, k_ref[0, pl.ds(k0, nk), :]) + bias

    def probs_exact(s):
        return jnp.exp2(s - jnp.max(s, axis=-1, keepdims=True)).astype(BF16)

    def probs_bounded(r):
        return jnp.exp2(scores(r)).astype(BF16)

    def attend(r, p):
        _, k0 = key_start(r)
        ox = _dot(p, vx_ref[pl.ds(k0, nk), :])
        o = ox[:, 0:LANE] / ox[:, LANE:2 * LANE]
        q0 = pl.multiple_of(r * GRID_W, GRID_W)
        o_ref[0, pl.ds(q0, GRID_W), :] = jnp.where(first_q, o[:GRID_W], o[GRID_W:]).astype(BF16)

    @pl.when(bound_ok)
    def _():
        U = NA_ROWS_PER_STEP

        def row_body(i, p_prev):
            r = i * U
            for j in range(U):
                attend(r - U + j, p_prev[j])
            return tuple(probs_bounded(r + j) for j in range(U))

        p_last = lax.fori_loop(1, rows // U, row_body, tuple(probs_bounded(j) for j in range(U)))
        for j in range(U):
            attend(rows - U + j, p_last[j])

    @pl.when(jnp.logical_not(bound_ok))
    def _():
        U = NA_ROWS_PER_STEP_EXACT

        def row_body(i, carry):
            s_cur, p_prev = carry
            r = i * U
            for j in range(U):
                attend(r - U + j, p_prev[j])
            p = tuple(probs_exact(s) for s in s_cur)
            s_next = tuple(scores(jnp.minimum(r + U + j, rows - 1)) for j in range(U))
            return s_next, p

        p0 = tuple(probs_exact(scores(j)) for j in range(U))
        s1 = tuple(scores(U + j) for j in range(U))
        _, p_last = lax.fori_loop(1, rows // U, row_body, (s1, p0))
        for j in range(U):
            attend(rows - U + j, p_last[j])


def _na_score_bound(rpb, q_norm_w, k_norm_w):
    H = rpb.shape[0]
    qk = (NA_HEAD_DIM ** 0.5 * LOG2E * (1.0 + NA_BOUND_SLACK)) * jnp.max(jnp.abs(q_norm_w)) * jnp.max(jnp.abs(k_norm_w))
    b_max = jnp.max(rpb.reshape(H, -1), axis=1) * LOG2E
    b_self = rpb[:, NA_WIN_H - 1, NA_WIN_W - 1] * LOG2E
    bound = qk + b_max
    flag = jnp.all(bound - (b_self - qk) <= NA_MAX_BOUND_GAP)
    return bound, flag.astype(jnp.int32).reshape(1)


def _na(u3, bias_rows, flag):
    B, S, _ = u3.shape
    assert S // GRID_W >= NA_WIN_H and NA_WIN_H % 2 == 0 and 2 * GRID_W == LANE
    return pl.pallas_call(
        _na_kernel,
        grid=(NA_HEADS // 2, B),
        in_specs=[
            pl.BlockSpec(memory_space=pltpu.SMEM),
            pl.BlockSpec((1, S, LANE), lambda h, b: (b, 0, U_NQ // LANE + h)),
            pl.BlockSpec((1, S, LANE), lambda h, b: (b, 0, U_NK // LANE + h)),
            pl.BlockSpec((1, S, LANE), lambda h, b: (b, 0, U_NV // LANE + h)),
            pl.BlockSpec((1, 2, 2 * NA_WIN_H, 2 * GRID_W), lambda h, b: (h, 0, 0, 0)),
        ],
        out_specs=pl.BlockSpec((1, S, LANE), lambda h, b: (b, 0, h)),
        out_shape=jax.ShapeDtypeStruct((B, S, NA_W), BF16),
        scratch_shapes=[
            pltpu.VMEM((S, 2 * LANE), BF16),
            pltpu.VMEM((2 * NA_WIN_H - 2, 2 * GRID_W, 2 * GRID_W), F32),
        ],
        compiler_params=_cparams(("parallel", "arbitrary")),
        name="na_attn",
    )(flag, u3, u3, u3, bias_rows)


def _merge_kernel(x_ref, ys_ref, yg_ref, yn_ref, gate_ref, ws_ref, wg_ref, wn_ref, wo_ref, o_ref):
    D = D_MODEL
    mixed = _sigmoid(gate_ref[:, 0:D].astype(F32)) * _dot(ys_ref[...], ws_ref[...])
    mixed += _sigmoid(gate_ref[:, D:2 * D].astype(F32)) * _dot(yg_ref[...], wg_ref[...])
    mixed += _sigmoid(gate_ref[:, 2 * D:3 * D].astype(F32)) * _dot(yn_ref[...], wn_ref[...])
    o_ref[...] = x_ref[...] + _dot(mixed.astype(BF16), wo_ref[...])


def _merge(x2, ys, yg, yn, u2, ws, wg, wn, wo, tm=512):
    T = x2.shape[0]
    D = D_MODEL
    row = lambda i: (i, 0)
    fixed = lambda i: (0, 0)
    return pl.pallas_call(
        _merge_kernel,
        grid=(T // tm,),
        in_specs=[
            pl.BlockSpec((tm, D), row),
            pl.BlockSpec((tm, D), row),
            pl.BlockSpec((tm, D), row),
            pl.BlockSpec((tm, D), row),
            pl.BlockSpec((tm, N_BRANCH * D), lambda i: (i, U_GATE // (N_BRANCH * D))),
            pl.BlockSpec((D, D), fixed),
            pl.BlockSpec((D, D), fixed),
            pl.BlockSpec((D, D), fixed),
            pl.BlockSpec((D, D), fixed),
        ],
        out_specs=pl.BlockSpec((tm, D), row),
        out_shape=jax.ShapeDtypeStruct((T, D), F32),
        compiler_params=_cparams(("parallel",)),
        name="merge",
    )(x2, ys, yg, yn, u2, ws, wg, wn, wo)


def _mlp_kernel(x_ref, nw_ref, w1_ref, w2_ref, o_ref, *, tf):
    x = x_ref[...]
    ms = jnp.mean(x * x, axis=-1, keepdims=True)
    h = (x * lax.rsqrt(ms + EPS) * nw_ref[...]).astype(BF16)
    acc = x
    for f in range(D_FF // tf):
        a = jnp.maximum(_dot(h, w1_ref[:, f * tf:(f + 1) * tf]), 0.0)
        acc = acc + _dot((a * a).astype(BF16), w2_ref[f * tf:(f + 1) * tf, :])
    o_ref[...] = acc


def _mlp(x2, nw, w1, w2, tm=512, tf=1024):
    T = x2.shape[0]
    D = D_MODEL
    return pl.pallas_call(
        functools.partial(_mlp_kernel, tf=tf),
        grid=(T // tm,),
        in_specs=[
            pl.BlockSpec((tm, D), lambda i: (i, 0)),
            pl.BlockSpec((1, D), lambda i: (0, 0)),
            pl.BlockSpec((D, D_FF), lambda i: (0, 0)),
            pl.BlockSpec((D_FF, D), lambda i: (0, 0)),
        ],
        out_specs=pl.BlockSpec((tm, D), lambda i: (i, 0)),
        out_shape=jax.ShapeDtypeStruct((T, D), F32),
        compiler_params=_cparams(("parallel",)),
        name="mlp",
    )(x2, nw, w1, w2)


def _pad_rows(w, start, total):
    return jnp.zeros((total, w.shape[1]), w.dtype).at[start:start + w.shape[0]].set(w)


def kernel(x, norm_mix_w, w_in, ssd_conv_w, ssd_conv_b, ssd_dt_bias_f, ssd_dt_bias_b, ssd_a_log_f,
           ssd_a_log_b, ssd_d, ssd_norm_w, gla_a2_f, gla_a2_bias_f, gla_a2_b, gla_a2_bias_b,
           gla_norm_w, na_q_norm_w, na_k_norm_w, na_rpb, w_branch_ssd, w_branch_gla, w_branch_na,
           w_out, norm_mlp_w, w_ff1, w_ff2):
    B, S, D = x.shape
    T = B * S
    depth = w_in.shape[0]
    rows = S // GRID_W
    x2 = x.reshape(T, D)
    for l in range(depth):
        w_big = jnp.concatenate([w_in[l][:, a:a + n] for a, n in _BIG_SEGS], axis=1).astype(BF16)
        w_small = _small_weight(w_in[l]).astype(BF16)
        hg = (SSD_GROUPS, SSD_HG)
        zeros_r = jnp.zeros((SSD_GROUPS, LANE - 2 * SSD_HG), F32)
        dt_bias = jnp.concatenate([ssd_dt_bias_f[l].reshape(hg), ssd_dt_bias_b[l].reshape(hg), zeros_r], axis=1)
        a_neg = jnp.concatenate([-jnp.exp(ssd_a_log_f[l]).reshape(hg), -jnp.exp(ssd_a_log_b[l]).reshape(hg),
                                 zeros_r], axis=1)
        prow3 = jnp.concatenate([dt_bias[:, None], a_neg[:, None], jnp.zeros((SSD_GROUPS, 6, LANE), F32)], axis=1)
        prow = prow3.reshape(SSD_GROUPS * 8, LANE)
        pcol = jnp.transpose(prow3[:, :, :SMALL_T_ROWS], (0, 2, 1))
        w_small_t = w_small.T.reshape(SSD_GROUPS, LANE, D)[:, :SMALL_T_ROWS].reshape(-1, D)
        drow = jnp.repeat(ssd_d[l], SSD_HEAD_DIM)[None, :]
        hk = (GLA_HEADS, 1, GLA_DK)
        a2 = jnp.concatenate([_pad_rows(gla_a2_f[l], SM_GAF, LANE).reshape((LANE,) + hk),
                              _pad_rows(gla_a2_b[l], SM_GAB, LANE).reshape((LANE,) + hk)],
                             axis=2).reshape(LANE, -1).astype(BF16)
        a2_bias = jnp.concatenate([gla_a2_bias_f[l].reshape(hk), gla_a2_bias_b[l].reshape(hk)],
                                  axis=1).reshape(1, -1)
        na_bound, na_flag = _na_score_bound(na_rpb[l], na_q_norm_w[l], na_k_norm_w[l])
        table = _na_bias_rows(na_rpb[l] * LOG2E - na_bound[:, None, None])
        q_row = jnp.tile(na_q_norm_w[l] * (NA_HEAD_DIM ** -0.5 * LOG2E), NA_HEADS)
        k_row = jnp.tile(na_k_norm_w[l], NA_HEADS)
        qkw = jnp.concatenate([q_row[None], k_row[None], jnp.zeros((6, NA_W), F32)], axis=0)

        u2, us2, ust = _inproj(x2, norm_mix_w[l][None, :], w_big, w_small, w_small_t, qkw)
        u3 = u2.reshape(B, S, U_WIDTH)
        us3 = us2.reshape(B, S, SMALL_W)
        xbc = _conv(u3, ssd_conv_w[l], ssd_conv_b[l][None, :])
        y_ssd = _ssd(xbc, u3, us3, ust, prow, pcol, drow, ssd_norm_w[l][None, :])
        y_gla = _gla(u3, us3, a2, a2_bias, gla_norm_w[l][None, :])
        y_na = _na(u3, table, na_flag)
        x2 = _merge(x2, y_ssd.reshape(T, -1), y_gla.reshape(T, -1), y_na.reshape(T, -1), u2,
                    w_branch_ssd[l].astype(BF16), w_branch_gla[l].astype(BF16),
                    w_branch_na[l].astype(BF16), w_out[l].astype(BF16))
        x2 = _mlp(x2, norm_mlp_w[l][None, :], w_ff1[l].astype(BF16), w_ff2[l].astype(BF16))
    return x2.reshape(B, S, D)
```

```python
import functools

import jax
import jax.numpy as jnp
import numpy as np
from jax import lax
from jax.experimental import pallas as pl
from jax.experimental.pallas import tpu as pltpu

F32 = jnp.float32
BF16 = jnp.bfloat16

EPS = 1e-6
D_MODEL = 1024
GRID_W = 64

SSD_HEADS = 16
SSD_HEAD_DIM = 64
SSD_D_INNER = 1024
SSD_GROUPS = 2
SSD_STATE = 128
SSD_CONV = 5
SSD_CONV_DIM = 1536
SSD_CHUNK = 128
SSD_CHUNKS_PER_STEP = 2
SSD_GROUP_W = SSD_D_INNER // SSD_GROUPS
SSD_HG = SSD_HEADS // SSD_GROUPS

GLA_HEADS = 4
GLA_DK = 128
GLA_DV = 256
GLA_KEY_W = 512
GLA_VAL_W = 1024
GLA_GATE_RANK = 16
GLA_GATE_NORM = 16.0
GLA_CHUNK = 64
GLA_BLOCKS_PER_STEP = 2
GLA_OUT_UNROLL = 4

NA_HEADS = 16
NA_HEAD_DIM = 64
NA_W = 1024
NA_WIN_H = 8
NA_WIN_W = 16
NA_ROWS_PER_STEP = 8
NA_ROWS_PER_STEP_EXACT = 2
LOG2E = 1.4426950408889634
NA_BOUND_SLACK = 0.02
NA_MAX_BOUND_GAP = 90.0

N_BRANCH = 3
D_FF = 4096

IN_SIZES = (SSD_D_INNER, SSD_CONV_DIM, SSD_HEADS, SSD_HEADS,
            GLA_KEY_W, GLA_KEY_W, GLA_VAL_W, GLA_VAL_W, GLA_GATE_RANK, GLA_GATE_RANK,
            NA_W, NA_W, NA_W, N_BRANCH * D_MODEL)
_IN_OFF = np.concatenate([[0], np.cumsum(IN_SIZES)])
(_O_Z, _O_XBC, _O_DTF, _O_DTB, _O_GQ, _O_GK, _O_GV, _O_GG, _O_GAF, _O_GAB,
 _O_NQ, _O_NK, _O_NV, _O_GATE) = [int(v) for v in _IN_OFF[:-1]]

U_NQ = 0
U_NK = 1024
U_NV = 2048
U_GATE = 3072
U_Z = 6144
U_XBC = 7168
U_GQ = 8704
U_GK = 9216
U_GV = 9728
U_GG = 10752
U_WIDTH = 11776
INPROJ_COL_CHUNK = 1024
LANE = 128
VMEM_LIMIT = 56 * 1024 * 1024

SMALL_W = SSD_GROUPS * LANE
SM_DTF, SM_DTB, SM_GAF, SM_GAB = 0, 8, 16, 32
SMALL_T_ROWS = 2 * SSD_HG

_BIG_SEGS = ((_O_NQ, 6144), (_O_Z, 2560), (_O_GQ, 3072))


def _small_weight(w):
    blocks = []
    for g in range(SSD_GROUPS):
        h0 = g * SSD_HG
        cols = [w[:, _O_DTF + h0:_O_DTF + h0 + SSD_HG], w[:, _O_DTB + h0:_O_DTB + h0 + SSD_HG]]
        used = 2 * SSD_HG
        if g == 0:
            cols += [w[:, _O_GAF:_O_GAF + GLA_GATE_RANK], w[:, _O_GAB:_O_GAB + GLA_GATE_RANK]]
            used += 2 * GLA_GATE_RANK
        cols.append(jnp.zeros((w.shape[0], LANE - used), w.dtype))
        blocks += cols
    return jnp.concatenate(blocks, axis=1)


def _cparams(sem, vmem=VMEM_LIMIT):
    return pltpu.CompilerParams(dimension_semantics=sem, vmem_limit_bytes=vmem)


def _sigmoid(x):
    return 1.0 / (1.0 + jnp.exp(-x))


def _silu(x):
    return x * _sigmoid(x)


def _softplus(x):
    return jnp.maximum(x, 0.0) + jnp.log1p(jnp.exp(-jnp.abs(x)))


def _log2_sigmoid(x):
    t = x * (-LOG2E)
    return -(jnp.maximum(t, 0.0) + jnp.log2(1.0 + jnp.exp2(-jnp.abs(t))))


def _nt(a, b):
    return lax.dot_general(a, b, (((1,), (1,)), ((), ())), preferred_element_type=F32)


def _tn(a, b):
    return lax.dot_general(a, b, (((0,), (0,)), ((), ())), preferred_element_type=F32)


def _dot(a, b):
    return jnp.dot(a, b, preferred_element_type=F32)


def _dot_exact(a, b):
    return jnp.dot(a, b, preferred_element_type=F32, precision=lax.Precision.HIGHEST)


def _iota2(shape, dim):
    return lax.broadcasted_iota(jnp.int32, shape, dim)


def _head_rms(r, w_row):
    G = 2 * LANE
    er = _iota2((G, G), 0) // NA_HEAD_DIM
    ec = _iota2((G, G), 1) // NA_HEAD_DIM
    e_blk = (er == ec).astype(BF16)
    outs = []
    for a in range(0, r.shape[1], G):
        x = r[:, a:a + G]
        ms = _dot((x * x).astype(BF16), e_blk) * (1.0 / NA_HEAD_DIM)
        outs.append(x * lax.rsqrt(ms + EPS))
    return jnp.concatenate(outs, axis=1) * w_row


def _inproj_kernel(x_ref, nw_ref, w_ref, ws_ref, wst_ref, qkw_ref, u_ref, us_ref, ust_ref, h_ref):
    tn = u_ref.shape[1]
    chunks = [(a, min(a + INPROJ_COL_CHUNK, tn)) for a in range(0, tn, INPROJ_COL_CHUNK)]

    @pl.when(pl.program_id(1) == 0)
    def _():
        x = x_ref[...]
        ms = jnp.mean(x * x, axis=-1, keepdims=True)
        h = (x * lax.rsqrt(ms + EPS) * nw_ref[...]).astype(BF16)
        h_ref[...] = h
        us_ref[...] = _dot(h, ws_ref[...])
        ust_ref[...] = _nt(wst_ref[...], h)
        for c, (a, b) in enumerate(chunks):
            r = _dot(h, w_ref[:, a:b])
            if c < 2:
                r = _head_rms(r, qkw_ref[c:c + 1, :])
            u_ref[:, a:b] = r.astype(BF16)

    @pl.when(pl.program_id(1) != 0)
    def _():
        for a, b in chunks:
            u_ref[:, a:b] = _dot(h_ref[...], w_ref[:, a:b]).astype(BF16)


def _inproj(x2, nw, w_big, w_small, w_small_t, qkw, tm=1024, tn=U_WIDTH // 4):
    assert (U_NQ, U_NK) == (0, INPROJ_COL_CHUNK) and NA_W == INPROJ_COL_CHUNK
    T = x2.shape[0]
    return pl.pallas_call(
        _inproj_kernel,
        grid=(T // tm, U_WIDTH // tn),
        in_specs=[
            pl.BlockSpec((tm, D_MODEL), lambda i, j: (i, 0)),
            pl.BlockSpec((1, D_MODEL), lambda i, j: (0, 0)),
            pl.BlockSpec((D_MODEL, tn), lambda i, j: (0, j)),
            pl.BlockSpec((D_MODEL, SMALL_W), lambda i, j: (0, 0)),
            pl.BlockSpec((SSD_GROUPS * SMALL_T_ROWS, D_MODEL), lambda i, j: (0, 0)),
            pl.BlockSpec((8, NA_W), lambda i, j: (0, 0)),
        ],
        out_specs=[
            pl.BlockSpec((tm, tn), lambda i, j: (i, j)),
            pl.BlockSpec((tm, SMALL_W), lambda i, j: (i, 0)),
            pl.BlockSpec((SSD_GROUPS * SMALL_T_ROWS, tm), lambda i, j: (0, i)),
        ],
        out_shape=[
            jax.ShapeDtypeStruct((T, U_WIDTH), BF16),
            jax.ShapeDtypeStruct((T, SMALL_W), F32),
            jax.ShapeDtypeStruct((SSD_GROUPS * SMALL_T_ROWS, T), F32),
        ],
        scratch_shapes=[pltpu.VMEM((tm, D_MODEL), BF16)],
        compiler_params=_cparams(("parallel", "arbitrary")),
        name="inproj",
    )(x2, nw, w_big, w_small, w_small_t, qkw)


def _conv_kernel(u_ref, w_ref, b_ref, o_ref):
    x = u_ref[0].astype(F32)
    S = x.shape[0]
    row = _iota2(x.shape, 0)
    acc = jnp.zeros_like(x) + b_ref[...]
    pad = SSD_CONV // 2
    for k in range(SSD_CONV):
        d = k - pad
        if d == 0:
            xs = x
        else:
            xs = pltpu.roll(x, (-d) % S, axis=0)
            valid = (row + d >= 0) & (row + d < S)
            xs = jnp.where(valid, xs, 0.0)
        acc = acc + w_ref[k:k + 1, :] * xs
    o_ref[0] = _silu(acc).astype(BF16)


def _conv(u3, conv_w, conv_b, tc=256):
    B, S, _ = u3.shape
    nblk = SSD_CONV_DIM // tc
    off = U_XBC // tc
    return pl.pallas_call(
        _conv_kernel,
        grid=(B, nblk),
        in_specs=[
            pl.BlockSpec((1, S, tc), lambda b, c: (b, 0, off + c)),
            pl.BlockSpec((SSD_CONV, tc), lambda b, c: (0, c)),
            pl.BlockSpec((1, tc), lambda b, c: (0, c)),
        ],
        out_specs=pl.BlockSpec((1, S, tc), lambda b, c: (b, 0, c)),
        out_shape=jax.ShapeDtypeStruct((B, S, SSD_CONV_DIM), BF16),
        compiler_params=_cparams(("parallel", "parallel")),
        name="ssd_conv",
    )(u3, conv_w, conv_b)


def _split_hi_lo(v):
    hi = v.astype(BF16)
    lo = (v - hi.astype(F32)).astype(BF16)
    return jnp.concatenate([hi, lo], axis=1)


def _split3(v, axis):
    hi = v.astype(BF16)
    r1 = v - hi.astype(F32)
    mid = r1.astype(BF16)
    lo = (r1 - mid.astype(F32)).astype(BF16)
    return jnp.concatenate([hi, mid, lo], axis=axis)


def _ssd_kernel(x_ref, b_ref, c_ref, z_ref, us_ref, ust_ref, prow_ref, pcol_ref, drow_ref, nw_ref,
                o_ref, acc_ref, cumc_ref, cumr_ref, wdt_ref, dec_ref, xs_ref, decx_ref, sst_ref, st_ref):
    L = SSD_CHUNK
    S = x_ref.shape[1]
    nc = S // L
    W = SSD_GROUP_W
    HG = SSD_HG
    R = 2 * HG

    ii = _iota2((L, L), 0)
    jj = _iota2((L, L), 1)
    tril = ii >= jj
    triu = jj >= ii
    tril_b = tril.astype(BF16)
    triu_b = triu.astype(BF16)

    er = _iota2((LANE, 2 * W), 0)
    ec = _iota2((LANE, 2 * W), 1)
    e = (er == jnp.where(ec < W, ec // SSD_HEAD_DIM, (ec - W) // SSD_HEAD_DIM + HG)).astype(BF16)
    e2 = jnp.concatenate([e, e], axis=0)

    bias_row = prow_ref[0:1, :]
    a_row = prow_ref[1:2, :]
    bias_col = pcol_ref[0, :, 0:1]
    a_col = pcol_ref[0, :, 1:2]

    lane_fwd = _iota2((L, LANE), 1) < HG
    row_fwd = _iota2((R, L), 0) < HG
    lane_half = _iota2((L, LANE), 1) < SSD_HEAD_DIM

    def decay_sums(c):
        rs = pl.ds(pl.multiple_of(c * L, L), L)
        dt_c = _softplus(us_ref[0, rs, :] + bias_row)
        a_c = dt_c * a_row
        dt_r = _softplus(ust_ref[:, rs] + bias_col)
        a_r = dt_r * a_col
        return dt_c, a_c, _dot(tril_b, _split3(a_c, 1)), dt_r, a_r, _dot(_split3(a_r, 0), triu_b)

    def decay_store(c, sums):
        rs = pl.ds(pl.multiple_of(c * L, L), L)
        dt_c, a_c, pp, dt_r, a_r, pr = sums
        p_c = pp[:, 0:LANE] + pp[:, LANE:2 * LANE] + pp[:, 2 * LANE:3 * LANE]
        tot_c = p_c[L - 1:L, :]
        cum_c = jnp.where(lane_fwd, p_c, tot_c - p_c + a_c)
        cumc_ref[rs, :] = cum_c * LOG2E
        wdt_ref[rs, :] = jnp.exp(tot_c - cum_c) * dt_c
        dec_ref[c] = jnp.broadcast_to(jnp.exp(tot_c), (16, LANE))
        p_r = pr[0:R] + pr[R:2 * R] + pr[2 * R:3 * R]
        tot_r = p_r[:, L - 1:L]
        cum_r = jnp.where(row_fwd, p_r, tot_r - p_r + a_r)
        cumr_ref[:, rs] = (cum_r - jnp.log(dt_r)) * LOG2E

    def intra_chunk(c):
        rs = pl.ds(pl.multiple_of(c * L, L), L)
        x_c = x_ref[0, rs, :]
        b_c = b_ref[0, rs, :]
        cb = _nt(c_ref[0, rs, :], b_c)
        ex = _dot(_split_hi_lo(jnp.concatenate([wdt_ref[rs, :], dec_ref[c]], axis=0)), e2)
        decx_ref[c] = ex[L:L + 8]
        xf = x_c.astype(F32)
        xw = jnp.concatenate([xf * ex[0:L, 0:W], xf * ex[0:L, W:2 * W]], axis=1).astype(BF16)
        xs_ref[c] = _tn(b_c, xw)
        cum_c = cumc_ref[rs, :]
        cum_r = cumr_ref[:, rs]
        for hp in range(HG // 2):
            ms = []
            for hh in range(2):
                hf = 2 * hp + hh
                hb = HG + 2 * hp + hh
                decf = jnp.exp2(jnp.where(tril, cum_c[:, hf:hf + 1] - cum_r[hf:hf + 1, :], -jnp.inf))
                decb = jnp.exp2(jnp.where(triu, cum_c[:, hb:hb + 1] - cum_r[hb:hb + 1, :], -jnp.inf))
                ms.append((cb * (decf + decb)).astype(BF16))
            m2 = jnp.concatenate(ms, axis=1)
            xp = x_c[:, hp * LANE:(hp + 1) * LANE]
            xz = jnp.zeros_like(xp)
            x2 = jnp.concatenate([jnp.where(lane_half, xp, xz), jnp.where(lane_half, xz, xp)], axis=0)
            acc_ref[rs, hp * LANE:(hp + 1) * LANE] = _dot(m2, x2)

    G = SSD_CHUNKS_PER_STEP
    n_steps = nc // G

    def fused_body(i, carry):
        nxt = jnp.minimum(i + 1, n_steps - 1)
        sums = [decay_sums(G * nxt + j) for j in range(G)]
        for j in range(G):
            intra_chunk(G * i + j)
        for j in range(G):
            decay_store(G * nxt + j, sums[j])
        return carry

    for j in range(G):
        decay_store(j, decay_sums(j))
    lax.fori_loop(0, n_steps, fused_body, 0)

    st_ref[...] = jnp.zeros_like(st_ref)

    def state_body(i, carry):
        for d, c in enumerate((i, nc - 1 - i)):
            ls = slice(d * W, (d + 1) * W)
            s_prev = st_ref[:, ls]
            sst_ref[c, :, ls] = s_prev.astype(BF16)
            st_ref[:, ls] = s_prev * decx_ref[c, 0:1, ls] + xs_ref[c, :, ls]
        return carry

    lax.fori_loop(0, nc, state_body, 0)

    def out_body(c, carry):
        rs = pl.ds(pl.multiple_of(c * L, L), L)
        yo = _dot(c_ref[0, rs, :], sst_ref[c])
        sc = _dot(_split_hi_lo(jnp.exp2(cumc_ref[rs, :])), e2)
        xf = x_ref[0, rs, :].astype(F32)
        y = acc_ref[rs, :] + yo[:, 0:W] * sc[:, 0:W] + yo[:, W:2 * W] * sc[:, W:2 * W] + xf * drow_ref[...]
        y = y * _silu(z_ref[0, rs, :].astype(F32))
        ms = jnp.mean(y * y, axis=-1, keepdims=True)
        o_ref[0, rs, :] = (y * lax.rsqrt(ms + EPS) * nw_ref[...]).astype(BF16)
        return carry

    lax.fori_loop(0, nc, out_body, 0, unroll=2)


def _ssd(xbc, u3, us3, ust, prow, pcol, drow, nw):
    B, S, _ = xbc.shape
    W = SSD_GROUP_W
    N = SSD_STATE
    nc = S // SSD_CHUNK
    return pl.pallas_call(
        _ssd_kernel,
        grid=(B, SSD_GROUPS),
        in_specs=[
            pl.BlockSpec((1, S, W), lambda b, g: (b, 0, g)),
            pl.BlockSpec((1, S, N), lambda b, g: (b, 0, SSD_D_INNER // N + g)),
            pl.BlockSpec((1, S, N), lambda b, g: (b, 0, SSD_D_INNER // N + SSD_GROUPS + g)),
            pl.BlockSpec((1, S, W), lambda b, g: (b, 0, U_Z // W + g)),
            pl.BlockSpec((1, S, LANE), lambda b, g: (b, 0, g)),
            pl.BlockSpec((SMALL_T_ROWS, S), lambda b, g: (g, b)),
            pl.BlockSpec((8, LANE), lambda b, g: (g, 0)),
            pl.BlockSpec((1, SMALL_T_ROWS, 8), lambda b, g: (g, 0, 0)),
            pl.BlockSpec((1, W), lambda b, g: (0, g)),
            pl.BlockSpec((1, W), lambda b, g: (0, g)),
        ],
        out_specs=pl.BlockSpec((1, S, W), lambda b, g: (b, 0, g)),
        out_shape=jax.ShapeDtypeStruct((B, S, SSD_D_INNER), BF16),
        scratch_shapes=[
            pltpu.VMEM((S, W), F32),
            pltpu.VMEM((S, LANE), F32),
            pltpu.VMEM((SMALL_T_ROWS, S), F32),
            pltpu.VMEM((S, LANE), F32),
            pltpu.VMEM((nc, 16, LANE), F32),
            pltpu.VMEM((nc, N, 2 * W), F32),
            pltpu.VMEM((nc, 8, 2 * W), F32),
            pltpu.VMEM((nc, N, 2 * W), BF16),
            pltpu.VMEM((N, 2 * W), F32),
        ],
        compiler_params=_cparams(("parallel", "parallel")),
        name="ssd_scan",
    )(xbc, xbc, xbc, u3, us3, ust, prow, pcol, drow, nw)


def _gla_kernel(q_ref, k_ref, v_ref, gg_ref, us_ref, a2_ref, bias_ref, nw_ref,
                o_ref, acc_ref, g_ref, qd_ref, kd_ref, kdp_ref, qcat_ref, x_ref, dec_ref, sst_ref):
    L = GLA_CHUNK
    BL = 2 * L
    DK = GLA_DK
    S = q_ref.shape[1]
    nb = S // BL
    scale = DK ** -0.5

    ii = _iota2((BL, BL), 0)
    jj = _iota2((BL, BL), 1)
    same = (ii // L) == (jj // L)
    masks = (same & (ii >= jj), same & (jj >= ii))
    tri2 = masks[0].astype(BF16)
    par_row = _iota2((BL, DK), 0) // L

    ga = us_ref[0].astype(BF16)
    g_ref[...] = _log2_sigmoid(_dot(ga, a2_ref[...]) + bias_ref[...]) * (1.0 / GLA_GATE_NORM)

    def decay_sums(i):
        g = g_ref[pl.ds(pl.multiple_of(i * BL, BL), BL), :]
        hi = g.astype(BF16)
        r1 = g - hi.astype(F32)
        mid = r1.astype(BF16)
        lo = (r1 - mid.astype(F32)).astype(BF16)
        return g, _dot(tri2, jnp.concatenate([hi, mid, lo], axis=1))

    def decay_block(i, sums):
        rs = pl.ds(pl.multiple_of(i * BL, BL), BL)
        g, pp = sums
        p = pp[:, 0:2 * DK] + pp[:, 2 * DK:4 * DK] + pp[:, 4 * DK:6 * DK]
        q_c = q_ref[0, rs, :].astype(F32) * scale
        k_c = k_ref[0, rs, :].astype(F32)
        zero = jnp.zeros((BL, DK), BF16)
        for d in range(2):
            p_d = p[:, d * DK:(d + 1) * DK]
            tot = jnp.where(par_row == 0, p_d[L - 1:L, :], p_d[BL - 1:BL, :])
            b = p_d if d == 0 else tot - p_d + g[:, DK:]
            qd = (q_c * jnp.exp2(b)).astype(BF16)
            kdec = (k_c * jnp.exp2(tot - b)).astype(BF16)
            qd_ref[d, rs, :] = qd
            kd_ref[d, rs, :] = (k_c * jnp.exp2(-b)).astype(BF16)
            for par in range(2):
                sel = par_row == par
                kdp_ref[d, rs, par * DK:(par + 1) * DK] = jnp.where(sel, kdec, zero)
                qcat_ref[rs, (2 * d + par) * DK:(2 * d + par + 1) * DK] = jnp.where(sel, qd, zero)
                last = (par + 1) * L - 1
                dec_ref[d, 2 * i + par] = jnp.broadcast_to(jnp.exp2(p_d[last:last + 1, :]), (8, DK))

    G = GLA_BLOCKS_PER_STEP
    n_groups = nb // G

    def intra_group(i):
        blks = [G * i + j for j in range(G)]
        rss = [pl.ds(pl.multiple_of(b * BL, BL), BL) for b in blks]
        vs = [v_ref[0, rs, :] for rs in rss]
        atts = [[_nt(qd_ref[d, rs, :], kd_ref[d, rs, :]) for d in range(2)] for rs in rss]
        for j, b in enumerate(blks):
            for d in range(2):
                x_ref[d, b] = _tn(vs[j], kdp_ref[d, rss[j], :])
        for j in range(G):
            att = jnp.where(masks[0], atts[j][0], 0.0) + jnp.where(masks[1], atts[j][1], 0.0)
            acc_ref[rss[j], :] = _dot(att.astype(BF16), vs[j])

    def fused_body(i, carry):
        nxt = jnp.minimum(i + 1, n_groups - 1)
        sums = [decay_sums(G * nxt + j) for j in range(G)]
        intra_group(i)
        for j in range(G):
            decay_block(G * nxt + j, sums[j])
        return carry

    for j in range(G):
        decay_block(j, decay_sums(j))
    lax.fori_loop(0, n_groups, fused_body, 0)

    def state_body(d):
        def body(i, s):
            b = i if d == 0 else nb - 1 - i
            for par in ((0, 1) if d == 0 else (1, 0)):
                lane0 = (2 * d + par) * DK
                sst_ref[b, :, lane0:lane0 + DK] = s.astype(BF16)
                s = s * dec_ref[d, 2 * b + par, 0:1, :] + x_ref[d, b, :, par * DK:(par + 1) * DK]
            return s
        return body

    for d in range(2):
        lax.fori_loop(0, nb, state_body(d), jnp.zeros((GLA_DV, DK), F32))

    def out_body(b, carry):
        rs = pl.ds(pl.multiple_of(b * BL, BL), BL)
        o = acc_ref[rs, :] + _nt(qcat_ref[rs, :], sst_ref[b])
        ms = jnp.mean(o * o, axis=-1, keepdims=True)
        o = o * lax.rsqrt(ms + EPS) * nw_ref[...]
        o_ref[0, rs, :] = (o * _silu(gg_ref[0, rs, :].astype(F32))).astype(BF16)
        return carry

    lax.fori_loop(0, nb, out_body, 0, unroll=GLA_OUT_UNROLL)


def _gla(u3, us3, a2, bias, nw):
    B, S, _ = u3.shape
    DK, DV = GLA_DK, GLA_DV
    nc = S // GLA_CHUNK
    nb = nc // 2
    return pl.pallas_call(
        _gla_kernel,
        grid=(B, GLA_HEADS),
        in_specs=[
            pl.BlockSpec((1, S, DK), lambda b, h: (b, 0, U_GQ // DK + h)),
            pl.BlockSpec((1, S, DK), lambda b, h: (b, 0, U_GK // DK + h)),
            pl.BlockSpec((1, S, DV), lambda b, h: (b, 0, U_GV // DV + h)),
            pl.BlockSpec((1, S, DV), lambda b, h: (b, 0, U_GG // DV + h)),
            pl.BlockSpec((1, S, LANE), lambda b, h: (b, 0, 0)),
            pl.BlockSpec((LANE, 2 * DK), lambda b, h: (0, h)),
            pl.BlockSpec((1, 2 * DK), lambda b, h: (0, h)),
            pl.BlockSpec((1, DV), lambda b, h: (0, 0)),
        ],
        out_specs=pl.BlockSpec((1, S, DV), lambda b, h: (b, 0, h)),
        out_shape=jax.ShapeDtypeStruct((B, S, GLA_VAL_W), BF16),
        scratch_shapes=[
            pltpu.VMEM((S, DV), F32),
            pltpu.VMEM((S, 2 * DK), F32),
            pltpu.VMEM((2, S, DK), BF16),
            pltpu.VMEM((2, S, DK), BF16),
            pltpu.VMEM((2, S, 2 * DK), BF16),
            pltpu.VMEM((S, 4 * DK), BF16),
            pltpu.VMEM((2, nb, DV, 2 * DK), F32),
            pltpu.VMEM((2, nc, 8, DK), F32),
            pltpu.VMEM((nb, DV, 4 * DK), BF16),
        ],
        compiler_params=_cparams(("parallel", "parallel")),
        name="gla_scan",
    )(u3, u3, u3, u3, us3, a2, bias, nw)


def _na_bias_rows(rpb):
    H, R, C = rpb.shape
    n_pos = GRID_W - NA_WIN_W
    n_neg = GRID_W - NA_WIN_W + 2
    ext = jnp.concatenate([rpb[:, :, NA_WIN_W - 1:], jnp.repeat(rpb[:, :, C - 1:], n_pos, axis=2),
                           jnp.repeat(rpb[:, :, 0:1], n_neg, axis=2), rpb[:, :, 1:NA_WIN_W - 1]], axis=2)
    return jnp.pad(ext, ((0, 0), (0, 1), (0, 0))).reshape(H // 2, 2, R + 1, 2 * GRID_W)


def _na_kernel(flag_ref, q_ref, k_ref, v_ref, ext_ref, o_ref, vx_ref, tab_ref):
    S = q_ref.shape[1]
    rows = S // GRID_W
    win_h = NA_WIN_H
    nk = win_h * GRID_W

    vx_ref[:, 0:LANE] = v_ref[0]
    vx_ref[:, LANE:2 * LANE] = jnp.ones((S, LANE), BF16)
    bound_ok = flag_ref[0] != 0

    first_q = _iota2((GRID_W, LANE), 1) < NA_HEAD_DIM

    @pl.when(pl.program_id(1) == 0)
    def _():
        q_col = _iota2((GRID_W, 2 * GRID_W), 0)
        k_col = _iota2((GRID_W, 2 * GRID_W), 1) % GRID_W
        w_start = jnp.clip(q_col - NA_WIN_W // 2, 0, GRID_W - NA_WIN_W)
        in_window = (k_col >= w_start) & (k_col < w_start + NA_WIN_W)
        low = _iota2((GRID_W, 2 * GRID_W), 1) < GRID_W
        for hd in range(2):
            def skewed(rr, shift):
                row = jnp.broadcast_to(ext_ref[0, hd, rr:rr + 1, :], (GRID_W, 2 * GRID_W))
                return pltpu.roll(row, shift, axis=1, stride=1, stride_axis=0)

            for e in range(2 * NA_WIN_H - 2):
                t = jnp.where(low, skewed(e, 0), skewed(e + 1, GRID_W))
                tab_ref[e, hd * GRID_W:(hd + 1) * GRID_W, :] = jnp.where(in_window, t, -jnp.inf)

    def key_start(r):
        r0 = jnp.clip(r - win_h // 2, 0, rows - win_h)
        return r0, pl.multiple_of(r0 * GRID_W, GRID_W)

    def scores(r):
        r0, k0 = key_start(r)
        q = q_ref[0, pl.ds(pl.multiple_of(r * GRID_W, GRID_W), GRID_W), :]
        zero = jnp.zeros_like(q)
        qs = jnp.concatenate([jnp.where(first_q, q, zero), jnp.where(first_q, zero, q)], axis=0)
        rr0 = (win_h - 1) - (r - r0)
        bias = jnp.concatenate([tab_ref[rr0 + w] for w in range(0, win_h, 2)], axis=1)
        return _nt(qs, k_ref[0, pl.ds(k0, nk), :]) + bias

    def probs_exact(s):
        return jnp.exp2(s - jnp.max(s, axis=-1, keepdims=True)).astype(BF16)

    def probs_bounded(r):
        return jnp.exp2(scores(r)).astype(BF16)

    def attend(r, p):
        _, k0 = key_start(r)
        ox = _dot(p, vx_ref[pl.ds(k0, nk), :])
        o = ox[:, 0:LANE] / ox[:, LANE:2 * LANE]
        q0 = pl.multiple_of(r * GRID_W, GRID_W)
        o_ref[0, pl.ds(q0, GRID_W), :] = jnp.where(first_q, o[:GRID_W], o[GRID_W:]).astype(BF16)

    @pl.when(bound_ok)
    def _():
        U = NA_ROWS_PER_STEP

        def row_body(i, p_prev):
            r = i * U
            for j in range(U):
                attend(r - U + j, p_prev[j])
            return tuple(probs_bounded(r + j) for j in range(U))

        p_last = lax.fori_loop(1, rows // U, row_body, tuple(probs_bounded(j) for j in range(U)))
        for j in range(U):
            attend(rows - U + j, p_last[j])

    @pl.when(jnp.logical_not(bound_ok))
    def _():
        U = NA_ROWS_PER_STEP_EXACT

        def row_body(i, carry):
            s_cur, p_prev = carry
            r = i * U
            for j in range(U):
                attend(r - U + j, p_prev[j])
            p = tuple(probs_exact(s) for s in s_cur)
            s_next = tuple(scores(jnp.minimum(r + U + j, rows - 1)) for j in range(U))
            return s_next, p

        p0 = tuple(probs_exact(scores(j)) for j in range(U))
        s1 = tuple(scores(U + j) for j in range(U))
        _, p_last = lax.fori_loop(1, rows // U, row_body, (s1, p0))
        for j in range(U):
            attend(rows - U + j, p_last[j])


def _na_score_bound(rpb, q_norm_w, k_norm_w):
    H = rpb.shape[0]
    qk = (NA_HEAD_DIM ** 0.5 * LOG2E * (1.0 + NA_BOUND_SLACK)) * jnp.max(jnp.abs(q_norm_w)) * jnp.max(jnp.abs(k_norm_w))
    b_max = jnp.max(rpb.reshape(H, -1), axis=1) * LOG2E
    b_self = rpb[:, NA_WIN_H - 1, NA_WIN_W - 1] * LOG2E
    bound = qk + b_max
    flag = jnp.all(bound - (b_self - qk) <= NA_MAX_BOUND_GAP)
    return bound, flag.astype(jnp.int32).reshape(1)


def _na(u3, bias_rows, flag):
    B, S, _ = u3.shape
    assert S // GRID_W >= NA_WIN_H and NA_WIN_H % 2 == 0 and 2 * GRID_W == LANE
    return pl.pallas_call(
        _na_kernel,
        grid=(NA_HEADS // 2, B),
        in_specs=[
            pl.BlockSpec(memory_space=pltpu.SMEM),
            pl.BlockSpec((1, S, LANE), lambda h, b: (b, 0, U_NQ // LANE + h)),
            pl.BlockSpec((1, S, LANE), lambda h, b: (b, 0, U_NK // LANE + h)),
            pl.BlockSpec((1, S, LANE), lambda h, b: (b, 0, U_NV // LANE + h)),
            pl.BlockSpec((1, 2, 2 * NA_WIN_H, 2 * GRID_W), lambda h, b: (h, 0, 0, 0)),
        ],
        out_specs=pl.BlockSpec((1, S, LANE), lambda h, b: (b, 0, h)),
        out_shape=jax.ShapeDtypeStruct((B, S, NA_W), BF16),
        scratch_shapes=[
            pltpu.VMEM((S, 2 * LANE), BF16),
            pltpu.VMEM((2 * NA_WIN_H - 2, 2 * GRID_W, 2 * GRID_W), F32),
        ],
        compiler_params=_cparams(("parallel", "arbitrary")),
        name="na_attn",
    )(flag, u3, u3, u3, bias_rows)


def _merge_kernel(x_ref, ys_ref, yg_ref, yn_ref, gate_ref, ws_ref, wg_ref, wn_ref, wo_ref, o_ref):
    D = D_MODEL
    mixed = _sigmoid(gate_ref[:, 0:D].astype(F32)) * _dot(ys_ref[...], ws_ref[...])
    mixed += _sigmoid(gate_ref[:, D:2 * D].astype(F32)) * _dot(yg_ref[...], wg_ref[...])
    mixed += _sigmoid(gate_ref[:, 2 * D:3 * D].astype(F32)) * _dot(yn_ref[...], wn_ref[...])
    o_ref[...] = x_ref[...] + _dot(mixed.astype(BF16), wo_ref[...])


def _merge(x2, ys, yg, yn, u2, ws, wg, wn, wo, tm=512):
    T = x2.shape[0]
    D = D_MODEL
    row = lambda i: (i, 0)
    fixed = lambda i: (0, 0)
    return pl.pallas_call(
        _merge_kernel,
        grid=(T // tm,),
        in_specs=[
            pl.BlockSpec((tm, D), row),
            pl.BlockSpec((tm, D), row),
            pl.BlockSpec((tm, D), row),
            pl.BlockSpec((tm, D), row),
            pl.BlockSpec((tm, N_BRANCH * D), lambda i: (i, U_GATE // (N_BRANCH * D))),
            pl.BlockSpec((D, D), fixed),
            pl.BlockSpec((D, D), fixed),
            pl.BlockSpec((D, D), fixed),
            pl.BlockSpec((D, D), fixed),
        ],
        out_specs=pl.BlockSpec((tm, D), row),
        out_shape=jax.ShapeDtypeStruct((T, D), F32),
        compiler_params=_cparams(("parallel",)),
        name="merge",
    )(x2, ys, yg, yn, u2, ws, wg, wn, wo)


def _mlp_kernel(x_ref, nw_ref, w1_ref, w2_ref, o_ref, *, tf):
    x = x_ref[...]
    ms = jnp.mean(x * x, axis=-1, keepdims=True)
    h = (x * lax.rsqrt(ms + EPS) * nw_ref[...]).astype(BF16)
    acc = x
    for f in range(D_FF // tf):
        a = jnp.maximum(_dot(h, w1_ref[:, f * tf:(f + 1) * tf]), 0.0)
        acc = acc + _dot((a * a).astype(BF16), w2_ref[f * tf:(f + 1) * tf, :])
    o_ref[...] = acc


def _mlp(x2, nw, w1, w2, tm=512, tf=1024):
    T = x2.shape[0]
    D = D_MODEL
    return pl.pallas_call(
        functools.partial(_mlp_kernel, tf=tf),
        grid=(T // tm,),
        in_specs=[
            pl.BlockSpec((tm, D), lambda i: (i, 0)),
            pl.BlockSpec((1, D), lambda i: (0, 0)),
            pl.BlockSpec((D, D_FF), lambda i: (0, 0)),
            pl.BlockSpec((D_FF, D), lambda i: (0, 0)),
        ],
        out_specs=pl.BlockSpec((tm, D), lambda i: (i, 0)),
        out_shape=jax.ShapeDtypeStruct((T, D), F32),
        compiler_params=_cparams(("parallel",)),
        name="mlp",
    )(x2, nw, w1, w2)


def _pad_rows(w, start, total):
    return jnp.zeros((total, w.shape[1]), w.dtype).at[start:start + w.shape[0]].set(w)


def kernel(x, norm_mix_w, w_in, ssd_conv_w, ssd_conv_b, ssd_dt_bias_f, ssd_dt_bias_b, ssd_a_log_f,
           ssd_a_log_b, ssd_d, ssd_norm_w, gla_a2_f, gla_a2_bias_f, gla_a2_b, gla_a2_bias_b,
           gla_norm_w, na_q_norm_w, na_k_norm_w, na_rpb, w_branch_ssd, w_branch_gla, w_branch_na,
           w_out, norm_mlp_w, w_ff1, w_ff2):
    B, S, D = x.shape
    T = B * S
    depth = w_in.shape[0]
    rows = S // GRID_W
    x2 = x.reshape(T, D)
    for l in range(depth):
        w_big = jnp.concatenate([w_in[l][:, a:a + n] for a, n in _BIG_SEGS], axis=1).astype(BF16)
        w_small = _small_weight(w_in[l]).astype(BF16)
        hg = (SSD_GROUPS, SSD_HG)
        zeros_r = jnp.zeros((SSD_GROUPS, LANE - 2 * SSD_HG), F32)
        dt_bias = jnp.concatenate([ssd_dt_bias_f[l].reshape(hg), ssd_dt_bias_b[l].reshape(hg), zeros_r], axis=1)
        a_neg = jnp.concatenate([-jnp.exp(ssd_a_log_f[l]).reshape(hg), -jnp.exp(ssd_a_log_b[l]).reshape(hg),
                                 zeros_r], axis=1)
        prow3 = jnp.concatenate([dt_bias[:, None], a_neg[:, None], jnp.zeros((SSD_GROUPS, 6, LANE), F32)], axis=1)
        prow = prow3.reshape(SSD_GROUPS * 8, LANE)
        pcol = jnp.transpose(prow3[:, :, :SMALL_T_ROWS], (0, 2, 1))
        w_small_t = w_small.T.reshape(SSD_GROUPS, LANE, D)[:, :SMALL_T_ROWS].reshape(-1, D)
        drow = jnp.repeat(ssd_d[l], SSD_HEAD_DIM)[None, :]
        hk = (GLA_HEADS, 1, GLA_DK)
        a2 = jnp.concatenate([_pad_rows(gla_a2_f[l], SM_GAF, LANE).reshape((LANE,) + hk),
                              _pad_rows(gla_a2_b[l], SM_GAB, LANE).reshape((LANE,) + hk)],
                             axis=2).reshape(LANE, -1).astype(BF16)
        a2_bias = jnp.concatenate([gla_a2_bias_f[l].reshape(hk), gla_a2_bias_b[l].reshape(hk)],
                                  axis=1).reshape(1, -1)
        na_bound, na_flag = _na_score_bound(na_rpb[l], na_q_norm_w[l], na_k_norm_w[l])
        table = _na_bias_rows(na_rpb[l] * LOG2E - na_bound[:, None, None])
        q_row = jnp.tile(na_q_norm_w[l] * (NA_HEAD_DIM ** -0.5 * LOG2E), NA_HEADS)
        k_row = jnp.tile(na_k_norm_w[l], NA_HEADS)
        qkw = jnp.concatenate([q_row[None], k_row[None], jnp.zeros((6, NA_W), F32)], axis=0)

        u2, us2, ust = _inproj(x2, norm_mix_w[l][None, :], w_big, w_small, w_small_t, qkw)
        u3 = u2.reshape(B, S, U_WIDTH)
        us3 = us2.reshape(B, S, SMALL_W)
        xbc = _conv(u3, ssd_conv_w[l], ssd_conv_b[l][None, :])
        y_ssd = _ssd(xbc, u3, us3, ust, prow, pcol, drow, ssd_norm_w[l][None, :])
        y_gla = _gla(u3, us3, a2, a2_bias, gla_norm_w[l][None, :])
        y_na = _na(u3, table, na_flag)
        x2 = _merge(x2, y_ssd.reshape(T, -1), y_gla.reshape(T, -1), y_na.reshape(T, -1), u2,
                    w_branch_ssd[l].astype(BF16), w_branch_gla[l].astype(BF16),
                    w_branch_na[l].astype(BF16), w_out[l].astype(BF16))
        x2 = _mlp(x2, norm_mlp_w[l][None, :], w_ff1[l].astype(BF16), w_ff2[l].astype(BF16))
    return x2.reshape(B, S, D)
```

```python
import functools
import math

import jax
import jax.numpy as jnp
import numpy as np
from jax import lax
from jax.experimental import pallas as pl
from jax.experimental.pallas import tpu as pltpu

F32 = jnp.float32
BF16 = jnp.bfloat16

EPS = 1e-6
D_MODEL = 1024
GRID_W = 64

SSD_HEADS = 16
SSD_HEAD_DIM = 64
SSD_D_INNER = 1024
SSD_GROUPS = 2
SSD_STATE = 128
SSD_CONV = 5
SSD_CONV_DIM = 1536
SSD_CHUNK = 128
SSD_CHUNKS_PER_STEP = 2
SSD_GROUP_W = SSD_D_INNER // SSD_GROUPS
SSD_HG = SSD_HEADS // SSD_GROUPS

GLA_HEADS = 4
GLA_DK = 128
GLA_DV = 256
GLA_KEY_W = 512
GLA_VAL_W = 1024
GLA_GATE_RANK = 16
GLA_GATE_NORM = 16.0
GLA_CHUNK = 64
GLA_BLOCKS_PER_STEP = 2
GLA_OUT_UNROLL = 4

NA_HEADS = 16
NA_HEAD_DIM = 64
NA_W = 1024
NA_WIN_H = 8
NA_WIN_W = 16
NA_BATCH_PER_STEP = 4
NA_ROWS_PER_STEP = 8
NA_ROWS_PER_STEP_EXACT = 2
LOG2E = 1.4426950408889634
NA_BOUND_SLACK = 0.02
NA_MAX_BOUND_GAP = 90.0

N_BRANCH = 3
D_FF = 4096

IN_SIZES = (SSD_D_INNER, SSD_CONV_DIM, SSD_HEADS, SSD_HEADS,
            GLA_KEY_W, GLA_KEY_W, GLA_VAL_W, GLA_VAL_W, GLA_GATE_RANK, GLA_GATE_RANK,
            NA_W, NA_W, NA_W, N_BRANCH * D_MODEL)
_IN_OFF = np.concatenate([[0], np.cumsum(IN_SIZES)])
(_O_Z, _O_XBC, _O_DTF, _O_DTB, _O_GQ, _O_GK, _O_GV, _O_GG, _O_GAF, _O_GAB,
 _O_NQ, _O_NK, _O_NV, _O_GATE) = [int(v) for v in _IN_OFF[:-1]]

U_NQ = 0
U_NK = 1024
U_NV = 2048
U_GATE = 3072
U_Z = 6144
U_XBC = 7168
U_GQ = 8704
U_GK = 9216
U_GV = 9728
U_GG = 10752
U_WIDTH = 11776
INPROJ_COL_CHUNK = 1024
LANE = 128
VMEM_LIMIT = 56 * 1024 * 1024

SMALL_W = SSD_GROUPS * LANE
SM_DTF, SM_DTB, SM_GAF, SM_GAB = 0, 8, 16, 32
SMALL_T_ROWS = 2 * SSD_HG

_BIG_SEGS = ((_O_NQ, 6144), (_O_Z, 2560), (_O_GQ, 3072))


def _small_weight(w):
    blocks = []
    for g in range(SSD_GROUPS):
        h0 = g * SSD_HG
        cols = [w[:, _O_DTF + h0:_O_DTF + h0 + SSD_HG], w[:, _O_DTB + h0:_O_DTB + h0 + SSD_HG]]
        used = 2 * SSD_HG
        if g == 0:
            cols += [w[:, _O_GAF:_O_GAF + GLA_GATE_RANK], w[:, _O_GAB:_O_GAB + GLA_GATE_RANK]]
            used += 2 * GLA_GATE_RANK
        cols.append(jnp.zeros((w.shape[0], LANE - used), w.dtype))
        blocks += cols
    return jnp.concatenate(blocks, axis=1)


def _cparams(sem, vmem=VMEM_LIMIT):
    return pltpu.CompilerParams(dimension_semantics=sem, vmem_limit_bytes=vmem)


def _sigmoid(x):
    return 1.0 / (1.0 + jnp.exp(-x))


def _silu(x):
    return x * _sigmoid(x)


def _softplus(x):
    return jnp.maximum(x, 0.0) + jnp.log1p(jnp.exp(-jnp.abs(x)))


def _log2_sigmoid(x):
    t = x * (-LOG2E)
    return -(jnp.maximum(t, 0.0) + jnp.log2(1.0 + jnp.exp2(-jnp.abs(t))))


def _nt(a, b):
    return lax.dot_general(a, b, (((1,), (1,)), ((), ())), preferred_element_type=F32)


def _tn(a, b):
    return lax.dot_general(a, b, (((0,), (0,)), ((), ())), preferred_element_type=F32)


def _dot(a, b):
    return jnp.dot(a, b, preferred_element_type=F32)


def _dot_exact(a, b):
    return jnp.dot(a, b, preferred_element_type=F32, precision=lax.Precision.HIGHEST)


def _iota2(shape, dim):
    return lax.broadcasted_iota(jnp.int32, shape, dim)


def _wperm_kernel(w_ref, o_ref):
    off = 0
    for a, n in _BIG_SEGS:
        o_ref[:, off:off + n] = w_ref[:, a:a + n].astype(BF16)
        off += n


def _permute_weight(w, tr=64):
    rows, n_in = w.shape
    return pl.pallas_call(
        _wperm_kernel,
        grid=(rows // tr,),
        in_specs=[pl.BlockSpec((tr, n_in), lambda i: (i, 0))],
        out_specs=pl.BlockSpec((tr, U_WIDTH), lambda i: (i, 0)),
        out_shape=jax.ShapeDtypeStruct((rows, U_WIDTH), BF16),
        compiler_params=_cparams(("parallel",)),
        name="w_permute",
    )(w)


def _head_rms(r, w_row):
    G = 2 * LANE
    er = _iota2((G, G), 0) // NA_HEAD_DIM
    ec = _iota2((G, G), 1) // NA_HEAD_DIM
    e_blk = (er == ec).astype(BF16)
    outs = []
    for a in range(0, r.shape[1], G):
        x = r[:, a:a + G]
        ms = _dot((x * x).astype(BF16), e_blk) * (1.0 / NA_HEAD_DIM)
        outs.append(x * lax.rsqrt(ms + EPS))
    return jnp.concatenate(outs, axis=1) * w_row


def _inproj_kernel(x_ref, nw_ref, w_ref, ws_ref, wst_ref, qkw_ref, u_ref, us_ref, ust_ref, h_ref):
    tn = u_ref.shape[1]
    chunks = [(a, min(a + INPROJ_COL_CHUNK, tn)) for a in range(0, tn, INPROJ_COL_CHUNK)]

    @pl.when(pl.program_id(1) == 0)
    def _():
        x = x_ref[...]
        ms = jnp.mean(x * x, axis=-1, keepdims=True)
        h = (x * lax.rsqrt(ms + EPS) * nw_ref[...]).astype(BF16)
        h_ref[...] = h
        us_ref[...] = _dot(h, ws_ref[...])
        ust_ref[...] = _nt(wst_ref[...], h)
        for c, (a, b) in enumerate(chunks):
            r = _dot(h, w_ref[:, a:b])
            if c < 2:
                r = _head_rms(r, qkw_ref[c:c + 1, :])
            u_ref[:, a:b] = r.astype(BF16)

    @pl.when(pl.program_id(1) != 0)
    def _():
        for a, b in chunks:
            u_ref[:, a:b] = _dot(h_ref[...], w_ref[:, a:b]).astype(BF16)


def _inproj(x2, nw, w_big, w_small, w_small_t, qkw, tm=1024, tn=U_WIDTH // 4):
    assert (U_NQ, U_NK) == (0, INPROJ_COL_CHUNK) and NA_W == INPROJ_COL_CHUNK
    T = x2.shape[0]
    return pl.pallas_call(
        _inproj_kernel,
        grid=(T // tm, U_WIDTH // tn),
        in_specs=[
            pl.BlockSpec((tm, D_MODEL), lambda i, j: (i, 0)),
            pl.BlockSpec((1, D_MODEL), lambda i, j: (0, 0)),
            pl.BlockSpec((D_MODEL, tn), lambda i, j: (0, j)),
            pl.BlockSpec((D_MODEL, SMALL_W), lambda i, j: (0, 0)),
            pl.BlockSpec((SSD_GROUPS * SMALL_T_ROWS, D_MODEL), lambda i, j: (0, 0)),
            pl.BlockSpec((8, NA_W), lambda i, j: (0, 0)),
        ],
        out_specs=[
            pl.BlockSpec((tm, tn), lambda i, j: (i, j)),
            pl.BlockSpec((tm, SMALL_W), lambda i, j: (i, 0)),
            pl.BlockSpec((SSD_GROUPS * SMALL_T_ROWS, tm), lambda i, j: (0, i)),
        ],
        out_shape=[
            jax.ShapeDtypeStruct((T, U_WIDTH), BF16),
            jax.ShapeDtypeStruct((T, SMALL_W), F32),
            jax.ShapeDtypeStruct((SSD_GROUPS * SMALL_T_ROWS, T), F32),
        ],
        scratch_shapes=[pltpu.VMEM((tm, D_MODEL), BF16)],
        compiler_params=_cparams(("parallel", "arbitrary")),
        name="inproj",
    )(x2, nw, w_big, w_small, w_small_t, qkw)


def _conv_kernel(u_ref, w_ref, b_ref, o_ref):
    x = u_ref[0].astype(F32)
    S = x.shape[0]
    row = _iota2(x.shape, 0)
    acc = jnp.zeros_like(x) + b_ref[...]
    pad = SSD_CONV // 2
    for k in range(SSD_CONV):
        d = k - pad
        if d == 0:
            xs = x
        else:
            xs = pltpu.roll(x, (-d) % S, axis=0)
            valid = (row + d >= 0) & (row + d < S)
            xs = jnp.where(valid, xs, 0.0)
        acc = acc + w_ref[k:k + 1, :] * xs
    o_ref[0] = _silu(acc).astype(BF16)


def _conv(u3, conv_w, conv_b, tc=256):
    B, S, _ = u3.shape
    nblk = SSD_CONV_DIM // tc
    off = U_XBC // tc
    return pl.pallas_call(
        _conv_kernel,
        grid=(B, nblk),
        in_specs=[
            pl.BlockSpec((1, S, tc), lambda b, c: (b, 0, off + c)),
            pl.BlockSpec((SSD_CONV, tc), lambda b, c: (0, c)),
            pl.BlockSpec((1, tc), lambda b, c: (0, c)),
        ],
        out_specs=pl.BlockSpec((1, S, tc), lambda b, c: (b, 0, c)),
        out_shape=jax.ShapeDtypeStruct((B, S, SSD_CONV_DIM), BF16),
        compiler_params=_cparams(("parallel", "parallel")),
        name="ssd_conv",
    )(u3, conv_w, conv_b)


def _split_hi_lo(v):
    hi = v.astype(BF16)
    lo = (v - hi.astype(F32)).astype(BF16)
    return jnp.concatenate([hi, lo], axis=1)


def _split3(v, axis):
    hi = v.astype(BF16)
    r1 = v - hi.astype(F32)
    mid = r1.astype(BF16)
    lo = (r1 - mid.astype(F32)).astype(BF16)
    return jnp.concatenate([hi, mid, lo], axis=axis)


def _ssd_kernel(x_ref, b_ref, c_ref, z_ref, us_ref, ust_ref, prow_ref, pcol_ref, drow_ref, nw_ref,
                o_ref, acc_ref, cumc_ref, cumr_ref, wdt_ref, dec_ref, xs_ref, decx_ref, sst_ref, st_ref):
    L = SSD_CHUNK
    S = x_ref.shape[1]
    nc = S // L
    W = SSD_GROUP_W
    HG = SSD_HG
    R = 2 * HG

    ii = _iota2((L, L), 0)
    jj = _iota2((L, L), 1)
    tril = ii >= jj
    triu = jj >= ii
    tril_b = tril.astype(BF16)
    triu_b = triu.astype(BF16)

    er = _iota2((LANE, 2 * W), 0)
    ec = _iota2((LANE, 2 * W), 1)
    e = (er == jnp.where(ec < W, ec // SSD_HEAD_DIM, (ec - W) // SSD_HEAD_DIM + HG)).astype(BF16)
    e2 = jnp.concatenate([e, e], axis=0)

    bias_row = prow_ref[0:1, :]
    a_row = prow_ref[1:2, :]
    bias_col = pcol_ref[0, :, 0:1]
    a_col = pcol_ref[0, :, 1:2]

    lane_fwd = _iota2((L, LANE), 1) < HG
    row_fwd = _iota2((R, L), 0) < HG
    lane_half = _iota2((L, LANE), 1) < SSD_HEAD_DIM

    def decay_sums(c):
        rs = pl.ds(pl.multiple_of(c * L, L), L)
        dt_c = _softplus(us_ref[0, rs, :] + bias_row)
        a_c = dt_c * a_row
        dt_r = _softplus(ust_ref[:, rs] + bias_col)
        a_r = dt_r * a_col
        return dt_c, a_c, _dot(tril_b, _split3(a_c, 1)), dt_r, a_r, _dot(_split3(a_r, 0), triu_b)

    def decay_store(c, sums):
        rs = pl.ds(pl.multiple_of(c * L, L), L)
        dt_c, a_c, pp, dt_r, a_r, pr = sums
        p_c = pp[:, 0:LANE] + pp[:, LANE:2 * LANE] + pp[:, 2 * LANE:3 * LANE]
        tot_c = p_c[L - 1:L, :]
        cum_c = jnp.where(lane_fwd, p_c, tot_c - p_c + a_c)
        cumc_ref[rs, :] = cum_c * LOG2E
        wdt_ref[rs, :] = jnp.exp(tot_c - cum_c) * dt_c
        dec_ref[c] = jnp.broadcast_to(jnp.exp(tot_c), (16, LANE))
        p_r = pr[0:R] + pr[R:2 * R] + pr[2 * R:3 * R]
        tot_r = p_r[:, L - 1:L]
        cum_r = jnp.where(row_fwd, p_r, tot_r - p_r + a_r)
        cumr_ref[:, rs] = (cum_r - jnp.log(dt_r)) * LOG2E

    def intra_chunk(c):
        rs = pl.ds(pl.multiple_of(c * L, L), L)
        x_c = x_ref[0, rs, :]
        b_c = b_ref[0, rs, :]
        cb = _nt(c_ref[0, rs, :], b_c)
        ex = _dot(_split_hi_lo(jnp.concatenate([wdt_ref[rs, :], dec_ref[c]], axis=0)), e2)
        decx_ref[c] = ex[L:L + 8]
        xf = x_c.astype(F32)
        xw = jnp.concatenate([xf * ex[0:L, 0:W], xf * ex[0:L, W:2 * W]], axis=1).astype(BF16)
        xs_ref[c] = _tn(b_c, xw)
        cum_c = cumc_ref[rs, :]
        cum_r = cumr_ref[:, rs]
        for hp in range(HG // 2):
            ms = []
            for hh in range(2):
                hf = 2 * hp + hh
                hb = HG + 2 * hp + hh
                decf = jnp.exp2(jnp.where(tril, cum_c[:, hf:hf + 1] - cum_r[hf:hf + 1, :], -jnp.inf))
                decb = jnp.exp2(jnp.where(triu, cum_c[:, hb:hb + 1] - cum_r[hb:hb + 1, :], -jnp.inf))
                ms.append((cb * (decf + decb)).astype(BF16))
            m2 = jnp.concatenate(ms, axis=1)
            xp = x_c[:, hp * LANE:(hp + 1) * LANE]
            xz = jnp.zeros_like(xp)
            x2 = jnp.concatenate([jnp.where(lane_half, xp, xz), jnp.where(lane_half, xz, xp)], axis=0)
            acc_ref[rs, hp * LANE:(hp + 1) * LANE] = _dot(m2, x2)

    G = SSD_CHUNKS_PER_STEP
    n_steps = nc // G

    def fused_body(i, carry):
        nxt = jnp.minimum(i + 1, n_steps - 1)
        sums = [decay_sums(G * nxt + j) for j in range(G)]
        for j in range(G):
            intra_chunk(G * i + j)
        for j in range(G):
            decay_store(G * nxt + j, sums[j])
        return carry

    for j in range(G):
        decay_store(j, decay_sums(j))
    lax.fori_loop(0, n_steps, fused_body, 0)

    st_ref[...] = jnp.zeros_like(st_ref)

    def state_body(i, carry):
        for d, c in enumerate((i, nc - 1 - i)):
            ls = slice(d * W, (d + 1) * W)
            s_prev = st_ref[:, ls]
            sst_ref[c, :, ls] = s_prev.astype(BF16)
            st_ref[:, ls] = s_prev * decx_ref[c, 0:1, ls] + xs_ref[c, :, ls]
        return carry

    lax.fori_loop(0, nc, state_body, 0)

    def out_body(c, carry):
        rs = pl.ds(pl.multiple_of(c * L, L), L)
        yo = _dot(c_ref[0, rs, :], sst_ref[c])
        sc = _dot(_split_hi_lo(jnp.exp2(cumc_ref[rs, :])), e2)
        xf = x_ref[0, rs, :].astype(F32)
        y = acc_ref[rs, :] + yo[:, 0:W] * sc[:, 0:W] + yo[:, W:2 * W] * sc[:, W:2 * W] + xf * drow_ref[...]
        y = y * _silu(z_ref[0, rs, :].astype(F32))
        ms = jnp.mean(y * y, axis=-1, keepdims=True)
        o_ref[0, rs, :] = (y * lax.rsqrt(ms + EPS) * nw_ref[...]).astype(BF16)
        return carry

    lax.fori_loop(0, nc, out_body, 0, unroll=2)


def _ssd(xbc, u3, us3, ust, prow, pcol, drow, nw):
    B, S, _ = xbc.shape
    W = SSD_GROUP_W
    N = SSD_STATE
    nc = S // SSD_CHUNK
    return pl.pallas_call(
        _ssd_kernel,
        grid=(B, SSD_GROUPS),
        in_specs=[
            pl.BlockSpec((1, S, W), lambda b, g: (b, 0, g)),
            pl.BlockSpec((1, S, N), lambda b, g: (b, 0, SSD_D_INNER // N + g)),
            pl.BlockSpec((1, S, N), lambda b, g: (b, 0, SSD_D_INNER // N + SSD_GROUPS + g)),
            pl.BlockSpec((1, S, W), lambda b, g: (b, 0, U_Z // W + g)),
            pl.BlockSpec((1, S, LANE), lambda b, g: (b, 0, g)),
            pl.BlockSpec((SMALL_T_ROWS, S), lambda b, g: (g, b)),
            pl.BlockSpec((8, LANE), lambda b, g: (g, 0)),
            pl.BlockSpec((1, SMALL_T_ROWS, 8), lambda b, g: (g, 0, 0)),
            pl.BlockSpec((1, W), lambda b, g: (0, g)),
            pl.BlockSpec((1, W), lambda b, g: (0, g)),
        ],
        out_specs=pl.BlockSpec((1, S, W), lambda b, g: (b, 0, g)),
        out_shape=jax.ShapeDtypeStruct((B, S, SSD_D_INNER), BF16),
        scratch_shapes=[
            pltpu.VMEM((S, W), F32),
            pltpu.VMEM((S, LANE), F32),
            pltpu.VMEM((SMALL_T_ROWS, S), F32),
            pltpu.VMEM((S, LANE), F32),
            pltpu.VMEM((nc, 16, LANE), F32),
            pltpu.VMEM((nc, N, 2 * W), F32),
            pltpu.VMEM((nc, 8, 2 * W), F32),
            pltpu.VMEM((nc, N, 2 * W), BF16),
            pltpu.VMEM((N, 2 * W), F32),
        ],
        compiler_params=_cparams(("parallel", "parallel")),
        name="ssd_scan",
    )(xbc, xbc, xbc, u3, us3, ust, prow, pcol, drow, nw)


def _gla_kernel(q_ref, k_ref, v_ref, gg_ref, us_ref, a2_ref, bias_ref, nw_ref,
                o_ref, acc_ref, g_ref, qd_ref, kd_ref, kdp_ref, qcat_ref, x_ref, dec_ref, sst_ref):
    L = GLA_CHUNK
    BL = 2 * L
    DK = GLA_DK
    S = q_ref.shape[1]
    nb = S // BL
    scale = DK ** -0.5

    ii = _iota2((BL, BL), 0)
    jj = _iota2((BL, BL), 1)
    same = (ii // L) == (jj // L)
    masks = (same & (ii >= jj), same & (jj >= ii))
    tri2 = masks[0].astype(BF16)
    par_row = _iota2((BL, DK), 0) // L

    ga = us_ref[0].astype(BF16)
    g_ref[...] = _log2_sigmoid(_dot(ga, a2_ref[...]) + bias_ref[...]) * (1.0 / GLA_GATE_NORM)

    def decay_sums(i):
        g = g_ref[pl.ds(pl.multiple_of(i * BL, BL), BL), :]
        hi = g.astype(BF16)
        r1 = g - hi.astype(F32)
        mid = r1.astype(BF16)
        lo = (r1 - mid.astype(F32)).astype(BF16)
        return g, _dot(tri2, jnp.concatenate([hi, mid, lo], axis=1))

    def decay_block(i, sums):
        rs = pl.ds(pl.multiple_of(i * BL, BL), BL)
        g, pp = sums
        p = pp[:, 0:2 * DK] + pp[:, 2 * DK:4 * DK] + pp[:, 4 * DK:6 * DK]
        q_c = q_ref[0, rs, :].astype(F32) * scale
        k_c = k_ref[0, rs, :].astype(F32)
        zero = jnp.zeros((BL, DK), BF16)
        for d in range(2):
            p_d = p[:, d * DK:(d + 1) * DK]
            tot = jnp.where(par_row == 0, p_d[L - 1:L, :], p_d[BL - 1:BL, :])
            b = p_d if d == 0 else tot - p_d + g[:, DK:]
            qd = (q_c * jnp.exp2(b)).astype(BF16)
            kdec = (k_c * jnp.exp2(tot - b)).astype(BF16)
            qd_ref[d, rs, :] = qd
            kd_ref[d, rs, :] = (k_c * jnp.exp2(-b)).astype(BF16)
            for par in range(2):
                sel = par_row == par
                kdp_ref[d, rs, par * DK:(par + 1) * DK] = jnp.where(sel, kdec, zero)
                qcat_ref[rs, (2 * d + par) * DK:(2 * d + par + 1) * DK] = jnp.where(sel, qd, zero)
                last = (par + 1) * L - 1
                dec_ref[d, 2 * i + par] = jnp.broadcast_to(jnp.exp2(p_d[last:last + 1, :]), (8, DK))

    G = GLA_BLOCKS_PER_STEP
    n_groups = nb // G

    def intra_group(i):
        blks = [G * i + j for j in range(G)]
        rss = [pl.ds(pl.multiple_of(b * BL, BL), BL) for b in blks]
        vs = [v_ref[0, rs, :] for rs in rss]
        atts = [[_nt(qd_ref[d, rs, :], kd_ref[d, rs, :]) for d in range(2)] for rs in rss]
        for j, b in enumerate(blks):
            for d in range(2):
                x_ref[d, b] = _tn(vs[j], kdp_ref[d, rss[j], :])
        for j in range(G):
            att = jnp.where(masks[0], atts[j][0], 0.0) + jnp.where(masks[1], atts[j][1], 0.0)
            acc_ref[rss[j], :] = _dot(att.astype(BF16), vs[j])

    def fused_body(i, carry):
        nxt = jnp.minimum(i + 1, n_groups - 1)
        sums = [decay_sums(G * nxt + j) for j in range(G)]
        intra_group(i)
        for j in range(G):
            decay_block(G * nxt + j, sums[j])
        return carry

    for j in range(G):
        decay_block(j, decay_sums(j))
    lax.fori_loop(0, n_groups, fused_body, 0)

    def state_body(d):
        def body(i, s):
            b = i if d == 0 else nb - 1 - i
            for par in ((0, 1) if d == 0 else (1, 0)):
                lane0 = (2 * d + par) * DK
                sst_ref[b, :, lane0:lane0 + DK] = s.astype(BF16)
                s = s * dec_ref[d, 2 * b + par, 0:1, :] + x_ref[d, b, :, par * DK:(par + 1) * DK]
            return s
        return body

    for d in range(2):
        lax.fori_loop(0, nb, state_body(d), jnp.zeros((GLA_DV, DK), F32))

    def out_body(b, carry):
        rs = pl.ds(pl.multiple_of(b * BL, BL), BL)
        o = acc_ref[rs, :] + _nt(qcat_ref[rs, :], sst_ref[b])
        ms = jnp.mean(o * o, axis=-1, keepdims=True)
        o = o * lax.rsqrt(ms + EPS) * nw_ref[...]
        o_ref[0, rs, :] = (o * _silu(gg_ref[0, rs, :].astype(F32))).astype(BF16)
        return carry

    lax.fori_loop(0, nb, out_body, 0, unroll=GLA_OUT_UNROLL)


def _gla(u3, us3, a2, bias, nw):
    B, S, _ = u3.shape
    DK, DV = GLA_DK, GLA_DV
    nc = S // GLA_CHUNK
    nb = nc // 2
    return pl.pallas_call(
        _gla_kernel,
        grid=(B, GLA_HEADS),
        in_specs=[
            pl.BlockSpec((1, S, DK), lambda b, h: (b, 0, U_GQ // DK + h)),
            pl.BlockSpec((1, S, DK), lambda b, h: (b, 0, U_GK // DK + h)),
            pl.BlockSpec((1, S, DV), lambda b, h: (b, 0, U_GV // DV + h)),
            pl.BlockSpec((1, S, DV), lambda b, h: (b, 0, U_GG // DV + h)),
            pl.BlockSpec((1, S, LANE), lambda b, h: (b, 0, 0)),
            pl.BlockSpec((LANE, 2 * DK), lambda b, h: (0, h)),
            pl.BlockSpec((1, 2 * DK), lambda b, h: (0, h)),
            pl.BlockSpec((1, DV), lambda b, h: (0, 0)),
        ],
        out_specs=pl.BlockSpec((1, S, DV), lambda b, h: (b, 0, h)),
        out_shape=jax.ShapeDtypeStruct((B, S, GLA_VAL_W), BF16),
        scratch_shapes=[
            pltpu.VMEM((S, DV), F32),
            pltpu.VMEM((S, 2 * DK), F32),
            pltpu.VMEM((2, S, DK), BF16),
            pltpu.VMEM((2, S, DK), BF16),
            pltpu.VMEM((2, S, 2 * DK), BF16),
            pltpu.VMEM((S, 4 * DK), BF16),
            pltpu.VMEM((2, nb, DV, 2 * DK), F32),
            pltpu.VMEM((2, nc, 8, DK), F32),
            pltpu.VMEM((nb, DV, 4 * DK), BF16),
        ],
        compiler_params=_cparams(("parallel", "parallel")),
        name="gla_scan",
    )(u3, u3, u3, u3, us3, a2, bias, nw)


def _na_bias_rows(rpb):
    H, R, C = rpb.shape
    n_pos = GRID_W - NA_WIN_W
    n_neg = GRID_W - NA_WIN_W + 2
    ext = jnp.concatenate([rpb[:, :, NA_WIN_W - 1:], jnp.repeat(rpb[:, :, C - 1:], n_pos, axis=2),
                           jnp.repeat(rpb[:, :, 0:1], n_neg, axis=2), rpb[:, :, 1:NA_WIN_W - 1]], axis=2)
    return jnp.pad(ext, ((0, 0), (0, 1), (0, 0))).reshape(H // 2, 2, R + 1, 2 * GRID_W)


def _na_kernel(flag_ref, q_ref, k_ref, v_ref, ext_ref, o_ref, vx_ref, tab_ref):
    NB, S, _ = q_ref.shape
    rows = S // GRID_W
    total = NB * rows
    win_h = NA_WIN_H
    nk = win_h * GRID_W

    vx_ref[:, :, 0:LANE] = v_ref[...]
    vx_ref[:, :, LANE:2 * LANE] = jnp.ones((NB, S, LANE), BF16)
    bound_ok = flag_ref[0] != 0

    first_q = _iota2((GRID_W, LANE), 1) < NA_HEAD_DIM

    @pl.when(pl.program_id(1) == 0)
    def _():
        q_col = _iota2((GRID_W, 2 * GRID_W), 0)
        k_col = _iota2((GRID_W, 2 * GRID_W), 1) % GRID_W
        w_start = jnp.clip(q_col - NA_WIN_W // 2, 0, GRID_W - NA_WIN_W)
        in_window = (k_col >= w_start) & (k_col < w_start + NA_WIN_W)
        low = _iota2((GRID_W, 2 * GRID_W), 1) < GRID_W
        for hd in range(2):
            def skewed(rr, shift):
                row = jnp.broadcast_to(ext_ref[0, hd, rr:rr + 1, :], (GRID_W, 2 * GRID_W))
                return pltpu.roll(row, shift, axis=1, stride=1, stride_axis=0)

            for e in range(2 * NA_WIN_H - 2):
                t = jnp.where(low, skewed(e, 0), skewed(e + 1, GRID_W))
                tab_ref[e, hd * GRID_W:(hd + 1) * GRID_W, :] = jnp.where(in_window, t, -jnp.inf)

    def locate(r):
        bi = r // rows
        rl = r - bi * rows
        r0 = jnp.clip(rl - win_h // 2, 0, rows - win_h)
        return bi, pl.multiple_of(rl * GRID_W, GRID_W), rl - r0, pl.multiple_of(r0 * GRID_W, GRID_W)

    def scores(r):
        bi, q0, delta, k0 = locate(r)
        q = q_ref[bi, pl.ds(q0, GRID_W), :]
        zero = jnp.zeros_like(q)
        qs = jnp.concatenate([jnp.where(first_q, q, zero), jnp.where(first_q, zero, q)], axis=0)
        rr0 = (win_h - 1) - delta
        bias = jnp.concatenate([tab_ref[rr0 + w] for w in range(0, win_h, 2)], axis=1)
        return _nt(qs, k_ref[bi, pl.ds(k0, nk), :]) + bias

    def probs_exact(s):
        return jnp.exp2(s - jnp.max(s, axis=-1, keepdims=True)).astype(BF16)

    def probs_bounded(r):
        return jnp.exp2(scores(r)).astype(BF16)

    def attend(r, p):
        bi, q0, _, k0 = locate(r)
        ox = _dot(p, vx_ref[bi, pl.ds(k0, nk), :])
        o = ox[:, 0:LANE] / ox[:, LANE:2 * LANE]
        o_ref[bi, pl.ds(q0, GRID_W), :] = jnp.where(first_q, o[:GRID_W], o[GRID_W:]).astype(BF16)

    @pl.when(bound_ok)
    def _():
        U = NA_ROWS_PER_STEP

        def row_body(i, p_prev):
            r = i * U
            for j in range(U):
                attend(r - U + j, p_prev[j])
            return tuple(probs_bounded(r + j) for j in range(U))

        p_last = lax.fori_loop(1, total // U, row_body, tuple(probs_bounded(j) for j in range(U)))
        for j in range(U):
            attend(total - U + j, p_last[j])

    @pl.when(jnp.logical_not(bound_ok))
    def _():
        U = NA_ROWS_PER_STEP_EXACT

        def row_body(i, carry):
            s_cur, p_prev = carry
            r = i * U
            for j in range(U):
                attend(r - U + j, p_prev[j])
            p = tuple(probs_exact(s) for s in s_cur)
            s_next = tuple(scores(jnp.minimum(r + U + j, total - 1)) for j in range(U))
            return s_next, p

        p0 = tuple(probs_exact(scores(j)) for j in range(U))
        s1 = tuple(scores(U + j) for j in range(U))
        _, p_last = lax.fori_loop(1, total // U, row_body, (s1, p0))
        for j in range(U):
            attend(total - U + j, p_last[j])


def _na_score_bound(rpb, q_norm_w, k_norm_w):
    H = rpb.shape[0]
    qk = (NA_HEAD_DIM ** 0.5 * LOG2E * (1.0 + NA_BOUND_SLACK)) * jnp.max(jnp.abs(q_norm_w)) * jnp.max(jnp.abs(k_norm_w))
    b_max = jnp.max(rpb.reshape(H, -1), axis=1) * LOG2E
    b_self = rpb[:, NA_WIN_H - 1, NA_WIN_W - 1] * LOG2E
    bound = qk + b_max
    flag = jnp.all(bound - (b_self - qk) <= NA_MAX_BOUND_GAP)
    return bound, flag.astype(jnp.int32).reshape(1)


def _na(u3, bias_rows, flag):
    B, S, _ = u3.shape
    assert S // GRID_W >= NA_WIN_H and NA_WIN_H % 2 == 0 and 2 * GRID_W == LANE
    nb = math.gcd(B, NA_BATCH_PER_STEP)
    assert (nb * (S // GRID_W)) % NA_ROWS_PER_STEP == 0
    return pl.pallas_call(
        _na_kernel,
        grid=(NA_HEADS // 2, B // nb),
        in_specs=[
            pl.BlockSpec(memory_space=pltpu.SMEM),
            pl.BlockSpec((nb, S, LANE), lambda h, b: (b, 0, U_NQ // LANE + h)),
            pl.BlockSpec((nb, S, LANE), lambda h, b: (b, 0, U_NK // LANE + h)),
            pl.BlockSpec((nb, S, LANE), lambda h, b: (b, 0, U_NV // LANE + h)),
            pl.BlockSpec((1, 2, 2 * NA_WIN_H, 2 * GRID_W), lambda h, b: (h, 0, 0, 0)),
        ],
        out_specs=pl.BlockSpec((nb, S, LANE), lambda h, b: (b, 0, h)),
        out_shape=jax.ShapeDtypeStruct((B, S, NA_W), BF16),
        scratch_shapes=[
            pltpu.VMEM((nb, S, 2 * LANE), BF16),
            pltpu.VMEM((2 * NA_WIN_H - 2, 2 * GRID_W, 2 * GRID_W), F32),
        ],
        compiler_params=_cparams(("parallel", "arbitrary")),
        name="na_attn",
    )(flag, u3, u3, u3, bias_rows)


def _merge_kernel(x_ref, ys_ref, yg_ref, yn_ref, gate_ref, ws_ref, wg_ref, wn_ref, wo_ref, o_ref):
    D = D_MODEL
    mixed = _sigmoid(gate_ref[:, 0:D].astype(F32)) * _dot(ys_ref[...], ws_ref[...])
    mixed += _sigmoid(gate_ref[:, D:2 * D].astype(F32)) * _dot(yg_ref[...], wg_ref[...])
    mixed += _sigmoid(gate_ref[:, 2 * D:3 * D].astype(F32)) * _dot(yn_ref[...], wn_ref[...])
    o_ref[...] = x_ref[...] + _dot(mixed.astype(BF16), wo_ref[...])


def _merge(x2, ys, yg, yn, u2, ws, wg, wn, wo, tm=512):
    T = x2.shape[0]
    D = D_MODEL
    row = lambda i: (i, 0)
    fixed = lambda i: (0, 0)
    return pl.pallas_call(
        _merge_kernel,
        grid=(T // tm,),
        in_specs=[
            pl.BlockSpec((tm, D), row),
            pl.BlockSpec((tm, D), row),
            pl.BlockSpec((tm, D), row),
            pl.BlockSpec((tm, D), row),
            pl.BlockSpec((tm, N_BRANCH * D), lambda i: (i, U_GATE // (N_BRANCH * D))),
            pl.BlockSpec((D, D), fixed),
            pl.BlockSpec((D, D), fixed),
            pl.BlockSpec((D, D), fixed),
            pl.BlockSpec((D, D), fixed),
        ],
        out_specs=pl.BlockSpec((tm, D), row),
        out_shape=jax.ShapeDtypeStruct((T, D), F32),
        compiler_params=_cparams(("parallel",)),
        name="merge",
    )(x2, ys, yg, yn, u2, ws, wg, wn, wo)


def _mlp_kernel(x_ref, nw_ref, w1_ref, w2_ref, o_ref, *, tf):
    x = x_ref[...]
    ms = jnp.mean(x * x, axis=-1, keepdims=True)
    h = (x * lax.rsqrt(ms + EPS) * nw_ref[...]).astype(BF16)
    acc = x
    for f in range(D_FF // tf):
        a = jnp.maximum(_dot(h, w1_ref[:, f * tf:(f + 1) * tf]), 0.0)
        acc = acc + _dot((a * a).astype(BF16), w2_ref[f * tf:(f + 1) * tf, :])
    o_ref[...] = acc


def _mlp(x2, nw, w1, w2, tm=512, tf=1024):
    T = x2.shape[0]
    D = D_MODEL
    return pl.pallas_call(
        functools.partial(_mlp_kernel, tf=tf),
        grid=(T // tm,),
        in_specs=[
            pl.BlockSpec((tm, D), lambda i: (i, 0)),
            pl.BlockSpec((1, D), lambda i: (0, 0)),
            pl.BlockSpec((D, D_FF), lambda i: (0, 0)),
            pl.BlockSpec((D_FF, D), lambda i: (0, 0)),
        ],
        out_specs=pl.BlockSpec((tm, D), lambda i: (i, 0)),
        out_shape=jax.ShapeDtypeStruct((T, D), F32),
        compiler_params=_cparams(("parallel",)),
        name="mlp",
    )(x2, nw, w1, w2)


def _pad_rows(w, start, total):
    return jnp.zeros((total, w.shape[1]), w.dtype).at[start:start + w.shape[0]].set(w)


def kernel(x, norm_mix_w, w_in, ssd_conv_w, ssd_conv_b, ssd_dt_bias_f, ssd_dt_bias_b, ssd_a_log_f,
           ssd_a_log_b, ssd_d, ssd_norm_w, gla_a2_f, gla_a2_bias_f, gla_a2_b, gla_a2_bias_b,
           gla_norm_w, na_q_norm_w, na_k_norm_w, na_rpb, w_branch_ssd, w_branch_gla, w_branch_na,
           w_out, norm_mlp_w, w_ff1, w_ff2):
    B, S, D = x.shape
    T = B * S
    depth = w_in.shape[0]
    rows = S // GRID_W
    x2 = x.reshape(T, D)
    for l in range(depth):
        w_big = _permute_weight(w_in[l])
        w_small = _small_weight(w_in[l]).astype(BF16)
        hg = (SSD_GROUPS, SSD_HG)
        zeros_r = jnp.zeros((SSD_GROUPS, LANE - 2 * SSD_HG), F32)
        dt_bias = jnp.concatenate([ssd_dt_bias_f[l].reshape(hg), ssd_dt_bias_b[l].reshape(hg), zeros_r], axis=1)
        a_neg = jnp.concatenate([-jnp.exp(ssd_a_log_f[l]).reshape(hg), -jnp.exp(ssd_a_log_b[l]).reshape(hg),
                                 zeros_r], axis=1)
        prow3 = jnp.concatenate([dt_bias[:, None], a_neg[:, None], jnp.zeros((SSD_GROUPS, 6, LANE), F32)], axis=1)
        prow = prow3.reshape(SSD_GROUPS * 8, LANE)
        pcol = jnp.transpose(prow3[:, :, :SMALL_T_ROWS], (0, 2, 1))
        w_small_t = w_small.T.reshape(SSD_GROUPS, LANE, D)[:, :SMALL_T_ROWS].reshape(-1, D)
        drow = jnp.repeat(ssd_d[l], SSD_HEAD_DIM)[None, :]
        hk = (GLA_HEADS, 1, GLA_DK)
        a2 = jnp.concatenate([_pad_rows(gla_a2_f[l], SM_GAF, LANE).reshape((LANE,) + hk),
                              _pad_rows(gla_a2_b[l], SM_GAB, LANE).reshape((LANE,) + hk)],
                             axis=2).reshape(LANE, -1).astype(BF16)
        a2_bias = jnp.concatenate([gla_a2_bias_f[l].reshape(hk), gla_a2_bias_b[l].reshape(hk)],
                                  axis=1).reshape(1, -1)
        na_bound, na_flag = _na_score_bound(na_rpb[l], na_q_norm_w[l], na_k_norm_w[l])
        table = _na_bias_rows(na_rpb[l] * LOG2E - na_bound[:, None, None])
        q_row = jnp.tile(na_q_norm_w[l] * (NA_HEAD_DIM ** -0.5 * LOG2E), NA_HEADS)
        k_row = jnp.tile(na_k_norm_w[l], NA_HEADS)
        qkw = jnp.concatenate([q_row[None], k_row[None], jnp.zeros((6, NA_W), F32)], axis=0)

        u2, us2, ust = _inproj(x2, norm_mix_w[l][None, :], w_big, w_small, w_small_t, qkw)
        u3 = u2.reshape(B, S, U_WIDTH)
        us3 = us2.reshape(B, S, SMALL_W)
        xbc = _conv(u3, ssd_conv_w[l], ssd_conv_b[l][None, :])
        y_ssd = _ssd(xbc, u3, us3, ust, prow, pcol, drow, ssd_norm_w[l][None, :])
        y_gla = _gla(u3, us3, a2, a2_bias, gla_norm_w[l][None, :])
        y_na = _na(u3, table, na_flag)
        x2 = _merge(x2, y_ssd.reshape(T, -1), y_gla.reshape(T, -1), y_na.reshape(T, -1), u2,
                    w_branch_ssd[l].astype(BF16), w_branch_gla[l].astype(BF16),
                    w_branch_na[l].astype(BF16), w_out[l].astype(BF16))
        x2 = _mlp(x2, norm_mlp_w[l][None, :], w_ff1[l].astype(BF16), w_ff2[l].astype(BF16))
    return x2.reshape(B, S, D)
```

```python
import functools
import math

import jax
import jax.numpy as jnp
import numpy as np
from jax import lax
from jax.experimental import pallas as pl
from jax.experimental.pallas import tpu as pltpu

F32 = jnp.float32
BF16 = jnp.bfloat16

EPS = 1e-6
D_MODEL = 1024
GRID_W = 64

SSD_HEADS = 16
SSD_HEAD_DIM = 64
SSD_D_INNER = 1024
SSD_GROUPS = 2
SSD_STATE = 128
SSD_CONV = 5
SSD_CONV_DIM = 1536
SSD_CHUNK = 128
SSD_CHUNKS_PER_STEP = 2
SSD_GROUP_W = SSD_D_INNER // SSD_GROUPS
SSD_HG = SSD_HEADS // SSD_GROUPS

GLA_HEADS = 4
GLA_DK = 128
GLA_DV = 256
GLA_KEY_W = 512
GLA_VAL_W = 1024
GLA_GATE_RANK = 16
GLA_GATE_NORM = 16.0
GLA_CHUNK = 64
GLA_BLOCKS_PER_STEP = 2
GLA_OUT_UNROLL = 4

NA_HEADS = 16
NA_HEAD_DIM = 64
NA_W = 1024
NA_WIN_H = 8
NA_WIN_W = 16
NA_BATCH_PER_STEP = 4
NA_ROWS_PER_STEP = 8
NA_ROWS_PER_STEP_EXACT = 2
LOG2E = 1.4426950408889634
NA_BOUND_SLACK = 0.02
NA_MAX_BOUND_GAP = 90.0

N_BRANCH = 3
D_FF = 4096

IN_SIZES = (SSD_D_INNER, SSD_CONV_DIM, SSD_HEADS, SSD_HEADS,
            GLA_KEY_W, GLA_KEY_W, GLA_VAL_W, GLA_VAL_W, GLA_GATE_RANK, GLA_GATE_RANK,
            NA_W, NA_W, NA_W, N_BRANCH * D_MODEL)
_IN_OFF = np.concatenate([[0], np.cumsum(IN_SIZES)])
(_O_Z, _O_XBC, _O_DTF, _O_DTB, _O_GQ, _O_GK, _O_GV, _O_GG, _O_GAF, _O_GAB,
 _O_NQ, _O_NK, _O_NV, _O_GATE) = [int(v) for v in _IN_OFF[:-1]]

U_NQ = 0
U_NK = 1024
U_NV = 2048
U_GATE = 3072
U_Z = 6144
U_XBC = 7168
U_GQ = 8704
U_GK = 9216
U_GV = 9728
U_GG = 10752
U_WIDTH = 11776
INPROJ_COL_CHUNK = 1024
LANE = 128
VMEM_LIMIT = 56 * 1024 * 1024

SMALL_W = SSD_GROUPS * LANE
SM_DTF, SM_DTB, SM_GAF, SM_GAB = 0, 8, 16, 32
SMALL_T_ROWS = 2 * SSD_HG

_BIG_SEGS = ((_O_NQ, 6144), (_O_Z, 2560), (_O_GQ, 3072))


def _small_weight(w):
    blocks = []
    for g in range(SSD_GROUPS):
        h0 = g * SSD_HG
        cols = [w[:, _O_DTF + h0:_O_DTF + h0 + SSD_HG], w[:, _O_DTB + h0:_O_DTB + h0 + SSD_HG]]
        used = 2 * SSD_HG
        if g == 0:
            cols += [w[:, _O_GAF:_O_GAF + GLA_GATE_RANK], w[:, _O_GAB:_O_GAB + GLA_GATE_RANK]]
            used += 2 * GLA_GATE_RANK
        cols.append(jnp.zeros((w.shape[0], LANE - used), w.dtype))
        blocks += cols
    return jnp.concatenate(blocks, axis=1)


def _cparams(sem, vmem=VMEM_LIMIT):
    return pltpu.CompilerParams(dimension_semantics=sem, vmem_limit_bytes=vmem)


def _sigmoid(x):
    return 1.0 / (1.0 + jnp.exp(-x))


def _silu(x):
    return x * _sigmoid(x)


def _softplus(x):
    return jnp.maximum(x, 0.0) + jnp.log1p(jnp.exp(-jnp.abs(x)))


def _log2_sigmoid(x):
    t = x * (-LOG2E)
    return -(jnp.maximum(t, 0.0) + jnp.log2(1.0 + jnp.exp2(-jnp.abs(t))))


def _nt(a, b):
    return lax.dot_general(a, b, (((1,), (1,)), ((), ())), preferred_element_type=F32)


def _tn(a, b):
    return lax.dot_general(a, b, (((0,), (0,)), ((), ())), preferred_element_type=F32)


def _dot(a, b):
    return jnp.dot(a, b, preferred_element_type=F32)


def _dot_exact(a, b):
    return jnp.dot(a, b, preferred_element_type=F32, precision=lax.Precision.HIGHEST)


def _iota2(shape, dim):
    return lax.broadcasted_iota(jnp.int32, shape, dim)


def _wperm_kernel(w_ref, o_ref):
    off = 0
    for a, n in _BIG_SEGS:
        o_ref[:, off:off + n] = w_ref[0, :, a:a + n].astype(BF16)
        off += n


def _permute_weight(w_all, layer, tr=64):
    _, rows, n_in = w_all.shape
    return pl.pallas_call(
        _wperm_kernel,
        grid=(rows // tr,),
        in_specs=[pl.BlockSpec((1, tr, n_in), lambda i: (layer, i, 0))],
        out_specs=pl.BlockSpec((tr, U_WIDTH), lambda i: (i, 0)),
        out_shape=jax.ShapeDtypeStruct((rows, U_WIDTH), BF16),
        compiler_params=_cparams(("parallel",)),
        name="w_permute",
    )(w_all)


def _head_rms(r, w_row):
    G = 2 * LANE
    er = _iota2((G, G), 0) // NA_HEAD_DIM
    ec = _iota2((G, G), 1) // NA_HEAD_DIM
    e_blk = (er == ec).astype(BF16)
    outs = []
    for a in range(0, r.shape[1], G):
        x = r[:, a:a + G]
        ms = _dot((x * x).astype(BF16), e_blk) * (1.0 / NA_HEAD_DIM)
        outs.append(x * lax.rsqrt(ms + EPS))
    return jnp.concatenate(outs, axis=1) * w_row


def _inproj_kernel(x_ref, nw_ref, w_ref, ws_ref, wst_ref, qkw_ref, u_ref, us_ref, ust_ref, h_ref):
    tn = u_ref.shape[1]
    chunks = [(a, min(a + INPROJ_COL_CHUNK, tn)) for a in range(0, tn, INPROJ_COL_CHUNK)]

    @pl.when(pl.program_id(1) == 0)
    def _():
        x = x_ref[...]
        ms = jnp.mean(x * x, axis=-1, keepdims=True)
        h = (x * lax.rsqrt(ms + EPS) * nw_ref[...]).astype(BF16)
        h_ref[...] = h
        us_ref[...] = _dot(h, ws_ref[...].astype(BF16))
        ust_ref[...] = _nt(wst_ref[...].astype(BF16), h)
        for c, (a, b) in enumerate(chunks):
            r = _dot(h, w_ref[:, a:b])
            if c < 2:
                r = _head_rms(r, qkw_ref[c:c + 1, :])
            u_ref[:, a:b] = r.astype(BF16)

    @pl.when(pl.program_id(1) != 0)
    def _():
        for a, b in chunks:
            u_ref[:, a:b] = _dot(h_ref[...], w_ref[:, a:b]).astype(BF16)


def _inproj(x2, nw, w_big, w_small, w_small_t, qkw, tm=1024, tn=U_WIDTH // 4):
    assert (U_NQ, U_NK) == (0, INPROJ_COL_CHUNK) and NA_W == INPROJ_COL_CHUNK
    T = x2.shape[0]
    return pl.pallas_call(
        _inproj_kernel,
        grid=(T // tm, U_WIDTH // tn),
        in_specs=[
            pl.BlockSpec((tm, D_MODEL), lambda i, j: (i, 0)),
            pl.BlockSpec((1, D_MODEL), lambda i, j: (0, 0)),
            pl.BlockSpec((D_MODEL, tn), lambda i, j: (0, j)),
            pl.BlockSpec((D_MODEL, SMALL_W), lambda i, j: (0, 0)),
            pl.BlockSpec((SSD_GROUPS * SMALL_T_ROWS, D_MODEL), lambda i, j: (0, 0)),
            pl.BlockSpec((8, NA_W), lambda i, j: (0, 0)),
        ],
        out_specs=[
            pl.BlockSpec((tm, tn), lambda i, j: (i, j)),
            pl.BlockSpec((tm, SMALL_W), lambda i, j: (i, 0)),
            pl.BlockSpec((SSD_GROUPS * SMALL_T_ROWS, tm), lambda i, j: (0, i)),
        ],
        out_shape=[
            jax.ShapeDtypeStruct((T, U_WIDTH), BF16),
            jax.ShapeDtypeStruct((T, SMALL_W), F32),
            jax.ShapeDtypeStruct((SSD_GROUPS * SMALL_T_ROWS, T), F32),
        ],
        scratch_shapes=[pltpu.VMEM((tm, D_MODEL), BF16)],
        compiler_params=_cparams(("parallel", "arbitrary")),
        name="inproj",
    )(x2, nw, w_big, w_small, w_small_t, qkw)


def _conv_kernel(u_ref, w_ref, b_ref, o_ref):
    x = u_ref[0].astype(F32)
    S = x.shape[0]
    row = _iota2(x.shape, 0)
    acc = jnp.zeros_like(x) + b_ref[...]
    pad = SSD_CONV // 2
    for k in range(SSD_CONV):
        d = k - pad
        if d == 0:
            xs = x
        else:
            xs = pltpu.roll(x, (-d) % S, axis=0)
            valid = (row + d >= 0) & (row + d < S)
            xs = jnp.where(valid, xs, 0.0)
        acc = acc + w_ref[k:k + 1, :] * xs
    o_ref[0] = _silu(acc).astype(BF16)


def _conv(u3, conv_w, conv_b, tc=256):
    B, S, _ = u3.shape
    nblk = SSD_CONV_DIM // tc
    off = U_XBC // tc
    return pl.pallas_call(
        _conv_kernel,
        grid=(B, nblk),
        in_specs=[
            pl.BlockSpec((1, S, tc), lambda b, c: (b, 0, off + c)),
            pl.BlockSpec((SSD_CONV, tc), lambda b, c: (0, c)),
            pl.BlockSpec((1, tc), lambda b, c: (0, c)),
        ],
        out_specs=pl.BlockSpec((1, S, tc), lambda b, c: (b, 0, c)),
        out_shape=jax.ShapeDtypeStruct((B, S, SSD_CONV_DIM), BF16),
        compiler_params=_cparams(("parallel", "parallel")),
        name="ssd_conv",
    )(u3, conv_w, conv_b)


def _split_hi_lo(v):
    hi = v.astype(BF16)
    lo = (v - hi.astype(F32)).astype(BF16)
    return jnp.concatenate([hi, lo], axis=1)


def _split3(v, axis):
    hi = v.astype(BF16)
    r1 = v - hi.astype(F32)
    mid = r1.astype(BF16)
    lo = (r1 - mid.astype(F32)).astype(BF16)
    return jnp.concatenate([hi, mid, lo], axis=axis)


def _ssd_kernel(x_ref, b_ref, c_ref, z_ref, us_ref, ust_ref, prow_ref, pcol_ref, drow_ref, nw_ref,
                o_ref, acc_ref, cumc_ref, cumr_ref, wdt_ref, dec_ref, xs_ref, decx_ref, sst_ref, st_ref):
    L = SSD_CHUNK
    S = x_ref.shape[1]
    nc = S // L
    W = SSD_GROUP_W
    HG = SSD_HG
    R = 2 * HG

    ii = _iota2((L, L), 0)
    jj = _iota2((L, L), 1)
    tril = ii >= jj
    triu = jj >= ii
    tril_b = tril.astype(BF16)
    triu_b = triu.astype(BF16)

    er = _iota2((LANE, 2 * W), 0)
    ec = _iota2((LANE, 2 * W), 1)
    e = (er == jnp.where(ec < W, ec // SSD_HEAD_DIM, (ec - W) // SSD_HEAD_DIM + HG)).astype(BF16)
    e2 = jnp.concatenate([e, e], axis=0)

    bias_row = prow_ref[0:1, :]
    a_row = prow_ref[1:2, :]
    bias_col = pcol_ref[0, :, 0:1]
    a_col = pcol_ref[0, :, 1:2]

    lane_fwd = _iota2((L, LANE), 1) < HG
    row_fwd = _iota2((R, L), 0) < HG
    lane_half = _iota2((L, LANE), 1) < SSD_HEAD_DIM

    def decay_sums(c):
        rs = pl.ds(pl.multiple_of(c * L, L), L)
        dt_c = _softplus(us_ref[0, rs, :] + bias_row)
        a_c = dt_c * a_row
        dt_r = _softplus(ust_ref[:, rs] + bias_col)
        a_r = dt_r * a_col
        return dt_c, a_c, _dot(tril_b, _split3(a_c, 1)), dt_r, a_r, _dot(_split3(a_r, 0), triu_b)

    def decay_store(c, sums):
        rs = pl.ds(pl.multiple_of(c * L, L), L)
        dt_c, a_c, pp, dt_r, a_r, pr = sums
        p_c = pp[:, 0:LANE] + pp[:, LANE:2 * LANE] + pp[:, 2 * LANE:3 * LANE]
        tot_c = p_c[L - 1:L, :]
        cum_c = jnp.where(lane_fwd, p_c, tot_c - p_c + a_c)
        cumc_ref[rs, :] = cum_c * LOG2E
        wdt_ref[rs, :] = jnp.exp(tot_c - cum_c) * dt_c
        dec_ref[c] = jnp.broadcast_to(jnp.exp(tot_c), (16, LANE))
        p_r = pr[0:R] + pr[R:2 * R] + pr[2 * R:3 * R]
        tot_r = p_r[:, L - 1:L]
        cum_r = jnp.where(row_fwd, p_r, tot_r - p_r + a_r)
        cumr_ref[:, rs] = (cum_r - jnp.log(dt_r)) * LOG2E

    def intra_chunk(c):
        rs = pl.ds(pl.multiple_of(c * L, L), L)
        x_c = x_ref[0, rs, :]
        b_c = b_ref[0, rs, :]
        cb = _nt(c_ref[0, rs, :], b_c)
        ex = _dot(_split_hi_lo(jnp.concatenate([wdt_ref[rs, :], dec_ref[c]], axis=0)), e2)
        decx_ref[c] = ex[L:L + 8]
        xf = x_c.astype(F32)
        xw = jnp.concatenate([xf * ex[0:L, 0:W], xf * ex[0:L, W:2 * W]], axis=1).astype(BF16)
        xs_ref[c] = _tn(b_c, xw)
        cum_c = cumc_ref[rs, :]
        cum_r = cumr_ref[:, rs]
        for hp in range(HG // 2):
            ms = []
            for hh in range(2):
                hf = 2 * hp + hh
                hb = HG + 2 * hp + hh
                decf = jnp.exp2(jnp.where(tril, cum_c[:, hf:hf + 1] - cum_r[hf:hf + 1, :], -jnp.inf))
                decb = jnp.exp2(jnp.where(triu, cum_c[:, hb:hb + 1] - cum_r[hb:hb + 1, :], -jnp.inf))
                ms.append((cb * (decf + decb)).astype(BF16))
            m2 = jnp.concatenate(ms, axis=1)
            xp = x_c[:, hp * LANE:(hp + 1) * LANE]
            xz = jnp.zeros_like(xp)
            x2 = jnp.concatenate([jnp.where(lane_half, xp, xz), jnp.where(lane_half, xz, xp)], axis=0)
            acc_ref[rs, hp * LANE:(hp + 1) * LANE] = _dot(m2, x2)

    G = SSD_CHUNKS_PER_STEP
    n_steps = nc // G

    def fused_body(i, carry):
        nxt = jnp.minimum(i + 1, n_steps - 1)
        sums = [decay_sums(G * nxt + j) for j in range(G)]
        for j in range(G):
            intra_chunk(G * i + j)
        for j in range(G):
            decay_store(G * nxt + j, sums[j])
        return carry

    for j in range(G):
        decay_store(j, decay_sums(j))
    lax.fori_loop(0, n_steps, fused_body, 0)

    st_ref[...] = jnp.zeros_like(st_ref)

    def state_body(i, carry):
        for d, c in enumerate((i, nc - 1 - i)):
            ls = slice(d * W, (d + 1) * W)
            s_prev = st_ref[:, ls]
            sst_ref[c, :, ls] = s_prev.astype(BF16)
            st_ref[:, ls] = s_prev * decx_ref[c, 0:1, ls] + xs_ref[c, :, ls]
        return carry

    lax.fori_loop(0, nc, state_body, 0)

    def out_body(c, carry):
        rs = pl.ds(pl.multiple_of(c * L, L), L)
        yo = _dot(c_ref[0, rs, :], sst_ref[c])
        sc = _dot(_split_hi_lo(jnp.exp2(cumc_ref[rs, :])), e2)
        xf = x_ref[0, rs, :].astype(F32)
        y = acc_ref[rs, :] + yo[:, 0:W] * sc[:, 0:W] + yo[:, W:2 * W] * sc[:, W:2 * W] + xf * drow_ref[...]
        y = y * _silu(z_ref[0, rs, :].astype(F32))
        ms = jnp.mean(y * y, axis=-1, keepdims=True)
        o_ref[0, rs, :] = (y * lax.rsqrt(ms + EPS) * nw_ref[...]).astype(BF16)
        return carry

    lax.fori_loop(0, nc, out_body, 0, unroll=2)


def _ssd(xbc, u3, us3, ust, prow, pcol, drow, nw):
    B, S, _ = xbc.shape
    W = SSD_GROUP_W
    N = SSD_STATE
    nc = S // SSD_CHUNK
    return pl.pallas_call(
        _ssd_kernel,
        grid=(B, SSD_GROUPS),
        in_specs=[
            pl.BlockSpec((1, S, W), lambda b, g: (b, 0, g)),
            pl.BlockSpec((1, S, N), lambda b, g: (b, 0, SSD_D_INNER // N + g)),
            pl.BlockSpec((1, S, N), lambda b, g: (b, 0, SSD_D_INNER // N + SSD_GROUPS + g)),
            pl.BlockSpec((1, S, W), lambda b, g: (b, 0, U_Z // W + g)),
            pl.BlockSpec((1, S, LANE), lambda b, g: (b, 0, g)),
            pl.BlockSpec((SMALL_T_ROWS, S), lambda b, g: (g, b)),
            pl.BlockSpec((8, LANE), lambda b, g: (g, 0)),
            pl.BlockSpec((1, SMALL_T_ROWS, 8), lambda b, g: (g, 0, 0)),
            pl.BlockSpec((1, W), lambda b, g: (0, g)),
            pl.BlockSpec((1, W), lambda b, g: (0, g)),
        ],
        out_specs=pl.BlockSpec((1, S, W), lambda b, g: (b, 0, g)),
        out_shape=jax.ShapeDtypeStruct((B, S, SSD_D_INNER), BF16),
        scratch_shapes=[
            pltpu.VMEM((S, W), F32),
            pltpu.VMEM((S, LANE), F32),
            pltpu.VMEM((SMALL_T_ROWS, S), F32),
            pltpu.VMEM((S, LANE), F32),
            pltpu.VMEM((nc, 16, LANE), F32),
            pltpu.VMEM((nc, N, 2 * W), F32),
            pltpu.VMEM((nc, 8, 2 * W), F32),
            pltpu.VMEM((nc, N, 2 * W), BF16),
            pltpu.VMEM((N, 2 * W), F32),
        ],
        compiler_params=_cparams(("parallel", "parallel")),
        name="ssd_scan",
    )(xbc, xbc, xbc, u3, us3, ust, prow, pcol, drow, nw)


def _gla_kernel(q_ref, k_ref, v_ref, gg_ref, us_ref, a2_ref, bias_ref, nw_ref,
                o_ref, acc_ref, g_ref, qd_ref, kd_ref, kdp_ref, qcat_ref, x_ref, dec_ref, sst_ref):
    L = GLA_CHUNK
    BL = 2 * L
    DK = GLA_DK
    S = q_ref.shape[1]
    nb = S // BL
    scale = DK ** -0.5

    ii = _iota2((BL, BL), 0)
    jj = _iota2((BL, BL), 1)
    same = (ii // L) == (jj // L)
    masks = (same & (ii >= jj), same & (jj >= ii))
    tri2 = masks[0].astype(BF16)
    par_row = _iota2((BL, DK), 0) // L

    ga = us_ref[0].astype(BF16)
    g_ref[...] = _log2_sigmoid(_dot(ga, a2_ref[...]) + bias_ref[...]) * (1.0 / GLA_GATE_NORM)

    def decay_sums(i):
        g = g_ref[pl.ds(pl.multiple_of(i * BL, BL), BL), :]
        hi = g.astype(BF16)
        r1 = g - hi.astype(F32)
        mid = r1.astype(BF16)
        lo = (r1 - mid.astype(F32)).astype(BF16)
        return g, _dot(tri2, jnp.concatenate([hi, mid, lo], axis=1))

    def decay_block(i, sums):
        rs = pl.ds(pl.multiple_of(i * BL, BL), BL)
        g, pp = sums
        p = pp[:, 0:2 * DK] + pp[:, 2 * DK:4 * DK] + pp[:, 4 * DK:6 * DK]
        q_c = q_ref[0, rs, :].astype(F32) * scale
        k_c = k_ref[0, rs, :].astype(F32)
        zero = jnp.zeros((BL, DK), BF16)
        for d in range(2):
            p_d = p[:, d * DK:(d + 1) * DK]
            tot = jnp.where(par_row == 0, p_d[L - 1:L, :], p_d[BL - 1:BL, :])
            b = p_d if d == 0 else tot - p_d + g[:, DK:]
            qd = (q_c * jnp.exp2(b)).astype(BF16)
            kdec = (k_c * jnp.exp2(tot - b)).astype(BF16)
            qd_ref[d, rs, :] = qd
            kd_ref[d, rs, :] = (k_c * jnp.exp2(-b)).astype(BF16)
            for par in range(2):
                sel = par_row == par
                kdp_ref[d, rs, par * DK:(par + 1) * DK] = jnp.where(sel, kdec, zero)
                qcat_ref[rs, (2 * d + par) * DK:(2 * d + par + 1) * DK] = jnp.where(sel, qd, zero)
                last = (par + 1) * L - 1
                dec_ref[d, 2 * i + par] = jnp.broadcast_to(jnp.exp2(p_d[last:last + 1, :]), (8, DK))

    G = GLA_BLOCKS_PER_STEP
    n_groups = nb // G

    def intra_group(i):
        blks = [G * i + j for j in range(G)]
        rss = [pl.ds(pl.multiple_of(b * BL, BL), BL) for b in blks]
        vs = [v_ref[0, rs, :] for rs in rss]
        atts = [[_nt(qd_ref[d, rs, :], kd_ref[d, rs, :]) for d in range(2)] for rs in rss]
        for j, b in enumerate(blks):
            for d in range(2):
                x_ref[d, b] = _tn(vs[j], kdp_ref[d, rss[j], :])
        for j in range(G):
            att = jnp.where(masks[0], atts[j][0], 0.0) + jnp.where(masks[1], atts[j][1], 0.0)
            acc_ref[rss[j], :] = _dot(att.astype(BF16), vs[j])

    def fused_body(i, carry):
        nxt = jnp.minimum(i + 1, n_groups - 1)
        sums = [decay_sums(G * nxt + j) for j in range(G)]
        intra_group(i)
        for j in range(G):
            decay_block(G * nxt + j, sums[j])
        return carry

    for j in range(G):
        decay_block(j, decay_sums(j))
    lax.fori_loop(0, n_groups, fused_body, 0)

    def state_body(d):
        def body(i, s):
            b = i if d == 0 else nb - 1 - i
            for par in ((0, 1) if d == 0 else (1, 0)):
                lane0 = (2 * d + par) * DK
                sst_ref[b, :, lane0:lane0 + DK] = s.astype(BF16)
                s = s * dec_ref[d, 2 * b + par, 0:1, :] + x_ref[d, b, :, par * DK:(par + 1) * DK]
            return s
        return body

    for d in range(2):
        lax.fori_loop(0, nb, state_body(d), jnp.zeros((GLA_DV, DK), F32))

    def out_body(b, carry):
        rs = pl.ds(pl.multiple_of(b * BL, BL), BL)
        o = acc_ref[rs, :] + _nt(qcat_ref[rs, :], sst_ref[b])
        ms = jnp.mean(o * o, axis=-1, keepdims=True)
        o = o * lax.rsqrt(ms + EPS) * nw_ref[...]
        o_ref[0, rs, :] = (o * _silu(gg_ref[0, rs, :].astype(F32))).astype(BF16)
        return carry

    lax.fori_loop(0, nb, out_body, 0, unroll=GLA_OUT_UNROLL)


def _gla(u3, us3, a2, bias, nw):
    B, S, _ = u3.shape
    DK, DV = GLA_DK, GLA_DV
    nc = S // GLA_CHUNK
    nb = nc // 2
    return pl.pallas_call(
        _gla_kernel,
        grid=(B, GLA_HEADS),
        in_specs=[
            pl.BlockSpec((1, S, DK), lambda b, h: (b, 0, U_GQ // DK + h)),
            pl.BlockSpec((1, S, DK), lambda b, h: (b, 0, U_GK // DK + h)),
            pl.BlockSpec((1, S, DV), lambda b, h: (b, 0, U_GV // DV + h)),
            pl.BlockSpec((1, S, DV), lambda b, h: (b, 0, U_GG // DV + h)),
            pl.BlockSpec((1, S, LANE), lambda b, h: (b, 0, 0)),
            pl.BlockSpec((LANE, 2 * DK), lambda b, h: (0, h)),
            pl.BlockSpec((1, 2 * DK), lambda b, h: (0, h)),
            pl.BlockSpec((1, DV), lambda b, h: (0, 0)),
        ],
        out_specs=pl.BlockSpec((1, S, DV), lambda b, h: (b, 0, h)),
        out_shape=jax.ShapeDtypeStruct((B, S, GLA_VAL_W), BF16),
        scratch_shapes=[
            pltpu.VMEM((S, DV), F32),
            pltpu.VMEM((S, 2 * DK), F32),
            pltpu.VMEM((2, S, DK), BF16),
            pltpu.VMEM((2, S, DK), BF16),
            pltpu.VMEM((2, S, 2 * DK), BF16),
            pltpu.VMEM((S, 4 * DK), BF16),
            pltpu.VMEM((2, nb, DV, 2 * DK), F32),
            pltpu.VMEM((2, nc, 8, DK), F32),
            pltpu.VMEM((nb, DV, 4 * DK), BF16),
        ],
        compiler_params=_cparams(("parallel", "parallel")),
        name="gla_scan",
    )(u3, u3, u3, u3, us3, a2, bias, nw)


def _na_bias_rows(rpb):
    H, R, C = rpb.shape
    n_pos = GRID_W - NA_WIN_W
    n_neg = GRID_W - NA_WIN_W + 2
    ext = jnp.concatenate([rpb[:, :, NA_WIN_W - 1:], jnp.repeat(rpb[:, :, C - 1:], n_pos, axis=2),
                           jnp.repeat(rpb[:, :, 0:1], n_neg, axis=2), rpb[:, :, 1:NA_WIN_W - 1]], axis=2)
    return jnp.pad(ext, ((0, 0), (0, 1), (0, 0))).reshape(H // 2, 2, R + 1, 2 * GRID_W)


def _na_kernel(flag_ref, q_ref, k_ref, v_ref, ext_ref, o_ref, vx_ref, tab_ref):
    NB, S, _ = q_ref.shape
    rows = S // GRID_W
    total = NB * rows
    win_h = NA_WIN_H
    nk = win_h * GRID_W

    vx_ref[:, :, 0:LANE] = v_ref[...]
    vx_ref[:, :, LANE:2 * LANE] = jnp.ones((NB, S, LANE), BF16)
    bound_ok = flag_ref[0] != 0

    first_q = _iota2((GRID_W, LANE), 1) < NA_HEAD_DIM

    @pl.when(pl.program_id(1) == 0)
    def _():
        q_col = _iota2((GRID_W, 2 * GRID_W), 0)
        k_col = _iota2((GRID_W, 2 * GRID_W), 1) % GRID_W
        w_start = jnp.clip(q_col - NA_WIN_W // 2, 0, GRID_W - NA_WIN_W)
        in_window = (k_col >= w_start) & (k_col < w_start + NA_WIN_W)
        low = _iota2((GRID_W, 2 * GRID_W), 1) < GRID_W
        for hd in range(2):
            def skewed(rr, shift):
                row = jnp.broadcast_to(ext_ref[0, hd, rr:rr + 1, :], (GRID_W, 2 * GRID_W))
                return pltpu.roll(row, shift, axis=1, stride=1, stride_axis=0)

            for e in range(2 * NA_WIN_H - 2):
                t = jnp.where(low, skewed(e, 0), skewed(e + 1, GRID_W))
                tab_ref[e, hd * GRID_W:(hd + 1) * GRID_W, :] = jnp.where(in_window, t, -jnp.inf)

    def locate(r):
        bi = r // rows
        rl = r - bi * rows
        r0 = jnp.clip(rl - win_h // 2, 0, rows - win_h)
        return bi, pl.multiple_of(rl * GRID_W, GRID_W), rl - r0, pl.multiple_of(r0 * GRID_W, GRID_W)

    def scores(r):
        bi, q0, delta, k0 = locate(r)
        q = q_ref[bi, pl.ds(q0, GRID_W), :]
        zero = jnp.zeros_like(q)
        qs = jnp.concatenate([jnp.where(first_q, q, zero), jnp.where(first_q, zero, q)], axis=0)
        rr0 = (win_h - 1) - delta
        bias = jnp.concatenate([tab_ref[rr0 + w] for w in range(0, win_h, 2)], axis=1)
        return _nt(qs, k_ref[bi, pl.ds(k0, nk), :]) + bias

    def probs_exact(s):
        return jnp.exp2(s - jnp.max(s, axis=-1, keepdims=True)).astype(BF16)

    def probs_bounded(r):
        return jnp.exp2(scores(r)).astype(BF16)

    def attend(r, p):
        bi, q0, _, k0 = locate(r)
        ox = _dot(p, vx_ref[bi, pl.ds(k0, nk), :])
        o = ox[:, 0:LANE] / ox[:, LANE:2 * LANE]
        o_ref[bi, pl.ds(q0, GRID_W), :] = jnp.where(first_q, o[:GRID_W], o[GRID_W:]).astype(BF16)

    @pl.when(bound_ok)
    def _():
        U = NA_ROWS_PER_STEP

        def row_body(i, p_prev):
            r = i * U
            for j in range(U):
                attend(r - U + j, p_prev[j])
            return tuple(probs_bounded(r + j) for j in range(U))

        p_last = lax.fori_loop(1, total // U, row_body, tuple(probs_bounded(j) for j in range(U)))
        for j in range(U):
            attend(total - U + j, p_last[j])

    @pl.when(jnp.logical_not(bound_ok))
    def _():
        U = NA_ROWS_PER_STEP_EXACT

        def row_body(i, carry):
            s_cur, p_prev = carry
            r = i * U
            for j in range(U):
                attend(r - U + j, p_prev[j])
            p = tuple(probs_exact(s) for s in s_cur)
            s_next = tuple(scores(jnp.minimum(r + U + j, total - 1)) for j in range(U))
            return s_next, p

        p0 = tuple(probs_exact(scores(j)) for j in range(U))
        s1 = tuple(scores(U + j) for j in range(U))
        _, p_last = lax.fori_loop(1, total // U, row_body, (s1, p0))
        for j in range(U):
            attend(total - U + j, p_last[j])


def _na_score_bound(rpb, q_norm_w, k_norm_w):
    H = rpb.shape[0]
    qk = (NA_HEAD_DIM ** 0.5 * LOG2E * (1.0 + NA_BOUND_SLACK)) * jnp.max(jnp.abs(q_norm_w)) * jnp.max(jnp.abs(k_norm_w))
    b_max = jnp.max(rpb.reshape(H, -1), axis=1) * LOG2E
    b_self = rpb[:, NA_WIN_H - 1, NA_WIN_W - 1] * LOG2E
    bound = qk + b_max
    flag = jnp.all(bound - (b_self - qk) <= NA_MAX_BOUND_GAP)
    return bound, flag.astype(jnp.int32).reshape(1)


def _na(u3, bias_rows, flag):
    B, S, _ = u3.shape
    assert S // GRID_W >= NA_WIN_H and NA_WIN_H % 2 == 0 and 2 * GRID_W == LANE
    nb = math.gcd(B, NA_BATCH_PER_STEP)
    assert (nb * (S // GRID_W)) % NA_ROWS_PER_STEP == 0
    return pl.pallas_call(
        _na_kernel,
        grid=(NA_HEADS // 2, B // nb),
        in_specs=[
            pl.BlockSpec(memory_space=pltpu.SMEM),
            pl.BlockSpec((nb, S, LANE), lambda h, b: (b, 0, U_NQ // LANE + h)),
            pl.BlockSpec((nb, S, LANE), lambda h, b: (b, 0, U_NK // LANE + h)),
            pl.BlockSpec((nb, S, LANE), lambda h, b: (b, 0, U_NV // LANE + h)),
            pl.BlockSpec((1, 2, 2 * NA_WIN_H, 2 * GRID_W), lambda h, b: (h, 0, 0, 0)),
        ],
        out_specs=pl.BlockSpec((nb, S, LANE), lambda h, b: (b, 0, h)),
        out_shape=jax.ShapeDtypeStruct((B, S, NA_W), BF16),
        scratch_shapes=[
            pltpu.VMEM((nb, S, 2 * LANE), BF16),
            pltpu.VMEM((2 * NA_WIN_H - 2, 2 * GRID_W, 2 * GRID_W), F32),
        ],
        compiler_params=_cparams(("parallel", "arbitrary")),
        name="na_attn",
    )(flag, u3, u3, u3, bias_rows)


def _merge_kernel(x_ref, ys_ref, yg_ref, yn_ref, gate_ref, ws_ref, wg_ref, wn_ref, wo_ref, o_ref):
    D = D_MODEL
    mixed = _sigmoid(gate_ref[:, 0:D].astype(F32)) * _dot(ys_ref[...], ws_ref[...])
    mixed += _sigmoid(gate_ref[:, D:2 * D].astype(F32)) * _dot(yg_ref[...], wg_ref[...])
    mixed += _sigmoid(gate_ref[:, 2 * D:3 * D].astype(F32)) * _dot(yn_ref[...], wn_ref[...])
    o_ref[...] = x_ref[...] + _dot(mixed.astype(BF16), wo_ref[...])


def _merge(x2, ys, yg, yn, u2, ws, wg, wn, wo, tm=512):
    T = x2.shape[0]
    D = D_MODEL
    row = lambda i: (i, 0)
    fixed = lambda i: (0, 0)
    return pl.pallas_call(
        _merge_kernel,
        grid=(T // tm,),
        in_specs=[
            pl.BlockSpec((tm, D), row),
            pl.BlockSpec((tm, D), row),
            pl.BlockSpec((tm, D), row),
            pl.BlockSpec((tm, D), row),
            pl.BlockSpec((tm, N_BRANCH * D), lambda i: (i, U_GATE // (N_BRANCH * D))),
            pl.BlockSpec((D, D), fixed),
            pl.BlockSpec((D, D), fixed),
            pl.BlockSpec((D, D), fixed),
            pl.BlockSpec((D, D), fixed),
        ],
        out_specs=pl.BlockSpec((tm, D), row),
        out_shape=jax.ShapeDtypeStruct((T, D), F32),
        compiler_params=_cparams(("parallel",)),
        name="merge",
    )(x2, ys, yg, yn, u2, ws, wg, wn, wo)


def _mlp_kernel(x_ref, nw_ref, w1_ref, w2_ref, o_ref, *, tf):
    x = x_ref[...]
    ms = jnp.mean(x * x, axis=-1, keepdims=True)
    h = (x * lax.rsqrt(ms + EPS) * nw_ref[...]).astype(BF16)
    acc = x
    for f in range(D_FF // tf):
        a = jnp.maximum(_dot(h, w1_ref[:, f * tf:(f + 1) * tf]), 0.0)
        acc = acc + _dot((a * a).astype(BF16), w2_ref[f * tf:(f + 1) * tf, :])
    o_ref[...] = acc


def _mlp(x2, nw, w1, w2, tm=512, tf=1024):
    T = x2.shape[0]
    D = D_MODEL
    return pl.pallas_call(
        functools.partial(_mlp_kernel, tf=tf),
        grid=(T // tm,),
        in_specs=[
            pl.BlockSpec((tm, D), lambda i: (i, 0)),
            pl.BlockSpec((1, D), lambda i: (0, 0)),
            pl.BlockSpec((D, D_FF), lambda i: (0, 0)),
            pl.BlockSpec((D_FF, D), lambda i: (0, 0)),
        ],
        out_specs=pl.BlockSpec((tm, D), lambda i: (i, 0)),
        out_shape=jax.ShapeDtypeStruct((T, D), F32),
        compiler_params=_cparams(("parallel",)),
        name="mlp",
    )(x2, nw, w1, w2)


def _pad_rows(w, start, total):
    return jnp.zeros((total, w.shape[1]), w.dtype).at[start:start + w.shape[0]].set(w)


def kernel(x, norm_mix_w, w_in, ssd_conv_w, ssd_conv_b, ssd_dt_bias_f, ssd_dt_bias_b, ssd_a_log_f,
           ssd_a_log_b, ssd_d, ssd_norm_w, gla_a2_f, gla_a2_bias_f, gla_a2_b, gla_a2_bias_b,
           gla_norm_w, na_q_norm_w, na_k_norm_w, na_rpb, w_branch_ssd, w_branch_gla, w_branch_na,
           w_out, norm_mlp_w, w_ff1, w_ff2):
    B, S, D = x.shape
    T = B * S
    depth = w_in.shape[0]
    rows = S // GRID_W
    x2 = x.reshape(T, D)
    for l in range(depth):
        w_big = _permute_weight(w_in, l)
        w_small = _small_weight(w_in[l])
        hg = (SSD_GROUPS, SSD_HG)
        zeros_r = jnp.zeros((SSD_GROUPS, LANE - 2 * SSD_HG), F32)
        dt_bias = jnp.concatenate([ssd_dt_bias_f[l].reshape(hg), ssd_dt_bias_b[l].reshape(hg), zeros_r], axis=1)
        a_neg = jnp.concatenate([-jnp.exp(ssd_a_log_f[l]).reshape(hg), -jnp.exp(ssd_a_log_b[l]).reshape(hg),
                                 zeros_r], axis=1)
        prow3 = jnp.concatenate([dt_bias[:, None], a_neg[:, None], jnp.zeros((SSD_GROUPS, 6, LANE), F32)], axis=1)
        prow = prow3.reshape(SSD_GROUPS * 8, LANE)
        pcol = jnp.transpose(prow3[:, :, :SMALL_T_ROWS], (0, 2, 1))
        w_small_t = w_small.T.reshape(SSD_GROUPS, LANE, D)[:, :SMALL_T_ROWS].reshape(-1, D)
        drow = jnp.repeat(ssd_d[l], SSD_HEAD_DIM)[None, :]
        hk = (GLA_HEADS, 1, GLA_DK)
        a2 = jnp.concatenate([_pad_rows(gla_a2_f[l], SM_GAF, LANE).reshape((LANE,) + hk),
                              _pad_rows(gla_a2_b[l], SM_GAB, LANE).reshape((LANE,) + hk)],
                             axis=2).reshape(LANE, -1).astype(BF16)
        a2_bias = jnp.concatenate([gla_a2_bias_f[l].reshape(hk), gla_a2_bias_b[l].reshape(hk)],
                                  axis=1).reshape(1, -1)
        na_bound, na_flag = _na_score_bound(na_rpb[l], na_q_norm_w[l], na_k_norm_w[l])
        table = _na_bias_rows(na_rpb[l] * LOG2E - na_bound[:, None, None])
        q_row = jnp.tile(na_q_norm_w[l] * (NA_HEAD_DIM ** -0.5 * LOG2E), NA_HEADS)
        k_row = jnp.tile(na_k_norm_w[l], NA_HEADS)
        qkw = jnp.concatenate([q_row[None], k_row[None], jnp.zeros((6, NA_W), F32)], axis=0)

        u2, us2, ust = _inproj(x2, norm_mix_w[l][None, :], w_big, w_small, w_small_t, qkw)
        u3 = u2.reshape(B, S, U_WIDTH)
        us3 = us2.reshape(B, S, SMALL_W)
        xbc = _conv(u3, ssd_conv_w[l], ssd_conv_b[l][None, :])
        y_ssd = _ssd(xbc, u3, us3, ust, prow, pcol, drow, ssd_norm_w[l][None, :])
        y_gla = _gla(u3, us3, a2, a2_bias, gla_norm_w[l][None, :])
        y_na = _na(u3, table, na_flag)
        x2 = _merge(x2, y_ssd.reshape(T, -1), y_gla.reshape(T, -1), y_na.reshape(T, -1), u2,
                    w_branch_ssd[l].astype(BF16), w_branch_gla[l].astype(BF16),
                    w_branch_na[l].astype(BF16), w_out[l].astype(BF16))
        x2 = _mlp(x2, norm_mlp_w[l][None, :], w_ff1[l].astype(BF16), w_ff2[l].astype(BF16))
    return x2.reshape(B, S, D)
```

```python
import functools
import math

import jax
import jax.numpy as jnp
import numpy as np
from jax import lax
from jax.experimental import pallas as pl
from jax.experimental.pallas import tpu as pltpu

F32 = jnp.float32
BF16 = jnp.bfloat16

EPS = 1e-6
D_MODEL = 1024
GRID_W = 64

SSD_HEADS = 16
SSD_HEAD_DIM = 64
SSD_D_INNER = 1024
SSD_GROUPS = 2
SSD_STATE = 128
SSD_CONV = 5
SSD_CONV_DIM = 1536
SSD_CHUNK = 128
SSD_CHUNKS_PER_STEP = 2
SSD_GROUP_W = SSD_D_INNER // SSD_GROUPS
SSD_HG = SSD_HEADS // SSD_GROUPS

GLA_HEADS = 4
GLA_DK = 128
GLA_DV = 256
GLA_KEY_W = 512
GLA_VAL_W = 1024
GLA_GATE_RANK = 16
GLA_GATE_NORM = 16.0
GLA_CHUNK = 64
GLA_BLOCKS_PER_STEP = 2
GLA_OUT_UNROLL = 4

NA_HEADS = 16
NA_HEAD_DIM = 64
NA_W = 1024
NA_WIN_H = 8
NA_WIN_W = 16
NA_BATCH_PER_STEP = 4
NA_ROWS_PER_STEP = 8
NA_ROWS_PER_STEP_EXACT = 2
LOG2E = 1.4426950408889634
NA_BOUND_SLACK = 0.02
NA_MAX_BOUND_GAP = 90.0

N_BRANCH = 3
D_FF = 4096

IN_SIZES = (SSD_D_INNER, SSD_CONV_DIM, SSD_HEADS, SSD_HEADS,
            GLA_KEY_W, GLA_KEY_W, GLA_VAL_W, GLA_VAL_W, GLA_GATE_RANK, GLA_GATE_RANK,
            NA_W, NA_W, NA_W, N_BRANCH * D_MODEL)
_IN_OFF = np.concatenate([[0], np.cumsum(IN_SIZES)])
(_O_Z, _O_XBC, _O_DTF, _O_DTB, _O_GQ, _O_GK, _O_GV, _O_GG, _O_GAF, _O_GAB,
 _O_NQ, _O_NK, _O_NV, _O_GATE) = [int(v) for v in _IN_OFF[:-1]]

U_NQ = 0
U_NK = 1024
U_NV = 2048
U_GATE = 3072
U_Z = 6144
U_XBC = 7168
U_GQ = 8704
U_GK = 9216
U_GV = 9728
U_GG = 10752
U_WIDTH = 11776
INPROJ_COL_CHUNK = 1024
LANE = 128
VMEM_LIMIT = 56 * 1024 * 1024

SMALL_W = SSD_GROUPS * LANE
SM_DTF, SM_DTB, SM_GAF, SM_GAB = 0, 8, 16, 32
SMALL_T_ROWS = 2 * SSD_HG

_BIG_SEGS = ((_O_NQ, 6144), (_O_Z, 2560), (_O_GQ, 3072))


def _cparams(sem, vmem=VMEM_LIMIT):
    return pltpu.CompilerParams(dimension_semantics=sem, vmem_limit_bytes=vmem)


def _sigmoid(x):
    return 1.0 / (1.0 + jnp.exp(-x))


def _silu(x):
    return x * _sigmoid(x)


def _softplus(x):
    return jnp.maximum(x, 0.0) + jnp.log1p(jnp.exp(-jnp.abs(x)))


def _log2_sigmoid(x):
    t = x * (-LOG2E)
    return -(jnp.maximum(t, 0.0) + jnp.log2(1.0 + jnp.exp2(-jnp.abs(t))))


def _nt(a, b):
    return lax.dot_general(a, b, (((1,), (1,)), ((), ())), preferred_element_type=F32)


def _tn(a, b):
    return lax.dot_general(a, b, (((0,), (0,)), ((), ())), preferred_element_type=F32)


def _dot(a, b):
    return jnp.dot(a, b, preferred_element_type=F32)


def _dot_exact(a, b):
    return jnp.dot(a, b, preferred_element_type=F32, precision=lax.Precision.HIGHEST)


def _iota2(shape, dim):
    return lax.broadcasted_iota(jnp.int32, shape, dim)


def _wperm_kernel(wt_ref, o_ref, ws_ref):
    off = 0
    for a, n in _BIG_SEGS:
        o_ref[off:off + n, :] = wt_ref[0, a:a + n, :].astype(BF16)
        off += n
    ws_ref[...] = jnp.zeros_like(ws_ref)
    for g in range(SSD_GROUPS):
        base, h0 = g * LANE, g * SSD_HG
        ws_ref[base + SM_DTF:base + SM_DTF + SSD_HG, :] = wt_ref[0, _O_DTF + h0:_O_DTF + h0 + SSD_HG, :]
        ws_ref[base + SM_DTB:base + SM_DTB + SSD_HG, :] = wt_ref[0, _O_DTB + h0:_O_DTB + h0 + SSD_HG, :]
    ws_ref[SM_GAF:SM_GAF + GLA_GATE_RANK, :] = wt_ref[0, _O_GAF:_O_GAF + GLA_GATE_RANK, :]
    ws_ref[SM_GAB:SM_GAB + GLA_GATE_RANK, :] = wt_ref[0, _O_GAB:_O_GAB + GLA_GATE_RANK, :]


def _permute_weight(w_t_all, layer, tc=256):
    _, n_in, d = w_t_all.shape
    return pl.pallas_call(
        _wperm_kernel,
        grid=(d // tc,),
        in_specs=[pl.BlockSpec((1, n_in, tc), lambda i: (layer, 0, i))],
        out_specs=[
            pl.BlockSpec((U_WIDTH, tc), lambda i: (0, i)),
            pl.BlockSpec((SMALL_W, tc), lambda i: (0, i)),
        ],
        out_shape=[
            jax.ShapeDtypeStruct((U_WIDTH, d), BF16),
            jax.ShapeDtypeStruct((SMALL_W, d), F32),
        ],
        compiler_params=_cparams(("parallel",)),
        name="w_permute",
    )(w_t_all)


def _head_rms(r, w_row):
    G = 2 * LANE
    er = _iota2((G, G), 0) // NA_HEAD_DIM
    ec = _iota2((G, G), 1) // NA_HEAD_DIM
    e_blk = (er == ec).astype(BF16)
    outs = []
    for a in range(0, r.shape[1], G):
        x = r[:, a:a + G]
        ms = _dot((x * x).astype(BF16), e_blk) * (1.0 / NA_HEAD_DIM)
        outs.append(x * lax.rsqrt(ms + EPS))
    return jnp.concatenate(outs, axis=1) * w_row


def _inproj_kernel(x_ref, nw_ref, w_ref, ws_ref, qkw_ref, u_ref, us_ref, ust_ref, h_ref):
    tn = u_ref.shape[1]
    chunks = [(a, min(a + INPROJ_COL_CHUNK, tn)) for a in range(0, tn, INPROJ_COL_CHUNK)]

    @pl.when(pl.program_id(1) == 0)
    def _():
        x = x_ref[...]
        ms = jnp.mean(x * x, axis=-1, keepdims=True)
        h = (x * lax.rsqrt(ms + EPS) * nw_ref[...]).astype(BF16)
        h_ref[...] = h
        ws = ws_ref[...].astype(BF16)
        us_ref[...] = _nt(h, ws)
        dt_rows = jnp.concatenate([ws[g * LANE:g * LANE + SMALL_T_ROWS] for g in range(SSD_GROUPS)], axis=0)
        ust_ref[...] = _nt(dt_rows, h)
        for c, (a, b) in enumerate(chunks):
            r = _nt(h, w_ref[a:b, :])
            if c < 2:
                r = _head_rms(r, qkw_ref[c:c + 1, :])
            u_ref[:, a:b] = r.astype(BF16)

    @pl.when(pl.program_id(1) != 0)
    def _():
        for a, b in chunks:
            u_ref[:, a:b] = _nt(h_ref[...], w_ref[a:b, :]).astype(BF16)


def _inproj(x2, nw, w_big, w_small, qkw, tm=1024, tn=U_WIDTH // 4):
    assert (U_NQ, U_NK) == (0, INPROJ_COL_CHUNK) and NA_W == INPROJ_COL_CHUNK
    T = x2.shape[0]
    return pl.pallas_call(
        _inproj_kernel,
        grid=(T // tm, U_WIDTH // tn),
        in_specs=[
            pl.BlockSpec((tm, D_MODEL), lambda i, j: (i, 0)),
            pl.BlockSpec((1, D_MODEL), lambda i, j: (0, 0)),
            pl.BlockSpec((tn, D_MODEL), lambda i, j: (j, 0)),
            pl.BlockSpec((SMALL_W, D_MODEL), lambda i, j: (0, 0)),
            pl.BlockSpec((8, NA_W), lambda i, j: (0, 0)),
        ],
        out_specs=[
            pl.BlockSpec((tm, tn), lambda i, j: (i, j)),
            pl.BlockSpec((tm, SMALL_W), lambda i, j: (i, 0)),
            pl.BlockSpec((SSD_GROUPS * SMALL_T_ROWS, tm), lambda i, j: (0, i)),
        ],
        out_shape=[
            jax.ShapeDtypeStruct((T, U_WIDTH), BF16),
            jax.ShapeDtypeStruct((T, SMALL_W), F32),
            jax.ShapeDtypeStruct((SSD_GROUPS * SMALL_T_ROWS, T), F32),
        ],
        scratch_shapes=[pltpu.VMEM((tm, D_MODEL), BF16)],
        compiler_params=_cparams(("parallel", "arbitrary")),
        name="inproj",
    )(x2, nw, w_big, w_small, qkw)


def _conv_kernel(u_ref, w_ref, b_ref, o_ref):
    x = u_ref[0].astype(F32)
    S = x.shape[0]
    row = _iota2(x.shape, 0)
    acc = jnp.zeros_like(x) + b_ref[...]
    pad = SSD_CONV // 2
    for k in range(SSD_CONV):
        d = k - pad
        if d == 0:
            xs = x
        else:
            xs = pltpu.roll(x, (-d) % S, axis=0)
            valid = (row + d >= 0) & (row + d < S)
            xs = jnp.where(valid, xs, 0.0)
        acc = acc + w_ref[k:k + 1, :] * xs
    o_ref[0] = _silu(acc).astype(BF16)


def _conv(u3, conv_w, conv_b, tc=256):
    B, S, _ = u3.shape
    nblk = SSD_CONV_DIM // tc
    off = U_XBC // tc
    return pl.pallas_call(
        _conv_kernel,
        grid=(B, nblk),
        in_specs=[
            pl.BlockSpec((1, S, tc), lambda b, c: (b, 0, off + c)),
            pl.BlockSpec((SSD_CONV, tc), lambda b, c: (0, c)),
            pl.BlockSpec((1, tc), lambda b, c: (0, c)),
        ],
        out_specs=pl.BlockSpec((1, S, tc), lambda b, c: (b, 0, c)),
        out_shape=jax.ShapeDtypeStruct((B, S, SSD_CONV_DIM), BF16),
        compiler_params=_cparams(("parallel", "parallel")),
        name="ssd_conv",
    )(u3, conv_w, conv_b)


def _split_hi_lo(v):
    hi = v.astype(BF16)
    lo = (v - hi.astype(F32)).astype(BF16)
    return jnp.concatenate([hi, lo], axis=1)


def _split3(v, axis):
    hi = v.astype(BF16)
    r1 = v - hi.astype(F32)
    mid = r1.astype(BF16)
    lo = (r1 - mid.astype(F32)).astype(BF16)
    return jnp.concatenate([hi, mid, lo], axis=axis)


def _ssd_kernel(x_ref, b_ref, c_ref, z_ref, us_ref, ust_ref, prow_ref, pcol_ref, drow_ref, nw_ref,
                o_ref, acc_ref, cumc_ref, cumr_ref, wdt_ref, dec_ref, xs_ref, decx_ref, sst_ref, st_ref):
    L = SSD_CHUNK
    S = x_ref.shape[1]
    nc = S // L
    W = SSD_GROUP_W
    HG = SSD_HG
    R = 2 * HG

    ii = _iota2((L, L), 0)
    jj = _iota2((L, L), 1)
    tril = ii >= jj
    triu = jj >= ii
    tril_b = tril.astype(BF16)
    triu_b = triu.astype(BF16)

    er = _iota2((LANE, 2 * W), 0)
    ec = _iota2((LANE, 2 * W), 1)
    e = (er == jnp.where(ec < W, ec // SSD_HEAD_DIM, (ec - W) // SSD_HEAD_DIM + HG)).astype(BF16)
    e2 = jnp.concatenate([e, e], axis=0)

    bias_row = prow_ref[0:1, :]
    a_row = prow_ref[1:2, :]
    bias_col = pcol_ref[0, :, 0:1]
    a_col = pcol_ref[0, :, 1:2]

    lane_fwd = _iota2((L, LANE), 1) < HG
    row_fwd = _iota2((R, L), 0) < HG
    lane_half = _iota2((L, LANE), 1) < SSD_HEAD_DIM

    def decay_sums(c):
        rs = pl.ds(pl.multiple_of(c * L, L), L)
        dt_c = _softplus(us_ref[0, rs, :] + bias_row)
        a_c = dt_c * a_row
        dt_r = _softplus(ust_ref[:, rs] + bias_col)
        a_r = dt_r * a_col
        return dt_c, a_c, _dot(tril_b, _split3(a_c, 1)), dt_r, a_r, _dot(_split3(a_r, 0), triu_b)

    def decay_store(c, sums):
        rs = pl.ds(pl.multiple_of(c * L, L), L)
        dt_c, a_c, pp, dt_r, a_r, pr = sums
        p_c = pp[:, 0:LANE] + pp[:, LANE:2 * LANE] + pp[:, 2 * LANE:3 * LANE]
        tot_c = p_c[L - 1:L, :]
        cum_c = jnp.where(lane_fwd, p_c, tot_c - p_c + a_c)
        cumc_ref[rs, :] = cum_c * LOG2E
        wdt_ref[rs, :] = jnp.exp(tot_c - cum_c) * dt_c
        dec_ref[c] = jnp.broadcast_to(jnp.exp(tot_c), (16, LANE))
        p_r = pr[0:R] + pr[R:2 * R] + pr[2 * R:3 * R]
        tot_r = p_r[:, L - 1:L]
        cum_r = jnp.where(row_fwd, p_r, tot_r - p_r + a_r)
        cumr_ref[:, rs] = (cum_r - jnp.log(dt_r)) * LOG2E

    def intra_chunk(c):
        rs = pl.ds(pl.multiple_of(c * L, L), L)
        x_c = x_ref[0, rs, :]
        b_c = b_ref[0, rs, :]
        cb = _nt(c_ref[0, rs, :], b_c)
        ex = _dot(_split_hi_lo(jnp.concatenate([wdt_ref[rs, :], dec_ref[c]], axis=0)), e2)
        decx_ref[c] = ex[L:L + 8]
        xf = x_c.astype(F32)
        xw = jnp.concatenate([xf * ex[0:L, 0:W], xf * ex[0:L, W:2 * W]], axis=1).astype(BF16)
        xs_ref[c] = _tn(b_c, xw)
        cum_c = cumc_ref[rs, :]
        cum_r = cumr_ref[:, rs]
        for hp in range(HG // 2):
            ms = []
            for hh in range(2):
                hf = 2 * hp + hh
                hb = HG + 2 * hp + hh
                decf = jnp.exp2(jnp.where(tril, cum_c[:, hf:hf + 1] - cum_r[hf:hf + 1, :], -jnp.inf))
                decb = jnp.exp2(jnp.where(triu, cum_c[:, hb:hb + 1] - cum_r[hb:hb + 1, :], -jnp.inf))
                ms.append((cb * (decf + decb)).astype(BF16))
            m2 = jnp.concatenate(ms, axis=1)
            xp = x_c[:, hp * LANE:(hp + 1) * LANE]
            xz = jnp.zeros_like(xp)
            x2 = jnp.concatenate([jnp.where(lane_half, xp, xz), jnp.where(lane_half, xz, xp)], axis=0)
            acc_ref[rs, hp * LANE:(hp + 1) * LANE] = _dot(m2, x2)

    G = SSD_CHUNKS_PER_STEP
    n_steps = nc // G

    def fused_body(i, carry):
        nxt = jnp.minimum(i + 1, n_steps - 1)
        sums = [decay_sums(G * nxt + j) for j in range(G)]
        for j in range(G):
            intra_chunk(G * i + j)
        for j in range(G):
            decay_store(G * nxt + j, sums[j])
        return carry

    for j in range(G):
        decay_store(j, decay_sums(j))
    lax.fori_loop(0, n_steps, fused_body, 0)

    st_ref[...] = jnp.zeros_like(st_ref)

    def state_body(i, carry):
        for d, c in enumerate((i, nc - 1 - i)):
            ls = slice(d * W, (d + 1) * W)
            s_prev = st_ref[:, ls]
            sst_ref[c, :, ls] = s_prev.astype(BF16)
            st_ref[:, ls] = s_prev * decx_ref[c, 0:1, ls] + xs_ref[c, :, ls]
        return carry

    lax.fori_loop(0, nc, state_body, 0)

    def out_body(c, carry):
        rs = pl.ds(pl.multiple_of(c * L, L), L)
        yo = _dot(c_ref[0, rs, :], sst_ref[c])
        sc = _dot(_split_hi_lo(jnp.exp2(cumc_ref[rs, :])), e2)
        xf = x_ref[0, rs, :].astype(F32)
        y = acc_ref[rs, :] + yo[:, 0:W] * sc[:, 0:W] + yo[:, W:2 * W] * sc[:, W:2 * W] + xf * drow_ref[...]
        y = y * _silu(z_ref[0, rs, :].astype(F32))
        ms = jnp.mean(y * y, axis=-1, keepdims=True)
        o_ref[0, rs, :] = (y * lax.rsqrt(ms + EPS) * nw_ref[...]).astype(BF16)
        return carry

    lax.fori_loop(0, nc, out_body, 0, unroll=2)


def _ssd(xbc, u3, us3, ust, prow, pcol, drow, nw):
    B, S, _ = xbc.shape
    W = SSD_GROUP_W
    N = SSD_STATE
    nc = S // SSD_CHUNK
    return pl.pallas_call(
        _ssd_kernel,
        grid=(B, SSD_GROUPS),
        in_specs=[
            pl.BlockSpec((1, S, W), lambda b, g: (b, 0, g)),
            pl.BlockSpec((1, S, N), lambda b, g: (b, 0, SSD_D_INNER // N + g)),
            pl.BlockSpec((1, S, N), lambda b, g: (b, 0, SSD_D_INNER // N + SSD_GROUPS + g)),
            pl.BlockSpec((1, S, W), lambda b, g: (b, 0, U_Z // W + g)),
            pl.BlockSpec((1, S, LANE), lambda b, g: (b, 0, g)),
            pl.BlockSpec((SMALL_T_ROWS, S), lambda b, g: (g, b)),
            pl.BlockSpec((8, LANE), lambda b, g: (g, 0)),
            pl.BlockSpec((1, SMALL_T_ROWS, 8), lambda b, g: (g, 0, 0)),
            pl.BlockSpec((1, W), lambda b, g: (0, g)),
            pl.BlockSpec((1, W), lambda b, g: (0, g)),
        ],
        out_specs=pl.BlockSpec((1, S, W), lambda b, g: (b, 0, g)),
        out_shape=jax.ShapeDtypeStruct((B, S, SSD_D_INNER), BF16),
        scratch_shapes=[
            pltpu.VMEM((S, W), F32),
            pltpu.VMEM((S, LANE), F32),
            pltpu.VMEM((SMALL_T_ROWS, S), F32),
            pltpu.VMEM((S, LANE), F32),
            pltpu.VMEM((nc, 16, LANE), F32),
            pltpu.VMEM((nc, N, 2 * W), F32),
            pltpu.VMEM((nc, 8, 2 * W), F32),
            pltpu.VMEM((nc, N, 2 * W), BF16),
            pltpu.VMEM((N, 2 * W), F32),
        ],
        compiler_params=_cparams(("parallel", "parallel")),
        name="ssd_scan",
    )(xbc, xbc, xbc, u3, us3, ust, prow, pcol, drow, nw)


def _gla_kernel(q_ref, k_ref, v_ref, gg_ref, us_ref, a2_ref, bias_ref, nw_ref,
                o_ref, acc_ref, g_ref, qd_ref, kd_ref, kdp_ref, qcat_ref, x_ref, dec_ref, sst_ref):
    L = GLA_CHUNK
    BL = 2 * L
    DK = GLA_DK
    S = q_ref.shape[1]
    nb = S // BL
    scale = DK ** -0.5

    ii = _iota2((BL, BL), 0)
    jj = _iota2((BL, BL), 1)
    same = (ii // L) == (jj // L)
    masks = (same & (ii >= jj), same & (jj >= ii))
    tri2 = masks[0].astype(BF16)
    par_row = _iota2((BL, DK), 0) // L

    ga = us_ref[0].astype(BF16)
    g_ref[...] = _log2_sigmoid(_dot(ga, a2_ref[...]) + bias_ref[...]) * (1.0 / GLA_GATE_NORM)

    def decay_sums(i):
        g = g_ref[pl.ds(pl.multiple_of(i * BL, BL), BL), :]
        hi = g.astype(BF16)
        r1 = g - hi.astype(F32)
        mid = r1.astype(BF16)
        lo = (r1 - mid.astype(F32)).astype(BF16)
        return g, _dot(tri2, jnp.concatenate([hi, mid, lo], axis=1))

    def decay_block(i, sums):
        rs = pl.ds(pl.multiple_of(i * BL, BL), BL)
        g, pp = sums
        p = pp[:, 0:2 * DK] + pp[:, 2 * DK:4 * DK] + pp[:, 4 * DK:6 * DK]
        q_c = q_ref[0, rs, :].astype(F32) * scale
        k_c = k_ref[0, rs, :].astype(F32)
        zero = jnp.zeros((BL, DK), BF16)
        for d in range(2):
            p_d = p[:, d * DK:(d + 1) * DK]
            tot = jnp.where(par_row == 0, p_d[L - 1:L, :], p_d[BL - 1:BL, :])
            b = p_d if d == 0 else tot - p_d + g[:, DK:]
            qd = (q_c * jnp.exp2(b)).astype(BF16)
            kdec = (k_c * jnp.exp2(tot - b)).astype(BF16)
            qd_ref[d, rs, :] = qd
            kd_ref[d, rs, :] = (k_c * jnp.exp2(-b)).astype(BF16)
            for par in range(2):
                sel = par_row == par
                kdp_ref[d, rs, par * DK:(par + 1) * DK] = jnp.where(sel, kdec, zero)
                qcat_ref[rs, (2 * d + par) * DK:(2 * d + par + 1) * DK] = jnp.where(sel, qd, zero)
                last = (par + 1) * L - 1
                dec_ref[d, 2 * i + par] = jnp.broadcast_to(jnp.exp2(p_d[last:last + 1, :]), (8, DK))

    G = GLA_BLOCKS_PER_STEP
    n_groups = nb // G

    def intra_group(i):
        blks = [G * i + j for j in range(G)]
        rss = [pl.ds(pl.multiple_of(b * BL, BL), BL) for b in blks]
        vs = [v_ref[0, rs, :] for rs in rss]
        atts = [[_nt(qd_ref[d, rs, :], kd_ref[d, rs, :]) for d in range(2)] for rs in rss]
        for j, b in enumerate(blks):
            for d in range(2):
                x_ref[d, b] = _tn(vs[j], kdp_ref[d, rss[j], :])
        for j in range(G):
            att = jnp.where(masks[0], atts[j][0], 0.0) + jnp.where(masks[1], atts[j][1], 0.0)
            acc_ref[rss[j], :] = _dot(att.astype(BF16), vs[j])

    def fused_body(i, carry):
        nxt = jnp.minimum(i + 1, n_groups - 1)
        sums = [decay_sums(G * nxt + j) for j in range(G)]
        intra_group(i)
        for j in range(G):
            decay_block(G * nxt + j, sums[j])
        return carry

    for j in range(G):
        decay_block(j, decay_sums(j))
    lax.fori_loop(0, n_groups, fused_body, 0)

    def state_body(d):
        def body(i, s):
            b = i if d == 0 else nb - 1 - i
            for par in ((0, 1) if d == 0 else (1, 0)):
                lane0 = (2 * d + par) * DK
                sst_ref[b, :, lane0:lane0 + DK] = s.astype(BF16)
                s = s * dec_ref[d, 2 * b + par, 0:1, :] + x_ref[d, b, :, par * DK:(par + 1) * DK]
            return s
        return body

    for d in range(2):
        lax.fori_loop(0, nb, state_body(d), jnp.zeros((GLA_DV, DK), F32))

    def out_body(b, carry):
        rs = pl.ds(pl.multiple_of(b * BL, BL), BL)
        o = acc_ref[rs, :] + _nt(qcat_ref[rs, :], sst_ref[b])
        ms = jnp.mean(o * o, axis=-1, keepdims=True)
        o = o * lax.rsqrt(ms + EPS) * nw_ref[...]
        o_ref[0, rs, :] = (o * _silu(gg_ref[0, rs, :].astype(F32))).astype(BF16)
        return carry

    lax.fori_loop(0, nb, out_body, 0, unroll=GLA_OUT_UNROLL)


def _gla(u3, us3, a2, bias, nw):
    B, S, _ = u3.shape
    DK, DV = GLA_DK, GLA_DV
    nc = S // GLA_CHUNK
    nb = nc // 2
    return pl.pallas_call(
        _gla_kernel,
        grid=(B, GLA_HEADS),
        in_specs=[
            pl.BlockSpec((1, S, DK), lambda b, h: (b, 0, U_GQ // DK + h)),
            pl.BlockSpec((1, S, DK), lambda b, h: (b, 0, U_GK // DK + h)),
            pl.BlockSpec((1, S, DV), lambda b, h: (b, 0, U_GV // DV + h)),
            pl.BlockSpec((1, S, DV), lambda b, h: (b, 0, U_GG // DV + h)),
            pl.BlockSpec((1, S, LANE), lambda b, h: (b, 0, 0)),
            pl.BlockSpec((LANE, 2 * DK), lambda b, h: (0, h)),
            pl.BlockSpec((1, 2 * DK), lambda b, h: (0, h)),
            pl.BlockSpec((1, DV), lambda b, h: (0, 0)),
        ],
        out_specs=pl.BlockSpec((1, S, DV), lambda b, h: (b, 0, h)),
        out_shape=jax.ShapeDtypeStruct((B, S, GLA_VAL_W), BF16),
        scratch_shapes=[
            pltpu.VMEM((S, DV), F32),
            pltpu.VMEM((S, 2 * DK), F32),
            pltpu.VMEM((2, S, DK), BF16),
            pltpu.VMEM((2, S, DK), BF16),
            pltpu.VMEM((2, S, 2 * DK), BF16),
            pltpu.VMEM((S, 4 * DK), BF16),
            pltpu.VMEM((2, nb, DV, 2 * DK), F32),
            pltpu.VMEM((2, nc, 8, DK), F32),
            pltpu.VMEM((nb, DV, 4 * DK), BF16),
        ],
        compiler_params=_cparams(("parallel", "parallel")),
        name="gla_scan",
    )(u3, u3, u3, u3, us3, a2, bias, nw)


def _na_bias_rows(rpb):
    H, R, C = rpb.shape
    n_pos = GRID_W - NA_WIN_W
    n_neg = GRID_W - NA_WIN_W + 2
    ext = jnp.concatenate([rpb[:, :, NA_WIN_W - 1:], jnp.repeat(rpb[:, :, C - 1:], n_pos, axis=2),
                           jnp.repeat(rpb[:, :, 0:1], n_neg, axis=2), rpb[:, :, 1:NA_WIN_W - 1]], axis=2)
    return jnp.pad(ext, ((0, 0), (0, 1), (0, 0))).reshape(H // 2, 2, R + 1, 2 * GRID_W)


def _na_kernel(flag_ref, q_ref, k_ref, v_ref, ext_ref, o_ref, vx_ref, tab_ref):
    NB, S, _ = q_ref.shape
    rows = S // GRID_W
    total = NB * rows
    win_h = NA_WIN_H
    nk = win_h * GRID_W

    vx_ref[:, :, 0:LANE] = v_ref[...]
    vx_ref[:, :, LANE:2 * LANE] = jnp.ones((NB, S, LANE), BF16)
    bound_ok = flag_ref[0] != 0

    first_q = _iota2((GRID_W, LANE), 1) < NA_HEAD_DIM

    @pl.when(pl.program_id(1) == 0)
    def _():
        q_col = _iota2((GRID_W, 2 * GRID_W), 0)
        k_col = _iota2((GRID_W, 2 * GRID_W), 1) % GRID_W
        w_start = jnp.clip(q_col - NA_WIN_W // 2, 0, GRID_W - NA_WIN_W)
        in_window = (k_col >= w_start) & (k_col < w_start + NA_WIN_W)
        low = _iota2((GRID_W, 2 * GRID_W), 1) < GRID_W
        for hd in range(2):
            def skewed(rr, shift):
                row = jnp.broadcast_to(ext_ref[0, hd, rr:rr + 1, :], (GRID_W, 2 * GRID_W))
                return pltpu.roll(row, shift, axis=1, stride=1, stride_axis=0)

            for e in range(2 * NA_WIN_H - 2):
                t = jnp.where(low, skewed(e, 0), skewed(e + 1, GRID_W))
                tab_ref[e, hd * GRID_W:(hd + 1) * GRID_W, :] = jnp.where(in_window, t, -jnp.inf)

    def locate(r):
        bi = r // rows
        rl = r - bi * rows
        r0 = jnp.clip(rl - win_h // 2, 0, rows - win_h)
        return bi, pl.multiple_of(rl * GRID_W, GRID_W), rl - r0, pl.multiple_of(r0 * GRID_W, GRID_W)

    def scores(r):
        bi, q0, delta, k0 = locate(r)
        q = q_ref[bi, pl.ds(q0, GRID_W), :]
        zero = jnp.zeros_like(q)
        qs = jnp.concatenate([jnp.where(first_q, q, zero), jnp.where(first_q, zero, q)], axis=0)
        rr0 = (win_h - 1) - delta
        bias = jnp.concatenate([tab_ref[rr0 + w] for w in range(0, win_h, 2)], axis=1)
        return _nt(qs, k_ref[bi, pl.ds(k0, nk), :]) + bias

    def probs_exact(s):
        return jnp.exp2(s - jnp.max(s, axis=-1, keepdims=True)).astype(BF16)

    def probs_bounded(r):
        return jnp.exp2(scores(r)).astype(BF16)

    def attend(r, p):
        bi, q0, _, k0 = locate(r)
        ox = _dot(p, vx_ref[bi, pl.ds(k0, nk), :])
        o = ox[:, 0:LANE] / ox[:, LANE:2 * LANE]
        o_ref[bi, pl.ds(q0, GRID_W), :] = jnp.where(first_q, o[:GRID_W], o[GRID_W:]).astype(BF16)

    @pl.when(bound_ok)
    def _():
        U = NA_ROWS_PER_STEP

        def row_body(i, p_prev):
            r = i * U
            for j in range(U):
                attend(r - U + j, p_prev[j])
            return tuple(probs_bounded(r + j) for j in range(U))

        p_last = lax.fori_loop(1, total // U, row_body, tuple(probs_bounded(j) for j in range(U)))
        for j in range(U):
            attend(total - U + j, p_last[j])

    @pl.when(jnp.logical_not(bound_ok))
    def _():
        U = NA_ROWS_PER_STEP_EXACT

        def row_body(i, carry):
            s_cur, p_prev = carry
            r = i * U
            for j in range(U):
                attend(r - U + j, p_prev[j])
            p = tuple(probs_exact(s) for s in s_cur)
            s_next = tuple(scores(jnp.minimum(r + U + j, total - 1)) for j in range(U))
            return s_next, p

        p0 = tuple(probs_exact(scores(j)) for j in range(U))
        s1 = tuple(scores(U + j) for j in range(U))
        _, p_last = lax.fori_loop(1, total // U, row_body, (s1, p0))
        for j in range(U):
            attend(total - U + j, p_last[j])


def _na_score_bound(rpb, q_norm_w, k_norm_w):
    H = rpb.shape[0]
    qk = (NA_HEAD_DIM ** 0.5 * LOG2E * (1.0 + NA_BOUND_SLACK)) * jnp.max(jnp.abs(q_norm_w)) * jnp.max(jnp.abs(k_norm_w))
    b_max = jnp.max(rpb.reshape(H, -1), axis=1) * LOG2E
    b_self = rpb[:, NA_WIN_H - 1, NA_WIN_W - 1] * LOG2E
    bound = qk + b_max
    flag = jnp.all(bound - (b_self - qk) <= NA_MAX_BOUND_GAP)
    return bound, flag.astype(jnp.int32).reshape(1)


def _na(u3, bias_rows, flag):
    B, S, _ = u3.shape
    assert S // GRID_W >= NA_WIN_H and NA_WIN_H % 2 == 0 and 2 * GRID_W == LANE
    nb = math.gcd(B, NA_BATCH_PER_STEP)
    assert (nb * (S // GRID_W)) % NA_ROWS_PER_STEP == 0
    return pl.pallas_call(
        _na_kernel,
        grid=(NA_HEADS // 2, B // nb),
        in_specs=[
            pl.BlockSpec(memory_space=pltpu.SMEM),
            pl.BlockSpec((nb, S, LANE), lambda h, b: (b, 0, U_NQ // LANE + h)),
            pl.BlockSpec((nb, S, LANE), lambda h, b: (b, 0, U_NK // LANE + h)),
            pl.BlockSpec((nb, S, LANE), lambda h, b: (b, 0, U_NV // LANE + h)),
            pl.BlockSpec((1, 2, 2 * NA_WIN_H, 2 * GRID_W), lambda h, b: (h, 0, 0, 0)),
        ],
        out_specs=pl.BlockSpec((nb, S, LANE), lambda h, b: (b, 0, h)),
        out_shape=jax.ShapeDtypeStruct((B, S, NA_W), BF16),
        scratch_shapes=[
            pltpu.VMEM((nb, S, 2 * LANE), BF16),
            pltpu.VMEM((2 * NA_WIN_H - 2, 2 * GRID_W, 2 * GRID_W), F32),
        ],
        compiler_params=_cparams(("parallel", "arbitrary")),
        name="na_attn",
    )(flag, u3, u3, u3, bias_rows)


def _merge_kernel(x_ref, ys_ref, yg_ref, yn_ref, gate_ref, ws_ref, wg_ref, wn_ref, wo_ref, o_ref):
    D = D_MODEL
    mixed = _sigmoid(gate_ref[:, 0:D].astype(F32)) * _dot(ys_ref[...], ws_ref[...])
    mixed += _sigmoid(gate_ref[:, D:2 * D].astype(F32)) * _dot(yg_ref[...], wg_ref[...])
    mixed += _sigmoid(gate_ref[:, 2 * D:3 * D].astype(F32)) * _dot(yn_ref[...], wn_ref[...])
    o_ref[...] = x_ref[...] + _dot(mixed.astype(BF16), wo_ref[...])


def _merge(x2, ys, yg, yn, u2, ws, wg, wn, wo, tm=512):
    T = x2.shape[0]
    D = D_MODEL
    row = lambda i: (i, 0)
    fixed = lambda i: (0, 0)
    return pl.pallas_call(
        _merge_kernel,
        grid=(T // tm,),
        in_specs=[
            pl.BlockSpec((tm, D), row),
            pl.BlockSpec((tm, D), row),
            pl.BlockSpec((tm, D), row),
            pl.BlockSpec((tm, D), row),
            pl.BlockSpec((tm, N_BRANCH * D), lambda i: (i, U_GATE // (N_BRANCH * D))),
            pl.BlockSpec((D, D), fixed),
            pl.BlockSpec((D, D), fixed),
            pl.BlockSpec((D, D), fixed),
            pl.BlockSpec((D, D), fixed),
        ],
        out_specs=pl.BlockSpec((tm, D), row),
        out_shape=jax.ShapeDtypeStruct((T, D), F32),
        compiler_params=_cparams(("parallel",)),
        name="merge",
    )(x2, ys, yg, yn, u2, ws, wg, wn, wo)


def _mlp_kernel(x_ref, nw_ref, w1_ref, w2_ref, o_ref, *, tf):
    x = x_ref[...]
    ms = jnp.mean(x * x, axis=-1, keepdims=True)
    h = (x * lax.rsqrt(ms + EPS) * nw_ref[...]).astype(BF16)
    acc = x
    for f in range(D_FF // tf):
        a = jnp.maximum(_dot(h, w1_ref[:, f * tf:(f + 1) * tf]), 0.0)
        acc = acc + _dot((a * a).astype(BF16), w2_ref[f * tf:(f + 1) * tf, :])
    o_ref[...] = acc


def _mlp(x2, nw, w1, w2, tm=512, tf=1024):
    T = x2.shape[0]
    D = D_MODEL
    return pl.pallas_call(
        functools.partial(_mlp_kernel, tf=tf),
        grid=(T // tm,),
        in_specs=[
            pl.BlockSpec((tm, D), lambda i: (i, 0)),
            pl.BlockSpec((1, D), lambda i: (0, 0)),
            pl.BlockSpec((D, D_FF), lambda i: (0, 0)),
            pl.BlockSpec((D_FF, D), lambda i: (0, 0)),
        ],
        out_specs=pl.BlockSpec((tm, D), lambda i: (i, 0)),
        out_shape=jax.ShapeDtypeStruct((T, D), F32),
        compiler_params=_cparams(("parallel",)),
        name="mlp",
    )(x2, nw, w1, w2)


def _pad_rows(w, start, total):
    return jnp.zeros((total, w.shape[1]), w.dtype).at[start:start + w.shape[0]].set(w)


def kernel(x, norm_mix_w, w_in, ssd_conv_w, ssd_conv_b, ssd_dt_bias_f, ssd_dt_bias_b, ssd_a_log_f,
           ssd_a_log_b, ssd_d, ssd_norm_w, gla_a2_f, gla_a2_bias_f, gla_a2_b, gla_a2_bias_b,
           gla_norm_w, na_q_norm_w, na_k_norm_w, na_rpb, w_branch_ssd, w_branch_gla, w_branch_na,
           w_out, norm_mlp_w, w_ff1, w_ff2):
    B, S, D = x.shape
    T = B * S
    depth = w_in.shape[0]
    rows = S // GRID_W
    x2 = x.reshape(T, D)
    w_in_t = jnp.swapaxes(w_in, 1, 2)
    for l in range(depth):
        w_big, w_small = _permute_weight(w_in_t, l)
        hg = (SSD_GROUPS, SSD_HG)
        zeros_r = jnp.zeros((SSD_GROUPS, LANE - 2 * SSD_HG), F32)
        dt_bias = jnp.concatenate([ssd_dt_bias_f[l].reshape(hg), ssd_dt_bias_b[l].reshape(hg), zeros_r], axis=1)
        a_neg = jnp.concatenate([-jnp.exp(ssd_a_log_f[l]).reshape(hg), -jnp.exp(ssd_a_log_b[l]).reshape(hg),
                                 zeros_r], axis=1)
        prow3 = jnp.concatenate([dt_bias[:, None], a_neg[:, None], jnp.zeros((SSD_GROUPS, 6, LANE), F32)], axis=1)
        prow = prow3.reshape(SSD_GROUPS * 8, LANE)
        pcol = jnp.transpose(prow3[:, :, :SMALL_T_ROWS], (0, 2, 1))
        drow = jnp.repeat(ssd_d[l], SSD_HEAD_DIM)[None, :]
        hk = (GLA_HEADS, 1, GLA_DK)
        a2 = jnp.concatenate([_pad_rows(gla_a2_f[l], SM_GAF, LANE).reshape((LANE,) + hk),
                              _pad_rows(gla_a2_b[l], SM_GAB, LANE).reshape((LANE,) + hk)],
                             axis=2).reshape(LANE, -1).astype(BF16)
        a2_bias = jnp.concatenate([gla_a2_bias_f[l].reshape(hk), gla_a2_bias_b[l].reshape(hk)],
                                  axis=1).reshape(1, -1)
        na_bound, na_flag = _na_score_bound(na_rpb[l], na_q_norm_w[l], na_k_norm_w[l])
        table = _na_bias_rows(na_rpb[l] * LOG2E - na_bound[:, None, None])
        q_row = jnp.tile(na_q_norm_w[l] * (NA_HEAD_DIM ** -0.5 * LOG2E), NA_HEADS)
        k_row = jnp.tile(na_k_norm_w[l], NA_HEADS)
        qkw = jnp.concatenate([q_row[None], k_row[None], jnp.zeros((6, NA_W), F32)], axis=0)

        u2, us2, ust = _inproj(x2, norm_mix_w[l][None, :], w_big, w_small, qkw)
        u3 = u2.reshape(B, S, U_WIDTH)
        us3 = us2.reshape(B, S, SMALL_W)
        xbc = _conv(u3, ssd_conv_w[l], ssd_conv_b[l][None, :])
        y_ssd = _ssd(xbc, u3, us3, ust, prow, pcol, drow, ssd_norm_w[l][None, :])
        y_gla = _gla(u3, us3, a2, a2_bias, gla_norm_w[l][None, :])
        y_na = _na(u3, table, na_flag)
        x2 = _merge(x2, y_ssd.reshape(T, -1), y_gla.reshape(T, -1), y_na.reshape(T, -1), u2,
                    w_branch_ssd[l].astype(BF16), w_branch_gla[l].astype(BF16),
                    w_branch_na[l].astype(BF16), w_out[l].astype(BF16))
        x2 = _mlp(x2, norm_mlp_w[l][None, :], w_ff1[l].astype(BF16), w_ff2[l].astype(BF16))
    return x2.reshape(B, S, D)
```

```python
import functools
import math

import jax
import jax.numpy as jnp
import numpy as np
from jax import lax
from jax.experimental import pallas as pl
from jax.experimental.pallas import tpu as pltpu

F32 = jnp.float32
BF16 = jnp.bfloat16

EPS = 1e-6
D_MODEL = 1024
GRID_W = 64

SSD_HEADS = 16
SSD_HEAD_DIM = 64
SSD_D_INNER = 1024
SSD_GROUPS = 2
SSD_STATE = 128
SSD_CONV = 5
SSD_CONV_DIM = 1536
SSD_CHUNK = 128
SSD_CHUNKS_PER_STEP = 2
SSD_GROUP_W = SSD_D_INNER // SSD_GROUPS
SSD_HG = SSD_HEADS // SSD_GROUPS

GLA_HEADS = 4
GLA_DK = 128
GLA_DV = 256
GLA_KEY_W = 512
GLA_VAL_W = 1024
GLA_GATE_RANK = 16
GLA_GATE_NORM = 16.0
GLA_CHUNK = 64
GLA_BLOCKS_PER_STEP = 2
GLA_OUT_UNROLL = 8

NA_HEADS = 16
NA_HEAD_DIM = 64
NA_W = 1024
NA_WIN_H = 8
NA_WIN_W = 16
NA_BATCH_PER_STEP = 4
NA_ROWS_PER_STEP = 8
NA_ROWS_PER_STEP_EXACT = 2
LOG2E = 1.4426950408889634
NA_BOUND_SLACK = 0.02
NA_MAX_BOUND_GAP = 90.0

N_BRANCH = 3
D_FF = 4096

IN_SIZES = (SSD_D_INNER, SSD_CONV_DIM, SSD_HEADS, SSD_HEADS,
            GLA_KEY_W, GLA_KEY_W, GLA_VAL_W, GLA_VAL_W, GLA_GATE_RANK, GLA_GATE_RANK,
            NA_W, NA_W, NA_W, N_BRANCH * D_MODEL)
_IN_OFF = np.concatenate([[0], np.cumsum(IN_SIZES)])
(_O_Z, _O_XBC, _O_DTF, _O_DTB, _O_GQ, _O_GK, _O_GV, _O_GG, _O_GAF, _O_GAB,
 _O_NQ, _O_NK, _O_NV, _O_GATE) = [int(v) for v in _IN_OFF[:-1]]

U_NQ = 0
U_NK = 1024
U_NV = 2048
U_GATE = 3072
U_Z = 6144
U_XBC = 7168
U_GQ = 8704
U_GK = 9216
U_GV = 9728
U_GG = 10752
U_WIDTH = 11776
INPROJ_COL_CHUNK = 1024
LANE = 128
VMEM_LIMIT = 56 * 1024 * 1024

SMALL_W = SSD_GROUPS * LANE
SM_DTF, SM_DTB, SM_GAF, SM_GAB = 0, 8, 16, 32
SMALL_T_ROWS = 2 * SSD_HG

_BIG_SEGS = ((_O_NQ, 6144), (_O_Z, 2560), (_O_GQ, 3072))


def _cparams(sem, vmem=VMEM_LIMIT):
    return pltpu.CompilerParams(dimension_semantics=sem, vmem_limit_bytes=vmem)


def _sigmoid(x):
    return 1.0 / (1.0 + jnp.exp(-x))


def _silu(x):
    return x * _sigmoid(x)


def _softplus(x):
    return jnp.maximum(x, 0.0) + jnp.log1p(jnp.exp(-jnp.abs(x)))


def _log2_sigmoid(x):
    t = x * (-LOG2E)
    return -(jnp.maximum(t, 0.0) + jnp.log2(1.0 + jnp.exp2(-jnp.abs(t))))


def _nt(a, b):
    return lax.dot_general(a, b, (((1,), (1,)), ((), ())), preferred_element_type=F32)


def _tn(a, b):
    return lax.dot_general(a, b, (((0,), (0,)), ((), ())), preferred_element_type=F32)


def _dot(a, b):
    return jnp.dot(a, b, preferred_element_type=F32)


def _dot_exact(a, b):
    return jnp.dot(a, b, preferred_element_type=F32, precision=lax.Precision.HIGHEST)


def _iota2(shape, dim):
    return lax.broadcasted_iota(jnp.int32, shape, dim)


def _wperm_kernel(wt_ref, o_ref, ws_ref):
    off = 0
    for a, n in _BIG_SEGS:
        o_ref[off:off + n, :] = wt_ref[0, a:a + n, :].astype(BF16)
        off += n
    ws_ref[...] = jnp.zeros_like(ws_ref)
    for g in range(SSD_GROUPS):
        base, h0 = g * LANE, g * SSD_HG
        ws_ref[base + SM_DTF:base + SM_DTF + SSD_HG, :] = wt_ref[0, _O_DTF + h0:_O_DTF + h0 + SSD_HG, :]
        ws_ref[base + SM_DTB:base + SM_DTB + SSD_HG, :] = wt_ref[0, _O_DTB + h0:_O_DTB + h0 + SSD_HG, :]
    ws_ref[SM_GAF:SM_GAF + GLA_GATE_RANK, :] = wt_ref[0, _O_GAF:_O_GAF + GLA_GATE_RANK, :]
    ws_ref[SM_GAB:SM_GAB + GLA_GATE_RANK, :] = wt_ref[0, _O_GAB:_O_GAB + GLA_GATE_RANK, :]


def _permute_weight(w_t_all, layer, tc=256):
    _, n_in, d = w_t_all.shape
    return pl.pallas_call(
        _wperm_kernel,
        grid=(d // tc,),
        in_specs=[pl.BlockSpec((1, n_in, tc), lambda i: (layer, 0, i))],
        out_specs=[
            pl.BlockSpec((U_WIDTH, tc), lambda i: (0, i)),
            pl.BlockSpec((SMALL_W, tc), lambda i: (0, i)),
        ],
        out_shape=[
            jax.ShapeDtypeStruct((U_WIDTH, d), BF16),
            jax.ShapeDtypeStruct((SMALL_W, d), F32),
        ],
        compiler_params=_cparams(("parallel",)),
        name="w_permute",
    )(w_t_all)


def _head_rms(r, w_row):
    G = 2 * LANE
    er = _iota2((G, G), 0) // NA_HEAD_DIM
    ec = _iota2((G, G), 1) // NA_HEAD_DIM
    e_blk = (er == ec).astype(BF16)
    outs = []
    for a in range(0, r.shape[1], G):
        x = r[:, a:a + G]
        ms = _dot((x * x).astype(BF16), e_blk) * (1.0 / NA_HEAD_DIM)
        outs.append(x * lax.rsqrt(ms + EPS))
    return jnp.concatenate(outs, axis=1) * w_row


def _inproj_kernel(x_ref, nw_ref, w_ref, ws_ref, qkw_ref, u_ref, us_ref, ust_ref, h_ref):
    tn = u_ref.shape[1]
    chunks = [(a, min(a + INPROJ_COL_CHUNK, tn)) for a in range(0, tn, INPROJ_COL_CHUNK)]

    @pl.when(pl.program_id(1) == 0)
    def _():
        x = x_ref[...]
        ms = jnp.mean(x * x, axis=-1, keepdims=True)
        h = (x * lax.rsqrt(ms + EPS) * nw_ref[...]).astype(BF16)
        h_ref[...] = h
        ws = ws_ref[...].astype(BF16)
        us_ref[...] = _nt(h, ws)
        dt_rows = jnp.concatenate([ws[g * LANE:g * LANE + SMALL_T_ROWS] for g in range(SSD_GROUPS)], axis=0)
        ust_ref[...] = _nt(dt_rows, h)
        for c, (a, b) in enumerate(chunks):
            r = _nt(h, w_ref[a:b, :])
            if c < 2:
                r = _head_rms(r, qkw_ref[c:c + 1, :])
            u_ref[:, a:b] = r.astype(BF16)

    @pl.when(pl.program_id(1) != 0)
    def _():
        for a, b in chunks:
            u_ref[:, a:b] = _nt(h_ref[...], w_ref[a:b, :]).astype(BF16)


def _inproj(x2, nw, w_big, w_small, qkw, tm=1024, tn=U_WIDTH // 4):
    assert (U_NQ, U_NK) == (0, INPROJ_COL_CHUNK) and NA_W == INPROJ_COL_CHUNK
    T = x2.shape[0]
    return pl.pallas_call(
        _inproj_kernel,
        grid=(T // tm, U_WIDTH // tn),
        in_specs=[
            pl.BlockSpec((tm, D_MODEL), lambda i, j: (i, 0)),
            pl.BlockSpec((1, D_MODEL), lambda i, j: (0, 0)),
            pl.BlockSpec((tn, D_MODEL), lambda i, j: (j, 0)),
            pl.BlockSpec((SMALL_W, D_MODEL), lambda i, j: (0, 0)),
            pl.BlockSpec((8, NA_W), lambda i, j: (0, 0)),
        ],
        out_specs=[
            pl.BlockSpec((tm, tn), lambda i, j: (i, j)),
            pl.BlockSpec((tm, SMALL_W), lambda i, j: (i, 0)),
            pl.BlockSpec((SSD_GROUPS * SMALL_T_ROWS, tm), lambda i, j: (0, i)),
        ],
        out_shape=[
            jax.ShapeDtypeStruct((T, U_WIDTH), BF16),
            jax.ShapeDtypeStruct((T, SMALL_W), F32),
            jax.ShapeDtypeStruct((SSD_GROUPS * SMALL_T_ROWS, T), F32),
        ],
        scratch_shapes=[pltpu.VMEM((tm, D_MODEL), BF16)],
        compiler_params=_cparams(("parallel", "arbitrary")),
        name="inproj",
    )(x2, nw, w_big, w_small, qkw)


def _conv_kernel(u_ref, w_ref, b_ref, o_ref):
    x = u_ref[0].astype(F32)
    S = x.shape[0]
    row = _iota2(x.shape, 0)
    acc = jnp.zeros_like(x) + b_ref[...]
    pad = SSD_CONV // 2
    for k in range(SSD_CONV):
        d = k - pad
        if d == 0:
            xs = x
        else:
            xs = pltpu.roll(x, (-d) % S, axis=0)
            valid = (row + d >= 0) & (row + d < S)
            xs = jnp.where(valid, xs, 0.0)
        acc = acc + w_ref[k:k + 1, :] * xs
    o_ref[0] = _silu(acc).astype(BF16)


def _conv(u3, conv_w, conv_b, tc=256):
    B, S, _ = u3.shape
    nblk = SSD_CONV_DIM // tc
    off = U_XBC // tc
    return pl.pallas_call(
        _conv_kernel,
        grid=(B, nblk),
        in_specs=[
            pl.BlockSpec((1, S, tc), lambda b, c: (b, 0, off + c)),
            pl.BlockSpec((SSD_CONV, tc), lambda b, c: (0, c)),
            pl.BlockSpec((1, tc), lambda b, c: (0, c)),
        ],
        out_specs=pl.BlockSpec((1, S, tc), lambda b, c: (b, 0, c)),
        out_shape=jax.ShapeDtypeStruct((B, S, SSD_CONV_DIM), BF16),
        compiler_params=_cparams(("parallel", "parallel")),
        name="ssd_conv",
    )(u3, conv_w, conv_b)


def _split_hi_lo(v):
    hi = v.astype(BF16)
    lo = (v - hi.astype(F32)).astype(BF16)
    return jnp.concatenate([hi, lo], axis=1)


def _split3(v, axis):
    hi = v.astype(BF16)
    r1 = v - hi.astype(F32)
    mid = r1.astype(BF16)
    lo = (r1 - mid.astype(F32)).astype(BF16)
    return jnp.concatenate([hi, mid, lo], axis=axis)


def _ssd_kernel(x_ref, b_ref, c_ref, z_ref, us_ref, ust_ref, prow_ref, pcol_ref, drow_ref, nw_ref,
                o_ref, acc_ref, cumc_ref, cumr_ref, wdt_ref, dec_ref, xs_ref, decx_ref, sst_ref, st_ref):
    L = SSD_CHUNK
    S = x_ref.shape[1]
    nc = S // L
    W = SSD_GROUP_W
    HG = SSD_HG
    R = 2 * HG

    ii = _iota2((L, L), 0)
    jj = _iota2((L, L), 1)
    tril = ii >= jj
    triu = jj >= ii
    tril_b = tril.astype(BF16)
    triu_b = triu.astype(BF16)

    er = _iota2((LANE, 2 * W), 0)
    ec = _iota2((LANE, 2 * W), 1)
    e = (er == jnp.where(ec < W, ec // SSD_HEAD_DIM, (ec - W) // SSD_HEAD_DIM + HG)).astype(BF16)
    e2 = jnp.concatenate([e, e], axis=0)

    bias_row = prow_ref[0:1, :]
    a_row = prow_ref[1:2, :]
    bias_col = pcol_ref[0, :, 0:1]
    a_col = pcol_ref[0, :, 1:2]

    lane_fwd = _iota2((L, LANE), 1) < HG
    row_fwd = _iota2((R, L), 0) < HG
    lane_half = _iota2((L, LANE), 1) < SSD_HEAD_DIM

    def decay_sums(c):
        rs = pl.ds(pl.multiple_of(c * L, L), L)
        dt_c = _softplus(us_ref[0, rs, :] + bias_row)
        a_c = dt_c * a_row
        dt_r = _softplus(ust_ref[:, rs] + bias_col)
        a_r = dt_r * a_col
        return dt_c, a_c, _dot(tril_b, _split3(a_c, 1)), dt_r, a_r, _dot(_split3(a_r, 0), triu_b)

    def decay_store(c, sums):
        rs = pl.ds(pl.multiple_of(c * L, L), L)
        dt_c, a_c, pp, dt_r, a_r, pr = sums
        p_c = pp[:, 0:LANE] + pp[:, LANE:2 * LANE] + pp[:, 2 * LANE:3 * LANE]
        tot_c = p_c[L - 1:L, :]
        cum_c = jnp.where(lane_fwd, p_c, tot_c - p_c + a_c)
        cumc_ref[rs, :] = cum_c * LOG2E
        wdt_ref[rs, :] = jnp.exp(tot_c - cum_c) * dt_c
        dec_ref[c] = jnp.broadcast_to(jnp.exp(tot_c), (16, LANE))
        p_r = pr[0:R] + pr[R:2 * R] + pr[2 * R:3 * R]
        tot_r = p_r[:, L - 1:L]
        cum_r = jnp.where(row_fwd, p_r, tot_r - p_r + a_r)
        cumr_ref[:, rs] = (cum_r - jnp.log(dt_r)) * LOG2E

    def intra_chunk(c):
        rs = pl.ds(pl.multiple_of(c * L, L), L)
        x_c = x_ref[0, rs, :]
        b_c = b_ref[0, rs, :]
        cb = _nt(c_ref[0, rs, :], b_c)
        ex = _dot(_split_hi_lo(jnp.concatenate([wdt_ref[rs, :], dec_ref[c]], axis=0)), e2)
        decx_ref[c] = ex[L:L + 8]
        xf = x_c.astype(F32)
        xw = jnp.concatenate([xf * ex[0:L, 0:W], xf * ex[0:L, W:2 * W]], axis=1).astype(BF16)
        xs_ref[c] = _tn(b_c, xw)
        cum_c = cumc_ref[rs, :]
        cum_r = cumr_ref[:, rs]
        for hp in range(HG // 2):
            ms = []
            for hh in range(2):
                hf = 2 * hp + hh
                hb = HG + 2 * hp + hh
                decf = jnp.exp2(jnp.where(tril, cum_c[:, hf:hf + 1] - cum_r[hf:hf + 1, :], -jnp.inf))
                decb = jnp.exp2(jnp.where(triu, cum_c[:, hb:hb + 1] - cum_r[hb:hb + 1, :], -jnp.inf))
                ms.append((cb * (decf + decb)).astype(BF16))
            m2 = jnp.concatenate(ms, axis=1)
            xp = x_c[:, hp * LANE:(hp + 1) * LANE]
            xz = jnp.zeros_like(xp)
            x2 = jnp.concatenate([jnp.where(lane_half, xp, xz), jnp.where(lane_half, xz, xp)], axis=0)
            acc_ref[rs, hp * LANE:(hp + 1) * LANE] = _dot(m2, x2)

    G = SSD_CHUNKS_PER_STEP
    n_steps = nc // G

    def fused_body(i, carry):
        nxt = jnp.minimum(i + 1, n_steps - 1)
        sums = [decay_sums(G * nxt + j) for j in range(G)]
        for j in range(G):
            intra_chunk(G * i + j)
        for j in range(G):
            decay_store(G * nxt + j, sums[j])
        return carry

    for j in range(G):
        decay_store(j, decay_sums(j))
    lax.fori_loop(0, n_steps, fused_body, 0)

    st_ref[...] = jnp.zeros_like(st_ref)

    def state_body(i, carry):
        for d, c in enumerate((i, nc - 1 - i)):
            ls = slice(d * W, (d + 1) * W)
            s_prev = st_ref[:, ls]
            sst_ref[c, :, ls] = s_prev.astype(BF16)
            st_ref[:, ls] = s_prev * decx_ref[c, 0:1, ls] + xs_ref[c, :, ls]
        return carry

    lax.fori_loop(0, nc, state_body, 0)

    def out_body(c, carry):
        rs = pl.ds(pl.multiple_of(c * L, L), L)
        yo = _dot(c_ref[0, rs, :], sst_ref[c])
        sc = _dot(_split_hi_lo(jnp.exp2(cumc_ref[rs, :])), e2)
        xf = x_ref[0, rs, :].astype(F32)
        y = acc_ref[rs, :] + yo[:, 0:W] * sc[:, 0:W] + yo[:, W:2 * W] * sc[:, W:2 * W] + xf * drow_ref[...]
        y = y * _silu(z_ref[0, rs, :].astype(F32))
        ms = jnp.mean(y * y, axis=-1, keepdims=True)
        o_ref[0, rs, :] = (y * lax.rsqrt(ms + EPS) * nw_ref[...]).astype(BF16)
        return carry

    lax.fori_loop(0, nc, out_body, 0, unroll=4)


def _ssd(xbc, u3, us3, ust, prow, pcol, drow, nw):
    B, S, _ = xbc.shape
    W = SSD_GROUP_W
    N = SSD_STATE
    nc = S // SSD_CHUNK
    return pl.pallas_call(
        _ssd_kernel,
        grid=(B, SSD_GROUPS),
        in_specs=[
            pl.BlockSpec((1, S, W), lambda b, g: (b, 0, g)),
            pl.BlockSpec((1, S, N), lambda b, g: (b, 0, SSD_D_INNER // N + g)),
            pl.BlockSpec((1, S, N), lambda b, g: (b, 0, SSD_D_INNER // N + SSD_GROUPS + g)),
            pl.BlockSpec((1, S, W), lambda b, g: (b, 0, U_Z // W + g)),
            pl.BlockSpec((1, S, LANE), lambda b, g: (b, 0, g)),
            pl.BlockSpec((SMALL_T_ROWS, S), lambda b, g: (g, b)),
            pl.BlockSpec((8, LANE), lambda b, g: (g, 0)),
            pl.BlockSpec((1, SMALL_T_ROWS, 8), lambda b, g: (g, 0, 0)),
            pl.BlockSpec((1, W), lambda b, g: (0, g)),
            pl.BlockSpec((1, W), lambda b, g: (0, g)),
        ],
        out_specs=pl.BlockSpec((1, S, W), lambda b, g: (b, 0, g)),
        out_shape=jax.ShapeDtypeStruct((B, S, SSD_D_INNER), BF16),
        scratch_shapes=[
            pltpu.VMEM((S, W), F32),
            pltpu.VMEM((S, LANE), F32),
            pltpu.VMEM((SMALL_T_ROWS, S), F32),
            pltpu.VMEM((S, LANE), F32),
            pltpu.VMEM((nc, 16, LANE), F32),
            pltpu.VMEM((nc, N, 2 * W), F32),
            pltpu.VMEM((nc, 8, 2 * W), F32),
            pltpu.VMEM((nc, N, 2 * W), BF16),
            pltpu.VMEM((N, 2 * W), F32),
        ],
        compiler_params=_cparams(("parallel", "parallel")),
        name="ssd_scan",
    )(xbc, xbc, xbc, u3, us3, ust, prow, pcol, drow, nw)


def _gla_kernel(q_ref, k_ref, v_ref, gg_ref, us_ref, a2_ref, bias_ref, nw_ref,
                o_ref, acc_ref, g_ref, qd_ref, kd_ref, kdp_ref, qcat_ref, x_ref, dec_ref, sst_ref):
    L = GLA_CHUNK
    BL = 2 * L
    DK = GLA_DK
    S = q_ref.shape[1]
    nb = S // BL
    scale = DK ** -0.5

    ii = _iota2((BL, BL), 0)
    jj = _iota2((BL, BL), 1)
    same = (ii // L) == (jj // L)
    masks = (same & (ii >= jj), same & (jj >= ii))
    tri2 = masks[0].astype(BF16)
    par_row = _iota2((BL, DK), 0) // L

    ga = us_ref[0].astype(BF16)
    g_ref[...] = _log2_sigmoid(_dot(ga, a2_ref[...]) + bias_ref[...]) * (1.0 / GLA_GATE_NORM)

    def decay_sums(i):
        g = g_ref[pl.ds(pl.multiple_of(i * BL, BL), BL), :]
        hi = g.astype(BF16)
        r1 = g - hi.astype(F32)
        mid = r1.astype(BF16)
        lo = (r1 - mid.astype(F32)).astype(BF16)
        return g, _dot(tri2, jnp.concatenate([hi, mid, lo], axis=1))

    def decay_block(i, sums):
        rs = pl.ds(pl.multiple_of(i * BL, BL), BL)
        g, pp = sums
        p = pp[:, 0:2 * DK] + pp[:, 2 * DK:4 * DK] + pp[:, 4 * DK:6 * DK]
        q_c = q_ref[0, rs, :].astype(F32) * scale
        k_c = k_ref[0, rs, :].astype(F32)
        zero = jnp.zeros((BL, DK), BF16)
        for d in range(2):
            p_d = p[:, d * DK:(d + 1) * DK]
            tot = jnp.where(par_row == 0, p_d[L - 1:L, :], p_d[BL - 1:BL, :])
            b = p_d if d == 0 else tot - p_d + g[:, DK:]
            qd = (q_c * jnp.exp2(b)).astype(BF16)
            kdec = (k_c * jnp.exp2(tot - b)).astype(BF16)
            qd_ref[d, rs, :] = qd
            kd_ref[d, rs, :] = (k_c * jnp.exp2(-b)).astype(BF16)
            for par in range(2):
                sel = par_row == par
                kdp_ref[d, rs, par * DK:(par + 1) * DK] = jnp.where(sel, kdec, zero)
                qcat_ref[rs, (2 * d + par) * DK:(2 * d + par + 1) * DK] = jnp.where(sel, qd, zero)
                last = (par + 1) * L - 1
                dec_ref[d, 2 * i + par] = jnp.broadcast_to(jnp.exp2(p_d[last:last + 1, :]), (8, DK))

    G = GLA_BLOCKS_PER_STEP
    n_groups = nb // G

    def intra_group(i):
        blks = [G * i + j for j in range(G)]
        rss = [pl.ds(pl.multiple_of(b * BL, BL), BL) for b in blks]
        vs = [v_ref[0, rs, :] for rs in rss]
        atts = [[_nt(qd_ref[d, rs, :], kd_ref[d, rs, :]) for d in range(2)] for rs in rss]
        for j, b in enumerate(blks):
            for d in range(2):
                x_ref[d, b] = _tn(vs[j], kdp_ref[d, rss[j], :])
        for j in range(G):
            att = jnp.where(masks[0], atts[j][0], 0.0) + jnp.where(masks[1], atts[j][1], 0.0)
            acc_ref[rss[j], :] = _dot(att.astype(BF16), vs[j])

    def fused_body(i, carry):
        nxt = jnp.minimum(i + 1, n_groups - 1)
        sums = [decay_sums(G * nxt + j) for j in range(G)]
        intra_group(i)
        for j in range(G):
            decay_block(G * nxt + j, sums[j])
        return carry

    for j in range(G):
        decay_block(j, decay_sums(j))
    lax.fori_loop(0, n_groups, fused_body, 0)

    def state_body(d):
        def body(i, s):
            b = i if d == 0 else nb - 1 - i
            for par in ((0, 1) if d == 0 else (1, 0)):
                lane0 = (2 * d + par) * DK
                sst_ref[b, :, lane0:lane0 + DK] = s.astype(BF16)
                s = s * dec_ref[d, 2 * b + par, 0:1, :] + x_ref[d, b, :, par * DK:(par + 1) * DK]
            return s
        return body

    for d in range(2):
        lax.fori_loop(0, nb, state_body(d), jnp.zeros((GLA_DV, DK), F32))

    def out_body(b, carry):
        rs = pl.ds(pl.multiple_of(b * BL, BL), BL)
        o = acc_ref[rs, :] + _nt(qcat_ref[rs, :], sst_ref[b])
        ms = jnp.mean(o * o, axis=-1, keepdims=True)
        o = o * lax.rsqrt(ms + EPS) * nw_ref[...]
        o_ref[0, rs, :] = (o * _silu(gg_ref[0, rs, :].astype(F32))).astype(BF16)
        return carry

    lax.fori_loop(0, nb, out_body, 0, unroll=GLA_OUT_UNROLL)


def _gla(u3, us3, a2, bias, nw):
    B, S, _ = u3.shape
    DK, DV = GLA_DK, GLA_DV
    nc = S // GLA_CHUNK
    nb = nc // 2
    return pl.pallas_call(
        _gla_kernel,
        grid=(B, GLA_HEADS),
        in_specs=[
            pl.BlockSpec((1, S, DK), lambda b, h: (b, 0, U_GQ // DK + h)),
            pl.BlockSpec((1, S, DK), lambda b, h: (b, 0, U_GK // DK + h)),
            pl.BlockSpec((1, S, DV), lambda b, h: (b, 0, U_GV // DV + h)),
            pl.BlockSpec((1, S, DV), lambda b, h: (b, 0, U_GG // DV + h)),
            pl.BlockSpec((1, S, LANE), lambda b, h: (b, 0, 0)),
            pl.BlockSpec((LANE, 2 * DK), lambda b, h: (0, h)),
            pl.BlockSpec((1, 2 * DK), lambda b, h: (0, h)),
            pl.BlockSpec((1, DV), lambda b, h: (0, 0)),
        ],
        out_specs=pl.BlockSpec((1, S, DV), lambda b, h: (b, 0, h)),
        out_shape=jax.ShapeDtypeStruct((B, S, GLA_VAL_W), BF16),
        scratch_shapes=[
            pltpu.VMEM((S, DV), F32),
            pltpu.VMEM((S, 2 * DK), F32),
            pltpu.VMEM((2, S, DK), BF16),
            pltpu.VMEM((2, S, DK), BF16),
            pltpu.VMEM((2, S, 2 * DK), BF16),
            pltpu.VMEM((S, 4 * DK), BF16),
            pltpu.VMEM((2, nb, DV, 2 * DK), F32),
            pltpu.VMEM((2, nc, 8, DK), F32),
            pltpu.VMEM((nb, DV, 4 * DK), BF16),
        ],
        compiler_params=_cparams(("parallel", "parallel")),
        name="gla_scan",
    )(u3, u3, u3, u3, us3, a2, bias, nw)


def _na_bias_rows(rpb):
    H, R, C = rpb.shape
    n_pos = GRID_W - NA_WIN_W
    n_neg = GRID_W - NA_WIN_W + 2
    ext = jnp.concatenate([rpb[:, :, NA_WIN_W - 1:], jnp.repeat(rpb[:, :, C - 1:], n_pos, axis=2),
                           jnp.repeat(rpb[:, :, 0:1], n_neg, axis=2), rpb[:, :, 1:NA_WIN_W - 1]], axis=2)
    return jnp.pad(ext, ((0, 0), (0, 1), (0, 0))).reshape(H // 2, 2, R + 1, 2 * GRID_W)


def _na_kernel(flag_ref, q_ref, k_ref, v_ref, ext_ref, o_ref, vx_ref, tab_ref):
    NB, S, _ = q_ref.shape
    rows = S // GRID_W
    total = NB * rows
    win_h = NA_WIN_H
    nk = win_h * GRID_W

    vx_ref[:, :, 0:LANE] = v_ref[...]
    vx_ref[:, :, LANE:2 * LANE] = jnp.ones((NB, S, LANE), BF16)
    bound_ok = flag_ref[0] != 0

    first_q = _iota2((GRID_W, LANE), 1) < NA_HEAD_DIM

    @pl.when(pl.program_id(1) == 0)
    def _():
        q_col = _iota2((GRID_W, 2 * GRID_W), 0)
        k_col = _iota2((GRID_W, 2 * GRID_W), 1) % GRID_W
        w_start = jnp.clip(q_col - NA_WIN_W // 2, 0, GRID_W - NA_WIN_W)
        in_window = (k_col >= w_start) & (k_col < w_start + NA_WIN_W)
        low = _iota2((GRID_W, 2 * GRID_W), 1) < GRID_W
        for hd in range(2):
            def skewed(rr, shift):
                row = jnp.broadcast_to(ext_ref[0, hd, rr:rr + 1, :], (GRID_W, 2 * GRID_W))
                return pltpu.roll(row, shift, axis=1, stride=1, stride_axis=0)

            for e in range(2 * NA_WIN_H - 2):
                t = jnp.where(low, skewed(e, 0), skewed(e + 1, GRID_W))
                tab_ref[e, hd * GRID_W:(hd + 1) * GRID_W, :] = jnp.where(in_window, t, -jnp.inf)

    def locate(r):
        bi = r // rows
        rl = r - bi * rows
        r0 = jnp.clip(rl - win_h // 2, 0, rows - win_h)
        return bi, pl.multiple_of(rl * GRID_W, GRID_W), rl - r0, pl.multiple_of(r0 * GRID_W, GRID_W)

    def scores(r):
        bi, q0, delta, k0 = locate(r)
        q = q_ref[bi, pl.ds(q0, GRID_W), :]
        zero = jnp.zeros_like(q)
        qs = jnp.concatenate([jnp.where(first_q, q, zero), jnp.where(first_q, zero, q)], axis=0)
        rr0 = (win_h - 1) - delta
        bias = jnp.concatenate([tab_ref[rr0 + w] for w in range(0, win_h, 2)], axis=1)
        return _nt(qs, k_ref[bi, pl.ds(k0, nk), :]) + bias

    def probs_exact(s):
        return jnp.exp2(s - jnp.max(s, axis=-1, keepdims=True)).astype(BF16)

    def probs_bounded(r):
        return jnp.exp2(scores(r)).astype(BF16)

    def attend(r, p):
        bi, q0, _, k0 = locate(r)
        ox = _dot(p, vx_ref[bi, pl.ds(k0, nk), :])
        o = ox[:, 0:LANE] / ox[:, LANE:2 * LANE]
        o_ref[bi, pl.ds(q0, GRID_W), :] = jnp.where(first_q, o[:GRID_W], o[GRID_W:]).astype(BF16)

    @pl.when(bound_ok)
    def _():
        U = NA_ROWS_PER_STEP

        def row_body(i, p_prev):
            r = i * U
            for j in range(U):
                attend(r - U + j, p_prev[j])
            return tuple(probs_bounded(r + j) for j in range(U))

        p_last = lax.fori_loop(1, total // U, row_body, tuple(probs_bounded(j) for j in range(U)))
        for j in range(U):
            attend(total - U + j, p_last[j])

    @pl.when(jnp.logical_not(bound_ok))
    def _():
        U = NA_ROWS_PER_STEP_EXACT

        def row_body(i, carry):
            s_cur, p_prev = carry
            r = i * U
            for j in range(U):
                attend(r - U + j, p_prev[j])
            p = tuple(probs_exact(s) for s in s_cur)
            s_next = tuple(scores(jnp.minimum(r + U + j, total - 1)) for j in range(U))
            return s_next, p

        p0 = tuple(probs_exact(scores(j)) for j in range(U))
        s1 = tuple(scores(U + j) for j in range(U))
        _, p_last = lax.fori_loop(1, total // U, row_body, (s1, p0))
        for j in range(U):
            attend(total - U + j, p_last[j])


def _na_score_bound(rpb, q_norm_w, k_norm_w):
    H = rpb.shape[0]
    qk = (NA_HEAD_DIM ** 0.5 * LOG2E * (1.0 + NA_BOUND_SLACK)) * jnp.max(jnp.abs(q_norm_w)) * jnp.max(jnp.abs(k_norm_w))
    b_max = jnp.max(rpb.reshape(H, -1), axis=1) * LOG2E
    b_self = rpb[:, NA_WIN_H - 1, NA_WIN_W - 1] * LOG2E
    bound = qk + b_max
    flag = jnp.all(bound - (b_self - qk) <= NA_MAX_BOUND_GAP)
    return bound, flag.astype(jnp.int32).reshape(1)


def _na(u3, bias_rows, flag):
    B, S, _ = u3.shape
    assert S // GRID_W >= NA_WIN_H and NA_WIN_H % 2 == 0 and 2 * GRID_W == LANE
    nb = math.gcd(B, NA_BATCH_PER_STEP)
    assert (nb * (S // GRID_W)) % NA_ROWS_PER_STEP == 0
    return pl.pallas_call(
        _na_kernel,
        grid=(NA_HEADS // 2, B // nb),
        in_specs=[
            pl.BlockSpec(memory_space=pltpu.SMEM),
            pl.BlockSpec((nb, S, LANE), lambda h, b: (b, 0, U_NQ // LANE + h)),
            pl.BlockSpec((nb, S, LANE), lambda h, b: (b, 0, U_NK // LANE + h)),
            pl.BlockSpec((nb, S, LANE), lambda h, b: (b, 0, U_NV // LANE + h)),
            pl.BlockSpec((1, 2, 2 * NA_WIN_H, 2 * GRID_W), lambda h, b: (h, 0, 0, 0)),
        ],
        out_specs=pl.BlockSpec((nb, S, LANE), lambda h, b: (b, 0, h)),
        out_shape=jax.ShapeDtypeStruct((B, S, NA_W), BF16),
        scratch_shapes=[
            pltpu.VMEM((nb, S, 2 * LANE), BF16),
            pltpu.VMEM((2 * NA_WIN_H - 2, 2 * GRID_W, 2 * GRID_W), F32),
        ],
        compiler_params=_cparams(("parallel", "arbitrary")),
        name="na_attn",
    )(flag, u3, u3, u3, bias_rows)


def _merge_kernel(x_ref, ys_ref, yg_ref, yn_ref, gate_ref, ws_ref, wg_ref, wn_ref, wo_ref, o_ref):
    D = D_MODEL
    mixed = _sigmoid(gate_ref[:, 0:D].astype(F32)) * _dot(ys_ref[...], ws_ref[...])
    mixed += _sigmoid(gate_ref[:, D:2 * D].astype(F32)) * _dot(yg_ref[...], wg_ref[...])
    mixed += _sigmoid(gate_ref[:, 2 * D:3 * D].astype(F32)) * _dot(yn_ref[...], wn_ref[...])
    o_ref[...] = x_ref[...] + _dot(mixed.astype(BF16), wo_ref[...])


def _merge(x2, ys, yg, yn, u2, ws, wg, wn, wo, tm=512):
    T = x2.shape[0]
    D = D_MODEL
    row = lambda i: (i, 0)
    fixed = lambda i: (0, 0)
    return pl.pallas_call(
        _merge_kernel,
        grid=(T // tm,),
        in_specs=[
            pl.BlockSpec((tm, D), row),
            pl.BlockSpec((tm, D), row),
            pl.BlockSpec((tm, D), row),
            pl.BlockSpec((tm, D), row),
            pl.BlockSpec((tm, N_BRANCH * D), lambda i: (i, U_GATE // (N_BRANCH * D))),
            pl.BlockSpec((D, D), fixed),
            pl.BlockSpec((D, D), fixed),
            pl.BlockSpec((D, D), fixed),
            pl.BlockSpec((D, D), fixed),
        ],
        out_specs=pl.BlockSpec((tm, D), row),
        out_shape=jax.ShapeDtypeStruct((T, D), F32),
        compiler_params=_cparams(("parallel",)),
        name="merge",
    )(x2, ys, yg, yn, u2, ws, wg, wn, wo)


def _mlp_kernel(x_ref, nw_ref, w1_ref, w2_ref, o_ref, *, tf):
    x = x_ref[...]
    ms = jnp.mean(x * x, axis=-1, keepdims=True)
    h = (x * lax.rsqrt(ms + EPS) * nw_ref[...]).astype(BF16)
    acc = x
    for f in range(D_FF // tf):
        a = jnp.maximum(_dot(h, w1_ref[:, f * tf:(f + 1) * tf]), 0.0)
        acc = acc + _dot((a * a).astype(BF16), w2_ref[f * tf:(f + 1) * tf, :])
    o_ref[...] = acc


def _mlp(x2, nw, w1, w2, tm=512, tf=1024):
    T = x2.shape[0]
    D = D_MODEL
    return pl.pallas_call(
        functools.partial(_mlp_kernel, tf=tf),
        grid=(T // tm,),
        in_specs=[
            pl.BlockSpec((tm, D), lambda i: (i, 0)),
            pl.BlockSpec((1, D), lambda i: (0, 0)),
            pl.BlockSpec((D, D_FF), lambda i: (0, 0)),
            pl.BlockSpec((D_FF, D), lambda i: (0, 0)),
        ],
        out_specs=pl.BlockSpec((tm, D), lambda i: (i, 0)),
        out_shape=jax.ShapeDtypeStruct((T, D), F32),
        compiler_params=_cparams(("parallel",)),
        name="mlp",
    )(x2, nw, w1, w2)


def _pad_rows(w, start, total):
    return jnp.zeros((total, w.shape[1]), w.dtype).at[start:start + w.shape[0]].set(w)


def kernel(x, norm_mix_w, w_in, ssd_conv_w, ssd_conv_b, ssd_dt_bias_f, ssd_dt_bias_b, ssd_a_log_f,
           ssd_a_log_b, ssd_d, ssd_norm_w, gla_a2_f, gla_a2_bias_f, gla_a2_b, gla_a2_bias_b,
           gla_norm_w, na_q_norm_w, na_k_norm_w, na_rpb, w_branch_ssd, w_branch_gla, w_branch_na,
           w_out, norm_mlp_w, w_ff1, w_ff2):
    B, S, D = x.shape
    T = B * S
    depth = w_in.shape[0]
    rows = S // GRID_W
    x2 = x.reshape(T, D)
    w_in_t = jnp.swapaxes(w_in, 1, 2)
    for l in range(depth):
        w_big, w_small = _permute_weight(w_in_t, l)
        hg = (SSD_GROUPS, SSD_HG)
        zeros_r = jnp.zeros((SSD_GROUPS, LANE - 2 * SSD_HG), F32)
        dt_bias = jnp.concatenate([ssd_dt_bias_f[l].reshape(hg), ssd_dt_bias_b[l].reshape(hg), zeros_r], axis=1)
        a_neg = jnp.concatenate([-jnp.exp(ssd_a_log_f[l]).reshape(hg), -jnp.exp(ssd_a_log_b[l]).reshape(hg),
                                 zeros_r], axis=1)
        prow3 = jnp.concatenate([dt_bias[:, None], a_neg[:, None], jnp.zeros((SSD_GROUPS, 6, LANE), F32)], axis=1)
        prow = prow3.reshape(SSD_GROUPS * 8, LANE)
        pcol = jnp.transpose(prow3[:, :, :SMALL_T_ROWS], (0, 2, 1))
        drow = jnp.repeat(ssd_d[l], SSD_HEAD_DIM)[None, :]
        hk = (GLA_HEADS, 1, GLA_DK)
        a2 = jnp.concatenate([_pad_rows(gla_a2_f[l], SM_GAF, LANE).reshape((LANE,) + hk),
                              _pad_rows(gla_a2_b[l], SM_GAB, LANE).reshape((LANE,) + hk)],
                             axis=2).reshape(LANE, -1).astype(BF16)
        a2_bias = jnp.concatenate([gla_a2_bias_f[l].reshape(hk), gla_a2_bias_b[l].reshape(hk)],
                                  axis=1).reshape(1, -1)
        na_bound, na_flag = _na_score_bound(na_rpb[l], na_q_norm_w[l], na_k_norm_w[l])
        table = _na_bias_rows(na_rpb[l] * LOG2E - na_bound[:, None, None])
        q_row = jnp.tile(na_q_norm_w[l] * (NA_HEAD_DIM ** -0.5 * LOG2E), NA_HEADS)
        k_row = jnp.tile(na_k_norm_w[l], NA_HEADS)
        qkw = jnp.concatenate([q_row[None], k_row[None], jnp.zeros((6, NA_W), F32)], axis=0)

        u2, us2, ust = _inproj(x2, norm_mix_w[l][None, :], w_big, w_small, qkw)
        u3 = u2.reshape(B, S, U_WIDTH)
        us3 = us2.reshape(B, S, SMALL_W)
        xbc = _conv(u3, ssd_conv_w[l], ssd_conv_b[l][None, :])
        y_ssd = _ssd(xbc, u3, us3, ust, prow, pcol, drow, ssd_norm_w[l][None, :])
        y_gla = _gla(u3, us3, a2, a2_bias, gla_norm_w[l][None, :])
        y_na = _na(u3, table, na_flag)
        x2 = _merge(x2, y_ssd.reshape(T, -1), y_gla.reshape(T, -1), y_na.reshape(T, -1), u2,
                    w_branch_ssd[l].astype(BF16), w_branch_gla[l].astype(BF16),
                    w_branch_na[l].astype(BF16), w_out[l].astype(BF16))
        x2 = _mlp(x2, norm_mlp_w[l][None, :], w_ff1[l].astype(BF16), w_ff2[l].astype(BF16))
    return x2.reshape(B, S, D)
```

```python
import functools
import math

import jax
import jax.numpy as jnp
import numpy as np
from jax import lax
from jax.experimental import pallas as pl
from jax.experimental.pallas import tpu as pltpu

F32 = jnp.float32
BF16 = jnp.bfloat16

EPS = 1e-6
D_MODEL = 1024
GRID_W = 64

SSD_HEADS = 16
SSD_HEAD_DIM = 64
SSD_D_INNER = 1024
SSD_GROUPS = 2
SSD_STATE = 128
SSD_CONV = 5
SSD_CONV_DIM = 1536
SSD_CHUNK = 128
SSD_CHUNKS_PER_STEP = 4
SSD_GROUP_W = SSD_D_INNER // SSD_GROUPS
SSD_HG = SSD_HEADS // SSD_GROUPS

GLA_HEADS = 4
GLA_DK = 128
GLA_DV = 256
GLA_KEY_W = 512
GLA_VAL_W = 1024
GLA_GATE_RANK = 16
GLA_GATE_NORM = 16.0
GLA_CHUNK = 64
GLA_BLOCKS_PER_STEP = 4
GLA_OUT_UNROLL = 8

NA_HEADS = 16
NA_HEAD_DIM = 64
NA_W = 1024
NA_WIN_H = 8
NA_WIN_W = 16
NA_BATCH_PER_STEP = 4
NA_ROWS_PER_STEP = 16
NA_ROWS_PER_STEP_EXACT = 2
LOG2E = 1.4426950408889634
NA_BOUND_SLACK = 0.02
NA_MAX_BOUND_GAP = 90.0

N_BRANCH = 3
D_FF = 4096

IN_SIZES = (SSD_D_INNER, SSD_CONV_DIM, SSD_HEADS, SSD_HEADS,
            GLA_KEY_W, GLA_KEY_W, GLA_VAL_W, GLA_VAL_W, GLA_GATE_RANK, GLA_GATE_RANK,
            NA_W, NA_W, NA_W, N_BRANCH * D_MODEL)
_IN_OFF = np.concatenate([[0], np.cumsum(IN_SIZES)])
(_O_Z, _O_XBC, _O_DTF, _O_DTB, _O_GQ, _O_GK, _O_GV, _O_GG, _O_GAF, _O_GAB,
 _O_NQ, _O_NK, _O_NV, _O_GATE) = [int(v) for v in _IN_OFF[:-1]]

U_NQ = 0
U_NK = 1024
U_NV = 2048
U_GATE = 3072
U_Z = 6144
U_XBC = 7168
U_GQ = 8704
U_GK = 9216
U_GV = 9728
U_GG = 10752
U_WIDTH = 11776
INPROJ_COL_CHUNK = 1024
LANE = 128
VMEM_LIMIT = 56 * 1024 * 1024

SMALL_W = SSD_GROUPS * LANE
SM_DTF, SM_DTB, SM_GAF, SM_GAB = 0, 8, 16, 32
SMALL_T_ROWS = 2 * SSD_HG

_BIG_SEGS = ((_O_NQ, 6144), (_O_Z, 2560), (_O_GQ, 3072))


def _cparams(sem, vmem=VMEM_LIMIT):
    return pltpu.CompilerParams(dimension_semantics=sem, vmem_limit_bytes=vmem)


def _sigmoid(x):
    return 1.0 / (1.0 + jnp.exp(-x))


def _silu(x):
    return x * _sigmoid(x)


def _softplus(x):
    return jnp.maximum(x, 0.0) + jnp.log1p(jnp.exp(-jnp.abs(x)))


def _log2_sigmoid(x):
    t = x * (-LOG2E)
    return -(jnp.maximum(t, 0.0) + jnp.log2(1.0 + jnp.exp2(-jnp.abs(t))))


def _nt(a, b):
    return lax.dot_general(a, b, (((1,), (1,)), ((), ())), preferred_element_type=F32)


def _tn(a, b):
    return lax.dot_general(a, b, (((0,), (0,)), ((), ())), preferred_element_type=F32)


def _dot(a, b):
    return jnp.dot(a, b, preferred_element_type=F32)


def _iota2(shape, dim):
    return lax.broadcasted_iota(jnp.int32, shape, dim)


def _wperm_kernel(wt_ref, o_ref, ws_ref):
    off = 0
    for a, n in _BIG_SEGS:
        o_ref[off:off + n, :] = wt_ref[0, a:a + n, :].astype(BF16)
        off += n
    ws_ref[...] = jnp.zeros_like(ws_ref)
    for g in range(SSD_GROUPS):
        base, h0 = g * LANE, g * SSD_HG
        ws_ref[base + SM_DTF:base + SM_DTF + SSD_HG, :] = wt_ref[0, _O_DTF + h0:_O_DTF + h0 + SSD_HG, :]
        ws_ref[base + SM_DTB:base + SM_DTB + SSD_HG, :] = wt_ref[0, _O_DTB + h0:_O_DTB + h0 + SSD_HG, :]
    ws_ref[SM_GAF:SM_GAF + GLA_GATE_RANK, :] = wt_ref[0, _O_GAF:_O_GAF + GLA_GATE_RANK, :]
    ws_ref[SM_GAB:SM_GAB + GLA_GATE_RANK, :] = wt_ref[0, _O_GAB:_O_GAB + GLA_GATE_RANK, :]


def _permute_weight(w_t_all, layer, tc=256):
    _, n_in, d = w_t_all.shape
    return pl.pallas_call(
        _wperm_kernel,
        grid=(d // tc,),
        in_specs=[pl.BlockSpec((1, n_in, tc), lambda i: (layer, 0, i))],
        out_specs=[
            pl.BlockSpec((U_WIDTH, tc), lambda i: (0, i)),
            pl.BlockSpec((SMALL_W, tc), lambda i: (0, i)),
        ],
        out_shape=[
            jax.ShapeDtypeStruct((U_WIDTH, d), BF16),
            jax.ShapeDtypeStruct((SMALL_W, d), F32),
        ],
        compiler_params=_cparams(("parallel",)),
        name="w_permute",
    )(w_t_all)


def _head_rms(r, w_row):
    G = 2 * LANE
    er = _iota2((G, G), 0) // NA_HEAD_DIM
    ec = _iota2((G, G), 1) // NA_HEAD_DIM
    e_blk = (er == ec).astype(BF16)
    outs = []
    for a in range(0, r.shape[1], G):
        x = r[:, a:a + G]
        ms = _dot((x * x).astype(BF16), e_blk) * (1.0 / NA_HEAD_DIM)
        outs.append(x * lax.rsqrt(ms + EPS))
    return jnp.concatenate(outs, axis=1) * w_row


def _inproj_kernel(x_ref, nw_ref, w_ref, ws_ref, qkw_ref, u_ref, us_ref, ust_ref, h_ref):
    tn = u_ref.shape[1]
    chunks = [(a, min(a + INPROJ_COL_CHUNK, tn)) for a in range(0, tn, INPROJ_COL_CHUNK)]

    @pl.when(pl.program_id(1) == 0)
    def _():
        x = x_ref[...]
        ms = jnp.mean(x * x, axis=-1, keepdims=True)
        h = (x * lax.rsqrt(ms + EPS) * nw_ref[...]).astype(BF16)
        h_ref[...] = h
        ws = ws_ref[...].astype(BF16)
        us_ref[...] = _nt(h, ws)
        dt_rows = jnp.concatenate([ws[g * LANE:g * LANE + SMALL_T_ROWS] for g in range(SSD_GROUPS)], axis=0)
        ust_ref[...] = _nt(dt_rows, h)
        for c, (a, b) in enumerate(chunks):
            r = _nt(h, w_ref[a:b, :])
            if c < 2:
                r = _head_rms(r, qkw_ref[c:c + 1, :])
            u_ref[:, a:b] = r.astype(BF16)

    @pl.when(pl.program_id(1) != 0)
    def _():
        for a, b in chunks:
            u_ref[:, a:b] = _nt(h_ref[...], w_ref[a:b, :]).astype(BF16)


def _inproj(x2, nw, w_big, w_small, qkw, tm=1024, tn=U_WIDTH // 4):
    assert (U_NQ, U_NK) == (0, INPROJ_COL_CHUNK) and NA_W == INPROJ_COL_CHUNK
    T = x2.shape[0]
    return pl.pallas_call(
        _inproj_kernel,
        grid=(T // tm, U_WIDTH // tn),
        in_specs=[
            pl.BlockSpec((tm, D_MODEL), lambda i, j: (i, 0)),
            pl.BlockSpec((1, D_MODEL), lambda i, j: (0, 0)),
            pl.BlockSpec((tn, D_MODEL), lambda i, j: (j, 0)),
            pl.BlockSpec((SMALL_W, D_MODEL), lambda i, j: (0, 0)),
            pl.BlockSpec((8, NA_W), lambda i, j: (0, 0)),
        ],
        out_specs=[
            pl.BlockSpec((tm, tn), lambda i, j: (i, j)),
            pl.BlockSpec((tm, SMALL_W), lambda i, j: (i, 0)),
            pl.BlockSpec((SSD_GROUPS * SMALL_T_ROWS, tm), lambda i, j: (0, i)),
        ],
        out_shape=[
            jax.ShapeDtypeStruct((T, U_WIDTH), BF16),
            jax.ShapeDtypeStruct((T, SMALL_W), F32),
            jax.ShapeDtypeStruct((SSD_GROUPS * SMALL_T_ROWS, T), F32),
        ],
        scratch_shapes=[pltpu.VMEM((tm, D_MODEL), BF16)],
        compiler_params=_cparams(("parallel", "arbitrary")),
        name="inproj",
    )(x2, nw, w_big, w_small, qkw)


def _conv_kernel(u_ref, w_ref, b_ref, o_ref):
    x = u_ref[0].astype(F32)
    S = x.shape[0]
    row = _iota2(x.shape, 0)
    acc = jnp.zeros_like(x) + b_ref[...]
    pad = SSD_CONV // 2
    for k in range(SSD_CONV):
        d = k - pad
        if d == 0:
            xs = x
        else:
            xs = pltpu.roll(x, (-d) % S, axis=0)
            valid = (row + d >= 0) & (row + d < S)
            xs = jnp.where(valid, xs, 0.0)
        acc = acc + w_ref[k:k + 1, :] * xs
    o_ref[0] = _silu(acc).astype(BF16)


def _conv(u3, conv_w, conv_b, tc=256):
    B, S, _ = u3.shape
    nblk = SSD_CONV_DIM // tc
    off = U_XBC // tc
    return pl.pallas_call(
        _conv_kernel,
        grid=(B, nblk),
        in_specs=[
            pl.BlockSpec((1, S, tc), lambda b, c: (b, 0, off + c)),
            pl.BlockSpec((SSD_CONV, tc), lambda b, c: (0, c)),
            pl.BlockSpec((1, tc), lambda b, c: (0, c)),
        ],
        out_specs=pl.BlockSpec((1, S, tc), lambda b, c: (b, 0, c)),
        out_shape=jax.ShapeDtypeStruct((B, S, SSD_CONV_DIM), BF16),
        compiler_params=_cparams(("parallel", "parallel")),
        name="ssd_conv",
    )(u3, conv_w, conv_b)


def _split_hi_lo(v):
    hi = v.astype(BF16)
    lo = (v - hi.astype(F32)).astype(BF16)
    return jnp.concatenate([hi, lo], axis=1)


def _split3(v, axis):
    hi = v.astype(BF16)
    r1 = v - hi.astype(F32)
    mid = r1.astype(BF16)
    lo = (r1 - mid.astype(F32)).astype(BF16)
    return jnp.concatenate([hi, mid, lo], axis=axis)


def _ssd_kernel(x_ref, b_ref, c_ref, z_ref, us_ref, ust_ref, prow_ref, pcol_ref, drow_ref, nw_ref,
                o_ref, acc_ref, cumc_ref, cumr_ref, wdt_ref, dec_ref, xs_ref, decx_ref, sst_ref, st_ref):
    L = SSD_CHUNK
    S = x_ref.shape[1]
    nc = S // L
    W = SSD_GROUP_W
    HG = SSD_HG
    R = 2 * HG

    ii = _iota2((L, L), 0)
    jj = _iota2((L, L), 1)
    tril = ii >= jj
    triu = jj >= ii
    tril_b = tril.astype(BF16)
    triu_b = triu.astype(BF16)

    er = _iota2((LANE, 2 * W), 0)
    ec = _iota2((LANE, 2 * W), 1)
    e = (er == jnp.where(ec < W, ec // SSD_HEAD_DIM, (ec - W) // SSD_HEAD_DIM + HG)).astype(BF16)
    e2 = jnp.concatenate([e, e], axis=0)

    bias_row = prow_ref[0:1, :]
    a_row = prow_ref[1:2, :]
    bias_col = pcol_ref[0, :, 0:1]
    a_col = pcol_ref[0, :, 1:2]

    lane_fwd = _iota2((L, LANE), 1) < HG
    row_fwd = _iota2((R, L), 0) < HG
    lane_half = _iota2((L, LANE), 1) < SSD_HEAD_DIM

    def decay_sums(c):
        rs = pl.ds(pl.multiple_of(c * L, L), L)
        dt_c = _softplus(us_ref[0, rs, :] + bias_row)
        a_c = dt_c * a_row
        dt_r = _softplus(ust_ref[:, rs] + bias_col)
        a_r = dt_r * a_col
        return dt_c, a_c, _dot(tril_b, _split3(a_c, 1)), dt_r, a_r, _dot(_split3(a_r, 0), triu_b)

    def decay_store(c, sums):
        rs = pl.ds(pl.multiple_of(c * L, L), L)
        dt_c, a_c, pp, dt_r, a_r, pr = sums
        p_c = pp[:, 0:LANE] + pp[:, LANE:2 * LANE] + pp[:, 2 * LANE:3 * LANE]
        tot_c = p_c[L - 1:L, :]
        cum_c = jnp.where(lane_fwd, p_c, tot_c - p_c + a_c)
        cumc_ref[rs, :] = cum_c * LOG2E
        wdt_ref[rs, :] = jnp.exp(tot_c - cum_c) * dt_c
        dec_ref[c] = jnp.broadcast_to(jnp.exp(tot_c), (16, LANE))
        p_r = pr[0:R] + pr[R:2 * R] + pr[2 * R:3 * R]
        tot_r = p_r[:, L - 1:L]
        cum_r = jnp.where(row_fwd, p_r, tot_r - p_r + a_r)
        cumr_ref[:, rs] = (cum_r - jnp.log(dt_r)) * LOG2E

    def intra_chunk(c):
        rs = pl.ds(pl.multiple_of(c * L, L), L)
        x_c = x_ref[0, rs, :]
        b_c = b_ref[0, rs, :]
        cb = _nt(c_ref[0, rs, :], b_c)
        ex = _dot(_split_hi_lo(jnp.concatenate([wdt_ref[rs, :], dec_ref[c]], axis=0)), e2)
        decx_ref[c] = ex[L:L + 8]
        xf = x_c.astype(F32)
        xw = jnp.concatenate([xf * ex[0:L, 0:W], xf * ex[0:L, W:2 * W]], axis=1).astype(BF16)
        xs_ref[c] = _tn(b_c, xw)
        cum_c = cumc_ref[rs, :]
        cum_r = cumr_ref[:, rs]
        for hp in range(HG // 2):
            ms = []
            for hh in range(2):
                hf = 2 * hp + hh
                hb = HG + 2 * hp + hh
                decf = jnp.exp2(jnp.where(tril, cum_c[:, hf:hf + 1] - cum_r[hf:hf + 1, :], -jnp.inf))
                decb = jnp.exp2(jnp.where(triu, cum_c[:, hb:hb + 1] - cum_r[hb:hb + 1, :], -jnp.inf))
                ms.append((cb * (decf + decb)).astype(BF16))
            m2 = jnp.concatenate(ms, axis=1)
            xp = x_c[:, hp * LANE:(hp + 1) * LANE]
            xz = jnp.zeros_like(xp)
            x2 = jnp.concatenate([jnp.where(lane_half, xp, xz), jnp.where(lane_half, xz, xp)], axis=0)
            acc_ref[rs, hp * LANE:(hp + 1) * LANE] = _dot(m2, x2)

    G = SSD_CHUNKS_PER_STEP
    n_steps = nc // G

    def fused_body(i, carry):
        nxt = jnp.minimum(i + 1, n_steps - 1)
        sums = [decay_sums(G * nxt + j) for j in range(G)]
        for j in range(G):
            intra_chunk(G * i + j)
        for j in range(G):
            decay_store(G * nxt + j, sums[j])
        return carry

    for j in range(G):
        decay_store(j, decay_sums(j))
    lax.fori_loop(0, n_steps, fused_body, 0)

    st_ref[...] = jnp.zeros_like(st_ref)

    def state_body(i, carry):
        for d, c in enumerate((i, nc - 1 - i)):
            ls = slice(d * W, (d + 1) * W)
            s_prev = st_ref[:, ls]
            sst_ref[c, :, ls] = s_prev.astype(BF16)
            st_ref[:, ls] = s_prev * decx_ref[c, 0:1, ls] + xs_ref[c, :, ls]
        return carry

    lax.fori_loop(0, nc, state_body, 0)

    def out_body(c, carry):
        rs = pl.ds(pl.multiple_of(c * L, L), L)
        yo = _dot(c_ref[0, rs, :], sst_ref[c])
        sc = _dot(_split_hi_lo(jnp.exp2(cumc_ref[rs, :])), e2)
        xf = x_ref[0, rs, :].astype(F32)
        y = acc_ref[rs, :] + yo[:, 0:W] * sc[:, 0:W] + yo[:, W:2 * W] * sc[:, W:2 * W] + xf * drow_ref[...]
        y = y * _silu(z_ref[0, rs, :].astype(F32))
        ms = jnp.mean(y * y, axis=-1, keepdims=True)
        o_ref[0, rs, :] = (y * lax.rsqrt(ms + EPS) * nw_ref[...]).astype(BF16)
        return carry

    lax.fori_loop(0, nc, out_body, 0, unroll=4)


def _ssd(xbc, u3, us3, ust, prow, pcol, drow, nw):
    B, S, _ = xbc.shape
    W = SSD_GROUP_W
    N = SSD_STATE
    nc = S // SSD_CHUNK
    return pl.pallas_call(
        _ssd_kernel,
        grid=(B, SSD_GROUPS),
        in_specs=[
            pl.BlockSpec((1, S, W), lambda b, g: (b, 0, g)),
            pl.BlockSpec((1, S, N), lambda b, g: (b, 0, SSD_D_INNER // N + g)),
            pl.BlockSpec((1, S, N), lambda b, g: (b, 0, SSD_D_INNER // N + SSD_GROUPS + g)),
            pl.BlockSpec((1, S, W), lambda b, g: (b, 0, U_Z // W + g)),
            pl.BlockSpec((1, S, LANE), lambda b, g: (b, 0, g)),
            pl.BlockSpec((SMALL_T_ROWS, S), lambda b, g: (g, b)),
            pl.BlockSpec((8, LANE), lambda b, g: (g, 0)),
            pl.BlockSpec((1, SMALL_T_ROWS, 8), lambda b, g: (g, 0, 0)),
            pl.BlockSpec((1, W), lambda b, g: (0, g)),
            pl.BlockSpec((1, W), lambda b, g: (0, g)),
        ],
        out_specs=pl.BlockSpec((1, S, W), lambda b, g: (b, 0, g)),
        out_shape=jax.ShapeDtypeStruct((B, S, SSD_D_INNER), BF16),
        scratch_shapes=[
            pltpu.VMEM((S, W), F32),
            pltpu.VMEM((S, LANE), F32),
            pltpu.VMEM((SMALL_T_ROWS, S), F32),
            pltpu.VMEM((S, LANE), F32),
            pltpu.VMEM((nc, 16, LANE), F32),
            pltpu.VMEM((nc, N, 2 * W), F32),
            pltpu.VMEM((nc, 8, 2 * W), F32),
            pltpu.VMEM((nc, N, 2 * W), BF16),
            pltpu.VMEM((N, 2 * W), F32),
        ],
        compiler_params=_cparams(("parallel", "parallel")),
        name="ssd_scan",
    )(xbc, xbc, xbc, u3, us3, ust, prow, pcol, drow, nw)


def _gla_kernel(q_ref, k_ref, v_ref, gg_ref, us_ref, a2_ref, bias_ref, nw_ref,
                o_ref, acc_ref, g_ref, qd_ref, kd_ref, kdp_ref, qcat_ref, x_ref, dec_ref, sst_ref):
    L = GLA_CHUNK
    BL = 2 * L
    DK = GLA_DK
    S = q_ref.shape[1]
    nb = S // BL
    scale = DK ** -0.5

    ii = _iota2((BL, BL), 0)
    jj = _iota2((BL, BL), 1)
    same = (ii // L) == (jj // L)
    masks = (same & (ii >= jj), same & (jj >= ii))
    tri2 = masks[0].astype(BF16)
    par_row = _iota2((BL, DK), 0) // L

    ga = us_ref[0].astype(BF16)
    g_ref[...] = _log2_sigmoid(_dot(ga, a2_ref[...]) + bias_ref[...]) * (1.0 / GLA_GATE_NORM)

    def decay_sums(i):
        g = g_ref[pl.ds(pl.multiple_of(i * BL, BL), BL), :]
        hi = g.astype(BF16)
        r1 = g - hi.astype(F32)
        mid = r1.astype(BF16)
        lo = (r1 - mid.astype(F32)).astype(BF16)
        return g, _dot(tri2, jnp.concatenate([hi, mid, lo], axis=1))

    def decay_block(i, sums):
        rs = pl.ds(pl.multiple_of(i * BL, BL), BL)
        g, pp = sums
        p = pp[:, 0:2 * DK] + pp[:, 2 * DK:4 * DK] + pp[:, 4 * DK:6 * DK]
        q_c = q_ref[0, rs, :].astype(F32) * scale
        k_c = k_ref[0, rs, :].astype(F32)
        zero = jnp.zeros((BL, DK), BF16)
        for d in range(2):
            p_d = p[:, d * DK:(d + 1) * DK]
            tot = jnp.where(par_row == 0, p_d[L - 1:L, :], p_d[BL - 1:BL, :])
            b = p_d if d == 0 else tot - p_d + g[:, DK:]
            qd = (q_c * jnp.exp2(b)).astype(BF16)
            kdec = (k_c * jnp.exp2(tot - b)).astype(BF16)
            qd_ref[d, rs, :] = qd
            kd_ref[d, rs, :] = (k_c * jnp.exp2(-b)).astype(BF16)
            for par in range(2):
                sel = par_row == par
                kdp_ref[d, rs, par * DK:(par + 1) * DK] = jnp.where(sel, kdec, zero)
                qcat_ref[rs, (2 * d + par) * DK:(2 * d + par + 1) * DK] = jnp.where(sel, qd, zero)
                last = (par + 1) * L - 1
                dec_ref[d, 2 * i + par] = jnp.broadcast_to(jnp.exp2(p_d[last:last + 1, :]), (8, DK))

    G = GLA_BLOCKS_PER_STEP
    n_groups = nb // G

    def intra_group(i):
        blks = [G * i + j for j in range(G)]
        rss = [pl.ds(pl.multiple_of(b * BL, BL), BL) for b in blks]
        vs = [v_ref[0, rs, :] for rs in rss]
        atts = [[_nt(qd_ref[d, rs, :], kd_ref[d, rs, :]) for d in range(2)] for rs in rss]
        for j, b in enumerate(blks):
            for d in range(2):
                x_ref[d, b] = _tn(vs[j], kdp_ref[d, rss[j], :])
        for j in range(G):
            att = jnp.where(masks[0], atts[j][0], 0.0) + jnp.where(masks[1], atts[j][1], 0.0)
            acc_ref[rss[j], :] = _dot(att.astype(BF16), vs[j])

    def fused_body(i, carry):
        nxt = jnp.minimum(i + 1, n_groups - 1)
        sums = [decay_sums(G * nxt + j) for j in range(G)]
        intra_group(i)
        for j in range(G):
            decay_block(G * nxt + j, sums[j])
        return carry

    for j in range(G):
        decay_block(j, decay_sums(j))
    lax.fori_loop(0, n_groups, fused_body, 0)

    def state_body(d):
        def body(i, s):
            b = i if d == 0 else nb - 1 - i
            for par in ((0, 1) if d == 0 else (1, 0)):
                lane0 = (2 * d + par) * DK
                sst_ref[b, :, lane0:lane0 + DK] = s.astype(BF16)
                s = s * dec_ref[d, 2 * b + par, 0:1, :] + x_ref[d, b, :, par * DK:(par + 1) * DK]
            return s
        return body

    for d in range(2):
        lax.fori_loop(0, nb, state_body(d), jnp.zeros((GLA_DV, DK), F32))

    def out_body(b, carry):
        rs = pl.ds(pl.multiple_of(b * BL, BL), BL)
        o = acc_ref[rs, :] + _nt(qcat_ref[rs, :], sst_ref[b])
        ms = jnp.mean(o * o, axis=-1, keepdims=True)
        o = o * lax.rsqrt(ms + EPS) * nw_ref[...]
        o_ref[0, rs, :] = (o * _silu(gg_ref[0, rs, :].astype(F32))).astype(BF16)
        return carry

    lax.fori_loop(0, nb, out_body, 0, unroll=GLA_OUT_UNROLL)


def _gla(u3, us3, a2, bias, nw):
    B, S, _ = u3.shape
    DK, DV = GLA_DK, GLA_DV
    nc = S // GLA_CHUNK
    nb = nc // 2
    return pl.pallas_call(
        _gla_kernel,
        grid=(B, GLA_HEADS),
        in_specs=[
            pl.BlockSpec((1, S, DK), lambda b, h: (b, 0, U_GQ // DK + h)),
            pl.BlockSpec((1, S, DK), lambda b, h: (b, 0, U_GK // DK + h)),
            pl.BlockSpec((1, S, DV), lambda b, h: (b, 0, U_GV // DV + h)),
            pl.BlockSpec((1, S, DV), lambda b, h: (b, 0, U_GG // DV + h)),
            pl.BlockSpec((1, S, LANE), lambda b, h: (b, 0, 0)),
            pl.BlockSpec((LANE, 2 * DK), lambda b, h: (0, h)),
            pl.BlockSpec((1, 2 * DK), lambda b, h: (0, h)),
            pl.BlockSpec((1, DV), lambda b, h: (0, 0)),
        ],
        out_specs=pl.BlockSpec((1, S, DV), lambda b, h: (b, 0, h)),
        out_shape=jax.ShapeDtypeStruct((B, S, GLA_VAL_W), BF16),
        scratch_shapes=[
            pltpu.VMEM((S, DV), F32),
            pltpu.VMEM((S, 2 * DK), F32),
            pltpu.VMEM((2, S, DK), BF16),
            pltpu.VMEM((2, S, DK), BF16),
            pltpu.VMEM((2, S, 2 * DK), BF16),
            pltpu.VMEM((S, 4 * DK), BF16),
            pltpu.VMEM((2, nb, DV, 2 * DK), F32),
            pltpu.VMEM((2, nc, 8, DK), F32),
            pltpu.VMEM((nb, DV, 4 * DK), BF16),
        ],
        compiler_params=_cparams(("parallel", "parallel")),
        name="gla_scan",
    )(u3, u3, u3, u3, us3, a2, bias, nw)


def _na_bias_rows(rpb):
    H, R, C = rpb.shape
    n_pos = GRID_W - NA_WIN_W
    n_neg = GRID_W - NA_WIN_W + 2
    ext = jnp.concatenate([rpb[:, :, NA_WIN_W - 1:], jnp.repeat(rpb[:, :, C - 1:], n_pos, axis=2),
                           jnp.repeat(rpb[:, :, 0:1], n_neg, axis=2), rpb[:, :, 1:NA_WIN_W - 1]], axis=2)
    return jnp.pad(ext, ((0, 0), (0, 1), (0, 0))).reshape(H // 2, 2, R + 1, 2 * GRID_W)


def _na_kernel(flag_ref, q_ref, k_ref, v_ref, ext_ref, o_ref, vx_ref, tab_ref):
    NB, S, _ = q_ref.shape
    rows = S // GRID_W
    total = NB * rows
    win_h = NA_WIN_H
    nk = win_h * GRID_W

    vx_ref[:, :, 0:LANE] = v_ref[...]
    vx_ref[:, :, LANE:2 * LANE] = jnp.ones((NB, S, LANE), BF16)
    bound_ok = flag_ref[0] != 0

    first_q = _iota2((GRID_W, LANE), 1) < NA_HEAD_DIM

    @pl.when(pl.program_id(1) == 0)
    def _():
        q_col = _iota2((GRID_W, 2 * GRID_W), 0)
        k_col = _iota2((GRID_W, 2 * GRID_W), 1) % GRID_W
        w_start = jnp.clip(q_col - NA_WIN_W // 2, 0, GRID_W - NA_WIN_W)
        in_window = (k_col >= w_start) & (k_col < w_start + NA_WIN_W)
        low = _iota2((GRID_W, 2 * GRID_W), 1) < GRID_W
        for hd in range(2):
            def skewed(rr, shift):
                row = jnp.broadcast_to(ext_ref[0, hd, rr:rr + 1, :], (GRID_W, 2 * GRID_W))
                return pltpu.roll(row, shift, axis=1, stride=1, stride_axis=0)

            for e in range(2 * NA_WIN_H - 2):
                t = jnp.where(low, skewed(e, 0), skewed(e + 1, GRID_W))
                tab_ref[e, hd * GRID_W:(hd + 1) * GRID_W, :] = jnp.where(in_window, t, -jnp.inf)

    def locate(r):
        bi = r // rows
        rl = r - bi * rows
        r0 = jnp.clip(rl - win_h // 2, 0, rows - win_h)
        return bi, pl.multiple_of(rl * GRID_W, GRID_W), rl - r0, pl.multiple_of(r0 * GRID_W, GRID_W)

    def scores(r):
        bi, q0, delta, k0 = locate(r)
        q = q_ref[bi, pl.ds(q0, GRID_W), :]
        zero = jnp.zeros_like(q)
        qs = jnp.concatenate([jnp.where(first_q, q, zero), jnp.where(first_q, zero, q)], axis=0)
        rr0 = (win_h - 1) - delta
        bias = jnp.concatenate([tab_ref[rr0 + w] for w in range(0, win_h, 2)], axis=1)
        return _nt(qs, k_ref[bi, pl.ds(k0, nk), :]) + bias

    def probs_exact(s):
        return jnp.exp2(s - jnp.max(s, axis=-1, keepdims=True)).astype(BF16)

    def probs_bounded(r):
        return jnp.exp2(scores(r)).astype(BF16)

    def attend(r, p):
        bi, q0, _, k0 = locate(r)
        ox = _dot(p, vx_ref[bi, pl.ds(k0, nk), :])
        o = ox[:, 0:LANE] / ox[:, LANE:2 * LANE]
        o_ref[bi, pl.ds(q0, GRID_W), :] = jnp.where(first_q, o[:GRID_W], o[GRID_W:]).astype(BF16)

    @pl.when(bound_ok)
    def _():
        U = NA_ROWS_PER_STEP

        def row_body(i, p_prev):
            r = i * U
            for j in range(U):
                attend(r - U + j, p_prev[j])
            return tuple(probs_bounded(r + j) for j in range(U))

        p_last = lax.fori_loop(1, total // U, row_body, tuple(probs_bounded(j) for j in range(U)))
        for j in range(U):
            attend(total - U + j, p_last[j])

    @pl.when(jnp.logical_not(bound_ok))
    def _():
        U = NA_ROWS_PER_STEP_EXACT

        def row_body(i, carry):
            s_cur, p_prev = carry
            r = i * U
            for j in range(U):
                attend(r - U + j, p_prev[j])
            p = tuple(probs_exact(s) for s in s_cur)
            s_next = tuple(scores(jnp.minimum(r + U + j, total - 1)) for j in range(U))
            return s_next, p

        p0 = tuple(probs_exact(scores(j)) for j in range(U))
        s1 = tuple(scores(U + j) for j in range(U))
        _, p_last = lax.fori_loop(1, total // U, row_body, (s1, p0))
        for j in range(U):
            attend(total - U + j, p_last[j])


def _na_score_bound(rpb, q_norm_w, k_norm_w):
    H = rpb.shape[0]
    qk = (NA_HEAD_DIM ** 0.5 * LOG2E * (1.0 + NA_BOUND_SLACK)) * jnp.max(jnp.abs(q_norm_w)) * jnp.max(jnp.abs(k_norm_w))
    b_max = jnp.max(rpb.reshape(H, -1), axis=1) * LOG2E
    b_self = rpb[:, NA_WIN_H - 1, NA_WIN_W - 1] * LOG2E
    bound = qk + b_max
    flag = jnp.all(bound - (b_self - qk) <= NA_MAX_BOUND_GAP)
    return bound, flag.astype(jnp.int32).reshape(1)


def _na(u3, bias_rows, flag):
    B, S, _ = u3.shape
    assert S // GRID_W >= NA_WIN_H and NA_WIN_H % 2 == 0 and 2 * GRID_W == LANE
    nb = math.gcd(B, NA_BATCH_PER_STEP)
    assert (nb * (S // GRID_W)) % NA_ROWS_PER_STEP == 0
    return pl.pallas_call(
        _na_kernel,
        grid=(NA_HEADS // 2, B // nb),
        in_specs=[
            pl.BlockSpec(memory_space=pltpu.SMEM),
            pl.BlockSpec((nb, S, LANE), lambda h, b: (b, 0, U_NQ // LANE + h)),
            pl.BlockSpec((nb, S, LANE), lambda h, b: (b, 0, U_NK // LANE + h)),
            pl.BlockSpec((nb, S, LANE), lambda h, b: (b, 0, U_NV // LANE + h)),
            pl.BlockSpec((1, 2, 2 * NA_WIN_H, 2 * GRID_W), lambda h, b: (h, 0, 0, 0)),
        ],
        out_specs=pl.BlockSpec((nb, S, LANE), lambda h, b: (b, 0, h)),
        out_shape=jax.ShapeDtypeStruct((B, S, NA_W), BF16),
        scratch_shapes=[
            pltpu.VMEM((nb, S, 2 * LANE), BF16),
            pltpu.VMEM((2 * NA_WIN_H - 2, 2 * GRID_W, 2 * GRID_W), F32),
        ],
        compiler_params=_cparams(("parallel", "arbitrary")),
        name="na_attn",
    )(flag, u3, u3, u3, bias_rows)


def _merge_kernel(x_ref, ys_ref, yg_ref, yn_ref, gate_ref, ws_ref, wg_ref, wn_ref, wo_ref, o_ref):
    D = D_MODEL
    mixed = _sigmoid(gate_ref[:, 0:D].astype(F32)) * _dot(ys_ref[...], ws_ref[...])
    mixed += _sigmoid(gate_ref[:, D:2 * D].astype(F32)) * _dot(yg_ref[...], wg_ref[...])
    mixed += _sigmoid(gate_ref[:, 2 * D:3 * D].astype(F32)) * _dot(yn_ref[...], wn_ref[...])
    o_ref[...] = x_ref[...] + _dot(mixed.astype(BF16), wo_ref[...])


def _merge(x2, ys, yg, yn, u2, ws, wg, wn, wo, tm=512):
    T = x2.shape[0]
    D = D_MODEL
    row = lambda i: (i, 0)
    fixed = lambda i: (0, 0)
    return pl.pallas_call(
        _merge_kernel,
        grid=(T // tm,),
        in_specs=[
            pl.BlockSpec((tm, D), row),
            pl.BlockSpec((tm, D), row),
            pl.BlockSpec((tm, D), row),
            pl.BlockSpec((tm, D), row),
            pl.BlockSpec((tm, N_BRANCH * D), lambda i: (i, U_GATE // (N_BRANCH * D))),
            pl.BlockSpec((D, D), fixed),
            pl.BlockSpec((D, D), fixed),
            pl.BlockSpec((D, D), fixed),
            pl.BlockSpec((D, D), fixed),
        ],
        out_specs=pl.BlockSpec((tm, D), row),
        out_shape=jax.ShapeDtypeStruct((T, D), F32),
        compiler_params=_cparams(("parallel",)),
        name="merge",
    )(x2, ys, yg, yn, u2, ws, wg, wn, wo)


def _mlp_kernel(x_ref, nw_ref, w1_ref, w2_ref, o_ref, *, tf):
    x = x_ref[...]
    ms = jnp.mean(x * x, axis=-1, keepdims=True)
    h = (x * lax.rsqrt(ms + EPS) * nw_ref[...]).astype(BF16)
    acc = x
    for f in range(D_FF // tf):
        a = jnp.maximum(_dot(h, w1_ref[:, f * tf:(f + 1) * tf]), 0.0)
        acc = acc + _dot((a * a).astype(BF16), w2_ref[f * tf:(f + 1) * tf, :])
    o_ref[...] = acc


def _mlp(x2, nw, w1, w2, tm=512, tf=1024):
    T = x2.shape[0]
    D = D_MODEL
    return pl.pallas_call(
        functools.partial(_mlp_kernel, tf=tf),
        grid=(T // tm,),
        in_specs=[
            pl.BlockSpec((tm, D), lambda i: (i, 0)),
            pl.BlockSpec((1, D), lambda i: (0, 0)),
            pl.BlockSpec((D, D_FF), lambda i: (0, 0)),
            pl.BlockSpec((D_FF, D), lambda i: (0, 0)),
        ],
        out_specs=pl.BlockSpec((tm, D), lambda i: (i, 0)),
        out_shape=jax.ShapeDtypeStruct((T, D), F32),
        compiler_params=_cparams(("parallel",)),
        name="mlp",
    )(x2, nw, w1, w2)


def _pad_rows(w, start, total):
    return jnp.zeros((total, w.shape[1]), w.dtype).at[start:start + w.shape[0]].set(w)


def kernel(x, norm_mix_w, w_in, ssd_conv_w, ssd_conv_b, ssd_dt_bias_f, ssd_dt_bias_b, ssd_a_log_f,
           ssd_a_log_b, ssd_d, ssd_norm_w, gla_a2_f, gla_a2_bias_f, gla_a2_b, gla_a2_bias_b,
           gla_norm_w, na_q_norm_w, na_k_norm_w, na_rpb, w_branch_ssd, w_branch_gla, w_branch_na,
           w_out, norm_mlp_w, w_ff1, w_ff2):
    B, S, D = x.shape
    T = B * S
    depth = w_in.shape[0]
    x2 = x.reshape(T, D)
    w_in_t = jnp.swapaxes(w_in, 1, 2)
    for l in range(depth):
        w_big, w_small = _permute_weight(w_in_t, l)
        hg = (SSD_GROUPS, SSD_HG)
        zeros_r = jnp.zeros((SSD_GROUPS, LANE - 2 * SSD_HG), F32)
        dt_bias = jnp.concatenate([ssd_dt_bias_f[l].reshape(hg), ssd_dt_bias_b[l].reshape(hg), zeros_r], axis=1)
        a_neg = jnp.concatenate([-jnp.exp(ssd_a_log_f[l]).reshape(hg), -jnp.exp(ssd_a_log_b[l]).reshape(hg),
                                 zeros_r], axis=1)
        prow3 = jnp.concatenate([dt_bias[:, None], a_neg[:, None], jnp.zeros((SSD_GROUPS, 6, LANE), F32)], axis=1)
        prow = prow3.reshape(SSD_GROUPS * 8, LANE)
        pcol = jnp.transpose(prow3[:, :, :SMALL_T_ROWS], (0, 2, 1))
        drow = jnp.repeat(ssd_d[l], SSD_HEAD_DIM)[None, :]
        hk = (GLA_HEADS, 1, GLA_DK)
        a2 = jnp.concatenate([_pad_rows(gla_a2_f[l], SM_GAF, LANE).reshape((LANE,) + hk),
                              _pad_rows(gla_a2_b[l], SM_GAB, LANE).reshape((LANE,) + hk)],
                             axis=2).reshape(LANE, -1).astype(BF16)
        a2_bias = jnp.concatenate([gla_a2_bias_f[l].reshape(hk), gla_a2_bias_b[l].reshape(hk)],
                                  axis=1).reshape(1, -1)
        na_bound, na_flag = _na_score_bound(na_rpb[l], na_q_norm_w[l], na_k_norm_w[l])
        table = _na_bias_rows(na_rpb[l] * LOG2E - na_bound[:, None, None])
        q_row = jnp.tile(na_q_norm_w[l] * (NA_HEAD_DIM ** -0.5 * LOG2E), NA_HEADS)
        k_row = jnp.tile(na_k_norm_w[l], NA_HEADS)
        qkw = jnp.concatenate([q_row[None], k_row[None], jnp.zeros((6, NA_W), F32)], axis=0)

        u2, us2, ust = _inproj(x2, norm_mix_w[l][None, :], w_big, w_small, qkw)
        u3 = u2.reshape(B, S, U_WIDTH)
        us3 = us2.reshape(B, S, SMALL_W)
        xbc = _conv(u3, ssd_conv_w[l], ssd_conv_b[l][None, :])
        y_ssd = _ssd(xbc, u3, us3, ust, prow, pcol, drow, ssd_norm_w[l][None, :])
        y_gla = _gla(u3, us3, a2, a2_bias, gla_norm_w[l][None, :])
        y_na = _na(u3, table, na_flag)
        x2 = _merge(x2, y_ssd.reshape(T, -1), y_gla.reshape(T, -1), y_na.reshape(T, -1), u2,
                    w_branch_ssd[l].astype(BF16), w_branch_gla[l].astype(BF16),
                    w_branch_na[l].astype(BF16), w_out[l].astype(BF16))
        x2 = _mlp(x2, norm_mlp_w[l][None, :], w_ff1[l].astype(BF16), w_ff2[l].astype(BF16))
    return x2.reshape(B, S, D)
```

```python
import functools
import math

import jax
import jax.numpy as jnp
import numpy as np
from jax import lax
from jax.experimental import pallas as pl
from jax.experimental.pallas import tpu as pltpu

F32 = jnp.float32
BF16 = jnp.bfloat16

EPS = 1e-6
D_MODEL = 1024
GRID_W = 64

SSD_HEADS = 16
SSD_HEAD_DIM = 64
SSD_D_INNER = 1024
SSD_GROUPS = 2
SSD_STATE = 128
SSD_CONV = 5
SSD_CONV_DIM = 1536
SSD_CHUNK = 128
SSD_CHUNKS_PER_STEP = 4
SSD_OUT_UNROLL = 8
SSD_GROUP_W = SSD_D_INNER // SSD_GROUPS
SSD_HG = SSD_HEADS // SSD_GROUPS

GLA_HEADS = 4
GLA_DK = 128
GLA_DV = 256
GLA_KEY_W = 512
GLA_VAL_W = 1024
GLA_GATE_RANK = 16
GLA_GATE_NORM = 16.0
GLA_CHUNK = 64
GLA_BLOCKS_PER_STEP = 4
GLA_OUT_UNROLL = 16

NA_HEADS = 16
NA_HEAD_DIM = 64
NA_W = 1024
NA_WIN_H = 8
NA_WIN_W = 16
NA_BATCH_PER_STEP = 4
NA_ROWS_PER_STEP = 32
NA_ROWS_PER_STEP_EXACT = 2
LOG2E = 1.4426950408889634
NA_BOUND_SLACK = 0.02
NA_MAX_BOUND_GAP = 90.0

N_BRANCH = 3
D_FF = 4096

IN_SIZES = (SSD_D_INNER, SSD_CONV_DIM, SSD_HEADS, SSD_HEADS,
            GLA_KEY_W, GLA_KEY_W, GLA_VAL_W, GLA_VAL_W, GLA_GATE_RANK, GLA_GATE_RANK,
            NA_W, NA_W, NA_W, N_BRANCH * D_MODEL)
_IN_OFF = np.concatenate([[0], np.cumsum(IN_SIZES)])
(_O_Z, _O_XBC, _O_DTF, _O_DTB, _O_GQ, _O_GK, _O_GV, _O_GG, _O_GAF, _O_GAB,
 _O_NQ, _O_NK, _O_NV, _O_GATE) = [int(v) for v in _IN_OFF[:-1]]

U_NQ = 0
U_NK = 1024
U_NV = 2048
U_GATE = 3072
U_Z = 6144
U_XBC = 7168
U_GQ = 8704
U_GK = 9216
U_GV = 9728
U_GG = 10752
U_WIDTH = 11776
INPROJ_COL_CHUNK = 1024
LANE = 128
VMEM_LIMIT = 56 * 1024 * 1024

SMALL_W = SSD_GROUPS * LANE
SM_DTF, SM_DTB, SM_GAF, SM_GAB = 0, 8, 16, 32
SMALL_T_ROWS = 2 * SSD_HG

_BIG_SEGS = ((_O_NQ, 6144), (_O_Z, 2560), (_O_GQ, 3072))


def _cparams(sem, vmem=VMEM_LIMIT):
    return pltpu.CompilerParams(dimension_semantics=sem, vmem_limit_bytes=vmem)


def _sigmoid(x):
    return 1.0 / (1.0 + jnp.exp(-x))


def _silu(x):
    return x * _sigmoid(x)


def _softplus(x):
    return jnp.maximum(x, 0.0) + jnp.log1p(jnp.exp(-jnp.abs(x)))


def _log2_sigmoid(x):
    t = x * (-LOG2E)
    return -(jnp.maximum(t, 0.0) + jnp.log2(1.0 + jnp.exp2(-jnp.abs(t))))


def _nt(a, b):
    return lax.dot_general(a, b, (((1,), (1,)), ((), ())), preferred_element_type=F32)


def _tn(a, b):
    return lax.dot_general(a, b, (((0,), (0,)), ((), ())), preferred_element_type=F32)


def _dot(a, b):
    return jnp.dot(a, b, preferred_element_type=F32)


def _iota2(shape, dim):
    return lax.broadcasted_iota(jnp.int32, shape, dim)


def _wperm_kernel(wt_ref, o_ref, ws_ref):
    off = 0
    for a, n in _BIG_SEGS:
        o_ref[off:off + n, :] = wt_ref[0, a:a + n, :].astype(BF16)
        off += n
    ws_ref[...] = jnp.zeros_like(ws_ref)
    for g in range(SSD_GROUPS):
        base, h0 = g * LANE, g * SSD_HG
        ws_ref[base + SM_DTF:base + SM_DTF + SSD_HG, :] = wt_ref[0, _O_DTF + h0:_O_DTF + h0 + SSD_HG, :]
        ws_ref[base + SM_DTB:base + SM_DTB + SSD_HG, :] = wt_ref[0, _O_DTB + h0:_O_DTB + h0 + SSD_HG, :]
    ws_ref[SM_GAF:SM_GAF + GLA_GATE_RANK, :] = wt_ref[0, _O_GAF:_O_GAF + GLA_GATE_RANK, :]
    ws_ref[SM_GAB:SM_GAB + GLA_GATE_RANK, :] = wt_ref[0, _O_GAB:_O_GAB + GLA_GATE_RANK, :]


def _permute_weight(w_t_all, layer, tc=256):
    _, n_in, d = w_t_all.shape
    return pl.pallas_call(
        _wperm_kernel,
        grid=(d // tc,),
        in_specs=[pl.BlockSpec((1, n_in, tc), lambda i: (layer, 0, i))],
        out_specs=[
            pl.BlockSpec((U_WIDTH, tc), lambda i: (0, i)),
            pl.BlockSpec((SMALL_W, tc), lambda i: (0, i)),
        ],
        out_shape=[
            jax.ShapeDtypeStruct((U_WIDTH, d), BF16),
            jax.ShapeDtypeStruct((SMALL_W, d), F32),
        ],
        compiler_params=_cparams(("parallel",)),
        name="w_permute",
    )(w_t_all)


def _head_rms(r, w_row):
    G = 2 * LANE
    er = _iota2((G, G), 0) // NA_HEAD_DIM
    ec = _iota2((G, G), 1) // NA_HEAD_DIM
    e_blk = (er == ec).astype(BF16)
    outs = []
    for a in range(0, r.shape[1], G):
        x = r[:, a:a + G]
        ms = _dot((x * x).astype(BF16), e_blk) * (1.0 / NA_HEAD_DIM)
        outs.append(x * lax.rsqrt(ms + EPS))
    return jnp.concatenate(outs, axis=1) * w_row


def _inproj_kernel(x_ref, nw_ref, w_ref, ws_ref, qkw_ref, u_ref, us_ref, ust_ref, h_ref):
    tn = u_ref.shape[1]
    chunks = [(a, min(a + INPROJ_COL_CHUNK, tn)) for a in range(0, tn, INPROJ_COL_CHUNK)]

    @pl.when(pl.program_id(1) == 0)
    def _():
        x = x_ref[...]
        ms = jnp.mean(x * x, axis=-1, keepdims=True)
        h = (x * lax.rsqrt(ms + EPS) * nw_ref[...]).astype(BF16)
        h_ref[...] = h
        ws = ws_ref[...].astype(BF16)
        us_ref[...] = _nt(h, ws)
        dt_rows = jnp.concatenate([ws[g * LANE:g * LANE + SMALL_T_ROWS] for g in range(SSD_GROUPS)], axis=0)
        ust_ref[...] = _nt(dt_rows, h)
        for c, (a, b) in enumerate(chunks):
            r = _nt(h, w_ref[a:b, :])
            if c < 2:
                r = _head_rms(r, qkw_ref[c:c + 1, :])
            u_ref[:, a:b] = r.astype(BF16)

    @pl.when(pl.program_id(1) != 0)
    def _():
        for a, b in chunks:
            u_ref[:, a:b] = _nt(h_ref[...], w_ref[a:b, :]).astype(BF16)


def _inproj(x2, nw, w_big, w_small, qkw, tm=1024, tn=U_WIDTH // 4):
    assert (U_NQ, U_NK) == (0, INPROJ_COL_CHUNK) and NA_W == INPROJ_COL_CHUNK
    T = x2.shape[0]
    return pl.pallas_call(
        _inproj_kernel,
        grid=(T // tm, U_WIDTH // tn),
        in_specs=[
            pl.BlockSpec((tm, D_MODEL), lambda i, j: (i, 0)),
            pl.BlockSpec((1, D_MODEL), lambda i, j: (0, 0)),
            pl.BlockSpec((tn, D_MODEL), lambda i, j: (j, 0)),
            pl.BlockSpec((SMALL_W, D_MODEL), lambda i, j: (0, 0)),
            pl.BlockSpec((8, NA_W), lambda i, j: (0, 0)),
        ],
        out_specs=[
            pl.BlockSpec((tm, tn), lambda i, j: (i, j)),
            pl.BlockSpec((tm, SMALL_W), lambda i, j: (i, 0)),
            pl.BlockSpec((SSD_GROUPS * SMALL_T_ROWS, tm), lambda i, j: (0, i)),
        ],
        out_shape=[
            jax.ShapeDtypeStruct((T, U_WIDTH), BF16),
            jax.ShapeDtypeStruct((T, SMALL_W), F32),
            jax.ShapeDtypeStruct((SSD_GROUPS * SMALL_T_ROWS, T), F32),
        ],
        scratch_shapes=[pltpu.VMEM((tm, D_MODEL), BF16)],
        compiler_params=_cparams(("parallel", "arbitrary")),
        name="inproj",
    )(x2, nw, w_big, w_small, qkw)


def _conv_kernel(u_ref, w_ref, b_ref, o_ref):
    x = u_ref[0].astype(F32)
    S = x.shape[0]
    row = _iota2(x.shape, 0)
    acc = jnp.zeros_like(x) + b_ref[...]
    pad = SSD_CONV // 2
    for k in range(SSD_CONV):
        d = k - pad
        if d == 0:
            xs = x
        else:
            xs = pltpu.roll(x, (-d) % S, axis=0)
            valid = (row + d >= 0) & (row + d < S)
            xs = jnp.where(valid, xs, 0.0)
        acc = acc + w_ref[k:k + 1, :] * xs
    o_ref[0] = _silu(acc).astype(BF16)


def _conv(u3, conv_w, conv_b, tc=256):
    B, S, _ = u3.shape
    nblk = SSD_CONV_DIM // tc
    off = U_XBC // tc
    return pl.pallas_call(
        _conv_kernel,
        grid=(B, nblk),
        in_specs=[
            pl.BlockSpec((1, S, tc), lambda b, c: (b, 0, off + c)),
            pl.BlockSpec((SSD_CONV, tc), lambda b, c: (0, c)),
            pl.BlockSpec((1, tc), lambda b, c: (0, c)),
        ],
        out_specs=pl.BlockSpec((1, S, tc), lambda b, c: (b, 0, c)),
        out_shape=jax.ShapeDtypeStruct((B, S, SSD_CONV_DIM), BF16),
        compiler_params=_cparams(("parallel", "parallel")),
        name="ssd_conv",
    )(u3, conv_w, conv_b)


def _split_hi_lo(v):
    hi = v.astype(BF16)
    lo = (v - hi.astype(F32)).astype(BF16)
    return jnp.concatenate([hi, lo], axis=1)


def _split3(v, axis):
    hi = v.astype(BF16)
    r1 = v - hi.astype(F32)
    mid = r1.astype(BF16)
    lo = (r1 - mid.astype(F32)).astype(BF16)
    return jnp.concatenate([hi, mid, lo], axis=axis)


def _ssd_kernel(x_ref, b_ref, c_ref, z_ref, us_ref, ust_ref, prow_ref, pcol_ref, drow_ref, nw_ref,
                o_ref, acc_ref, cumc_ref, cumr_ref, wdt_ref, dec_ref, xs_ref, decx_ref, sst_ref, st_ref):
    L = SSD_CHUNK
    S = x_ref.shape[1]
    nc = S // L
    W = SSD_GROUP_W
    HG = SSD_HG
    R = 2 * HG

    ii = _iota2((L, L), 0)
    jj = _iota2((L, L), 1)
    tril = ii >= jj
    triu = jj >= ii
    tril_b = tril.astype(BF16)
    triu_b = triu.astype(BF16)

    er = _iota2((LANE, 2 * W), 0)
    ec = _iota2((LANE, 2 * W), 1)
    e = (er == jnp.where(ec < W, ec // SSD_HEAD_DIM, (ec - W) // SSD_HEAD_DIM + HG)).astype(BF16)
    e2 = jnp.concatenate([e, e], axis=0)

    bias_row = prow_ref[0:1, :]
    a_row = prow_ref[1:2, :]
    bias_col = pcol_ref[0, :, 0:1]
    a_col = pcol_ref[0, :, 1:2]

    lane_fwd = _iota2((L, LANE), 1) < HG
    row_fwd = _iota2((R, L), 0) < HG
    lane_half = _iota2((L, LANE), 1) < SSD_HEAD_DIM

    def decay_sums(c):
        rs = pl.ds(pl.multiple_of(c * L, L), L)
        dt_c = _softplus(us_ref[0, rs, :] + bias_row)
        a_c = dt_c * a_row
        dt_r = _softplus(ust_ref[:, rs] + bias_col)
        a_r = dt_r * a_col
        return dt_c, a_c, _dot(tril_b, _split3(a_c, 1)), dt_r, a_r, _dot(_split3(a_r, 0), triu_b)

    def decay_store(c, sums):
        rs = pl.ds(pl.multiple_of(c * L, L), L)
        dt_c, a_c, pp, dt_r, a_r, pr = sums
        p_c = pp[:, 0:LANE] + pp[:, LANE:2 * LANE] + pp[:, 2 * LANE:3 * LANE]
        tot_c = p_c[L - 1:L, :]
        cum_c = jnp.where(lane_fwd, p_c, tot_c - p_c + a_c)
        cumc_ref[rs, :] = cum_c * LOG2E
        wdt_ref[rs, :] = jnp.exp(tot_c - cum_c) * dt_c
        dec_ref[c] = jnp.broadcast_to(jnp.exp(tot_c), (16, LANE))
        p_r = pr[0:R] + pr[R:2 * R] + pr[2 * R:3 * R]
        tot_r = p_r[:, L - 1:L]
        cum_r = jnp.where(row_fwd, p_r, tot_r - p_r + a_r)
        cumr_ref[:, rs] = (cum_r - jnp.log(dt_r)) * LOG2E

    def intra_chunk(c):
        rs = pl.ds(pl.multiple_of(c * L, L), L)
        x_c = x_ref[0, rs, :]
        b_c = b_ref[0, rs, :]
        cb = _nt(c_ref[0, rs, :], b_c)
        ex = _dot(_split_hi_lo(jnp.concatenate([wdt_ref[rs, :], dec_ref[c]], axis=0)), e2)
        decx_ref[c] = ex[L:L + 8]
        xf = x_c.astype(F32)
        xw = jnp.concatenate([xf * ex[0:L, 0:W], xf * ex[0:L, W:2 * W]], axis=1).astype(BF16)
        xs_ref[c] = _tn(b_c, xw)
        cum_c = cumc_ref[rs, :]
        cum_r = cumr_ref[:, rs]
        for hp in range(HG // 2):
            ms = []
            for hh in range(2):
                hf = 2 * hp + hh
                hb = HG + 2 * hp + hh
                decf = jnp.exp2(jnp.where(tril, cum_c[:, hf:hf + 1] - cum_r[hf:hf + 1, :], -jnp.inf))
                decb = jnp.exp2(jnp.where(triu, cum_c[:, hb:hb + 1] - cum_r[hb:hb + 1, :], -jnp.inf))
                ms.append((cb * (decf + decb)).astype(BF16))
            m2 = jnp.concatenate(ms, axis=1)
            xp = x_c[:, hp * LANE:(hp + 1) * LANE]
            xz = jnp.zeros_like(xp)
            x2 = jnp.concatenate([jnp.where(lane_half, xp, xz), jnp.where(lane_half, xz, xp)], axis=0)
            acc_ref[rs, hp * LANE:(hp + 1) * LANE] = _dot(m2, x2)

    G = SSD_CHUNKS_PER_STEP
    n_steps = nc // G

    def fused_body(i, carry):
        sums = [decay_sums(G * (i + 1) + j) for j in range(G)]
        for j in range(G):
            intra_chunk(G * i + j)
        for j in range(G):
            decay_store(G * (i + 1) + j, sums[j])
        return carry

    for j in range(G):
        decay_store(j, decay_sums(j))
    lax.fori_loop(0, n_steps - 1, fused_body, 0)
    for j in range(G):
        intra_chunk(G * (n_steps - 1) + j)

    st_ref[...] = jnp.zeros_like(st_ref)

    def state_body(i, carry):
        for d, c in enumerate((i, nc - 1 - i)):
            ls = slice(d * W, (d + 1) * W)
            s_prev = st_ref[:, ls]
            sst_ref[c, :, ls] = s_prev.astype(BF16)
            st_ref[:, ls] = s_prev * decx_ref[c, 0:1, ls] + xs_ref[c, :, ls]
        return carry

    lax.fori_loop(0, nc, state_body, 0)

    def out_body(c, carry):
        rs = pl.ds(pl.multiple_of(c * L, L), L)
        yo = _dot(c_ref[0, rs, :], sst_ref[c])
        sc = _dot(_split_hi_lo(jnp.exp2(cumc_ref[rs, :])), e2)
        xf = x_ref[0, rs, :].astype(F32)
        y = acc_ref[rs, :] + yo[:, 0:W] * sc[:, 0:W] + yo[:, W:2 * W] * sc[:, W:2 * W] + xf * drow_ref[...]
        y = y * _silu(z_ref[0, rs, :].astype(F32))
        ms = jnp.mean(y * y, axis=-1, keepdims=True)
        o_ref[0, rs, :] = (y * lax.rsqrt(ms + EPS) * nw_ref[...]).astype(BF16)
        return carry

    lax.fori_loop(0, nc, out_body, 0, unroll=SSD_OUT_UNROLL)


def _ssd(xbc, u3, us3, ust, prow, pcol, drow, nw):
    B, S, _ = xbc.shape
    W = SSD_GROUP_W
    N = SSD_STATE
    nc = S // SSD_CHUNK
    return pl.pallas_call(
        _ssd_kernel,
        grid=(B, SSD_GROUPS),
        in_specs=[
            pl.BlockSpec((1, S, W), lambda b, g: (b, 0, g)),
            pl.BlockSpec((1, S, N), lambda b, g: (b, 0, SSD_D_INNER // N + g)),
            pl.BlockSpec((1, S, N), lambda b, g: (b, 0, SSD_D_INNER // N + SSD_GROUPS + g)),
            pl.BlockSpec((1, S, W), lambda b, g: (b, 0, U_Z // W + g)),
            pl.BlockSpec((1, S, LANE), lambda b, g: (b, 0, g)),
            pl.BlockSpec((SMALL_T_ROWS, S), lambda b, g: (g, b)),
            pl.BlockSpec((8, LANE), lambda b, g: (g, 0)),
            pl.BlockSpec((1, SMALL_T_ROWS, 8), lambda b, g: (g, 0, 0)),
            pl.BlockSpec((1, W), lambda b, g: (0, g)),
            pl.BlockSpec((1, W), lambda b, g: (0, g)),
        ],
        out_specs=pl.BlockSpec((1, S, W), lambda b, g: (b, 0, g)),
        out_shape=jax.ShapeDtypeStruct((B, S, SSD_D_INNER), BF16),
        scratch_shapes=[
            pltpu.VMEM((S, W), F32),
            pltpu.VMEM((S, LANE), F32),
            pltpu.VMEM((SMALL_T_ROWS, S), F32),
            pltpu.VMEM((S, LANE), F32),
            pltpu.VMEM((nc, 16, LANE), F32),
            pltpu.VMEM((nc, N, 2 * W), F32),
            pltpu.VMEM((nc, 8, 2 * W), F32),
            pltpu.VMEM((nc, N, 2 * W), BF16),
            pltpu.VMEM((N, 2 * W), F32),
        ],
        compiler_params=_cparams(("parallel", "parallel")),
        name="ssd_scan",
    )(xbc, xbc, xbc, u3, us3, ust, prow, pcol, drow, nw)


def _gla_kernel(q_ref, k_ref, v_ref, gg_ref, us_ref, a2_ref, bias_ref, nw_ref,
                o_ref, acc_ref, g_ref, qd_ref, kd_ref, kdp_ref, qcat_ref, x_ref, dec_ref, sst_ref):
    L = GLA_CHUNK
    BL = 2 * L
    DK = GLA_DK
    S = q_ref.shape[1]
    nb = S // BL
    scale = DK ** -0.5

    ii = _iota2((BL, BL), 0)
    jj = _iota2((BL, BL), 1)
    same = (ii // L) == (jj // L)
    masks = (same & (ii >= jj), same & (jj >= ii))
    tri2 = masks[0].astype(BF16)
    par_row = _iota2((BL, DK), 0) // L

    ga = us_ref[0].astype(BF16)
    g_ref[...] = _log2_sigmoid(_dot(ga, a2_ref[...]) + bias_ref[...]) * (1.0 / GLA_GATE_NORM)

    def decay_sums(i):
        g = g_ref[pl.ds(pl.multiple_of(i * BL, BL), BL), :]
        hi = g.astype(BF16)
        r1 = g - hi.astype(F32)
        mid = r1.astype(BF16)
        lo = (r1 - mid.astype(F32)).astype(BF16)
        return g, _dot(tri2, jnp.concatenate([hi, mid, lo], axis=1))

    def decay_block(i, sums):
        rs = pl.ds(pl.multiple_of(i * BL, BL), BL)
        g, pp = sums
        p = pp[:, 0:2 * DK] + pp[:, 2 * DK:4 * DK] + pp[:, 4 * DK:6 * DK]
        q_c = q_ref[0, rs, :].astype(F32) * scale
        k_c = k_ref[0, rs, :].astype(F32)
        zero = jnp.zeros((BL, DK), BF16)
        for d in range(2):
            p_d = p[:, d * DK:(d + 1) * DK]
            tot = jnp.where(par_row == 0, p_d[L - 1:L, :], p_d[BL - 1:BL, :])
            b = p_d if d == 0 else tot - p_d + g[:, DK:]
            qd = (q_c * jnp.exp2(b)).astype(BF16)
            kdec = (k_c * jnp.exp2(tot - b)).astype(BF16)
            qd_ref[d, rs, :] = qd
            kd_ref[d, rs, :] = (k_c * jnp.exp2(-b)).astype(BF16)
            for par in range(2):
                sel = par_row == par
                kdp_ref[d, rs, par * DK:(par + 1) * DK] = jnp.where(sel, kdec, zero)
                qcat_ref[rs, (2 * d + par) * DK:(2 * d + par + 1) * DK] = jnp.where(sel, qd, zero)
                last = (par + 1) * L - 1
                dec_ref[d, 2 * i + par] = jnp.broadcast_to(jnp.exp2(p_d[last:last + 1, :]), (8, DK))

    G = GLA_BLOCKS_PER_STEP
    n_groups = nb // G

    def intra_group(i):
        blks = [G * i + j for j in range(G)]
        rss = [pl.ds(pl.multiple_of(b * BL, BL), BL) for b in blks]
        vs = [v_ref[0, rs, :] for rs in rss]
        atts = [[_nt(qd_ref[d, rs, :], kd_ref[d, rs, :]) for d in range(2)] for rs in rss]
        for j, b in enumerate(blks):
            for d in range(2):
                x_ref[d, b] = _tn(vs[j], kdp_ref[d, rss[j], :])
        for j in range(G):
            att = jnp.where(masks[0], atts[j][0], 0.0) + jnp.where(masks[1], atts[j][1], 0.0)
            acc_ref[rss[j], :] = _dot(att.astype(BF16), vs[j])

    def fused_body(i, carry):
        sums = [decay_sums(G * (i + 1) + j) for j in range(G)]
        intra_group(i)
        for j in range(G):
            decay_block(G * (i + 1) + j, sums[j])
        return carry

    for j in range(G):
        decay_block(j, decay_sums(j))
    lax.fori_loop(0, n_groups - 1, fused_body, 0)
    intra_group(n_groups - 1)

    def state_body(d):
        def body(i, s):
            b = i if d == 0 else nb - 1 - i
            for par in ((0, 1) if d == 0 else (1, 0)):
                lane0 = (2 * d + par) * DK
                sst_ref[b, :, lane0:lane0 + DK] = s.astype(BF16)
                s = s * dec_ref[d, 2 * b + par, 0:1, :] + x_ref[d, b, :, par * DK:(par + 1) * DK]
            return s
        return body

    for d in range(2):
        lax.fori_loop(0, nb, state_body(d), jnp.zeros((GLA_DV, DK), F32))

    def out_body(b, carry):
        rs = pl.ds(pl.multiple_of(b * BL, BL), BL)
        o = acc_ref[rs, :] + _nt(qcat_ref[rs, :], sst_ref[b])
        ms = jnp.mean(o * o, axis=-1, keepdims=True)
        o = o * lax.rsqrt(ms + EPS) * nw_ref[...]
        o_ref[0, rs, :] = (o * _silu(gg_ref[0, rs, :].astype(F32))).astype(BF16)
        return carry

    lax.fori_loop(0, nb, out_body, 0, unroll=GLA_OUT_UNROLL)


def _gla(u3, us3, a2, bias, nw):
    B, S, _ = u3.shape
    DK, DV = GLA_DK, GLA_DV
    nc = S // GLA_CHUNK
    nb = nc // 2
    return pl.pallas_call(
        _gla_kernel,
        grid=(B, GLA_HEADS),
        in_specs=[
            pl.BlockSpec((1, S, DK), lambda b, h: (b, 0, U_GQ // DK + h)),
            pl.BlockSpec((1, S, DK), lambda b, h: (b, 0, U_GK // DK + h)),
            pl.BlockSpec((1, S, DV), lambda b, h: (b, 0, U_GV // DV + h)),
            pl.BlockSpec((1, S, DV), lambda b, h: (b, 0, U_GG // DV + h)),
            pl.BlockSpec((1, S, LANE), lambda b, h: (b, 0, 0)),
            pl.BlockSpec((LANE, 2 * DK), lambda b, h: (0, h)),
            pl.BlockSpec((1, 2 * DK), lambda b, h: (0, h)),
            pl.BlockSpec((1, DV), lambda b, h: (0, 0)),
        ],
        out_specs=pl.BlockSpec((1, S, DV), lambda b, h: (b, 0, h)),
        out_shape=jax.ShapeDtypeStruct((B, S, GLA_VAL_W), BF16),
        scratch_shapes=[
            pltpu.VMEM((S, DV), F32),
            pltpu.VMEM((S, 2 * DK), F32),
            pltpu.VMEM((2, S, DK), BF16),
            pltpu.VMEM((2, S, DK), BF16),
            pltpu.VMEM((2, S, 2 * DK), BF16),
            pltpu.VMEM((S, 4 * DK), BF16),
            pltpu.VMEM((2, nb, DV, 2 * DK), F32),
            pltpu.VMEM((2, nc, 8, DK), F32),
            pltpu.VMEM((nb, DV, 4 * DK), BF16),
        ],
        compiler_params=_cparams(("parallel", "parallel")),
        name="gla_scan",
    )(u3, u3, u3, u3, us3, a2, bias, nw)


def _na_bias_rows(rpb):
    H, R, C = rpb.shape
    n_pos = GRID_W - NA_WIN_W
    n_neg = GRID_W - NA_WIN_W + 2
    ext = jnp.concatenate([rpb[:, :, NA_WIN_W - 1:], jnp.repeat(rpb[:, :, C - 1:], n_pos, axis=2),
                           jnp.repeat(rpb[:, :, 0:1], n_neg, axis=2), rpb[:, :, 1:NA_WIN_W - 1]], axis=2)
    return jnp.pad(ext, ((0, 0), (0, 1), (0, 0))).reshape(H // 2, 2, R + 1, 2 * GRID_W)


def _na_kernel(flag_ref, q_ref, k_ref, v_ref, ext_ref, o_ref, vx_ref, tab_ref):
    NB, S, _ = q_ref.shape
    rows = S // GRID_W
    total = NB * rows
    win_h = NA_WIN_H
    nk = win_h * GRID_W

    vx_ref[:, :, 0:LANE] = v_ref[...]
    vx_ref[:, :, LANE:2 * LANE] = jnp.ones((NB, S, LANE), BF16)
    bound_ok = flag_ref[0] != 0

    first_q = _iota2((GRID_W, LANE), 1) < NA_HEAD_DIM

    @pl.when(pl.program_id(1) == 0)
    def _():
        q_col = _iota2((GRID_W, 2 * GRID_W), 0)
        k_col = _iota2((GRID_W, 2 * GRID_W), 1) % GRID_W
        w_start = jnp.clip(q_col - NA_WIN_W // 2, 0, GRID_W - NA_WIN_W)
        in_window = (k_col >= w_start) & (k_col < w_start + NA_WIN_W)
        low = _iota2((GRID_W, 2 * GRID_W), 1) < GRID_W
        for hd in range(2):
            def skewed(rr, shift):
                row = jnp.broadcast_to(ext_ref[0, hd, rr:rr + 1, :], (GRID_W, 2 * GRID_W))
                return pltpu.roll(row, shift, axis=1, stride=1, stride_axis=0)

            for e in range(2 * NA_WIN_H - 2):
                t = jnp.where(low, skewed(e, 0), skewed(e + 1, GRID_W))
                tab_ref[e, hd * GRID_W:(hd + 1) * GRID_W, :] = jnp.where(in_window, t, -jnp.inf)

    def locate(r):
        bi = r // rows
        rl = r - bi * rows
        r0 = jnp.clip(rl - win_h // 2, 0, rows - win_h)
        return bi, pl.multiple_of(rl * GRID_W, GRID_W), rl - r0, pl.multiple_of(r0 * GRID_W, GRID_W)

    def scores(r):
        bi, q0, delta, k0 = locate(r)
        q = q_ref[bi, pl.ds(q0, GRID_W), :]
        zero = jnp.zeros_like(q)
        qs = jnp.concatenate([jnp.where(first_q, q, zero), jnp.where(first_q, zero, q)], axis=0)
        rr0 = (win_h - 1) - delta
        bias = jnp.concatenate([tab_ref[rr0 + w] for w in range(0, win_h, 2)], axis=1)
        return _nt(qs, k_ref[bi, pl.ds(k0, nk), :]) + bias

    def probs_exact(s):
        return jnp.exp2(s - jnp.max(s, axis=-1, keepdims=True)).astype(BF16)

    def probs_bounded(r):
        return jnp.exp2(scores(r)).astype(BF16)

    def attend(r, p):
        bi, q0, _, k0 = locate(r)
        ox = _dot(p, vx_ref[bi, pl.ds(k0, nk), :])
        o = ox[:, 0:LANE] / ox[:, LANE:2 * LANE]
        o_ref[bi, pl.ds(q0, GRID_W), :] = jnp.where(first_q, o[:GRID_W], o[GRID_W:]).astype(BF16)

    @pl.when(bound_ok)
    def _():
        U = NA_ROWS_PER_STEP

        def row_body(i, p_prev):
            r = i * U
            for j in range(U):
                attend(r - U + j, p_prev[j])
            return tuple(probs_bounded(r + j) for j in range(U))

        p_last = lax.fori_loop(1, total // U, row_body, tuple(probs_bounded(j) for j in range(U)))
        for j in range(U):
            attend(total - U + j, p_last[j])

    @pl.when(jnp.logical_not(bound_ok))
    def _():
        U = NA_ROWS_PER_STEP_EXACT

        def row_body(i, carry):
            s_cur, p_prev = carry
            r = i * U
            for j in range(U):
                attend(r - U + j, p_prev[j])
            p = tuple(probs_exact(s) for s in s_cur)
            s_next = tuple(scores(jnp.minimum(r + U + j, total - 1)) for j in range(U))
            return s_next, p

        p0 = tuple(probs_exact(scores(j)) for j in range(U))
        s1 = tuple(scores(U + j) for j in range(U))
        _, p_last = lax.fori_loop(1, total // U, row_body, (s1, p0))
        for j in range(U):
            attend(total - U + j, p_last[j])


def _na_score_bound(rpb, q_norm_w, k_norm_w):
    H = rpb.shape[0]
    qk = (NA_HEAD_DIM ** 0.5 * LOG2E * (1.0 + NA_BOUND_SLACK)) * jnp.max(jnp.abs(q_norm_w)) * jnp.max(jnp.abs(k_norm_w))
    b_max = jnp.max(rpb.reshape(H, -1), axis=1) * LOG2E
    b_self = rpb[:, NA_WIN_H - 1, NA_WIN_W - 1] * LOG2E
    bound = qk + b_max
    flag = jnp.all(bound - (b_self - qk) <= NA_MAX_BOUND_GAP)
    return bound, flag.astype(jnp.int32).reshape(1)


def _na(u3, bias_rows, flag):
    B, S, _ = u3.shape
    assert S // GRID_W >= NA_WIN_H and NA_WIN_H % 2 == 0 and 2 * GRID_W == LANE
    nb = math.gcd(B, NA_BATCH_PER_STEP)
    assert (nb * (S // GRID_W)) % NA_ROWS_PER_STEP == 0
    return pl.pallas_call(
        _na_kernel,
        grid=(NA_HEADS // 2, B // nb),
        in_specs=[
            pl.BlockSpec(memory_space=pltpu.SMEM),
            pl.BlockSpec((nb, S, LANE), lambda h, b: (b, 0, U_NQ // LANE + h)),
            pl.BlockSpec((nb, S, LANE), lambda h, b: (b, 0, U_NK // LANE + h)),
            pl.BlockSpec((nb, S, LANE), lambda h, b: (b, 0, U_NV // LANE + h)),
            pl.BlockSpec((1, 2, 2 * NA_WIN_H, 2 * GRID_W), lambda h, b: (h, 0, 0, 0)),
        ],
        out_specs=pl.BlockSpec((nb, S, LANE), lambda h, b: (b, 0, h)),
        out_shape=jax.ShapeDtypeStruct((B, S, NA_W), BF16),
        scratch_shapes=[
            pltpu.VMEM((nb, S, 2 * LANE), BF16),
            pltpu.VMEM((2 * NA_WIN_H - 2, 2 * GRID_W, 2 * GRID_W), F32),
        ],
        compiler_params=_cparams(("parallel", "arbitrary")),
        name="na_attn",
    )(flag, u3, u3, u3, bias_rows)


def _merge_kernel(x_ref, ys_ref, yg_ref, yn_ref, gate_ref, ws_ref, wg_ref, wn_ref, wo_ref, o_ref):
    D = D_MODEL
    mixed = _sigmoid(gate_ref[:, 0:D].astype(F32)) * _dot(ys_ref[...], ws_ref[...])
    mixed += _sigmoid(gate_ref[:, D:2 * D].astype(F32)) * _dot(yg_ref[...], wg_ref[...])
    mixed += _sigmoid(gate_ref[:, 2 * D:3 * D].astype(F32)) * _dot(yn_ref[...], wn_ref[...])
    o_ref[...] = x_ref[...] + _dot(mixed.astype(BF16), wo_ref[...])


def _merge(x2, ys, yg, yn, u2, ws, wg, wn, wo, tm=512):
    T = x2.shape[0]
    D = D_MODEL
    row = lambda i: (i, 0)
    fixed = lambda i: (0, 0)
    return pl.pallas_call(
        _merge_kernel,
        grid=(T // tm,),
        in_specs=[
            pl.BlockSpec((tm, D), row),
            pl.BlockSpec((tm, D), row),
            pl.BlockSpec((tm, D), row),
            pl.BlockSpec((tm, D), row),
            pl.BlockSpec((tm, N_BRANCH * D), lambda i: (i, U_GATE // (N_BRANCH * D))),
            pl.BlockSpec((D, D), fixed),
            pl.BlockSpec((D, D), fixed),
            pl.BlockSpec((D, D), fixed),
            pl.BlockSpec((D, D), fixed),
        ],
        out_specs=pl.BlockSpec((tm, D), row),
        out_shape=jax.ShapeDtypeStruct((T, D), F32),
        compiler_params=_cparams(("parallel",)),
        name="merge",
    )(x2, ys, yg, yn, u2, ws, wg, wn, wo)


def _mlp_kernel(x_ref, nw_ref, w1_ref, w2_ref, o_ref, *, tf):
    x = x_ref[...]
    ms = jnp.mean(x * x, axis=-1, keepdims=True)
    h = (x * lax.rsqrt(ms + EPS) * nw_ref[...]).astype(BF16)
    acc = x
    for f in range(D_FF // tf):
        a = jnp.maximum(_dot(h, w1_ref[:, f * tf:(f + 1) * tf]), 0.0)
        acc = acc + _dot((a * a).astype(BF16), w2_ref[f * tf:(f + 1) * tf, :])
    o_ref[...] = acc


def _mlp(x2, nw, w1, w2, tm=512, tf=1024):
    T = x2.shape[0]
    D = D_MODEL
    return pl.pallas_call(
        functools.partial(_mlp_kernel, tf=tf),
        grid=(T // tm,),
        in_specs=[
            pl.BlockSpec((tm, D), lambda i: (i, 0)),
            pl.BlockSpec((1, D), lambda i: (0, 0)),
            pl.BlockSpec((D, D_FF), lambda i: (0, 0)),
            pl.BlockSpec((D_FF, D), lambda i: (0, 0)),
        ],
        out_specs=pl.BlockSpec((tm, D), lambda i: (i, 0)),
        out_shape=jax.ShapeDtypeStruct((T, D), F32),
        compiler_params=_cparams(("parallel",)),
        name="mlp",
    )(x2, nw, w1, w2)


def _pad_rows(w, start, total):
    return jnp.zeros((total, w.shape[1]), w.dtype).at[start:start + w.shape[0]].set(w)


def kernel(x, norm_mix_w, w_in, ssd_conv_w, ssd_conv_b, ssd_dt_bias_f, ssd_dt_bias_b, ssd_a_log_f,
           ssd_a_log_b, ssd_d, ssd_norm_w, gla_a2_f, gla_a2_bias_f, gla_a2_b, gla_a2_bias_b,
           gla_norm_w, na_q_norm_w, na_k_norm_w, na_rpb, w_branch_ssd, w_branch_gla, w_branch_na,
           w_out, norm_mlp_w, w_ff1, w_ff2):
    B, S, D = x.shape
    T = B * S
    depth = w_in.shape[0]
    x2 = x.reshape(T, D)
    w_in_t = jnp.swapaxes(w_in, 1, 2)
    for l in range(depth):
        w_big, w_small = _permute_weight(w_in_t, l)
        hg = (SSD_GROUPS, SSD_HG)
        zeros_r = jnp.zeros((SSD_GROUPS, LANE - 2 * SSD_HG), F32)
        dt_bias = jnp.concatenate([ssd_dt_bias_f[l].reshape(hg), ssd_dt_bias_b[l].reshape(hg), zeros_r], axis=1)
        a_neg = jnp.concatenate([-jnp.exp(ssd_a_log_f[l]).reshape(hg), -jnp.exp(ssd_a_log_b[l]).reshape(hg),
                                 zeros_r], axis=1)
        prow3 = jnp.concatenate([dt_bias[:, None], a_neg[:, None], jnp.zeros((SSD_GROUPS, 6, LANE), F32)], axis=1)
        prow = prow3.reshape(SSD_GROUPS * 8, LANE)
        pcol = jnp.transpose(prow3[:, :, :SMALL_T_ROWS], (0, 2, 1))
        drow = jnp.repeat(ssd_d[l], SSD_HEAD_DIM)[None, :]
        hk = (GLA_HEADS, 1, GLA_DK)
        a2 = jnp.concatenate([_pad_rows(gla_a2_f[l], SM_GAF, LANE).reshape((LANE,) + hk),
                              _pad_rows(gla_a2_b[l], SM_GAB, LANE).reshape((LANE,) + hk)],
                             axis=2).reshape(LANE, -1).astype(BF16)
        a2_bias = jnp.concatenate([gla_a2_bias_f[l].reshape(hk), gla_a2_bias_b[l].reshape(hk)],
                                  axis=1).reshape(1, -1)
        na_bound, na_flag = _na_score_bound(na_rpb[l], na_q_norm_w[l], na_k_norm_w[l])
        table = _na_bias_rows(na_rpb[l] * LOG2E - na_bound[:, None, None])
        q_row = jnp.tile(na_q_norm_w[l] * (NA_HEAD_DIM ** -0.5 * LOG2E), NA_HEADS)
        k_row = jnp.tile(na_k_norm_w[l], NA_HEADS)
        qkw = jnp.concatenate([q_row[None], k_row[None], jnp.zeros((6, NA_W), F32)], axis=0)

        u2, us2, ust = _inproj(x2, norm_mix_w[l][None, :], w_big, w_small, qkw)
        u3 = u2.reshape(B, S, U_WIDTH)
        us3 = us2.reshape(B, S, SMALL_W)
        xbc = _conv(u3, ssd_conv_w[l], ssd_conv_b[l][None, :])
        y_ssd = _ssd(xbc, u3, us3, ust, prow, pcol, drow, ssd_norm_w[l][None, :])
        y_gla = _gla(u3, us3, a2, a2_bias, gla_norm_w[l][None, :])
        y_na = _na(u3, table, na_flag)
        x2 = _merge(x2, y_ssd.reshape(T, -1), y_gla.reshape(T, -1), y_na.reshape(T, -1), u2,
                    w_branch_ssd[l].astype(BF16), w_branch_gla[l].astype(BF16),
                    w_branch_na[l].astype(BF16), w_out[l].astype(BF16))
        x2 = _mlp(x2, norm_mlp_w[l][None, :], w_ff1[l].astype(BF16), w_ff2[l].astype(BF16))
    return x2.reshape(B, S, D)
```

```python
import functools
import math

import jax
import jax.numpy as jnp
import numpy as np
from jax import lax
from jax.experimental import pallas as pl
from jax.experimental.pallas import tpu as pltpu

F32 = jnp.float32
BF16 = jnp.bfloat16

EPS = 1e-6
D_MODEL = 1024
GRID_W = 64

SSD_HEADS = 16
SSD_HEAD_DIM = 64
SSD_D_INNER = 1024
SSD_GROUPS = 2
SSD_STATE = 128
SSD_CONV = 5
SSD_CONV_DIM = 1536
SSD_CHUNK = 128
SSD_CHUNKS_PER_STEP = 4
SSD_OUT_UNROLL = 16
SSD_GROUP_W = SSD_D_INNER // SSD_GROUPS
SSD_HG = SSD_HEADS // SSD_GROUPS

GLA_HEADS = 4
GLA_DK = 128
GLA_DV = 256
GLA_KEY_W = 512
GLA_VAL_W = 1024
GLA_GATE_RANK = 16
GLA_GATE_NORM = 16.0
GLA_CHUNK = 64
GLA_BLOCKS_PER_STEP = 4
GLA_OUT_UNROLL = 16

NA_HEADS = 16
NA_HEAD_DIM = 64
NA_W = 1024
NA_WIN_H = 8
NA_WIN_W = 16
NA_BATCH_PER_STEP = 4
NA_ROWS_PER_STEP = 32
NA_ROWS_PER_STEP_EXACT = 2
LOG2E = 1.4426950408889634
NA_BOUND_SLACK = 0.02
NA_MAX_BOUND_GAP = 90.0

N_BRANCH = 3
D_FF = 4096

IN_SIZES = (SSD_D_INNER, SSD_CONV_DIM, SSD_HEADS, SSD_HEADS,
            GLA_KEY_W, GLA_KEY_W, GLA_VAL_W, GLA_VAL_W, GLA_GATE_RANK, GLA_GATE_RANK,
            NA_W, NA_W, NA_W, N_BRANCH * D_MODEL)
_IN_OFF = np.concatenate([[0], np.cumsum(IN_SIZES)])
(_O_Z, _O_XBC, _O_DTF, _O_DTB, _O_GQ, _O_GK, _O_GV, _O_GG, _O_GAF, _O_GAB,
 _O_NQ, _O_NK, _O_NV, _O_GATE) = [int(v) for v in _IN_OFF[:-1]]

U_NQ = 0
U_NK = 1024
U_NV = 2048
U_GATE = 3072
U_Z = 6144
U_XBC = 7168
U_GQ = 8704
U_GK = 9216
U_GV = 9728
U_GG = 10752
U_WIDTH = 11776
INPROJ_COL_CHUNK = 1024
LANE = 128
VMEM_LIMIT = 56 * 1024 * 1024

SMALL_W = SSD_GROUPS * LANE
SM_DTF, SM_DTB, SM_GAF, SM_GAB = 0, 8, 16, 32
SMALL_T_ROWS = 2 * SSD_HG

_BIG_SEGS = ((_O_NQ, 6144), (_O_Z, 2560), (_O_GQ, 3072))


def _cparams(sem, vmem=VMEM_LIMIT):
    return pltpu.CompilerParams(dimension_semantics=sem, vmem_limit_bytes=vmem)


def _sigmoid(x):
    return 1.0 / (1.0 + jnp.exp(-x))


def _silu(x):
    return x * _sigmoid(x)


def _softplus(x):
    return jnp.maximum(x, 0.0) + jnp.log1p(jnp.exp(-jnp.abs(x)))


def _log2_sigmoid(x):
    t = x * (-LOG2E)
    return -(jnp.maximum(t, 0.0) + jnp.log2(1.0 + jnp.exp2(-jnp.abs(t))))


def _nt(a, b):
    return lax.dot_general(a, b, (((1,), (1,)), ((), ())), preferred_element_type=F32)


def _tn(a, b):
    return lax.dot_general(a, b, (((0,), (0,)), ((), ())), preferred_element_type=F32)


def _dot(a, b):
    return jnp.dot(a, b, preferred_element_type=F32)


def _iota2(shape, dim):
    return lax.broadcasted_iota(jnp.int32, shape, dim)


def _wperm_kernel(wt_ref, o_ref, ws_ref):
    off = 0
    for a, n in _BIG_SEGS:
        o_ref[off:off + n, :] = wt_ref[0, a:a + n, :].astype(BF16)
        off += n
    ws_ref[...] = jnp.zeros_like(ws_ref)
    for g in range(SSD_GROUPS):
        base, h0 = g * LANE, g * SSD_HG
        ws_ref[base + SM_DTF:base + SM_DTF + SSD_HG, :] = wt_ref[0, _O_DTF + h0:_O_DTF + h0 + SSD_HG, :]
        ws_ref[base + SM_DTB:base + SM_DTB + SSD_HG, :] = wt_ref[0, _O_DTB + h0:_O_DTB + h0 + SSD_HG, :]
    ws_ref[SM_GAF:SM_GAF + GLA_GATE_RANK, :] = wt_ref[0, _O_GAF:_O_GAF + GLA_GATE_RANK, :]
    ws_ref[SM_GAB:SM_GAB + GLA_GATE_RANK, :] = wt_ref[0, _O_GAB:_O_GAB + GLA_GATE_RANK, :]


def _permute_weight(w_t_all, layer, tc=256):
    _, n_in, d = w_t_all.shape
    return pl.pallas_call(
        _wperm_kernel,
        grid=(d // tc,),
        in_specs=[pl.BlockSpec((1, n_in, tc), lambda i: (layer, 0, i))],
        out_specs=[
            pl.BlockSpec((U_WIDTH, tc), lambda i: (0, i)),
            pl.BlockSpec((SMALL_W, tc), lambda i: (0, i)),
        ],
        out_shape=[
            jax.ShapeDtypeStruct((U_WIDTH, d), BF16),
            jax.ShapeDtypeStruct((SMALL_W, d), F32),
        ],
        compiler_params=_cparams(("parallel",)),
        name="w_permute",
    )(w_t_all)


def _head_rms(r, w_row):
    G = 2 * LANE
    er = _iota2((G, G), 0) // NA_HEAD_DIM
    ec = _iota2((G, G), 1) // NA_HEAD_DIM
    e_blk = (er == ec).astype(BF16)
    outs = []
    for a in range(0, r.shape[1], G):
        x = r[:, a:a + G]
        ms = _dot((x * x).astype(BF16), e_blk) * (1.0 / NA_HEAD_DIM)
        outs.append(x * lax.rsqrt(ms + EPS))
    return jnp.concatenate(outs, axis=1) * w_row


def _inproj_kernel(x_ref, nw_ref, w_ref, ws_ref, qkw_ref, u_ref, us_ref, ust_ref, h_ref):
    tn = u_ref.shape[1]
    chunks = [(a, min(a + INPROJ_COL_CHUNK, tn)) for a in range(0, tn, INPROJ_COL_CHUNK)]

    @pl.when(pl.program_id(1) == 0)
    def _():
        x = x_ref[...]
        ms = jnp.mean(x * x, axis=-1, keepdims=True)
        h = (x * lax.rsqrt(ms + EPS) * nw_ref[...]).astype(BF16)
        h_ref[...] = h
        ws = ws_ref[...].astype(BF16)
        us_ref[...] = _nt(h, ws)
        dt_rows = jnp.concatenate([ws[g * LANE:g * LANE + SMALL_T_ROWS] for g in range(SSD_GROUPS)], axis=0)
        ust_ref[...] = _nt(dt_rows, h)
        for c, (a, b) in enumerate(chunks):
            r = _nt(h, w_ref[a:b, :])
            if c < 2:
                r = _head_rms(r, qkw_ref[c:c + 1, :])
            u_ref[:, a:b] = r.astype(BF16)

    @pl.when(pl.program_id(1) != 0)
    def _():
        for a, b in chunks:
            u_ref[:, a:b] = _nt(h_ref[...], w_ref[a:b, :]).astype(BF16)


def _inproj(x2, nw, w_big, w_small, qkw, tm=1024, tn=U_WIDTH // 4):
    assert (U_NQ, U_NK) == (0, INPROJ_COL_CHUNK) and NA_W == INPROJ_COL_CHUNK
    T = x2.shape[0]
    return pl.pallas_call(
        _inproj_kernel,
        grid=(T // tm, U_WIDTH // tn),
        in_specs=[
            pl.BlockSpec((tm, D_MODEL), lambda i, j: (i, 0)),
            pl.BlockSpec((1, D_MODEL), lambda i, j: (0, 0)),
            pl.BlockSpec((tn, D_MODEL), lambda i, j: (j, 0)),
            pl.BlockSpec((SMALL_W, D_MODEL), lambda i, j: (0, 0)),
            pl.BlockSpec((8, NA_W), lambda i, j: (0, 0)),
        ],
        out_specs=[
            pl.BlockSpec((tm, tn), lambda i, j: (i, j)),
            pl.BlockSpec((tm, SMALL_W), lambda i, j: (i, 0)),
            pl.BlockSpec((SSD_GROUPS * SMALL_T_ROWS, tm), lambda i, j: (0, i)),
        ],
        out_shape=[
            jax.ShapeDtypeStruct((T, U_WIDTH), BF16),
            jax.ShapeDtypeStruct((T, SMALL_W), F32),
            jax.ShapeDtypeStruct((SSD_GROUPS * SMALL_T_ROWS, T), F32),
        ],
        scratch_shapes=[pltpu.VMEM((tm, D_MODEL), BF16)],
        compiler_params=_cparams(("parallel", "arbitrary")),
        name="inproj",
    )(x2, nw, w_big, w_small, qkw)


def _conv_kernel(u_ref, w_ref, b_ref, o_ref):
    x = u_ref[0].astype(F32)
    S = x.shape[0]
    row = _iota2(x.shape, 0)
    acc = jnp.zeros_like(x) + b_ref[...]
    pad = SSD_CONV // 2
    for k in range(SSD_CONV):
        d = k - pad
        if d == 0:
            xs = x
        else:
            xs = pltpu.roll(x, (-d) % S, axis=0)
            valid = (row + d >= 0) & (row + d < S)
            xs = jnp.where(valid, xs, 0.0)
        acc = acc + w_ref[k:k + 1, :] * xs
    o_ref[0] = _silu(acc).astype(BF16)


def _conv(u3, conv_w, conv_b, tc=256):
    B, S, _ = u3.shape
    nblk = SSD_CONV_DIM // tc
    off = U_XBC // tc
    return pl.pallas_call(
        _conv_kernel,
        grid=(B, nblk),
        in_specs=[
            pl.BlockSpec((1, S, tc), lambda b, c: (b, 0, off + c)),
            pl.BlockSpec((SSD_CONV, tc), lambda b, c: (0, c)),
            pl.BlockSpec((1, tc), lambda b, c: (0, c)),
        ],
        out_specs=pl.BlockSpec((1, S, tc), lambda b, c: (b, 0, c)),
        out_shape=jax.ShapeDtypeStruct((B, S, SSD_CONV_DIM), BF16),
        compiler_params=_cparams(("parallel", "parallel")),
        name="ssd_conv",
    )(u3, conv_w, conv_b)


def _split_hi_lo(v):
    hi = v.astype(BF16)
    lo = (v - hi.astype(F32)).astype(BF16)
    return jnp.concatenate([hi, lo], axis=1)


def _split3(v, axis):
    hi = v.astype(BF16)
    r1 = v - hi.astype(F32)
    mid = r1.astype(BF16)
    lo = (r1 - mid.astype(F32)).astype(BF16)
    return jnp.concatenate([hi, mid, lo], axis=axis)


def _ssd_kernel(x_ref, b_ref, c_ref, z_ref, us_ref, ust_ref, prow_ref, pcol_ref, drow_ref, nw_ref,
                o_ref, acc_ref, cumc_ref, cumr_ref, wdt_ref, dec_ref, xs_ref, decx_ref, sst_ref, st_ref):
    L = SSD_CHUNK
    S = x_ref.shape[1]
    nc = S // L
    W = SSD_GROUP_W
    HG = SSD_HG
    R = 2 * HG

    ii = _iota2((L, L), 0)
    jj = _iota2((L, L), 1)
    tril = ii >= jj
    triu = jj >= ii
    tril_b = tril.astype(BF16)
    triu_b = triu.astype(BF16)

    er = _iota2((LANE, 2 * W), 0)
    ec = _iota2((LANE, 2 * W), 1)
    e = (er == jnp.where(ec < W, ec // SSD_HEAD_DIM, (ec - W) // SSD_HEAD_DIM + HG)).astype(BF16)
    e2 = jnp.concatenate([e, e], axis=0)

    bias_row = prow_ref[0:1, :]
    a_row = prow_ref[1:2, :]
    bias_col = pcol_ref[0, :, 0:1]
    a_col = pcol_ref[0, :, 1:2]

    lane_fwd = _iota2((L, LANE), 1) < HG
    row_fwd = _iota2((R, L), 0) < HG
    lane_half = _iota2((L, LANE), 1) < SSD_HEAD_DIM

    def decay_sums(c):
        rs = pl.ds(pl.multiple_of(c * L, L), L)
        dt_c = _softplus(us_ref[0, rs, :] + bias_row)
        a_c = dt_c * a_row
        dt_r = _softplus(ust_ref[:, rs] + bias_col)
        a_r = dt_r * a_col
        return dt_c, a_c, _dot(tril_b, _split3(a_c, 1)), dt_r, a_r, _dot(_split3(a_r, 0), triu_b)

    def decay_store(c, sums):
        rs = pl.ds(pl.multiple_of(c * L, L), L)
        dt_c, a_c, pp, dt_r, a_r, pr = sums
        p_c = pp[:, 0:LANE] + pp[:, LANE:2 * LANE] + pp[:, 2 * LANE:3 * LANE]
        tot_c = p_c[L - 1:L, :]
        cum_c = jnp.where(lane_fwd, p_c, tot_c - p_c + a_c)
        cumc_ref[rs, :] = cum_c * LOG2E
        wdt_ref[rs, :] = jnp.exp(tot_c - cum_c) * dt_c
        dec_ref[c] = jnp.broadcast_to(jnp.exp(tot_c), (16, LANE))
        p_r = pr[0:R] + pr[R:2 * R] + pr[2 * R:3 * R]
        tot_r = p_r[:, L - 1:L]
        cum_r = jnp.where(row_fwd, p_r, tot_r - p_r + a_r)
        cumr_ref[:, rs] = (cum_r - jnp.log(dt_r)) * LOG2E

    def intra_chunk(c):
        rs = pl.ds(pl.multiple_of(c * L, L), L)
        x_c = x_ref[0, rs, :]
        b_c = b_ref[0, rs, :]
        cb = _nt(c_ref[0, rs, :], b_c)
        ex = _dot(_split_hi_lo(jnp.concatenate([wdt_ref[rs, :], dec_ref[c]], axis=0)), e2)
        decx_ref[c] = ex[L:L + 8]
        xf = x_c.astype(F32)
        xw = jnp.concatenate([xf * ex[0:L, 0:W], xf * ex[0:L, W:2 * W]], axis=1).astype(BF16)
        xs_ref[c] = _tn(b_c, xw)
        cum_c = cumc_ref[rs, :]
        cum_r = cumr_ref[:, rs]
        for hp in range(HG // 2):
            ms = []
            for hh in range(2):
                hf = 2 * hp + hh
                hb = HG + 2 * hp + hh
                decf = jnp.exp2(jnp.where(tril, cum_c[:, hf:hf + 1] - cum_r[hf:hf + 1, :], -jnp.inf))
                decb = jnp.exp2(jnp.where(triu, cum_c[:, hb:hb + 1] - cum_r[hb:hb + 1, :], -jnp.inf))
                ms.append((cb * (decf + decb)).astype(BF16))
            m2 = jnp.concatenate(ms, axis=1)
            xp = x_c[:, hp * LANE:(hp + 1) * LANE]
            xz = jnp.zeros_like(xp)
            x2 = jnp.concatenate([jnp.where(lane_half, xp, xz), jnp.where(lane_half, xz, xp)], axis=0)
            acc_ref[rs, hp * LANE:(hp + 1) * LANE] = _dot(m2, x2)

    G = SSD_CHUNKS_PER_STEP
    n_steps = nc // G

    def fused_body(i, carry):
        sums = [decay_sums(G * (i + 1) + j) for j in range(G)]
        for j in range(G):
            intra_chunk(G * i + j)
        for j in range(G):
            decay_store(G * (i + 1) + j, sums[j])
        return carry

    for j in range(G):
        decay_store(j, decay_sums(j))
    lax.fori_loop(0, n_steps - 1, fused_body, 0, unroll=True)
    for j in range(G):
        intra_chunk(G * (n_steps - 1) + j)

    st_ref[...] = jnp.zeros_like(st_ref)

    def state_body(i, carry):
        for d, c in enumerate((i, nc - 1 - i)):
            ls = slice(d * W, (d + 1) * W)
            s_prev = st_ref[:, ls]
            sst_ref[c, :, ls] = s_prev.astype(BF16)
            st_ref[:, ls] = s_prev * decx_ref[c, 0:1, ls] + xs_ref[c, :, ls]
        return carry

    lax.fori_loop(0, nc, state_body, 0, unroll=True)

    def out_body(c, carry):
        rs = pl.ds(pl.multiple_of(c * L, L), L)
        yo = _dot(c_ref[0, rs, :], sst_ref[c])
        sc = _dot(_split_hi_lo(jnp.exp2(cumc_ref[rs, :])), e2)
        xf = x_ref[0, rs, :].astype(F32)
        y = acc_ref[rs, :] + yo[:, 0:W] * sc[:, 0:W] + yo[:, W:2 * W] * sc[:, W:2 * W] + xf * drow_ref[...]
        y = y * _silu(z_ref[0, rs, :].astype(F32))
        ms = jnp.mean(y * y, axis=-1, keepdims=True)
        o_ref[0, rs, :] = (y * lax.rsqrt(ms + EPS) * nw_ref[...]).astype(BF16)
        return carry

    lax.fori_loop(0, nc, out_body, 0, unroll=SSD_OUT_UNROLL)


def _ssd(xbc, u3, us3, ust, prow, pcol, drow, nw):
    B, S, _ = xbc.shape
    W = SSD_GROUP_W
    N = SSD_STATE
    nc = S // SSD_CHUNK
    return pl.pallas_call(
        _ssd_kernel,
        grid=(B, SSD_GROUPS),
        in_specs=[
            pl.BlockSpec((1, S, W), lambda b, g: (b, 0, g)),
            pl.BlockSpec((1, S, N), lambda b, g: (b, 0, SSD_D_INNER // N + g)),
            pl.BlockSpec((1, S, N), lambda b, g: (b, 0, SSD_D_INNER // N + SSD_GROUPS + g)),
            pl.BlockSpec((1, S, W), lambda b, g: (b, 0, U_Z // W + g)),
            pl.BlockSpec((1, S, LANE), lambda b, g: (b, 0, g)),
            pl.BlockSpec((SMALL_T_ROWS, S), lambda b, g: (g, b)),
            pl.BlockSpec((8, LANE), lambda b, g: (g, 0)),
            pl.BlockSpec((1, SMALL_T_ROWS, 8), lambda b, g: (g, 0, 0)),
            pl.BlockSpec((1, W), lambda b, g: (0, g)),
            pl.BlockSpec((1, W), lambda b, g: (0, g)),
        ],
        out_specs=pl.BlockSpec((1, S, W), lambda b, g: (b, 0, g)),
        out_shape=jax.ShapeDtypeStruct((B, S, SSD_D_INNER), BF16),
        scratch_shapes=[
            pltpu.VMEM((S, W), F32),
            pltpu.VMEM((S, LANE), F32),
            pltpu.VMEM((SMALL_T_ROWS, S), F32),
            pltpu.VMEM((S, LANE), F32),
            pltpu.VMEM((nc, 16, LANE), F32),
            pltpu.VMEM((nc, N, 2 * W), F32),
            pltpu.VMEM((nc, 8, 2 * W), F32),
            pltpu.VMEM((nc, N, 2 * W), BF16),
            pltpu.VMEM((N, 2 * W), F32),
        ],
        compiler_params=_cparams(("parallel", "parallel")),
        name="ssd_scan",
    )(xbc, xbc, xbc, u3, us3, ust, prow, pcol, drow, nw)


def _gla_kernel(q_ref, k_ref, v_ref, gg_ref, us_ref, a2_ref, bias_ref, nw_ref,
                o_ref, acc_ref, g_ref, qd_ref, kd_ref, kdp_ref, qcat_ref, x_ref, dec_ref, sst_ref):
    L = GLA_CHUNK
    BL = 2 * L
    DK = GLA_DK
    S = q_ref.shape[1]
    nb = S // BL
    scale = DK ** -0.5

    ii = _iota2((BL, BL), 0)
    jj = _iota2((BL, BL), 1)
    same = (ii // L) == (jj // L)
    masks = (same & (ii >= jj), same & (jj >= ii))
    tri2 = masks[0].astype(BF16)
    par_row = _iota2((BL, DK), 0) // L

    ga = us_ref[0].astype(BF16)
    g_ref[...] = _log2_sigmoid(_dot(ga, a2_ref[...]) + bias_ref[...]) * (1.0 / GLA_GATE_NORM)

    def decay_sums(i):
        g = g_ref[pl.ds(pl.multiple_of(i * BL, BL), BL), :]
        hi = g.astype(BF16)
        r1 = g - hi.astype(F32)
        mid = r1.astype(BF16)
        lo = (r1 - mid.astype(F32)).astype(BF16)
        return g, _dot(tri2, jnp.concatenate([hi, mid, lo], axis=1))

    def decay_block(i, sums):
        rs = pl.ds(pl.multiple_of(i * BL, BL), BL)
        g, pp = sums
        p = pp[:, 0:2 * DK] + pp[:, 2 * DK:4 * DK] + pp[:, 4 * DK:6 * DK]
        q_c = q_ref[0, rs, :].astype(F32) * scale
        k_c = k_ref[0, rs, :].astype(F32)
        zero = jnp.zeros((BL, DK), BF16)
        for d in range(2):
            p_d = p[:, d * DK:(d + 1) * DK]
            tot = jnp.where(par_row == 0, p_d[L - 1:L, :], p_d[BL - 1:BL, :])
            b = p_d if d == 0 else tot - p_d + g[:, DK:]
            qd = (q_c * jnp.exp2(b)).astype(BF16)
            kdec = (k_c * jnp.exp2(tot - b)).astype(BF16)
            qd_ref[d, rs, :] = qd
            kd_ref[d, rs, :] = (k_c * jnp.exp2(-b)).astype(BF16)
            for par in range(2):
                sel = par_row == par
                kdp_ref[d, rs, par * DK:(par + 1) * DK] = jnp.where(sel, kdec, zero)
                qcat_ref[rs, (2 * d + par) * DK:(2 * d + par + 1) * DK] = jnp.where(sel, qd, zero)
                last = (par + 1) * L - 1
                dec_ref[d, 2 * i + par] = jnp.broadcast_to(jnp.exp2(p_d[last:last + 1, :]), (8, DK))

    G = GLA_BLOCKS_PER_STEP
    n_groups = nb // G

    def intra_group(i):
        blks = [G * i + j for j in range(G)]
        rss = [pl.ds(pl.multiple_of(b * BL, BL), BL) for b in blks]
        vs = [v_ref[0, rs, :] for rs in rss]
        atts = [[_nt(qd_ref[d, rs, :], kd_ref[d, rs, :]) for d in range(2)] for rs in rss]
        for j, b in enumerate(blks):
            for d in range(2):
                x_ref[d, b] = _tn(vs[j], kdp_ref[d, rss[j], :])
        for j in range(G):
            att = jnp.where(masks[0], atts[j][0], 0.0) + jnp.where(masks[1], atts[j][1], 0.0)
            acc_ref[rss[j], :] = _dot(att.astype(BF16), vs[j])

    def fused_body(i, carry):
        sums = [decay_sums(G * (i + 1) + j) for j in range(G)]
        intra_group(i)
        for j in range(G):
            decay_block(G * (i + 1) + j, sums[j])
        return carry

    for j in range(G):
        decay_block(j, decay_sums(j))
    lax.fori_loop(0, n_groups - 1, fused_body, 0, unroll=True)
    intra_group(n_groups - 1)

    def state_body(d):
        def body(i, s):
            b = i if d == 0 else nb - 1 - i
            for par in ((0, 1) if d == 0 else (1, 0)):
                lane0 = (2 * d + par) * DK
                sst_ref[b, :, lane0:lane0 + DK] = s.astype(BF16)
                s = s * dec_ref[d, 2 * b + par, 0:1, :] + x_ref[d, b, :, par * DK:(par + 1) * DK]
            return s
        return body

    for d in range(2):
        lax.fori_loop(0, nb, state_body(d), jnp.zeros((GLA_DV, DK), F32), unroll=True)

    def out_body(b, carry):
        rs = pl.ds(pl.multiple_of(b * BL, BL), BL)
        o = acc_ref[rs, :] + _nt(qcat_ref[rs, :], sst_ref[b])
        ms = jnp.mean(o * o, axis=-1, keepdims=True)
        o = o * lax.rsqrt(ms + EPS) * nw_ref[...]
        o_ref[0, rs, :] = (o * _silu(gg_ref[0, rs, :].astype(F32))).astype(BF16)
        return carry

    lax.fori_loop(0, nb, out_body, 0, unroll=GLA_OUT_UNROLL)


def _gla(u3, us3, a2, bias, nw):
    B, S, _ = u3.shape
    DK, DV = GLA_DK, GLA_DV
    nc = S // GLA_CHUNK
    nb = nc // 2
    return pl.pallas_call(
        _gla_kernel,
        grid=(B, GLA_HEADS),
        in_specs=[
            pl.BlockSpec((1, S, DK), lambda b, h: (b, 0, U_GQ // DK + h)),
            pl.BlockSpec((1, S, DK), lambda b, h: (b, 0, U_GK // DK + h)),
            pl.BlockSpec((1, S, DV), lambda b, h: (b, 0, U_GV // DV + h)),
            pl.BlockSpec((1, S, DV), lambda b, h: (b, 0, U_GG // DV + h)),
            pl.BlockSpec((1, S, LANE), lambda b, h: (b, 0, 0)),
            pl.BlockSpec((LANE, 2 * DK), lambda b, h: (0, h)),
            pl.BlockSpec((1, 2 * DK), lambda b, h: (0, h)),
            pl.BlockSpec((1, DV), lambda b, h: (0, 0)),
        ],
        out_specs=pl.BlockSpec((1, S, DV), lambda b, h: (b, 0, h)),
        out_shape=jax.ShapeDtypeStruct((B, S, GLA_VAL_W), BF16),
        scratch_shapes=[
            pltpu.VMEM((S, DV), F32),
            pltpu.VMEM((S, 2 * DK), F32),
            pltpu.VMEM((2, S, DK), BF16),
            pltpu.VMEM((2, S, DK), BF16),
            pltpu.VMEM((2, S, 2 * DK), BF16),
            pltpu.VMEM((S, 4 * DK), BF16),
            pltpu.VMEM((2, nb, DV, 2 * DK), F32),
            pltpu.VMEM((2, nc, 8, DK), F32),
            pltpu.VMEM((nb, DV, 4 * DK), BF16),
        ],
        compiler_params=_cparams(("parallel", "parallel")),
        name="gla_scan",
    )(u3, u3, u3, u3, us3, a2, bias, nw)


def _na_bias_rows(rpb):
    H, R, C = rpb.shape
    n_pos = GRID_W - NA_WIN_W
    n_neg = GRID_W - NA_WIN_W + 2
    ext = jnp.concatenate([rpb[:, :, NA_WIN_W - 1:], jnp.repeat(rpb[:, :, C - 1:], n_pos, axis=2),
                           jnp.repeat(rpb[:, :, 0:1], n_neg, axis=2), rpb[:, :, 1:NA_WIN_W - 1]], axis=2)
    return jnp.pad(ext, ((0, 0), (0, 1), (0, 0))).reshape(H // 2, 2, R + 1, 2 * GRID_W)


def _na_kernel(flag_ref, q_ref, k_ref, v_ref, ext_ref, o_ref, vx_ref, tab_ref):
    NB, S, _ = q_ref.shape
    rows = S // GRID_W
    total = NB * rows
    win_h = NA_WIN_H
    nk = win_h * GRID_W

    vx_ref[:, :, 0:LANE] = v_ref[...]
    vx_ref[:, :, LANE:2 * LANE] = jnp.ones((NB, S, LANE), BF16)
    bound_ok = flag_ref[0] != 0

    first_q = _iota2((GRID_W, LANE), 1) < NA_HEAD_DIM

    @pl.when(pl.program_id(1) == 0)
    def _():
        q_col = _iota2((GRID_W, 2 * GRID_W), 0)
        k_col = _iota2((GRID_W, 2 * GRID_W), 1) % GRID_W
        w_start = jnp.clip(q_col - NA_WIN_W // 2, 0, GRID_W - NA_WIN_W)
        in_window = (k_col >= w_start) & (k_col < w_start + NA_WIN_W)
        low = _iota2((GRID_W, 2 * GRID_W), 1) < GRID_W
        for hd in range(2):
            def skewed(rr, shift):
                row = jnp.broadcast_to(ext_ref[0, hd, rr:rr + 1, :], (GRID_W, 2 * GRID_W))
                return pltpu.roll(row, shift, axis=1, stride=1, stride_axis=0)

            for e in range(2 * NA_WIN_H - 2):
                t = jnp.where(low, skewed(e, 0), skewed(e + 1, GRID_W))
                tab_ref[e, hd * GRID_W:(hd + 1) * GRID_W, :] = jnp.where(in_window, t, -jnp.inf)

    def locate(r):
        bi = r // rows
        rl = r - bi * rows
        r0 = jnp.clip(rl - win_h // 2, 0, rows - win_h)
        return bi, pl.multiple_of(rl * GRID_W, GRID_W), rl - r0, pl.multiple_of(r0 * GRID_W, GRID_W)

    def scores(r):
        bi, q0, delta, k0 = locate(r)
        q = q_ref[bi, pl.ds(q0, GRID_W), :]
        zero = jnp.zeros_like(q)
        qs = jnp.concatenate([jnp.where(first_q, q, zero), jnp.where(first_q, zero, q)], axis=0)
        rr0 = (win_h - 1) - delta
        bias = jnp.concatenate([tab_ref[rr0 + w] for w in range(0, win_h, 2)], axis=1)
        return _nt(qs, k_ref[bi, pl.ds(k0, nk), :]) + bias

    def probs_exact(s):
        return jnp.exp2(s - jnp.max(s, axis=-1, keepdims=True)).astype(BF16)

    def probs_bounded(r):
        return jnp.exp2(scores(r)).astype(BF16)

    def attend(r, p):
        bi, q0, _, k0 = locate(r)
        ox = _dot(p, vx_ref[bi, pl.ds(k0, nk), :])
        o = ox[:, 0:LANE] / ox[:, LANE:2 * LANE]
        o_ref[bi, pl.ds(q0, GRID_W), :] = jnp.where(first_q, o[:GRID_W], o[GRID_W:]).astype(BF16)

    @pl.when(bound_ok)
    def _():
        U = NA_ROWS_PER_STEP

        def row_body(i, p_prev):
            r = i * U
            for j in range(U):
                attend(r - U + j, p_prev[j])
            return tuple(probs_bounded(r + j) for j in range(U))

        p_last = lax.fori_loop(1, total // U, row_body, tuple(probs_bounded(j) for j in range(U)))
        for j in range(U):
            attend(total - U + j, p_last[j])

    @pl.when(jnp.logical_not(bound_ok))
    def _():
        U = NA_ROWS_PER_STEP_EXACT

        def row_body(i, carry):
            s_cur, p_prev = carry
            r = i * U
            for j in range(U):
                attend(r - U + j, p_prev[j])
            p = tuple(probs_exact(s) for s in s_cur)
            s_next = tuple(scores(jnp.minimum(r + U + j, total - 1)) for j in range(U))
            return s_next, p

        p0 = tuple(probs_exact(scores(j)) for j in range(U))
        s1 = tuple(scores(U + j) for j in range(U))
        _, p_last = lax.fori_loop(1, total // U, row_body, (s1, p0))
        for j in range(U):
            attend(total - U + j, p_last[j])


def _na_score_bound(rpb, q_norm_w, k_norm_w):
    H = rpb.shape[0]
    qk = (NA_HEAD_DIM ** 0.5 * LOG2E * (1.0 + NA_BOUND_SLACK)) * jnp.max(jnp.abs(q_norm_w)) * jnp.max(jnp.abs(k_norm_w))
    b_max = jnp.max(rpb.reshape(H, -1), axis=1) * LOG2E
    b_self = rpb[:, NA_WIN_H - 1, NA_WIN_W - 1] * LOG2E
    bound = qk + b_max
    flag = jnp.all(bound - (b_self - qk) <= NA_MAX_BOUND_GAP)
    return bound, flag.astype(jnp.int32).reshape(1)


def _na(u3, bias_rows, flag):
    B, S, _ = u3.shape
    assert S // GRID_W >= NA_WIN_H and NA_WIN_H % 2 == 0 and 2 * GRID_W == LANE
    nb = math.gcd(B, NA_BATCH_PER_STEP)
    assert (nb * (S // GRID_W)) % NA_ROWS_PER_STEP == 0
    return pl.pallas_call(
        _na_kernel,
        grid=(NA_HEADS // 2, B // nb),
        in_specs=[
            pl.BlockSpec(memory_space=pltpu.SMEM),
            pl.BlockSpec((nb, S, LANE), lambda h, b: (b, 0, U_NQ // LANE + h)),
            pl.BlockSpec((nb, S, LANE), lambda h, b: (b, 0, U_NK // LANE + h)),
            pl.BlockSpec((nb, S, LANE), lambda h, b: (b, 0, U_NV // LANE + h)),
            pl.BlockSpec((1, 2, 2 * NA_WIN_H, 2 * GRID_W), lambda h, b: (h, 0, 0, 0)),
        ],
        out_specs=pl.BlockSpec((nb, S, LANE), lambda h, b: (b, 0, h)),
        out_shape=jax.ShapeDtypeStruct((B, S, NA_W), BF16),
        scratch_shapes=[
            pltpu.VMEM((nb, S, 2 * LANE), BF16),
            pltpu.VMEM((2 * NA_WIN_H - 2, 2 * GRID_W, 2 * GRID_W), F32),
        ],
        compiler_params=_cparams(("parallel", "arbitrary")),
        name="na_attn",
    )(flag, u3, u3, u3, bias_rows)


def _merge_kernel(x_ref, ys_ref, yg_ref, yn_ref, gate_ref, ws_ref, wg_ref, wn_ref, wo_ref, o_ref):
    D = D_MODEL
    mixed = _sigmoid(gate_ref[:, 0:D].astype(F32)) * _dot(ys_ref[...], ws_ref[...])
    mixed += _sigmoid(gate_ref[:, D:2 * D].astype(F32)) * _dot(yg_ref[...], wg_ref[...])
    mixed += _sigmoid(gate_ref[:, 2 * D:3 * D].astype(F32)) * _dot(yn_ref[...], wn_ref[...])
    o_ref[...] = x_ref[...] + _dot(mixed.astype(BF16), wo_ref[...])


def _merge(x2, ys, yg, yn, u2, ws, wg, wn, wo, tm=512):
    T = x2.shape[0]
    D = D_MODEL
    row = lambda i: (i, 0)
    fixed = lambda i: (0, 0)
    return pl.pallas_call(
        _merge_kernel,
        grid=(T // tm,),
        in_specs=[
            pl.BlockSpec((tm, D), row),
            pl.BlockSpec((tm, D), row),
            pl.BlockSpec((tm, D), row),
            pl.BlockSpec((tm, D), row),
            pl.BlockSpec((tm, N_BRANCH * D), lambda i: (i, U_GATE // (N_BRANCH * D))),
            pl.BlockSpec((D, D), fixed),
            pl.BlockSpec((D, D), fixed),
            pl.BlockSpec((D, D), fixed),
            pl.BlockSpec((D, D), fixed),
        ],
        out_specs=pl.BlockSpec((tm, D), row),
        out_shape=jax.ShapeDtypeStruct((T, D), F32),
        compiler_params=_cparams(("parallel",)),
        name="merge",
    )(x2, ys, yg, yn, u2, ws, wg, wn, wo)


def _mlp_kernel(x_ref, nw_ref, w1_ref, w2_ref, o_ref, *, tf):
    x = x_ref[...]
    ms = jnp.mean(x * x, axis=-1, keepdims=True)
    h = (x * lax.rsqrt(ms + EPS) * nw_ref[...]).astype(BF16)
    acc = x
    for f in range(D_FF // tf):
        a = jnp.maximum(_dot(h, w1_ref[:, f * tf:(f + 1) * tf]), 0.0)
        acc = acc + _dot((a * a).astype(BF16), w2_ref[f * tf:(f + 1) * tf, :])
    o_ref[...] = acc


def _mlp(x2, nw, w1, w2, tm=512, tf=1024):
    T = x2.shape[0]
    D = D_MODEL
    return pl.pallas_call(
        functools.partial(_mlp_kernel, tf=tf),
        grid=(T // tm,),
        in_specs=[
            pl.BlockSpec((tm, D), lambda i: (i, 0)),
            pl.BlockSpec((1, D), lambda i: (0, 0)),
            pl.BlockSpec((D, D_FF), lambda i: (0, 0)),
            pl.BlockSpec((D_FF, D), lambda i: (0, 0)),
        ],
        out_specs=pl.BlockSpec((tm, D), lambda i: (i, 0)),
        out_shape=jax.ShapeDtypeStruct((T, D), F32),
        compiler_params=_cparams(("parallel",)),
        name="mlp",
    )(x2, nw, w1, w2)


def _pad_rows(w, start, total):
    return jnp.zeros((total, w.shape[1]), w.dtype).at[start:start + w.shape[0]].set(w)


def kernel(x, norm_mix_w, w_in, ssd_conv_w, ssd_conv_b, ssd_dt_bias_f, ssd_dt_bias_b, ssd_a_log_f,
           ssd_a_log_b, ssd_d, ssd_norm_w, gla_a2_f, gla_a2_bias_f, gla_a2_b, gla_a2_bias_b,
           gla_norm_w, na_q_norm_w, na_k_norm_w, na_rpb, w_branch_ssd, w_branch_gla, w_branch_na,
           w_out, norm_mlp_w, w_ff1, w_ff2):
    B, S, D = x.shape
    T = B * S
    depth = w_in.shape[0]
    x2 = x.reshape(T, D)
    w_in_t = jnp.swapaxes(w_in, 1, 2)
    for l in range(depth):
        w_big, w_small = _permute_weight(w_in_t, l)
        hg = (SSD_GROUPS, SSD_HG)
        zeros_r = jnp.zeros((SSD_GROUPS, LANE - 2 * SSD_HG), F32)
        dt_bias = jnp.concatenate([ssd_dt_bias_f[l].reshape(hg), ssd_dt_bias_b[l].reshape(hg), zeros_r], axis=1)
        a_neg = jnp.concatenate([-jnp.exp(ssd_a_log_f[l]).reshape(hg), -jnp.exp(ssd_a_log_b[l]).reshape(hg),
                                 zeros_r], axis=1)
        prow3 = jnp.concatenate([dt_bias[:, None], a_neg[:, None], jnp.zeros((SSD_GROUPS, 6, LANE), F32)], axis=1)
        prow = prow3.reshape(SSD_GROUPS * 8, LANE)
        pcol = jnp.transpose(prow3[:, :, :SMALL_T_ROWS], (0, 2, 1))
        drow = jnp.repeat(ssd_d[l], SSD_HEAD_DIM)[None, :]
        hk = (GLA_HEADS, 1, GLA_DK)
        a2 = jnp.concatenate([_pad_rows(gla_a2_f[l], SM_GAF, LANE).reshape((LANE,) + hk),
                              _pad_rows(gla_a2_b[l], SM_GAB, LANE).reshape((LANE,) + hk)],
                             axis=2).reshape(LANE, -1).astype(BF16)
        a2_bias = jnp.concatenate([gla_a2_bias_f[l].reshape(hk), gla_a2_bias_b[l].reshape(hk)],
                                  axis=1).reshape(1, -1)
        na_bound, na_flag = _na_score_bound(na_rpb[l], na_q_norm_w[l], na_k_norm_w[l])
        table = _na_bias_rows(na_rpb[l] * LOG2E - na_bound[:, None, None])
        q_row = jnp.tile(na_q_norm_w[l] * (NA_HEAD_DIM ** -0.5 * LOG2E), NA_HEADS)
        k_row = jnp.tile(na_k_norm_w[l], NA_HEADS)
        qkw = jnp.concatenate([q_row[None], k_row[None], jnp.zeros((6, NA_W), F32)], axis=0)

        u2, us2, ust = _inproj(x2, norm_mix_w[l][None, :], w_big, w_small, qkw)
        u3 = u2.reshape(B, S, U_WIDTH)
        us3 = us2.reshape(B, S, SMALL_W)
        xbc = _conv(u3, ssd_conv_w[l], ssd_conv_b[l][None, :])
        y_ssd = _ssd(xbc, u3, us3, ust, prow, pcol, drow, ssd_norm_w[l][None, :])
        y_gla = _gla(u3, us3, a2, a2_bias, gla_norm_w[l][None, :])
        y_na = _na(u3, table, na_flag)
        x2 = _merge(x2, y_ssd.reshape(T, -1), y_gla.reshape(T, -1), y_na.reshape(T, -1), u2,
                    w_branch_ssd[l].astype(BF16), w_branch_gla[l].astype(BF16),
                    w_branch_na[l].astype(BF16), w_out[l].astype(BF16))
        x2 = _mlp(x2, norm_mlp_w[l][None, :], w_ff1[l].astype(BF16), w_ff2[l].astype(BF16))
    return x2.reshape(B, S, D)
```

```python
import functools
import math

import jax
import jax.numpy as jnp
import numpy as np
from jax import lax
from jax.experimental import pallas as pl
from jax.experimental.pallas import tpu as pltpu

F32 = jnp.float32
BF16 = jnp.bfloat16

EPS = 1e-6
D_MODEL = 1024
GRID_W = 64

SSD_HEADS = 16
SSD_HEAD_DIM = 64
SSD_D_INNER = 1024
SSD_GROUPS = 2
SSD_STATE = 128
SSD_CONV = 5
SSD_CONV_DIM = 1536
SSD_CHUNK = 128
SSD_CHUNKS_PER_STEP = 4
SSD_OUT_UNROLL = 16
SSD_GROUP_W = SSD_D_INNER // SSD_GROUPS
SSD_HG = SSD_HEADS // SSD_GROUPS

GLA_HEADS = 4
GLA_DK = 128
GLA_DV = 256
GLA_KEY_W = 512
GLA_VAL_W = 1024
GLA_GATE_RANK = 16
GLA_GATE_NORM = 16.0
GLA_CHUNK = 64
GLA_BLOCKS_PER_STEP = 4
GLA_OUT_UNROLL = 16

NA_HEADS = 16
NA_HEAD_DIM = 64
NA_W = 1024
NA_WIN_H = 8
NA_WIN_W = 16
NA_BATCH_PER_STEP = 4
NA_ROWS_PER_STEP = 32
NA_ROWS_PER_STEP_EXACT = 2
LOG2E = 1.4426950408889634
NA_BOUND_SLACK = 0.02
NA_MAX_BOUND_GAP = 90.0

N_BRANCH = 3
D_FF = 4096

IN_SIZES = (SSD_D_INNER, SSD_CONV_DIM, SSD_HEADS, SSD_HEADS,
            GLA_KEY_W, GLA_KEY_W, GLA_VAL_W, GLA_VAL_W, GLA_GATE_RANK, GLA_GATE_RANK,
            NA_W, NA_W, NA_W, N_BRANCH * D_MODEL)
_IN_OFF = np.concatenate([[0], np.cumsum(IN_SIZES)])
(_O_Z, _O_XBC, _O_DTF, _O_DTB, _O_GQ, _O_GK, _O_GV, _O_GG, _O_GAF, _O_GAB,
 _O_NQ, _O_NK, _O_NV, _O_GATE) = [int(v) for v in _IN_OFF[:-1]]

U_NQ = 0
U_NK = 1024
U_NV = 2048
U_GATE = 3072
U_Z = 6144
U_XBC = 7168
U_GQ = 8704
U_GK = 9216
U_GV = 9728
U_GG = 10752
U_WIDTH = 11776
INPROJ_COL_CHUNK = 1024
LANE = 128
VMEM_LIMIT = 56 * 1024 * 1024

SMALL_W = SSD_GROUPS * LANE
SM_DTF, SM_DTB, SM_GAF, SM_GAB = 0, 8, 16, 32
SMALL_T_ROWS = 2 * SSD_HG

_BIG_SEGS = ((_O_NQ, 6144), (_O_Z, 2560), (_O_GQ, 3072))


def _cparams(sem, vmem=VMEM_LIMIT):
    return pltpu.CompilerParams(dimension_semantics=sem, vmem_limit_bytes=vmem)


def _sigmoid(x):
    return 1.0 / (1.0 + jnp.exp(-x))


def _silu(x):
    return x * _sigmoid(x)


def _softplus(x):
    return jnp.maximum(x, 0.0) + jnp.log1p(jnp.exp(-jnp.abs(x)))


def _log2_sigmoid(x):
    t = x * (-LOG2E)
    return -(jnp.maximum(t, 0.0) + jnp.log2(1.0 + jnp.exp2(-jnp.abs(t))))


def _nt(a, b):
    return lax.dot_general(a, b, (((1,), (1,)), ((), ())), preferred_element_type=F32)


def _tn(a, b):
    return lax.dot_general(a, b, (((0,), (0,)), ((), ())), preferred_element_type=F32)


def _dot(a, b):
    return jnp.dot(a, b, preferred_element_type=F32)


def _iota2(shape, dim):
    return lax.broadcasted_iota(jnp.int32, shape, dim)


def _wperm_kernel(wt_ref, o_ref, ws_ref):
    off = 0
    for a, n in _BIG_SEGS:
        o_ref[off:off + n, :] = wt_ref[0, a:a + n, :].astype(BF16)
        off += n
    ws_ref[...] = jnp.zeros_like(ws_ref)
    for g in range(SSD_GROUPS):
        base, h0 = g * LANE, g * SSD_HG
        ws_ref[base + SM_DTF:base + SM_DTF + SSD_HG, :] = wt_ref[0, _O_DTF + h0:_O_DTF + h0 + SSD_HG, :]
        ws_ref[base + SM_DTB:base + SM_DTB + SSD_HG, :] = wt_ref[0, _O_DTB + h0:_O_DTB + h0 + SSD_HG, :]
    ws_ref[SM_GAF:SM_GAF + GLA_GATE_RANK, :] = wt_ref[0, _O_GAF:_O_GAF + GLA_GATE_RANK, :]
    ws_ref[SM_GAB:SM_GAB + GLA_GATE_RANK, :] = wt_ref[0, _O_GAB:_O_GAB + GLA_GATE_RANK, :]


def _permute_weight(w_t_all, layer, tc=256):
    _, n_in, d = w_t_all.shape
    return pl.pallas_call(
        _wperm_kernel,
        grid=(d // tc,),
        in_specs=[pl.BlockSpec((1, n_in, tc), lambda i: (layer, 0, i))],
        out_specs=[
            pl.BlockSpec((U_WIDTH, tc), lambda i: (0, i)),
            pl.BlockSpec((SMALL_W, tc), lambda i: (0, i)),
        ],
        out_shape=[
            jax.ShapeDtypeStruct((U_WIDTH, d), BF16),
            jax.ShapeDtypeStruct((SMALL_W, d), F32),
        ],
        compiler_params=_cparams(("parallel",)),
        name="w_permute",
    )(w_t_all)


def _head_rms(r, w_row):
    G = 2 * LANE
    er = _iota2((G, G), 0) // NA_HEAD_DIM
    ec = _iota2((G, G), 1) // NA_HEAD_DIM
    e_blk = (er == ec).astype(BF16)
    outs = []
    for a in range(0, r.shape[1], G):
        x = r[:, a:a + G]
        ms = _dot((x * x).astype(BF16), e_blk) * (1.0 / NA_HEAD_DIM)
        outs.append(x * lax.rsqrt(ms + EPS))
    return jnp.concatenate(outs, axis=1) * w_row


def _inproj_kernel(x_ref, nw_ref, w_ref, ws_ref, qkw_ref, u_ref, us_ref, ust_ref, h_ref):
    tn = u_ref.shape[1]
    chunks = [(a, min(a + INPROJ_COL_CHUNK, tn)) for a in range(0, tn, INPROJ_COL_CHUNK)]

    @pl.when(pl.program_id(1) == 0)
    def _():
        x = x_ref[...]
        ms = jnp.mean(x * x, axis=-1, keepdims=True)
        h = (x * lax.rsqrt(ms + EPS) * nw_ref[...]).astype(BF16)
        h_ref[...] = h
        ws = ws_ref[...].astype(BF16)
        us_ref[...] = _nt(h, ws)
        dt_rows = jnp.concatenate([ws[g * LANE:g * LANE + SMALL_T_ROWS] for g in range(SSD_GROUPS)], axis=0)
        ust_ref[...] = _nt(dt_rows, h)
        for c, (a, b) in enumerate(chunks):
            r = _nt(h, w_ref[a:b, :])
            if c < 2:
                r = _head_rms(r, qkw_ref[c:c + 1, :])
            u_ref[:, a:b] = r.astype(BF16)

    @pl.when(pl.program_id(1) != 0)
    def _():
        for a, b in chunks:
            u_ref[:, a:b] = _nt(h_ref[...], w_ref[a:b, :]).astype(BF16)


def _inproj(x2, nw, w_big, w_small, qkw, tm=1024, tn=U_WIDTH // 4):
    assert (U_NQ, U_NK) == (0, INPROJ_COL_CHUNK) and NA_W == INPROJ_COL_CHUNK
    T = x2.shape[0]
    return pl.pallas_call(
        _inproj_kernel,
        grid=(T // tm, U_WIDTH // tn),
        in_specs=[
            pl.BlockSpec((tm, D_MODEL), lambda i, j: (i, 0)),
            pl.BlockSpec((1, D_MODEL), lambda i, j: (0, 0)),
            pl.BlockSpec((tn, D_MODEL), lambda i, j: (j, 0)),
            pl.BlockSpec((SMALL_W, D_MODEL), lambda i, j: (0, 0)),
            pl.BlockSpec((8, NA_W), lambda i, j: (0, 0)),
        ],
        out_specs=[
            pl.BlockSpec((tm, tn), lambda i, j: (i, j)),
            pl.BlockSpec((tm, SMALL_W), lambda i, j: (i, 0)),
            pl.BlockSpec((SSD_GROUPS * SMALL_T_ROWS, tm), lambda i, j: (0, i)),
        ],
        out_shape=[
            jax.ShapeDtypeStruct((T, U_WIDTH), BF16),
            jax.ShapeDtypeStruct((T, SMALL_W), F32),
            jax.ShapeDtypeStruct((SSD_GROUPS * SMALL_T_ROWS, T), F32),
        ],
        scratch_shapes=[pltpu.VMEM((tm, D_MODEL), BF16)],
        compiler_params=_cparams(("parallel", "arbitrary")),
        name="inproj",
    )(x2, nw, w_big, w_small, qkw)


def _conv_kernel(u_ref, w_ref, b_ref, o_ref):
    x = u_ref[0].astype(F32)
    S = x.shape[0]
    row = _iota2(x.shape, 0)
    acc = jnp.zeros_like(x) + b_ref[...]
    pad = SSD_CONV // 2
    for k in range(SSD_CONV):
        d = k - pad
        if d == 0:
            xs = x
        else:
            xs = pltpu.roll(x, (-d) % S, axis=0)
            valid = (row + d >= 0) & (row + d < S)
            xs = jnp.where(valid, xs, 0.0)
        acc = acc + w_ref[k:k + 1, :] * xs
    o_ref[0] = _silu(acc).astype(BF16)


def _conv(u3, conv_w, conv_b, tc=256):
    B, S, _ = u3.shape
    nblk = SSD_CONV_DIM // tc
    off = U_XBC // tc
    return pl.pallas_call(
        _conv_kernel,
        grid=(B, nblk),
        in_specs=[
            pl.BlockSpec((1, S, tc), lambda b, c: (b, 0, off + c)),
            pl.BlockSpec((SSD_CONV, tc), lambda b, c: (0, c)),
            pl.BlockSpec((1, tc), lambda b, c: (0, c)),
        ],
        out_specs=pl.BlockSpec((1, S, tc), lambda b, c: (b, 0, c)),
        out_shape=jax.ShapeDtypeStruct((B, S, SSD_CONV_DIM), BF16),
        compiler_params=_cparams(("parallel", "parallel")),
        name="ssd_conv",
    )(u3, conv_w, conv_b)


def _split_hi_lo(v):
    hi = v.astype(BF16)
    lo = (v - hi.astype(F32)).astype(BF16)
    return jnp.concatenate([hi, lo], axis=1)


def _split3(v, axis):
    hi = v.astype(BF16)
    r1 = v - hi.astype(F32)
    mid = r1.astype(BF16)
    lo = (r1 - mid.astype(F32)).astype(BF16)
    return jnp.concatenate([hi, mid, lo], axis=axis)


def _ssd_kernel(x_ref, b_ref, c_ref, z_ref, us_ref, ust_ref, prow_ref, pcol_ref, drow_ref, nw_ref,
                o_ref, acc_ref, cumc_ref, cumr_ref, wdt_ref, dec_ref, xs_ref, decx_ref, sst_ref, st_ref):
    L = SSD_CHUNK
    S = x_ref.shape[1]
    nc = S // L
    W = SSD_GROUP_W
    HG = SSD_HG
    R = 2 * HG

    ii = _iota2((L, L), 0)
    jj = _iota2((L, L), 1)
    tril = ii >= jj
    triu = jj >= ii
    tril_b = tril.astype(BF16)
    triu_b = triu.astype(BF16)

    er = _iota2((LANE, 2 * W), 0)
    ec = _iota2((LANE, 2 * W), 1)
    e = (er == jnp.where(ec < W, ec // SSD_HEAD_DIM, (ec - W) // SSD_HEAD_DIM + HG)).astype(BF16)
    e2 = jnp.concatenate([e, e], axis=0)

    bias_row = prow_ref[0:1, :]
    a_row = prow_ref[1:2, :]
    bias_col = pcol_ref[0, :, 0:1]
    a_col = pcol_ref[0, :, 1:2]

    lane_fwd = _iota2((L, LANE), 1) < HG
    row_fwd = _iota2((R, L), 0) < HG
    lane_half = _iota2((L, LANE), 1) < SSD_HEAD_DIM

    def decay_sums(c):
        rs = pl.ds(pl.multiple_of(c * L, L), L)
        dt_c = _softplus(us_ref[0, rs, :] + bias_row)
        a_c = dt_c * a_row
        dt_r = _softplus(ust_ref[:, rs] + bias_col)
        a_r = dt_r * a_col
        return dt_c, a_c, _dot(tril_b, _split3(a_c, 1)), dt_r, a_r, _dot(_split3(a_r, 0), triu_b)

    def decay_store(c, sums):
        rs = pl.ds(pl.multiple_of(c * L, L), L)
        dt_c, a_c, pp, dt_r, a_r, pr = sums
        p_c = pp[:, 0:LANE] + pp[:, LANE:2 * LANE] + pp[:, 2 * LANE:3 * LANE]
        tot_c = p_c[L - 1:L, :]
        cum_c = jnp.where(lane_fwd, p_c, tot_c - p_c + a_c)
        cumc_ref[rs, :] = cum_c * LOG2E
        wdt_ref[rs, :] = jnp.exp(tot_c - cum_c) * dt_c
        dec_ref[c] = jnp.broadcast_to(jnp.exp(tot_c), (16, LANE))
        p_r = pr[0:R] + pr[R:2 * R] + pr[2 * R:3 * R]
        tot_r = p_r[:, L - 1:L]
        cum_r = jnp.where(row_fwd, p_r, tot_r - p_r + a_r)
        cumr_ref[:, rs] = (cum_r - jnp.log(dt_r)) * LOG2E

    def intra_chunk(c):
        rs = pl.ds(pl.multiple_of(c * L, L), L)
        x_c = x_ref[0, rs, :]
        b_c = b_ref[0, rs, :]
        cb = _nt(c_ref[0, rs, :], b_c)
        ex = _dot(_split_hi_lo(jnp.concatenate([wdt_ref[rs, :], dec_ref[c]], axis=0)), e2)
        decx_ref[c] = ex[L:L + 8]
        xf = x_c.astype(F32)
        xw = jnp.concatenate([xf * ex[0:L, 0:W], xf * ex[0:L, W:2 * W]], axis=1).astype(BF16)
        xs_ref[c] = _tn(b_c, xw)
        cum_c = cumc_ref[rs, :]
        cum_r = cumr_ref[:, rs]
        for hp in range(HG // 2):
            ms = []
            for hh in range(2):
                hf = 2 * hp + hh
                hb = HG + 2 * hp + hh
                decf = jnp.exp2(jnp.where(tril, cum_c[:, hf:hf + 1] - cum_r[hf:hf + 1, :], -jnp.inf))
                decb = jnp.exp2(jnp.where(triu, cum_c[:, hb:hb + 1] - cum_r[hb:hb + 1, :], -jnp.inf))
                ms.append((cb * (decf + decb)).astype(BF16))
            m2 = jnp.concatenate(ms, axis=1)
            xp = x_c[:, hp * LANE:(hp + 1) * LANE]
            xz = jnp.zeros_like(xp)
            x2 = jnp.concatenate([jnp.where(lane_half, xp, xz), jnp.where(lane_half, xz, xp)], axis=0)
            acc_ref[rs, hp * LANE:(hp + 1) * LANE] = _dot(m2, x2)

    G = SSD_CHUNKS_PER_STEP
    n_steps = nc // G

    def fused_body(i, carry):
        sums = [decay_sums(G * (i + 1) + j) for j in range(G)]
        for j in range(G):
            intra_chunk(G * i + j)
        for j in range(G):
            decay_store(G * (i + 1) + j, sums[j])
        return carry

    for j in range(G):
        decay_store(j, decay_sums(j))
    lax.fori_loop(0, n_steps - 1, fused_body, 0, unroll=True)
    for j in range(G):
        intra_chunk(G * (n_steps - 1) + j)

    st_ref[...] = jnp.zeros_like(st_ref)

    def state_body(i, carry):
        for d, c in enumerate((i, nc - 1 - i)):
            ls = slice(d * W, (d + 1) * W)
            s_prev = st_ref[:, ls]
            sst_ref[c, :, ls] = s_prev.astype(BF16)
            st_ref[:, ls] = s_prev * decx_ref[c, 0:1, ls] + xs_ref[c, :, ls]
        return carry

    lax.fori_loop(0, nc, state_body, 0, unroll=True)

    def out_body(c, carry):
        rs = pl.ds(pl.multiple_of(c * L, L), L)
        yo = _dot(c_ref[0, rs, :], sst_ref[c])
        sc = _dot(_split_hi_lo(jnp.exp2(cumc_ref[rs, :])), e2)
        xf = x_ref[0, rs, :].astype(F32)
        y = acc_ref[rs, :] + yo[:, 0:W] * sc[:, 0:W] + yo[:, W:2 * W] * sc[:, W:2 * W] + xf * drow_ref[...]
        y = y * _silu(z_ref[0, rs, :].astype(F32))
        ms = jnp.mean(y * y, axis=-1, keepdims=True)
        o_ref[0, rs, :] = (y * lax.rsqrt(ms + EPS) * nw_ref[...]).astype(BF16)
        return carry

    lax.fori_loop(0, nc, out_body, 0, unroll=SSD_OUT_UNROLL)


def _ssd(xbc, u3, us3, ust, prow, pcol, drow, nw):
    B, S, _ = xbc.shape
    W = SSD_GROUP_W
    N = SSD_STATE
    nc = S // SSD_CHUNK
    return pl.pallas_call(
        _ssd_kernel,
        grid=(B, SSD_GROUPS),
        in_specs=[
            pl.BlockSpec((1, S, W), lambda b, g: (b, 0, g)),
            pl.BlockSpec((1, S, N), lambda b, g: (b, 0, SSD_D_INNER // N + g)),
            pl.BlockSpec((1, S, N), lambda b, g: (b, 0, SSD_D_INNER // N + SSD_GROUPS + g)),
            pl.BlockSpec((1, S, W), lambda b, g: (b, 0, U_Z // W + g)),
            pl.BlockSpec((1, S, LANE), lambda b, g: (b, 0, g)),
            pl.BlockSpec((SMALL_T_ROWS, S), lambda b, g: (g, b)),
            pl.BlockSpec((8, LANE), lambda b, g: (g, 0)),
            pl.BlockSpec((1, SMALL_T_ROWS, 8), lambda b, g: (g, 0, 0)),
            pl.BlockSpec((1, W), lambda b, g: (0, g)),
            pl.BlockSpec((1, W), lambda b, g: (0, g)),
        ],
        out_specs=pl.BlockSpec((1, S, W), lambda b, g: (b, 0, g)),
        out_shape=jax.ShapeDtypeStruct((B, S, SSD_D_INNER), BF16),
        scratch_shapes=[
            pltpu.VMEM((S, W), F32),
            pltpu.VMEM((S, LANE), F32),
            pltpu.VMEM((SMALL_T_ROWS, S), F32),
            pltpu.VMEM((S, LANE), F32),
            pltpu.VMEM((nc, 16, LANE), F32),
            pltpu.VMEM((nc, N, 2 * W), F32),
            pltpu.VMEM((nc, 8, 2 * W), F32),
            pltpu.VMEM((nc, N, 2 * W), BF16),
            pltpu.VMEM((N, 2 * W), F32),
        ],
        compiler_params=_cparams(("parallel", "parallel")),
        name="ssd_scan",
    )(xbc, xbc, xbc, u3, us3, ust, prow, pcol, drow, nw)


def _gla_kernel(q_ref, k_ref, v_ref, gg_ref, us_ref, a2_ref, bias_ref, nw_ref,
                o_ref, acc_ref, g_ref, qd_ref, kd_ref, kdp_ref, qcat_ref, x_ref, dec_ref, sst_ref):
    L = GLA_CHUNK
    BL = 2 * L
    DK = GLA_DK
    S = q_ref.shape[1]
    nb = S // BL
    scale = DK ** -0.5

    ii = _iota2((BL, BL), 0)
    jj = _iota2((BL, BL), 1)
    same = (ii // L) == (jj // L)
    masks = (same & (ii >= jj), same & (jj >= ii))
    tri2 = masks[0].astype(BF16)
    par_row = _iota2((BL, DK), 0) // L

    ga = us_ref[0].astype(BF16)
    g_ref[...] = _log2_sigmoid(_dot(ga, a2_ref[...]) + bias_ref[...]) * (1.0 / GLA_GATE_NORM)

    def decay_sums(i):
        g = g_ref[pl.ds(pl.multiple_of(i * BL, BL), BL), :]
        hi = g.astype(BF16)
        r1 = g - hi.astype(F32)
        mid = r1.astype(BF16)
        lo = (r1 - mid.astype(F32)).astype(BF16)
        return g, _dot(tri2, jnp.concatenate([hi, mid, lo], axis=1))

    def decay_block(i, sums):
        rs = pl.ds(pl.multiple_of(i * BL, BL), BL)
        g, pp = sums
        p = pp[:, 0:2 * DK] + pp[:, 2 * DK:4 * DK] + pp[:, 4 * DK:6 * DK]
        q_c = q_ref[0, rs, :].astype(F32) * scale
        k_c = k_ref[0, rs, :].astype(F32)
        zero = jnp.zeros((BL, DK), BF16)
        for d in range(2):
            p_d = p[:, d * DK:(d + 1) * DK]
            tot = jnp.where(par_row == 0, p_d[L - 1:L, :], p_d[BL - 1:BL, :])
            b = p_d if d == 0 else tot - p_d + g[:, DK:]
            qd = (q_c * jnp.exp2(b)).astype(BF16)
            kdec = (k_c * jnp.exp2(tot - b)).astype(BF16)
            qd_ref[d, rs, :] = qd
            kd_ref[d, rs, :] = (k_c * jnp.exp2(-b)).astype(BF16)
            for par in range(2):
                sel = par_row == par
                kdp_ref[d, rs, par * DK:(par + 1) * DK] = jnp.where(sel, kdec, zero)
                qcat_ref[rs, (2 * d + par) * DK:(2 * d + par + 1) * DK] = jnp.where(sel, qd, zero)
                last = (par + 1) * L - 1
                dec_ref[d, 2 * i + par] = jnp.broadcast_to(jnp.exp2(p_d[last:last + 1, :]), (8, DK))

    G = GLA_BLOCKS_PER_STEP
    n_groups = nb // G

    def intra_group(i):
        blks = [G * i + j for j in range(G)]
        rss = [pl.ds(pl.multiple_of(b * BL, BL), BL) for b in blks]
        vs = [v_ref[0, rs, :] for rs in rss]
        atts = [[_nt(qd_ref[d, rs, :], kd_ref[d, rs, :]) for d in range(2)] for rs in rss]
        for j, b in enumerate(blks):
            for d in range(2):
                x_ref[d, b] = _tn(vs[j], kdp_ref[d, rss[j], :])
        for j in range(G):
            att = jnp.where(masks[0], atts[j][0], 0.0) + jnp.where(masks[1], atts[j][1], 0.0)
            acc_ref[rss[j], :] = _dot(att.astype(BF16), vs[j])

    def fused_body(i, carry):
        sums = [decay_sums(G * (i + 1) + j) for j in range(G)]
        intra_group(i)
        for j in range(G):
            decay_block(G * (i + 1) + j, sums[j])
        return carry

    for j in range(G):
        decay_block(j, decay_sums(j))
    lax.fori_loop(0, n_groups - 1, fused_body, 0, unroll=True)
    intra_group(n_groups - 1)

    def state_body(d):
        def body(i, s):
            b = i if d == 0 else nb - 1 - i
            for par in ((0, 1) if d == 0 else (1, 0)):
                lane0 = (2 * d + par) * DK
                sst_ref[b, :, lane0:lane0 + DK] = s.astype(BF16)
                s = s * dec_ref[d, 2 * b + par, 0:1, :] + x_ref[d, b, :, par * DK:(par + 1) * DK]
            return s
        return body

    for d in range(2):
        lax.fori_loop(0, nb, state_body(d), jnp.zeros((GLA_DV, DK), F32), unroll=True)

    def out_body(b, carry):
        rs = pl.ds(pl.multiple_of(b * BL, BL), BL)
        o = acc_ref[rs, :] + _nt(qcat_ref[rs, :], sst_ref[b])
        ms = jnp.mean(o * o, axis=-1, keepdims=True)
        o = o * lax.rsqrt(ms + EPS) * nw_ref[...]
        o_ref[0, rs, :] = (o * _silu(gg_ref[0, rs, :].astype(F32))).astype(BF16)
        return carry

    lax.fori_loop(0, nb, out_body, 0, unroll=GLA_OUT_UNROLL)


def _gla(u3, us3, a2, bias, nw):
    B, S, _ = u3.shape
    DK, DV = GLA_DK, GLA_DV
    nc = S // GLA_CHUNK
    nb = nc // 2
    return pl.pallas_call(
        _gla_kernel,
        grid=(B, GLA_HEADS),
        in_specs=[
            pl.BlockSpec((1, S, DK), lambda b, h: (b, 0, U_GQ // DK + h)),
            pl.BlockSpec((1, S, DK), lambda b, h: (b, 0, U_GK // DK + h)),
            pl.BlockSpec((1, S, DV), lambda b, h: (b, 0, U_GV // DV + h)),
            pl.BlockSpec((1, S, DV), lambda b, h: (b, 0, U_GG // DV + h)),
            pl.BlockSpec((1, S, LANE), lambda b, h: (b, 0, 0)),
            pl.BlockSpec((LANE, 2 * DK), lambda b, h: (0, h)),
            pl.BlockSpec((1, 2 * DK), lambda b, h: (0, h)),
            pl.BlockSpec((1, DV), lambda b, h: (0, 0)),
        ],
        out_specs=pl.BlockSpec((1, S, DV), lambda b, h: (b, 0, h)),
        out_shape=jax.ShapeDtypeStruct((B, S, GLA_VAL_W), BF16),
        scratch_shapes=[
            pltpu.VMEM((S, DV), F32),
            pltpu.VMEM((S, 2 * DK), F32),
            pltpu.VMEM((2, S, DK), BF16),
            pltpu.VMEM((2, S, DK), BF16),
            pltpu.VMEM((2, S, 2 * DK), BF16),
            pltpu.VMEM((S, 4 * DK), BF16),
            pltpu.VMEM((2, nb, DV, 2 * DK), F32),
            pltpu.VMEM((2, nc, 8, DK), F32),
            pltpu.VMEM((nb, DV, 4 * DK), BF16),
        ],
        compiler_params=_cparams(("parallel", "parallel")),
        name="gla_scan",
    )(u3, u3, u3, u3, us3, a2, bias, nw)


def _na_bias_rows(rpb):
    H, R, C = rpb.shape
    n_pos = GRID_W - NA_WIN_W
    n_neg = GRID_W - NA_WIN_W + 2
    ext = jnp.concatenate([rpb[:, :, NA_WIN_W - 1:], jnp.repeat(rpb[:, :, C - 1:], n_pos, axis=2),
                           jnp.repeat(rpb[:, :, 0:1], n_neg, axis=2), rpb[:, :, 1:NA_WIN_W - 1]], axis=2)
    return jnp.pad(ext, ((0, 0), (0, 1), (0, 0))).reshape(H // 2, 2, R + 1, 2 * GRID_W)


def _na_kernel(flag_ref, q_ref, k_ref, v_ref, ext_ref, o_ref, vx_ref, tab_ref):
    NB, S, _ = q_ref.shape
    rows = S // GRID_W
    total = NB * rows
    win_h = NA_WIN_H
    nk = win_h * GRID_W

    vx_ref[:, :, 0:LANE] = v_ref[...]
    vx_ref[:, :, LANE:2 * LANE] = jnp.ones((NB, S, LANE), BF16)
    bound_ok = flag_ref[0] != 0

    first_q = _iota2((GRID_W, LANE), 1) < NA_HEAD_DIM

    @pl.when(pl.program_id(1) == 0)
    def _():
        q_col = _iota2((GRID_W, 2 * GRID_W), 0)
        k_col = _iota2((GRID_W, 2 * GRID_W), 1) % GRID_W
        w_start = jnp.clip(q_col - NA_WIN_W // 2, 0, GRID_W - NA_WIN_W)
        in_window = (k_col >= w_start) & (k_col < w_start + NA_WIN_W)
        low = _iota2((GRID_W, 2 * GRID_W), 1) < GRID_W
        for hd in range(2):
            def skewed(rr, shift):
                row = jnp.broadcast_to(ext_ref[0, hd, rr:rr + 1, :], (GRID_W, 2 * GRID_W))
                return pltpu.roll(row, shift, axis=1, stride=1, stride_axis=0)

            for e in range(2 * NA_WIN_H - 2):
                t = jnp.where(low, skewed(e, 0), skewed(e + 1, GRID_W))
                tab_ref[e, hd * GRID_W:(hd + 1) * GRID_W, :] = jnp.where(in_window, t, -jnp.inf)

    def locate(r):
        bi = r // rows
        rl = r - bi * rows
        r0 = jnp.clip(rl - win_h // 2, 0, rows - win_h)
        return bi, pl.multiple_of(rl * GRID_W, GRID_W), rl - r0, pl.multiple_of(r0 * GRID_W, GRID_W)

    def scores(r):
        bi, q0, delta, k0 = locate(r)
        q = q_ref[bi, pl.ds(q0, GRID_W), :]
        zero = jnp.zeros_like(q)
        qs = jnp.concatenate([jnp.where(first_q, q, zero), jnp.where(first_q, zero, q)], axis=0)
        rr0 = (win_h - 1) - delta
        bias = jnp.concatenate([tab_ref[rr0 + w] for w in range(0, win_h, 2)], axis=1)
        return _nt(qs, k_ref[bi, pl.ds(k0, nk), :]) + bias

    def probs_exact(s):
        return jnp.exp2(s - jnp.max(s, axis=-1, keepdims=True)).astype(BF16)

    def probs_bounded(r):
        return jnp.exp2(scores(r)).astype(BF16)

    def attend(r, p):
        bi, q0, _, k0 = locate(r)
        ox = _dot(p, vx_ref[bi, pl.ds(k0, nk), :])
        o = ox[:, 0:LANE] / ox[:, LANE:2 * LANE]
        o_ref[bi, pl.ds(q0, GRID_W), :] = jnp.where(first_q, o[:GRID_W], o[GRID_W:]).astype(BF16)

    @pl.when(bound_ok)
    def _():
        U = NA_ROWS_PER_STEP

        def row_body(i, p_prev):
            r = i * U
            for j in range(U):
                attend(r - U + j, p_prev[j])
            return tuple(probs_bounded(r + j) for j in range(U))

        p_last = lax.fori_loop(1, total // U, row_body, tuple(probs_bounded(j) for j in range(U)), unroll=True)
        for j in range(U):
            attend(total - U + j, p_last[j])

    @pl.when(jnp.logical_not(bound_ok))
    def _():
        U = NA_ROWS_PER_STEP_EXACT

        def row_body(i, carry):
            s_cur, p_prev = carry
            r = i * U
            for j in range(U):
                attend(r - U + j, p_prev[j])
            p = tuple(probs_exact(s) for s in s_cur)
            s_next = tuple(scores(jnp.minimum(r + U + j, total - 1)) for j in range(U))
            return s_next, p

        p0 = tuple(probs_exact(scores(j)) for j in range(U))
        s1 = tuple(scores(U + j) for j in range(U))
        _, p_last = lax.fori_loop(1, total // U, row_body, (s1, p0))
        for j in range(U):
            attend(total - U + j, p_last[j])


def _na_score_bound(rpb, q_norm_w, k_norm_w):
    H = rpb.shape[0]
    qk = (NA_HEAD_DIM ** 0.5 * LOG2E * (1.0 + NA_BOUND_SLACK)) * jnp.max(jnp.abs(q_norm_w)) * jnp.max(jnp.abs(k_norm_w))
    b_max = jnp.max(rpb.reshape(H, -1), axis=1) * LOG2E
    b_self = rpb[:, NA_WIN_H - 1, NA_WIN_W - 1] * LOG2E
    bound = qk + b_max
    flag = jnp.all(bound - (b_self - qk) <= NA_MAX_BOUND_GAP)
    return bound, flag.astype(jnp.int32).reshape(1)


def _na(u3, bias_rows, flag):
    B, S, _ = u3.shape
    assert S // GRID_W >= NA_WIN_H and NA_WIN_H % 2 == 0 and 2 * GRID_W == LANE
    nb = math.gcd(B, NA_BATCH_PER_STEP)
    assert (nb * (S // GRID_W)) % NA_ROWS_PER_STEP == 0
    return pl.pallas_call(
        _na_kernel,
        grid=(NA_HEADS // 2, B // nb),
        in_specs=[
            pl.BlockSpec(memory_space=pltpu.SMEM),
            pl.BlockSpec((nb, S, LANE), lambda h, b: (b, 0, U_NQ // LANE + h)),
            pl.BlockSpec((nb, S, LANE), lambda h, b: (b, 0, U_NK // LANE + h)),
            pl.BlockSpec((nb, S, LANE), lambda h, b: (b, 0, U_NV // LANE + h)),
            pl.BlockSpec((1, 2, 2 * NA_WIN_H, 2 * GRID_W), lambda h, b: (h, 0, 0, 0)),
        ],
        out_specs=pl.BlockSpec((nb, S, LANE), lambda h, b: (b, 0, h)),
        out_shape=jax.ShapeDtypeStruct((B, S, NA_W), BF16),
        scratch_shapes=[
            pltpu.VMEM((nb, S, 2 * LANE), BF16),
            pltpu.VMEM((2 * NA_WIN_H - 2, 2 * GRID_W, 2 * GRID_W), F32),
        ],
        compiler_params=_cparams(("parallel", "arbitrary")),
        name="na_attn",
    )(flag, u3, u3, u3, bias_rows)


def _merge_kernel(x_ref, ys_ref, yg_ref, yn_ref, gate_ref, ws_ref, wg_ref, wn_ref, wo_ref, o_ref):
    D = D_MODEL
    mixed = _sigmoid(gate_ref[:, 0:D].astype(F32)) * _dot(ys_ref[...], ws_ref[...])
    mixed += _sigmoid(gate_ref[:, D:2 * D].astype(F32)) * _dot(yg_ref[...], wg_ref[...])
    mixed += _sigmoid(gate_ref[:, 2 * D:3 * D].astype(F32)) * _dot(yn_ref[...], wn_ref[...])
    o_ref[...] = x_ref[...] + _dot(mixed.astype(BF16), wo_ref[...])


def _merge(x2, ys, yg, yn, u2, ws, wg, wn, wo, tm=512):
    T = x2.shape[0]
    D = D_MODEL
    row = lambda i: (i, 0)
    fixed = lambda i: (0, 0)
    return pl.pallas_call(
        _merge_kernel,
        grid=(T // tm,),
        in_specs=[
            pl.BlockSpec((tm, D), row),
            pl.BlockSpec((tm, D), row),
            pl.BlockSpec((tm, D), row),
            pl.BlockSpec((tm, D), row),
            pl.BlockSpec((tm, N_BRANCH * D), lambda i: (i, U_GATE // (N_BRANCH * D))),
            pl.BlockSpec((D, D), fixed),
            pl.BlockSpec((D, D), fixed),
            pl.BlockSpec((D, D), fixed),
            pl.BlockSpec((D, D), fixed),
        ],
        out_specs=pl.BlockSpec((tm, D), row),
        out_shape=jax.ShapeDtypeStruct((T, D), F32),
        compiler_params=_cparams(("parallel",)),
        name="merge",
    )(x2, ys, yg, yn, u2, ws, wg, wn, wo)


def _mlp_kernel(x_ref, nw_ref, w1_ref, w2_ref, o_ref, *, tf):
    x = x_ref[...]
    ms = jnp.mean(x * x, axis=-1, keepdims=True)
    h = (x * lax.rsqrt(ms + EPS) * nw_ref[...]).astype(BF16)
    acc = x
    for f in range(D_FF // tf):
        a = jnp.maximum(_dot(h, w1_ref[:, f * tf:(f + 1) * tf]), 0.0)
        acc = acc + _dot((a * a).astype(BF16), w2_ref[f * tf:(f + 1) * tf, :])
    o_ref[...] = acc


def _mlp(x2, nw, w1, w2, tm=512, tf=1024):
    T = x2.shape[0]
    D = D_MODEL
    return pl.pallas_call(
        functools.partial(_mlp_kernel, tf=tf),
        grid=(T // tm,),
        in_specs=[
            pl.BlockSpec((tm, D), lambda i: (i, 0)),
            pl.BlockSpec((1, D), lambda i: (0, 0)),
            pl.BlockSpec((D, D_FF), lambda i: (0, 0)),
            pl.BlockSpec((D_FF, D), lambda i: (0, 0)),
        ],
        out_specs=pl.BlockSpec((tm, D), lambda i: (i, 0)),
        out_shape=jax.ShapeDtypeStruct((T, D), F32),
        compiler_params=_cparams(("parallel",)),
        name="mlp",
    )(x2, nw, w1, w2)


def _pad_rows(w, start, total):
    return jnp.zeros((total, w.shape[1]), w.dtype).at[start:start + w.shape[0]].set(w)


def kernel(x, norm_mix_w, w_in, ssd_conv_w, ssd_conv_b, ssd_dt_bias_f, ssd_dt_bias_b, ssd_a_log_f,
           ssd_a_log_b, ssd_d, ssd_norm_w, gla_a2_f, gla_a2_bias_f, gla_a2_b, gla_a2_bias_b,
           gla_norm_w, na_q_norm_w, na_k_norm_w, na_rpb, w_branch_ssd, w_branch_gla, w_branch_na,
           w_out, norm_mlp_w, w_ff1, w_ff2):
    B, S, D = x.shape
    T = B * S
    depth = w_in.shape[0]
    x2 = x.reshape(T, D)
    w_in_t = jnp.swapaxes(w_in, 1, 2)
    for l in range(depth):
        w_big, w_small = _permute_weight(w_in_t, l)
        hg = (SSD_GROUPS, SSD_HG)
        zeros_r = jnp.zeros((SSD_GROUPS, LANE - 2 * SSD_HG), F32)
        dt_bias = jnp.concatenate([ssd_dt_bias_f[l].reshape(hg), ssd_dt_bias_b[l].reshape(hg), zeros_r], axis=1)
        a_neg = jnp.concatenate([-jnp.exp(ssd_a_log_f[l]).reshape(hg), -jnp.exp(ssd_a_log_b[l]).reshape(hg),
                                 zeros_r], axis=1)
        prow3 = jnp.concatenate([dt_bias[:, None], a_neg[:, None], jnp.zeros((SSD_GROUPS, 6, LANE), F32)], axis=1)
        prow = prow3.reshape(SSD_GROUPS * 8, LANE)
        pcol = jnp.transpose(prow3[:, :, :SMALL_T_ROWS], (0, 2, 1))
        drow = jnp.repeat(ssd_d[l], SSD_HEAD_DIM)[None, :]
        hk = (GLA_HEADS, 1, GLA_DK)
        a2 = jnp.concatenate([_pad_rows(gla_a2_f[l], SM_GAF, LANE).reshape((LANE,) + hk),
                              _pad_rows(gla_a2_b[l], SM_GAB, LANE).reshape((LANE,) + hk)],
                             axis=2).reshape(LANE, -1).astype(BF16)
        a2_bias = jnp.concatenate([gla_a2_bias_f[l].reshape(hk), gla_a2_bias_b[l].reshape(hk)],
                                  axis=1).reshape(1, -1)
        na_bound, na_flag = _na_score_bound(na_rpb[l], na_q_norm_w[l], na_k_norm_w[l])
        table = _na_bias_rows(na_rpb[l] * LOG2E - na_bound[:, None, None])
        q_row = jnp.tile(na_q_norm_w[l] * (NA_HEAD_DIM ** -0.5 * LOG2E), NA_HEADS)
        k_row = jnp.tile(na_k_norm_w[l], NA_HEADS)
        qkw = jnp.concatenate([q_row[None], k_row[None], jnp.zeros((6, NA_W), F32)], axis=0)

        u2, us2, ust = _inproj(x2, norm_mix_w[l][None, :], w_big, w_small, qkw)
        u3 = u2.reshape(B, S, U_WIDTH)
        us3 = us2.reshape(B, S, SMALL_W)
        xbc = _conv(u3, ssd_conv_w[l], ssd_conv_b[l][None, :])
        y_ssd = _ssd(xbc, u3, us3, ust, prow, pcol, drow, ssd_norm_w[l][None, :])
        y_gla = _gla(u3, us3, a2, a2_bias, gla_norm_w[l][None, :])
        y_na = _na(u3, table, na_flag)
        x2 = _merge(x2, y_ssd.reshape(T, -1), y_gla.reshape(T, -1), y_na.reshape(T, -1), u2,
                    w_branch_ssd[l].astype(BF16), w_branch_gla[l].astype(BF16),
                    w_branch_na[l].astype(BF16), w_out[l].astype(BF16))
        x2 = _mlp(x2, norm_mlp_w[l][None, :], w_ff1[l].astype(BF16), w_ff2[l].astype(BF16))
    return x2.reshape(B, S, D)
```

```python
import functools
import math

import jax
import jax.numpy as jnp
import numpy as np
from jax import lax
from jax.experimental import pallas as pl
from jax.experimental.pallas import tpu as pltpu

F32 = jnp.float32
BF16 = jnp.bfloat16

EPS = 1e-6
D_MODEL = 1024
GRID_W = 64

SSD_HEADS = 16
SSD_HEAD_DIM = 64
SSD_D_INNER = 1024
SSD_GROUPS = 2
SSD_STATE = 128
SSD_CONV = 5
SSD_CONV_DIM = 1536
SSD_CHUNK = 128
CONV_IN_BLOCKS = 3
CONV_TILE = 256
CONV_ROWS = 128
SSD_CHUNKS_PER_STEP = 4
SSD_OUT_UNROLL = 16
SSD_GROUP_W = SSD_D_INNER // SSD_GROUPS
SSD_HG = SSD_HEADS // SSD_GROUPS

GLA_HEADS = 4
GLA_DK = 128
GLA_DV = 256
GLA_KEY_W = 512
GLA_VAL_W = 1024
GLA_GATE_RANK = 16
GLA_GATE_NORM = 16.0
GLA_CHUNK = 64
GLA_BLOCKS_PER_STEP = 4
GLA_OUT_UNROLL = 16

NA_HEADS = 16
NA_HEAD_DIM = 64
NA_W = 1024
NA_WIN_H = 8
NA_WIN_W = 16
NA_BATCH_PER_STEP = 4
NA_ROWS_PER_STEP = 32
NA_ROWS_PER_STEP_EXACT = 2
LOG2E = 1.4426950408889634
NA_BOUND_SLACK = 0.02
NA_MAX_BOUND_GAP = 90.0

N_BRANCH = 3
D_FF = 4096

IN_SIZES = (SSD_D_INNER, SSD_CONV_DIM, SSD_HEADS, SSD_HEADS,
            GLA_KEY_W, GLA_KEY_W, GLA_VAL_W, GLA_VAL_W, GLA_GATE_RANK, GLA_GATE_RANK,
            NA_W, NA_W, NA_W, N_BRANCH * D_MODEL)
_IN_OFF = np.concatenate([[0], np.cumsum(IN_SIZES)])
(_O_Z, _O_XBC, _O_DTF, _O_DTB, _O_GQ, _O_GK, _O_GV, _O_GG, _O_GAF, _O_GAB,
 _O_NQ, _O_NK, _O_NV, _O_GATE) = [int(v) for v in _IN_OFF[:-1]]

U_NQ = 0
U_NK = 1024
U_NV = 2048
U_GATE = 3072
U_Z = 6144
U_XBC = 7168
U_GQ = 8704
U_GK = 9216
U_GV = 9728
U_GG = 10752
U_WIDTH = 11776
INPROJ_COL_CHUNK = 1024
LANE = 128
VMEM_LIMIT = 56 * 1024 * 1024

SMALL_W = SSD_GROUPS * LANE
SM_DTF, SM_DTB, SM_GAF, SM_GAB = 0, 8, 16, 32
SMALL_T_ROWS = 2 * SSD_HG

_BIG_SEGS = ((_O_NQ, 6144), (_O_Z, 2560), (_O_GQ, 3072))


def _cparams(sem, vmem=VMEM_LIMIT):
    return pltpu.CompilerParams(dimension_semantics=sem, vmem_limit_bytes=vmem)


def _sigmoid(x):
    return 1.0 / (1.0 + jnp.exp2(x * (-LOG2E)))


def _silu(x):
    return x * _sigmoid(x)


def _softplus(x):
    return jnp.maximum(x, 0.0) + jnp.log1p(jnp.exp(-jnp.abs(x)))


def _log2_sigmoid(x):
    t = x * (-LOG2E)
    return -(jnp.maximum(t, 0.0) + jnp.log2(1.0 + jnp.exp2(-jnp.abs(t))))


def _nt(a, b):
    return lax.dot_general(a, b, (((1,), (1,)), ((), ())), preferred_element_type=F32)


def _tn(a, b):
    return lax.dot_general(a, b, (((0,), (0,)), ((), ())), preferred_element_type=F32)


def _dot(a, b):
    return jnp.dot(a, b, preferred_element_type=F32)


def _iota2(shape, dim):
    return lax.broadcasted_iota(jnp.int32, shape, dim)


def _wperm_kernel(wt_ref, o_ref, ws_ref):
    off = 0
    for a, n in _BIG_SEGS:
        o_ref[off:off + n, :] = wt_ref[0, a:a + n, :].astype(BF16)
        off += n
    ws_ref[...] = jnp.zeros_like(ws_ref)
    for g in range(SSD_GROUPS):
        base, h0 = g * LANE, g * SSD_HG
        ws_ref[base + SM_DTF:base + SM_DTF + SSD_HG, :] = wt_ref[0, _O_DTF + h0:_O_DTF + h0 + SSD_HG, :]
        ws_ref[base + SM_DTB:base + SM_DTB + SSD_HG, :] = wt_ref[0, _O_DTB + h0:_O_DTB + h0 + SSD_HG, :]
    ws_ref[SM_GAF:SM_GAF + GLA_GATE_RANK, :] = wt_ref[0, _O_GAF:_O_GAF + GLA_GATE_RANK, :]
    ws_ref[SM_GAB:SM_GAB + GLA_GATE_RANK, :] = wt_ref[0, _O_GAB:_O_GAB + GLA_GATE_RANK, :]


def _permute_weight(w_t_all, layer, tc=256):
    _, n_in, d = w_t_all.shape
    return pl.pallas_call(
        _wperm_kernel,
        grid=(d // tc,),
        in_specs=[pl.BlockSpec((1, n_in, tc), lambda i: (layer, 0, i))],
        out_specs=[
            pl.BlockSpec((U_WIDTH, tc), lambda i: (0, i)),
            pl.BlockSpec((SMALL_W, tc), lambda i: (0, i)),
        ],
        out_shape=[
            jax.ShapeDtypeStruct((U_WIDTH, d), BF16),
            jax.ShapeDtypeStruct((SMALL_W, d), F32),
        ],
        compiler_params=_cparams(("parallel",)),
        name="w_permute",
    )(w_t_all)


def _head_rms(r, w_row):
    G = 2 * LANE
    er = _iota2((G, G), 0) // NA_HEAD_DIM
    ec = _iota2((G, G), 1) // NA_HEAD_DIM
    e_blk = (er == ec).astype(BF16)
    outs = []
    for a in range(0, r.shape[1], G):
        x = r[:, a:a + G]
        ms = _dot((x * x).astype(BF16), e_blk) * (1.0 / NA_HEAD_DIM)
        outs.append(x * lax.rsqrt(ms + EPS))
    return jnp.concatenate(outs, axis=1) * w_row


def _inproj_kernel(x_ref, nw_ref, w_ref, ws_ref, qkw_ref, u_ref, us_ref, ust_ref, h_ref):
    tn = u_ref.shape[1]
    chunks = [(a, min(a + INPROJ_COL_CHUNK, tn)) for a in range(0, tn, INPROJ_COL_CHUNK)]

    @pl.when(pl.program_id(1) == 0)
    def _():
        x = x_ref[...]
        ms = jnp.mean(x * x, axis=-1, keepdims=True)
        h = (x * lax.rsqrt(ms + EPS) * nw_ref[...]).astype(BF16)
        h_ref[...] = h
        ws = ws_ref[...].astype(BF16)
        us_ref[...] = _nt(h, ws)
        dt_rows = jnp.concatenate([ws[g * LANE:g * LANE + SMALL_T_ROWS] for g in range(SSD_GROUPS)], axis=0)
        ust_ref[...] = _nt(dt_rows, h)
        for c, (a, b) in enumerate(chunks):
            r = _nt(h, w_ref[a:b, :])
            if c < 2:
                r = _head_rms(r, qkw_ref[c:c + 1, :])
            u_ref[:, a:b] = r.astype(BF16)

    @pl.when(pl.program_id(1) != 0)
    def _():
        for a, b in chunks:
            u_ref[:, a:b] = _nt(h_ref[...], w_ref[a:b, :]).astype(BF16)


def _inproj(x2, nw, w_big, w_small, qkw, tm=1024, tn=U_WIDTH // 4):
    assert (U_NQ, U_NK) == (0, INPROJ_COL_CHUNK) and NA_W == INPROJ_COL_CHUNK
    T = x2.shape[0]
    return pl.pallas_call(
        _inproj_kernel,
        grid=(T // tm, U_WIDTH // tn),
        in_specs=[
            pl.BlockSpec((tm, D_MODEL), lambda i, j: (i, 0)),
            pl.BlockSpec((1, D_MODEL), lambda i, j: (0, 0)),
            pl.BlockSpec((tn, D_MODEL), lambda i, j: (j, 0)),
            pl.BlockSpec((SMALL_W, D_MODEL), lambda i, j: (0, 0)),
            pl.BlockSpec((8, NA_W), lambda i, j: (0, 0)),
        ],
        out_specs=[
            pl.BlockSpec((tm, tn), lambda i, j: (i, j)),
            pl.BlockSpec((tm, SMALL_W), lambda i, j: (i, 0)),
            pl.BlockSpec((SSD_GROUPS * SMALL_T_ROWS, tm), lambda i, j: (0, i)),
        ],
        out_shape=[
            jax.ShapeDtypeStruct((T, U_WIDTH), BF16),
            jax.ShapeDtypeStruct((T, SMALL_W), F32),
            jax.ShapeDtypeStruct((SSD_GROUPS * SMALL_T_ROWS, T), F32),
        ],
        scratch_shapes=[pltpu.VMEM((tm, D_MODEL), BF16)],
        compiler_params=_cparams(("parallel", "arbitrary")),
        name="inproj",
    )(x2, nw, w_big, w_small, qkw)


def _conv_kernel(*refs):
    u_refs, (w_ref, b_ref, o_ref, xp_ref) = refs[:CONV_IN_BLOCKS], refs[CONV_IN_BLOCKS:]
    S, wb = u_refs[0].shape[1], u_refs[0].shape[2]
    C = o_ref.shape[2]
    R = CONV_ROWS
    tc = CONV_TILE
    pad = SSD_CONV // 2
    xp_ref[0:R, :] = jnp.zeros((R, C), BF16)
    xp_ref[R + S:R + S + R, :] = jnp.zeros((R, C), BF16)
    for i, u_ref in enumerate(u_refs):
        xp_ref[R:R + S, i * wb:(i + 1) * wb] = u_ref[0]
    t_i = _iota2((R, 2 * R), 0)
    j_i = _iota2((R, 2 * R), 1)
    side = [k for k in range(SSD_CONV) if k != pad]
    shifts = jnp.concatenate([(j_i == t_i + R // 2 + (k - pad)).astype(BF16) for k in side], axis=0)
    for blk in range(S // R):
        for c0 in range(0, C, tc):
            cs = slice(c0, c0 + tc)
            win = xp_ref[blk * R + R // 2:blk * R + R // 2 + 2 * R, cs]
            sh = _dot(shifts, win)
            acc = b_ref[:, cs] + w_ref[pad:pad + 1, cs] * xp_ref[R + blk * R:R + (blk + 1) * R, cs].astype(F32)
            for n, k in enumerate(side):
                acc = acc + w_ref[k:k + 1, cs] * sh[n * R:(n + 1) * R]
            o_ref[0, blk * R:(blk + 1) * R, cs] = _silu(acc).astype(BF16)


def _conv(u3, conv_w, conv_b):
    B, S, _ = u3.shape
    wb = SSD_CONV_DIM // CONV_IN_BLOCKS
    assert U_XBC % wb == 0 and wb % CONV_TILE == 0
    off = U_XBC // wb
    in_blocks = [pl.BlockSpec((1, S, wb), functools.partial(lambda b, i: (b, 0, off + i), i=i))
                 for i in range(CONV_IN_BLOCKS)]
    return pl.pallas_call(
        _conv_kernel,
        grid=(B,),
        in_specs=in_blocks + [
            pl.BlockSpec((SSD_CONV, SSD_CONV_DIM), lambda b: (0, 0)),
            pl.BlockSpec((1, SSD_CONV_DIM), lambda b: (0, 0)),
        ],
        out_specs=pl.BlockSpec((1, S, SSD_CONV_DIM), lambda b: (b, 0, 0)),
        out_shape=jax.ShapeDtypeStruct((B, S, SSD_CONV_DIM), BF16),
        scratch_shapes=[pltpu.VMEM((S + 2 * CONV_ROWS, SSD_CONV_DIM), BF16)],
        compiler_params=_cparams(("parallel",)),
        name="ssd_conv",
    )(*([u3] * CONV_IN_BLOCKS), conv_w, conv_b)


def _split_hi_lo(v):
    hi = v.astype(BF16)
    lo = (v - hi.astype(F32)).astype(BF16)
    return jnp.concatenate([hi, lo], axis=1)


def _split3(v, axis):
    hi = v.astype(BF16)
    r1 = v - hi.astype(F32)
    mid = r1.astype(BF16)
    lo = (r1 - mid.astype(F32)).astype(BF16)
    return jnp.concatenate([hi, mid, lo], axis=axis)


def _ssd_kernel(x_ref, b_ref, c_ref, z_ref, us_ref, ust_ref, prow_ref, pcol_ref, drow_ref, nw_ref,
                o_ref, acc_ref, cumc_ref, cumr_ref, wdt_ref, dec_ref, xs_ref, decx_ref, sst_ref, st_ref):
    L = SSD_CHUNK
    S = x_ref.shape[1]
    nc = S // L
    W = SSD_GROUP_W
    HG = SSD_HG
    R = 2 * HG

    ii = _iota2((L, L), 0)
    jj = _iota2((L, L), 1)
    tril = ii >= jj
    triu = jj >= ii
    tril_b = tril.astype(BF16)
    triu_b = triu.astype(BF16)

    er = _iota2((LANE, 2 * W), 0)
    ec = _iota2((LANE, 2 * W), 1)
    e = (er == jnp.where(ec < W, ec // SSD_HEAD_DIM, (ec - W) // SSD_HEAD_DIM + HG)).astype(BF16)
    e2 = jnp.concatenate([e, e], axis=0)

    bias_row = prow_ref[0:1, :]
    a_row = prow_ref[1:2, :]
    bias_col = pcol_ref[0, :, 0:1]
    a_col = pcol_ref[0, :, 1:2]

    lane_fwd = _iota2((L, LANE), 1) < HG
    row_fwd = _iota2((R, L), 0) < HG
    lane_half = _iota2((L, LANE), 1) < SSD_HEAD_DIM

    def decay_sums(c):
        rs = pl.ds(pl.multiple_of(c * L, L), L)
        dt_c = _softplus(us_ref[0, rs, :] + bias_row)
        a_c = dt_c * a_row
        dt_r = _softplus(ust_ref[:, rs] + bias_col)
        a_r = dt_r * a_col
        return dt_c, a_c, _dot(tril_b, _split3(a_c, 1)), dt_r, a_r, _dot(_split3(a_r, 0), triu_b)

    def decay_store(c, sums):
        rs = pl.ds(pl.multiple_of(c * L, L), L)
        dt_c, a_c, pp, dt_r, a_r, pr = sums
        p_c = pp[:, 0:LANE] + pp[:, LANE:2 * LANE] + pp[:, 2 * LANE:3 * LANE]
        tot_c = p_c[L - 1:L, :]
        cum_c = jnp.where(lane_fwd, p_c, tot_c - p_c + a_c)
        cumc_ref[rs, :] = cum_c * LOG2E
        wdt_ref[rs, :] = jnp.exp(tot_c - cum_c) * dt_c
        dec_ref[c] = jnp.broadcast_to(jnp.exp(tot_c), (16, LANE))
        p_r = pr[0:R] + pr[R:2 * R] + pr[2 * R:3 * R]
        tot_r = p_r[:, L - 1:L]
        cum_r = jnp.where(row_fwd, p_r, tot_r - p_r + a_r)
        cumr_ref[:, rs] = (cum_r - jnp.log(dt_r)) * LOG2E

    def intra_chunk(c):
        rs = pl.ds(pl.multiple_of(c * L, L), L)
        x_c = x_ref[0, rs, :]
        b_c = b_ref[0, rs, :]
        cb = _nt(c_ref[0, rs, :], b_c)
        ex = _dot(_split_hi_lo(jnp.concatenate([wdt_ref[rs, :], dec_ref[c]], axis=0)), e2)
        decx_ref[c] = ex[L:L + 8]
        xf = x_c.astype(F32)
        xw = jnp.concatenate([xf * ex[0:L, 0:W], xf * ex[0:L, W:2 * W]], axis=1).astype(BF16)
        xs_ref[c] = _tn(b_c, xw)
        cum_c = cumc_ref[rs, :]
        cum_r = cumr_ref[:, rs]
        for hp in range(HG // 2):
            ms = []
            for hh in range(2):
                hf = 2 * hp + hh
                hb = HG + 2 * hp + hh
                decf = jnp.exp2(jnp.where(tril, cum_c[:, hf:hf + 1] - cum_r[hf:hf + 1, :], -jnp.inf))
                decb = jnp.exp2(jnp.where(triu, cum_c[:, hb:hb + 1] - cum_r[hb:hb + 1, :], -jnp.inf))
                ms.append((cb * (decf + decb)).astype(BF16))
            m2 = jnp.concatenate(ms, axis=1)
            xp = x_c[:, hp * LANE:(hp + 1) * LANE]
            xz = jnp.zeros_like(xp)
            x2 = jnp.concatenate([jnp.where(lane_half, xp, xz), jnp.where(lane_half, xz, xp)], axis=0)
            acc_ref[rs, hp * LANE:(hp + 1) * LANE] = _dot(m2, x2)

    G = SSD_CHUNKS_PER_STEP
    n_steps = nc // G

    def fused_body(i, carry):
        sums = [decay_sums(G * (i + 1) + j) for j in range(G)]
        for j in range(G):
            intra_chunk(G * i + j)
        for j in range(G):
            decay_store(G * (i + 1) + j, sums[j])
        return carry

    for j in range(G):
        decay_store(j, decay_sums(j))
    lax.fori_loop(0, n_steps - 1, fused_body, 0, unroll=True)
    for j in range(G):
        intra_chunk(G * (n_steps - 1) + j)

    st_ref[...] = jnp.zeros_like(st_ref)

    def state_body(i, carry):
        for d, c in enumerate((i, nc - 1 - i)):
            ls = slice(d * W, (d + 1) * W)
            s_prev = st_ref[:, ls]
            sst_ref[c, :, ls] = s_prev.astype(BF16)
            st_ref[:, ls] = s_prev * decx_ref[c, 0:1, ls] + xs_ref[c, :, ls]
        return carry

    lax.fori_loop(0, nc, state_body, 0, unroll=True)

    def out_body(c, carry):
        rs = pl.ds(pl.multiple_of(c * L, L), L)
        yo = _dot(c_ref[0, rs, :], sst_ref[c])
        sc = _dot(_split_hi_lo(jnp.exp2(cumc_ref[rs, :])), e2)
        xf = x_ref[0, rs, :].astype(F32)
        y = acc_ref[rs, :] + yo[:, 0:W] * sc[:, 0:W] + yo[:, W:2 * W] * sc[:, W:2 * W] + xf * drow_ref[...]
        y = y * _silu(z_ref[0, rs, :].astype(F32))
        ms = jnp.mean(y * y, axis=-1, keepdims=True)
        o_ref[0, rs, :] = (y * lax.rsqrt(ms + EPS) * nw_ref[...]).astype(BF16)
        return carry

    lax.fori_loop(0, nc, out_body, 0, unroll=SSD_OUT_UNROLL)


def _ssd(xbc, u3, us3, ust, prow, pcol, drow, nw):
    B, S, _ = xbc.shape
    W = SSD_GROUP_W
    N = SSD_STATE
    nc = S // SSD_CHUNK
    return pl.pallas_call(
        _ssd_kernel,
        grid=(B, SSD_GROUPS),
        in_specs=[
            pl.BlockSpec((1, S, W), lambda b, g: (b, 0, g)),
            pl.BlockSpec((1, S, N), lambda b, g: (b, 0, SSD_D_INNER // N + g)),
            pl.BlockSpec((1, S, N), lambda b, g: (b, 0, SSD_D_INNER // N + SSD_GROUPS + g)),
            pl.BlockSpec((1, S, W), lambda b, g: (b, 0, U_Z // W + g)),
            pl.BlockSpec((1, S, LANE), lambda b, g: (b, 0, g)),
            pl.BlockSpec((SMALL_T_ROWS, S), lambda b, g: (g, b)),
            pl.BlockSpec((8, LANE), lambda b, g: (g, 0)),
            pl.BlockSpec((1, SMALL_T_ROWS, 8), lambda b, g: (g, 0, 0)),
            pl.BlockSpec((1, W), lambda b, g: (0, g)),
            pl.BlockSpec((1, W), lambda b, g: (0, g)),
        ],
        out_specs=pl.BlockSpec((1, S, W), lambda b, g: (b, 0, g)),
        out_shape=jax.ShapeDtypeStruct((B, S, SSD_D_INNER), BF16),
        scratch_shapes=[
            pltpu.VMEM((S, W), F32),
            pltpu.VMEM((S, LANE), F32),
            pltpu.VMEM((SMALL_T_ROWS, S), F32),
            pltpu.VMEM((S, LANE), F32),
            pltpu.VMEM((nc, 16, LANE), F32),
            pltpu.VMEM((nc, N, 2 * W), F32),
            pltpu.VMEM((nc, 8, 2 * W), F32),
            pltpu.VMEM((nc, N, 2 * W), BF16),
            pltpu.VMEM((N, 2 * W), F32),
        ],
        compiler_params=_cparams(("parallel", "parallel")),
        name="ssd_scan",
    )(xbc, xbc, xbc, u3, us3, ust, prow, pcol, drow, nw)


def _gla_kernel(q_ref, k_ref, v_ref, gg_ref, us_ref, a2_ref, bias_ref, nw_ref,
                o_ref, acc_ref, g_ref, qd_ref, kd_ref, kdp_ref, qcat_ref, x_ref, dec_ref, sst_ref):
    L = GLA_CHUNK
    BL = 2 * L
    DK = GLA_DK
    S = q_ref.shape[1]
    nb = S // BL
    scale = DK ** -0.5

    ii = _iota2((BL, BL), 0)
    jj = _iota2((BL, BL), 1)
    same = (ii // L) == (jj // L)
    masks = (same & (ii >= jj), same & (jj >= ii))
    tri2 = masks[0].astype(BF16)
    par_row = _iota2((BL, DK), 0) // L

    ga = us_ref[0].astype(BF16)
    g_ref[...] = _log2_sigmoid(_dot(ga, a2_ref[...]) + bias_ref[...]) * (1.0 / GLA_GATE_NORM)

    def decay_sums(i):
        g = g_ref[pl.ds(pl.multiple_of(i * BL, BL), BL), :]
        hi = g.astype(BF16)
        r1 = g - hi.astype(F32)
        mid = r1.astype(BF16)
        lo = (r1 - mid.astype(F32)).astype(BF16)
        return g, _dot(tri2, jnp.concatenate([hi, mid, lo], axis=1))

    def decay_block(i, sums):
        rs = pl.ds(pl.multiple_of(i * BL, BL), BL)
        g, pp = sums
        p = pp[:, 0:2 * DK] + pp[:, 2 * DK:4 * DK] + pp[:, 4 * DK:6 * DK]
        q_c = q_ref[0, rs, :].astype(F32) * scale
        k_c = k_ref[0, rs, :].astype(F32)
        zero = jnp.zeros((BL, DK), BF16)
        for d in range(2):
            p_d = p[:, d * DK:(d + 1) * DK]
            tot = jnp.where(par_row == 0, p_d[L - 1:L, :], p_d[BL - 1:BL, :])
            b = p_d if d == 0 else tot - p_d + g[:, DK:]
            qd = (q_c * jnp.exp2(b)).astype(BF16)
            kdec = (k_c * jnp.exp2(tot - b)).astype(BF16)
            qd_ref[d, rs, :] = qd
            kd_ref[d, rs, :] = (k_c * jnp.exp2(-b)).astype(BF16)
            for par in range(2):
                sel = par_row == par
                kdp_ref[d, rs, par * DK:(par + 1) * DK] = jnp.where(sel, kdec, zero)
                qcat_ref[rs, (2 * d + par) * DK:(2 * d + par + 1) * DK] = jnp.where(sel, qd, zero)
                last = (par + 1) * L - 1
                dec_ref[d, 2 * i + par] = jnp.broadcast_to(jnp.exp2(p_d[last:last + 1, :]), (8, DK))

    G = GLA_BLOCKS_PER_STEP
    n_groups = nb // G

    def intra_group(i):
        blks = [G * i + j for j in range(G)]
        rss = [pl.ds(pl.multiple_of(b * BL, BL), BL) for b in blks]
        vs = [v_ref[0, rs, :] for rs in rss]
        atts = [[_nt(qd_ref[d, rs, :], kd_ref[d, rs, :]) for d in range(2)] for rs in rss]
        for j, b in enumerate(blks):
            for d in range(2):
                x_ref[d, b] = _tn(vs[j], kdp_ref[d, rss[j], :])
        for j in range(G):
            att = jnp.where(masks[0], atts[j][0], 0.0) + jnp.where(masks[1], atts[j][1], 0.0)
            acc_ref[rss[j], :] = _dot(att.astype(BF16), vs[j])

    def fused_body(i, carry):
        sums = [decay_sums(G * (i + 1) + j) for j in range(G)]
        intra_group(i)
        for j in range(G):
            decay_block(G * (i + 1) + j, sums[j])
        return carry

    for j in range(G):
        decay_block(j, decay_sums(j))
    lax.fori_loop(0, n_groups - 1, fused_body, 0, unroll=True)
    intra_group(n_groups - 1)

    def state_body(d):
        def body(i, s):
            b = i if d == 0 else nb - 1 - i
            for par in ((0, 1) if d == 0 else (1, 0)):
                lane0 = (2 * d + par) * DK
                sst_ref[b, :, lane0:lane0 + DK] = s.astype(BF16)
                s = s * dec_ref[d, 2 * b + par, 0:1, :] + x_ref[d, b, :, par * DK:(par + 1) * DK]
            return s
        return body

    for d in range(2):
        lax.fori_loop(0, nb, state_body(d), jnp.zeros((GLA_DV, DK), F32), unroll=True)

    def out_body(b, carry):
        rs = pl.ds(pl.multiple_of(b * BL, BL), BL)
        o = acc_ref[rs, :] + _nt(qcat_ref[rs, :], sst_ref[b])
        ms = jnp.mean(o * o, axis=-1, keepdims=True)
        o = o * lax.rsqrt(ms + EPS) * nw_ref[...]
        o_ref[0, rs, :] = (o * _silu(gg_ref[0, rs, :].astype(F32))).astype(BF16)
        return carry

    lax.fori_loop(0, nb, out_body, 0, unroll=GLA_OUT_UNROLL)


def _gla(u3, us3, a2, bias, nw):
    B, S, _ = u3.shape
    DK, DV = GLA_DK, GLA_DV
    nc = S // GLA_CHUNK
    nb = nc // 2
    return pl.pallas_call(
        _gla_kernel,
        grid=(B, GLA_HEADS),
        in_specs=[
            pl.BlockSpec((1, S, DK), lambda b, h: (b, 0, U_GQ // DK + h)),
            pl.BlockSpec((1, S, DK), lambda b, h: (b, 0, U_GK // DK + h)),
            pl.BlockSpec((1, S, DV), lambda b, h: (b, 0, U_GV // DV + h)),
            pl.BlockSpec((1, S, DV), lambda b, h: (b, 0, U_GG // DV + h)),
            pl.BlockSpec((1, S, LANE), lambda b, h: (b, 0, 0)),
            pl.BlockSpec((LANE, 2 * DK), lambda b, h: (0, h)),
            pl.BlockSpec((1, 2 * DK), lambda b, h: (0, h)),
            pl.BlockSpec((1, DV), lambda b, h: (0, 0)),
        ],
        out_specs=pl.BlockSpec((1, S, DV), lambda b, h: (b, 0, h)),
        out_shape=jax.ShapeDtypeStruct((B, S, GLA_VAL_W), BF16),
        scratch_shapes=[
            pltpu.VMEM((S, DV), F32),
            pltpu.VMEM((S, 2 * DK), F32),
            pltpu.VMEM((2, S, DK), BF16),
            pltpu.VMEM((2, S, DK), BF16),
            pltpu.VMEM((2, S, 2 * DK), BF16),
            pltpu.VMEM((S, 4 * DK), BF16),
            pltpu.VMEM((2, nb, DV, 2 * DK), F32),
            pltpu.VMEM((2, nc, 8, DK), F32),
            pltpu.VMEM((nb, DV, 4 * DK), BF16),
        ],
        compiler_params=_cparams(("parallel", "parallel")),
        name="gla_scan",
    )(u3, u3, u3, u3, us3, a2, bias, nw)


def _na_bias_rows(rpb):
    H, R, C = rpb.shape
    n_pos = GRID_W - NA_WIN_W
    n_neg = GRID_W - NA_WIN_W + 2
    ext = jnp.concatenate([rpb[:, :, NA_WIN_W - 1:], jnp.repeat(rpb[:, :, C - 1:], n_pos, axis=2),
                           jnp.repeat(rpb[:, :, 0:1], n_neg, axis=2), rpb[:, :, 1:NA_WIN_W - 1]], axis=2)
    return jnp.pad(ext, ((0, 0), (0, 1), (0, 0))).reshape(H // 2, 2, R + 1, 2 * GRID_W)


def _na_kernel(flag_ref, q_ref, k_ref, v_ref, ext_ref, o_ref, vx_ref, tab_ref):
    NB, S, _ = q_ref.shape
    rows = S // GRID_W
    total = NB * rows
    win_h = NA_WIN_H
    nk = win_h * GRID_W

    vx_ref[:, :, 0:LANE] = v_ref[...]
    vx_ref[:, :, LANE:2 * LANE] = jnp.ones((NB, S, LANE), BF16)
    bound_ok = flag_ref[0] != 0

    first_q = _iota2((GRID_W, LANE), 1) < NA_HEAD_DIM

    @pl.when(pl.program_id(1) == 0)
    def _():
        q_col = _iota2((GRID_W, 2 * GRID_W), 0)
        k_col = _iota2((GRID_W, 2 * GRID_W), 1) % GRID_W
        w_start = jnp.clip(q_col - NA_WIN_W // 2, 0, GRID_W - NA_WIN_W)
        in_window = (k_col >= w_start) & (k_col < w_start + NA_WIN_W)
        low = _iota2((GRID_W, 2 * GRID_W), 1) < GRID_W
        for hd in range(2):
            def skewed(rr, shift):
                row = jnp.broadcast_to(ext_ref[0, hd, rr:rr + 1, :], (GRID_W, 2 * GRID_W))
                return pltpu.roll(row, shift, axis=1, stride=1, stride_axis=0)

            for e in range(2 * NA_WIN_H - 2):
                t = jnp.where(low, skewed(e, 0), skewed(e + 1, GRID_W))
                tab_ref[e, hd * GRID_W:(hd + 1) * GRID_W, :] = jnp.where(in_window, t, -jnp.inf)

    def locate(r):
        bi = r // rows
        rl = r - bi * rows
        r0 = jnp.clip(rl - win_h // 2, 0, rows - win_h)
        return bi, pl.multiple_of(rl * GRID_W, GRID_W), rl - r0, pl.multiple_of(r0 * GRID_W, GRID_W)

    def scores(r):
        bi, q0, delta, k0 = locate(r)
        q = q_ref[bi, pl.ds(q0, GRID_W), :]
        zero = jnp.zeros_like(q)
        qs = jnp.concatenate([jnp.where(first_q, q, zero), jnp.where(first_q, zero, q)], axis=0)
        rr0 = (win_h - 1) - delta
        bias = jnp.concatenate([tab_ref[rr0 + w] for w in range(0, win_h, 2)], axis=1)
        return _nt(qs, k_ref[bi, pl.ds(k0, nk), :]) + bias

    def probs_exact(s):
        return jnp.exp2(s - jnp.max(s, axis=-1, keepdims=True)).astype(BF16)

    def probs_bounded(r):
        return jnp.exp2(scores(r)).astype(BF16)

    def attend(r, p):
        bi, q0, _, k0 = locate(r)
        ox = _dot(p, vx_ref[bi, pl.ds(k0, nk), :])
        o = ox[:, 0:LANE] / ox[:, LANE:2 * LANE]
        o_ref[bi, pl.ds(q0, GRID_W), :] = jnp.where(first_q, o[:GRID_W], o[GRID_W:]).astype(BF16)

    @pl.when(bound_ok)
    def _():
        U = NA_ROWS_PER_STEP

        def row_body(i, p_prev):
            r = i * U
            for j in range(U):
                attend(r - U + j, p_prev[j])
            return tuple(probs_bounded(r + j) for j in range(U))

        p_last = lax.fori_loop(1, total // U, row_body, tuple(probs_bounded(j) for j in range(U)), unroll=True)
        for j in range(U):
            attend(total - U + j, p_last[j])

    @pl.when(jnp.logical_not(bound_ok))
    def _():
        U = NA_ROWS_PER_STEP_EXACT

        def row_body(i, carry):
            s_cur, p_prev = carry
            r = i * U
            for j in range(U):
                attend(r - U + j, p_prev[j])
            p = tuple(probs_exact(s) for s in s_cur)
            s_next = tuple(scores(jnp.minimum(r + U + j, total - 1)) for j in range(U))
            return s_next, p

        p0 = tuple(probs_exact(scores(j)) for j in range(U))
        s1 = tuple(scores(U + j) for j in range(U))
        _, p_last = lax.fori_loop(1, total // U, row_body, (s1, p0))
        for j in range(U):
            attend(total - U + j, p_last[j])


def _na_score_bound(rpb, q_norm_w, k_norm_w):
    H = rpb.shape[0]
    qk = (NA_HEAD_DIM ** 0.5 * LOG2E * (1.0 + NA_BOUND_SLACK)) * jnp.max(jnp.abs(q_norm_w)) * jnp.max(jnp.abs(k_norm_w))
    b_max = jnp.max(rpb.reshape(H, -1), axis=1) * LOG2E
    b_self = rpb[:, NA_WIN_H - 1, NA_WIN_W - 1] * LOG2E
    bound = qk + b_max
    flag = jnp.all(bound - (b_self - qk) <= NA_MAX_BOUND_GAP)
    return bound, flag.astype(jnp.int32).reshape(1)


def _na(u3, bias_rows, flag):
    B, S, _ = u3.shape
    assert S // GRID_W >= NA_WIN_H and NA_WIN_H % 2 == 0 and 2 * GRID_W == LANE
    nb = math.gcd(B, NA_BATCH_PER_STEP)
    assert (nb * (S // GRID_W)) % NA_ROWS_PER_STEP == 0
    return pl.pallas_call(
        _na_kernel,
        grid=(NA_HEADS // 2, B // nb),
        in_specs=[
            pl.BlockSpec(memory_space=pltpu.SMEM),
            pl.BlockSpec((nb, S, LANE), lambda h, b: (b, 0, U_NQ // LANE + h)),
            pl.BlockSpec((nb, S, LANE), lambda h, b: (b, 0, U_NK // LANE + h)),
            pl.BlockSpec((nb, S, LANE), lambda h, b: (b, 0, U_NV // LANE + h)),
            pl.BlockSpec((1, 2, 2 * NA_WIN_H, 2 * GRID_W), lambda h, b: (h, 0, 0, 0)),
        ],
        out_specs=pl.BlockSpec((nb, S, LANE), lambda h, b: (b, 0, h)),
        out_shape=jax.ShapeDtypeStruct((B, S, NA_W), BF16),
        scratch_shapes=[
            pltpu.VMEM((nb, S, 2 * LANE), BF16),
            pltpu.VMEM((2 * NA_WIN_H - 2, 2 * GRID_W, 2 * GRID_W), F32),
        ],
        compiler_params=_cparams(("parallel", "arbitrary")),
        name="na_attn",
    )(flag, u3, u3, u3, bias_rows)


def _merge_kernel(x_ref, ys_ref, yg_ref, yn_ref, gate_ref, ws_ref, wg_ref, wn_ref, wo_ref, o_ref):
    D = D_MODEL
    mixed = _sigmoid(gate_ref[:, 0:D].astype(F32)) * _dot(ys_ref[...], ws_ref[...])
    mixed += _sigmoid(gate_ref[:, D:2 * D].astype(F32)) * _dot(yg_ref[...], wg_ref[...])
    mixed += _sigmoid(gate_ref[:, 2 * D:3 * D].astype(F32)) * _dot(yn_ref[...], wn_ref[...])
    o_ref[...] = x_ref[...] + _dot(mixed.astype(BF16), wo_ref[...])


def _merge(x2, ys, yg, yn, u2, ws, wg, wn, wo, tm=512):
    T = x2.shape[0]
    D = D_MODEL
    row = lambda i: (i, 0)
    fixed = lambda i: (0, 0)
    return pl.pallas_call(
        _merge_kernel,
        grid=(T // tm,),
        in_specs=[
            pl.BlockSpec((tm, D), row),
            pl.BlockSpec((tm, D), row),
            pl.BlockSpec((tm, D), row),
            pl.BlockSpec((tm, D), row),
            pl.BlockSpec((tm, N_BRANCH * D), lambda i: (i, U_GATE // (N_BRANCH * D))),
            pl.BlockSpec((D, D), fixed),
            pl.BlockSpec((D, D), fixed),
            pl.BlockSpec((D, D), fixed),
            pl.BlockSpec((D, D), fixed),
        ],
        out_specs=pl.BlockSpec((tm, D), row),
        out_shape=jax.ShapeDtypeStruct((T, D), F32),
        compiler_params=_cparams(("parallel",)),
        name="merge",
    )(x2, ys, yg, yn, u2, ws, wg, wn, wo)


def _mlp_kernel(x_ref, nw_ref, w1_ref, w2_ref, o_ref, *, tf):
    x = x_ref[...]
    ms = jnp.mean(x * x, axis=-1, keepdims=True)
    h = (x * lax.rsqrt(ms + EPS) * nw_ref[...]).astype(BF16)
    acc = x
    for f in range(D_FF // tf):
        a = jnp.maximum(_dot(h, w1_ref[:, f * tf:(f + 1) * tf]), 0.0)
        acc = acc + _dot((a * a).astype(BF16), w2_ref[f * tf:(f + 1) * tf, :])
    o_ref[...] = acc


def _mlp(x2, nw, w1, w2, tm=512, tf=1024):
    T = x2.shape[0]
    D = D_MODEL
    return pl.pallas_call(
        functools.partial(_mlp_kernel, tf=tf),
        grid=(T // tm,),
        in_specs=[
            pl.BlockSpec((tm, D), lambda i: (i, 0)),
            pl.BlockSpec((1, D), lambda i: (0, 0)),
            pl.BlockSpec((D, D_FF), lambda i: (0, 0)),
            pl.BlockSpec((D_FF, D), lambda i: (0, 0)),
        ],
        out_specs=pl.BlockSpec((tm, D), lambda i: (i, 0)),
        out_shape=jax.ShapeDtypeStruct((T, D), F32),
        compiler_params=_cparams(("parallel",)),
        name="mlp",
    )(x2, nw, w1, w2)


def _pad_rows(w, start, total):
    return jnp.zeros((total, w.shape[1]), w.dtype).at[start:start + w.shape[0]].set(w)


def kernel(x, norm_mix_w, w_in, ssd_conv_w, ssd_conv_b, ssd_dt_bias_f, ssd_dt_bias_b, ssd_a_log_f,
           ssd_a_log_b, ssd_d, ssd_norm_w, gla_a2_f, gla_a2_bias_f, gla_a2_b, gla_a2_bias_b,
           gla_norm_w, na_q_norm_w, na_k_norm_w, na_rpb, w_branch_ssd, w_branch_gla, w_branch_na,
           w_out, norm_mlp_w, w_ff1, w_ff2):
    B, S, D = x.shape
    T = B * S
    depth = w_in.shape[0]
    x2 = x.reshape(T, D)
    w_in_t = jnp.swapaxes(w_in, 1, 2)
    for l in range(depth):
        w_big, w_small = _permute_weight(w_in_t, l)
        hg = (SSD_GROUPS, SSD_HG)
        zeros_r = jnp.zeros((SSD_GROUPS, LANE - 2 * SSD_HG), F32)
        dt_bias = jnp.concatenate([ssd_dt_bias_f[l].reshape(hg), ssd_dt_bias_b[l].reshape(hg), zeros_r], axis=1)
        a_neg = jnp.concatenate([-jnp.exp(ssd_a_log_f[l]).reshape(hg), -jnp.exp(ssd_a_log_b[l]).reshape(hg),
                                 zeros_r], axis=1)
        prow3 = jnp.concatenate([dt_bias[:, None], a_neg[:, None], jnp.zeros((SSD_GROUPS, 6, LANE), F32)], axis=1)
        prow = prow3.reshape(SSD_GROUPS * 8, LANE)
        pcol = jnp.transpose(prow3[:, :, :SMALL_T_ROWS], (0, 2, 1))
        drow = jnp.repeat(ssd_d[l], SSD_HEAD_DIM)[None, :]
        hk = (GLA_HEADS, 1, GLA_DK)
        a2 = jnp.concatenate([_pad_rows(gla_a2_f[l], SM_GAF, LANE).reshape((LANE,) + hk),
                              _pad_rows(gla_a2_b[l], SM_GAB, LANE).reshape((LANE,) + hk)],
                             axis=2).reshape(LANE, -1).astype(BF16)
        a2_bias = jnp.concatenate([gla_a2_bias_f[l].reshape(hk), gla_a2_bias_b[l].reshape(hk)],
                                  axis=1).reshape(1, -1)
        na_bound, na_flag = _na_score_bound(na_rpb[l], na_q_norm_w[l], na_k_norm_w[l])
        table = _na_bias_rows(na_rpb[l] * LOG2E - na_bound[:, None, None])
        q_row = jnp.tile(na_q_norm_w[l] * (NA_HEAD_DIM ** -0.5 * LOG2E), NA_HEADS)
        k_row = jnp.tile(na_k_norm_w[l], NA_HEADS)
        qkw = jnp.concatenate([q_row[None], k_row[None], jnp.zeros((6, NA_W), F32)], axis=0)

        u2, us2, ust = _inproj(x2, norm_mix_w[l][None, :], w_big, w_small, qkw)
        u3 = u2.reshape(B, S, U_WIDTH)
        us3 = us2.reshape(B, S, SMALL_W)
        xbc = _conv(u3, ssd_conv_w[l], ssd_conv_b[l][None, :])
        y_ssd = _ssd(xbc, u3, us3, ust, prow, pcol, drow, ssd_norm_w[l][None, :])
        y_gla = _gla(u3, us3, a2, a2_bias, gla_norm_w[l][None, :])
        y_na = _na(u3, table, na_flag)
        x2 = _merge(x2, y_ssd.reshape(T, -1), y_gla.reshape(T, -1), y_na.reshape(T, -1), u2,
                    w_branch_ssd[l].astype(BF16), w_branch_gla[l].astype(BF16),
                    w_branch_na[l].astype(BF16), w_out[l].astype(BF16))
        x2 = _mlp(x2, norm_mlp_w[l][None, :], w_ff1[l].astype(BF16), w_ff2[l].astype(BF16))
    return x2.reshape(B, S, D)
```

```python
import functools
import math

import jax
import jax.numpy as jnp
import numpy as np
from jax import lax
from jax.experimental import pallas as pl
from jax.experimental.pallas import tpu as pltpu

F32 = jnp.float32
BF16 = jnp.bfloat16

EPS = 1e-6
D_MODEL = 1024
GRID_W = 64

SSD_HEADS = 16
SSD_HEAD_DIM = 64
SSD_D_INNER = 1024
SSD_GROUPS = 2
SSD_STATE = 128
SSD_CONV = 5
SSD_CONV_DIM = 1536
SSD_CHUNK = 128
CONV_IN_BLOCKS = 3
CONV_TILE = 256
CONV_ROWS = 128
SSD_CHUNKS_PER_STEP = 4
SSD_OUT_UNROLL = 16
SSD_GROUP_W = SSD_D_INNER // SSD_GROUPS
SSD_HG = SSD_HEADS // SSD_GROUPS

GLA_HEADS = 4
GLA_DK = 128
GLA_DV = 256
GLA_KEY_W = 512
GLA_VAL_W = 1024
GLA_GATE_RANK = 16
GLA_GATE_NORM = 16.0
GLA_CHUNK = 64
GLA_BLOCKS_PER_STEP = 4
GLA_OUT_UNROLL = 16

NA_HEADS = 16
NA_HEAD_DIM = 64
NA_W = 1024
NA_WIN_H = 8
NA_WIN_W = 16
NA_BATCH_PER_STEP = 4
NA_ROWS_PER_STEP = 32
NA_ROWS_PER_STEP_EXACT = 2
LOG2E = 1.4426950408889634
NA_BOUND_SLACK = 0.02
NA_MAX_BOUND_GAP = 90.0

N_BRANCH = 3
D_FF = 4096

IN_SIZES = (SSD_D_INNER, SSD_CONV_DIM, SSD_HEADS, SSD_HEADS,
            GLA_KEY_W, GLA_KEY_W, GLA_VAL_W, GLA_VAL_W, GLA_GATE_RANK, GLA_GATE_RANK,
            NA_W, NA_W, NA_W, N_BRANCH * D_MODEL)
_IN_OFF = np.concatenate([[0], np.cumsum(IN_SIZES)])
(_O_Z, _O_XBC, _O_DTF, _O_DTB, _O_GQ, _O_GK, _O_GV, _O_GG, _O_GAF, _O_GAB,
 _O_NQ, _O_NK, _O_NV, _O_GATE) = [int(v) for v in _IN_OFF[:-1]]

U_NQ = 0
U_NK = 1024
U_NV = 2048
U_GATE = 3072
U_Z = 6144
U_XBC = 7168
U_GQ = 8704
U_GK = 9216
U_GV = 9728
U_GG = 10752
U_WIDTH = 11776
INPROJ_COL_CHUNK = 1024
LANE = 128
VMEM_LIMIT = 56 * 1024 * 1024

SMALL_W = SSD_GROUPS * LANE
SM_DTF, SM_DTB, SM_GAF, SM_GAB = 0, 8, 16, 32
SMALL_T_ROWS = 2 * SSD_HG

_BIG_SEGS = ((_O_NQ, 6144), (_O_Z, 2560), (_O_GQ, 3072))


def _cparams(sem, vmem=VMEM_LIMIT):
    return pltpu.CompilerParams(dimension_semantics=sem, vmem_limit_bytes=vmem)


def _sigmoid(x):
    return 1.0 / (1.0 + jnp.exp2(x * (-LOG2E)))


def _silu(x):
    return x * _sigmoid(x)


def _softplus(x):
    return jnp.maximum(x, 0.0) + jnp.log1p(jnp.exp(-jnp.abs(x)))


def _log2_sigmoid(x):
    t = x * (-LOG2E)
    return -(jnp.maximum(t, 0.0) + jnp.log2(1.0 + jnp.exp2(-jnp.abs(t))))


def _nt(a, b):
    return lax.dot_general(a, b, (((1,), (1,)), ((), ())), preferred_element_type=F32)


def _tn(a, b):
    return lax.dot_general(a, b, (((0,), (0,)), ((), ())), preferred_element_type=F32)


def _dot(a, b):
    return jnp.dot(a, b, preferred_element_type=F32)


def _iota2(shape, dim):
    return lax.broadcasted_iota(jnp.int32, shape, dim)


def _wperm_kernel(wt_ref, o_ref, ws_ref):
    off = 0
    for a, n in _BIG_SEGS:
        o_ref[off:off + n, :] = wt_ref[0, a:a + n, :].astype(BF16)
        off += n
    ws_ref[...] = jnp.zeros_like(ws_ref)
    for g in range(SSD_GROUPS):
        base, h0 = g * LANE, g * SSD_HG
        ws_ref[base + SM_DTF:base + SM_DTF + SSD_HG, :] = wt_ref[0, _O_DTF + h0:_O_DTF + h0 + SSD_HG, :]
        ws_ref[base + SM_DTB:base + SM_DTB + SSD_HG, :] = wt_ref[0, _O_DTB + h0:_O_DTB + h0 + SSD_HG, :]
    ws_ref[SM_GAF:SM_GAF + GLA_GATE_RANK, :] = wt_ref[0, _O_GAF:_O_GAF + GLA_GATE_RANK, :]
    ws_ref[SM_GAB:SM_GAB + GLA_GATE_RANK, :] = wt_ref[0, _O_GAB:_O_GAB + GLA_GATE_RANK, :]


def _permute_weight(w_t_all, layer, tc=256):
    _, n_in, d = w_t_all.shape
    return pl.pallas_call(
        _wperm_kernel,
        grid=(d // tc,),
        in_specs=[pl.BlockSpec((1, n_in, tc), lambda i: (layer, 0, i))],
        out_specs=[
            pl.BlockSpec((U_WIDTH, tc), lambda i: (0, i)),
            pl.BlockSpec((SMALL_W, tc), lambda i: (0, i)),
        ],
        out_shape=[
            jax.ShapeDtypeStruct((U_WIDTH, d), BF16),
            jax.ShapeDtypeStruct((SMALL_W, d), F32),
        ],
        compiler_params=_cparams(("parallel",)),
        name="w_permute",
    )(w_t_all)


def _head_rms(r, w_row):
    G = 2 * LANE
    er = _iota2((G, G), 0) // NA_HEAD_DIM
    ec = _iota2((G, G), 1) // NA_HEAD_DIM
    e_blk = (er == ec).astype(BF16)
    outs = []
    for a in range(0, r.shape[1], G):
        x = r[:, a:a + G]
        ms = _dot((x * x).astype(BF16), e_blk) * (1.0 / NA_HEAD_DIM)
        outs.append(x * lax.rsqrt(ms + EPS))
    return jnp.concatenate(outs, axis=1) * w_row


def _inproj_kernel(x_ref, nw_ref, w_ref, ws_ref, qkw_ref, u_ref, us_ref, ust_ref, h_ref):
    tn = u_ref.shape[1]
    chunks = [(a, min(a + INPROJ_COL_CHUNK, tn)) for a in range(0, tn, INPROJ_COL_CHUNK)]

    @pl.when(pl.program_id(1) == 0)
    def _():
        x = x_ref[...]
        ms = jnp.mean(x * x, axis=-1, keepdims=True)
        h = (x * lax.rsqrt(ms + EPS) * nw_ref[...]).astype(BF16)
        h_ref[...] = h
        ws = ws_ref[...].astype(BF16)
        us_ref[...] = _nt(h, ws)
        dt_rows = jnp.concatenate([ws[g * LANE:g * LANE + SMALL_T_ROWS] for g in range(SSD_GROUPS)], axis=0)
        ust_ref[...] = _nt(dt_rows, h)
        for c, (a, b) in enumerate(chunks):
            r = _nt(h, w_ref[a:b, :])
            if c < 2:
                r = _head_rms(r, qkw_ref[c:c + 1, :])
            u_ref[:, a:b] = r.astype(BF16)

    @pl.when(pl.program_id(1) != 0)
    def _():
        for a, b in chunks:
            u_ref[:, a:b] = _nt(h_ref[...], w_ref[a:b, :]).astype(BF16)


def _inproj(x2, nw, w_big, w_small, qkw, tm=1024, tn=U_WIDTH // 4):
    assert (U_NQ, U_NK) == (0, INPROJ_COL_CHUNK) and NA_W == INPROJ_COL_CHUNK
    T = x2.shape[0]
    return pl.pallas_call(
        _inproj_kernel,
        grid=(T // tm, U_WIDTH // tn),
        in_specs=[
            pl.BlockSpec((tm, D_MODEL), lambda i, j: (i, 0)),
            pl.BlockSpec((1, D_MODEL), lambda i, j: (0, 0)),
            pl.BlockSpec((tn, D_MODEL), lambda i, j: (j, 0)),
            pl.BlockSpec((SMALL_W, D_MODEL), lambda i, j: (0, 0)),
            pl.BlockSpec((8, NA_W), lambda i, j: (0, 0)),
        ],
        out_specs=[
            pl.BlockSpec((tm, tn), lambda i, j: (i, j)),
            pl.BlockSpec((tm, SMALL_W), lambda i, j: (i, 0)),
            pl.BlockSpec((SSD_GROUPS * SMALL_T_ROWS, tm), lambda i, j: (0, i)),
        ],
        out_shape=[
            jax.ShapeDtypeStruct((T, U_WIDTH), BF16),
            jax.ShapeDtypeStruct((T, SMALL_W), F32),
            jax.ShapeDtypeStruct((SSD_GROUPS * SMALL_T_ROWS, T), F32),
        ],
        scratch_shapes=[pltpu.VMEM((tm, D_MODEL), BF16)],
        compiler_params=_cparams(("parallel", "arbitrary")),
        name="inproj",
    )(x2, nw, w_big, w_small, qkw)


def _conv_kernel(*refs):
    u_refs, (w_ref, b_ref, o_ref, xp_ref) = refs[:CONV_IN_BLOCKS], refs[CONV_IN_BLOCKS:]
    S, wb = u_refs[0].shape[1], u_refs[0].shape[2]
    C = o_ref.shape[2]
    R = CONV_ROWS
    tc = CONV_TILE
    pad = SSD_CONV // 2
    xp_ref[0:R, :] = jnp.zeros((R, C), BF16)
    xp_ref[R + S:R + S + R, :] = jnp.zeros((R, C), BF16)
    for i, u_ref in enumerate(u_refs):
        xp_ref[R:R + S, i * wb:(i + 1) * wb] = u_ref[0]
    t_i = _iota2((R, 2 * R), 0)
    j_i = _iota2((R, 2 * R), 1)
    side = [k for k in range(SSD_CONV) if k != pad]
    shifts = jnp.concatenate([(j_i == t_i + R // 2 + (k - pad)).astype(BF16) for k in side], axis=0)
    for blk in range(S // R):
        for c0 in range(0, C, tc):
            cs = slice(c0, c0 + tc)
            win = xp_ref[blk * R + R // 2:blk * R + R // 2 + 2 * R, cs]
            sh = _dot(shifts, win)
            acc = b_ref[:, cs] + w_ref[pad:pad + 1, cs] * xp_ref[R + blk * R:R + (blk + 1) * R, cs].astype(F32)
            for n, k in enumerate(side):
                acc = acc + w_ref[k:k + 1, cs] * sh[n * R:(n + 1) * R]
            o_ref[0, blk * R:(blk + 1) * R, cs] = _silu(acc).astype(BF16)


def _conv(u3, conv_w, conv_b):
    B, S, _ = u3.shape
    wb = SSD_CONV_DIM // CONV_IN_BLOCKS
    assert U_XBC % wb == 0 and wb % CONV_TILE == 0
    off = U_XBC // wb
    in_blocks = [pl.BlockSpec((1, S, wb), functools.partial(lambda b, i: (b, 0, off + i), i=i))
                 for i in range(CONV_IN_BLOCKS)]
    return pl.pallas_call(
        _conv_kernel,
        grid=(B,),
        in_specs=in_blocks + [
            pl.BlockSpec((SSD_CONV, SSD_CONV_DIM), lambda b: (0, 0)),
            pl.BlockSpec((1, SSD_CONV_DIM), lambda b: (0, 0)),
        ],
        out_specs=pl.BlockSpec((1, S, SSD_CONV_DIM), lambda b: (b, 0, 0)),
        out_shape=jax.ShapeDtypeStruct((B, S, SSD_CONV_DIM), BF16),
        scratch_shapes=[pltpu.VMEM((S + 2 * CONV_ROWS, SSD_CONV_DIM), BF16)],
        compiler_params=_cparams(("parallel",)),
        name="ssd_conv",
    )(*([u3] * CONV_IN_BLOCKS), conv_w, conv_b)


def _split_hi_lo(v):
    hi = v.astype(BF16)
    lo = (v - hi.astype(F32)).astype(BF16)
    return jnp.concatenate([hi, lo], axis=1)


def _split3(v, axis):
    hi = v.astype(BF16)
    r1 = v - hi.astype(F32)
    mid = r1.astype(BF16)
    lo = (r1 - mid.astype(F32)).astype(BF16)
    return jnp.concatenate([hi, mid, lo], axis=axis)


def _ssd_kernel(x_ref, b_ref, c_ref, z_ref, us_ref, ust_ref, prow_ref, pcol_ref, drow_ref, nw_ref,
                o_ref, acc_ref, cumc_ref, cumr_ref, wdt_ref, dec_ref, xs_ref, decx_ref, sst_ref, st_ref):
    L = SSD_CHUNK
    S = x_ref.shape[1]
    nc = S // L
    W = SSD_GROUP_W
    HG = SSD_HG
    R = 2 * HG

    ii = _iota2((L, L), 0)
    jj = _iota2((L, L), 1)
    tril = ii >= jj
    triu = jj >= ii
    tril_b = tril.astype(BF16)
    triu_b = triu.astype(BF16)

    er = _iota2((LANE, 2 * W), 0)
    ec = _iota2((LANE, 2 * W), 1)
    e = (er == jnp.where(ec < W, ec // SSD_HEAD_DIM, (ec - W) // SSD_HEAD_DIM + HG)).astype(BF16)
    e2 = jnp.concatenate([e, e], axis=0)

    bias_row = prow_ref[0:1, :]
    a_row = prow_ref[1:2, :]
    bias_col = pcol_ref[0, :, 0:1]
    a_col = pcol_ref[0, :, 1:2]

    lane_fwd = _iota2((L, LANE), 1) < HG
    row_fwd = _iota2((R, L), 0) < HG
    lane_half = _iota2((L, LANE), 1) < SSD_HEAD_DIM

    def decay_sums(c):
        rs = pl.ds(pl.multiple_of(c * L, L), L)
        dt_c = _softplus(us_ref[0, rs, :] + bias_row)
        a_c = dt_c * a_row
        dt_r = _softplus(ust_ref[:, rs] + bias_col)
        a_r = dt_r * a_col
        return dt_c, a_c, _dot(tril_b, _split3(a_c, 1)), dt_r, a_r, _dot(_split3(a_r, 0), triu_b)

    def decay_store(c, sums):
        rs = pl.ds(pl.multiple_of(c * L, L), L)
        dt_c, a_c, pp, dt_r, a_r, pr = sums
        p_c = pp[:, 0:LANE] + pp[:, LANE:2 * LANE] + pp[:, 2 * LANE:3 * LANE]
        tot_c = p_c[L - 1:L, :]
        cum_c = jnp.where(lane_fwd, p_c, tot_c - p_c + a_c)
        cumc_ref[rs, :] = cum_c * LOG2E
        wdt_ref[rs, :] = jnp.exp(tot_c - cum_c) * dt_c
        dec_ref[c] = jnp.broadcast_to(jnp.exp(tot_c), (16, LANE))
        p_r = pr[0:R] + pr[R:2 * R] + pr[2 * R:3 * R]
        tot_r = p_r[:, L - 1:L]
        cum_r = jnp.where(row_fwd, p_r, tot_r - p_r + a_r)
        cumr_ref[:, rs] = (cum_r - jnp.log(dt_r)) * LOG2E

    def intra_chunk(c):
        rs = pl.ds(pl.multiple_of(c * L, L), L)
        x_c = x_ref[0, rs, :]
        b_c = b_ref[0, rs, :]
        cb = _nt(c_ref[0, rs, :], b_c)
        ex = _dot(_split_hi_lo(jnp.concatenate([wdt_ref[rs, :], dec_ref[c]], axis=0)), e2)
        decx_ref[c] = ex[L:L + 8]
        xf = x_c.astype(F32)
        xw = jnp.concatenate([xf * ex[0:L, 0:W], xf * ex[0:L, W:2 * W]], axis=1).astype(BF16)
        xs_ref[c] = _tn(b_c, xw)
        cum_c = cumc_ref[rs, :]
        cum_r = cumr_ref[:, rs]
        for hp in range(HG // 2):
            ms = []
            for hh in range(2):
                hf = 2 * hp + hh
                hb = HG + 2 * hp + hh
                decf = jnp.exp2(jnp.where(tril, cum_c[:, hf:hf + 1] - cum_r[hf:hf + 1, :], -jnp.inf))
                decb = jnp.exp2(jnp.where(triu, cum_c[:, hb:hb + 1] - cum_r[hb:hb + 1, :], -jnp.inf))
                ms.append((cb * (decf + decb)).astype(BF16))
            m2 = jnp.concatenate(ms, axis=1)
            xp = x_c[:, hp * LANE:(hp + 1) * LANE]
            xz = jnp.zeros_like(xp)
            x2 = jnp.concatenate([jnp.where(lane_half, xp, xz), jnp.where(lane_half, xz, xp)], axis=0)
            acc_ref[rs, hp * LANE:(hp + 1) * LANE] = _dot(m2, x2)

    G = SSD_CHUNKS_PER_STEP
    n_steps = nc // G

    def fused_body(i, carry):
        sums = [decay_sums(G * (i + 1) + j) for j in range(G)]
        for j in range(G):
            intra_chunk(G * i + j)
        for j in range(G):
            decay_store(G * (i + 1) + j, sums[j])
        return carry

    for j in range(G):
        decay_store(j, decay_sums(j))
    lax.fori_loop(0, n_steps - 1, fused_body, 0, unroll=True)
    for j in range(G):
        intra_chunk(G * (n_steps - 1) + j)

    st_ref[...] = jnp.zeros_like(st_ref)

    def state_body(i, carry):
        for d, c in enumerate((i, nc - 1 - i)):
            ls = slice(d * W, (d + 1) * W)
            s_prev = st_ref[:, ls]
            sst_ref[c, :, ls] = s_prev.astype(BF16)
            st_ref[:, ls] = s_prev * decx_ref[c, 0:1, ls] + xs_ref[c, :, ls]
        return carry

    lax.fori_loop(0, nc, state_body, 0, unroll=True)

    def out_body(c, carry):
        rs = pl.ds(pl.multiple_of(c * L, L), L)
        yo = _dot(c_ref[0, rs, :], sst_ref[c])
        sc = _dot(_split_hi_lo(jnp.exp2(cumc_ref[rs, :])), e2)
        xf = x_ref[0, rs, :].astype(F32)
        y = acc_ref[rs, :] + yo[:, 0:W] * sc[:, 0:W] + yo[:, W:2 * W] * sc[:, W:2 * W] + xf * drow_ref[...]
        y = y * _silu(z_ref[0, rs, :].astype(F32))
        ms = jnp.mean(y * y, axis=-1, keepdims=True)
        o_ref[0, rs, :] = (y * lax.rsqrt(ms + EPS) * nw_ref[...]).astype(BF16)
        return carry

    lax.fori_loop(0, nc, out_body, 0, unroll=SSD_OUT_UNROLL)


def _ssd(xbc, u3, us3, ust, prow, pcol, drow, nw):
    B, S, _ = xbc.shape
    W = SSD_GROUP_W
    N = SSD_STATE
    nc = S // SSD_CHUNK
    return pl.pallas_call(
        _ssd_kernel,
        grid=(B, SSD_GROUPS),
        in_specs=[
            pl.BlockSpec((1, S, W), lambda b, g: (b, 0, g)),
            pl.BlockSpec((1, S, N), lambda b, g: (b, 0, SSD_D_INNER // N + g)),
            pl.BlockSpec((1, S, N), lambda b, g: (b, 0, SSD_D_INNER // N + SSD_GROUPS + g)),
            pl.BlockSpec((1, S, W), lambda b, g: (b, 0, U_Z // W + g)),
            pl.BlockSpec((1, S, LANE), lambda b, g: (b, 0, g)),
            pl.BlockSpec((SMALL_T_ROWS, S), lambda b, g: (g, b)),
            pl.BlockSpec((8, LANE), lambda b, g: (g, 0)),
            pl.BlockSpec((1, SMALL_T_ROWS, 8), lambda b, g: (g, 0, 0)),
            pl.BlockSpec((1, W), lambda b, g: (0, g)),
            pl.BlockSpec((1, W), lambda b, g: (0, g)),
        ],
        out_specs=pl.BlockSpec((1, S, W), lambda b, g: (b, 0, g)),
        out_shape=jax.ShapeDtypeStruct((B, S, SSD_D_INNER), BF16),
        scratch_shapes=[
            pltpu.VMEM((S, W), F32),
            pltpu.VMEM((S, LANE), F32),
            pltpu.VMEM((SMALL_T_ROWS, S), F32),
            pltpu.VMEM((S, LANE), F32),
            pltpu.VMEM((nc, 16, LANE), F32),
            pltpu.VMEM((nc, N, 2 * W), F32),
            pltpu.VMEM((nc, 8, 2 * W), F32),
            pltpu.VMEM((nc, N, 2 * W), BF16),
            pltpu.VMEM((N, 2 * W), F32),
        ],
        compiler_params=_cparams(("parallel", "parallel")),
        name="ssd_scan",
    )(xbc, xbc, xbc, u3, us3, ust, prow, pcol, drow, nw)


def _gla_kernel(q_ref, k_ref, v_ref, gg_ref, us_ref, a2_ref, bias_ref, nw_ref,
                o_ref, acc_ref, g_ref, qd_ref, kd_ref, kdp_ref, qcat_ref, x_ref, dec_ref, sst_ref):
    L = GLA_CHUNK
    BL = 2 * L
    DK = GLA_DK
    S = q_ref.shape[1]
    nb = S // BL
    scale = DK ** -0.5

    ii = _iota2((BL, BL), 0)
    jj = _iota2((BL, BL), 1)
    same = (ii // L) == (jj // L)
    masks = (same & (ii >= jj), same & (jj >= ii))
    tri2 = masks[0].astype(BF16)
    par_row = _iota2((BL, DK), 0) // L

    ga = us_ref[0].astype(BF16)
    g_ref[...] = _log2_sigmoid(_dot(ga, a2_ref[...]) + bias_ref[...]) * (1.0 / GLA_GATE_NORM)

    def decay_sums(i):
        g = g_ref[pl.ds(pl.multiple_of(i * BL, BL), BL), :]
        hi = g.astype(BF16)
        r1 = g - hi.astype(F32)
        mid = r1.astype(BF16)
        lo = (r1 - mid.astype(F32)).astype(BF16)
        return g, _dot(tri2, jnp.concatenate([hi, mid, lo], axis=1))

    def decay_block(i, sums):
        rs = pl.ds(pl.multiple_of(i * BL, BL), BL)
        g, pp = sums
        p = pp[:, 0:2 * DK] + pp[:, 2 * DK:4 * DK] + pp[:, 4 * DK:6 * DK]
        q_c = q_ref[0, rs, :].astype(F32) * scale
        k_c = k_ref[0, rs, :].astype(F32)
        zero = jnp.zeros((BL, DK), BF16)
        for d in range(2):
            p_d = p[:, d * DK:(d + 1) * DK]
            tot = jnp.where(par_row == 0, p_d[L - 1:L, :], p_d[BL - 1:BL, :])
            b = p_d if d == 0 else tot - p_d + g[:, DK:]
            qd = (q_c * jnp.exp2(b)).astype(BF16)
            kdec = (k_c * jnp.exp2(tot - b)).astype(BF16)
            qd_ref[d, rs, :] = qd
            kd_ref[d, rs, :] = (k_c * jnp.exp2(-b)).astype(BF16)
            for par in range(2):
                sel = par_row == par
                kdp_ref[d, rs, par * DK:(par + 1) * DK] = jnp.where(sel, kdec, zero)
                qcat_ref[rs, (2 * d + par) * DK:(2 * d + par + 1) * DK] = jnp.where(sel, qd, zero)
                last = (par + 1) * L - 1
                dec_ref[d, 2 * i + par] = jnp.broadcast_to(jnp.exp2(p_d[last:last + 1, :]), (8, DK))

    G = GLA_BLOCKS_PER_STEP
    n_groups = nb // G

    def intra_group(i):
        blks = [G * i + j for j in range(G)]
        rss = [pl.ds(pl.multiple_of(b * BL, BL), BL) for b in blks]
        vs = [v_ref[0, rs, :] for rs in rss]
        atts = [[_nt(qd_ref[d, rs, :], kd_ref[d, rs, :]) for d in range(2)] for rs in rss]
        for j, b in enumerate(blks):
            for d in range(2):
                x_ref[d, b] = _tn(vs[j], kdp_ref[d, rss[j], :])
        for j in range(G):
            att = jnp.where(masks[0], atts[j][0], 0.0) + jnp.where(masks[1], atts[j][1], 0.0)
            acc_ref[rss[j], :] = _dot(att.astype(BF16), vs[j])

    def fused_body(i, carry):
        sums = [decay_sums(G * (i + 1) + j) for j in range(G)]
        intra_group(i)
        for j in range(G):
            decay_block(G * (i + 1) + j, sums[j])
        return carry

    for j in range(G):
        decay_block(j, decay_sums(j))
    lax.fori_loop(0, n_groups - 1, fused_body, 0, unroll=True)
    intra_group(n_groups - 1)

    def state_body(d):
        def body(i, s):
            b = i if d == 0 else nb - 1 - i
            for par in ((0, 1) if d == 0 else (1, 0)):
                lane0 = (2 * d + par) * DK
                sst_ref[b, :, lane0:lane0 + DK] = s.astype(BF16)
                s = s * dec_ref[d, 2 * b + par, 0:1, :] + x_ref[d, b, :, par * DK:(par + 1) * DK]
            return s
        return body

    for d in range(2):
        lax.fori_loop(0, nb, state_body(d), jnp.zeros((GLA_DV, DK), F32), unroll=True)

    def out_body(b, carry):
        rs = pl.ds(pl.multiple_of(b * BL, BL), BL)
        o = acc_ref[rs, :] + _nt(qcat_ref[rs, :], sst_ref[b])
        ms = jnp.mean(o * o, axis=-1, keepdims=True)
        o = o * lax.rsqrt(ms + EPS) * nw_ref[...]
        o_ref[0, rs, :] = (o * _silu(gg_ref[0, rs, :].astype(F32))).astype(BF16)
        return carry

    lax.fori_loop(0, nb, out_body, 0, unroll=GLA_OUT_UNROLL)


def _gla(u3, us3, a2, bias, nw):
    B, S, _ = u3.shape
    DK, DV = GLA_DK, GLA_DV
    nc = S // GLA_CHUNK
    nb = nc // 2
    return pl.pallas_call(
        _gla_kernel,
        grid=(B, GLA_HEADS),
        in_specs=[
            pl.BlockSpec((1, S, DK), lambda b, h: (b, 0, U_GQ // DK + h)),
            pl.BlockSpec((1, S, DK), lambda b, h: (b, 0, U_GK // DK + h)),
            pl.BlockSpec((1, S, DV), lambda b, h: (b, 0, U_GV // DV + h)),
            pl.BlockSpec((1, S, DV), lambda b, h: (b, 0, U_GG // DV + h)),
            pl.BlockSpec((1, S, LANE), lambda b, h: (b, 0, 0)),
            pl.BlockSpec((LANE, 2 * DK), lambda b, h: (0, h)),
            pl.BlockSpec((1, 2 * DK), lambda b, h: (0, h)),
            pl.BlockSpec((1, DV), lambda b, h: (0, 0)),
        ],
        out_specs=pl.BlockSpec((1, S, DV), lambda b, h: (b, 0, h)),
        out_shape=jax.ShapeDtypeStruct((B, S, GLA_VAL_W), BF16),
        scratch_shapes=[
            pltpu.VMEM((S, DV), F32),
            pltpu.VMEM((S, 2 * DK), F32),
            pltpu.VMEM((2, S, DK), BF16),
            pltpu.VMEM((2, S, DK), BF16),
            pltpu.VMEM((2, S, 2 * DK), BF16),
            pltpu.VMEM((S, 4 * DK), BF16),
            pltpu.VMEM((2, nb, DV, 2 * DK), F32),
            pltpu.VMEM((2, nc, 8, DK), F32),
            pltpu.VMEM((nb, DV, 4 * DK), BF16),
        ],
        compiler_params=_cparams(("parallel", "parallel")),
        name="gla_scan",
    )(u3, u3, u3, u3, us3, a2, bias, nw)


def _na_bias_rows(rpb):
    H, R, C = rpb.shape
    n_pos = GRID_W - NA_WIN_W
    n_neg = GRID_W - NA_WIN_W + 2
    ext = jnp.concatenate([rpb[:, :, NA_WIN_W - 1:], jnp.repeat(rpb[:, :, C - 1:], n_pos, axis=2),
                           jnp.repeat(rpb[:, :, 0:1], n_neg, axis=2), rpb[:, :, 1:NA_WIN_W - 1]], axis=2)
    return jnp.pad(ext, ((0, 0), (0, 1), (0, 0))).reshape(H // 2, 2, R + 1, 2 * GRID_W)


def _na_kernel(flag_ref, q_ref, k_ref, v_ref, ext_ref, o_ref, vx_ref, tab_ref):
    NB, S, _ = q_ref.shape
    rows = S // GRID_W
    total = NB * rows
    win_h = NA_WIN_H
    nk = win_h * GRID_W

    vx_ref[:, :, 0:LANE] = v_ref[...]
    vx_ref[:, :, LANE:2 * LANE] = jnp.ones((NB, S, LANE), BF16)
    bound_ok = flag_ref[0] != 0

    first_q = _iota2((GRID_W, LANE), 1) < NA_HEAD_DIM

    @pl.when(pl.program_id(1) == 0)
    def _():
        q_col = _iota2((GRID_W, 2 * GRID_W), 0)
        k_col = _iota2((GRID_W, 2 * GRID_W), 1) % GRID_W
        w_start = jnp.clip(q_col - NA_WIN_W // 2, 0, GRID_W - NA_WIN_W)
        in_window = (k_col >= w_start) & (k_col < w_start + NA_WIN_W)
        low = _iota2((GRID_W, 2 * GRID_W), 1) < GRID_W
        for hd in range(2):
            def skewed(rr, shift):
                row = jnp.broadcast_to(ext_ref[0, hd, rr:rr + 1, :], (GRID_W, 2 * GRID_W))
                return pltpu.roll(row, shift, axis=1, stride=1, stride_axis=0)

            for e in range(2 * NA_WIN_H - 2):
                t = jnp.where(low, skewed(e, 0), skewed(e + 1, GRID_W))
                tab_ref[e, hd * GRID_W:(hd + 1) * GRID_W, :] = jnp.where(in_window, t, -jnp.inf)

    def locate(r):
        bi = r // rows
        rl = r - bi * rows
        r0 = jnp.clip(rl - win_h // 2, 0, rows - win_h)
        return bi, pl.multiple_of(rl * GRID_W, GRID_W), rl - r0, pl.multiple_of(r0 * GRID_W, GRID_W)

    def scores(r):
        bi, q0, delta, k0 = locate(r)
        q = q_ref[bi, pl.ds(q0, GRID_W), :]
        zero = jnp.zeros_like(q)
        qs = jnp.concatenate([jnp.where(first_q, q, zero), jnp.where(first_q, zero, q)], axis=0)
        rr0 = (win_h - 1) - delta
        bias = jnp.concatenate([tab_ref[rr0 + w] for w in range(0, win_h, 2)], axis=1)
        return _nt(qs, k_ref[bi, pl.ds(k0, nk), :]) + bias

    def probs_exact(s):
        return jnp.exp2(s - jnp.max(s, axis=-1, keepdims=True)).astype(BF16)

    def probs_bounded(r):
        return jnp.exp2(scores(r)).astype(BF16)

    def attend(r, p):
        bi, q0, _, k0 = locate(r)
        ox = _dot(p, vx_ref[bi, pl.ds(k0, nk), :])
        o = ox[:, 0:LANE] / ox[:, LANE:2 * LANE]
        o_ref[bi, pl.ds(q0, GRID_W), :] = jnp.where(first_q, o[:GRID_W], o[GRID_W:]).astype(BF16)

    @pl.when(bound_ok)
    def _():
        U = NA_ROWS_PER_STEP

        def row_body(i, p_prev):
            r = i * U
            for j in range(U):
                attend(r - U + j, p_prev[j])
            return tuple(probs_bounded(r + j) for j in range(U))

        p_last = lax.fori_loop(1, total // U, row_body, tuple(probs_bounded(j) for j in range(U)), unroll=True)
        for j in range(U):
            attend(total - U + j, p_last[j])

    @pl.when(jnp.logical_not(bound_ok))
    def _():
        U = NA_ROWS_PER_STEP_EXACT

        def row_body(i, carry):
            s_cur, p_prev = carry
            r = i * U
            for j in range(U):
                attend(r - U + j, p_prev[j])
            p = tuple(probs_exact(s) for s in s_cur)
            s_next = tuple(scores(jnp.minimum(r + U + j, total - 1)) for j in range(U))
            return s_next, p

        p0 = tuple(probs_exact(scores(j)) for j in range(U))
        s1 = tuple(scores(U + j) for j in range(U))
        _, p_last = lax.fori_loop(1, total // U, row_body, (s1, p0))
        for j in range(U):
            attend(total - U + j, p_last[j])


def _na_score_bound(rpb, q_norm_w, k_norm_w):
    H = rpb.shape[0]
    qk = (NA_HEAD_DIM ** 0.5 * LOG2E * (1.0 + NA_BOUND_SLACK)) * jnp.max(jnp.abs(q_norm_w)) * jnp.max(jnp.abs(k_norm_w))
    b_max = jnp.max(rpb.reshape(H, -1), axis=1) * LOG2E
    b_self = rpb[:, NA_WIN_H - 1, NA_WIN_W - 1] * LOG2E
    bound = qk + b_max
    flag = jnp.all(bound - (b_self - qk) <= NA_MAX_BOUND_GAP)
    return bound, flag.astype(jnp.int32).reshape(1)


def _na(u3, bias_rows, flag):
    B, S, _ = u3.shape
    assert S // GRID_W >= NA_WIN_H and NA_WIN_H % 2 == 0 and 2 * GRID_W == LANE
    nb = math.gcd(B, NA_BATCH_PER_STEP)
    assert (nb * (S // GRID_W)) % NA_ROWS_PER_STEP == 0
    return pl.pallas_call(
        _na_kernel,
        grid=(NA_HEADS // 2, B // nb),
        in_specs=[
            pl.BlockSpec(memory_space=pltpu.SMEM),
            pl.BlockSpec((nb, S, LANE), lambda h, b: (b, 0, U_NQ // LANE + h)),
            pl.BlockSpec((nb, S, LANE), lambda h, b: (b, 0, U_NK // LANE + h)),
            pl.BlockSpec((nb, S, LANE), lambda h, b: (b, 0, U_NV // LANE + h)),
            pl.BlockSpec((1, 2, 2 * NA_WIN_H, 2 * GRID_W), lambda h, b: (h, 0, 0, 0)),
        ],
        out_specs=pl.BlockSpec((nb, S, LANE), lambda h, b: (b, 0, h)),
        out_shape=jax.ShapeDtypeStruct((B, S, NA_W), BF16),
        scratch_shapes=[
            pltpu.VMEM((nb, S, 2 * LANE), BF16),
            pltpu.VMEM((2 * NA_WIN_H - 2, 2 * GRID_W, 2 * GRID_W), F32),
        ],
        compiler_params=_cparams(("parallel", "arbitrary")),
        name="na_attn",
    )(flag, u3, u3, u3, bias_rows)


def _merge_kernel(x_ref, ys_ref, yg_ref, yn_ref, gate_ref, ws_ref, wg_ref, wn_ref, wo_ref, o_ref):
    D = D_MODEL
    mixed = _sigmoid(gate_ref[:, 0:D].astype(F32)) * _dot(ys_ref[...], ws_ref[...])
    mixed += _sigmoid(gate_ref[:, D:2 * D].astype(F32)) * _dot(yg_ref[...], wg_ref[...])
    mixed += _sigmoid(gate_ref[:, 2 * D:3 * D].astype(F32)) * _dot(yn_ref[...], wn_ref[...])
    o_ref[...] = x_ref[...] + _dot(mixed.astype(BF16), wo_ref[...])


def _merge(x2, ys, yg, yn, u2, ws, wg, wn, wo, tm=512):
    T = x2.shape[0]
    D = D_MODEL
    row = lambda i: (i, 0)
    fixed = lambda i: (0, 0)
    return pl.pallas_call(
        _merge_kernel,
        grid=(T // tm,),
        in_specs=[
            pl.BlockSpec((tm, D), row),
            pl.BlockSpec((tm, D), row),
            pl.BlockSpec((tm, D), row),
            pl.BlockSpec((tm, D), row),
            pl.BlockSpec((tm, N_BRANCH * D), lambda i: (i, U_GATE // (N_BRANCH * D))),
            pl.BlockSpec((D, D), fixed),
            pl.BlockSpec((D, D), fixed),
            pl.BlockSpec((D, D), fixed),
            pl.BlockSpec((D, D), fixed),
        ],
        out_specs=pl.BlockSpec((tm, D), row),
        out_shape=jax.ShapeDtypeStruct((T, D), F32),
        compiler_params=_cparams(("parallel",)),
        name="merge",
    )(x2, ys, yg, yn, u2, ws, wg, wn, wo)


def _mlp_kernel(x_ref, nw_ref, w1_ref, w2_ref, o_ref, *, tf):
    x = x_ref[...]
    ms = jnp.mean(x * x, axis=-1, keepdims=True)
    h = (x * lax.rsqrt(ms + EPS) * nw_ref[...]).astype(BF16)
    acc = x
    for f in range(D_FF // tf):
        a = jnp.maximum(_dot(h, w1_ref[:, f * tf:(f + 1) * tf]), 0.0)
        acc = acc + _dot((a * a).astype(BF16), w2_ref[f * tf:(f + 1) * tf, :])
    o_ref[...] = acc


def _mlp(x2, nw, w1, w2, tm=1024, tf=1024):
    T = x2.shape[0]
    D = D_MODEL
    resident = pl.Buffered(1)
    return pl.pallas_call(
        functools.partial(_mlp_kernel, tf=tf),
        grid=(T // tm,),
        in_specs=[
            pl.BlockSpec((tm, D), lambda i: (i, 0)),
            pl.BlockSpec((1, D), lambda i: (0, 0)),
            pl.BlockSpec((D, D_FF), lambda i: (0, 0), pipeline_mode=resident),
            pl.BlockSpec((D_FF, D), lambda i: (0, 0), pipeline_mode=resident),
        ],
        out_specs=pl.BlockSpec((tm, D), lambda i: (i, 0)),
        out_shape=jax.ShapeDtypeStruct((T, D), F32),
        compiler_params=_cparams(("parallel",)),
        name="mlp",
    )(x2, nw, w1, w2)


def _pad_rows(w, start, total):
    return jnp.zeros((total, w.shape[1]), w.dtype).at[start:start + w.shape[0]].set(w)


def kernel(x, norm_mix_w, w_in, ssd_conv_w, ssd_conv_b, ssd_dt_bias_f, ssd_dt_bias_b, ssd_a_log_f,
           ssd_a_log_b, ssd_d, ssd_norm_w, gla_a2_f, gla_a2_bias_f, gla_a2_b, gla_a2_bias_b,
           gla_norm_w, na_q_norm_w, na_k_norm_w, na_rpb, w_branch_ssd, w_branch_gla, w_branch_na,
           w_out, norm_mlp_w, w_ff1, w_ff2):
    B, S, D = x.shape
    T = B * S
    depth = w_in.shape[0]
    x2 = x.reshape(T, D)
    w_in_t = jnp.swapaxes(w_in, 1, 2)
    for l in range(depth):
        w_big, w_small = _permute_weight(w_in_t, l)
        hg = (SSD_GROUPS, SSD_HG)
        zeros_r = jnp.zeros((SSD_GROUPS, LANE - 2 * SSD_HG), F32)
        dt_bias = jnp.concatenate([ssd_dt_bias_f[l].reshape(hg), ssd_dt_bias_b[l].reshape(hg), zeros_r], axis=1)
        a_neg = jnp.concatenate([-jnp.exp(ssd_a_log_f[l]).reshape(hg), -jnp.exp(ssd_a_log_b[l]).reshape(hg),
                                 zeros_r], axis=1)
        prow3 = jnp.concatenate([dt_bias[:, None], a_neg[:, None], jnp.zeros((SSD_GROUPS, 6, LANE), F32)], axis=1)
        prow = prow3.reshape(SSD_GROUPS * 8, LANE)
        pcol = jnp.transpose(prow3[:, :, :SMALL_T_ROWS], (0, 2, 1))
        drow = jnp.repeat(ssd_d[l], SSD_HEAD_DIM)[None, :]
        hk = (GLA_HEADS, 1, GLA_DK)
        a2 = jnp.concatenate([_pad_rows(gla_a2_f[l], SM_GAF, LANE).reshape((LANE,) + hk),
                              _pad_rows(gla_a2_b[l], SM_GAB, LANE).reshape((LANE,) + hk)],
                             axis=2).reshape(LANE, -1).astype(BF16)
        a2_bias = jnp.concatenate([gla_a2_bias_f[l].reshape(hk), gla_a2_bias_b[l].reshape(hk)],
                                  axis=1).reshape(1, -1)
        na_bound, na_flag = _na_score_bound(na_rpb[l], na_q_norm_w[l], na_k_norm_w[l])
        table = _na_bias_rows(na_rpb[l] * LOG2E - na_bound[:, None, None])
        q_row = jnp.tile(na_q_norm_w[l] * (NA_HEAD_DIM ** -0.5 * LOG2E), NA_HEADS)
        k_row = jnp.tile(na_k_norm_w[l], NA_HEADS)
        qkw = jnp.concatenate([q_row[None], k_row[None], jnp.zeros((6, NA_W), F32)], axis=0)

        u2, us2, ust = _inproj(x2, norm_mix_w[l][None, :], w_big, w_small, qkw)
        u3 = u2.reshape(B, S, U_WIDTH)
        us3 = us2.reshape(B, S, SMALL_W)
        xbc = _conv(u3, ssd_conv_w[l], ssd_conv_b[l][None, :])
        y_ssd = _ssd(xbc, u3, us3, ust, prow, pcol, drow, ssd_norm_w[l][None, :])
        y_gla = _gla(u3, us3, a2, a2_bias, gla_norm_w[l][None, :])
        y_na = _na(u3, table, na_flag)
        x2 = _merge(x2, y_ssd.reshape(T, -1), y_gla.reshape(T, -1), y_na.reshape(T, -1), u2,
                    w_branch_ssd[l].astype(BF16), w_branch_gla[l].astype(BF16),
                    w_branch_na[l].astype(BF16), w_out[l].astype(BF16))
        x2 = _mlp(x2, norm_mlp_w[l][None, :], w_ff1[l].astype(BF16), w_ff2[l].astype(BF16))
    return x2.reshape(B, S, D)
```

```python
import functools
import math

import jax
import jax.numpy as jnp
import numpy as np
from jax import lax
from jax.experimental import pallas as pl
from jax.experimental.pallas import tpu as pltpu

F32 = jnp.float32
BF16 = jnp.bfloat16

EPS = 1e-6
D_MODEL = 1024
GRID_W = 64

SSD_HEADS = 16
SSD_HEAD_DIM = 64
SSD_D_INNER = 1024
SSD_GROUPS = 2
SSD_STATE = 128
SSD_CONV = 5
SSD_CONV_DIM = 1536
SSD_CHUNK = 128
CONV_IN_BLOCKS = 3
CONV_TILE = 256
CONV_ROWS = 128
SSD_CHUNKS_PER_STEP = 2
SSD_OUT_UNROLL = 16
SSD_GROUP_W = SSD_D_INNER // SSD_GROUPS
SSD_HG = SSD_HEADS // SSD_GROUPS

GLA_HEADS = 4
GLA_DK = 128
GLA_DV = 256
GLA_KEY_W = 512
GLA_VAL_W = 1024
GLA_GATE_RANK = 16
GLA_GATE_NORM = 16.0
GLA_CHUNK = 64
GLA_BLOCKS_PER_STEP = 2
GLA_OUT_UNROLL = 16

NA_HEADS = 16
NA_HEAD_DIM = 64
NA_W = 1024
NA_WIN_H = 8
NA_WIN_W = 16
NA_BATCH_PER_STEP = 4
NA_ROWS_PER_STEP = 32
NA_ROWS_PER_STEP_EXACT = 2
LOG2E = 1.4426950408889634
NA_BOUND_SLACK = 0.02
NA_MAX_BOUND_GAP = 90.0

N_BRANCH = 3
D_FF = 4096

IN_SIZES = (SSD_D_INNER, SSD_CONV_DIM, SSD_HEADS, SSD_HEADS,
            GLA_KEY_W, GLA_KEY_W, GLA_VAL_W, GLA_VAL_W, GLA_GATE_RANK, GLA_GATE_RANK,
            NA_W, NA_W, NA_W, N_BRANCH * D_MODEL)
_IN_OFF = np.concatenate([[0], np.cumsum(IN_SIZES)])
(_O_Z, _O_XBC, _O_DTF, _O_DTB, _O_GQ, _O_GK, _O_GV, _O_GG, _O_GAF, _O_GAB,
 _O_NQ, _O_NK, _O_NV, _O_GATE) = [int(v) for v in _IN_OFF[:-1]]

U_NQ = 0
U_NK = 1024
U_NV = 2048
U_GATE = 3072
U_Z = 6144
U_XBC = 7168
U_GQ = 8704
U_GK = 9216
U_GV = 9728
U_GG = 10752
U_WIDTH = 11776
INPROJ_COL_CHUNK = 1024
LANE = 128
VMEM_LIMIT = 56 * 1024 * 1024

SMALL_W = SSD_GROUPS * LANE
SM_DTF, SM_DTB, SM_GAF, SM_GAB = 0, 8, 16, 32
SMALL_T_ROWS = 2 * SSD_HG

_BIG_SEGS = ((_O_NQ, 6144), (_O_Z, 2560), (_O_GQ, 3072))


def _cparams(sem, vmem=VMEM_LIMIT):
    return pltpu.CompilerParams(dimension_semantics=sem, vmem_limit_bytes=vmem)


def _sigmoid(x):
    return 1.0 / (1.0 + jnp.exp2(x * (-LOG2E)))


def _silu(x):
    return x * _sigmoid(x)


def _softplus(x):
    return jnp.maximum(x, 0.0) + jnp.log1p(jnp.exp(-jnp.abs(x)))


def _log2_sigmoid(x):
    t = x * (-LOG2E)
    return -(jnp.maximum(t, 0.0) + jnp.log2(1.0 + jnp.exp2(-jnp.abs(t))))


def _nt(a, b):
    return lax.dot_general(a, b, (((1,), (1,)), ((), ())), preferred_element_type=F32)


def _tn(a, b):
    return lax.dot_general(a, b, (((0,), (0,)), ((), ())), preferred_element_type=F32)


def _dot(a, b):
    return jnp.dot(a, b, preferred_element_type=F32)


def _iota2(shape, dim):
    return lax.broadcasted_iota(jnp.int32, shape, dim)


def _wperm_kernel(wt_ref, o_ref, ws_ref):
    off = 0
    for a, n in _BIG_SEGS:
        o_ref[off:off + n, :] = wt_ref[0, a:a + n, :].astype(BF16)
        off += n
    ws_ref[...] = jnp.zeros_like(ws_ref)
    for g in range(SSD_GROUPS):
        base, h0 = g * LANE, g * SSD_HG
        ws_ref[base + SM_DTF:base + SM_DTF + SSD_HG, :] = wt_ref[0, _O_DTF + h0:_O_DTF + h0 + SSD_HG, :]
        ws_ref[base + SM_DTB:base + SM_DTB + SSD_HG, :] = wt_ref[0, _O_DTB + h0:_O_DTB + h0 + SSD_HG, :]
    ws_ref[SM_GAF:SM_GAF + GLA_GATE_RANK, :] = wt_ref[0, _O_GAF:_O_GAF + GLA_GATE_RANK, :]
    ws_ref[SM_GAB:SM_GAB + GLA_GATE_RANK, :] = wt_ref[0, _O_GAB:_O_GAB + GLA_GATE_RANK, :]


def _permute_weight(w_t_all, layer, tc=256):
    _, n_in, d = w_t_all.shape
    return pl.pallas_call(
        _wperm_kernel,
        grid=(d // tc,),
        in_specs=[pl.BlockSpec((1, n_in, tc), lambda i: (layer, 0, i))],
        out_specs=[
            pl.BlockSpec((U_WIDTH, tc), lambda i: (0, i)),
            pl.BlockSpec((SMALL_W, tc), lambda i: (0, i)),
        ],
        out_shape=[
            jax.ShapeDtypeStruct((U_WIDTH, d), BF16),
            jax.ShapeDtypeStruct((SMALL_W, d), F32),
        ],
        compiler_params=_cparams(("parallel",)),
        name="w_permute",
    )(w_t_all)


def _head_rms(r, w_row):
    G = 2 * LANE
    er = _iota2((G, G), 0) // NA_HEAD_DIM
    ec = _iota2((G, G), 1) // NA_HEAD_DIM
    e_blk = (er == ec).astype(BF16)
    outs = []
    for a in range(0, r.shape[1], G):
        x = r[:, a:a + G]
        ms = _dot((x * x).astype(BF16), e_blk) * (1.0 / NA_HEAD_DIM)
        outs.append(x * lax.rsqrt(ms + EPS))
    return jnp.concatenate(outs, axis=1) * w_row


def _inproj_kernel(x_ref, nw_ref, w_ref, ws_ref, qkw_ref, u_ref, us_ref, ust_ref, h_ref):
    tn = u_ref.shape[1]
    chunks = [(a, min(a + INPROJ_COL_CHUNK, tn)) for a in range(0, tn, INPROJ_COL_CHUNK)]

    @pl.when(pl.program_id(1) == 0)
    def _():
        x = x_ref[...]
        ms = jnp.mean(x * x, axis=-1, keepdims=True)
        h = (x * lax.rsqrt(ms + EPS) * nw_ref[...]).astype(BF16)
        h_ref[...] = h
        ws = ws_ref[...].astype(BF16)
        us_ref[...] = _nt(h, ws)
        dt_rows = jnp.concatenate([ws[g * LANE:g * LANE + SMALL_T_ROWS] for g in range(SSD_GROUPS)], axis=0)
        ust_ref[...] = _nt(dt_rows, h)
        for c, (a, b) in enumerate(chunks):
            r = _nt(h, w_ref[a:b, :])
            if c < 2:
                r = _head_rms(r, qkw_ref[c:c + 1, :])
            u_ref[:, a:b] = r.astype(BF16)

    @pl.when(pl.program_id(1) != 0)
    def _():
        for a, b in chunks:
            u_ref[:, a:b] = _nt(h_ref[...], w_ref[a:b, :]).astype(BF16)


def _inproj(x2, nw, w_big, w_small, qkw, tm=1024, tn=U_WIDTH // 4):
    assert (U_NQ, U_NK) == (0, INPROJ_COL_CHUNK) and NA_W == INPROJ_COL_CHUNK
    T = x2.shape[0]
    return pl.pallas_call(
        _inproj_kernel,
        grid=(T // tm, U_WIDTH // tn),
        in_specs=[
            pl.BlockSpec((tm, D_MODEL), lambda i, j: (i, 0)),
            pl.BlockSpec((1, D_MODEL), lambda i, j: (0, 0)),
            pl.BlockSpec((tn, D_MODEL), lambda i, j: (j, 0)),
            pl.BlockSpec((SMALL_W, D_MODEL), lambda i, j: (0, 0)),
            pl.BlockSpec((8, NA_W), lambda i, j: (0, 0)),
        ],
        out_specs=[
            pl.BlockSpec((tm, tn), lambda i, j: (i, j)),
            pl.BlockSpec((tm, SMALL_W), lambda i, j: (i, 0)),
            pl.BlockSpec((SSD_GROUPS * SMALL_T_ROWS, tm), lambda i, j: (0, i)),
        ],
        out_shape=[
            jax.ShapeDtypeStruct((T, U_WIDTH), BF16),
            jax.ShapeDtypeStruct((T, SMALL_W), F32),
            jax.ShapeDtypeStruct((SSD_GROUPS * SMALL_T_ROWS, T), F32),
        ],
        scratch_shapes=[pltpu.VMEM((tm, D_MODEL), BF16)],
        compiler_params=_cparams(("parallel", "arbitrary")),
        name="inproj",
    )(x2, nw, w_big, w_small, qkw)


def _conv_kernel(*refs):
    u_refs, (w_ref, b_ref, o_ref, xp_ref) = refs[:CONV_IN_BLOCKS], refs[CONV_IN_BLOCKS:]
    S, wb = u_refs[0].shape[1], u_refs[0].shape[2]
    C = o_ref.shape[2]
    R = CONV_ROWS
    tc = CONV_TILE
    pad = SSD_CONV // 2
    xp_ref[0:R, :] = jnp.zeros((R, C), BF16)
    xp_ref[R + S:R + S + R, :] = jnp.zeros((R, C), BF16)
    for i, u_ref in enumerate(u_refs):
        xp_ref[R:R + S, i * wb:(i + 1) * wb] = u_ref[0]
    t_i = _iota2((R, 2 * R), 0)
    j_i = _iota2((R, 2 * R), 1)
    side = [k for k in range(SSD_CONV) if k != pad]
    shifts = jnp.concatenate([(j_i == t_i + R // 2 + (k - pad)).astype(BF16) for k in side], axis=0)
    for blk in range(S // R):
        for c0 in range(0, C, tc):
            cs = slice(c0, c0 + tc)
            win = xp_ref[blk * R + R // 2:blk * R + R // 2 + 2 * R, cs]
            sh = _dot(shifts, win)
            acc = b_ref[:, cs] + w_ref[pad:pad + 1, cs] * xp_ref[R + blk * R:R + (blk + 1) * R, cs].astype(F32)
            for n, k in enumerate(side):
                acc = acc + w_ref[k:k + 1, cs] * sh[n * R:(n + 1) * R]
            o_ref[0, blk * R:(blk + 1) * R, cs] = _silu(acc).astype(BF16)


def _conv(u3, conv_w, conv_b):
    B, S, _ = u3.shape
    wb = SSD_CONV_DIM // CONV_IN_BLOCKS
    assert U_XBC % wb == 0 and wb % CONV_TILE == 0
    off = U_XBC // wb
    in_blocks = [pl.BlockSpec((1, S, wb), functools.partial(lambda b, i: (b, 0, off + i), i=i))
                 for i in range(CONV_IN_BLOCKS)]
    return pl.pallas_call(
        _conv_kernel,
        grid=(B,),
        in_specs=in_blocks + [
            pl.BlockSpec((SSD_CONV, SSD_CONV_DIM), lambda b: (0, 0)),
            pl.BlockSpec((1, SSD_CONV_DIM), lambda b: (0, 0)),
        ],
        out_specs=pl.BlockSpec((1, S, SSD_CONV_DIM), lambda b: (b, 0, 0)),
        out_shape=jax.ShapeDtypeStruct((B, S, SSD_CONV_DIM), BF16),
        scratch_shapes=[pltpu.VMEM((S + 2 * CONV_ROWS, SSD_CONV_DIM), BF16)],
        compiler_params=_cparams(("parallel",)),
        name="ssd_conv",
    )(*([u3] * CONV_IN_BLOCKS), conv_w, conv_b)


def _split_hi_lo(v):
    hi = v.astype(BF16)
    lo = (v - hi.astype(F32)).astype(BF16)
    return jnp.concatenate([hi, lo], axis=1)


def _split3(v, axis):
    hi = v.astype(BF16)
    r1 = v - hi.astype(F32)
    mid = r1.astype(BF16)
    lo = (r1 - mid.astype(F32)).astype(BF16)
    return jnp.concatenate([hi, mid, lo], axis=axis)


def _ssd_kernel(x_ref, b_ref, c_ref, z_ref, ust_ref, prow_ref, pcol_ref, drow_ref, nw_ref,
                o_ref, acc_ref, cumc_ref, cumr_ref, wdt_ref, dec_ref, xs_ref, decx_ref, sst_ref, st_ref):
    L = SSD_CHUNK
    S = x_ref.shape[1]
    nc = S // L
    W = SSD_GROUP_W
    HG = SSD_HG
    R = 2 * HG

    ii = _iota2((L, L), 0)
    jj = _iota2((L, L), 1)
    tril = ii >= jj
    triu = jj >= ii
    tril_b = tril.astype(BF16)
    triu_b = triu.astype(BF16)

    er = _iota2((LANE, 2 * W), 0)
    ec = _iota2((LANE, 2 * W), 1)
    e = (er == jnp.where(ec < W, ec // SSD_HEAD_DIM, (ec - W) // SSD_HEAD_DIM + HG)).astype(BF16)
    e2 = jnp.concatenate([e, e], axis=0)

    a_row = prow_ref[1:2, :]
    bias_col = pcol_ref[0, :, 0:1]
    a_col = pcol_ref[0, :, 1:2]

    lane_fwd = _iota2((L, LANE), 1) < HG
    row_fwd = _iota2((R, L), 0) < HG
    lane_half = _iota2((L, LANE), 1) < SSD_HEAD_DIM

    def decay_sums(c):
        rs = pl.ds(pl.multiple_of(c * L, L), L)
        dt_r = _softplus(ust_ref[:, rs] + bias_col)
        a_r = dt_r * a_col
        dt_c = jnp.concatenate([dt_r, jnp.zeros((L - R, L), F32)], axis=0).T
        a_c = dt_c * a_row
        return dt_c, a_c, _dot(tril_b, _split3(a_c, 1)), dt_r, a_r, _dot(_split3(a_r, 0), triu_b)

    def decay_store(c, sums):
        rs = pl.ds(pl.multiple_of(c * L, L), L)
        dt_c, a_c, pp, dt_r, a_r, pr = sums
        p_c = pp[:, 0:LANE] + pp[:, LANE:2 * LANE] + pp[:, 2 * LANE:3 * LANE]
        tot_c = p_c[L - 1:L, :]
        cum_c = jnp.where(lane_fwd, p_c, tot_c - p_c + a_c)
        cumc_ref[rs, :] = cum_c * LOG2E
        wdt_ref[rs, :] = jnp.exp(tot_c - cum_c) * dt_c
        dec_ref[c] = jnp.broadcast_to(jnp.exp(tot_c), (16, LANE))
        p_r = pr[0:R] + pr[R:2 * R] + pr[2 * R:3 * R]
        tot_r = p_r[:, L - 1:L]
        cum_r = jnp.where(row_fwd, p_r, tot_r - p_r + a_r)
        cumr_ref[:, rs] = (cum_r - jnp.log(dt_r)) * LOG2E

    def intra_chunk(c):
        rs = pl.ds(pl.multiple_of(c * L, L), L)
        x_c = x_ref[0, rs, :]
        b_c = b_ref[0, rs, :]
        cb = _nt(c_ref[0, rs, :], b_c)
        ex = _dot(_split_hi_lo(jnp.concatenate([wdt_ref[rs, :], dec_ref[c]], axis=0)), e2)
        decx_ref[c] = ex[L:L + 8]
        xf = x_c.astype(F32)
        xw = jnp.concatenate([xf * ex[0:L, 0:W], xf * ex[0:L, W:2 * W]], axis=1).astype(BF16)
        xs_ref[c] = _tn(b_c, xw)
        cum_c = cumc_ref[rs, :]
        cum_r = cumr_ref[:, rs]
        for hp in range(HG // 2):
            ms = []
            for hh in range(2):
                hf = 2 * hp + hh
                hb = HG + 2 * hp + hh
                decf = jnp.exp2(jnp.where(tril, cum_c[:, hf:hf + 1] - cum_r[hf:hf + 1, :], -jnp.inf))
                decb = jnp.exp2(jnp.where(triu, cum_c[:, hb:hb + 1] - cum_r[hb:hb + 1, :], -jnp.inf))
                ms.append((cb * (decf + decb)).astype(BF16))
            m2 = jnp.concatenate(ms, axis=1)
            xp = x_c[:, hp * LANE:(hp + 1) * LANE]
            xz = jnp.zeros_like(xp)
            x2 = jnp.concatenate([jnp.where(lane_half, xp, xz), jnp.where(lane_half, xz, xp)], axis=0)
            acc_ref[rs, hp * LANE:(hp + 1) * LANE] = _dot(m2, x2)

    G = SSD_CHUNKS_PER_STEP
    n_steps = nc // G

    def fused_body(i, carry):
        sums = [decay_sums(G * (i + 1) + j) for j in range(G)]
        for j in range(G):
            intra_chunk(G * i + j)
        for j in range(G):
            decay_store(G * (i + 1) + j, sums[j])
        return carry

    for j in range(G):
        decay_store(j, decay_sums(j))
    lax.fori_loop(0, n_steps - 1, fused_body, 0, unroll=True)
    for j in range(G):
        intra_chunk(G * (n_steps - 1) + j)

    st_ref[...] = jnp.zeros_like(st_ref)

    def state_body(i, carry):
        for d, c in enumerate((i, nc - 1 - i)):
            ls = slice(d * W, (d + 1) * W)
            s_prev = st_ref[:, ls]
            sst_ref[c, :, ls] = s_prev.astype(BF16)
            st_ref[:, ls] = s_prev * decx_ref[c, 0:1, ls] + xs_ref[c, :, ls]
        return carry

    lax.fori_loop(0, nc, state_body, 0, unroll=True)

    def out_body(c, carry):
        rs = pl.ds(pl.multiple_of(c * L, L), L)
        yo = _dot(c_ref[0, rs, :], sst_ref[c])
        sc = _dot(_split_hi_lo(jnp.exp2(cumc_ref[rs, :])), e2)
        xf = x_ref[0, rs, :].astype(F32)
        y = acc_ref[rs, :] + yo[:, 0:W] * sc[:, 0:W] + yo[:, W:2 * W] * sc[:, W:2 * W] + xf * drow_ref[...]
        y = y * _silu(z_ref[0, rs, :].astype(F32))
        ms = jnp.mean(y * y, axis=-1, keepdims=True)
        o_ref[0, rs, :] = (y * lax.rsqrt(ms + EPS) * nw_ref[...]).astype(BF16)
        return carry

    lax.fori_loop(0, nc, out_body, 0, unroll=SSD_OUT_UNROLL)


def _ssd(xbc, u3, ust, prow, pcol, drow, nw):
    B, S, _ = xbc.shape
    W = SSD_GROUP_W
    N = SSD_STATE
    nc = S // SSD_CHUNK
    return pl.pallas_call(
        _ssd_kernel,
        grid=(B, SSD_GROUPS),
        in_specs=[
            pl.BlockSpec((1, S, W), lambda b, g: (b, 0, g)),
            pl.BlockSpec((1, S, N), lambda b, g: (b, 0, SSD_D_INNER // N + g)),
            pl.BlockSpec((1, S, N), lambda b, g: (b, 0, SSD_D_INNER // N + SSD_GROUPS + g)),
            pl.BlockSpec((1, S, W), lambda b, g: (b, 0, U_Z // W + g)),
            pl.BlockSpec((SMALL_T_ROWS, S), lambda b, g: (g, b)),
            pl.BlockSpec((8, LANE), lambda b, g: (g, 0)),
            pl.BlockSpec((1, SMALL_T_ROWS, 8), lambda b, g: (g, 0, 0)),
            pl.BlockSpec((1, W), lambda b, g: (0, g)),
            pl.BlockSpec((1, W), lambda b, g: (0, g)),
        ],
        out_specs=pl.BlockSpec((1, S, W), lambda b, g: (b, 0, g)),
        out_shape=jax.ShapeDtypeStruct((B, S, SSD_D_INNER), BF16),
        scratch_shapes=[
            pltpu.VMEM((S, W), F32),
            pltpu.VMEM((S, LANE), F32),
            pltpu.VMEM((SMALL_T_ROWS, S), F32),
            pltpu.VMEM((S, LANE), F32),
            pltpu.VMEM((nc, 16, LANE), F32),
            pltpu.VMEM((nc, N, 2 * W), F32),
            pltpu.VMEM((nc, 8, 2 * W), F32),
            pltpu.VMEM((nc, N, 2 * W), BF16),
            pltpu.VMEM((N, 2 * W), F32),
        ],
        compiler_params=_cparams(("parallel", "parallel")),
        name="ssd_scan",
    )(xbc, xbc, xbc, u3, ust, prow, pcol, drow, nw)


def _gla_kernel(q_ref, k_ref, v_ref, gg_ref, us_ref, a2_ref, bias_ref, nw_ref,
                o_ref, acc_ref, g_ref, qd_ref, kd_ref, kdp_ref, qcat_ref, x_ref, dec_ref, sst_ref):
    L = GLA_CHUNK
    BL = 2 * L
    DK = GLA_DK
    S = q_ref.shape[1]
    nb = S // BL
    scale = DK ** -0.5

    ii = _iota2((BL, BL), 0)
    jj = _iota2((BL, BL), 1)
    same = (ii // L) == (jj // L)
    masks = (same & (ii >= jj), same & (jj >= ii))
    tri2 = masks[0].astype(BF16)
    par_row = _iota2((BL, DK), 0) // L

    ga = us_ref[0].astype(BF16)
    g_ref[...] = _log2_sigmoid(_dot(ga, a2_ref[...]) + bias_ref[...]) * (1.0 / GLA_GATE_NORM)

    def decay_sums(i):
        g = g_ref[pl.ds(pl.multiple_of(i * BL, BL), BL), :]
        hi = g.astype(BF16)
        r1 = g - hi.astype(F32)
        mid = r1.astype(BF16)
        lo = (r1 - mid.astype(F32)).astype(BF16)
        return g, _dot(tri2, jnp.concatenate([hi, mid, lo], axis=1))

    def decay_block(i, sums):
        rs = pl.ds(pl.multiple_of(i * BL, BL), BL)
        g, pp = sums
        p = pp[:, 0:2 * DK] + pp[:, 2 * DK:4 * DK] + pp[:, 4 * DK:6 * DK]
        q_c = q_ref[0, rs, :].astype(F32) * scale
        k_c = k_ref[0, rs, :].astype(F32)
        zero = jnp.zeros((BL, DK), BF16)
        for d in range(2):
            p_d = p[:, d * DK:(d + 1) * DK]
            tot = jnp.where(par_row == 0, p_d[L - 1:L, :], p_d[BL - 1:BL, :])
            b = p_d if d == 0 else tot - p_d + g[:, DK:]
            qd = (q_c * jnp.exp2(b)).astype(BF16)
            kdec = (k_c * jnp.exp2(tot - b)).astype(BF16)
            qd_ref[d, rs, :] = qd
            kd_ref[d, rs, :] = (k_c * jnp.exp2(-b)).astype(BF16)
            for par in range(2):
                sel = par_row == par
                kdp_ref[d, rs, par * DK:(par + 1) * DK] = jnp.where(sel, kdec, zero)
                qcat_ref[rs, (2 * d + par) * DK:(2 * d + par + 1) * DK] = jnp.where(sel, qd, zero)
                last = (par + 1) * L - 1
                dec_ref[d, 2 * i + par] = jnp.broadcast_to(jnp.exp2(p_d[last:last + 1, :]), (8, DK))

    G = GLA_BLOCKS_PER_STEP
    n_groups = nb // G

    def intra_group(i):
        blks = [G * i + j for j in range(G)]
        rss = [pl.ds(pl.multiple_of(b * BL, BL), BL) for b in blks]
        vs = [v_ref[0, rs, :] for rs in rss]
        atts = [[_nt(qd_ref[d, rs, :], kd_ref[d, rs, :]) for d in range(2)] for rs in rss]
        for j, b in enumerate(blks):
            for d in range(2):
                x_ref[d, b] = _tn(vs[j], kdp_ref[d, rss[j], :])
        for j in range(G):
            att = jnp.where(masks[0], atts[j][0], 0.0) + jnp.where(masks[1], atts[j][1], 0.0)
            acc_ref[rss[j], :] = _dot(att.astype(BF16), vs[j])

    def fused_body(i, carry):
        sums = [decay_sums(G * (i + 1) + j) for j in range(G)]
        intra_group(i)
        for j in range(G):
            decay_block(G * (i + 1) + j, sums[j])
        return carry

    for j in range(G):
        decay_block(j, decay_sums(j))
    lax.fori_loop(0, n_groups - 1, fused_body, 0, unroll=True)
    intra_group(n_groups - 1)

    def state_body(d):
        def body(i, s):
            b = i if d == 0 else nb - 1 - i
            for par in ((0, 1) if d == 0 else (1, 0)):
                lane0 = (2 * d + par) * DK
                sst_ref[b, :, lane0:lane0 + DK] = s.astype(BF16)
                s = s * dec_ref[d, 2 * b + par, 0:1, :] + x_ref[d, b, :, par * DK:(par + 1) * DK]
            return s
        return body

    for d in range(2):
        lax.fori_loop(0, nb, state_body(d), jnp.zeros((GLA_DV, DK), F32), unroll=True)

    def out_body(b, carry):
        rs = pl.ds(pl.multiple_of(b * BL, BL), BL)
        o = acc_ref[rs, :] + _nt(qcat_ref[rs, :], sst_ref[b])
        ms = jnp.mean(o * o, axis=-1, keepdims=True)
        o = o * lax.rsqrt(ms + EPS) * nw_ref[...]
        o_ref[0, rs, :] = (o * _silu(gg_ref[0, rs, :].astype(F32))).astype(BF16)
        return carry

    lax.fori_loop(0, nb, out_body, 0, unroll=GLA_OUT_UNROLL)


def _gla(u3, us3, a2, bias, nw):
    B, S, _ = u3.shape
    DK, DV = GLA_DK, GLA_DV
    nc = S // GLA_CHUNK
    nb = nc // 2
    return pl.pallas_call(
        _gla_kernel,
        grid=(B, GLA_HEADS),
        in_specs=[
            pl.BlockSpec((1, S, DK), lambda b, h: (b, 0, U_GQ // DK + h)),
            pl.BlockSpec((1, S, DK), lambda b, h: (b, 0, U_GK // DK + h)),
            pl.BlockSpec((1, S, DV), lambda b, h: (b, 0, U_GV // DV + h)),
            pl.BlockSpec((1, S, DV), lambda b, h: (b, 0, U_GG // DV + h)),
            pl.BlockSpec((1, S, LANE), lambda b, h: (b, 0, 0)),
            pl.BlockSpec((LANE, 2 * DK), lambda b, h: (0, h)),
            pl.BlockSpec((1, 2 * DK), lambda b, h: (0, h)),
            pl.BlockSpec((1, DV), lambda b, h: (0, 0)),
        ],
        out_specs=pl.BlockSpec((1, S, DV), lambda b, h: (b, 0, h)),
        out_shape=jax.ShapeDtypeStruct((B, S, GLA_VAL_W), BF16),
        scratch_shapes=[
            pltpu.VMEM((S, DV), F32),
            pltpu.VMEM((S, 2 * DK), F32),
            pltpu.VMEM((2, S, DK), BF16),
            pltpu.VMEM((2, S, DK), BF16),
            pltpu.VMEM((2, S, 2 * DK), BF16),
            pltpu.VMEM((S, 4 * DK), BF16),
            pltpu.VMEM((2, nb, DV, 2 * DK), F32),
            pltpu.VMEM((2, nc, 8, DK), F32),
            pltpu.VMEM((nb, DV, 4 * DK), BF16),
        ],
        compiler_params=_cparams(("parallel", "parallel")),
        name="gla_scan",
    )(u3, u3, u3, u3, us3, a2, bias, nw)


def _na_bias_rows(rpb):
    H, R, C = rpb.shape
    n_pos = GRID_W - NA_WIN_W
    n_neg = GRID_W - NA_WIN_W + 2
    ext = jnp.concatenate([rpb[:, :, NA_WIN_W - 1:], jnp.repeat(rpb[:, :, C - 1:], n_pos, axis=2),
                           jnp.repeat(rpb[:, :, 0:1], n_neg, axis=2), rpb[:, :, 1:NA_WIN_W - 1]], axis=2)
    return jnp.pad(ext, ((0, 0), (0, 1), (0, 0))).reshape(H // 2, 2, R + 1, 2 * GRID_W)


def _na_kernel(flag_ref, q_ref, k_ref, v_ref, ext_ref, o_ref, vx_ref, tab_ref):
    NB, S, _ = q_ref.shape
    rows = S // GRID_W
    total = NB * rows
    win_h = NA_WIN_H
    nk = win_h * GRID_W

    vx_ref[:, :, 0:LANE] = v_ref[...]
    vx_ref[:, :, LANE:2 * LANE] = jnp.ones((NB, S, LANE), BF16)
    bound_ok = flag_ref[0] != 0

    first_q = _iota2((GRID_W, LANE), 1) < NA_HEAD_DIM

    @pl.when(pl.program_id(1) == 0)
    def _():
        q_col = _iota2((GRID_W, 2 * GRID_W), 0)
        k_col = _iota2((GRID_W, 2 * GRID_W), 1) % GRID_W
        w_start = jnp.clip(q_col - NA_WIN_W // 2, 0, GRID_W - NA_WIN_W)
        in_window = (k_col >= w_start) & (k_col < w_start + NA_WIN_W)
        low = _iota2((GRID_W, 2 * GRID_W), 1) < GRID_W
        for hd in range(2):
            def skewed(rr, shift):
                row = jnp.broadcast_to(ext_ref[0, hd, rr:rr + 1, :], (GRID_W, 2 * GRID_W))
                return pltpu.roll(row, shift, axis=1, stride=1, stride_axis=0)

            for e in range(2 * NA_WIN_H - 2):
                t = jnp.where(low, skewed(e, 0), skewed(e + 1, GRID_W))
                tab_ref[e, hd * GRID_W:(hd + 1) * GRID_W, :] = jnp.where(in_window, t, -jnp.inf)

    def locate(r):
        bi = r // rows
        rl = r - bi * rows
        r0 = jnp.clip(rl - win_h // 2, 0, rows - win_h)
        return bi, pl.multiple_of(rl * GRID_W, GRID_W), rl - r0, pl.multiple_of(r0 * GRID_W, GRID_W)

    def scores(r):
        bi, q0, delta, k0 = locate(r)
        q = q_ref[bi, pl.ds(q0, GRID_W), :]
        zero = jnp.zeros_like(q)
        qs = jnp.concatenate([jnp.where(first_q, q, zero), jnp.where(first_q, zero, q)], axis=0)
        rr0 = (win_h - 1) - delta
        bias = jnp.concatenate([tab_ref[rr0 + w] for w in range(0, win_h, 2)], axis=1)
        return _nt(qs, k_ref[bi, pl.ds(k0, nk), :]) + bias

    def probs_exact(s):
        return jnp.exp2(s - jnp.max(s, axis=-1, keepdims=True)).astype(BF16)

    def probs_bounded(r):
        return jnp.exp2(scores(r)).astype(BF16)

    def attend(r, p):
        bi, q0, _, k0 = locate(r)
        ox = _dot(p, vx_ref[bi, pl.ds(k0, nk), :])
        o = ox[:, 0:LANE] / ox[:, LANE:2 * LANE]
        o_ref[bi, pl.ds(q0, GRID_W), :] = jnp.where(first_q, o[:GRID_W], o[GRID_W:]).astype(BF16)

    @pl.when(bound_ok)
    def _():
        U = NA_ROWS_PER_STEP

        def row_body(i, p_prev):
            r = i * U
            for j in range(U):
                attend(r - U + j, p_prev[j])
            return tuple(probs_bounded(r + j) for j in range(U))

        p_last = lax.fori_loop(1, total // U, row_body, tuple(probs_bounded(j) for j in range(U)), unroll=True)
        for j in range(U):
            attend(total - U + j, p_last[j])

    @pl.when(jnp.logical_not(bound_ok))
    def _():
        U = NA_ROWS_PER_STEP_EXACT

        def row_body(i, carry):
            s_cur, p_prev = carry
            r = i * U
            for j in range(U):
                attend(r - U + j, p_prev[j])
            p = tuple(probs_exact(s) for s in s_cur)
            s_next = tuple(scores(jnp.minimum(r + U + j, total - 1)) for j in range(U))
            return s_next, p

        p0 = tuple(probs_exact(scores(j)) for j in range(U))
        s1 = tuple(scores(U + j) for j in range(U))
        _, p_last = lax.fori_loop(1, total // U, row_body, (s1, p0))
        for j in range(U):
            attend(total - U + j, p_last[j])


def _na_score_bound(rpb, q_norm_w, k_norm_w):
    H = rpb.shape[0]
    qk = (NA_HEAD_DIM ** 0.5 * LOG2E * (1.0 + NA_BOUND_SLACK)) * jnp.max(jnp.abs(q_norm_w)) * jnp.max(jnp.abs(k_norm_w))
    b_max = jnp.max(rpb.reshape(H, -1), axis=1) * LOG2E
    b_self = rpb[:, NA_WIN_H - 1, NA_WIN_W - 1] * LOG2E
    bound = qk + b_max
    flag = jnp.all(bound - (b_self - qk) <= NA_MAX_BOUND_GAP)
    return bound, flag.astype(jnp.int32).reshape(1)


def _na(u3, bias_rows, flag):
    B, S, _ = u3.shape
    assert S // GRID_W >= NA_WIN_H and NA_WIN_H % 2 == 0 and 2 * GRID_W == LANE
    nb = math.gcd(B, NA_BATCH_PER_STEP)
    assert (nb * (S // GRID_W)) % NA_ROWS_PER_STEP == 0
    return pl.pallas_call(
        _na_kernel,
        grid=(NA_HEADS // 2, B // nb),
        in_specs=[
            pl.BlockSpec(memory_space=pltpu.SMEM),
            pl.BlockSpec((nb, S, LANE), lambda h, b: (b, 0, U_NQ // LANE + h)),
            pl.BlockSpec((nb, S, LANE), lambda h, b: (b, 0, U_NK // LANE + h)),
            pl.BlockSpec((nb, S, LANE), lambda h, b: (b, 0, U_NV // LANE + h)),
            pl.BlockSpec((1, 2, 2 * NA_WIN_H, 2 * GRID_W), lambda h, b: (h, 0, 0, 0)),
        ],
        out_specs=pl.BlockSpec((nb, S, LANE), lambda h, b: (b, 0, h)),
        out_shape=jax.ShapeDtypeStruct((B, S, NA_W), BF16),
        scratch_shapes=[
            pltpu.VMEM((nb, S, 2 * LANE), BF16),
            pltpu.VMEM((2 * NA_WIN_H - 2, 2 * GRID_W, 2 * GRID_W), F32),
        ],
        compiler_params=_cparams(("parallel", "arbitrary")),
        name="na_attn",
    )(flag, u3, u3, u3, bias_rows)


def _merge_kernel(x_ref, ys_ref, yg_ref, yn_ref, gate_ref, ws_ref, wg_ref, wn_ref, wo_ref, o_ref):
    D = D_MODEL
    mixed = _sigmoid(gate_ref[:, 0:D].astype(F32)) * _dot(ys_ref[...], ws_ref[...])
    mixed += _sigmoid(gate_ref[:, D:2 * D].astype(F32)) * _dot(yg_ref[...], wg_ref[...])
    mixed += _sigmoid(gate_ref[:, 2 * D:3 * D].astype(F32)) * _dot(yn_ref[...], wn_ref[...])
    o_ref[...] = x_ref[...] + _dot(mixed.astype(BF16), wo_ref[...])


def _merge(x2, ys, yg, yn, u2, ws, wg, wn, wo, tm=512):
    T = x2.shape[0]
    D = D_MODEL
    row = lambda i: (i, 0)
    fixed = lambda i: (0, 0)
    return pl.pallas_call(
        _merge_kernel,
        grid=(T // tm,),
        in_specs=[
            pl.BlockSpec((tm, D), row),
            pl.BlockSpec((tm, D), row),
            pl.BlockSpec((tm, D), row),
            pl.BlockSpec((tm, D), row),
            pl.BlockSpec((tm, N_BRANCH * D), lambda i: (i, U_GATE // (N_BRANCH * D))),
            pl.BlockSpec((D, D), fixed),
            pl.BlockSpec((D, D), fixed),
            pl.BlockSpec((D, D), fixed),
            pl.BlockSpec((D, D), fixed),
        ],
        out_specs=pl.BlockSpec((tm, D), row),
        out_shape=jax.ShapeDtypeStruct((T, D), F32),
        compiler_params=_cparams(("parallel",)),
        name="merge",
    )(x2, ys, yg, yn, u2, ws, wg, wn, wo)


def _mlp_kernel(x_ref, nw_ref, w1_ref, w2_ref, o_ref, *, tf):
    x = x_ref[...]
    ms = jnp.mean(x * x, axis=-1, keepdims=True)
    h = (x * lax.rsqrt(ms + EPS) * nw_ref[...]).astype(BF16)
    acc = x
    for f in range(D_FF // tf):
        a = jnp.maximum(_dot(h, w1_ref[:, f * tf:(f + 1) * tf]), 0.0)
        acc = acc + _dot((a * a).astype(BF16), w2_ref[f * tf:(f + 1) * tf, :])
    o_ref[...] = acc


def _mlp(x2, nw, w1, w2, tm=1024, tf=1024):
    T = x2.shape[0]
    D = D_MODEL
    resident = pl.Buffered(1)
    return pl.pallas_call(
        functools.partial(_mlp_kernel, tf=tf),
        grid=(T // tm,),
        in_specs=[
            pl.BlockSpec((tm, D), lambda i: (i, 0)),
            pl.BlockSpec((1, D), lambda i: (0, 0)),
            pl.BlockSpec((D, D_FF), lambda i: (0, 0), pipeline_mode=resident),
            pl.BlockSpec((D_FF, D), lambda i: (0, 0), pipeline_mode=resident),
        ],
        out_specs=pl.BlockSpec((tm, D), lambda i: (i, 0)),
        out_shape=jax.ShapeDtypeStruct((T, D), F32),
        compiler_params=_cparams(("parallel",)),
        name="mlp",
    )(x2, nw, w1, w2)


def _pad_rows(w, start, total):
    return jnp.zeros((total, w.shape[1]), w.dtype).at[start:start + w.shape[0]].set(w)


def kernel(x, norm_mix_w, w_in, ssd_conv_w, ssd_conv_b, ssd_dt_bias_f, ssd_dt_bias_b, ssd_a_log_f,
           ssd_a_log_b, ssd_d, ssd_norm_w, gla_a2_f, gla_a2_bias_f, gla_a2_b, gla_a2_bias_b,
           gla_norm_w, na_q_norm_w, na_k_norm_w, na_rpb, w_branch_ssd, w_branch_gla, w_branch_na,
           w_out, norm_mlp_w, w_ff1, w_ff2):
    B, S, D = x.shape
    T = B * S
    depth = w_in.shape[0]
    x2 = x.reshape(T, D)
    w_in_t = jnp.swapaxes(w_in, 1, 2)
    for l in range(depth):
        w_big, w_small = _permute_weight(w_in_t, l)
        hg = (SSD_GROUPS, SSD_HG)
        zeros_r = jnp.zeros((SSD_GROUPS, LANE - 2 * SSD_HG), F32)
        dt_bias = jnp.concatenate([ssd_dt_bias_f[l].reshape(hg), ssd_dt_bias_b[l].reshape(hg), zeros_r], axis=1)
        a_neg = jnp.concatenate([-jnp.exp(ssd_a_log_f[l]).reshape(hg), -jnp.exp(ssd_a_log_b[l]).reshape(hg),
                                 zeros_r], axis=1)
        prow3 = jnp.concatenate([dt_bias[:, None], a_neg[:, None], jnp.zeros((SSD_GROUPS, 6, LANE), F32)], axis=1)
        prow = prow3.reshape(SSD_GROUPS * 8, LANE)
        pcol = jnp.transpose(prow3[:, :, :SMALL_T_ROWS], (0, 2, 1))
        drow = jnp.repeat(ssd_d[l], SSD_HEAD_DIM)[None, :]
        hk = (GLA_HEADS, 1, GLA_DK)
        a2 = jnp.concatenate([_pad_rows(gla_a2_f[l], SM_GAF, LANE).reshape((LANE,) + hk),
                              _pad_rows(gla_a2_b[l], SM_GAB, LANE).reshape((LANE,) + hk)],
                             axis=2).reshape(LANE, -1).astype(BF16)
        a2_bias = jnp.concatenate([gla_a2_bias_f[l].reshape(hk), gla_a2_bias_b[l].reshape(hk)],
                                  axis=1).reshape(1, -1)
        na_bound, na_flag = _na_score_bound(na_rpb[l], na_q_norm_w[l], na_k_norm_w[l])
        table = _na_bias_rows(na_rpb[l] * LOG2E - na_bound[:, None, None])
        q_row = jnp.tile(na_q_norm_w[l] * (NA_HEAD_DIM ** -0.5 * LOG2E), NA_HEADS)
        k_row = jnp.tile(na_k_norm_w[l], NA_HEADS)
        qkw = jnp.concatenate([q_row[None], k_row[None], jnp.zeros((6, NA_W), F32)], axis=0)

        u2, us2, ust = _inproj(x2, norm_mix_w[l][None, :], w_big, w_small, qkw)
        u3 = u2.reshape(B, S, U_WIDTH)
        us3 = us2.reshape(B, S, SMALL_W)
        xbc = _conv(u3, ssd_conv_w[l], ssd_conv_b[l][None, :])
        y_ssd = _ssd(xbc, u3, ust, prow, pcol, drow, ssd_norm_w[l][None, :])
        y_gla = _gla(u3, us3, a2, a2_bias, gla_norm_w[l][None, :])
        y_na = _na(u3, table, na_flag)
        x2 = _merge(x2, y_ssd.reshape(T, -1), y_gla.reshape(T, -1), y_na.reshape(T, -1), u2,
                    w_branch_ssd[l].astype(BF16), w_branch_gla[l].astype(BF16),
                    w_branch_na[l].astype(BF16), w_out[l].astype(BF16))
        x2 = _mlp(x2, norm_mlp_w[l][None, :], w_ff1[l].astype(BF16), w_ff2[l].astype(BF16))
    return x2.reshape(B, S, D)
```

```python
import functools
import math

import jax
import jax.numpy as jnp
import numpy as np
from jax import lax
from jax.experimental import pallas as pl
from jax.experimental.pallas import tpu as pltpu

F32 = jnp.float32
BF16 = jnp.bfloat16

EPS = 1e-6
D_MODEL = 1024
GRID_W = 64

SSD_HEADS = 16
SSD_HEAD_DIM = 64
SSD_D_INNER = 1024
SSD_GROUPS = 2
SSD_STATE = 128
SSD_CONV = 5
SSD_CONV_DIM = 1536
SSD_CHUNK = 128
CONV_IN_BLOCKS = 3
CONV_TILE = 256
CONV_ROWS = 128
SSD_CHUNKS_PER_STEP = 2
SSD_OUT_UNROLL = 16
SSD_GROUP_W = SSD_D_INNER // SSD_GROUPS
SSD_HG = SSD_HEADS // SSD_GROUPS

GLA_HEADS = 4
GLA_DK = 128
GLA_DV = 256
GLA_KEY_W = 512
GLA_VAL_W = 1024
GLA_GATE_RANK = 16
GLA_GATE_NORM = 16.0
GLA_CHUNK = 64
GLA_BLOCKS_PER_STEP = 2
GLA_OUT_UNROLL = 16

NA_HEADS = 16
NA_HEAD_DIM = 64
NA_W = 1024
NA_WIN_H = 8
NA_WIN_W = 16
NA_BATCH_PER_STEP = 4
NA_ROWS_PER_STEP = 32
NA_ROWS_PER_STEP_EXACT = 2
LOG2E = 1.4426950408889634
NA_BOUND_SLACK = 0.02
NA_MAX_BOUND_GAP = 90.0

N_BRANCH = 3
D_FF = 4096

IN_SIZES = (SSD_D_INNER, SSD_CONV_DIM, SSD_HEADS, SSD_HEADS,
            GLA_KEY_W, GLA_KEY_W, GLA_VAL_W, GLA_VAL_W, GLA_GATE_RANK, GLA_GATE_RANK,
            NA_W, NA_W, NA_W, N_BRANCH * D_MODEL)
_IN_OFF = np.concatenate([[0], np.cumsum(IN_SIZES)])
(_O_Z, _O_XBC, _O_DTF, _O_DTB, _O_GQ, _O_GK, _O_GV, _O_GG, _O_GAF, _O_GAB,
 _O_NQ, _O_NK, _O_NV, _O_GATE) = [int(v) for v in _IN_OFF[:-1]]

U_NQ = 0
U_NK = 1024
U_NV = 2048
U_GATE = 3072
U_Z = 6144
U_XBC = 7168
U_GQ = 8704
U_GK = 9216
U_GV = 9728
U_GG = 10752
U_WIDTH = 11776
INPROJ_COL_CHUNK = 1024
LANE = 128
VMEM_LIMIT = 56 * 1024 * 1024

SMALL_W = SSD_GROUPS * LANE
SM_DTF, SM_DTB, SM_GAF, SM_GAB = 0, 8, 16, 32
SMALL_T_ROWS = 2 * SSD_HG

_BIG_SEGS = ((_O_NQ, 6144), (_O_Z, 2560), (_O_GQ, 3072))


def _cparams(sem, vmem=VMEM_LIMIT):
    return pltpu.CompilerParams(dimension_semantics=sem, vmem_limit_bytes=vmem)


def _sigmoid(x):
    return 1.0 / (1.0 + jnp.exp2(x * (-LOG2E)))


def _silu(x):
    return x * _sigmoid(x)


def _softplus(x):
    return jnp.maximum(x, 0.0) + jnp.log1p(jnp.exp(-jnp.abs(x)))


def _log2_sigmoid(x):
    t = x * (-LOG2E)
    return -(jnp.maximum(t, 0.0) + jnp.log2(1.0 + jnp.exp2(-jnp.abs(t))))


def _nt(a, b):
    return lax.dot_general(a, b, (((1,), (1,)), ((), ())), preferred_element_type=F32)


def _tn(a, b):
    return lax.dot_general(a, b, (((0,), (0,)), ((), ())), preferred_element_type=F32)


def _dot(a, b):
    return jnp.dot(a, b, preferred_element_type=F32)


def _iota2(shape, dim):
    return lax.broadcasted_iota(jnp.int32, shape, dim)


def _wperm_kernel(wt_ref, o_ref, ws_ref):
    off = 0
    for a, n in _BIG_SEGS:
        o_ref[off:off + n, :] = wt_ref[0, a:a + n, :].astype(BF16)
        off += n
    ws_ref[...] = jnp.zeros_like(ws_ref)
    for g in range(SSD_GROUPS):
        base, h0 = g * LANE, g * SSD_HG
        ws_ref[base + SM_DTF:base + SM_DTF + SSD_HG, :] = wt_ref[0, _O_DTF + h0:_O_DTF + h0 + SSD_HG, :]
        ws_ref[base + SM_DTB:base + SM_DTB + SSD_HG, :] = wt_ref[0, _O_DTB + h0:_O_DTB + h0 + SSD_HG, :]
    ws_ref[SM_GAF:SM_GAF + GLA_GATE_RANK, :] = wt_ref[0, _O_GAF:_O_GAF + GLA_GATE_RANK, :]
    ws_ref[SM_GAB:SM_GAB + GLA_GATE_RANK, :] = wt_ref[0, _O_GAB:_O_GAB + GLA_GATE_RANK, :]


def _permute_weight(w_t_all, layer, tc=256):
    _, n_in, d = w_t_all.shape
    return pl.pallas_call(
        _wperm_kernel,
        grid=(d // tc,),
        in_specs=[pl.BlockSpec((1, n_in, tc), lambda i: (layer, 0, i))],
        out_specs=[
            pl.BlockSpec((U_WIDTH, tc), lambda i: (0, i)),
            pl.BlockSpec((SMALL_W, tc), lambda i: (0, i)),
        ],
        out_shape=[
            jax.ShapeDtypeStruct((U_WIDTH, d), BF16),
            jax.ShapeDtypeStruct((SMALL_W, d), F32),
        ],
        compiler_params=_cparams(("parallel",)),
        name="w_permute",
    )(w_t_all)


def _head_rms(r, w_row):
    G = 2 * LANE
    er = _iota2((G, G), 0) // NA_HEAD_DIM
    ec = _iota2((G, G), 1) // NA_HEAD_DIM
    e_blk = jnp.where(er == ec, 1.0 / NA_HEAD_DIM, 0.0).astype(BF16)
    outs = []
    for a in range(0, r.shape[1], G):
        x = r[:, a:a + G]
        ms = _dot((x * x).astype(BF16), e_blk)
        outs.append(x * lax.rsqrt(ms + EPS))
    return jnp.concatenate(outs, axis=1) * w_row


def _inproj_kernel(x_ref, nw_ref, w_ref, ws_ref, qkw_ref, u_ref, us_ref, ust_ref, h_ref):
    tn = u_ref.shape[1]
    chunks = [(a, min(a + INPROJ_COL_CHUNK, tn)) for a in range(0, tn, INPROJ_COL_CHUNK)]

    @pl.when(pl.program_id(1) == 0)
    def _():
        x = x_ref[...]
        ms = jnp.mean(x * x, axis=-1, keepdims=True)
        h = (x * lax.rsqrt(ms + EPS) * nw_ref[...]).astype(BF16)
        h_ref[...] = h
        ws = ws_ref[...].astype(BF16)
        us_ref[...] = _nt(h, ws)
        dt_rows = jnp.concatenate([ws[g * LANE:g * LANE + SMALL_T_ROWS] for g in range(SSD_GROUPS)], axis=0)
        ust_ref[...] = _nt(dt_rows, h)
        for c, (a, b) in enumerate(chunks):
            r = _nt(h, w_ref[a:b, :])
            if c < 2:
                r = _head_rms(r, qkw_ref[c:c + 1, :])
            u_ref[:, a:b] = r.astype(BF16)

    @pl.when(pl.program_id(1) != 0)
    def _():
        for a, b in chunks:
            u_ref[:, a:b] = _nt(h_ref[...], w_ref[a:b, :]).astype(BF16)


def _inproj(x2, nw, w_big, w_small, qkw, tm=1024, tn=U_WIDTH // 4):
    assert (U_NQ, U_NK) == (0, INPROJ_COL_CHUNK) and NA_W == INPROJ_COL_CHUNK
    T = x2.shape[0]
    return pl.pallas_call(
        _inproj_kernel,
        grid=(T // tm, U_WIDTH // tn),
        in_specs=[
            pl.BlockSpec((tm, D_MODEL), lambda i, j: (i, 0)),
            pl.BlockSpec((1, D_MODEL), lambda i, j: (0, 0)),
            pl.BlockSpec((tn, D_MODEL), lambda i, j: (j, 0)),
            pl.BlockSpec((SMALL_W, D_MODEL), lambda i, j: (0, 0)),
            pl.BlockSpec((8, NA_W), lambda i, j: (0, 0)),
        ],
        out_specs=[
            pl.BlockSpec((tm, tn), lambda i, j: (i, j)),
            pl.BlockSpec((tm, SMALL_W), lambda i, j: (i, 0)),
            pl.BlockSpec((SSD_GROUPS * SMALL_T_ROWS, tm), lambda i, j: (0, i)),
        ],
        out_shape=[
            jax.ShapeDtypeStruct((T, U_WIDTH), BF16),
            jax.ShapeDtypeStruct((T, SMALL_W), F32),
            jax.ShapeDtypeStruct((SSD_GROUPS * SMALL_T_ROWS, T), F32),
        ],
        scratch_shapes=[pltpu.VMEM((tm, D_MODEL), BF16)],
        compiler_params=_cparams(("parallel", "arbitrary")),
        name="inproj",
    )(x2, nw, w_big, w_small, qkw)


def _conv_kernel(*refs):
    u_refs, (w_ref, b_ref, o_ref, xp_ref) = refs[:CONV_IN_BLOCKS], refs[CONV_IN_BLOCKS:]
    S, wb = u_refs[0].shape[1], u_refs[0].shape[2]
    C = o_ref.shape[2]
    R = CONV_ROWS
    tc = CONV_TILE
    pad = SSD_CONV // 2
    xp_ref[0:R, :] = jnp.zeros((R, C), BF16)
    xp_ref[R + S:R + S + R, :] = jnp.zeros((R, C), BF16)
    for i, u_ref in enumerate(u_refs):
        xp_ref[R:R + S, i * wb:(i + 1) * wb] = u_ref[0]
    t_i = _iota2((R, 2 * R), 0)
    j_i = _iota2((R, 2 * R), 1)
    side = [k for k in range(SSD_CONV) if k != pad]
    shifts = jnp.concatenate([(j_i == t_i + R // 2 + (k - pad)).astype(BF16) for k in side], axis=0)
    for blk in range(S // R):
        for c0 in range(0, C, tc):
            cs = slice(c0, c0 + tc)
            win = xp_ref[blk * R + R // 2:blk * R + R // 2 + 2 * R, cs]
            sh = _dot(shifts, win)
            acc = b_ref[:, cs] + w_ref[pad:pad + 1, cs] * xp_ref[R + blk * R:R + (blk + 1) * R, cs].astype(F32)
            for n, k in enumerate(side):
                acc = acc + w_ref[k:k + 1, cs] * sh[n * R:(n + 1) * R]
            o_ref[0, blk * R:(blk + 1) * R, cs] = _silu(acc).astype(BF16)


def _conv(u3, conv_w, conv_b):
    B, S, _ = u3.shape
    wb = SSD_CONV_DIM // CONV_IN_BLOCKS
    assert U_XBC % wb == 0 and wb % CONV_TILE == 0
    off = U_XBC // wb
    in_blocks = [pl.BlockSpec((1, S, wb), functools.partial(lambda b, i: (b, 0, off + i), i=i))
                 for i in range(CONV_IN_BLOCKS)]
    return pl.pallas_call(
        _conv_kernel,
        grid=(B,),
        in_specs=in_blocks + [
            pl.BlockSpec((SSD_CONV, SSD_CONV_DIM), lambda b: (0, 0)),
            pl.BlockSpec((1, SSD_CONV_DIM), lambda b: (0, 0)),
        ],
        out_specs=pl.BlockSpec((1, S, SSD_CONV_DIM), lambda b: (b, 0, 0)),
        out_shape=jax.ShapeDtypeStruct((B, S, SSD_CONV_DIM), BF16),
        scratch_shapes=[pltpu.VMEM((S + 2 * CONV_ROWS, SSD_CONV_DIM), BF16)],
        compiler_params=_cparams(("parallel",)),
        name="ssd_conv",
    )(*([u3] * CONV_IN_BLOCKS), conv_w, conv_b)


def _split_hi_lo(v):
    hi = v.astype(BF16)
    lo = (v - hi.astype(F32)).astype(BF16)
    return jnp.concatenate([hi, lo], axis=1)


def _split3(v, axis):
    hi = v.astype(BF16)
    r1 = v - hi.astype(F32)
    mid = r1.astype(BF16)
    lo = (r1 - mid.astype(F32)).astype(BF16)
    return jnp.concatenate([hi, mid, lo], axis=axis)


def _ssd_kernel(x_ref, b_ref, c_ref, z_ref, ust_ref, prow_ref, pcol_ref, drow_ref, nw_ref,
                o_ref, acc_ref, cumc_ref, cumr_ref, wdt_ref, dec_ref, xs_ref, decx_ref, sst_ref, st_ref):
    L = SSD_CHUNK
    S = x_ref.shape[1]
    nc = S // L
    W = SSD_GROUP_W
    HG = SSD_HG
    R = 2 * HG

    ii = _iota2((L, L), 0)
    jj = _iota2((L, L), 1)
    tril = ii >= jj
    triu = jj >= ii
    tril_b = tril.astype(BF16)
    triu_b = triu.astype(BF16)

    er = _iota2((LANE, 2 * W), 0)
    ec = _iota2((LANE, 2 * W), 1)
    e = (er == jnp.where(ec < W, ec // SSD_HEAD_DIM, (ec - W) // SSD_HEAD_DIM + HG)).astype(BF16)
    e2 = jnp.concatenate([e, e], axis=0)

    a_row = prow_ref[1:2, :]
    bias_col = pcol_ref[0, :, 0:1]
    a_col = pcol_ref[0, :, 1:2]

    lane_fwd = _iota2((L, LANE), 1) < HG
    row_fwd = _iota2((R, L), 0) < HG
    lane_half = _iota2((L, LANE), 1) < SSD_HEAD_DIM

    def decay_sums(c):
        rs = pl.ds(pl.multiple_of(c * L, L), L)
        dt_r = _softplus(ust_ref[:, rs] + bias_col)
        a_r = dt_r * a_col
        dt_c = jnp.concatenate([dt_r, jnp.zeros((L - R, L), F32)], axis=0).T
        a_c = dt_c * a_row
        return dt_c, a_c, _dot(tril_b, _split3(a_c, 1)), dt_r, a_r, _dot(_split3(a_r, 0), triu_b)

    def decay_store(c, sums):
        rs = pl.ds(pl.multiple_of(c * L, L), L)
        dt_c, a_c, pp, dt_r, a_r, pr = sums
        p_c = pp[:, 0:LANE] + pp[:, LANE:2 * LANE] + pp[:, 2 * LANE:3 * LANE]
        tot_c = p_c[L - 1:L, :]
        cum_c = jnp.where(lane_fwd, p_c, tot_c - p_c + a_c)
        cumc_ref[rs, :] = cum_c * LOG2E
        wdt_ref[rs, :] = jnp.exp(tot_c - cum_c) * dt_c
        dec_ref[c] = jnp.broadcast_to(jnp.exp(tot_c), (16, LANE))
        p_r = pr[0:R] + pr[R:2 * R] + pr[2 * R:3 * R]
        tot_r = p_r[:, L - 1:L]
        cum_r = jnp.where(row_fwd, p_r, tot_r - p_r + a_r)
        cumr_ref[:, rs] = (cum_r - jnp.log(dt_r)) * LOG2E

    def intra_chunk(c):
        rs = pl.ds(pl.multiple_of(c * L, L), L)
        x_c = x_ref[0, rs, :]
        b_c = b_ref[0, rs, :]
        cb = _nt(c_ref[0, rs, :], b_c)
        ex = _dot(_split_hi_lo(jnp.concatenate([wdt_ref[rs, :], dec_ref[c]], axis=0)), e2)
        decx_ref[c] = ex[L:L + 8]
        xf = x_c.astype(F32)
        xw = jnp.concatenate([xf * ex[0:L, 0:W], xf * ex[0:L, W:2 * W]], axis=1).astype(BF16)
        xs_ref[c] = _tn(b_c, xw)
        cum_c = cumc_ref[rs, :]
        cum_r = cumr_ref[:, rs]
        for hp in range(HG // 2):
            ms = []
            for hh in range(2):
                hf = 2 * hp + hh
                hb = HG + 2 * hp + hh
                decf = jnp.exp2(jnp.where(tril, cum_c[:, hf:hf + 1] - cum_r[hf:hf + 1, :], -jnp.inf))
                decb = jnp.exp2(jnp.where(triu, cum_c[:, hb:hb + 1] - cum_r[hb:hb + 1, :], -jnp.inf))
                ms.append((cb * (decf + decb)).astype(BF16))
            m2 = jnp.concatenate(ms, axis=1)
            xp = x_c[:, hp * LANE:(hp + 1) * LANE]
            xz = jnp.zeros_like(xp)
            x2 = jnp.concatenate([jnp.where(lane_half, xp, xz), jnp.where(lane_half, xz, xp)], axis=0)
            acc_ref[rs, hp * LANE:(hp + 1) * LANE] = _dot(m2, x2)

    G = SSD_CHUNKS_PER_STEP
    n_steps = nc // G

    def fused_body(i, carry):
        sums = [decay_sums(G * (i + 1) + j) for j in range(G)]
        for j in range(G):
            intra_chunk(G * i + j)
        for j in range(G):
            decay_store(G * (i + 1) + j, sums[j])
        return carry

    for j in range(G):
        decay_store(j, decay_sums(j))
    lax.fori_loop(0, n_steps - 1, fused_body, 0, unroll=True)
    for j in range(G):
        intra_chunk(G * (n_steps - 1) + j)

    st_ref[...] = jnp.zeros_like(st_ref)

    def state_body(i, carry):
        for d, c in enumerate((i, nc - 1 - i)):
            ls = slice(d * W, (d + 1) * W)
            s_prev = st_ref[:, ls]
            sst_ref[c, :, ls] = s_prev.astype(BF16)
            st_ref[:, ls] = s_prev * decx_ref[c, 0:1, ls] + xs_ref[c, :, ls]
        return carry

    lax.fori_loop(0, nc, state_body, 0, unroll=True)

    def out_body(c, carry):
        rs = pl.ds(pl.multiple_of(c * L, L), L)
        c_c = c_ref[0, rs, :]
        ex = _split_hi_lo(jnp.exp2(cumc_ref[rs, :]))
        y = acc_ref[rs, :] + x_ref[0, rs, :].astype(F32) * drow_ref[...]
        for d in range(2):
            ls = slice(d * W, (d + 1) * W)
            y = y + _dot(c_c, sst_ref[c, :, ls]) * _dot(ex, e2[:, ls])
        y = y * _silu(z_ref[0, rs, :].astype(F32))
        ms = jnp.mean(y * y, axis=-1, keepdims=True)
        o_ref[0, rs, :] = (y * lax.rsqrt(ms + EPS) * nw_ref[...]).astype(BF16)
        return carry

    lax.fori_loop(0, nc, out_body, 0, unroll=SSD_OUT_UNROLL)


def _ssd(xbc, u3, ust, prow, pcol, drow, nw):
    B, S, _ = xbc.shape
    W = SSD_GROUP_W
    N = SSD_STATE
    nc = S // SSD_CHUNK
    return pl.pallas_call(
        _ssd_kernel,
        grid=(B, SSD_GROUPS),
        in_specs=[
            pl.BlockSpec((1, S, W), lambda b, g: (b, 0, g)),
            pl.BlockSpec((1, S, N), lambda b, g: (b, 0, SSD_D_INNER // N + g)),
            pl.BlockSpec((1, S, N), lambda b, g: (b, 0, SSD_D_INNER // N + SSD_GROUPS + g)),
            pl.BlockSpec((1, S, W), lambda b, g: (b, 0, U_Z // W + g)),
            pl.BlockSpec((SMALL_T_ROWS, S), lambda b, g: (g, b)),
            pl.BlockSpec((8, LANE), lambda b, g: (g, 0)),
            pl.BlockSpec((1, SMALL_T_ROWS, 8), lambda b, g: (g, 0, 0)),
            pl.BlockSpec((1, W), lambda b, g: (0, g)),
            pl.BlockSpec((1, W), lambda b, g: (0, g)),
        ],
        out_specs=pl.BlockSpec((1, S, W), lambda b, g: (b, 0, g)),
        out_shape=jax.ShapeDtypeStruct((B, S, SSD_D_INNER), BF16),
        scratch_shapes=[
            pltpu.VMEM((S, W), F32),
            pltpu.VMEM((S, LANE), F32),
            pltpu.VMEM((SMALL_T_ROWS, S), F32),
            pltpu.VMEM((S, LANE), F32),
            pltpu.VMEM((nc, 16, LANE), F32),
            pltpu.VMEM((nc, N, 2 * W), F32),
            pltpu.VMEM((nc, 8, 2 * W), F32),
            pltpu.VMEM((nc, N, 2 * W), BF16),
            pltpu.VMEM((N, 2 * W), F32),
        ],
        compiler_params=_cparams(("parallel", "parallel")),
        name="ssd_scan",
    )(xbc, xbc, xbc, u3, ust, prow, pcol, drow, nw)


def _gla_kernel(q_ref, k_ref, v_ref, gg_ref, us_ref, a2_ref, bias_ref, nw_ref,
                o_ref, acc_ref, g_ref, qd_ref, kd_ref, kdp_ref, qcat_ref, x_ref, dec_ref, sst_ref):
    L = GLA_CHUNK
    BL = 2 * L
    DK = GLA_DK
    S = q_ref.shape[1]
    nb = S // BL
    scale = DK ** -0.5

    ii = _iota2((BL, BL), 0)
    jj = _iota2((BL, BL), 1)
    same = (ii // L) == (jj // L)
    masks = (same & (ii >= jj), same & (jj >= ii))
    tri2 = masks[0].astype(BF16)
    par_row = _iota2((BL, DK), 0) // L

    ga = us_ref[0].astype(BF16)
    g_ref[...] = _log2_sigmoid(_dot(ga, a2_ref[...]) + bias_ref[...]) * (1.0 / GLA_GATE_NORM)

    def decay_sums(i):
        g = g_ref[pl.ds(pl.multiple_of(i * BL, BL), BL), :]
        hi = g.astype(BF16)
        r1 = g - hi.astype(F32)
        mid = r1.astype(BF16)
        lo = (r1 - mid.astype(F32)).astype(BF16)
        return g, _dot(tri2, jnp.concatenate([hi, mid, lo], axis=1))

    def decay_block(i, sums):
        rs = pl.ds(pl.multiple_of(i * BL, BL), BL)
        g, pp = sums
        p = pp[:, 0:2 * DK] + pp[:, 2 * DK:4 * DK] + pp[:, 4 * DK:6 * DK]
        q_c = q_ref[0, rs, :].astype(F32) * scale
        k_c = k_ref[0, rs, :].astype(F32)
        zero = jnp.zeros((BL, DK), BF16)
        for d in range(2):
            p_d = p[:, d * DK:(d + 1) * DK]
            tot = jnp.where(par_row == 0, p_d[L - 1:L, :], p_d[BL - 1:BL, :])
            b = p_d if d == 0 else tot - p_d + g[:, DK:]
            qd = (q_c * jnp.exp2(b)).astype(BF16)
            kdec = (k_c * jnp.exp2(tot - b)).astype(BF16)
            qd_ref[d, rs, :] = qd
            kd_ref[d, rs, :] = (k_c * jnp.exp2(-b)).astype(BF16)
            for par in range(2):
                sel = par_row == par
                kdp_ref[d, rs, par * DK:(par + 1) * DK] = jnp.where(sel, kdec, zero)
                qcat_ref[rs, (2 * d + par) * DK:(2 * d + par + 1) * DK] = jnp.where(sel, qd, zero)
                last = (par + 1) * L - 1
                dec_ref[d, 2 * i + par] = jnp.broadcast_to(jnp.exp2(p_d[last:last + 1, :]), (8, DK))

    G = GLA_BLOCKS_PER_STEP
    n_groups = nb // G

    def intra_group(i):
        blks = [G * i + j for j in range(G)]
        rss = [pl.ds(pl.multiple_of(b * BL, BL), BL) for b in blks]
        vs = [v_ref[0, rs, :] for rs in rss]
        atts = [[_nt(qd_ref[d, rs, :], kd_ref[d, rs, :]) for d in range(2)] for rs in rss]
        for j, b in enumerate(blks):
            for d in range(2):
                x_ref[d, b] = _tn(vs[j], kdp_ref[d, rss[j], :])
        for j in range(G):
            att = jnp.where(masks[0], atts[j][0], 0.0) + jnp.where(masks[1], atts[j][1], 0.0)
            acc_ref[rss[j], :] = _dot(att.astype(BF16), vs[j])

    def fused_body(i, carry):
        sums = [decay_sums(G * (i + 1) + j) for j in range(G)]
        intra_group(i)
        for j in range(G):
            decay_block(G * (i + 1) + j, sums[j])
        return carry

    for j in range(G):
        decay_block(j, decay_sums(j))
    lax.fori_loop(0, n_groups - 1, fused_body, 0, unroll=True)
    intra_group(n_groups - 1)

    def state_body(d):
        def body(i, s):
            b = i if d == 0 else nb - 1 - i
            for par in ((0, 1) if d == 0 else (1, 0)):
                lane0 = (2 * d + par) * DK
                sst_ref[b, :, lane0:lane0 + DK] = s.astype(BF16)
                s = s * dec_ref[d, 2 * b + par, 0:1, :] + x_ref[d, b, :, par * DK:(par + 1) * DK]
            return s
        return body

    for d in range(2):
        lax.fori_loop(0, nb, state_body(d), jnp.zeros((GLA_DV, DK), F32), unroll=True)

    def out_body(b, carry):
        rs = pl.ds(pl.multiple_of(b * BL, BL), BL)
        o = acc_ref[rs, :] + _nt(qcat_ref[rs, :], sst_ref[b])
        ms = jnp.mean(o * o, axis=-1, keepdims=True)
        o = o * lax.rsqrt(ms + EPS) * nw_ref[...]
        o_ref[0, rs, :] = (o * _silu(gg_ref[0, rs, :].astype(F32))).astype(BF16)
        return carry

    lax.fori_loop(0, nb, out_body, 0, unroll=GLA_OUT_UNROLL)


def _gla(u3, us3, a2, bias, nw):
    B, S, _ = u3.shape
    DK, DV = GLA_DK, GLA_DV
    nc = S // GLA_CHUNK
    nb = nc // 2
    return pl.pallas_call(
        _gla_kernel,
        grid=(B, GLA_HEADS),
        in_specs=[
            pl.BlockSpec((1, S, DK), lambda b, h: (b, 0, U_GQ // DK + h)),
            pl.BlockSpec((1, S, DK), lambda b, h: (b, 0, U_GK // DK + h)),
            pl.BlockSpec((1, S, DV), lambda b, h: (b, 0, U_GV // DV + h)),
            pl.BlockSpec((1, S, DV), lambda b, h: (b, 0, U_GG // DV + h)),
            pl.BlockSpec((1, S, LANE), lambda b, h: (b, 0, 0)),
            pl.BlockSpec((LANE, 2 * DK), lambda b, h: (0, h)),
            pl.BlockSpec((1, 2 * DK), lambda b, h: (0, h)),
            pl.BlockSpec((1, DV), lambda b, h: (0, 0)),
        ],
        out_specs=pl.BlockSpec((1, S, DV), lambda b, h: (b, 0, h)),
        out_shape=jax.ShapeDtypeStruct((B, S, GLA_VAL_W), BF16),
        scratch_shapes=[
            pltpu.VMEM((S, DV), F32),
            pltpu.VMEM((S, 2 * DK), F32),
            pltpu.VMEM((2, S, DK), BF16),
            pltpu.VMEM((2, S, DK), BF16),
            pltpu.VMEM((2, S, 2 * DK), BF16),
            pltpu.VMEM((S, 4 * DK), BF16),
            pltpu.VMEM((2, nb, DV, 2 * DK), F32),
            pltpu.VMEM((2, nc, 8, DK), F32),
            pltpu.VMEM((nb, DV, 4 * DK), BF16),
        ],
        compiler_params=_cparams(("parallel", "parallel")),
        name="gla_scan",
    )(u3, u3, u3, u3, us3, a2, bias, nw)


def _na_bias_rows(rpb):
    H, R, C = rpb.shape
    n_pos = GRID_W - NA_WIN_W
    n_neg = GRID_W - NA_WIN_W + 2
    ext = jnp.concatenate([rpb[:, :, NA_WIN_W - 1:], jnp.repeat(rpb[:, :, C - 1:], n_pos, axis=2),
                           jnp.repeat(rpb[:, :, 0:1], n_neg, axis=2), rpb[:, :, 1:NA_WIN_W - 1]], axis=2)
    return jnp.pad(ext, ((0, 0), (0, 1), (0, 0))).reshape(H // 2, 2, R + 1, 2 * GRID_W)


def _na_kernel(flag_ref, q_ref, k_ref, v_ref, ext_ref, o_ref, vx_ref, tab_ref):
    NB, S, _ = q_ref.shape
    rows = S // GRID_W
    total = NB * rows
    win_h = NA_WIN_H
    nk = win_h * GRID_W

    vx_ref[:, :, 0:LANE] = v_ref[...]
    vx_ref[:, :, LANE:2 * LANE] = jnp.ones((NB, S, LANE), BF16)
    bound_ok = flag_ref[0] != 0

    first_q = _iota2((GRID_W, LANE), 1) < NA_HEAD_DIM

    @pl.when(pl.program_id(1) == 0)
    def _():
        q_col = _iota2((GRID_W, 2 * GRID_W), 0)
        k_col = _iota2((GRID_W, 2 * GRID_W), 1) % GRID_W
        w_start = jnp.clip(q_col - NA_WIN_W // 2, 0, GRID_W - NA_WIN_W)
        in_window = (k_col >= w_start) & (k_col < w_start + NA_WIN_W)
        low = _iota2((GRID_W, 2 * GRID_W), 1) < GRID_W
        for hd in range(2):
            def skewed(rr, shift):
                row = jnp.broadcast_to(ext_ref[0, hd, rr:rr + 1, :], (GRID_W, 2 * GRID_W))
                return pltpu.roll(row, shift, axis=1, stride=1, stride_axis=0)

            for e in range(2 * NA_WIN_H - 2):
                t = jnp.where(low, skewed(e, 0), skewed(e + 1, GRID_W))
                tab_ref[e, hd * GRID_W:(hd + 1) * GRID_W, :] = jnp.where(in_window, t, -jnp.inf)

    def locate(r):
        bi = r // rows
        rl = r - bi * rows
        r0 = jnp.clip(rl - win_h // 2, 0, rows - win_h)
        return bi, pl.multiple_of(rl * GRID_W, GRID_W), rl - r0, pl.multiple_of(r0 * GRID_W, GRID_W)

    def scores(r):
        bi, q0, delta, k0 = locate(r)
        q = q_ref[bi, pl.ds(q0, GRID_W), :]
        zero = jnp.zeros_like(q)
        qs = jnp.concatenate([jnp.where(first_q, q, zero), jnp.where(first_q, zero, q)], axis=0)
        rr0 = (win_h - 1) - delta
        bias = jnp.concatenate([tab_ref[rr0 + w] for w in range(0, win_h, 2)], axis=1)
        return _nt(qs, k_ref[bi, pl.ds(k0, nk), :]) + bias

    def probs_exact(s):
        return jnp.exp2(s - jnp.max(s, axis=-1, keepdims=True)).astype(BF16)

    def probs_bounded(r):
        return jnp.exp2(scores(r)).astype(BF16)

    def attend(r, p):
        bi, q0, _, k0 = locate(r)
        ox = _dot(p, vx_ref[bi, pl.ds(k0, nk), :])
        o = ox[:, 0:LANE] / ox[:, LANE:2 * LANE]
        o_ref[bi, pl.ds(q0, GRID_W), :] = jnp.where(first_q, o[:GRID_W], o[GRID_W:]).astype(BF16)

    @pl.when(bound_ok)
    def _():
        U = NA_ROWS_PER_STEP

        def row_body(i, p_prev):
            r = i * U
            for j in range(U):
                attend(r - U + j, p_prev[j])
            return tuple(probs_bounded(r + j) for j in range(U))

        p_last = lax.fori_loop(1, total // U, row_body, tuple(probs_bounded(j) for j in range(U)), unroll=True)
        for j in range(U):
            attend(total - U + j, p_last[j])

    @pl.when(jnp.logical_not(bound_ok))
    def _():
        U = NA_ROWS_PER_STEP_EXACT

        def row_body(i, carry):
            s_cur, p_prev = carry
            r = i * U
            for j in range(U):
                attend(r - U + j, p_prev[j])
            p = tuple(probs_exact(s) for s in s_cur)
            s_next = tuple(scores(jnp.minimum(r + U + j, total - 1)) for j in range(U))
            return s_next, p

        p0 = tuple(probs_exact(scores(j)) for j in range(U))
        s1 = tuple(scores(U + j) for j in range(U))
        _, p_last = lax.fori_loop(1, total // U, row_body, (s1, p0))
        for j in range(U):
            attend(total - U + j, p_last[j])


def _na_score_bound(rpb, q_norm_w, k_norm_w):
    H = rpb.shape[0]
    qk = (NA_HEAD_DIM ** 0.5 * LOG2E * (1.0 + NA_BOUND_SLACK)) * jnp.max(jnp.abs(q_norm_w)) * jnp.max(jnp.abs(k_norm_w))
    b_max = jnp.max(rpb.reshape(H, -1), axis=1) * LOG2E
    b_self = rpb[:, NA_WIN_H - 1, NA_WIN_W - 1] * LOG2E
    bound = qk + b_max
    flag = jnp.all(bound - (b_self - qk) <= NA_MAX_BOUND_GAP)
    return bound, flag.astype(jnp.int32).reshape(1)


def _na(u3, bias_rows, flag):
    B, S, _ = u3.shape
    assert S // GRID_W >= NA_WIN_H and NA_WIN_H % 2 == 0 and 2 * GRID_W == LANE
    nb = math.gcd(B, NA_BATCH_PER_STEP)
    assert (nb * (S // GRID_W)) % NA_ROWS_PER_STEP == 0
    return pl.pallas_call(
        _na_kernel,
        grid=(NA_HEADS // 2, B // nb),
        in_specs=[
            pl.BlockSpec(memory_space=pltpu.SMEM),
            pl.BlockSpec((nb, S, LANE), lambda h, b: (b, 0, U_NQ // LANE + h)),
            pl.BlockSpec((nb, S, LANE), lambda h, b: (b, 0, U_NK // LANE + h)),
            pl.BlockSpec((nb, S, LANE), lambda h, b: (b, 0, U_NV // LANE + h)),
            pl.BlockSpec((1, 2, 2 * NA_WIN_H, 2 * GRID_W), lambda h, b: (h, 0, 0, 0)),
        ],
        out_specs=pl.BlockSpec((nb, S, LANE), lambda h, b: (b, 0, h)),
        out_shape=jax.ShapeDtypeStruct((B, S, NA_W), BF16),
        scratch_shapes=[
            pltpu.VMEM((nb, S, 2 * LANE), BF16),
            pltpu.VMEM((2 * NA_WIN_H - 2, 2 * GRID_W, 2 * GRID_W), F32),
        ],
        compiler_params=_cparams(("parallel", "arbitrary")),
        name="na_attn",
    )(flag, u3, u3, u3, bias_rows)


def _merge_kernel(x_ref, ys_ref, yg_ref, yn_ref, gate_ref, ws_ref, wg_ref, wn_ref, wo_ref, o_ref):
    D = D_MODEL
    mixed = _sigmoid(gate_ref[:, 0:D].astype(F32)) * _dot(ys_ref[...], ws_ref[...])
    mixed += _sigmoid(gate_ref[:, D:2 * D].astype(F32)) * _dot(yg_ref[...], wg_ref[...])
    mixed += _sigmoid(gate_ref[:, 2 * D:3 * D].astype(F32)) * _dot(yn_ref[...], wn_ref[...])
    o_ref[...] = x_ref[...] + _dot(mixed.astype(BF16), wo_ref[...])


def _merge(x2, ys, yg, yn, u2, ws, wg, wn, wo, tm=512):
    T = x2.shape[0]
    D = D_MODEL
    row = lambda i: (i, 0)
    fixed = lambda i: (0, 0)
    return pl.pallas_call(
        _merge_kernel,
        grid=(T // tm,),
        in_specs=[
            pl.BlockSpec((tm, D), row),
            pl.BlockSpec((tm, D), row),
            pl.BlockSpec((tm, D), row),
            pl.BlockSpec((tm, D), row),
            pl.BlockSpec((tm, N_BRANCH * D), lambda i: (i, U_GATE // (N_BRANCH * D))),
            pl.BlockSpec((D, D), fixed),
            pl.BlockSpec((D, D), fixed),
            pl.BlockSpec((D, D), fixed),
            pl.BlockSpec((D, D), fixed),
        ],
        out_specs=pl.BlockSpec((tm, D), row),
        out_shape=jax.ShapeDtypeStruct((T, D), F32),
        compiler_params=_cparams(("parallel",)),
        name="merge",
    )(x2, ys, yg, yn, u2, ws, wg, wn, wo)


def _mlp_kernel(x_ref, nw_ref, w1_ref, w2_ref, o_ref, *, tf):
    x = x_ref[...]
    ms = jnp.mean(x * x, axis=-1, keepdims=True)
    h = (x * lax.rsqrt(ms + EPS) * nw_ref[...]).astype(BF16)
    acc = x
    for f in range(D_FF // tf):
        a = jnp.maximum(_dot(h, w1_ref[:, f * tf:(f + 1) * tf]), 0.0)
        acc = acc + _dot((a * a).astype(BF16), w2_ref[f * tf:(f + 1) * tf, :])
    o_ref[...] = acc


def _mlp(x2, nw, w1, w2, tm=1024, tf=1024):
    T = x2.shape[0]
    D = D_MODEL
    resident = pl.Buffered(1)
    return pl.pallas_call(
        functools.partial(_mlp_kernel, tf=tf),
        grid=(T // tm,),
        in_specs=[
            pl.BlockSpec((tm, D), lambda i: (i, 0)),
            pl.BlockSpec((1, D), lambda i: (0, 0)),
            pl.BlockSpec((D, D_FF), lambda i: (0, 0), pipeline_mode=resident),
            pl.BlockSpec((D_FF, D), lambda i: (0, 0), pipeline_mode=resident),
        ],
        out_specs=pl.BlockSpec((tm, D), lambda i: (i, 0)),
        out_shape=jax.ShapeDtypeStruct((T, D), F32),
        compiler_params=_cparams(("parallel",)),
        name="mlp",
    )(x2, nw, w1, w2)


def _pad_rows(w, start, total):
    return jnp.zeros((total, w.shape[1]), w.dtype).at[start:start + w.shape[0]].set(w)


def kernel(x, norm_mix_w, w_in, ssd_conv_w, ssd_conv_b, ssd_dt_bias_f, ssd_dt_bias_b, ssd_a_log_f,
           ssd_a_log_b, ssd_d, ssd_norm_w, gla_a2_f, gla_a2_bias_f, gla_a2_b, gla_a2_bias_b,
           gla_norm_w, na_q_norm_w, na_k_norm_w, na_rpb, w_branch_ssd, w_branch_gla, w_branch_na,
           w_out, norm_mlp_w, w_ff1, w_ff2):
    B, S, D = x.shape
    T = B * S
    depth = w_in.shape[0]
    x2 = x.reshape(T, D)
    w_in_t = jnp.swapaxes(w_in, 1, 2)
    for l in range(depth):
        w_big, w_small = _permute_weight(w_in_t, l)
        hg = (SSD_GROUPS, SSD_HG)
        zeros_r = jnp.zeros((SSD_GROUPS, LANE - 2 * SSD_HG), F32)
        dt_bias = jnp.concatenate([ssd_dt_bias_f[l].reshape(hg), ssd_dt_bias_b[l].reshape(hg), zeros_r], axis=1)
        a_neg = jnp.concatenate([-jnp.exp(ssd_a_log_f[l]).reshape(hg), -jnp.exp(ssd_a_log_b[l]).reshape(hg),
                                 zeros_r], axis=1)
        prow3 = jnp.concatenate([dt_bias[:, None], a_neg[:, None], jnp.zeros((SSD_GROUPS, 6, LANE), F32)], axis=1)
        prow = prow3.reshape(SSD_GROUPS * 8, LANE)
        pcol = jnp.transpose(prow3[:, :, :SMALL_T_ROWS], (0, 2, 1))
        drow = jnp.repeat(ssd_d[l], SSD_HEAD_DIM)[None, :]
        hk = (GLA_HEADS, 1, GLA_DK)
        a2 = jnp.concatenate([_pad_rows(gla_a2_f[l], SM_GAF, LANE).reshape((LANE,) + hk),
                              _pad_rows(gla_a2_b[l], SM_GAB, LANE).reshape((LANE,) + hk)],
                             axis=2).reshape(LANE, -1).astype(BF16)
        a2_bias = jnp.concatenate([gla_a2_bias_f[l].reshape(hk), gla_a2_bias_b[l].reshape(hk)],
                                  axis=1).reshape(1, -1)
        na_bound, na_flag = _na_score_bound(na_rpb[l], na_q_norm_w[l], na_k_norm_w[l])
        table = _na_bias_rows(na_rpb[l] * LOG2E - na_bound[:, None, None])
        q_row = jnp.tile(na_q_norm_w[l] * (NA_HEAD_DIM ** -0.5 * LOG2E), NA_HEADS)
        k_row = jnp.tile(na_k_norm_w[l], NA_HEADS)
        qkw = jnp.concatenate([q_row[None], k_row[None], jnp.zeros((6, NA_W), F32)], axis=0)

        u2, us2, ust = _inproj(x2, norm_mix_w[l][None, :], w_big, w_small, qkw)
        u3 = u2.reshape(B, S, U_WIDTH)
        us3 = us2.reshape(B, S, SMALL_W)
        xbc = _conv(u3, ssd_conv_w[l], ssd_conv_b[l][None, :])
        y_ssd = _ssd(xbc, u3, ust, prow, pcol, drow, ssd_norm_w[l][None, :])
        y_gla = _gla(u3, us3, a2, a2_bias, gla_norm_w[l][None, :])
        y_na = _na(u3, table, na_flag)
        x2 = _merge(x2, y_ssd.reshape(T, -1), y_gla.reshape(T, -1), y_na.reshape(T, -1), u2,
                    w_branch_ssd[l].astype(BF16), w_branch_gla[l].astype(BF16),
                    w_branch_na[l].astype(BF16), w_out[l].astype(BF16))
        x2 = _mlp(x2, norm_mlp_w[l][None, :], w_ff1[l].astype(BF16), w_ff2[l].astype(BF16))
    return x2.reshape(B, S, D)
```

```python
import functools
import math

import jax
import jax.numpy as jnp
import numpy as np
from jax import lax
from jax.experimental import pallas as pl
from jax.experimental.pallas import tpu as pltpu

F32 = jnp.float32
BF16 = jnp.bfloat16

EPS = 1e-6
D_MODEL = 1024
GRID_W = 64

SSD_HEADS = 16
SSD_HEAD_DIM = 64
SSD_D_INNER = 1024
SSD_GROUPS = 2
SSD_STATE = 128
SSD_CONV = 5
SSD_CONV_DIM = 1536
SSD_CHUNK = 128
CONV_IN_BLOCKS = 3
CONV_TILE = 256
CONV_ROWS = 128
SSD_CHUNKS_PER_STEP = 2
SSD_OUT_UNROLL = 16
SSD_GROUP_W = SSD_D_INNER // SSD_GROUPS
SSD_HG = SSD_HEADS // SSD_GROUPS

GLA_HEADS = 4
GLA_DK = 128
GLA_DV = 256
GLA_KEY_W = 512
GLA_VAL_W = 1024
GLA_GATE_RANK = 16
GLA_GATE_NORM = 16.0
GLA_CHUNK = 64
GLA_BLOCKS_PER_STEP = 2
GLA_OUT_UNROLL = 16

NA_HEADS = 16
NA_HEAD_DIM = 64
NA_W = 1024
NA_WIN_H = 8
NA_WIN_W = 16
NA_BATCH_PER_STEP = 4
NA_ROWS_PER_STEP = 32
NA_ROWS_PER_STEP_EXACT = 2
LOG2E = 1.4426950408889634
NA_BOUND_SLACK = 0.02
NA_MAX_BOUND_GAP = 90.0

N_BRANCH = 3
D_FF = 4096

IN_SIZES = (SSD_D_INNER, SSD_CONV_DIM, SSD_HEADS, SSD_HEADS,
            GLA_KEY_W, GLA_KEY_W, GLA_VAL_W, GLA_VAL_W, GLA_GATE_RANK, GLA_GATE_RANK,
            NA_W, NA_W, NA_W, N_BRANCH * D_MODEL)
_IN_OFF = np.concatenate([[0], np.cumsum(IN_SIZES)])
(_O_Z, _O_XBC, _O_DTF, _O_DTB, _O_GQ, _O_GK, _O_GV, _O_GG, _O_GAF, _O_GAB,
 _O_NQ, _O_NK, _O_NV, _O_GATE) = [int(v) for v in _IN_OFF[:-1]]

U_NQ = 0
U_NK = 1024
U_NV = 2048
U_GATE = 3072
U_Z = 6144
U_XBC = 7168
U_GQ = 8704
U_GK = 9216
U_GV = 9728
U_GG = 10752
U_WIDTH = 11776
INPROJ_COL_CHUNK = 1024
LANE = 128
VMEM_LIMIT = 56 * 1024 * 1024

SMALL_W = SSD_GROUPS * LANE
SM_DTF, SM_DTB, SM_GAF, SM_GAB = 0, 8, 16, 32
SMALL_T_ROWS = 2 * SSD_HG

_BIG_SEGS = ((_O_NQ, 6144), (_O_Z, 2560), (_O_GQ, 3072))


def _cparams(sem, vmem=VMEM_LIMIT):
    return pltpu.CompilerParams(dimension_semantics=sem, vmem_limit_bytes=vmem)


def _sigmoid(x):
    return 1.0 / (1.0 + jnp.exp2(x * (-LOG2E)))


def _silu(x):
    return x * _sigmoid(x)


def _softplus(x):
    return jnp.maximum(x, 0.0) + jnp.log1p(jnp.exp(-jnp.abs(x)))


def _neg_log2_sigmoid(x):
    t = x * (-LOG2E)
    return jnp.maximum(t, 0.0) + jnp.log2(1.0 + jnp.exp2(-jnp.abs(t)))


def _nt(a, b):
    return lax.dot_general(a, b, (((1,), (1,)), ((), ())), preferred_element_type=F32)


def _tn(a, b):
    return lax.dot_general(a, b, (((0,), (0,)), ((), ())), preferred_element_type=F32)


def _dot(a, b):
    return jnp.dot(a, b, preferred_element_type=F32)


def _iota2(shape, dim):
    return lax.broadcasted_iota(jnp.int32, shape, dim)


def _wperm_kernel(wt_ref, o_ref, ws_ref):
    off = 0
    for a, n in _BIG_SEGS:
        o_ref[off:off + n, :] = wt_ref[0, a:a + n, :].astype(BF16)
        off += n
    ws_ref[...] = jnp.zeros_like(ws_ref)
    for g in range(SSD_GROUPS):
        base, h0 = g * LANE, g * SSD_HG
        ws_ref[base + SM_DTF:base + SM_DTF + SSD_HG, :] = wt_ref[0, _O_DTF + h0:_O_DTF + h0 + SSD_HG, :]
        ws_ref[base + SM_DTB:base + SM_DTB + SSD_HG, :] = wt_ref[0, _O_DTB + h0:_O_DTB + h0 + SSD_HG, :]
    ws_ref[SM_GAF:SM_GAF + GLA_GATE_RANK, :] = wt_ref[0, _O_GAF:_O_GAF + GLA_GATE_RANK, :]
    ws_ref[SM_GAB:SM_GAB + GLA_GATE_RANK, :] = wt_ref[0, _O_GAB:_O_GAB + GLA_GATE_RANK, :]


def _permute_weight(w_t_all, layer, tc=256):
    _, n_in, d = w_t_all.shape
    return pl.pallas_call(
        _wperm_kernel,
        grid=(d // tc,),
        in_specs=[pl.BlockSpec((1, n_in, tc), lambda i: (layer, 0, i))],
        out_specs=[
            pl.BlockSpec((U_WIDTH, tc), lambda i: (0, i)),
            pl.BlockSpec((SMALL_W, tc), lambda i: (0, i)),
        ],
        out_shape=[
            jax.ShapeDtypeStruct((U_WIDTH, d), BF16),
            jax.ShapeDtypeStruct((SMALL_W, d), F32),
        ],
        compiler_params=_cparams(("parallel",)),
        name="w_permute",
    )(w_t_all)


def _head_rms(r, w_row):
    G = 2 * LANE
    er = _iota2((G, G), 0) // NA_HEAD_DIM
    ec = _iota2((G, G), 1) // NA_HEAD_DIM
    e_blk = jnp.where(er == ec, 1.0 / NA_HEAD_DIM, 0.0).astype(BF16)
    outs = []
    for a in range(0, r.shape[1], G):
        x = r[:, a:a + G]
        ms = _dot((x * x).astype(BF16), e_blk)
        outs.append(x * lax.rsqrt(ms + EPS))
    return jnp.concatenate(outs, axis=1) * w_row


def _inproj_kernel(x_ref, nw_ref, w_ref, ws_ref, qkw_ref, u_ref, us_ref, ust_ref, h_ref):
    tn = u_ref.shape[1]
    chunks = [(a, min(a + INPROJ_COL_CHUNK, tn)) for a in range(0, tn, INPROJ_COL_CHUNK)]

    @pl.when(pl.program_id(1) == 0)
    def _():
        x = x_ref[...]
        ms = jnp.mean(x * x, axis=-1, keepdims=True)
        h = (x * lax.rsqrt(ms + EPS) * nw_ref[...]).astype(BF16)
        h_ref[...] = h
        ws = ws_ref[...].astype(BF16)
        us_ref[...] = _nt(h, ws)
        dt_rows = jnp.concatenate([ws[g * LANE:g * LANE + SMALL_T_ROWS] for g in range(SSD_GROUPS)], axis=0)
        ust_ref[...] = _nt(dt_rows, h)
        for c, (a, b) in enumerate(chunks):
            r = _nt(h, w_ref[a:b, :])
            if c < 2:
                r = _head_rms(r, qkw_ref[c:c + 1, :])
            u_ref[:, a:b] = r.astype(BF16)

    @pl.when(pl.program_id(1) != 0)
    def _():
        for a, b in chunks:
            u_ref[:, a:b] = _nt(h_ref[...], w_ref[a:b, :]).astype(BF16)


def _inproj(x2, nw, w_big, w_small, qkw, tm=1024, tn=U_WIDTH // 4):
    assert (U_NQ, U_NK) == (0, INPROJ_COL_CHUNK) and NA_W == INPROJ_COL_CHUNK
    T = x2.shape[0]
    return pl.pallas_call(
        _inproj_kernel,
        grid=(T // tm, U_WIDTH // tn),
        in_specs=[
            pl.BlockSpec((tm, D_MODEL), lambda i, j: (i, 0)),
            pl.BlockSpec((1, D_MODEL), lambda i, j: (0, 0)),
            pl.BlockSpec((tn, D_MODEL), lambda i, j: (j, 0)),
            pl.BlockSpec((SMALL_W, D_MODEL), lambda i, j: (0, 0)),
            pl.BlockSpec((8, NA_W), lambda i, j: (0, 0)),
        ],
        out_specs=[
            pl.BlockSpec((tm, tn), lambda i, j: (i, j)),
            pl.BlockSpec((tm, SMALL_W), lambda i, j: (i, 0)),
            pl.BlockSpec((SSD_GROUPS * SMALL_T_ROWS, tm), lambda i, j: (0, i)),
        ],
        out_shape=[
            jax.ShapeDtypeStruct((T, U_WIDTH), BF16),
            jax.ShapeDtypeStruct((T, SMALL_W), F32),
            jax.ShapeDtypeStruct((SSD_GROUPS * SMALL_T_ROWS, T), F32),
        ],
        scratch_shapes=[pltpu.VMEM((tm, D_MODEL), BF16)],
        compiler_params=_cparams(("parallel", "arbitrary")),
        name="inproj",
    )(x2, nw, w_big, w_small, qkw)


def _conv_kernel(*refs):
    u_refs, (w_ref, b_ref, o_ref, xp_ref) = refs[:CONV_IN_BLOCKS], refs[CONV_IN_BLOCKS:]
    S, wb = u_refs[0].shape[1], u_refs[0].shape[2]
    C = o_ref.shape[2]
    R = CONV_ROWS
    tc = CONV_TILE
    pad = SSD_CONV // 2
    xp_ref[0:R, :] = jnp.zeros((R, C), BF16)
    xp_ref[R + S:R + S + R, :] = jnp.zeros((R, C), BF16)
    for i, u_ref in enumerate(u_refs):
        xp_ref[R:R + S, i * wb:(i + 1) * wb] = u_ref[0]
    t_i = _iota2((R, 2 * R), 0)
    j_i = _iota2((R, 2 * R), 1)
    side = [k for k in range(SSD_CONV) if k != pad]
    shifts = jnp.concatenate([(j_i == t_i + R // 2 + (k - pad)).astype(BF16) for k in side], axis=0)
    for blk in range(S // R):
        for c0 in range(0, C, tc):
            cs = slice(c0, c0 + tc)
            win = xp_ref[blk * R + R // 2:blk * R + R // 2 + 2 * R, cs]
            sh = _dot(shifts, win)
            acc = b_ref[:, cs] + w_ref[pad:pad + 1, cs] * xp_ref[R + blk * R:R + (blk + 1) * R, cs].astype(F32)
            for n, k in enumerate(side):
                acc = acc + w_ref[k:k + 1, cs] * sh[n * R:(n + 1) * R]
            o_ref[0, blk * R:(blk + 1) * R, cs] = _silu(acc).astype(BF16)


def _conv(u3, conv_w, conv_b):
    B, S, _ = u3.shape
    wb = SSD_CONV_DIM // CONV_IN_BLOCKS
    assert U_XBC % wb == 0 and wb % CONV_TILE == 0
    off = U_XBC // wb
    in_blocks = [pl.BlockSpec((1, S, wb), functools.partial(lambda b, i: (b, 0, off + i), i=i))
                 for i in range(CONV_IN_BLOCKS)]
    return pl.pallas_call(
        _conv_kernel,
        grid=(B,),
        in_specs=in_blocks + [
            pl.BlockSpec((SSD_CONV, SSD_CONV_DIM), lambda b: (0, 0)),
            pl.BlockSpec((1, SSD_CONV_DIM), lambda b: (0, 0)),
        ],
        out_specs=pl.BlockSpec((1, S, SSD_CONV_DIM), lambda b: (b, 0, 0)),
        out_shape=jax.ShapeDtypeStruct((B, S, SSD_CONV_DIM), BF16),
        scratch_shapes=[pltpu.VMEM((S + 2 * CONV_ROWS, SSD_CONV_DIM), BF16)],
        compiler_params=_cparams(("parallel",)),
        name="ssd_conv",
    )(*([u3] * CONV_IN_BLOCKS), conv_w, conv_b)


def _split_hi_lo(v):
    hi = v.astype(BF16)
    lo = (v - hi.astype(F32)).astype(BF16)
    return jnp.concatenate([hi, lo], axis=1)


def _split3(v, axis):
    hi = v.astype(BF16)
    r1 = v - hi.astype(F32)
    mid = r1.astype(BF16)
    lo = (r1 - mid.astype(F32)).astype(BF16)
    return jnp.concatenate([hi, mid, lo], axis=axis)


def _ssd_kernel(x_ref, b_ref, c_ref, z_ref, ust_ref, prow_ref, pcol_ref, drow_ref, nw_ref,
                o_ref, acc_ref, cumc_ref, cumr_ref, wdt_ref, dec_ref, xs_ref, decx_ref, sst_ref, st_ref):
    L = SSD_CHUNK
    S = x_ref.shape[1]
    nc = S // L
    W = SSD_GROUP_W
    HG = SSD_HG
    R = 2 * HG

    ii = _iota2((L, L), 0)
    jj = _iota2((L, L), 1)
    tril = ii >= jj
    triu = jj >= ii
    tril_b = tril.astype(BF16)
    triu_b = triu.astype(BF16)

    er = _iota2((LANE, 2 * W), 0)
    ec = _iota2((LANE, 2 * W), 1)
    e = (er == jnp.where(ec < W, ec // SSD_HEAD_DIM, (ec - W) // SSD_HEAD_DIM + HG)).astype(BF16)
    e2 = jnp.concatenate([e, e], axis=0)

    a_row = prow_ref[1:2, :]
    bias_col = pcol_ref[0, :, 0:1]
    a_col = pcol_ref[0, :, 1:2]

    lane_fwd = _iota2((L, LANE), 1) < HG
    row_fwd = _iota2((R, L), 0) < HG
    lane_half = _iota2((L, LANE), 1) < SSD_HEAD_DIM

    def decay_sums(c):
        rs = pl.ds(pl.multiple_of(c * L, L), L)
        dt_r = _softplus(ust_ref[:, rs] + bias_col)
        a_r = dt_r * a_col
        dt_c = jnp.concatenate([dt_r, jnp.zeros((L - R, L), F32)], axis=0).T
        a_c = dt_c * a_row
        return dt_c, a_c, _dot(tril_b, _split3(a_c, 1)), dt_r, a_r, _dot(_split3(a_r, 0), triu_b)

    def decay_store(c, sums):
        rs = pl.ds(pl.multiple_of(c * L, L), L)
        dt_c, a_c, pp, dt_r, a_r, pr = sums
        p_c = pp[:, 0:LANE] + pp[:, LANE:2 * LANE] + pp[:, 2 * LANE:3 * LANE]
        tot_c = p_c[L - 1:L, :]
        cum_c = jnp.where(lane_fwd, p_c, tot_c - p_c + a_c)
        cumc_ref[rs, :] = cum_c * LOG2E
        wdt_ref[rs, :] = jnp.exp(tot_c - cum_c) * dt_c
        dec_ref[c] = jnp.broadcast_to(jnp.exp(tot_c), (16, LANE))
        p_r = pr[0:R] + pr[R:2 * R] + pr[2 * R:3 * R]
        tot_r = p_r[:, L - 1:L]
        cum_r = jnp.where(row_fwd, p_r, tot_r - p_r + a_r)
        cumr_ref[:, rs] = (cum_r - jnp.log(dt_r)) * LOG2E

    def intra_chunk(c):
        rs = pl.ds(pl.multiple_of(c * L, L), L)
        x_c = x_ref[0, rs, :]
        b_c = b_ref[0, rs, :]
        cb = _nt(c_ref[0, rs, :], b_c)
        ex = _dot(_split_hi_lo(jnp.concatenate([wdt_ref[rs, :], dec_ref[c]], axis=0)), e2)
        decx_ref[c] = ex[L:L + 8]
        xf = x_c.astype(F32)
        xw = jnp.concatenate([xf * ex[0:L, 0:W], xf * ex[0:L, W:2 * W]], axis=1).astype(BF16)
        xs_ref[c] = _tn(b_c, xw)
        cum_c = cumc_ref[rs, :]
        cum_r = cumr_ref[:, rs]
        for hp in range(HG // 2):
            ms = []
            for hh in range(2):
                hf = 2 * hp + hh
                hb = HG + 2 * hp + hh
                decf = jnp.exp2(jnp.where(tril, cum_c[:, hf:hf + 1] - cum_r[hf:hf + 1, :], -jnp.inf))
                decb = jnp.exp2(jnp.where(triu, cum_c[:, hb:hb + 1] - cum_r[hb:hb + 1, :], -jnp.inf))
                ms.append((cb * (decf + decb)).astype(BF16))
            m2 = jnp.concatenate(ms, axis=1)
            xp = x_c[:, hp * LANE:(hp + 1) * LANE]
            xz = jnp.zeros_like(xp)
            x2 = jnp.concatenate([jnp.where(lane_half, xp, xz), jnp.where(lane_half, xz, xp)], axis=0)
            acc_ref[rs, hp * LANE:(hp + 1) * LANE] = _dot(m2, x2)

    G = SSD_CHUNKS_PER_STEP
    n_steps = nc // G

    def fused_body(i, carry):
        sums = [decay_sums(G * (i + 1) + j) for j in range(G)]
        for j in range(G):
            intra_chunk(G * i + j)
        for j in range(G):
            decay_store(G * (i + 1) + j, sums[j])
        return carry

    for j in range(G):
        decay_store(j, decay_sums(j))
    lax.fori_loop(0, n_steps - 1, fused_body, 0, unroll=True)
    for j in range(G):
        intra_chunk(G * (n_steps - 1) + j)

    st_ref[...] = jnp.zeros_like(st_ref)

    def state_body(i, carry):
        for d, c in enumerate((i, nc - 1 - i)):
            ls = slice(d * W, (d + 1) * W)
            s_prev = st_ref[:, ls]
            sst_ref[c, :, ls] = s_prev.astype(BF16)
            st_ref[:, ls] = s_prev * decx_ref[c, 0:1, ls] + xs_ref[c, :, ls]
        return carry

    lax.fori_loop(0, nc, state_body, 0, unroll=True)

    def out_body(c, carry):
        rs = pl.ds(pl.multiple_of(c * L, L), L)
        c_c = c_ref[0, rs, :]
        ex = _split_hi_lo(jnp.exp2(cumc_ref[rs, :]))
        y = acc_ref[rs, :] + x_ref[0, rs, :].astype(F32) * drow_ref[...]
        for d in range(2):
            ls = slice(d * W, (d + 1) * W)
            y = y + _dot(c_c, sst_ref[c, :, ls]) * _dot(ex, e2[:, ls])
        y = y * _silu(z_ref[0, rs, :].astype(F32))
        ms = jnp.mean(y * y, axis=-1, keepdims=True)
        o_ref[0, rs, :] = (y * lax.rsqrt(ms + EPS) * nw_ref[...]).astype(BF16)
        return carry

    lax.fori_loop(0, nc, out_body, 0, unroll=SSD_OUT_UNROLL)


def _ssd(xbc, u3, ust, prow, pcol, drow, nw):
    B, S, _ = xbc.shape
    W = SSD_GROUP_W
    N = SSD_STATE
    nc = S // SSD_CHUNK
    return pl.pallas_call(
        _ssd_kernel,
        grid=(B, SSD_GROUPS),
        in_specs=[
            pl.BlockSpec((1, S, W), lambda b, g: (b, 0, g)),
            pl.BlockSpec((1, S, N), lambda b, g: (b, 0, SSD_D_INNER // N + g)),
            pl.BlockSpec((1, S, N), lambda b, g: (b, 0, SSD_D_INNER // N + SSD_GROUPS + g)),
            pl.BlockSpec((1, S, W), lambda b, g: (b, 0, U_Z // W + g)),
            pl.BlockSpec((SMALL_T_ROWS, S), lambda b, g: (g, b)),
            pl.BlockSpec((8, LANE), lambda b, g: (g, 0)),
            pl.BlockSpec((1, SMALL_T_ROWS, 8), lambda b, g: (g, 0, 0)),
            pl.BlockSpec((1, W), lambda b, g: (0, g)),
            pl.BlockSpec((1, W), lambda b, g: (0, g)),
        ],
        out_specs=pl.BlockSpec((1, S, W), lambda b, g: (b, 0, g)),
        out_shape=jax.ShapeDtypeStruct((B, S, SSD_D_INNER), BF16),
        scratch_shapes=[
            pltpu.VMEM((S, W), F32),
            pltpu.VMEM((S, LANE), F32),
            pltpu.VMEM((SMALL_T_ROWS, S), F32),
            pltpu.VMEM((S, LANE), F32),
            pltpu.VMEM((nc, 16, LANE), F32),
            pltpu.VMEM((nc, N, 2 * W), F32),
            pltpu.VMEM((nc, 8, 2 * W), F32),
            pltpu.VMEM((nc, N, 2 * W), BF16),
            pltpu.VMEM((N, 2 * W), F32),
        ],
        compiler_params=_cparams(("parallel", "parallel")),
        name="ssd_scan",
    )(xbc, xbc, xbc, u3, ust, prow, pcol, drow, nw)


def _gla_kernel(q_ref, k_ref, v_ref, gg_ref, us_ref, a2_ref, bias_ref, nw_ref,
                o_ref, acc_ref, g_ref, qd_ref, kd_ref, kdp_ref, qcat_ref, x_ref, dec_ref, sst_ref):
    L = GLA_CHUNK
    BL = 2 * L
    DK = GLA_DK
    S = q_ref.shape[1]
    nb = S // BL
    scale = DK ** -0.5

    ii = _iota2((BL, BL), 0)
    jj = _iota2((BL, BL), 1)
    same = (ii // L) == (jj // L)
    masks = (same & (ii >= jj), same & (jj >= ii))
    tri2 = masks[0].astype(BF16)
    par_row = _iota2((BL, DK), 0) // L

    ga = us_ref[0].astype(BF16)
    g_ref[...] = _neg_log2_sigmoid(_dot(ga, a2_ref[...]) + bias_ref[...]) * (-1.0 / GLA_GATE_NORM)

    def decay_sums(i):
        g = g_ref[pl.ds(pl.multiple_of(i * BL, BL), BL), :]
        return g, _dot(tri2, _split_hi_lo(g))

    def decay_block(i, sums):
        rs = pl.ds(pl.multiple_of(i * BL, BL), BL)
        g, pp = sums
        p = pp[:, 0:2 * DK] + pp[:, 2 * DK:4 * DK]
        q_c = q_ref[0, rs, :].astype(F32) * scale
        k_c = k_ref[0, rs, :].astype(F32)
        zero = jnp.zeros((BL, DK), BF16)
        for d in range(2):
            p_d = p[:, d * DK:(d + 1) * DK]
            tot = jnp.where(par_row == 0, p_d[L - 1:L, :], p_d[BL - 1:BL, :])
            b = p_d if d == 0 else tot - p_d + g[:, DK:]
            qd = (q_c * jnp.exp2(b)).astype(BF16)
            kdec = (k_c * jnp.exp2(tot - b)).astype(BF16)
            qd_ref[d, rs, :] = qd
            kd_ref[d, rs, :] = (k_c * jnp.exp2(-b)).astype(BF16)
            for par in range(2):
                sel = par_row == par
                kdp_ref[d, rs, par * DK:(par + 1) * DK] = jnp.where(sel, kdec, zero)
                qcat_ref[rs, (2 * d + par) * DK:(2 * d + par + 1) * DK] = jnp.where(sel, qd, zero)
                last = (par + 1) * L - 1
                dec_ref[d, 2 * i + par] = jnp.broadcast_to(jnp.exp2(p_d[last:last + 1, :]), (8, DK))

    G = GLA_BLOCKS_PER_STEP
    n_groups = nb // G

    def intra_group(i):
        blks = [G * i + j for j in range(G)]
        rss = [pl.ds(pl.multiple_of(b * BL, BL), BL) for b in blks]
        vs = [v_ref[0, rs, :] for rs in rss]
        atts = [[_nt(qd_ref[d, rs, :], kd_ref[d, rs, :]) for d in range(2)] for rs in rss]
        for j, b in enumerate(blks):
            for d in range(2):
                x_ref[d, b] = _tn(vs[j], kdp_ref[d, rss[j], :])
        for j in range(G):
            att = jnp.where(masks[0], atts[j][0], 0.0) + jnp.where(masks[1], atts[j][1], 0.0)
            acc_ref[rss[j], :] = _dot(att.astype(BF16), vs[j])

    def fused_body(i, carry):
        sums = [decay_sums(G * (i + 1) + j) for j in range(G)]
        intra_group(i)
        for j in range(G):
            decay_block(G * (i + 1) + j, sums[j])
        return carry

    for j in range(G):
        decay_block(j, decay_sums(j))
    lax.fori_loop(0, n_groups - 1, fused_body, 0, unroll=True)
    intra_group(n_groups - 1)

    def state_body(d):
        def body(i, s):
            b = i if d == 0 else nb - 1 - i
            for par in ((0, 1) if d == 0 else (1, 0)):
                lane0 = (2 * d + par) * DK
                sst_ref[b, :, lane0:lane0 + DK] = s.astype(BF16)
                s = s * dec_ref[d, 2 * b + par, 0:1, :] + x_ref[d, b, :, par * DK:(par + 1) * DK]
            return s
        return body

    for d in range(2):
        lax.fori_loop(0, nb, state_body(d), jnp.zeros((GLA_DV, DK), F32), unroll=True)

    def out_body(b, carry):
        rs = pl.ds(pl.multiple_of(b * BL, BL), BL)
        o = acc_ref[rs, :] + _nt(qcat_ref[rs, :], sst_ref[b])
        ms = jnp.mean(o * o, axis=-1, keepdims=True)
        o = o * lax.rsqrt(ms + EPS) * nw_ref[...]
        o_ref[0, rs, :] = (o * _silu(gg_ref[0, rs, :].astype(F32))).astype(BF16)
        return carry

    lax.fori_loop(0, nb, out_body, 0, unroll=GLA_OUT_UNROLL)


def _gla(u3, us3, a2, bias, nw):
    B, S, _ = u3.shape
    DK, DV = GLA_DK, GLA_DV
    nc = S // GLA_CHUNK
    nb = nc // 2
    return pl.pallas_call(
        _gla_kernel,
        grid=(B, GLA_HEADS),
        in_specs=[
            pl.BlockSpec((1, S, DK), lambda b, h: (b, 0, U_GQ // DK + h)),
            pl.BlockSpec((1, S, DK), lambda b, h: (b, 0, U_GK // DK + h)),
            pl.BlockSpec((1, S, DV), lambda b, h: (b, 0, U_GV // DV + h)),
            pl.BlockSpec((1, S, DV), lambda b, h: (b, 0, U_GG // DV + h)),
            pl.BlockSpec((1, S, LANE), lambda b, h: (b, 0, 0)),
            pl.BlockSpec((LANE, 2 * DK), lambda b, h: (0, h)),
            pl.BlockSpec((1, 2 * DK), lambda b, h: (0, h)),
            pl.BlockSpec((1, DV), lambda b, h: (0, 0)),
        ],
        out_specs=pl.BlockSpec((1, S, DV), lambda b, h: (b, 0, h)),
        out_shape=jax.ShapeDtypeStruct((B, S, GLA_VAL_W), BF16),
        scratch_shapes=[
            pltpu.VMEM((S, DV), F32),
            pltpu.VMEM((S, 2 * DK), F32),
            pltpu.VMEM((2, S, DK), BF16),
            pltpu.VMEM((2, S, DK), BF16),
            pltpu.VMEM((2, S, 2 * DK), BF16),
            pltpu.VMEM((S, 4 * DK), BF16),
            pltpu.VMEM((2, nb, DV, 2 * DK), F32),
            pltpu.VMEM((2, nc, 8, DK), F32),
            pltpu.VMEM((nb, DV, 4 * DK), BF16),
        ],
        compiler_params=_cparams(("parallel", "parallel")),
        name="gla_scan",
    )(u3, u3, u3, u3, us3, a2, bias, nw)


def _na_bias_rows(rpb):
    H, R, C = rpb.shape
    n_pos = GRID_W - NA_WIN_W
    n_neg = GRID_W - NA_WIN_W + 2
    ext = jnp.concatenate([rpb[:, :, NA_WIN_W - 1:], jnp.repeat(rpb[:, :, C - 1:], n_pos, axis=2),
                           jnp.repeat(rpb[:, :, 0:1], n_neg, axis=2), rpb[:, :, 1:NA_WIN_W - 1]], axis=2)
    return jnp.pad(ext, ((0, 0), (0, 1), (0, 0))).reshape(H // 2, 2, R + 1, 2 * GRID_W)


def _na_kernel(flag_ref, q_ref, k_ref, v_ref, ext_ref, o_ref, vx_ref, tab_ref):
    NB, S, _ = q_ref.shape
    rows = S // GRID_W
    total = NB * rows
    win_h = NA_WIN_H
    nk = win_h * GRID_W

    vx_ref[:, :, 0:LANE] = v_ref[...]
    vx_ref[:, :, LANE:2 * LANE] = jnp.ones((NB, S, LANE), BF16)
    bound_ok = flag_ref[0] != 0

    first_q = _iota2((GRID_W, LANE), 1) < NA_HEAD_DIM

    @pl.when(pl.program_id(1) == 0)
    def _():
        q_col = _iota2((GRID_W, 2 * GRID_W), 0)
        k_col = _iota2((GRID_W, 2 * GRID_W), 1) % GRID_W
        w_start = jnp.clip(q_col - NA_WIN_W // 2, 0, GRID_W - NA_WIN_W)
        in_window = (k_col >= w_start) & (k_col < w_start + NA_WIN_W)
        low = _iota2((GRID_W, 2 * GRID_W), 1) < GRID_W
        for hd in range(2):
            def skewed(rr, shift):
                row = jnp.broadcast_to(ext_ref[0, hd, rr:rr + 1, :], (GRID_W, 2 * GRID_W))
                return pltpu.roll(row, shift, axis=1, stride=1, stride_axis=0)

            for e in range(2 * NA_WIN_H - 2):
                t = jnp.where(low, skewed(e, 0), skewed(e + 1, GRID_W))
                tab_ref[e, hd * GRID_W:(hd + 1) * GRID_W, :] = jnp.where(in_window, t, -jnp.inf)

    def locate(r):
        bi = r // rows
        rl = r - bi * rows
        r0 = jnp.clip(rl - win_h // 2, 0, rows - win_h)
        return bi, pl.multiple_of(rl * GRID_W, GRID_W), rl - r0, pl.multiple_of(r0 * GRID_W, GRID_W)

    def scores(r):
        bi, q0, delta, k0 = locate(r)
        q = q_ref[bi, pl.ds(q0, GRID_W), :]
        zero = jnp.zeros_like(q)
        qs = jnp.concatenate([jnp.where(first_q, q, zero), jnp.where(first_q, zero, q)], axis=0)
        rr0 = (win_h - 1) - delta
        bias = jnp.concatenate([tab_ref[rr0 + w] for w in range(0, win_h, 2)], axis=1)
        return _nt(qs, k_ref[bi, pl.ds(k0, nk), :]) + bias

    def probs_exact(s):
        return jnp.exp2(s - jnp.max(s, axis=-1, keepdims=True)).astype(BF16)

    def probs_bounded(r):
        return jnp.exp2(scores(r)).astype(BF16)

    def attend(r, p):
        bi, q0, _, k0 = locate(r)
        ox = _dot(p, vx_ref[bi, pl.ds(k0, nk), :])
        o = ox[:, 0:LANE] / ox[:, LANE:2 * LANE]
        o_ref[bi, pl.ds(q0, GRID_W), :] = jnp.where(first_q, o[:GRID_W], o[GRID_W:]).astype(BF16)

    @pl.when(bound_ok)
    def _():
        U = NA_ROWS_PER_STEP

        def row_body(i, p_prev):
            r = i * U
            for j in range(U):
                attend(r - U + j, p_prev[j])
            return tuple(probs_bounded(r + j) for j in range(U))

        p_last = lax.fori_loop(1, total // U, row_body, tuple(probs_bounded(j) for j in range(U)), unroll=True)
        for j in range(U):
            attend(total - U + j, p_last[j])

    @pl.when(jnp.logical_not(bound_ok))
    def _():
        U = NA_ROWS_PER_STEP_EXACT

        def row_body(i, carry):
            s_cur, p_prev = carry
            r = i * U
            for j in range(U):
                attend(r - U + j, p_prev[j])
            p = tuple(probs_exact(s) for s in s_cur)
            s_next = tuple(scores(jnp.minimum(r + U + j, total - 1)) for j in range(U))
            return s_next, p

        p0 = tuple(probs_exact(scores(j)) for j in range(U))
        s1 = tuple(scores(U + j) for j in range(U))
        _, p_last = lax.fori_loop(1, total // U, row_body, (s1, p0))
        for j in range(U):
            attend(total - U + j, p_last[j])


def _na_score_bound(rpb, q_norm_w, k_norm_w):
    H = rpb.shape[0]
    qk = (NA_HEAD_DIM ** 0.5 * LOG2E * (1.0 + NA_BOUND_SLACK)) * jnp.max(jnp.abs(q_norm_w)) * jnp.max(jnp.abs(k_norm_w))
    b_max = jnp.max(rpb.reshape(H, -1), axis=1) * LOG2E
    b_self = rpb[:, NA_WIN_H - 1, NA_WIN_W - 1] * LOG2E
    bound = qk + b_max
    flag = jnp.all(bound - (b_self - qk) <= NA_MAX_BOUND_GAP)
    return bound, flag.astype(jnp.int32).reshape(1)


def _na(u3, bias_rows, flag):
    B, S, _ = u3.shape
    assert S // GRID_W >= NA_WIN_H and NA_WIN_H % 2 == 0 and 2 * GRID_W == LANE
    nb = math.gcd(B, NA_BATCH_PER_STEP)
    assert (nb * (S // GRID_W)) % NA_ROWS_PER_STEP == 0
    return pl.pallas_call(
        _na_kernel,
        grid=(NA_HEADS // 2, B // nb),
        in_specs=[
            pl.BlockSpec(memory_space=pltpu.SMEM),
            pl.BlockSpec((nb, S, LANE), lambda h, b: (b, 0, U_NQ // LANE + h)),
            pl.BlockSpec((nb, S, LANE), lambda h, b: (b, 0, U_NK // LANE + h)),
            pl.BlockSpec((nb, S, LANE), lambda h, b: (b, 0, U_NV // LANE + h)),
            pl.BlockSpec((1, 2, 2 * NA_WIN_H, 2 * GRID_W), lambda h, b: (h, 0, 0, 0)),
        ],
        out_specs=pl.BlockSpec((nb, S, LANE), lambda h, b: (b, 0, h)),
        out_shape=jax.ShapeDtypeStruct((B, S, NA_W), BF16),
        scratch_shapes=[
            pltpu.VMEM((nb, S, 2 * LANE), BF16),
            pltpu.VMEM((2 * NA_WIN_H - 2, 2 * GRID_W, 2 * GRID_W), F32),
        ],
        compiler_params=_cparams(("parallel", "arbitrary")),
        name="na_attn",
    )(flag, u3, u3, u3, bias_rows)


def _merge_kernel(x_ref, ys_ref, yg_ref, yn_ref, gate_ref, ws_ref, wg_ref, wn_ref, wo_ref, o_ref):
    D = D_MODEL
    mixed = _sigmoid(gate_ref[:, 0:D].astype(F32)) * _dot(ys_ref[...], ws_ref[...])
    mixed += _sigmoid(gate_ref[:, D:2 * D].astype(F32)) * _dot(yg_ref[...], wg_ref[...])
    mixed += _sigmoid(gate_ref[:, 2 * D:3 * D].astype(F32)) * _dot(yn_ref[...], wn_ref[...])
    o_ref[...] = x_ref[...] + _dot(mixed.astype(BF16), wo_ref[...])


def _merge(x2, ys, yg, yn, u2, ws, wg, wn, wo, tm=512):
    T = x2.shape[0]
    D = D_MODEL
    row = lambda i: (i, 0)
    fixed = lambda i: (0, 0)
    return pl.pallas_call(
        _merge_kernel,
        grid=(T // tm,),
        in_specs=[
            pl.BlockSpec((tm, D), row),
            pl.BlockSpec((tm, D), row),
            pl.BlockSpec((tm, D), row),
            pl.BlockSpec((tm, D), row),
            pl.BlockSpec((tm, N_BRANCH * D), lambda i: (i, U_GATE // (N_BRANCH * D))),
            pl.BlockSpec((D, D), fixed),
            pl.BlockSpec((D, D), fixed),
            pl.BlockSpec((D, D), fixed),
            pl.BlockSpec((D, D), fixed),
        ],
        out_specs=pl.BlockSpec((tm, D), row),
        out_shape=jax.ShapeDtypeStruct((T, D), F32),
        compiler_params=_cparams(("parallel",)),
        name="merge",
    )(x2, ys, yg, yn, u2, ws, wg, wn, wo)


def _mlp_kernel(x_ref, nw_ref, w1_ref, w2_ref, o_ref, *, tf):
    x = x_ref[...]
    ms = jnp.mean(x * x, axis=-1, keepdims=True)
    h = (x * lax.rsqrt(ms + EPS) * nw_ref[...]).astype(BF16)
    acc = x
    for f in range(D_FF // tf):
        a = jnp.maximum(_dot(h, w1_ref[:, f * tf:(f + 1) * tf]), 0.0)
        acc = acc + _dot((a * a).astype(BF16), w2_ref[f * tf:(f + 1) * tf, :])
    o_ref[...] = acc


def _mlp(x2, nw, w1, w2, tm=1024, tf=1024):
    T = x2.shape[0]
    D = D_MODEL
    resident = pl.Buffered(1)
    return pl.pallas_call(
        functools.partial(_mlp_kernel, tf=tf),
        grid=(T // tm,),
        in_specs=[
            pl.BlockSpec((tm, D), lambda i: (i, 0)),
            pl.BlockSpec((1, D), lambda i: (0, 0)),
            pl.BlockSpec((D, D_FF), lambda i: (0, 0), pipeline_mode=resident),
            pl.BlockSpec((D_FF, D), lambda i: (0, 0), pipeline_mode=resident),
        ],
        out_specs=pl.BlockSpec((tm, D), lambda i: (i, 0)),
        out_shape=jax.ShapeDtypeStruct((T, D), F32),
        compiler_params=_cparams(("parallel",)),
        name="mlp",
    )(x2, nw, w1, w2)


def _pad_rows(w, start, total):
    return jnp.zeros((total, w.shape[1]), w.dtype).at[start:start + w.shape[0]].set(w)


def kernel(x, norm_mix_w, w_in, ssd_conv_w, ssd_conv_b, ssd_dt_bias_f, ssd_dt_bias_b, ssd_a_log_f,
           ssd_a_log_b, ssd_d, ssd_norm_w, gla_a2_f, gla_a2_bias_f, gla_a2_b, gla_a2_bias_b,
           gla_norm_w, na_q_norm_w, na_k_norm_w, na_rpb, w_branch_ssd, w_branch_gla, w_branch_na,
           w_out, norm_mlp_w, w_ff1, w_ff2):
    B, S, D = x.shape
    T = B * S
    depth = w_in.shape[0]
    x2 = x.reshape(T, D)
    w_in_t = jnp.swapaxes(w_in, 1, 2)
    for l in range(depth):
        w_big, w_small = _permute_weight(w_in_t, l)
        hg = (SSD_GROUPS, SSD_HG)
        zeros_r = jnp.zeros((SSD_GROUPS, LANE - 2 * SSD_HG), F32)
        dt_bias = jnp.concatenate([ssd_dt_bias_f[l].reshape(hg), ssd_dt_bias_b[l].reshape(hg), zeros_r], axis=1)
        a_neg = jnp.concatenate([-jnp.exp(ssd_a_log_f[l]).reshape(hg), -jnp.exp(ssd_a_log_b[l]).reshape(hg),
                                 zeros_r], axis=1)
        prow3 = jnp.concatenate([dt_bias[:, None], a_neg[:, None], jnp.zeros((SSD_GROUPS, 6, LANE), F32)], axis=1)
        prow = prow3.reshape(SSD_GROUPS * 8, LANE)
        pcol = jnp.transpose(prow3[:, :, :SMALL_T_ROWS], (0, 2, 1))
        drow = jnp.repeat(ssd_d[l], SSD_HEAD_DIM)[None, :]
        hk = (GLA_HEADS, 1, GLA_DK)
        a2 = jnp.concatenate([_pad_rows(gla_a2_f[l], SM_GAF, LANE).reshape((LANE,) + hk),
                              _pad_rows(gla_a2_b[l], SM_GAB, LANE).reshape((LANE,) + hk)],
                             axis=2).reshape(LANE, -1).astype(BF16)
        a2_bias = jnp.concatenate([gla_a2_bias_f[l].reshape(hk), gla_a2_bias_b[l].reshape(hk)],
                                  axis=1).reshape(1, -1)
        na_bound, na_flag = _na_score_bound(na_rpb[l], na_q_norm_w[l], na_k_norm_w[l])
        table = _na_bias_rows(na_rpb[l] * LOG2E - na_bound[:, None, None])
        q_row = jnp.tile(na_q_norm_w[l] * (NA_HEAD_DIM ** -0.5 * LOG2E), NA_HEADS)
        k_row = jnp.tile(na_k_norm_w[l], NA_HEADS)
        qkw = jnp.concatenate([q_row[None], k_row[None], jnp.zeros((6, NA_W), F32)], axis=0)

        u2, us2, ust = _inproj(x2, norm_mix_w[l][None, :], w_big, w_small, qkw)
        u3 = u2.reshape(B, S, U_WIDTH)
        us3 = us2.reshape(B, S, SMALL_W)
        xbc = _conv(u3, ssd_conv_w[l], ssd_conv_b[l][None, :])
        y_ssd = _ssd(xbc, u3, ust, prow, pcol, drow, ssd_norm_w[l][None, :])
        y_gla = _gla(u3, us3, a2, a2_bias, gla_norm_w[l][None, :])
        y_na = _na(u3, table, na_flag)
        x2 = _merge(x2, y_ssd.reshape(T, -1), y_gla.reshape(T, -1), y_na.reshape(T, -1), u2,
                    w_branch_ssd[l].astype(BF16), w_branch_gla[l].astype(BF16),
                    w_branch_na[l].astype(BF16), w_out[l].astype(BF16))
        x2 = _mlp(x2, norm_mlp_w[l][None, :], w_ff1[l].astype(BF16), w_ff2[l].astype(BF16))
    return x2.reshape(B, S, D)
```

```python
import functools
import math

import jax
import jax.numpy as jnp
import numpy as np
from jax import lax
from jax.experimental import pallas as pl
from jax.experimental.pallas import tpu as pltpu

F32 = jnp.float32
BF16 = jnp.bfloat16

EPS = 1e-6
D_MODEL = 1024
GRID_W = 64

SSD_HEADS = 16
SSD_HEAD_DIM = 64
SSD_D_INNER = 1024
SSD_GROUPS = 2
SSD_STATE = 128
SSD_CONV = 5
SSD_CONV_DIM = 1536
SSD_CHUNK = 128
CONV_IN_BLOCKS = 3
CONV_TILE = 256
CONV_ROWS = 128
SSD_CHUNKS_PER_STEP = 2
SSD_OUT_UNROLL = 16
SSD_GROUP_W = SSD_D_INNER // SSD_GROUPS
SSD_HG = SSD_HEADS // SSD_GROUPS

GLA_HEADS = 4
GLA_DK = 128
GLA_DV = 256
GLA_KEY_W = 512
GLA_VAL_W = 1024
GLA_GATE_RANK = 16
GLA_GATE_NORM = 16.0
GLA_CHUNK = 64
GLA_BLOCKS_PER_STEP = 2
GLA_OUT_UNROLL = 16

NA_HEADS = 16
NA_HEAD_DIM = 64
NA_W = 1024
NA_WIN_H = 8
NA_WIN_W = 16
NA_BATCH_PER_STEP = 4
NA_ROWS_PER_STEP = 32
NA_ROWS_PER_STEP_EXACT = 2
LOG2E = 1.4426950408889634
NA_BOUND_SLACK = 0.02
NA_MAX_BOUND_GAP = 90.0

N_BRANCH = 3
D_FF = 4096

IN_SIZES = (SSD_D_INNER, SSD_CONV_DIM, SSD_HEADS, SSD_HEADS,
            GLA_KEY_W, GLA_KEY_W, GLA_VAL_W, GLA_VAL_W, GLA_GATE_RANK, GLA_GATE_RANK,
            NA_W, NA_W, NA_W, N_BRANCH * D_MODEL)
_IN_OFF = np.concatenate([[0], np.cumsum(IN_SIZES)])
(_O_Z, _O_XBC, _O_DTF, _O_DTB, _O_GQ, _O_GK, _O_GV, _O_GG, _O_GAF, _O_GAB,
 _O_NQ, _O_NK, _O_NV, _O_GATE) = [int(v) for v in _IN_OFF[:-1]]

U_NQ = 0
U_NK = 1024
U_NV = 2048
U_GATE = 3072
U_Z = 6144
U_XBC = 7168
U_GQ = 8704
U_GK = 9216
U_GV = 9728
U_GG = 10752
U_WIDTH = 11776
INPROJ_COL_CHUNK = 1024
LANE = 128
VMEM_LIMIT = 56 * 1024 * 1024

SMALL_W = SSD_GROUPS * LANE
SM_DTF, SM_DTB, SM_GAF, SM_GAB = 0, 8, 16, 32
SMALL_T_ROWS = 2 * SSD_HG

_BIG_SEGS = ((_O_NQ, 6144), (_O_Z, 2560), (_O_GQ, 3072))


def _cparams(sem, vmem=VMEM_LIMIT):
    return pltpu.CompilerParams(dimension_semantics=sem, vmem_limit_bytes=vmem)


def _sigmoid(x):
    return 1.0 / (1.0 + jnp.exp2(x * (-LOG2E)))


def _silu(x):
    return x * _sigmoid(x)


def _softplus(x):
    return jnp.maximum(x, 0.0) + jnp.log1p(jnp.exp(-jnp.abs(x)))


def _neg_log2_sigmoid(x):
    t = x * (-LOG2E)
    return jnp.maximum(t, 0.0) + jnp.log2(1.0 + jnp.exp2(-jnp.abs(t)))


def _nt(a, b):
    return lax.dot_general(a, b, (((1,), (1,)), ((), ())), preferred_element_type=F32)


def _tn(a, b):
    return lax.dot_general(a, b, (((0,), (0,)), ((), ())), preferred_element_type=F32)


def _dot(a, b):
    return jnp.dot(a, b, preferred_element_type=F32)


def _iota2(shape, dim):
    return lax.broadcasted_iota(jnp.int32, shape, dim)


def _wperm_kernel(wt_ref, o_ref, ws_ref):
    off = 0
    for a, n in _BIG_SEGS:
        o_ref[off:off + n, :] = wt_ref[0, a:a + n, :].astype(BF16)
        off += n
    ws_ref[...] = jnp.zeros_like(ws_ref)
    for g in range(SSD_GROUPS):
        base, h0 = g * LANE, g * SSD_HG
        ws_ref[base + SM_DTF:base + SM_DTF + SSD_HG, :] = wt_ref[0, _O_DTF + h0:_O_DTF + h0 + SSD_HG, :]
        ws_ref[base + SM_DTB:base + SM_DTB + SSD_HG, :] = wt_ref[0, _O_DTB + h0:_O_DTB + h0 + SSD_HG, :]
    ws_ref[SM_GAF:SM_GAF + GLA_GATE_RANK, :] = wt_ref[0, _O_GAF:_O_GAF + GLA_GATE_RANK, :]
    ws_ref[SM_GAB:SM_GAB + GLA_GATE_RANK, :] = wt_ref[0, _O_GAB:_O_GAB + GLA_GATE_RANK, :]


def _permute_weight(w_t_all, layer, tc=256):
    _, n_in, d = w_t_all.shape
    return pl.pallas_call(
        _wperm_kernel,
        grid=(d // tc,),
        in_specs=[pl.BlockSpec((1, n_in, tc), lambda i: (layer, 0, i))],
        out_specs=[
            pl.BlockSpec((U_WIDTH, tc), lambda i: (0, i)),
            pl.BlockSpec((SMALL_W, tc), lambda i: (0, i)),
        ],
        out_shape=[
            jax.ShapeDtypeStruct((U_WIDTH, d), BF16),
            jax.ShapeDtypeStruct((SMALL_W, d), F32),
        ],
        compiler_params=_cparams(("parallel",)),
        name="w_permute",
    )(w_t_all)


def _head_rms(r, w_row):
    G = 2 * LANE
    er = _iota2((G, G), 0) // NA_HEAD_DIM
    ec = _iota2((G, G), 1) // NA_HEAD_DIM
    e_blk = jnp.where(er == ec, 1.0 / NA_HEAD_DIM, 0.0).astype(BF16)
    outs = []
    for a in range(0, r.shape[1], G):
        x = r[:, a:a + G]
        ms = _dot((x * x).astype(BF16), e_blk)
        outs.append(x * lax.rsqrt(ms + EPS))
    return jnp.concatenate(outs, axis=1) * w_row


def _inproj_kernel(x_ref, nw_ref, w_ref, ws_ref, qkw_ref, u_ref, us_ref, ust_ref, h_ref):
    tn = u_ref.shape[1]
    chunks = [(a, min(a + INPROJ_COL_CHUNK, tn)) for a in range(0, tn, INPROJ_COL_CHUNK)]

    @pl.when(pl.program_id(1) == 0)
    def _():
        x = x_ref[...]
        ms = jnp.mean(x * x, axis=-1, keepdims=True)
        h = (x * lax.rsqrt(ms + EPS) * nw_ref[...]).astype(BF16)
        h_ref[...] = h
        ws = ws_ref[...].astype(BF16)
        us_ref[...] = _nt(h, ws)
        dt_rows = jnp.concatenate([ws[g * LANE:g * LANE + SMALL_T_ROWS] for g in range(SSD_GROUPS)], axis=0)
        ust_ref[...] = _nt(dt_rows, h)
        for c, (a, b) in enumerate(chunks):
            r = _nt(h, w_ref[a:b, :])
            if c < 2:
                r = _head_rms(r, qkw_ref[c:c + 1, :])
            u_ref[:, a:b] = r.astype(BF16)

    @pl.when(pl.program_id(1) != 0)
    def _():
        for a, b in chunks:
            u_ref[:, a:b] = _nt(h_ref[...], w_ref[a:b, :]).astype(BF16)


def _inproj(x2, nw, w_big, w_small, qkw, tm=1024, tn=U_WIDTH // 4):
    assert (U_NQ, U_NK) == (0, INPROJ_COL_CHUNK) and NA_W == INPROJ_COL_CHUNK
    T = x2.shape[0]
    return pl.pallas_call(
        _inproj_kernel,
        grid=(T // tm, U_WIDTH // tn),
        in_specs=[
            pl.BlockSpec((tm, D_MODEL), lambda i, j: (i, 0)),
            pl.BlockSpec((1, D_MODEL), lambda i, j: (0, 0)),
            pl.BlockSpec((tn, D_MODEL), lambda i, j: (j, 0)),
            pl.BlockSpec((SMALL_W, D_MODEL), lambda i, j: (0, 0)),
            pl.BlockSpec((8, NA_W), lambda i, j: (0, 0)),
        ],
        out_specs=[
            pl.BlockSpec((tm, tn), lambda i, j: (i, j)),
            pl.BlockSpec((tm, SMALL_W), lambda i, j: (i, 0)),
            pl.BlockSpec((SSD_GROUPS * SMALL_T_ROWS, tm), lambda i, j: (0, i)),
        ],
        out_shape=[
            jax.ShapeDtypeStruct((T, U_WIDTH), BF16),
            jax.ShapeDtypeStruct((T, SMALL_W), F32),
            jax.ShapeDtypeStruct((SSD_GROUPS * SMALL_T_ROWS, T), F32),
        ],
        scratch_shapes=[pltpu.VMEM((tm, D_MODEL), BF16)],
        compiler_params=_cparams(("parallel", "arbitrary")),
        name="inproj",
    )(x2, nw, w_big, w_small, qkw)


def _conv_kernel(*refs):
    u_refs, (w_ref, b_ref, o_ref, xp_ref) = refs[:CONV_IN_BLOCKS], refs[CONV_IN_BLOCKS:]
    S, wb = u_refs[0].shape[1], u_refs[0].shape[2]
    C = o_ref.shape[2]
    R = CONV_ROWS
    tc = CONV_TILE
    pad = SSD_CONV // 2
    xp_ref[0:R, :] = jnp.zeros((R, C), BF16)
    xp_ref[R + S:R + S + R, :] = jnp.zeros((R, C), BF16)
    for i, u_ref in enumerate(u_refs):
        xp_ref[R:R + S, i * wb:(i + 1) * wb] = u_ref[0]
    t_i = _iota2((R, 2 * R), 0)
    j_i = _iota2((R, 2 * R), 1)
    side = [k for k in range(SSD_CONV) if k != pad]
    shifts = jnp.concatenate([(j_i == t_i + R // 2 + (k - pad)).astype(BF16) for k in side], axis=0)
    for blk in range(S // R):
        for c0 in range(0, C, tc):
            cs = slice(c0, c0 + tc)
            win = xp_ref[blk * R + R // 2:blk * R + R // 2 + 2 * R, cs]
            sh = _dot(shifts, win)
            acc = b_ref[:, cs] + w_ref[pad:pad + 1, cs] * xp_ref[R + blk * R:R + (blk + 1) * R, cs].astype(F32)
            for n, k in enumerate(side):
                acc = acc + w_ref[k:k + 1, cs] * sh[n * R:(n + 1) * R]
            o_ref[0, blk * R:(blk + 1) * R, cs] = _silu(acc).astype(BF16)


def _conv(u3, conv_w, conv_b):
    B, S, _ = u3.shape
    wb = SSD_CONV_DIM // CONV_IN_BLOCKS
    assert U_XBC % wb == 0 and wb % CONV_TILE == 0
    off = U_XBC // wb
    in_blocks = [pl.BlockSpec((1, S, wb), functools.partial(lambda b, i: (b, 0, off + i), i=i))
                 for i in range(CONV_IN_BLOCKS)]
    return pl.pallas_call(
        _conv_kernel,
        grid=(B,),
        in_specs=in_blocks + [
            pl.BlockSpec((SSD_CONV, SSD_CONV_DIM), lambda b: (0, 0)),
            pl.BlockSpec((1, SSD_CONV_DIM), lambda b: (0, 0)),
        ],
        out_specs=pl.BlockSpec((1, S, SSD_CONV_DIM), lambda b: (b, 0, 0)),
        out_shape=jax.ShapeDtypeStruct((B, S, SSD_CONV_DIM), BF16),
        scratch_shapes=[pltpu.VMEM((S + 2 * CONV_ROWS, SSD_CONV_DIM), BF16)],
        compiler_params=_cparams(("parallel",)),
        name="ssd_conv",
    )(*([u3] * CONV_IN_BLOCKS), conv_w, conv_b)


def _split_hi_lo(v):
    hi = v.astype(BF16)
    lo = (v - hi.astype(F32)).astype(BF16)
    return jnp.concatenate([hi, lo], axis=1)


def _split3(v, axis):
    hi = v.astype(BF16)
    r1 = v - hi.astype(F32)
    mid = r1.astype(BF16)
    lo = (r1 - mid.astype(F32)).astype(BF16)
    return jnp.concatenate([hi, mid, lo], axis=axis)


def _ssd_kernel(x_ref, b_ref, c_ref, z_ref, ust_ref, prow_ref, pcol_ref, drow_ref, nw_ref,
                o_ref, acc_ref, cumc_ref, cumr_ref, wdt_ref, dec_ref, xs_ref, decx_ref, sst_ref, st_ref):
    L = SSD_CHUNK
    S = x_ref.shape[1]
    nc = S // L
    W = SSD_GROUP_W
    HG = SSD_HG
    R = 2 * HG

    ii = _iota2((L, L), 0)
    jj = _iota2((L, L), 1)
    tril = ii >= jj
    triu = jj >= ii
    tril_b = tril.astype(BF16)
    triu_b = triu.astype(BF16)

    er = _iota2((LANE, 2 * W), 0)
    ec = _iota2((LANE, 2 * W), 1)
    e = (er == jnp.where(ec < W, ec // SSD_HEAD_DIM, (ec - W) // SSD_HEAD_DIM + HG)).astype(BF16)
    e2 = jnp.concatenate([e, e], axis=0)

    a_row = prow_ref[1:2, :]
    bias_col = pcol_ref[0, :, 0:1]
    a_col = pcol_ref[0, :, 1:2]

    lane_fwd = _iota2((L, LANE), 1) < HG
    row_fwd = _iota2((R, L), 0) < HG
    lane_half = _iota2((L, LANE), 1) < SSD_HEAD_DIM

    def decay_sums(c):
        rs = pl.ds(pl.multiple_of(c * L, L), L)
        dt_r = _softplus(ust_ref[:, rs] + bias_col)
        a_r = dt_r * a_col
        dt_c = jnp.concatenate([dt_r, jnp.zeros((L - R, L), F32)], axis=0).T
        a_c = dt_c * a_row
        return dt_c, a_c, _dot(tril_b, _split3(a_c, 1)), dt_r, a_r, _dot(_split3(a_r, 0), triu_b)

    def decay_store(c, sums):
        rs = pl.ds(pl.multiple_of(c * L, L), L)
        dt_c, a_c, pp, dt_r, a_r, pr = sums
        p_c = pp[:, 0:LANE] + pp[:, LANE:2 * LANE] + pp[:, 2 * LANE:3 * LANE]
        tot_c = p_c[L - 1:L, :]
        cum_c = jnp.where(lane_fwd, p_c, tot_c - p_c + a_c)
        cumc_ref[rs, :] = cum_c * LOG2E
        wdt_ref[rs, :] = jnp.exp(tot_c - cum_c) * dt_c
        dec_ref[c] = jnp.broadcast_to(jnp.exp(tot_c), (16, LANE))
        p_r = pr[0:R] + pr[R:2 * R] + pr[2 * R:3 * R]
        tot_r = p_r[:, L - 1:L]
        cum_r = jnp.where(row_fwd, p_r, tot_r - p_r + a_r)
        cumr_ref[:, rs] = (cum_r - jnp.log(dt_r)) * LOG2E

    def intra_chunk(c):
        rs = pl.ds(pl.multiple_of(c * L, L), L)
        x_c = x_ref[0, rs, :]
        b_c = b_ref[0, rs, :]
        cb = _nt(c_ref[0, rs, :], b_c)
        ex = _dot(_split_hi_lo(jnp.concatenate([wdt_ref[rs, :], dec_ref[c]], axis=0)), e2)
        decx_ref[c] = ex[L:L + 8]
        xf = x_c.astype(F32)
        xw = jnp.concatenate([xf * ex[0:L, 0:W], xf * ex[0:L, W:2 * W]], axis=1).astype(BF16)
        xs_ref[c] = _tn(b_c, xw)
        cum_c = cumc_ref[rs, :]
        cum_r = cumr_ref[:, rs]
        for hp in range(HG // 2):
            ms = []
            for hh in range(2):
                hf = 2 * hp + hh
                hb = HG + 2 * hp + hh
                decf = jnp.exp2(jnp.where(tril, cum_c[:, hf:hf + 1] - cum_r[hf:hf + 1, :], -jnp.inf))
                decb = jnp.exp2(jnp.where(triu, cum_c[:, hb:hb + 1] - cum_r[hb:hb + 1, :], -jnp.inf))
                ms.append((cb * (decf + decb)).astype(BF16))
            m2 = jnp.concatenate(ms, axis=1)
            xp = x_c[:, hp * LANE:(hp + 1) * LANE]
            xz = jnp.zeros_like(xp)
            x2 = jnp.concatenate([jnp.where(lane_half, xp, xz), jnp.where(lane_half, xz, xp)], axis=0)
            acc_ref[rs, hp * LANE:(hp + 1) * LANE] = _dot(m2, x2)

    G = SSD_CHUNKS_PER_STEP
    n_steps = nc // G

    def fused_body(i, carry):
        sums = [decay_sums(G * (i + 1) + j) for j in range(G)]
        for j in range(G):
            intra_chunk(G * i + j)
        for j in range(G):
            decay_store(G * (i + 1) + j, sums[j])
        return carry

    for j in range(G):
        decay_store(j, decay_sums(j))
    lax.fori_loop(0, n_steps - 1, fused_body, 0, unroll=True)
    for j in range(G):
        intra_chunk(G * (n_steps - 1) + j)

    st_ref[...] = jnp.zeros_like(st_ref)

    def state_body(i, carry):
        for d, c in enumerate((i, nc - 1 - i)):
            ls = slice(d * W, (d + 1) * W)
            s_prev = st_ref[:, ls]
            sst_ref[c, :, ls] = s_prev.astype(BF16)
            st_ref[:, ls] = s_prev * decx_ref[c, 0:1, ls] + xs_ref[c, :, ls]
        return carry

    lax.fori_loop(0, nc, state_body, 0, unroll=True)

    def out_body(c, carry):
        rs = pl.ds(pl.multiple_of(c * L, L), L)
        c_c = c_ref[0, rs, :]
        ex = _split_hi_lo(jnp.exp2(cumc_ref[rs, :]))
        y = acc_ref[rs, :] + x_ref[0, rs, :].astype(F32) * drow_ref[...]
        for d in range(2):
            ls = slice(d * W, (d + 1) * W)
            y = y + _dot(c_c, sst_ref[c, :, ls]) * _dot(ex, e2[:, ls])
        y = y * _silu(z_ref[0, rs, :].astype(F32))
        ms = jnp.mean(y * y, axis=-1, keepdims=True)
        o_ref[0, rs, :] = (y * lax.rsqrt(ms + EPS) * nw_ref[...]).astype(BF16)
        return carry

    lax.fori_loop(0, nc, out_body, 0, unroll=SSD_OUT_UNROLL)


def _ssd(xbc, u3, ust, prow, pcol, drow, nw):
    B, S, _ = xbc.shape
    W = SSD_GROUP_W
    N = SSD_STATE
    nc = S // SSD_CHUNK
    return pl.pallas_call(
        _ssd_kernel,
        grid=(B, SSD_GROUPS),
        in_specs=[
            pl.BlockSpec((1, S, W), lambda b, g: (b, 0, g)),
            pl.BlockSpec((1, S, N), lambda b, g: (b, 0, SSD_D_INNER // N + g)),
            pl.BlockSpec((1, S, N), lambda b, g: (b, 0, SSD_D_INNER // N + SSD_GROUPS + g)),
            pl.BlockSpec((1, S, W), lambda b, g: (b, 0, U_Z // W + g)),
            pl.BlockSpec((SMALL_T_ROWS, S), lambda b, g: (g, b)),
            pl.BlockSpec((8, LANE), lambda b, g: (g, 0)),
            pl.BlockSpec((1, SMALL_T_ROWS, 8), lambda b, g: (g, 0, 0)),
            pl.BlockSpec((1, W), lambda b, g: (0, g)),
            pl.BlockSpec((1, W), lambda b, g: (0, g)),
        ],
        out_specs=pl.BlockSpec((1, S, W), lambda b, g: (b, 0, g)),
        out_shape=jax.ShapeDtypeStruct((B, S, SSD_D_INNER), BF16),
        scratch_shapes=[
            pltpu.VMEM((S, W), F32),
            pltpu.VMEM((S, LANE), F32),
            pltpu.VMEM((SMALL_T_ROWS, S), F32),
            pltpu.VMEM((S, LANE), F32),
            pltpu.VMEM((nc, 16, LANE), F32),
            pltpu.VMEM((nc, N, 2 * W), F32),
            pltpu.VMEM((nc, 8, 2 * W), F32),
            pltpu.VMEM((nc, N, 2 * W), BF16),
            pltpu.VMEM((N, 2 * W), F32),
        ],
        compiler_params=_cparams(("parallel", "parallel")),
        name="ssd_scan",
    )(xbc, xbc, xbc, u3, ust, prow, pcol, drow, nw)


def _gla_kernel(q_ref, k_ref, v_ref, gg_ref, us_ref, a2_ref, bias_ref, nw_ref,
                o_ref, acc_ref, g_ref, qd_ref, kd_ref, kdp_ref, qcat_ref, x_ref, dec_ref, sst_ref):
    L = GLA_CHUNK
    BL = 2 * L
    DK = GLA_DK
    S = q_ref.shape[1]
    nb = S // BL
    scale = DK ** -0.5

    ii = _iota2((BL, BL), 0)
    jj = _iota2((BL, BL), 1)
    same = (ii // L) == (jj // L)
    masks = (same & (ii >= jj), same & (jj >= ii))
    tri2 = masks[0].astype(BF16)
    par_row = _iota2((BL, DK), 0) // L

    ga = us_ref[0].astype(BF16)
    g_ref[...] = _neg_log2_sigmoid(_dot(ga, a2_ref[...]) + bias_ref[...]) * (-1.0 / GLA_GATE_NORM)

    def decay_sums(i):
        g = g_ref[pl.ds(pl.multiple_of(i * BL, BL), BL), :]
        return g, _dot(tri2, _split_hi_lo(g))

    def decay_block(i, sums):
        rs = pl.ds(pl.multiple_of(i * BL, BL), BL)
        g, pp = sums
        p = pp[:, 0:2 * DK] + pp[:, 2 * DK:4 * DK]
        q_c = q_ref[0, rs, :].astype(F32) * scale
        k_c = k_ref[0, rs, :].astype(F32)
        zero = jnp.zeros((BL, DK), BF16)
        for d in range(2):
            p_d = p[:, d * DK:(d + 1) * DK]
            tot = jnp.where(par_row == 0, p_d[L - 1:L, :], p_d[BL - 1:BL, :])
            b = p_d if d == 0 else tot - p_d + g[:, DK:]
            qd = (q_c * jnp.exp2(b)).astype(BF16)
            kdec = (k_c * jnp.exp2(tot - b)).astype(BF16)
            qd_ref[d, rs, :] = qd
            kd_ref[d, rs, :] = (k_c * jnp.exp2(-b)).astype(BF16)
            for par in range(2):
                sel = par_row == par
                kdp_ref[d, rs, par * DK:(par + 1) * DK] = jnp.where(sel, kdec, zero)
                qcat_ref[rs, (2 * d + par) * DK:(2 * d + par + 1) * DK] = jnp.where(sel, qd, zero)
                last = (par + 1) * L - 1
                dec_ref[d, 2 * i + par] = jnp.broadcast_to(jnp.exp2(p_d[last:last + 1, :]), (8, DK))

    G = GLA_BLOCKS_PER_STEP
    n_groups = nb // G

    def intra_group(i):
        blks = [G * i + j for j in range(G)]
        rss = [pl.ds(pl.multiple_of(b * BL, BL), BL) for b in blks]
        vs = [v_ref[0, rs, :] for rs in rss]
        atts = [[_nt(qd_ref[d, rs, :], kd_ref[d, rs, :]) for d in range(2)] for rs in rss]
        for j, b in enumerate(blks):
            for d in range(2):
                x_ref[d, b] = _tn(vs[j], kdp_ref[d, rss[j], :])
        for j in range(G):
            att = jnp.where(masks[0], atts[j][0], 0.0) + jnp.where(masks[1], atts[j][1], 0.0)
            acc_ref[rss[j], :] = _dot(att.astype(BF16), vs[j])

    def fused_body(i, carry):
        sums = [decay_sums(G * (i + 1) + j) for j in range(G)]
        intra_group(i)
        for j in range(G):
            decay_block(G * (i + 1) + j, sums[j])
        return carry

    for j in range(G):
        decay_block(j, decay_sums(j))
    lax.fori_loop(0, n_groups - 1, fused_body, 0, unroll=True)
    intra_group(n_groups - 1)

    def state_body(d):
        def body(i, s):
            b = i if d == 0 else nb - 1 - i
            for par in ((0, 1) if d == 0 else (1, 0)):
                lane0 = (2 * d + par) * DK
                sst_ref[b, :, lane0:lane0 + DK] = s.astype(BF16)
                s = s * dec_ref[d, 2 * b + par, 0:1, :] + x_ref[d, b, :, par * DK:(par + 1) * DK]
            return s
        return body

    for d in range(2):
        lax.fori_loop(0, nb, state_body(d), jnp.zeros((GLA_DV, DK), F32), unroll=True)

    def out_body(b, carry):
        rs = pl.ds(pl.multiple_of(b * BL, BL), BL)
        o = acc_ref[rs, :] + _nt(qcat_ref[rs, :], sst_ref[b])
        ms = jnp.mean(o * o, axis=-1, keepdims=True)
        o = o * lax.rsqrt(ms + EPS) * nw_ref[...]
        o_ref[0, rs, :] = (o * _silu(gg_ref[0, rs, :].astype(F32))).astype(BF16)
        return carry

    lax.fori_loop(0, nb, out_body, 0, unroll=GLA_OUT_UNROLL)


def _gla(u3, us3, a2, bias, nw):
    B, S, _ = u3.shape
    DK, DV = GLA_DK, GLA_DV
    nc = S // GLA_CHUNK
    nb = nc // 2
    return pl.pallas_call(
        _gla_kernel,
        grid=(B, GLA_HEADS),
        in_specs=[
            pl.BlockSpec((1, S, DK), lambda b, h: (b, 0, U_GQ // DK + h)),
            pl.BlockSpec((1, S, DK), lambda b, h: (b, 0, U_GK // DK + h)),
            pl.BlockSpec((1, S, DV), lambda b, h: (b, 0, U_GV // DV + h)),
            pl.BlockSpec((1, S, DV), lambda b, h: (b, 0, U_GG // DV + h)),
            pl.BlockSpec((1, S, LANE), lambda b, h: (b, 0, 0)),
            pl.BlockSpec((LANE, 2 * DK), lambda b, h: (0, h)),
            pl.BlockSpec((1, 2 * DK), lambda b, h: (0, h)),
            pl.BlockSpec((1, DV), lambda b, h: (0, 0)),
        ],
        out_specs=pl.BlockSpec((1, S, DV), lambda b, h: (b, 0, h)),
        out_shape=jax.ShapeDtypeStruct((B, S, GLA_VAL_W), BF16),
        scratch_shapes=[
            pltpu.VMEM((S, DV), F32),
            pltpu.VMEM((S, 2 * DK), F32),
            pltpu.VMEM((2, S, DK), BF16),
            pltpu.VMEM((2, S, DK), BF16),
            pltpu.VMEM((2, S, 2 * DK), BF16),
            pltpu.VMEM((S, 4 * DK), BF16),
            pltpu.VMEM((2, nb, DV, 2 * DK), F32),
            pltpu.VMEM((2, nc, 8, DK), F32),
            pltpu.VMEM((nb, DV, 4 * DK), BF16),
        ],
        compiler_params=_cparams(("parallel", "parallel")),
        name="gla_scan",
    )(u3, u3, u3, u3, us3, a2, bias, nw)


def _na_bias_rows(rpb):
    H, R, C = rpb.shape
    n_pos = GRID_W - NA_WIN_W
    n_neg = GRID_W - NA_WIN_W + 2
    ext = jnp.concatenate([rpb[:, :, NA_WIN_W - 1:], jnp.repeat(rpb[:, :, C - 1:], n_pos, axis=2),
                           jnp.repeat(rpb[:, :, 0:1], n_neg, axis=2), rpb[:, :, 1:NA_WIN_W - 1]], axis=2)
    return jnp.pad(ext, ((0, 0), (0, 1), (0, 0))).reshape(H // 2, 2, R + 1, 2 * GRID_W)


def _na_kernel(flag_ref, q_ref, k_ref, v_ref, ext_ref, o_ref, vx_ref, tab_ref):
    NB, S, _ = q_ref.shape
    rows = S // GRID_W
    total = NB * rows
    win_h = NA_WIN_H
    nk = win_h * GRID_W

    vx_ref[:, :, 0:LANE] = v_ref[...]
    vx_ref[:, :, LANE:2 * LANE] = jnp.ones((NB, S, LANE), BF16)
    bound_ok = flag_ref[0] != 0

    first_q = _iota2((GRID_W, LANE), 1) < NA_HEAD_DIM

    @pl.when(pl.program_id(1) == 0)
    def _():
        q_col = _iota2((GRID_W, 2 * GRID_W), 0)
        k_col = _iota2((GRID_W, 2 * GRID_W), 1) % GRID_W
        w_start = jnp.clip(q_col - NA_WIN_W // 2, 0, GRID_W - NA_WIN_W)
        in_window = (k_col >= w_start) & (k_col < w_start + NA_WIN_W)
        low = _iota2((GRID_W, 2 * GRID_W), 1) < GRID_W
        for hd in range(2):
            def skewed(rr, shift):
                row = jnp.broadcast_to(ext_ref[0, hd, rr:rr + 1, :], (GRID_W, 2 * GRID_W))
                return pltpu.roll(row, shift, axis=1, stride=1, stride_axis=0)

            for e in range(2 * NA_WIN_H - 2):
                t = jnp.where(low, skewed(e, 0), skewed(e + 1, GRID_W))
                tab_ref[e, hd * GRID_W:(hd + 1) * GRID_W, :] = jnp.where(in_window, t, -jnp.inf)

    def locate(r):
        bi = r // rows
        rl = r - bi * rows
        r0 = jnp.clip(rl - win_h // 2, 0, rows - win_h)
        return bi, pl.multiple_of(rl * GRID_W, GRID_W), rl - r0, pl.multiple_of(r0 * GRID_W, GRID_W)

    def scores(r):
        bi, q0, delta, k0 = locate(r)
        q = q_ref[bi, pl.ds(q0, GRID_W), :]
        zero = jnp.zeros_like(q)
        qs = jnp.concatenate([jnp.where(first_q, q, zero), jnp.where(first_q, zero, q)], axis=0)
        rr0 = (win_h - 1) - delta
        bias = jnp.concatenate([tab_ref[rr0 + w] for w in range(0, win_h, 2)], axis=1)
        return _nt(qs, k_ref[bi, pl.ds(k0, nk), :]) + bias

    def probs_exact(s):
        return jnp.exp2(s - jnp.max(s, axis=-1, keepdims=True)).astype(BF16)

    def probs_bounded(r):
        return jnp.exp2(scores(r)).astype(BF16)

    def attend(r, p):
        bi, q0, _, k0 = locate(r)
        ox = _dot(p, vx_ref[bi, pl.ds(k0, nk), :])
        o = ox[:, 0:LANE] / ox[:, LANE:2 * LANE]
        o_ref[bi, pl.ds(q0, GRID_W), :] = jnp.where(first_q, o[:GRID_W], o[GRID_W:]).astype(BF16)

    @pl.when(bound_ok)
    def _():
        U = NA_ROWS_PER_STEP

        def row_body(i, p_prev):
            r = i * U
            for j in range(U):
                attend(r - U + j, p_prev[j])
            return tuple(probs_bounded(r + j) for j in range(U))

        p_last = lax.fori_loop(1, total // U, row_body, tuple(probs_bounded(j) for j in range(U)), unroll=True)
        for j in range(U):
            attend(total - U + j, p_last[j])

    @pl.when(jnp.logical_not(bound_ok))
    def _():
        U = NA_ROWS_PER_STEP_EXACT

        def row_body(i, carry):
            s_cur, p_prev = carry
            r = i * U
            for j in range(U):
                attend(r - U + j, p_prev[j])
            p = tuple(probs_exact(s) for s in s_cur)
            s_next = tuple(scores(jnp.minimum(r + U + j, total - 1)) for j in range(U))
            return s_next, p

        p0 = tuple(probs_exact(scores(j)) for j in range(U))
        s1 = tuple(scores(U + j) for j in range(U))
        _, p_last = lax.fori_loop(1, total // U, row_body, (s1, p0))
        for j in range(U):
            attend(total - U + j, p_last[j])


def _na_score_bound(rpb, q_norm_w, k_norm_w):
    H = rpb.shape[0]
    qk = (NA_HEAD_DIM ** 0.5 * LOG2E * (1.0 + NA_BOUND_SLACK)) * jnp.max(jnp.abs(q_norm_w)) * jnp.max(jnp.abs(k_norm_w))
    b_max = jnp.max(rpb.reshape(H, -1), axis=1) * LOG2E
    b_self = rpb[:, NA_WIN_H - 1, NA_WIN_W - 1] * LOG2E
    bound = qk + b_max
    flag = jnp.all(bound - (b_self - qk) <= NA_MAX_BOUND_GAP)
    return bound, flag.astype(jnp.int32).reshape(1)


def _na(u3, bias_rows, flag):
    B, S, _ = u3.shape
    assert S // GRID_W >= NA_WIN_H and NA_WIN_H % 2 == 0 and 2 * GRID_W == LANE
    nb = math.gcd(B, NA_BATCH_PER_STEP)
    assert (nb * (S // GRID_W)) % NA_ROWS_PER_STEP == 0
    return pl.pallas_call(
        _na_kernel,
        grid=(NA_HEADS // 2, B // nb),
        in_specs=[
            pl.BlockSpec(memory_space=pltpu.SMEM),
            pl.BlockSpec((nb, S, LANE), lambda h, b: (b, 0, U_NQ // LANE + h)),
            pl.BlockSpec((nb, S, LANE), lambda h, b: (b, 0, U_NK // LANE + h)),
            pl.BlockSpec((nb, S, LANE), lambda h, b: (b, 0, U_NV // LANE + h)),
            pl.BlockSpec((1, 2, 2 * NA_WIN_H, 2 * GRID_W), lambda h, b: (h, 0, 0, 0)),
        ],
        out_specs=pl.BlockSpec((nb, S, LANE), lambda h, b: (b, 0, h)),
        out_shape=jax.ShapeDtypeStruct((B, S, NA_W), BF16),
        scratch_shapes=[
            pltpu.VMEM((nb, S, 2 * LANE), BF16),
            pltpu.VMEM((2 * NA_WIN_H - 2, 2 * GRID_W, 2 * GRID_W), F32),
        ],
        compiler_params=_cparams(("parallel", "arbitrary")),
        name="na_attn",
    )(flag, u3, u3, u3, bias_rows)


def _merge_kernel(x_ref, ys_ref, yg_ref, yn_ref, gate_ref, ws_ref, wg_ref, wn_ref, wo_ref, o_ref):
    D = D_MODEL
    mixed = _sigmoid(gate_ref[:, 0:D].astype(F32)) * _dot(ys_ref[...], ws_ref[0])
    mixed += _sigmoid(gate_ref[:, D:2 * D].astype(F32)) * _dot(yg_ref[...], wg_ref[0])
    mixed += _sigmoid(gate_ref[:, 2 * D:3 * D].astype(F32)) * _dot(yn_ref[...], wn_ref[0])
    o_ref[...] = x_ref[...] + _dot(mixed.astype(BF16), wo_ref[0])


def _merge(x2, ys, yg, yn, u2, ws, wg, wn, wo, layer, tm=512):
    T = x2.shape[0]
    D = D_MODEL
    row = lambda i: (i, 0)
    fixed = lambda i: (layer, 0, 0)
    return pl.pallas_call(
        _merge_kernel,
        grid=(T // tm,),
        in_specs=[
            pl.BlockSpec((tm, D), row),
            pl.BlockSpec((tm, D), row),
            pl.BlockSpec((tm, D), row),
            pl.BlockSpec((tm, D), row),
            pl.BlockSpec((tm, N_BRANCH * D), lambda i: (i, U_GATE // (N_BRANCH * D))),
            pl.BlockSpec((1, D, D), fixed),
            pl.BlockSpec((1, D, D), fixed),
            pl.BlockSpec((1, D, D), fixed),
            pl.BlockSpec((1, D, D), fixed),
        ],
        out_specs=pl.BlockSpec((tm, D), row),
        out_shape=jax.ShapeDtypeStruct((T, D), F32),
        compiler_params=_cparams(("parallel",)),
        name="merge",
    )(x2, ys, yg, yn, u2, ws, wg, wn, wo)


def _mlp_kernel(x_ref, nw_ref, w1_ref, w2_ref, o_ref, *, tf):
    x = x_ref[...]
    ms = jnp.mean(x * x, axis=-1, keepdims=True)
    h = (x * lax.rsqrt(ms + EPS) * nw_ref[...]).astype(BF16)
    acc = x
    for f in range(D_FF // tf):
        a = jnp.maximum(_dot(h, w1_ref[0, :, f * tf:(f + 1) * tf]), 0.0)
        acc = acc + _dot((a * a).astype(BF16), w2_ref[0, f * tf:(f + 1) * tf, :])
    o_ref[...] = acc


def _mlp(x2, nw, w1, w2, layer, tm=1024, tf=1024):
    T = x2.shape[0]
    D = D_MODEL
    resident = pl.Buffered(1)
    return pl.pallas_call(
        functools.partial(_mlp_kernel, tf=tf),
        grid=(T // tm,),
        in_specs=[
            pl.BlockSpec((tm, D), lambda i: (i, 0)),
            pl.BlockSpec((1, D), lambda i: (0, 0)),
            pl.BlockSpec((1, D, D_FF), lambda i: (layer, 0, 0), pipeline_mode=resident),
            pl.BlockSpec((1, D_FF, D), lambda i: (layer, 0, 0), pipeline_mode=resident),
        ],
        out_specs=pl.BlockSpec((tm, D), lambda i: (i, 0)),
        out_shape=jax.ShapeDtypeStruct((T, D), F32),
        compiler_params=_cparams(("parallel",)),
        name="mlp",
    )(x2, nw, w1, w2)


def _pad_rows(w, start, total):
    return jnp.zeros((total, w.shape[1]), w.dtype).at[start:start + w.shape[0]].set(w)


def kernel(x, norm_mix_w, w_in, ssd_conv_w, ssd_conv_b, ssd_dt_bias_f, ssd_dt_bias_b, ssd_a_log_f,
           ssd_a_log_b, ssd_d, ssd_norm_w, gla_a2_f, gla_a2_bias_f, gla_a2_b, gla_a2_bias_b,
           gla_norm_w, na_q_norm_w, na_k_norm_w, na_rpb, w_branch_ssd, w_branch_gla, w_branch_na,
           w_out, norm_mlp_w, w_ff1, w_ff2):
    B, S, D = x.shape
    T = B * S
    depth = w_in.shape[0]
    x2 = x.reshape(T, D)
    w_in_t = jnp.swapaxes(w_in, 1, 2)
    wb_ssd, wb_gla, wb_na, wb_out = (w.astype(BF16) for w in (w_branch_ssd, w_branch_gla, w_branch_na, w_out))
    w1_all, w2_all = w_ff1.astype(BF16), w_ff2.astype(BF16)
    for l in range(depth):
        w_big, w_small = _permute_weight(w_in_t, l)
        hg = (SSD_GROUPS, SSD_HG)
        zeros_r = jnp.zeros((SSD_GROUPS, LANE - 2 * SSD_HG), F32)
        dt_bias = jnp.concatenate([ssd_dt_bias_f[l].reshape(hg), ssd_dt_bias_b[l].reshape(hg), zeros_r], axis=1)
        a_neg = jnp.concatenate([-jnp.exp(ssd_a_log_f[l]).reshape(hg), -jnp.exp(ssd_a_log_b[l]).reshape(hg),
                                 zeros_r], axis=1)
        prow3 = jnp.concatenate([dt_bias[:, None], a_neg[:, None], jnp.zeros((SSD_GROUPS, 6, LANE), F32)], axis=1)
        prow = prow3.reshape(SSD_GROUPS * 8, LANE)
        pcol = jnp.transpose(prow3[:, :, :SMALL_T_ROWS], (0, 2, 1))
        drow = jnp.repeat(ssd_d[l], SSD_HEAD_DIM)[None, :]
        hk = (GLA_HEADS, 1, GLA_DK)
        a2 = jnp.concatenate([_pad_rows(gla_a2_f[l], SM_GAF, LANE).reshape((LANE,) + hk),
                              _pad_rows(gla_a2_b[l], SM_GAB, LANE).reshape((LANE,) + hk)],
                             axis=2).reshape(LANE, -1).astype(BF16)
        a2_bias = jnp.concatenate([gla_a2_bias_f[l].reshape(hk), gla_a2_bias_b[l].reshape(hk)],
                                  axis=1).reshape(1, -1)
        na_bound, na_flag = _na_score_bound(na_rpb[l], na_q_norm_w[l], na_k_norm_w[l])
        table = _na_bias_rows(na_rpb[l] * LOG2E - na_bound[:, None, None])
        q_row = jnp.tile(na_q_norm_w[l] * (NA_HEAD_DIM ** -0.5 * LOG2E), NA_HEADS)
        k_row = jnp.tile(na_k_norm_w[l], NA_HEADS)
        qkw = jnp.concatenate([q_row[None], k_row[None], jnp.zeros((6, NA_W), F32)], axis=0)

        u2, us2, ust = _inproj(x2, norm_mix_w[l][None, :], w_big, w_small, qkw)
        u3 = u2.reshape(B, S, U_WIDTH)
        us3 = us2.reshape(B, S, SMALL_W)
        xbc = _conv(u3, ssd_conv_w[l], ssd_conv_b[l][None, :])
        y_ssd = _ssd(xbc, u3, ust, prow, pcol, drow, ssd_norm_w[l][None, :])
        y_gla = _gla(u3, us3, a2, a2_bias, gla_norm_w[l][None, :])
        y_na = _na(u3, table, na_flag)
        x2 = _merge(x2, y_ssd.reshape(T, -1), y_gla.reshape(T, -1), y_na.reshape(T, -1), u2,
                    wb_ssd, wb_gla, wb_na, wb_out, l)
        x2 = _mlp(x2, norm_mlp_w[l][None, :], w1_all, w2_all, l)
    return x2.reshape(B, S, D)
```

```python
import functools
import math

import jax
import jax.numpy as jnp
import numpy as np
from jax import lax
from jax.experimental import pallas as pl
from jax.experimental.pallas import tpu as pltpu

F32 = jnp.float32
BF16 = jnp.bfloat16

EPS = 1e-6
D_MODEL = 1024
GRID_W = 64

SSD_HEADS = 16
SSD_HEAD_DIM = 64
SSD_D_INNER = 1024
SSD_GROUPS = 2
SSD_STATE = 128
SSD_CONV = 5
SSD_CONV_DIM = 1536
SSD_CHUNK = 128
CONV_IN_BLOCKS = 3
CONV_TILE = 256
CONV_ROWS = 128
SSD_CHUNKS_PER_STEP = 2
SSD_OUT_UNROLL = 16
SSD_GROUP_W = SSD_D_INNER // SSD_GROUPS
SSD_HG = SSD_HEADS // SSD_GROUPS

GLA_HEADS = 4
GLA_DK = 128
GLA_DV = 256
GLA_KEY_W = 512
GLA_VAL_W = 1024
GLA_GATE_RANK = 16
GLA_GATE_NORM = 16.0
GLA_CHUNK = 64
GLA_BLOCKS_PER_STEP = 2
GLA_HEADS_PER_STEP = 2
GLA_OUT_UNROLL = 16

NA_HEADS = 16
NA_HEAD_DIM = 64
NA_W = 1024
NA_WIN_H = 8
NA_WIN_W = 16
NA_BATCH_PER_STEP = 4
NA_ROWS_PER_STEP = 32
NA_ROWS_PER_STEP_EXACT = 2
LOG2E = 1.4426950408889634
NA_BOUND_SLACK = 0.02
NA_MAX_BOUND_GAP = 90.0

N_BRANCH = 3
D_FF = 4096

IN_SIZES = (SSD_D_INNER, SSD_CONV_DIM, SSD_HEADS, SSD_HEADS,
            GLA_KEY_W, GLA_KEY_W, GLA_VAL_W, GLA_VAL_W, GLA_GATE_RANK, GLA_GATE_RANK,
            NA_W, NA_W, NA_W, N_BRANCH * D_MODEL)
_IN_OFF = np.concatenate([[0], np.cumsum(IN_SIZES)])
(_O_Z, _O_XBC, _O_DTF, _O_DTB, _O_GQ, _O_GK, _O_GV, _O_GG, _O_GAF, _O_GAB,
 _O_NQ, _O_NK, _O_NV, _O_GATE) = [int(v) for v in _IN_OFF[:-1]]

U_NQ = 0
U_NK = 1024
U_NV = 2048
U_GATE = 3072
U_Z = 6144
U_XBC = 7168
U_GQ = 8704
U_GK = 9216
U_GV = 9728
U_GG = 10752
U_WIDTH = 11776
INPROJ_COL_CHUNK = 1024
LANE = 128
VMEM_LIMIT = 56 * 1024 * 1024

SMALL_W = SSD_GROUPS * LANE
SM_DTF, SM_DTB, SM_GAF, SM_GAB = 0, 8, 16, 32
SMALL_T_ROWS = 2 * SSD_HG

_BIG_SEGS = ((_O_NQ, 6144), (_O_Z, 2560), (_O_GQ, 3072))


def _cparams(sem, vmem=VMEM_LIMIT):
    return pltpu.CompilerParams(dimension_semantics=sem, vmem_limit_bytes=vmem)


def _sigmoid(x):
    return 1.0 / (1.0 + jnp.exp2(x * (-LOG2E)))


def _silu(x):
    return x * _sigmoid(x)


def _softplus(x):
    return jnp.maximum(x, 0.0) + jnp.log1p(jnp.exp(-jnp.abs(x)))


def _neg_log2_sigmoid(x):
    t = x * (-LOG2E)
    return jnp.maximum(t, 0.0) + jnp.log2(1.0 + jnp.exp2(-jnp.abs(t)))


def _nt(a, b):
    return lax.dot_general(a, b, (((1,), (1,)), ((), ())), preferred_element_type=F32)


def _tn(a, b):
    return lax.dot_general(a, b, (((0,), (0,)), ((), ())), preferred_element_type=F32)


def _dot(a, b):
    return jnp.dot(a, b, preferred_element_type=F32)


def _iota2(shape, dim):
    return lax.broadcasted_iota(jnp.int32, shape, dim)


def _wperm_kernel(wt_ref, o_ref, ws_ref):
    off = 0
    for a, n in _BIG_SEGS:
        o_ref[off:off + n, :] = wt_ref[0, a:a + n, :].astype(BF16)
        off += n
    ws_ref[...] = jnp.zeros_like(ws_ref)
    for g in range(SSD_GROUPS):
        base, h0 = g * LANE, g * SSD_HG
        ws_ref[base + SM_DTF:base + SM_DTF + SSD_HG, :] = wt_ref[0, _O_DTF + h0:_O_DTF + h0 + SSD_HG, :]
        ws_ref[base + SM_DTB:base + SM_DTB + SSD_HG, :] = wt_ref[0, _O_DTB + h0:_O_DTB + h0 + SSD_HG, :]
    ws_ref[SM_GAF:SM_GAF + GLA_GATE_RANK, :] = wt_ref[0, _O_GAF:_O_GAF + GLA_GATE_RANK, :]
    ws_ref[SM_GAB:SM_GAB + GLA_GATE_RANK, :] = wt_ref[0, _O_GAB:_O_GAB + GLA_GATE_RANK, :]


def _permute_weight(w_t_all, layer, tc=256):
    _, n_in, d = w_t_all.shape
    return pl.pallas_call(
        _wperm_kernel,
        grid=(d // tc,),
        in_specs=[pl.BlockSpec((1, n_in, tc), lambda i: (layer, 0, i))],
        out_specs=[
            pl.BlockSpec((U_WIDTH, tc), lambda i: (0, i)),
            pl.BlockSpec((SMALL_W, tc), lambda i: (0, i)),
        ],
        out_shape=[
            jax.ShapeDtypeStruct((U_WIDTH, d), BF16),
            jax.ShapeDtypeStruct((SMALL_W, d), F32),
        ],
        compiler_params=_cparams(("parallel",)),
        name="w_permute",
    )(w_t_all)


def _head_rms(r, w_row):
    G = 2 * LANE
    er = _iota2((G, G), 0) // NA_HEAD_DIM
    ec = _iota2((G, G), 1) // NA_HEAD_DIM
    e_blk = jnp.where(er == ec, 1.0 / NA_HEAD_DIM, 0.0).astype(BF16)
    outs = []
    for a in range(0, r.shape[1], G):
        x = r[:, a:a + G]
        ms = _dot((x * x).astype(BF16), e_blk)
        outs.append(x * lax.rsqrt(ms + EPS))
    return jnp.concatenate(outs, axis=1) * w_row


def _inproj_kernel(x_ref, nw_ref, w_ref, ws_ref, qkw_ref, u_ref, us_ref, ust_ref, h_ref):
    tn = u_ref.shape[1]
    chunks = [(a, min(a + INPROJ_COL_CHUNK, tn)) for a in range(0, tn, INPROJ_COL_CHUNK)]

    @pl.when(pl.program_id(1) == 0)
    def _():
        x = x_ref[...]
        ms = jnp.mean(x * x, axis=-1, keepdims=True)
        h = (x * lax.rsqrt(ms + EPS) * nw_ref[...]).astype(BF16)
        h_ref[...] = h
        ws = ws_ref[...].astype(BF16)
        us_ref[...] = _nt(h, ws)
        dt_rows = jnp.concatenate([ws[g * LANE:g * LANE + SMALL_T_ROWS] for g in range(SSD_GROUPS)], axis=0)
        ust_ref[...] = _nt(dt_rows, h)
        for c, (a, b) in enumerate(chunks):
            r = _nt(h, w_ref[a:b, :])
            if c < 2:
                r = _head_rms(r, qkw_ref[c:c + 1, :])
            u_ref[:, a:b] = r.astype(BF16)

    @pl.when(pl.program_id(1) != 0)
    def _():
        for a, b in chunks:
            u_ref[:, a:b] = _nt(h_ref[...], w_ref[a:b, :]).astype(BF16)


def _inproj(x2, nw, w_big, w_small, qkw, tm=1024, tn=U_WIDTH // 4):
    assert (U_NQ, U_NK) == (0, INPROJ_COL_CHUNK) and NA_W == INPROJ_COL_CHUNK
    T = x2.shape[0]
    return pl.pallas_call(
        _inproj_kernel,
        grid=(T // tm, U_WIDTH // tn),
        in_specs=[
            pl.BlockSpec((tm, D_MODEL), lambda i, j: (i, 0)),
            pl.BlockSpec((1, D_MODEL), lambda i, j: (0, 0)),
            pl.BlockSpec((tn, D_MODEL), lambda i, j: (j, 0)),
            pl.BlockSpec((SMALL_W, D_MODEL), lambda i, j: (0, 0)),
            pl.BlockSpec((8, NA_W), lambda i, j: (0, 0)),
        ],
        out_specs=[
            pl.BlockSpec((tm, tn), lambda i, j: (i, j)),
            pl.BlockSpec((tm, SMALL_W), lambda i, j: (i, 0)),
            pl.BlockSpec((SSD_GROUPS * SMALL_T_ROWS, tm), lambda i, j: (0, i)),
        ],
        out_shape=[
            jax.ShapeDtypeStruct((T, U_WIDTH), BF16),
            jax.ShapeDtypeStruct((T, SMALL_W), F32),
            jax.ShapeDtypeStruct((SSD_GROUPS * SMALL_T_ROWS, T), F32),
        ],
        scratch_shapes=[pltpu.VMEM((tm, D_MODEL), BF16)],
        compiler_params=_cparams(("parallel", "arbitrary")),
        name="inproj",
    )(x2, nw, w_big, w_small, qkw)


def _conv_kernel(*refs):
    u_refs, (w_ref, b_ref, o_ref, xp_ref) = refs[:CONV_IN_BLOCKS], refs[CONV_IN_BLOCKS:]
    S, wb = u_refs[0].shape[1], u_refs[0].shape[2]
    C = o_ref.shape[2]
    R = CONV_ROWS
    tc = CONV_TILE
    pad = SSD_CONV // 2
    xp_ref[0:R, :] = jnp.zeros((R, C), BF16)
    xp_ref[R + S:R + S + R, :] = jnp.zeros((R, C), BF16)
    for i, u_ref in enumerate(u_refs):
        xp_ref[R:R + S, i * wb:(i + 1) * wb] = u_ref[0]
    t_i = _iota2((R, 2 * R), 0)
    j_i = _iota2((R, 2 * R), 1)
    side = [k for k in range(SSD_CONV) if k != pad]
    shifts = jnp.concatenate([(j_i == t_i + R // 2 + (k - pad)).astype(BF16) for k in side], axis=0)
    for blk in range(S // R):
        for c0 in range(0, C, tc):
            cs = slice(c0, c0 + tc)
            win = xp_ref[blk * R + R // 2:blk * R + R // 2 + 2 * R, cs]
            sh = _dot(shifts, win)
            acc = b_ref[:, cs] + w_ref[pad:pad + 1, cs] * xp_ref[R + blk * R:R + (blk + 1) * R, cs].astype(F32)
            for n, k in enumerate(side):
                acc = acc + w_ref[k:k + 1, cs] * sh[n * R:(n + 1) * R]
            o_ref[0, blk * R:(blk + 1) * R, cs] = _silu(acc).astype(BF16)


def _conv(u3, conv_w, conv_b):
    B, S, _ = u3.shape
    wb = SSD_CONV_DIM // CONV_IN_BLOCKS
    assert U_XBC % wb == 0 and wb % CONV_TILE == 0
    off = U_XBC // wb
    in_blocks = [pl.BlockSpec((1, S, wb), functools.partial(lambda b, i: (b, 0, off + i), i=i))
                 for i in range(CONV_IN_BLOCKS)]
    return pl.pallas_call(
        _conv_kernel,
        grid=(B,),
        in_specs=in_blocks + [
            pl.BlockSpec((SSD_CONV, SSD_CONV_DIM), lambda b: (0, 0)),
            pl.BlockSpec((1, SSD_CONV_DIM), lambda b: (0, 0)),
        ],
        out_specs=pl.BlockSpec((1, S, SSD_CONV_DIM), lambda b: (b, 0, 0)),
        out_shape=jax.ShapeDtypeStruct((B, S, SSD_CONV_DIM), BF16),
        scratch_shapes=[pltpu.VMEM((S + 2 * CONV_ROWS, SSD_CONV_DIM), BF16)],
        compiler_params=_cparams(("parallel",)),
        name="ssd_conv",
    )(*([u3] * CONV_IN_BLOCKS), conv_w, conv_b)


def _split_hi_lo(v):
    hi = v.astype(BF16)
    lo = (v - hi.astype(F32)).astype(BF16)
    return jnp.concatenate([hi, lo], axis=1)


def _split3(v, axis):
    hi = v.astype(BF16)
    r1 = v - hi.astype(F32)
    mid = r1.astype(BF16)
    lo = (r1 - mid.astype(F32)).astype(BF16)
    return jnp.concatenate([hi, mid, lo], axis=axis)


def _ssd_kernel(x_ref, b_ref, c_ref, z_ref, ust_ref, prow_ref, pcol_ref, drow_ref, nw_ref,
                o_ref, acc_ref, cumc_ref, cumr_ref, wdt_ref, dec_ref, xs_ref, decx_ref, sst_ref, st_ref):
    L = SSD_CHUNK
    S = x_ref.shape[1]
    nc = S // L
    W = SSD_GROUP_W
    HG = SSD_HG
    R = 2 * HG

    ii = _iota2((L, L), 0)
    jj = _iota2((L, L), 1)
    tril = ii >= jj
    triu = jj >= ii
    tril_b = tril.astype(BF16)
    triu_b = triu.astype(BF16)

    er = _iota2((LANE, 2 * W), 0)
    ec = _iota2((LANE, 2 * W), 1)
    e = (er == jnp.where(ec < W, ec // SSD_HEAD_DIM, (ec - W) // SSD_HEAD_DIM + HG)).astype(BF16)
    e2 = jnp.concatenate([e, e], axis=0)

    a_row = prow_ref[1:2, :]
    bias_col = pcol_ref[0, :, 0:1]
    a_col = pcol_ref[0, :, 1:2]

    lane_fwd = _iota2((L, LANE), 1) < HG
    row_fwd = _iota2((R, L), 0) < HG
    lane_half = _iota2((L, LANE), 1) < SSD_HEAD_DIM

    def decay_sums(c):
        rs = pl.ds(pl.multiple_of(c * L, L), L)
        dt_r = _softplus(ust_ref[:, rs] + bias_col)
        a_r = dt_r * a_col
        dt_c = jnp.concatenate([dt_r, jnp.zeros((L - R, L), F32)], axis=0).T
        a_c = dt_c * a_row
        return dt_c, a_c, _dot(tril_b, _split3(a_c, 1)), dt_r, a_r, _dot(_split3(a_r, 0), triu_b)

    def decay_store(c, sums):
        rs = pl.ds(pl.multiple_of(c * L, L), L)
        dt_c, a_c, pp, dt_r, a_r, pr = sums
        p_c = pp[:, 0:LANE] + pp[:, LANE:2 * LANE] + pp[:, 2 * LANE:3 * LANE]
        tot_c = p_c[L - 1:L, :]
        cum_c = jnp.where(lane_fwd, p_c, tot_c - p_c + a_c)
        cumc_ref[rs, :] = cum_c * LOG2E
        wdt_ref[rs, :] = jnp.exp(tot_c - cum_c) * dt_c
        dec_ref[c] = jnp.broadcast_to(jnp.exp(tot_c), (16, LANE))
        p_r = pr[0:R] + pr[R:2 * R] + pr[2 * R:3 * R]
        tot_r = p_r[:, L - 1:L]
        cum_r = jnp.where(row_fwd, p_r, tot_r - p_r + a_r)
        cumr_ref[:, rs] = (cum_r - jnp.log(dt_r)) * LOG2E

    def intra_chunk(c):
        rs = pl.ds(pl.multiple_of(c * L, L), L)
        x_c = x_ref[0, rs, :]
        b_c = b_ref[0, rs, :]
        cb = _nt(c_ref[0, rs, :], b_c)
        ex = _dot(_split_hi_lo(jnp.concatenate([wdt_ref[rs, :], dec_ref[c]], axis=0)), e2)
        decx_ref[c] = ex[L:L + 8]
        xf = x_c.astype(F32)
        xw = jnp.concatenate([xf * ex[0:L, 0:W], xf * ex[0:L, W:2 * W]], axis=1).astype(BF16)
        xs_ref[c] = _tn(b_c, xw)
        cum_c = cumc_ref[rs, :]
        cum_r = cumr_ref[:, rs]
        for hp in range(HG // 2):
            ms = []
            for hh in range(2):
                hf = 2 * hp + hh
                hb = HG + 2 * hp + hh
                decf = jnp.exp2(jnp.where(tril, cum_c[:, hf:hf + 1] - cum_r[hf:hf + 1, :], -jnp.inf))
                decb = jnp.exp2(jnp.where(triu, cum_c[:, hb:hb + 1] - cum_r[hb:hb + 1, :], -jnp.inf))
                ms.append((cb * (decf + decb)).astype(BF16))
            m2 = jnp.concatenate(ms, axis=1)
            xp = x_c[:, hp * LANE:(hp + 1) * LANE]
            xz = jnp.zeros_like(xp)
            x2 = jnp.concatenate([jnp.where(lane_half, xp, xz), jnp.where(lane_half, xz, xp)], axis=0)
            acc_ref[rs, hp * LANE:(hp + 1) * LANE] = _dot(m2, x2)

    G = SSD_CHUNKS_PER_STEP
    n_steps = nc // G

    def fused_body(i, carry):
        sums = [decay_sums(G * (i + 1) + j) for j in range(G)]
        for j in range(G):
            intra_chunk(G * i + j)
        for j in range(G):
            decay_store(G * (i + 1) + j, sums[j])
        return carry

    for j in range(G):
        decay_store(j, decay_sums(j))
    lax.fori_loop(0, n_steps - 1, fused_body, 0, unroll=True)
    for j in range(G):
        intra_chunk(G * (n_steps - 1) + j)

    st_ref[...] = jnp.zeros_like(st_ref)

    def state_body(i, carry):
        for d, c in enumerate((i, nc - 1 - i)):
            ls = slice(d * W, (d + 1) * W)
            s_prev = st_ref[:, ls]
            sst_ref[c, :, ls] = s_prev.astype(BF16)
            st_ref[:, ls] = s_prev * decx_ref[c, 0:1, ls] + xs_ref[c, :, ls]
        return carry

    lax.fori_loop(0, nc, state_body, 0, unroll=True)

    def out_body(c, carry):
        rs = pl.ds(pl.multiple_of(c * L, L), L)
        c_c = c_ref[0, rs, :]
        ex = _split_hi_lo(jnp.exp2(cumc_ref[rs, :]))
        y = acc_ref[rs, :] + x_ref[0, rs, :].astype(F32) * drow_ref[...]
        for d in range(2):
            ls = slice(d * W, (d + 1) * W)
            y = y + _dot(c_c, sst_ref[c, :, ls]) * _dot(ex, e2[:, ls])
        y = y * _silu(z_ref[0, rs, :].astype(F32))
        ms = jnp.mean(y * y, axis=-1, keepdims=True)
        o_ref[0, rs, :] = (y * lax.rsqrt(ms + EPS) * nw_ref[...]).astype(BF16)
        return carry

    lax.fori_loop(0, nc, out_body, 0, unroll=SSD_OUT_UNROLL)


def _ssd(xbc, u3, ust, prow, pcol, drow, nw):
    B, S, _ = xbc.shape
    W = SSD_GROUP_W
    N = SSD_STATE
    nc = S // SSD_CHUNK
    return pl.pallas_call(
        _ssd_kernel,
        grid=(B, SSD_GROUPS),
        in_specs=[
            pl.BlockSpec((1, S, W), lambda b, g: (b, 0, g)),
            pl.BlockSpec((1, S, N), lambda b, g: (b, 0, SSD_D_INNER // N + g)),
            pl.BlockSpec((1, S, N), lambda b, g: (b, 0, SSD_D_INNER // N + SSD_GROUPS + g)),
            pl.BlockSpec((1, S, W), lambda b, g: (b, 0, U_Z // W + g)),
            pl.BlockSpec((SMALL_T_ROWS, S), lambda b, g: (g, b)),
            pl.BlockSpec((8, LANE), lambda b, g: (g, 0)),
            pl.BlockSpec((1, SMALL_T_ROWS, 8), lambda b, g: (g, 0, 0)),
            pl.BlockSpec((1, W), lambda b, g: (0, g)),
            pl.BlockSpec((1, W), lambda b, g: (0, g)),
        ],
        out_specs=pl.BlockSpec((1, S, W), lambda b, g: (b, 0, g)),
        out_shape=jax.ShapeDtypeStruct((B, S, SSD_D_INNER), BF16),
        scratch_shapes=[
            pltpu.VMEM((S, W), F32),
            pltpu.VMEM((S, LANE), F32),
            pltpu.VMEM((SMALL_T_ROWS, S), F32),
            pltpu.VMEM((S, LANE), F32),
            pltpu.VMEM((nc, 16, LANE), F32),
            pltpu.VMEM((nc, N, 2 * W), F32),
            pltpu.VMEM((nc, 8, 2 * W), F32),
            pltpu.VMEM((nc, N, 2 * W), BF16),
            pltpu.VMEM((N, 2 * W), F32),
        ],
        compiler_params=_cparams(("parallel", "parallel")),
        name="ssd_scan",
    )(xbc, xbc, xbc, u3, ust, prow, pcol, drow, nw)


def _gla_kernel(q_ref, k_ref, v_ref, gg_ref, us_ref, a2_ref, bias_ref, nw_ref,
                o_ref, acc_ref, g_ref, qd_ref, kd_ref, kdp_ref, qcat_ref, x_ref, dec_ref, sst_ref):
    L = GLA_CHUNK
    BL = 2 * L
    DK = GLA_DK
    S = q_ref.shape[1]
    nb = S // BL
    scale = DK ** -0.5

    ii = _iota2((BL, BL), 0)
    jj = _iota2((BL, BL), 1)
    same = (ii // L) == (jj // L)
    masks = (same & (ii >= jj), same & (jj >= ii))
    tri2 = masks[0].astype(BF16)
    par_row = _iota2((BL, DK), 0) // L

    ga = us_ref[0].astype(BF16)
    g_ref[...] = _neg_log2_sigmoid(_dot(ga, a2_ref[...]) + bias_ref[...]) * (-1.0 / GLA_GATE_NORM)

    def decay_sums(i):
        g = g_ref[pl.ds(pl.multiple_of(i * BL, BL), BL), :]
        return g, _dot(tri2, _split_hi_lo(g))

    def decay_block(i, sums):
        rs = pl.ds(pl.multiple_of(i * BL, BL), BL)
        g, pp = sums
        p = pp[:, 0:2 * DK] + pp[:, 2 * DK:4 * DK]
        q_c = q_ref[0, rs, :].astype(F32) * scale
        k_c = k_ref[0, rs, :].astype(F32)
        zero = jnp.zeros((BL, DK), BF16)
        for d in range(2):
            p_d = p[:, d * DK:(d + 1) * DK]
            tot = jnp.where(par_row == 0, p_d[L - 1:L, :], p_d[BL - 1:BL, :])
            b = p_d if d == 0 else tot - p_d + g[:, DK:]
            qd = (q_c * jnp.exp2(b)).astype(BF16)
            kdec = (k_c * jnp.exp2(tot - b)).astype(BF16)
            qd_ref[d, rs, :] = qd
            kd_ref[d, rs, :] = (k_c * jnp.exp2(-b)).astype(BF16)
            for par in range(2):
                sel = par_row == par
                kdp_ref[d, rs, par * DK:(par + 1) * DK] = jnp.where(sel, kdec, zero)
                qcat_ref[rs, (2 * d + par) * DK:(2 * d + par + 1) * DK] = jnp.where(sel, qd, zero)
                last = (par + 1) * L - 1
                dec_ref[d, 2 * i + par] = jnp.broadcast_to(jnp.exp2(p_d[last:last + 1, :]), (8, DK))

    G = GLA_BLOCKS_PER_STEP
    n_groups = nb // G

    def intra_group(i):
        blks = [G * i + j for j in range(G)]
        rss = [pl.ds(pl.multiple_of(b * BL, BL), BL) for b in blks]
        vs = [v_ref[0, rs, :] for rs in rss]
        atts = [[_nt(qd_ref[d, rs, :], kd_ref[d, rs, :]) for d in range(2)] for rs in rss]
        for j, b in enumerate(blks):
            for d in range(2):
                x_ref[d, b] = _tn(vs[j], kdp_ref[d, rss[j], :])
        for j in range(G):
            att = jnp.where(masks[0], atts[j][0], 0.0) + jnp.where(masks[1], atts[j][1], 0.0)
            acc_ref[rss[j], :] = _dot(att.astype(BF16), vs[j])

    def fused_body(i, carry):
        sums = [decay_sums(G * (i + 1) + j) for j in range(G)]
        intra_group(i)
        for j in range(G):
            decay_block(G * (i + 1) + j, sums[j])
        return carry

    for j in range(G):
        decay_block(j, decay_sums(j))
    lax.fori_loop(0, n_groups - 1, fused_body, 0, unroll=True)
    intra_group(n_groups - 1)

    def state_body(d):
        def body(i, s):
            b = i if d == 0 else nb - 1 - i
            for par in ((0, 1) if d == 0 else (1, 0)):
                lane0 = (2 * d + par) * DK
                sst_ref[b, :, lane0:lane0 + DK] = s.astype(BF16)
                s = s * dec_ref[d, 2 * b + par, 0:1, :] + x_ref[d, b, :, par * DK:(par + 1) * DK]
            return s
        return body

    for d in range(2):
        lax.fori_loop(0, nb, state_body(d), jnp.zeros((GLA_DV, DK), F32), unroll=True)

    def out_body(b, carry):
        rs = pl.ds(pl.multiple_of(b * BL, BL), BL)
        o = acc_ref[rs, :] + _nt(qcat_ref[rs, :], sst_ref[b])
        ms = jnp.mean(o * o, axis=-1, keepdims=True)
        o = o * lax.rsqrt(ms + EPS) * nw_ref[...]
        o_ref[0, rs, :] = (o * _silu(gg_ref[0, rs, :].astype(F32))).astype(BF16)
        return carry

    lax.fori_loop(0, nb, out_body, 0, unroll=GLA_OUT_UNROLL)


def _gla_heads_kernel(q_ref, k_ref, v_ref, gg_ref, us_ref, a2_ref, bias_ref, nw_ref, o_ref, *scratch):
    DK, DV = GLA_DK, GLA_DV
    for hh in range(GLA_HEADS_PER_STEP):
        ks, vs, gs = slice(hh * DK, (hh + 1) * DK), slice(hh * DV, (hh + 1) * DV), slice(hh * 2 * DK, (hh + 1) * 2 * DK)
        _gla_kernel(q_ref.at[:, :, ks], k_ref.at[:, :, ks], v_ref.at[:, :, vs], gg_ref.at[:, :, vs], us_ref,
                    a2_ref.at[:, gs], bias_ref.at[:, gs], nw_ref, o_ref.at[:, :, vs], *scratch)


def _gla(u3, us3, a2, bias, nw):
    B, S, _ = u3.shape
    HPS = GLA_HEADS_PER_STEP
    DK, DV = GLA_DK, GLA_DV
    nc = S // GLA_CHUNK
    nb = nc // 2
    return pl.pallas_call(
        _gla_heads_kernel,
        grid=(B, GLA_HEADS // HPS),
        in_specs=[
            pl.BlockSpec((1, S, HPS * DK), lambda b, h: (b, 0, U_GQ // (HPS * DK) + h)),
            pl.BlockSpec((1, S, HPS * DK), lambda b, h: (b, 0, U_GK // (HPS * DK) + h)),
            pl.BlockSpec((1, S, HPS * DV), lambda b, h: (b, 0, U_GV // (HPS * DV) + h)),
            pl.BlockSpec((1, S, HPS * DV), lambda b, h: (b, 0, U_GG // (HPS * DV) + h)),
            pl.BlockSpec((1, S, LANE), lambda b, h: (b, 0, 0)),
            pl.BlockSpec((LANE, HPS * 2 * DK), lambda b, h: (0, h)),
            pl.BlockSpec((1, HPS * 2 * DK), lambda b, h: (0, h)),
            pl.BlockSpec((1, DV), lambda b, h: (0, 0)),
        ],
        out_specs=pl.BlockSpec((1, S, HPS * DV), lambda b, h: (b, 0, h)),
        out_shape=jax.ShapeDtypeStruct((B, S, GLA_VAL_W), BF16),
        scratch_shapes=[
            pltpu.VMEM((S, DV), F32),
            pltpu.VMEM((S, 2 * DK), F32),
            pltpu.VMEM((2, S, DK), BF16),
            pltpu.VMEM((2, S, DK), BF16),
            pltpu.VMEM((2, S, 2 * DK), BF16),
            pltpu.VMEM((S, 4 * DK), BF16),
            pltpu.VMEM((2, nb, DV, 2 * DK), F32),
            pltpu.VMEM((2, nc, 8, DK), F32),
            pltpu.VMEM((nb, DV, 4 * DK), BF16),
        ],
        compiler_params=_cparams(("parallel", "parallel")),
        name="gla_scan",
    )(u3, u3, u3, u3, us3, a2, bias, nw)


def _na_bias_rows(rpb):
    H, R, C = rpb.shape
    n_pos = GRID_W - NA_WIN_W
    n_neg = GRID_W - NA_WIN_W + 2
    ext = jnp.concatenate([rpb[:, :, NA_WIN_W - 1:], jnp.repeat(rpb[:, :, C - 1:], n_pos, axis=2),
                           jnp.repeat(rpb[:, :, 0:1], n_neg, axis=2), rpb[:, :, 1:NA_WIN_W - 1]], axis=2)
    return jnp.pad(ext, ((0, 0), (0, 1), (0, 0))).reshape(H // 2, 2, R + 1, 2 * GRID_W)


def _na_kernel(flag_ref, q_ref, k_ref, v_ref, ext_ref, o_ref, vx_ref, tab_ref):
    NB, S, _ = q_ref.shape
    rows = S // GRID_W
    total = NB * rows
    win_h = NA_WIN_H
    nk = win_h * GRID_W

    vx_ref[:, :, 0:LANE] = v_ref[...]
    vx_ref[:, :, LANE:2 * LANE] = jnp.ones((NB, S, LANE), BF16)
    bound_ok = flag_ref[0] != 0

    first_q = _iota2((GRID_W, LANE), 1) < NA_HEAD_DIM

    @pl.when(pl.program_id(1) == 0)
    def _():
        q_col = _iota2((GRID_W, 2 * GRID_W), 0)
        k_col = _iota2((GRID_W, 2 * GRID_W), 1) % GRID_W
        w_start = jnp.clip(q_col - NA_WIN_W // 2, 0, GRID_W - NA_WIN_W)
        in_window = (k_col >= w_start) & (k_col < w_start + NA_WIN_W)
        low = _iota2((GRID_W, 2 * GRID_W), 1) < GRID_W
        for hd in range(2):
            def skewed(rr, shift):
                row = jnp.broadcast_to(ext_ref[0, hd, rr:rr + 1, :], (GRID_W, 2 * GRID_W))
                return pltpu.roll(row, shift, axis=1, stride=1, stride_axis=0)

            for e in range(2 * NA_WIN_H - 2):
                t = jnp.where(low, skewed(e, 0), skewed(e + 1, GRID_W))
                tab_ref[e, hd * GRID_W:(hd + 1) * GRID_W, :] = jnp.where(in_window, t, -jnp.inf)

    def locate(r):
        bi = r // rows
        rl = r - bi * rows
        r0 = jnp.clip(rl - win_h // 2, 0, rows - win_h)
        return bi, pl.multiple_of(rl * GRID_W, GRID_W), rl - r0, pl.multiple_of(r0 * GRID_W, GRID_W)

    def scores(r):
        bi, q0, delta, k0 = locate(r)
        q = q_ref[bi, pl.ds(q0, GRID_W), :]
        zero = jnp.zeros_like(q)
        qs = jnp.concatenate([jnp.where(first_q, q, zero), jnp.where(first_q, zero, q)], axis=0)
        rr0 = (win_h - 1) - delta
        bias = jnp.concatenate([tab_ref[rr0 + w] for w in range(0, win_h, 2)], axis=1)
        return _nt(qs, k_ref[bi, pl.ds(k0, nk), :]) + bias

    def probs_exact(s):
        return jnp.exp2(s - jnp.max(s, axis=-1, keepdims=True)).astype(BF16)

    def probs_bounded(r):
        return jnp.exp2(scores(r)).astype(BF16)

    def attend(r, p):
        bi, q0, _, k0 = locate(r)
        ox = _dot(p, vx_ref[bi, pl.ds(k0, nk), :])
        o = ox[:, 0:LANE] / ox[:, LANE:2 * LANE]
        o_ref[bi, pl.ds(q0, GRID_W), :] = jnp.where(first_q, o[:GRID_W], o[GRID_W:]).astype(BF16)

    @pl.when(bound_ok)
    def _():
        U = NA_ROWS_PER_STEP

        def row_body(i, p_prev):
            r = i * U
            for j in range(U):
                attend(r - U + j, p_prev[j])
            return tuple(probs_bounded(r + j) for j in range(U))

        p_last = lax.fori_loop(1, total // U, row_body, tuple(probs_bounded(j) for j in range(U)), unroll=True)
        for j in range(U):
            attend(total - U + j, p_last[j])

    @pl.when(jnp.logical_not(bound_ok))
    def _():
        U = NA_ROWS_PER_STEP_EXACT

        def row_body(i, carry):
            s_cur, p_prev = carry
            r = i * U
            for j in range(U):
                attend(r - U + j, p_prev[j])
            p = tuple(probs_exact(s) for s in s_cur)
            s_next = tuple(scores(jnp.minimum(r + U + j, total - 1)) for j in range(U))
            return s_next, p

        p0 = tuple(probs_exact(scores(j)) for j in range(U))
        s1 = tuple(scores(U + j) for j in range(U))
        _, p_last = lax.fori_loop(1, total // U, row_body, (s1, p0))
        for j in range(U):
            attend(total - U + j, p_last[j])


def _na_score_bound(rpb, q_norm_w, k_norm_w):
    H = rpb.shape[0]
    qk = (NA_HEAD_DIM ** 0.5 * LOG2E * (1.0 + NA_BOUND_SLACK)) * jnp.max(jnp.abs(q_norm_w)) * jnp.max(jnp.abs(k_norm_w))
    b_max = jnp.max(rpb.reshape(H, -1), axis=1) * LOG2E
    b_self = rpb[:, NA_WIN_H - 1, NA_WIN_W - 1] * LOG2E
    bound = qk + b_max
    flag = jnp.all(bound - (b_self - qk) <= NA_MAX_BOUND_GAP)
    return bound, flag.astype(jnp.int32).reshape(1)


def _na(u3, bias_rows, flag):
    B, S, _ = u3.shape
    assert S // GRID_W >= NA_WIN_H and NA_WIN_H % 2 == 0 and 2 * GRID_W == LANE
    nb = math.gcd(B, NA_BATCH_PER_STEP)
    assert (nb * (S // GRID_W)) % NA_ROWS_PER_STEP == 0
    return pl.pallas_call(
        _na_kernel,
        grid=(NA_HEADS // 2, B // nb),
        in_specs=[
            pl.BlockSpec(memory_space=pltpu.SMEM),
            pl.BlockSpec((nb, S, LANE), lambda h, b: (b, 0, U_NQ // LANE + h)),
            pl.BlockSpec((nb, S, LANE), lambda h, b: (b, 0, U_NK // LANE + h)),
            pl.BlockSpec((nb, S, LANE), lambda h, b: (b, 0, U_NV // LANE + h)),
            pl.BlockSpec((1, 2, 2 * NA_WIN_H, 2 * GRID_W), lambda h, b: (h, 0, 0, 0)),
        ],
        out_specs=pl.BlockSpec((nb, S, LANE), lambda h, b: (b, 0, h)),
        out_shape=jax.ShapeDtypeStruct((B, S, NA_W), BF16),
        scratch_shapes=[
            pltpu.VMEM((nb, S, 2 * LANE), BF16),
            pltpu.VMEM((2 * NA_WIN_H - 2, 2 * GRID_W, 2 * GRID_W), F32),
        ],
        compiler_params=_cparams(("parallel", "arbitrary")),
        name="na_attn",
    )(flag, u3, u3, u3, bias_rows)


def _merge_kernel(x_ref, ys_ref, yg_ref, yn_ref, gate_ref, ws_ref, wg_ref, wn_ref, wo_ref, o_ref):
    D = D_MODEL
    mixed = _sigmoid(gate_ref[:, 0:D].astype(F32)) * _dot(ys_ref[...], ws_ref[...])
    mixed += _sigmoid(gate_ref[:, D:2 * D].astype(F32)) * _dot(yg_ref[...], wg_ref[...])
    mixed += _sigmoid(gate_ref[:, 2 * D:3 * D].astype(F32)) * _dot(yn_ref[...], wn_ref[...])
    o_ref[...] = x_ref[...] + _dot(mixed.astype(BF16), wo_ref[...])


def _merge(x2, ys, yg, yn, u2, ws, wg, wn, wo, tm=512):
    T = x2.shape[0]
    D = D_MODEL
    row = lambda i: (i, 0)
    fixed = lambda i: (0, 0)
    return pl.pallas_call(
        _merge_kernel,
        grid=(T // tm,),
        in_specs=[
            pl.BlockSpec((tm, D), row),
            pl.BlockSpec((tm, D), row),
            pl.BlockSpec((tm, D), row),
            pl.BlockSpec((tm, D), row),
            pl.BlockSpec((tm, N_BRANCH * D), lambda i: (i, U_GATE // (N_BRANCH * D))),
            pl.BlockSpec((D, D), fixed),
            pl.BlockSpec((D, D), fixed),
            pl.BlockSpec((D, D), fixed),
            pl.BlockSpec((D, D), fixed),
        ],
        out_specs=pl.BlockSpec((tm, D), row),
        out_shape=jax.ShapeDtypeStruct((T, D), F32),
        compiler_params=_cparams(("parallel",)),
        name="merge",
    )(x2, ys, yg, yn, u2, ws, wg, wn, wo)


def _mlp_kernel(x_ref, nw_ref, w1_ref, w2_ref, o_ref, *, tf):
    x = x_ref[...]
    ms = jnp.mean(x * x, axis=-1, keepdims=True)
    h = (x * lax.rsqrt(ms + EPS) * nw_ref[...]).astype(BF16)
    acc = x
    for f in range(D_FF // tf):
        a = jnp.maximum(_dot(h, w1_ref[:, f * tf:(f + 1) * tf]), 0.0)
        acc = acc + _dot((a * a).astype(BF16), w2_ref[f * tf:(f + 1) * tf, :])
    o_ref[...] = acc


def _mlp(x2, nw, w1, w2, tm=1024, tf=1024):
    T = x2.shape[0]
    D = D_MODEL
    resident = pl.Buffered(1)
    return pl.pallas_call(
        functools.partial(_mlp_kernel, tf=tf),
        grid=(T // tm,),
        in_specs=[
            pl.BlockSpec((tm, D), lambda i: (i, 0)),
            pl.BlockSpec((1, D), lambda i: (0, 0)),
            pl.BlockSpec((D, D_FF), lambda i: (0, 0), pipeline_mode=resident),
            pl.BlockSpec((D_FF, D), lambda i: (0, 0), pipeline_mode=resident),
        ],
        out_specs=pl.BlockSpec((tm, D), lambda i: (i, 0)),
        out_shape=jax.ShapeDtypeStruct((T, D), F32),
        compiler_params=_cparams(("parallel",)),
        name="mlp",
    )(x2, nw, w1, w2)


def _pad_rows(w, start, total):
    return jnp.zeros((total, w.shape[1]), w.dtype).at[start:start + w.shape[0]].set(w)


def kernel(x, norm_mix_w, w_in, ssd_conv_w, ssd_conv_b, ssd_dt_bias_f, ssd_dt_bias_b, ssd_a_log_f,
           ssd_a_log_b, ssd_d, ssd_norm_w, gla_a2_f, gla_a2_bias_f, gla_a2_b, gla_a2_bias_b,
           gla_norm_w, na_q_norm_w, na_k_norm_w, na_rpb, w_branch_ssd, w_branch_gla, w_branch_na,
           w_out, norm_mlp_w, w_ff1, w_ff2):
    B, S, D = x.shape
    T = B * S
    depth = w_in.shape[0]
    x2 = x.reshape(T, D)
    w_in_t = jnp.swapaxes(w_in, 1, 2)
    for l in range(depth):
        w_big, w_small = _permute_weight(w_in_t, l)
        hg = (SSD_GROUPS, SSD_HG)
        zeros_r = jnp.zeros((SSD_GROUPS, LANE - 2 * SSD_HG), F32)
        dt_bias = jnp.concatenate([ssd_dt_bias_f[l].reshape(hg), ssd_dt_bias_b[l].reshape(hg), zeros_r], axis=1)
        a_neg = jnp.concatenate([-jnp.exp(ssd_a_log_f[l]).reshape(hg), -jnp.exp(ssd_a_log_b[l]).reshape(hg),
                                 zeros_r], axis=1)
        prow3 = jnp.concatenate([dt_bias[:, None], a_neg[:, None], jnp.zeros((SSD_GROUPS, 6, LANE), F32)], axis=1)
        prow = prow3.reshape(SSD_GROUPS * 8, LANE)
        pcol = jnp.transpose(prow3[:, :, :SMALL_T_ROWS], (0, 2, 1))
        drow = jnp.repeat(ssd_d[l], SSD_HEAD_DIM)[None, :]
        hk = (GLA_HEADS, 1, GLA_DK)
        a2 = jnp.concatenate([_pad_rows(gla_a2_f[l], SM_GAF, LANE).reshape((LANE,) + hk),
                              _pad_rows(gla_a2_b[l], SM_GAB, LANE).reshape((LANE,) + hk)],
                             axis=2).reshape(LANE, -1).astype(BF16)
        a2_bias = jnp.concatenate([gla_a2_bias_f[l].reshape(hk), gla_a2_bias_b[l].reshape(hk)],
                                  axis=1).reshape(1, -1)
        na_bound, na_flag = _na_score_bound(na_rpb[l], na_q_norm_w[l], na_k_norm_w[l])
        table = _na_bias_rows(na_rpb[l] * LOG2E - na_bound[:, None, None])
        q_row = jnp.tile(na_q_norm_w[l] * (NA_HEAD_DIM ** -0.5 * LOG2E), NA_HEADS)
        k_row = jnp.tile(na_k_norm_w[l], NA_HEADS)
        qkw = jnp.concatenate([q_row[None], k_row[None], jnp.zeros((6, NA_W), F32)], axis=0)

        u2, us2, ust = _inproj(x2, norm_mix_w[l][None, :], w_big, w_small, qkw)
        u3 = u2.reshape(B, S, U_WIDTH)
        us3 = us2.reshape(B, S, SMALL_W)
        xbc = _conv(u3, ssd_conv_w[l], ssd_conv_b[l][None, :])
        y_ssd = _ssd(xbc, u3, ust, prow, pcol, drow, ssd_norm_w[l][None, :])
        y_gla = _gla(u3, us3, a2, a2_bias, gla_norm_w[l][None, :])
        y_na = _na(u3, table, na_flag)
        x2 = _merge(x2, y_ssd.reshape(T, -1), y_gla.reshape(T, -1), y_na.reshape(T, -1), u2,
                    w_branch_ssd[l].astype(BF16), w_branch_gla[l].astype(BF16),
                    w_branch_na[l].astype(BF16), w_out[l].astype(BF16))
        x2 = _mlp(x2, norm_mlp_w[l][None, :], w_ff1[l].astype(BF16), w_ff2[l].astype(BF16))
    return x2.reshape(B, S, D)
```

```python
import functools
import math

import jax
import jax.numpy as jnp
import numpy as np
from jax import lax
from jax.experimental import pallas as pl
from jax.experimental.pallas import tpu as pltpu

F32 = jnp.float32
BF16 = jnp.bfloat16

EPS = 1e-6
D_MODEL = 1024
GRID_W = 64

SSD_HEADS = 16
SSD_HEAD_DIM = 64
SSD_D_INNER = 1024
SSD_GROUPS = 2
SSD_STATE = 128
SSD_CONV = 5
SSD_CONV_DIM = 1536
SSD_CHUNK = 128
CONV_IN_BLOCKS = 3
CONV_TILE = 256
CONV_ROWS = 128
SSD_CHUNKS_PER_STEP = 2
SSD_OUT_UNROLL = 16
SSD_GROUP_W = SSD_D_INNER // SSD_GROUPS
SSD_HG = SSD_HEADS // SSD_GROUPS

GLA_HEADS = 4
GLA_DK = 128
GLA_DV = 256
GLA_KEY_W = 512
GLA_VAL_W = 1024
GLA_GATE_RANK = 16
GLA_GATE_NORM = 16.0
GLA_CHUNK = 64
GLA_BLOCKS_PER_STEP = 2
GLA_HEADS_PER_STEP = 2
GLA_OUT_UNROLL = 16

NA_HEADS = 16
NA_HEAD_DIM = 64
NA_W = 1024
NA_WIN_H = 8
NA_WIN_W = 16
NA_BATCH_PER_STEP = 4
NA_ROWS_PER_STEP = 32
NA_ROWS_PER_STEP_EXACT = 2
LOG2E = 1.4426950408889634
NA_BOUND_SLACK = 0.02
NA_MAX_BOUND_GAP = 90.0

N_BRANCH = 3
D_FF = 4096

IN_SIZES = (SSD_D_INNER, SSD_CONV_DIM, SSD_HEADS, SSD_HEADS,
            GLA_KEY_W, GLA_KEY_W, GLA_VAL_W, GLA_VAL_W, GLA_GATE_RANK, GLA_GATE_RANK,
            NA_W, NA_W, NA_W, N_BRANCH * D_MODEL)
_IN_OFF = np.concatenate([[0], np.cumsum(IN_SIZES)])
(_O_Z, _O_XBC, _O_DTF, _O_DTB, _O_GQ, _O_GK, _O_GV, _O_GG, _O_GAF, _O_GAB,
 _O_NQ, _O_NK, _O_NV, _O_GATE) = [int(v) for v in _IN_OFF[:-1]]

U_NQ = 0
U_NK = 1024
U_NV = 2048
U_GATE = 3072
U_Z = 6144
U_XBC = 7168
U_GQ = 8704
U_GK = 9216
U_GV = 9728
U_GG = 10752
U_WIDTH = 11776
INPROJ_COL_CHUNK = 1024
LANE = 128
VMEM_LIMIT = 56 * 1024 * 1024

SMALL_W = SSD_GROUPS * LANE
SM_DTF, SM_DTB, SM_GAF, SM_GAB = 0, 8, 16, 32
SMALL_T_ROWS = 2 * SSD_HG

_BIG_SEGS = ((_O_NQ, 6144), (_O_Z, 2560), (_O_GQ, 3072))


def _cparams(sem, vmem=VMEM_LIMIT):
    return pltpu.CompilerParams(dimension_semantics=sem, vmem_limit_bytes=vmem)


def _sigmoid(x):
    return 1.0 / (1.0 + jnp.exp2(x * (-LOG2E)))


def _silu(x):
    return x * _sigmoid(x)


def _softplus(x):
    return jnp.maximum(x, 0.0) + jnp.log1p(jnp.exp(-jnp.abs(x)))


def _neg_log2_sigmoid(x):
    t = x * (-LOG2E)
    return jnp.maximum(t, 0.0) + jnp.log2(1.0 + jnp.exp2(-jnp.abs(t)))


def _nt(a, b):
    return lax.dot_general(a, b, (((1,), (1,)), ((), ())), preferred_element_type=F32)


def _tn(a, b):
    return lax.dot_general(a, b, (((0,), (0,)), ((), ())), preferred_element_type=F32)


def _dot(a, b):
    return jnp.dot(a, b, preferred_element_type=F32)


def _iota2(shape, dim):
    return lax.broadcasted_iota(jnp.int32, shape, dim)


def _wperm_kernel(wt_ref, o_ref, ws_ref):
    off = 0
    for a, n in _BIG_SEGS:
        o_ref[off:off + n, :] = wt_ref[0, a:a + n, :].astype(BF16)
        off += n
    ws_ref[...] = jnp.zeros_like(ws_ref)
    for g in range(SSD_GROUPS):
        base, h0 = g * LANE, g * SSD_HG
        ws_ref[base + SM_DTF:base + SM_DTF + SSD_HG, :] = wt_ref[0, _O_DTF + h0:_O_DTF + h0 + SSD_HG, :]
        ws_ref[base + SM_DTB:base + SM_DTB + SSD_HG, :] = wt_ref[0, _O_DTB + h0:_O_DTB + h0 + SSD_HG, :]
    ws_ref[SM_GAF:SM_GAF + GLA_GATE_RANK, :] = wt_ref[0, _O_GAF:_O_GAF + GLA_GATE_RANK, :]
    ws_ref[SM_GAB:SM_GAB + GLA_GATE_RANK, :] = wt_ref[0, _O_GAB:_O_GAB + GLA_GATE_RANK, :]


def _permute_weight(w_t_all, layer, tc=256):
    _, n_in, d = w_t_all.shape
    return pl.pallas_call(
        _wperm_kernel,
        grid=(d // tc,),
        in_specs=[pl.BlockSpec((1, n_in, tc), lambda i: (layer, 0, i))],
        out_specs=[
            pl.BlockSpec((U_WIDTH, tc), lambda i: (0, i)),
            pl.BlockSpec((SMALL_W, tc), lambda i: (0, i)),
        ],
        out_shape=[
            jax.ShapeDtypeStruct((U_WIDTH, d), BF16),
            jax.ShapeDtypeStruct((SMALL_W, d), F32),
        ],
        compiler_params=_cparams(("parallel",)),
        name="w_permute",
    )(w_t_all)


def _head_rms(r, w_row):
    G = 2 * LANE
    er = _iota2((G, G), 0) // NA_HEAD_DIM
    ec = _iota2((G, G), 1) // NA_HEAD_DIM
    e_blk = jnp.where(er == ec, 1.0 / NA_HEAD_DIM, 0.0).astype(BF16)
    outs = []
    for a in range(0, r.shape[1], G):
        x = r[:, a:a + G]
        ms = _dot((x * x).astype(BF16), e_blk)
        outs.append(x * lax.rsqrt(ms + EPS))
    return jnp.concatenate(outs, axis=1) * w_row


def _inproj_kernel(x_ref, nw_ref, w_ref, ws_ref, qkw_ref, u_ref, us_ref, ust_ref, h_ref):
    tn = u_ref.shape[1]
    chunks = [(a, min(a + INPROJ_COL_CHUNK, tn)) for a in range(0, tn, INPROJ_COL_CHUNK)]

    @pl.when(pl.program_id(1) == 0)
    def _():
        x = x_ref[...]
        ms = jnp.mean(x * x, axis=-1, keepdims=True)
        h = (x * lax.rsqrt(ms + EPS) * nw_ref[...]).astype(BF16)
        h_ref[...] = h
        ws = ws_ref[...].astype(BF16)
        us_ref[...] = _nt(h, ws)
        dt_rows = jnp.concatenate([ws[g * LANE:g * LANE + SMALL_T_ROWS] for g in range(SSD_GROUPS)], axis=0)
        ust_ref[...] = _nt(dt_rows, h)
        for c, (a, b) in enumerate(chunks):
            r = _nt(h, w_ref[a:b, :])
            if c < 2:
                r = _head_rms(r, qkw_ref[c:c + 1, :])
            u_ref[:, a:b] = r.astype(BF16)

    @pl.when(pl.program_id(1) != 0)
    def _():
        for a, b in chunks:
            u_ref[:, a:b] = _nt(h_ref[...], w_ref[a:b, :]).astype(BF16)


def _inproj(x2, nw, w_big, w_small, qkw, tm=1024, tn=U_WIDTH // 4):
    assert (U_NQ, U_NK) == (0, INPROJ_COL_CHUNK) and NA_W == INPROJ_COL_CHUNK
    T = x2.shape[0]
    return pl.pallas_call(
        _inproj_kernel,
        grid=(T // tm, U_WIDTH // tn),
        in_specs=[
            pl.BlockSpec((tm, D_MODEL), lambda i, j: (i, 0)),
            pl.BlockSpec((1, D_MODEL), lambda i, j: (0, 0)),
            pl.BlockSpec((tn, D_MODEL), lambda i, j: (j, 0)),
            pl.BlockSpec((SMALL_W, D_MODEL), lambda i, j: (0, 0)),
            pl.BlockSpec((8, NA_W), lambda i, j: (0, 0)),
        ],
        out_specs=[
            pl.BlockSpec((tm, tn), lambda i, j: (i, j)),
            pl.BlockSpec((tm, SMALL_W), lambda i, j: (i, 0)),
            pl.BlockSpec((SSD_GROUPS * SMALL_T_ROWS, tm), lambda i, j: (0, i)),
        ],
        out_shape=[
            jax.ShapeDtypeStruct((T, U_WIDTH), BF16),
            jax.ShapeDtypeStruct((T, SMALL_W), F32),
            jax.ShapeDtypeStruct((SSD_GROUPS * SMALL_T_ROWS, T), F32),
        ],
        scratch_shapes=[pltpu.VMEM((tm, D_MODEL), BF16)],
        compiler_params=_cparams(("parallel", "arbitrary")),
        name="inproj",
    )(x2, nw, w_big, w_small, qkw)


def _conv_kernel(*refs):
    u_refs, (w_ref, b_ref, o_ref, xp_ref) = refs[:CONV_IN_BLOCKS], refs[CONV_IN_BLOCKS:]
    S, wb = u_refs[0].shape[1], u_refs[0].shape[2]
    C = o_ref.shape[2]
    R = CONV_ROWS
    tc = CONV_TILE
    pad = SSD_CONV // 2
    xp_ref[0:R, :] = jnp.zeros((R, C), BF16)
    xp_ref[R + S:R + S + R, :] = jnp.zeros((R, C), BF16)
    for i, u_ref in enumerate(u_refs):
        xp_ref[R:R + S, i * wb:(i + 1) * wb] = u_ref[0]
    t_i = _iota2((R, 2 * R), 0)
    j_i = _iota2((R, 2 * R), 1)
    side = [k for k in range(SSD_CONV) if k != pad]
    shifts = jnp.concatenate([(j_i == t_i + R // 2 + (k - pad)).astype(BF16) for k in side], axis=0)
    for blk in range(S // R):
        for c0 in range(0, C, tc):
            cs = slice(c0, c0 + tc)
            win = xp_ref[blk * R + R // 2:blk * R + R // 2 + 2 * R, cs]
            sh = _dot(shifts, win)
            acc = b_ref[:, cs] + w_ref[pad:pad + 1, cs] * xp_ref[R + blk * R:R + (blk + 1) * R, cs].astype(F32)
            for n, k in enumerate(side):
                acc = acc + w_ref[k:k + 1, cs] * sh[n * R:(n + 1) * R]
            o_ref[0, blk * R:(blk + 1) * R, cs] = _silu(acc).astype(BF16)


def _conv(u3, conv_w, conv_b):
    B, S, _ = u3.shape
    wb = SSD_CONV_DIM // CONV_IN_BLOCKS
    assert U_XBC % wb == 0 and wb % CONV_TILE == 0
    off = U_XBC // wb
    in_blocks = [pl.BlockSpec((1, S, wb), functools.partial(lambda b, i: (b, 0, off + i), i=i))
                 for i in range(CONV_IN_BLOCKS)]
    return pl.pallas_call(
        _conv_kernel,
        grid=(B,),
        in_specs=in_blocks + [
            pl.BlockSpec((SSD_CONV, SSD_CONV_DIM), lambda b: (0, 0)),
            pl.BlockSpec((1, SSD_CONV_DIM), lambda b: (0, 0)),
        ],
        out_specs=pl.BlockSpec((1, S, SSD_CONV_DIM), lambda b: (b, 0, 0)),
        out_shape=jax.ShapeDtypeStruct((B, S, SSD_CONV_DIM), BF16),
        scratch_shapes=[pltpu.VMEM((S + 2 * CONV_ROWS, SSD_CONV_DIM), BF16)],
        compiler_params=_cparams(("parallel",)),
        name="ssd_conv",
    )(*([u3] * CONV_IN_BLOCKS), conv_w, conv_b)


def _split_hi_lo(v):
    hi = v.astype(BF16)
    lo = (v - hi.astype(F32)).astype(BF16)
    return jnp.concatenate([hi, lo], axis=1)


def _split3(v, axis):
    hi = v.astype(BF16)
    r1 = v - hi.astype(F32)
    mid = r1.astype(BF16)
    lo = (r1 - mid.astype(F32)).astype(BF16)
    return jnp.concatenate([hi, mid, lo], axis=axis)


def _ssd_kernel(x_ref, b_ref, c_ref, z_ref, ust_ref, prow_ref, pcol_ref, drow_ref, nw_ref,
                o_ref, acc_ref, cumc_ref, cumr_ref, wdt_ref, dec_ref, xs_ref, decx_ref, sst_ref, st_ref):
    L = SSD_CHUNK
    S = x_ref.shape[1]
    nc = S // L
    W = SSD_GROUP_W
    HG = SSD_HG
    R = 2 * HG

    ii = _iota2((L, L), 0)
    jj = _iota2((L, L), 1)
    tril = ii >= jj
    triu = jj >= ii
    tril_b = tril.astype(BF16)
    triu_b = triu.astype(BF16)

    er = _iota2((LANE, 2 * W), 0)
    ec = _iota2((LANE, 2 * W), 1)
    e = (er == jnp.where(ec < W, ec // SSD_HEAD_DIM, (ec - W) // SSD_HEAD_DIM + HG)).astype(BF16)
    e2 = jnp.concatenate([e, e], axis=0)

    a_row = prow_ref[1:2, :]
    bias_col = pcol_ref[0, :, 0:1]
    a_col = pcol_ref[0, :, 1:2]

    lane_fwd = _iota2((L, LANE), 1) < HG
    row_fwd = _iota2((R, L), 0) < HG
    lane_half = _iota2((L, LANE), 1) < SSD_HEAD_DIM

    def decay_sums(c):
        rs = pl.ds(pl.multiple_of(c * L, L), L)
        dt_r = _softplus(ust_ref[:, rs] + bias_col)
        a_r = dt_r * a_col
        dt_c = jnp.concatenate([dt_r, jnp.zeros((L - R, L), F32)], axis=0).T
        a_c = dt_c * a_row
        return dt_c, a_c, _dot(tril_b, _split3(a_c, 1)), dt_r, a_r, _dot(_split3(a_r, 0), triu_b)

    def decay_store(c, sums):
        rs = pl.ds(pl.multiple_of(c * L, L), L)
        dt_c, a_c, pp, dt_r, a_r, pr = sums
        p_c = pp[:, 0:LANE] + pp[:, LANE:2 * LANE] + pp[:, 2 * LANE:3 * LANE]
        tot_c = p_c[L - 1:L, :]
        cum_c = jnp.where(lane_fwd, p_c, tot_c - p_c + a_c)
        cumc_ref[rs, :] = cum_c * LOG2E
        wdt_ref[rs, :] = jnp.exp(tot_c - cum_c) * dt_c
        dec_ref[c] = jnp.broadcast_to(jnp.exp(tot_c), (16, LANE))
        p_r = pr[0:R] + pr[R:2 * R] + pr[2 * R:3 * R]
        tot_r = p_r[:, L - 1:L]
        cum_r = jnp.where(row_fwd, p_r, tot_r - p_r + a_r)
        cumr_ref[:, rs] = (cum_r - jnp.log(dt_r)) * LOG2E

    def intra_chunk(c):
        rs = pl.ds(pl.multiple_of(c * L, L), L)
        x_c = x_ref[0, rs, :]
        b_c = b_ref[0, rs, :]
        cb = _nt(c_ref[0, rs, :], b_c)
        ex = _dot(_split_hi_lo(jnp.concatenate([wdt_ref[rs, :], dec_ref[c]], axis=0)), e2)
        decx_ref[c] = ex[L:L + 8]
        xf = x_c.astype(F32)
        xw = jnp.concatenate([xf * ex[0:L, 0:W], xf * ex[0:L, W:2 * W]], axis=1).astype(BF16)
        xs_ref[c] = _tn(b_c, xw)
        cum_c = cumc_ref[rs, :]
        cum_r = cumr_ref[:, rs]
        for hp in range(HG // 2):
            ms = []
            for hh in range(2):
                hf = 2 * hp + hh
                hb = HG + 2 * hp + hh
                decf = jnp.exp2(jnp.where(tril, cum_c[:, hf:hf + 1] - cum_r[hf:hf + 1, :], -jnp.inf))
                decb = jnp.exp2(jnp.where(triu, cum_c[:, hb:hb + 1] - cum_r[hb:hb + 1, :], -jnp.inf))
                ms.append((cb * (decf + decb)).astype(BF16))
            m2 = jnp.concatenate(ms, axis=1)
            xp = x_c[:, hp * LANE:(hp + 1) * LANE]
            xz = jnp.zeros_like(xp)
            x2 = jnp.concatenate([jnp.where(lane_half, xp, xz), jnp.where(lane_half, xz, xp)], axis=0)
            acc_ref[rs, hp * LANE:(hp + 1) * LANE] = _dot(m2, x2)

    G = SSD_CHUNKS_PER_STEP
    n_steps = nc // G

    def fused_body(i, carry):
        sums = [decay_sums(G * (i + 1) + j) for j in range(G)]
        for j in range(G):
            intra_chunk(G * i + j)
        for j in range(G):
            decay_store(G * (i + 1) + j, sums[j])
        return carry

    for j in range(G):
        decay_store(j, decay_sums(j))
    lax.fori_loop(0, n_steps - 1, fused_body, 0, unroll=True)
    for j in range(G):
        intra_chunk(G * (n_steps - 1) + j)

    st_ref[...] = jnp.zeros_like(st_ref)

    def state_body(i, carry):
        for d, c in enumerate((i, nc - 1 - i)):
            ls = slice(d * W, (d + 1) * W)
            s_prev = st_ref[:, ls]
            sst_ref[c, :, ls] = s_prev.astype(BF16)
            st_ref[:, ls] = s_prev * decx_ref[c, 0:1, ls] + xs_ref[c, :, ls]
        return carry

    lax.fori_loop(0, nc, state_body, 0, unroll=True)

    def out_body(c, carry):
        rs = pl.ds(pl.multiple_of(c * L, L), L)
        c_c = c_ref[0, rs, :]
        ex = _split_hi_lo(jnp.exp2(cumc_ref[rs, :]))
        y = acc_ref[rs, :] + x_ref[0, rs, :].astype(F32) * drow_ref[...]
        for d in range(2):
            ls = slice(d * W, (d + 1) * W)
            y = y + _dot(c_c, sst_ref[c, :, ls]) * _dot(ex, e2[:, ls])
        y = y * _silu(z_ref[0, rs, :].astype(F32))
        ms = jnp.mean(y * y, axis=-1, keepdims=True)
        o_ref[0, rs, :] = (y * lax.rsqrt(ms + EPS) * nw_ref[...]).astype(BF16)
        return carry

    lax.fori_loop(0, nc, out_body, 0, unroll=SSD_OUT_UNROLL)


def _ssd_groups_kernel(x_ref, b_ref, c_ref, z_ref, ust_ref, prow_ref, pcol_ref, drow_ref, nw_ref, o_ref, *scratch):
    W, N, R = SSD_GROUP_W, SSD_STATE, SMALL_T_ROWS
    for g in range(SSD_GROUPS):
        ws, ns = slice(g * W, (g + 1) * W), slice(g * N, (g + 1) * N)
        _ssd_kernel(x_ref.at[:, :, ws], b_ref.at[:, :, ns], c_ref.at[:, :, ns], z_ref.at[:, :, ws],
                    ust_ref.at[g * R:(g + 1) * R, :], prow_ref.at[g * 8:(g + 1) * 8, :], pcol_ref.at[g:g + 1],
                    drow_ref.at[:, ws], nw_ref.at[:, ws], o_ref.at[:, :, ws], *scratch)


def _ssd(xbc, u3, ust, prow, pcol, drow, nw):
    B, S, _ = xbc.shape
    G = SSD_GROUPS
    W = SSD_GROUP_W
    N = SSD_STATE
    nc = S // SSD_CHUNK
    return pl.pallas_call(
        _ssd_groups_kernel,
        grid=(B,),
        in_specs=[
            pl.BlockSpec((1, S, G * W), lambda b: (b, 0, 0)),
            pl.BlockSpec((1, S, G * N), lambda b: (b, 0, SSD_D_INNER // (G * N))),
            pl.BlockSpec((1, S, G * N), lambda b: (b, 0, SSD_D_INNER // (G * N) + 1)),
            pl.BlockSpec((1, S, G * W), lambda b: (b, 0, U_Z // (G * W))),
            pl.BlockSpec((G * SMALL_T_ROWS, S), lambda b: (0, b)),
            pl.BlockSpec((G * 8, LANE), lambda b: (0, 0)),
            pl.BlockSpec((G, SMALL_T_ROWS, 8), lambda b: (0, 0, 0)),
            pl.BlockSpec((1, G * W), lambda b: (0, 0)),
            pl.BlockSpec((1, G * W), lambda b: (0, 0)),
        ],
        out_specs=pl.BlockSpec((1, S, G * W), lambda b: (b, 0, 0)),
        out_shape=jax.ShapeDtypeStruct((B, S, SSD_D_INNER), BF16),
        scratch_shapes=[
            pltpu.VMEM((S, W), F32),
            pltpu.VMEM((S, LANE), F32),
            pltpu.VMEM((SMALL_T_ROWS, S), F32),
            pltpu.VMEM((S, LANE), F32),
            pltpu.VMEM((nc, 16, LANE), F32),
            pltpu.VMEM((nc, N, 2 * W), F32),
            pltpu.VMEM((nc, 8, 2 * W), F32),
            pltpu.VMEM((nc, N, 2 * W), BF16),
            pltpu.VMEM((N, 2 * W), F32),
        ],
        compiler_params=_cparams(("parallel",)),
        name="ssd_scan",
    )(xbc, xbc, xbc, u3, ust, prow, pcol, drow, nw)


def _gla_kernel(q_ref, k_ref, v_ref, gg_ref, us_ref, a2_ref, bias_ref, nw_ref,
                o_ref, acc_ref, g_ref, qd_ref, kd_ref, kdp_ref, qcat_ref, x_ref, dec_ref, sst_ref):
    L = GLA_CHUNK
    BL = 2 * L
    DK = GLA_DK
    S = q_ref.shape[1]
    nb = S // BL
    scale = DK ** -0.5

    ii = _iota2((BL, BL), 0)
    jj = _iota2((BL, BL), 1)
    same = (ii // L) == (jj // L)
    masks = (same & (ii >= jj), same & (jj >= ii))
    tri2 = masks[0].astype(BF16)
    par_row = _iota2((BL, DK), 0) // L

    ga = us_ref[0].astype(BF16)
    g_ref[...] = _neg_log2_sigmoid(_dot(ga, a2_ref[...]) + bias_ref[...]) * (-1.0 / GLA_GATE_NORM)

    def decay_sums(i):
        g = g_ref[pl.ds(pl.multiple_of(i * BL, BL), BL), :]
        return g, _dot(tri2, _split_hi_lo(g))

    def decay_block(i, sums):
        rs = pl.ds(pl.multiple_of(i * BL, BL), BL)
        g, pp = sums
        p = pp[:, 0:2 * DK] + pp[:, 2 * DK:4 * DK]
        q_c = q_ref[0, rs, :].astype(F32) * scale
        k_c = k_ref[0, rs, :].astype(F32)
        zero = jnp.zeros((BL, DK), BF16)
        for d in range(2):
            p_d = p[:, d * DK:(d + 1) * DK]
            tot = jnp.where(par_row == 0, p_d[L - 1:L, :], p_d[BL - 1:BL, :])
            b = p_d if d == 0 else tot - p_d + g[:, DK:]
            qd = (q_c * jnp.exp2(b)).astype(BF16)
            kdec = (k_c * jnp.exp2(tot - b)).astype(BF16)
            qd_ref[d, rs, :] = qd
            kd_ref[d, rs, :] = (k_c * jnp.exp2(-b)).astype(BF16)
            for par in range(2):
                sel = par_row == par
                kdp_ref[d, rs, par * DK:(par + 1) * DK] = jnp.where(sel, kdec, zero)
                qcat_ref[rs, (2 * d + par) * DK:(2 * d + par + 1) * DK] = jnp.where(sel, qd, zero)
                last = (par + 1) * L - 1
                dec_ref[d, 2 * i + par] = jnp.broadcast_to(jnp.exp2(p_d[last:last + 1, :]), (8, DK))

    G = GLA_BLOCKS_PER_STEP
    n_groups = nb // G

    def intra_group(i):
        blks = [G * i + j for j in range(G)]
        rss = [pl.ds(pl.multiple_of(b * BL, BL), BL) for b in blks]
        vs = [v_ref[0, rs, :] for rs in rss]
        atts = [[_nt(qd_ref[d, rs, :], kd_ref[d, rs, :]) for d in range(2)] for rs in rss]
        for j, b in enumerate(blks):
            for d in range(2):
                x_ref[d, b] = _tn(vs[j], kdp_ref[d, rss[j], :])
        for j in range(G):
            att = jnp.where(masks[0], atts[j][0], 0.0) + jnp.where(masks[1], atts[j][1], 0.0)
            acc_ref[rss[j], :] = _dot(att.astype(BF16), vs[j])

    def fused_body(i, carry):
        sums = [decay_sums(G * (i + 1) + j) for j in range(G)]
        intra_group(i)
        for j in range(G):
            decay_block(G * (i + 1) + j, sums[j])
        return carry

    for j in range(G):
        decay_block(j, decay_sums(j))
    lax.fori_loop(0, n_groups - 1, fused_body, 0, unroll=True)
    intra_group(n_groups - 1)

    def state_body(d):
        def body(i, s):
            b = i if d == 0 else nb - 1 - i
            for par in ((0, 1) if d == 0 else (1, 0)):
                lane0 = (2 * d + par) * DK
                sst_ref[b, :, lane0:lane0 + DK] = s.astype(BF16)
                s = s * dec_ref[d, 2 * b + par, 0:1, :] + x_ref[d, b, :, par * DK:(par + 1) * DK]
            return s
        return body

    for d in range(2):
        lax.fori_loop(0, nb, state_body(d), jnp.zeros((GLA_DV, DK), F32), unroll=True)

    def out_body(b, carry):
        rs = pl.ds(pl.multiple_of(b * BL, BL), BL)
        o = acc_ref[rs, :] + _nt(qcat_ref[rs, :], sst_ref[b])
        ms = jnp.mean(o * o, axis=-1, keepdims=True)
        o = o * lax.rsqrt(ms + EPS) * nw_ref[...]
        o_ref[0, rs, :] = (o * _silu(gg_ref[0, rs, :].astype(F32))).astype(BF16)
        return carry

    lax.fori_loop(0, nb, out_body, 0, unroll=GLA_OUT_UNROLL)


def _gla_heads_kernel(q_ref, k_ref, v_ref, gg_ref, us_ref, a2_ref, bias_ref, nw_ref, o_ref, *scratch):
    DK, DV = GLA_DK, GLA_DV
    for hh in range(GLA_HEADS_PER_STEP):
        ks, vs, gs = slice(hh * DK, (hh + 1) * DK), slice(hh * DV, (hh + 1) * DV), slice(hh * 2 * DK, (hh + 1) * 2 * DK)
        _gla_kernel(q_ref.at[:, :, ks], k_ref.at[:, :, ks], v_ref.at[:, :, vs], gg_ref.at[:, :, vs], us_ref,
                    a2_ref.at[:, gs], bias_ref.at[:, gs], nw_ref, o_ref.at[:, :, vs], *scratch)


def _gla(u3, us3, a2, bias, nw):
    B, S, _ = u3.shape
    HPS = GLA_HEADS_PER_STEP
    DK, DV = GLA_DK, GLA_DV
    nc = S // GLA_CHUNK
    nb = nc // 2
    return pl.pallas_call(
        _gla_heads_kernel,
        grid=(B, GLA_HEADS // HPS),
        in_specs=[
            pl.BlockSpec((1, S, HPS * DK), lambda b, h: (b, 0, U_GQ // (HPS * DK) + h)),
            pl.BlockSpec((1, S, HPS * DK), lambda b, h: (b, 0, U_GK // (HPS * DK) + h)),
            pl.BlockSpec((1, S, HPS * DV), lambda b, h: (b, 0, U_GV // (HPS * DV) + h)),
            pl.BlockSpec((1, S, HPS * DV), lambda b, h: (b, 0, U_GG // (HPS * DV) + h)),
            pl.BlockSpec((1, S, LANE), lambda b, h: (b, 0, 0)),
            pl.BlockSpec((LANE, HPS * 2 * DK), lambda b, h: (0, h)),
            pl.BlockSpec((1, HPS * 2 * DK), lambda b, h: (0, h)),
            pl.BlockSpec((1, DV), lambda b, h: (0, 0)),
        ],
        out_specs=pl.BlockSpec((1, S, HPS * DV), lambda b, h: (b, 0, h)),
        out_shape=jax.ShapeDtypeStruct((B, S, GLA_VAL_W), BF16),
        scratch_shapes=[
            pltpu.VMEM((S, DV), F32),
            pltpu.VMEM((S, 2 * DK), F32),
            pltpu.VMEM((2, S, DK), BF16),
            pltpu.VMEM((2, S, DK), BF16),
            pltpu.VMEM((2, S, 2 * DK), BF16),
            pltpu.VMEM((S, 4 * DK), BF16),
            pltpu.VMEM((2, nb, DV, 2 * DK), F32),
            pltpu.VMEM((2, nc, 8, DK), F32),
            pltpu.VMEM((nb, DV, 4 * DK), BF16),
        ],
        compiler_params=_cparams(("parallel", "parallel")),
        name="gla_scan",
    )(u3, u3, u3, u3, us3, a2, bias, nw)


def _na_bias_rows(rpb):
    H, R, C = rpb.shape
    n_pos = GRID_W - NA_WIN_W
    n_neg = GRID_W - NA_WIN_W + 2
    ext = jnp.concatenate([rpb[:, :, NA_WIN_W - 1:], jnp.repeat(rpb[:, :, C - 1:], n_pos, axis=2),
                           jnp.repeat(rpb[:, :, 0:1], n_neg, axis=2), rpb[:, :, 1:NA_WIN_W - 1]], axis=2)
    return jnp.pad(ext, ((0, 0), (0, 1), (0, 0))).reshape(H // 2, 2, R + 1, 2 * GRID_W)


def _na_kernel(flag_ref, q_ref, k_ref, v_ref, ext_ref, o_ref, vx_ref, tab_ref):
    NB, S, _ = q_ref.shape
    rows = S // GRID_W
    total = NB * rows
    win_h = NA_WIN_H
    nk = win_h * GRID_W

    vx_ref[:, :, 0:LANE] = v_ref[...]
    vx_ref[:, :, LANE:2 * LANE] = jnp.ones((NB, S, LANE), BF16)
    bound_ok = flag_ref[0] != 0

    first_q = _iota2((GRID_W, LANE), 1) < NA_HEAD_DIM

    @pl.when(pl.program_id(1) == 0)
    def _():
        q_col = _iota2((GRID_W, 2 * GRID_W), 0)
        k_col = _iota2((GRID_W, 2 * GRID_W), 1) % GRID_W
        w_start = jnp.clip(q_col - NA_WIN_W // 2, 0, GRID_W - NA_WIN_W)
        in_window = (k_col >= w_start) & (k_col < w_start + NA_WIN_W)
        low = _iota2((GRID_W, 2 * GRID_W), 1) < GRID_W
        for hd in range(2):
            def skewed(rr, shift):
                row = jnp.broadcast_to(ext_ref[0, hd, rr:rr + 1, :], (GRID_W, 2 * GRID_W))
                return pltpu.roll(row, shift, axis=1, stride=1, stride_axis=0)

            for e in range(2 * NA_WIN_H - 2):
                t = jnp.where(low, skewed(e, 0), skewed(e + 1, GRID_W))
                tab_ref[e, hd * GRID_W:(hd + 1) * GRID_W, :] = jnp.where(in_window, t, -jnp.inf)

    def locate(r):
        bi = r // rows
        rl = r - bi * rows
        r0 = jnp.clip(rl - win_h // 2, 0, rows - win_h)
        return bi, pl.multiple_of(rl * GRID_W, GRID_W), rl - r0, pl.multiple_of(r0 * GRID_W, GRID_W)

    def scores(r):
        bi, q0, delta, k0 = locate(r)
        q = q_ref[bi, pl.ds(q0, GRID_W), :]
        zero = jnp.zeros_like(q)
        qs = jnp.concatenate([jnp.where(first_q, q, zero), jnp.where(first_q, zero, q)], axis=0)
        rr0 = (win_h - 1) - delta
        bias = jnp.concatenate([tab_ref[rr0 + w] for w in range(0, win_h, 2)], axis=1)
        return _nt(qs, k_ref[bi, pl.ds(k0, nk), :]) + bias

    def probs_exact(s):
        return jnp.exp2(s - jnp.max(s, axis=-1, keepdims=True)).astype(BF16)

    def probs_bounded(r):
        return jnp.exp2(scores(r)).astype(BF16)

    def attend(r, p):
        bi, q0, _, k0 = locate(r)
        ox = _dot(p, vx_ref[bi, pl.ds(k0, nk), :])
        o = ox[:, 0:LANE] / ox[:, LANE:2 * LANE]
        o_ref[bi, pl.ds(q0, GRID_W), :] = jnp.where(first_q, o[:GRID_W], o[GRID_W:]).astype(BF16)

    @pl.when(bound_ok)
    def _():
        U = NA_ROWS_PER_STEP

        def row_body(i, p_prev):
            r = i * U
            for j in range(U):
                attend(r - U + j, p_prev[j])
            return tuple(probs_bounded(r + j) for j in range(U))

        p_last = lax.fori_loop(1, total // U, row_body, tuple(probs_bounded(j) for j in range(U)), unroll=True)
        for j in range(U):
            attend(total - U + j, p_last[j])

    @pl.when(jnp.logical_not(bound_ok))
    def _():
        U = NA_ROWS_PER_STEP_EXACT

        def row_body(i, carry):
            s_cur, p_prev = carry
            r = i * U
            for j in range(U):
                attend(r - U + j, p_prev[j])
            p = tuple(probs_exact(s) for s in s_cur)
            s_next = tuple(scores(jnp.minimum(r + U + j, total - 1)) for j in range(U))
            return s_next, p

        p0 = tuple(probs_exact(scores(j)) for j in range(U))
        s1 = tuple(scores(U + j) for j in range(U))
        _, p_last = lax.fori_loop(1, total // U, row_body, (s1, p0))
        for j in range(U):
            attend(total - U + j, p_last[j])


def _na_score_bound(rpb, q_norm_w, k_norm_w):
    H = rpb.shape[0]
    qk = (NA_HEAD_DIM ** 0.5 * LOG2E * (1.0 + NA_BOUND_SLACK)) * jnp.max(jnp.abs(q_norm_w)) * jnp.max(jnp.abs(k_norm_w))
    b_max = jnp.max(rpb.reshape(H, -1), axis=1) * LOG2E
    b_self = rpb[:, NA_WIN_H - 1, NA_WIN_W - 1] * LOG2E
    bound = qk + b_max
    flag = jnp.all(bound - (b_self - qk) <= NA_MAX_BOUND_GAP)
    return bound, flag.astype(jnp.int32).reshape(1)


def _na(u3, bias_rows, flag):
    B, S, _ = u3.shape
    assert S // GRID_W >= NA_WIN_H and NA_WIN_H % 2 == 0 and 2 * GRID_W == LANE
    nb = math.gcd(B, NA_BATCH_PER_STEP)
    assert (nb * (S // GRID_W)) % NA_ROWS_PER_STEP == 0
    return pl.pallas_call(
        _na_kernel,
        grid=(NA_HEADS // 2, B // nb),
        in_specs=[
            pl.BlockSpec(memory_space=pltpu.SMEM),
            pl.BlockSpec((nb, S, LANE), lambda h, b: (b, 0, U_NQ // LANE + h)),
            pl.BlockSpec((nb, S, LANE), lambda h, b: (b, 0, U_NK // LANE + h)),
            pl.BlockSpec((nb, S, LANE), lambda h, b: (b, 0, U_NV // LANE + h)),
            pl.BlockSpec((1, 2, 2 * NA_WIN_H, 2 * GRID_W), lambda h, b: (h, 0, 0, 0)),
        ],
        out_specs=pl.BlockSpec((nb, S, LANE), lambda h, b: (b, 0, h)),
        out_shape=jax.ShapeDtypeStruct((B, S, NA_W), BF16),
        scratch_shapes=[
            pltpu.VMEM((nb, S, 2 * LANE), BF16),
            pltpu.VMEM((2 * NA_WIN_H - 2, 2 * GRID_W, 2 * GRID_W), F32),
        ],
        compiler_params=_cparams(("parallel", "arbitrary")),
        name="na_attn",
    )(flag, u3, u3, u3, bias_rows)


def _merge_kernel(x_ref, ys_ref, yg_ref, yn_ref, gate_ref, ws_ref, wg_ref, wn_ref, wo_ref, o_ref):
    D = D_MODEL
    mixed = _sigmoid(gate_ref[:, 0:D].astype(F32)) * _dot(ys_ref[...], ws_ref[...])
    mixed += _sigmoid(gate_ref[:, D:2 * D].astype(F32)) * _dot(yg_ref[...], wg_ref[...])
    mixed += _sigmoid(gate_ref[:, 2 * D:3 * D].astype(F32)) * _dot(yn_ref[...], wn_ref[...])
    o_ref[...] = x_ref[...] + _dot(mixed.astype(BF16), wo_ref[...])


def _merge(x2, ys, yg, yn, u2, ws, wg, wn, wo, tm=512):
    T = x2.shape[0]
    D = D_MODEL
    row = lambda i: (i, 0)
    fixed = lambda i: (0, 0)
    return pl.pallas_call(
        _merge_kernel,
        grid=(T // tm,),
        in_specs=[
            pl.BlockSpec((tm, D), row),
            pl.BlockSpec((tm, D), row),
            pl.BlockSpec((tm, D), row),
            pl.BlockSpec((tm, D), row),
            pl.BlockSpec((tm, N_BRANCH * D), lambda i: (i, U_GATE // (N_BRANCH * D))),
            pl.BlockSpec((D, D), fixed),
            pl.BlockSpec((D, D), fixed),
            pl.BlockSpec((D, D), fixed),
            pl.BlockSpec((D, D), fixed),
        ],
        out_specs=pl.BlockSpec((tm, D), row),
        out_shape=jax.ShapeDtypeStruct((T, D), F32),
        compiler_params=_cparams(("parallel",)),
        name="merge",
    )(x2, ys, yg, yn, u2, ws, wg, wn, wo)


def _mlp_kernel(x_ref, nw_ref, w1_ref, w2_ref, o_ref, *, tf):
    x = x_ref[...]
    ms = jnp.mean(x * x, axis=-1, keepdims=True)
    h = (x * lax.rsqrt(ms + EPS) * nw_ref[...]).astype(BF16)
    acc = x
    for f in range(D_FF // tf):
        a = jnp.maximum(_dot(h, w1_ref[:, f * tf:(f + 1) * tf]), 0.0)
        acc = acc + _dot((a * a).astype(BF16), w2_ref[f * tf:(f + 1) * tf, :])
    o_ref[...] = acc


def _mlp(x2, nw, w1, w2, tm=1024, tf=1024):
    T = x2.shape[0]
    D = D_MODEL
    resident = pl.Buffered(1)
    return pl.pallas_call(
        functools.partial(_mlp_kernel, tf=tf),
        grid=(T // tm,),
        in_specs=[
            pl.BlockSpec((tm, D), lambda i: (i, 0)),
            pl.BlockSpec((1, D), lambda i: (0, 0)),
            pl.BlockSpec((D, D_FF), lambda i: (0, 0), pipeline_mode=resident),
            pl.BlockSpec((D_FF, D), lambda i: (0, 0), pipeline_mode=resident),
        ],
        out_specs=pl.BlockSpec((tm, D), lambda i: (i, 0)),
        out_shape=jax.ShapeDtypeStruct((T, D), F32),
        compiler_params=_cparams(("parallel",)),
        name="mlp",
    )(x2, nw, w1, w2)


def _pad_rows(w, start, total):
    return jnp.zeros((total, w.shape[1]), w.dtype).at[start:start + w.shape[0]].set(w)


def kernel(x, norm_mix_w, w_in, ssd_conv_w, ssd_conv_b, ssd_dt_bias_f, ssd_dt_bias_b, ssd_a_log_f,
           ssd_a_log_b, ssd_d, ssd_norm_w, gla_a2_f, gla_a2_bias_f, gla_a2_b, gla_a2_bias_b,
           gla_norm_w, na_q_norm_w, na_k_norm_w, na_rpb, w_branch_ssd, w_branch_gla, w_branch_na,
           w_out, norm_mlp_w, w_ff1, w_ff2):
    B, S, D = x.shape
    T = B * S
    depth = w_in.shape[0]
    x2 = x.reshape(T, D)
    w_in_t = jnp.swapaxes(w_in, 1, 2)
    for l in range(depth):
        w_big, w_small = _permute_weight(w_in_t, l)
        hg = (SSD_GROUPS, SSD_HG)
        zeros_r = jnp.zeros((SSD_GROUPS, LANE - 2 * SSD_HG), F32)
        dt_bias = jnp.concatenate([ssd_dt_bias_f[l].reshape(hg), ssd_dt_bias_b[l].reshape(hg), zeros_r], axis=1)
        a_neg = jnp.concatenate([-jnp.exp(ssd_a_log_f[l]).reshape(hg), -jnp.exp(ssd_a_log_b[l]).reshape(hg),
                                 zeros_r], axis=1)
        prow3 = jnp.concatenate([dt_bias[:, None], a_neg[:, None], jnp.zeros((SSD_GROUPS, 6, LANE), F32)], axis=1)
        prow = prow3.reshape(SSD_GROUPS * 8, LANE)
        pcol = jnp.transpose(prow3[:, :, :SMALL_T_ROWS], (0, 2, 1))
        drow = jnp.repeat(ssd_d[l], SSD_HEAD_DIM)[None, :]
        hk = (GLA_HEADS, 1, GLA_DK)
        a2 = jnp.concatenate([_pad_rows(gla_a2_f[l], SM_GAF, LANE).reshape((LANE,) + hk),
                              _pad_rows(gla_a2_b[l], SM_GAB, LANE).reshape((LANE,) + hk)],
                             axis=2).reshape(LANE, -1).astype(BF16)
        a2_bias = jnp.concatenate([gla_a2_bias_f[l].reshape(hk), gla_a2_bias_b[l].reshape(hk)],
                                  axis=1).reshape(1, -1)
        na_bound, na_flag = _na_score_bound(na_rpb[l], na_q_norm_w[l], na_k_norm_w[l])
        table = _na_bias_rows(na_rpb[l] * LOG2E - na_bound[:, None, None])
        q_row = jnp.tile(na_q_norm_w[l] * (NA_HEAD_DIM ** -0.5 * LOG2E), NA_HEADS)
        k_row = jnp.tile(na_k_norm_w[l], NA_HEADS)
        qkw = jnp.concatenate([q_row[None], k_row[None], jnp.zeros((6, NA_W), F32)], axis=0)

        u2, us2, ust = _inproj(x2, norm_mix_w[l][None, :], w_big, w_small, qkw)
        u3 = u2.reshape(B, S, U_WIDTH)
        us3 = us2.reshape(B, S, SMALL_W)
        xbc = _conv(u3, ssd_conv_w[l], ssd_conv_b[l][None, :])
        y_ssd = _ssd(xbc, u3, ust, prow, pcol, drow, ssd_norm_w[l][None, :])
        y_gla = _gla(u3, us3, a2, a2_bias, gla_norm_w[l][None, :])
        y_na = _na(u3, table, na_flag)
        x2 = _merge(x2, y_ssd.reshape(T, -1), y_gla.reshape(T, -1), y_na.reshape(T, -1), u2,
                    w_branch_ssd[l].astype(BF16), w_branch_gla[l].astype(BF16),
                    w_branch_na[l].astype(BF16), w_out[l].astype(BF16))
        x2 = _mlp(x2, norm_mlp_w[l][None, :], w_ff1[l].astype(BF16), w_ff2[l].astype(BF16))
    return x2.reshape(B, S, D)
```

```python
import functools
import math

import jax
import jax.numpy as jnp
import numpy as np
from jax import lax
from jax.experimental import pallas as pl
from jax.experimental.pallas import tpu as pltpu

F32 = jnp.float32
BF16 = jnp.bfloat16

EPS = 1e-6
D_MODEL = 1024
GRID_W = 64

SSD_HEADS = 16
SSD_HEAD_DIM = 64
SSD_D_INNER = 1024
SSD_GROUPS = 2
SSD_STATE = 128
SSD_CONV = 5
SSD_CONV_DIM = 1536
SSD_CHUNK = 128
CONV_IN_BLOCKS = 3
CONV_TILE = 256
CONV_ROWS = 128
SSD_CHUNKS_PER_STEP = 2
SSD_OUT_UNROLL = 16
SSD_GROUP_W = SSD_D_INNER // SSD_GROUPS
SSD_HG = SSD_HEADS // SSD_GROUPS

GLA_HEADS = 4
GLA_DK = 128
GLA_DV = 256
GLA_KEY_W = 512
GLA_VAL_W = 1024
GLA_GATE_RANK = 16
GLA_GATE_NORM = 16.0
GLA_CHUNK = 64
GLA_BLOCKS_PER_STEP = 2
GLA_HEADS_PER_STEP = 2
GLA_OUT_UNROLL = 16

NA_HEADS = 16
NA_HEAD_DIM = 64
NA_W = 1024
NA_WIN_H = 8
NA_WIN_W = 16
NA_BATCH_PER_STEP = 4
NA_ROWS_PER_STEP = 32
NA_ROWS_PER_STEP_EXACT = 2
LOG2E = 1.4426950408889634
NA_BOUND_SLACK = 0.02
NA_MAX_BOUND_GAP = 90.0

N_BRANCH = 3
D_FF = 4096

IN_SIZES = (SSD_D_INNER, SSD_CONV_DIM, SSD_HEADS, SSD_HEADS,
            GLA_KEY_W, GLA_KEY_W, GLA_VAL_W, GLA_VAL_W, GLA_GATE_RANK, GLA_GATE_RANK,
            NA_W, NA_W, NA_W, N_BRANCH * D_MODEL)
_IN_OFF = np.concatenate([[0], np.cumsum(IN_SIZES)])
(_O_Z, _O_XBC, _O_DTF, _O_DTB, _O_GQ, _O_GK, _O_GV, _O_GG, _O_GAF, _O_GAB,
 _O_NQ, _O_NK, _O_NV, _O_GATE) = [int(v) for v in _IN_OFF[:-1]]

U_NQ = 0
U_NK = 1024
U_NV = 2048
U_GATE = 3072
U_Z = 6144
U_XBC = 7168
U_GQ = 8704
U_GK = 9216
U_GV = 9728
U_GG = 10752
U_WIDTH = 11776
INPROJ_COL_CHUNK = 1024
LANE = 128
VMEM_LIMIT = 56 * 1024 * 1024

SMALL_W = SSD_GROUPS * LANE
SM_DTF, SM_DTB, SM_GAF, SM_GAB = 0, 8, 16, 32
SMALL_T_ROWS = 2 * SSD_HG

_BIG_SEGS = ((_O_NQ, 6144), (_O_Z, 2560), (_O_GQ, 3072))


def _cparams(sem, vmem=VMEM_LIMIT):
    return pltpu.CompilerParams(dimension_semantics=sem, vmem_limit_bytes=vmem)


def _sigmoid(x):
    return 1.0 / (1.0 + jnp.exp2(x * (-LOG2E)))


def _silu(x):
    return x * _sigmoid(x)


def _softplus(x):
    return jnp.maximum(x, 0.0) + jnp.log1p(jnp.exp(-jnp.abs(x)))


def _neg_log2_sigmoid(x):
    t = x * (-LOG2E)
    return jnp.maximum(t, 0.0) + jnp.log2(1.0 + jnp.exp2(-jnp.abs(t)))


def _nt(a, b):
    return lax.dot_general(a, b, (((1,), (1,)), ((), ())), preferred_element_type=F32)


def _tn(a, b):
    return lax.dot_general(a, b, (((0,), (0,)), ((), ())), preferred_element_type=F32)


def _dot(a, b):
    return jnp.dot(a, b, preferred_element_type=F32)


def _iota2(shape, dim):
    return lax.broadcasted_iota(jnp.int32, shape, dim)


def _wperm_kernel(wt_ref, o_ref, ws_ref):
    off = 0
    for a, n in _BIG_SEGS:
        o_ref[off:off + n, :] = wt_ref[0, a:a + n, :].astype(BF16)
        off += n
    ws_ref[...] = jnp.zeros_like(ws_ref)
    for g in range(SSD_GROUPS):
        base, h0 = g * LANE, g * SSD_HG
        ws_ref[base + SM_DTF:base + SM_DTF + SSD_HG, :] = wt_ref[0, _O_DTF + h0:_O_DTF + h0 + SSD_HG, :]
        ws_ref[base + SM_DTB:base + SM_DTB + SSD_HG, :] = wt_ref[0, _O_DTB + h0:_O_DTB + h0 + SSD_HG, :]
    ws_ref[SM_GAF:SM_GAF + GLA_GATE_RANK, :] = wt_ref[0, _O_GAF:_O_GAF + GLA_GATE_RANK, :]
    ws_ref[SM_GAB:SM_GAB + GLA_GATE_RANK, :] = wt_ref[0, _O_GAB:_O_GAB + GLA_GATE_RANK, :]


def _permute_weight(w_t_all, layer, tc=256):
    _, n_in, d = w_t_all.shape
    return pl.pallas_call(
        _wperm_kernel,
        grid=(d // tc,),
        in_specs=[pl.BlockSpec((1, n_in, tc), lambda i: (layer, 0, i))],
        out_specs=[
            pl.BlockSpec((U_WIDTH, tc), lambda i: (0, i)),
            pl.BlockSpec((SMALL_W, tc), lambda i: (0, i)),
        ],
        out_shape=[
            jax.ShapeDtypeStruct((U_WIDTH, d), BF16),
            jax.ShapeDtypeStruct((SMALL_W, d), F32),
        ],
        compiler_params=_cparams(("parallel",)),
        name="w_permute",
    )(w_t_all)


def _head_rms(r, w_row):
    G = 2 * LANE
    er = _iota2((G, G), 0) // NA_HEAD_DIM
    ec = _iota2((G, G), 1) // NA_HEAD_DIM
    e_blk = jnp.where(er == ec, 1.0 / NA_HEAD_DIM, 0.0).astype(BF16)
    outs = []
    for a in range(0, r.shape[1], G):
        x = r[:, a:a + G]
        ms = _dot((x * x).astype(BF16), e_blk)
        outs.append(x * lax.rsqrt(ms + EPS))
    return jnp.concatenate(outs, axis=1) * w_row


def _inproj_kernel(x_ref, nw_ref, w_ref, ws_ref, qkw_ref, u_ref, us_ref, ust_ref, h_ref):
    tn = u_ref.shape[1]
    chunks = [(a, min(a + INPROJ_COL_CHUNK, tn)) for a in range(0, tn, INPROJ_COL_CHUNK)]

    @pl.when(pl.program_id(1) == 0)
    def _():
        x = x_ref[...]
        ms = jnp.mean(x * x, axis=-1, keepdims=True)
        h = (x * lax.rsqrt(ms + EPS) * nw_ref[...]).astype(BF16)
        h_ref[...] = h
        ws = ws_ref[...].astype(BF16)
        us_ref[...] = _nt(h, ws)
        dt_rows = jnp.concatenate([ws[g * LANE:g * LANE + SMALL_T_ROWS] for g in range(SSD_GROUPS)], axis=0)
        ust_ref[...] = _nt(dt_rows, h)
        for c, (a, b) in enumerate(chunks):
            r = _nt(h, w_ref[a:b, :])
            if c < 2:
                r = _head_rms(r, qkw_ref[c:c + 1, :])
            u_ref[:, a:b] = r.astype(BF16)

    @pl.when(pl.program_id(1) != 0)
    def _():
        for a, b in chunks:
            u_ref[:, a:b] = _nt(h_ref[...], w_ref[a:b, :]).astype(BF16)


def _inproj(x2, nw, w_big, w_small, qkw, tm=1024, tn=U_WIDTH // 4):
    assert (U_NQ, U_NK) == (0, INPROJ_COL_CHUNK) and NA_W == INPROJ_COL_CHUNK
    T = x2.shape[0]
    return pl.pallas_call(
        _inproj_kernel,
        grid=(T // tm, U_WIDTH // tn),
        in_specs=[
            pl.BlockSpec((tm, D_MODEL), lambda i, j: (i, 0)),
            pl.BlockSpec((1, D_MODEL), lambda i, j: (0, 0)),
            pl.BlockSpec((tn, D_MODEL), lambda i, j: (j, 0)),
            pl.BlockSpec((SMALL_W, D_MODEL), lambda i, j: (0, 0)),
            pl.BlockSpec((8, NA_W), lambda i, j: (0, 0)),
        ],
        out_specs=[
            pl.BlockSpec((tm, tn), lambda i, j: (i, j)),
            pl.BlockSpec((tm, SMALL_W), lambda i, j: (i, 0)),
            pl.BlockSpec((SSD_GROUPS * SMALL_T_ROWS, tm), lambda i, j: (0, i)),
        ],
        out_shape=[
            jax.ShapeDtypeStruct((T, U_WIDTH), BF16),
            jax.ShapeDtypeStruct((T, SMALL_W), F32),
            jax.ShapeDtypeStruct((SSD_GROUPS * SMALL_T_ROWS, T), F32),
        ],
        scratch_shapes=[pltpu.VMEM((tm, D_MODEL), BF16)],
        compiler_params=_cparams(("parallel", "arbitrary")),
        name="inproj",
    )(x2, nw, w_big, w_small, qkw)


def _conv_kernel(*refs):
    u_refs, (w_ref, b_ref, o_ref, xp_ref) = refs[:CONV_IN_BLOCKS], refs[CONV_IN_BLOCKS:]
    S, wb = u_refs[0].shape[1], u_refs[0].shape[2]
    C = o_ref.shape[2]
    R = CONV_ROWS
    tc = CONV_TILE
    pad = SSD_CONV // 2
    xp_ref[0:R, :] = jnp.zeros((R, C), BF16)
    xp_ref[R + S:R + S + R, :] = jnp.zeros((R, C), BF16)
    for i, u_ref in enumerate(u_refs):
        xp_ref[R:R + S, i * wb:(i + 1) * wb] = u_ref[0]
    t_i = _iota2((R, 2 * R), 0)
    j_i = _iota2((R, 2 * R), 1)
    side = [k for k in range(SSD_CONV) if k != pad]
    shifts = jnp.concatenate([(j_i == t_i + R // 2 + (k - pad)).astype(BF16) for k in side], axis=0)
    for blk in range(S // R):
        for c0 in range(0, C, tc):
            cs = slice(c0, c0 + tc)
            win = xp_ref[blk * R + R // 2:blk * R + R // 2 + 2 * R, cs]
            sh = _dot(shifts, win)
            acc = b_ref[:, cs] + w_ref[pad:pad + 1, cs] * xp_ref[R + blk * R:R + (blk + 1) * R, cs].astype(F32)
            for n, k in enumerate(side):
                acc = acc + w_ref[k:k + 1, cs] * sh[n * R:(n + 1) * R]
            o_ref[0, blk * R:(blk + 1) * R, cs] = _silu(acc).astype(BF16)


def _conv(u3, conv_w, conv_b):
    B, S, _ = u3.shape
    wb = SSD_CONV_DIM // CONV_IN_BLOCKS
    assert U_XBC % wb == 0 and wb % CONV_TILE == 0
    off = U_XBC // wb
    in_blocks = [pl.BlockSpec((1, S, wb), functools.partial(lambda b, i: (b, 0, off + i), i=i))
                 for i in range(CONV_IN_BLOCKS)]
    return pl.pallas_call(
        _conv_kernel,
        grid=(B,),
        in_specs=in_blocks + [
            pl.BlockSpec((SSD_CONV, SSD_CONV_DIM), lambda b: (0, 0)),
            pl.BlockSpec((1, SSD_CONV_DIM), lambda b: (0, 0)),
        ],
        out_specs=pl.BlockSpec((1, S, SSD_CONV_DIM), lambda b: (b, 0, 0)),
        out_shape=jax.ShapeDtypeStruct((B, S, SSD_CONV_DIM), BF16),
        scratch_shapes=[pltpu.VMEM((S + 2 * CONV_ROWS, SSD_CONV_DIM), BF16)],
        compiler_params=_cparams(("parallel",)),
        name="ssd_conv",
    )(*([u3] * CONV_IN_BLOCKS), conv_w, conv_b)


def _split_hi_lo(v):
    hi = v.astype(BF16)
    lo = (v - hi.astype(F32)).astype(BF16)
    return jnp.concatenate([hi, lo], axis=1)


def _split3(v, axis):
    hi = v.astype(BF16)
    r1 = v - hi.astype(F32)
    mid = r1.astype(BF16)
    lo = (r1 - mid.astype(F32)).astype(BF16)
    return jnp.concatenate([hi, mid, lo], axis=axis)


def _ssd_kernel(x_ref, b_ref, c_ref, z_ref, ust_ref, prow_ref, pcol_ref, drow_ref, nw_ref,
                o_ref, acc_ref, cumc_ref, cumr_ref, wdt_ref, dec_ref, xs_ref, decx_ref, sst_ref, st_ref):
    L = SSD_CHUNK
    S = x_ref.shape[1]
    nc = S // L
    W = SSD_GROUP_W
    HG = SSD_HG
    R = 2 * HG

    ii = _iota2((L, L), 0)
    jj = _iota2((L, L), 1)
    tril = ii >= jj
    triu = jj >= ii
    tril_b = tril.astype(BF16)
    triu_b = triu.astype(BF16)

    er = _iota2((LANE, 2 * W), 0)
    ec = _iota2((LANE, 2 * W), 1)
    e = (er == jnp.where(ec < W, ec // SSD_HEAD_DIM, (ec - W) // SSD_HEAD_DIM + HG)).astype(BF16)
    e2 = jnp.concatenate([e, e], axis=0)

    a_row = prow_ref[1:2, :]
    bias_col = pcol_ref[0, :, 0:1]
    a_col = pcol_ref[0, :, 1:2]

    lane_fwd = _iota2((L, LANE), 1) < HG
    row_fwd = _iota2((R, L), 0) < HG
    lane_half = _iota2((L, LANE), 1) < SSD_HEAD_DIM

    def decay_sums(c):
        rs = pl.ds(pl.multiple_of(c * L, L), L)
        dt_r = _softplus(ust_ref[:, rs] + bias_col)
        a_r = dt_r * a_col
        dt_c = jnp.concatenate([dt_r, jnp.zeros((L - R, L), F32)], axis=0).T
        a_c = dt_c * a_row
        return dt_c, a_c, _dot(tril_b, _split3(a_c, 1)), dt_r, a_r, _dot(_split3(a_r, 0), triu_b)

    def decay_store(c, sums):
        rs = pl.ds(pl.multiple_of(c * L, L), L)
        dt_c, a_c, pp, dt_r, a_r, pr = sums
        p_c = pp[:, 0:LANE] + pp[:, LANE:2 * LANE] + pp[:, 2 * LANE:3 * LANE]
        tot_c = p_c[L - 1:L, :]
        cum_c = jnp.where(lane_fwd, p_c, tot_c - p_c + a_c)
        cumc_ref[rs, :] = cum_c * LOG2E
        wdt_ref[rs, :] = jnp.exp(tot_c - cum_c) * dt_c
        dec_ref[c] = jnp.broadcast_to(jnp.exp(tot_c), (16, LANE))
        p_r = pr[0:R] + pr[R:2 * R] + pr[2 * R:3 * R]
        tot_r = p_r[:, L - 1:L]
        cum_r = jnp.where(row_fwd, p_r, tot_r - p_r + a_r)
        cumr_ref[:, rs] = (cum_r - jnp.log(dt_r)) * LOG2E

    def intra_chunk(c):
        rs = pl.ds(pl.multiple_of(c * L, L), L)
        x_c = x_ref[0, rs, :]
        b_c = b_ref[0, rs, :]
        cb = _nt(c_ref[0, rs, :], b_c)
        ex = _dot(_split_hi_lo(jnp.concatenate([wdt_ref[rs, :], dec_ref[c]], axis=0)), e2)
        decx_ref[c] = ex[L:L + 8]
        xf = x_c.astype(F32)
        xw = jnp.concatenate([xf * ex[0:L, 0:W], xf * ex[0:L, W:2 * W]], axis=1).astype(BF16)
        xs_ref[c] = _tn(b_c, xw)
        cum_c = cumc_ref[rs, :]
        cum_r = cumr_ref[:, rs]
        for hp in range(HG // 2):
            ms = []
            for hh in range(2):
                hf = 2 * hp + hh
                hb = HG + 2 * hp + hh
                decf = jnp.exp2(jnp.where(tril, cum_c[:, hf:hf + 1] - cum_r[hf:hf + 1, :], -jnp.inf))
                decb = jnp.exp2(jnp.where(triu, cum_c[:, hb:hb + 1] - cum_r[hb:hb + 1, :], -jnp.inf))
                ms.append((cb * (decf + decb)).astype(BF16))
            m2 = jnp.concatenate(ms, axis=1)
            xp = x_c[:, hp * LANE:(hp + 1) * LANE]
            xz = jnp.zeros_like(xp)
            x2 = jnp.concatenate([jnp.where(lane_half, xp, xz), jnp.where(lane_half, xz, xp)], axis=0)
            acc_ref[rs, hp * LANE:(hp + 1) * LANE] = _dot(m2, x2)

    G = SSD_CHUNKS_PER_STEP
    n_steps = nc // G

    def fused_body(i, carry):
        sums = [decay_sums(G * (i + 1) + j) for j in range(G)]
        for j in range(G):
            intra_chunk(G * i + j)
        for j in range(G):
            decay_store(G * (i + 1) + j, sums[j])
        return carry

    for j in range(G):
        decay_store(j, decay_sums(j))
    lax.fori_loop(0, n_steps - 1, fused_body, 0, unroll=True)
    for j in range(G):
        intra_chunk(G * (n_steps - 1) + j)

    st_ref[...] = jnp.zeros_like(st_ref)

    def state_body(i, carry):
        for d, c in enumerate((i, nc - 1 - i)):
            ls = slice(d * W, (d + 1) * W)
            s_prev = st_ref[:, ls]
            sst_ref[c, :, ls] = s_prev.astype(BF16)
            st_ref[:, ls] = s_prev * decx_ref[c, 0:1, ls] + xs_ref[c, :, ls]
        return carry

    lax.fori_loop(0, nc, state_body, 0, unroll=True)

    def out_body(c, carry):
        rs = pl.ds(pl.multiple_of(c * L, L), L)
        c_c = c_ref[0, rs, :]
        ex = _split_hi_lo(jnp.exp2(cumc_ref[rs, :]))
        y = acc_ref[rs, :] + x_ref[0, rs, :].astype(F32) * drow_ref[...]
        for d in range(2):
            ls = slice(d * W, (d + 1) * W)
            y = y + _dot(c_c, sst_ref[c, :, ls]) * _dot(ex, e2[:, ls])
        y = y * _silu(z_ref[0, rs, :].astype(F32))
        ms = jnp.mean(y * y, axis=-1, keepdims=True)
        o_ref[0, rs, :] = (y * lax.rsqrt(ms + EPS) * nw_ref[...]).astype(BF16)
        return carry

    lax.fori_loop(0, nc, out_body, 0, unroll=SSD_OUT_UNROLL)


def _ssd(xbc, u3, ust, prow, pcol, drow, nw):
    B, S, _ = xbc.shape
    W = SSD_GROUP_W
    N = SSD_STATE
    nc = S // SSD_CHUNK
    return pl.pallas_call(
        _ssd_kernel,
        grid=(B, SSD_GROUPS),
        in_specs=[
            pl.BlockSpec((1, S, W), lambda b, g: (b, 0, g)),
            pl.BlockSpec((1, S, N), lambda b, g: (b, 0, SSD_D_INNER // N + g)),
            pl.BlockSpec((1, S, N), lambda b, g: (b, 0, SSD_D_INNER // N + SSD_GROUPS + g)),
            pl.BlockSpec((1, S, W), lambda b, g: (b, 0, U_Z // W + g)),
            pl.BlockSpec((SMALL_T_ROWS, S), lambda b, g: (g, b)),
            pl.BlockSpec((8, LANE), lambda b, g: (g, 0)),
            pl.BlockSpec((1, SMALL_T_ROWS, 8), lambda b, g: (g, 0, 0)),
            pl.BlockSpec((1, W), lambda b, g: (0, g)),
            pl.BlockSpec((1, W), lambda b, g: (0, g)),
        ],
        out_specs=pl.BlockSpec((1, S, W), lambda b, g: (b, 0, g)),
        out_shape=jax.ShapeDtypeStruct((B, S, SSD_D_INNER), BF16),
        scratch_shapes=[
            pltpu.VMEM((S, W), F32),
            pltpu.VMEM((S, LANE), F32),
            pltpu.VMEM((SMALL_T_ROWS, S), F32),
            pltpu.VMEM((S, LANE), F32),
            pltpu.VMEM((nc, 16, LANE), F32),
            pltpu.VMEM((nc, N, 2 * W), F32),
            pltpu.VMEM((nc, 8, 2 * W), F32),
            pltpu.VMEM((nc, N, 2 * W), BF16),
            pltpu.VMEM((N, 2 * W), F32),
        ],
        compiler_params=_cparams(("parallel", "parallel")),
        name="ssd_scan",
    )(xbc, xbc, xbc, u3, ust, prow, pcol, drow, nw)


def _gla_kernel(q_ref, k_ref, v_ref, gg_ref, us_ref, a2_ref, bias_ref, nw_ref,
                o_ref, acc_ref, g_ref, qd_ref, kd_ref, kdp_ref, qcat_ref, x_ref, dec_ref, sst_ref):
    L = GLA_CHUNK
    BL = 2 * L
    DK = GLA_DK
    S = q_ref.shape[1]
    nb = S // BL
    scale = DK ** -0.5

    ii = _iota2((BL, BL), 0)
    jj = _iota2((BL, BL), 1)
    same = (ii // L) == (jj // L)
    masks = (same & (ii >= jj), same & (jj >= ii))
    tri2 = masks[0].astype(BF16)
    par_row = _iota2((BL, DK), 0) // L

    ga = us_ref[0].astype(BF16)
    g_ref[...] = _neg_log2_sigmoid(_dot(ga, a2_ref[...]) + bias_ref[...]) * (-1.0 / GLA_GATE_NORM)

    def decay_sums(i):
        g = g_ref[pl.ds(pl.multiple_of(i * BL, BL), BL), :]
        return g, _dot(tri2, _split_hi_lo(g))

    def decay_block(i, sums):
        rs = pl.ds(pl.multiple_of(i * BL, BL), BL)
        g, pp = sums
        p = pp[:, 0:2 * DK] + pp[:, 2 * DK:4 * DK]
        q_c = q_ref[0, rs, :].astype(F32) * scale
        k_c = k_ref[0, rs, :].astype(F32)
        zero = jnp.zeros((BL, DK), BF16)
        for d in range(2):
            p_d = p[:, d * DK:(d + 1) * DK]
            tot = jnp.where(par_row == 0, p_d[L - 1:L, :], p_d[BL - 1:BL, :])
            b = p_d if d == 0 else tot - p_d + g[:, DK:]
            qd = (q_c * jnp.exp2(b)).astype(BF16)
            kdec = (k_c * jnp.exp2(tot - b)).astype(BF16)
            qd_ref[d, rs, :] = qd
            kd_ref[d, rs, :] = (k_c * jnp.exp2(-b)).astype(BF16)
            for par in range(2):
                sel = par_row == par
                kdp_ref[d, rs, par * DK:(par + 1) * DK] = jnp.where(sel, kdec, zero)
                qcat_ref[rs, (2 * d + par) * DK:(2 * d + par + 1) * DK] = jnp.where(sel, qd, zero)
                last = (par + 1) * L - 1
                dec_ref[d, 2 * i + par] = jnp.broadcast_to(jnp.exp2(p_d[last:last + 1, :]), (8, DK))

    G = GLA_BLOCKS_PER_STEP
    n_groups = nb // G

    def intra_group(i):
        blks = [G * i + j for j in range(G)]
        rss = [pl.ds(pl.multiple_of(b * BL, BL), BL) for b in blks]
        vs = [v_ref[0, rs, :] for rs in rss]
        atts = [[_nt(qd_ref[d, rs, :], kd_ref[d, rs, :]) for d in range(2)] for rs in rss]
        for j, b in enumerate(blks):
            for d in range(2):
                x_ref[d, b] = _tn(vs[j], kdp_ref[d, rss[j], :])
        for j in range(G):
            att = jnp.where(masks[0], atts[j][0], 0.0) + jnp.where(masks[1], atts[j][1], 0.0)
            acc_ref[rss[j], :] = _dot(att.astype(BF16), vs[j])

    def fused_body(i, carry):
        sums = [decay_sums(G * (i + 1) + j) for j in range(G)]
        intra_group(i)
        for j in range(G):
            decay_block(G * (i + 1) + j, sums[j])
        return carry

    for j in range(G):
        decay_block(j, decay_sums(j))
    lax.fori_loop(0, n_groups - 1, fused_body, 0, unroll=True)
    intra_group(n_groups - 1)

    def state_body(d):
        def body(i, s):
            b = i if d == 0 else nb - 1 - i
            for par in ((0, 1) if d == 0 else (1, 0)):
                lane0 = (2 * d + par) * DK
                sst_ref[b, :, lane0:lane0 + DK] = s.astype(BF16)
                s = s * dec_ref[d, 2 * b + par, 0:1, :] + x_ref[d, b, :, par * DK:(par + 1) * DK]
            return s
        return body

    for d in range(2):
        lax.fori_loop(0, nb, state_body(d), jnp.zeros((GLA_DV, DK), F32), unroll=True)

    def out_body(b, carry):
        rs = pl.ds(pl.multiple_of(b * BL, BL), BL)
        o = acc_ref[rs, :] + _nt(qcat_ref[rs, :], sst_ref[b])
        ms = jnp.mean(o * o, axis=-1, keepdims=True)
        o = o * lax.rsqrt(ms + EPS) * nw_ref[...]
        o_ref[0, rs, :] = (o * _silu(gg_ref[0, rs, :].astype(F32))).astype(BF16)
        return carry

    lax.fori_loop(0, nb, out_body, 0, unroll=GLA_OUT_UNROLL)


def _gla_heads_kernel(q_ref, k_ref, v_ref, gg_ref, us_ref, a2_ref, bias_ref, nw_ref, o_ref, *scratch):
    DK, DV = GLA_DK, GLA_DV
    for hh in range(GLA_HEADS_PER_STEP):
        ks, vs, gs = slice(hh * DK, (hh + 1) * DK), slice(hh * DV, (hh + 1) * DV), slice(hh * 2 * DK, (hh + 1) * 2 * DK)
        _gla_kernel(q_ref.at[:, :, ks], k_ref.at[:, :, ks], v_ref.at[:, :, vs], gg_ref.at[:, :, vs], us_ref,
                    a2_ref.at[:, gs], bias_ref.at[:, gs], nw_ref, o_ref.at[:, :, vs], *scratch)


def _gla(u3, us3, a2, bias, nw):
    B, S, _ = u3.shape
    HPS = GLA_HEADS_PER_STEP
    DK, DV = GLA_DK, GLA_DV
    nc = S // GLA_CHUNK
    nb = nc // 2
    return pl.pallas_call(
        _gla_heads_kernel,
        grid=(B, GLA_HEADS // HPS),
        in_specs=[
            pl.BlockSpec((1, S, HPS * DK), lambda b, h: (b, 0, U_GQ // (HPS * DK) + h)),
            pl.BlockSpec((1, S, HPS * DK), lambda b, h: (b, 0, U_GK // (HPS * DK) + h)),
            pl.BlockSpec((1, S, HPS * DV), lambda b, h: (b, 0, U_GV // (HPS * DV) + h)),
            pl.BlockSpec((1, S, HPS * DV), lambda b, h: (b, 0, U_GG // (HPS * DV) + h)),
            pl.BlockSpec((1, S, LANE), lambda b, h: (b, 0, 0)),
            pl.BlockSpec((LANE, HPS * 2 * DK), lambda b, h: (0, h)),
            pl.BlockSpec((1, HPS * 2 * DK), lambda b, h: (0, h)),
            pl.BlockSpec((1, DV), lambda b, h: (0, 0)),
        ],
        out_specs=pl.BlockSpec((1, S, HPS * DV), lambda b, h: (b, 0, h)),
        out_shape=jax.ShapeDtypeStruct((B, S, GLA_VAL_W), BF16),
        scratch_shapes=[
            pltpu.VMEM((S, DV), F32),
            pltpu.VMEM((S, 2 * DK), F32),
            pltpu.VMEM((2, S, DK), BF16),
            pltpu.VMEM((2, S, DK), BF16),
            pltpu.VMEM((2, S, 2 * DK), BF16),
            pltpu.VMEM((S, 4 * DK), BF16),
            pltpu.VMEM((2, nb, DV, 2 * DK), F32),
            pltpu.VMEM((2, nc, 8, DK), F32),
            pltpu.VMEM((nb, DV, 4 * DK), BF16),
        ],
        compiler_params=_cparams(("parallel", "parallel")),
        name="gla_scan",
    )(u3, u3, u3, u3, us3, a2, bias, nw)


def _na_bias_rows(rpb):
    H, R, C = rpb.shape
    n_pos = GRID_W - NA_WIN_W
    n_neg = GRID_W - NA_WIN_W + 2
    ext = jnp.concatenate([rpb[:, :, NA_WIN_W - 1:], jnp.repeat(rpb[:, :, C - 1:], n_pos, axis=2),
                           jnp.repeat(rpb[:, :, 0:1], n_neg, axis=2), rpb[:, :, 1:NA_WIN_W - 1]], axis=2)
    return jnp.pad(ext, ((0, 0), (0, 1), (0, 0))).reshape(H // 2, 2, R + 1, 2 * GRID_W)


def _na_kernel(flag_ref, q_ref, k_ref, v_ref, ext_ref, o_ref, vx_ref, tab_ref):
    NB, S, _ = q_ref.shape
    rows = S // GRID_W
    total = NB * rows
    win_h = NA_WIN_H
    nk = win_h * GRID_W

    vx_ref[:, :, 0:LANE] = v_ref[...]
    vx_ref[:, :, LANE:2 * LANE] = jnp.ones((NB, S, LANE), BF16)
    bound_ok = flag_ref[0] != 0

    first_q = _iota2((GRID_W, LANE), 1) < NA_HEAD_DIM

    @pl.when(pl.program_id(1) == 0)
    def _():
        q_col = _iota2((GRID_W, 2 * GRID_W), 0)
        k_col = _iota2((GRID_W, 2 * GRID_W), 1) % GRID_W
        w_start = jnp.clip(q_col - NA_WIN_W // 2, 0, GRID_W - NA_WIN_W)
        in_window = (k_col >= w_start) & (k_col < w_start + NA_WIN_W)
        low = _iota2((GRID_W, 2 * GRID_W), 1) < GRID_W
        for hd in range(2):
            def skewed(rr, shift):
                row = jnp.broadcast_to(ext_ref[0, hd, rr:rr + 1, :], (GRID_W, 2 * GRID_W))
                return pltpu.roll(row, shift, axis=1, stride=1, stride_axis=0)

            for e in range(2 * NA_WIN_H - 2):
                t = jnp.where(low, skewed(e, 0), skewed(e + 1, GRID_W))
                tab_ref[e, hd * GRID_W:(hd + 1) * GRID_W, :] = jnp.where(in_window, t, -jnp.inf)

    def locate(r):
        bi = r // rows
        rl = r - bi * rows
        r0 = jnp.clip(rl - win_h // 2, 0, rows - win_h)
        return bi, pl.multiple_of(rl * GRID_W, GRID_W), rl - r0, pl.multiple_of(r0 * GRID_W, GRID_W)

    def scores(r):
        bi, q0, delta, k0 = locate(r)
        q = q_ref[bi, pl.ds(q0, GRID_W), :]
        zero = jnp.zeros_like(q)
        qs = jnp.concatenate([jnp.where(first_q, q, zero), jnp.where(first_q, zero, q)], axis=0)
        rr0 = (win_h - 1) - delta
        bias = jnp.concatenate([tab_ref[rr0 + w] for w in range(0, win_h, 2)], axis=1)
        return _nt(qs, k_ref[bi, pl.ds(k0, nk), :]) + bias

    def probs_exact(s):
        return jnp.exp2(s - jnp.max(s, axis=-1, keepdims=True)).astype(BF16)

    def probs_bounded(r):
        return jnp.exp2(scores(r)).astype(BF16)

    def attend(r, p):
        bi, q0, _, k0 = locate(r)
        ox = _dot(p, vx_ref[bi, pl.ds(k0, nk), :])
        o = ox[:, 0:LANE] / ox[:, LANE:2 * LANE]
        o_ref[bi, pl.ds(q0, GRID_W), :] = jnp.where(first_q, o[:GRID_W], o[GRID_W:]).astype(BF16)

    @pl.when(bound_ok)
    def _():
        U = NA_ROWS_PER_STEP

        def row_body(i, p_prev):
            r = i * U
            for j in range(U):
                attend(r - U + j, p_prev[j])
            return tuple(probs_bounded(r + j) for j in range(U))

        p_last = lax.fori_loop(1, total // U, row_body, tuple(probs_bounded(j) for j in range(U)), unroll=True)
        for j in range(U):
            attend(total - U + j, p_last[j])

    @pl.when(jnp.logical_not(bound_ok))
    def _():
        U = NA_ROWS_PER_STEP_EXACT

        def row_body(i, carry):
            s_cur, p_prev = carry
            r = i * U
            for j in range(U):
                attend(r - U + j, p_prev[j])
            p = tuple(probs_exact(s) for s in s_cur)
            s_next = tuple(scores(jnp.minimum(r + U + j, total - 1)) for j in range(U))
            return s_next, p

        p0 = tuple(probs_exact(scores(j)) for j in range(U))
        s1 = tuple(scores(U + j) for j in range(U))
        _, p_last = lax.fori_loop(1, total // U, row_body, (s1, p0))
        for j in range(U):
            attend(total - U + j, p_last[j])


def _na_score_bound(rpb, q_norm_w, k_norm_w):
    H = rpb.shape[0]
    qk = (NA_HEAD_DIM ** 0.5 * LOG2E * (1.0 + NA_BOUND_SLACK)) * jnp.max(jnp.abs(q_norm_w)) * jnp.max(jnp.abs(k_norm_w))
    b_max = jnp.max(rpb.reshape(H, -1), axis=1) * LOG2E
    b_self = rpb[:, NA_WIN_H - 1, NA_WIN_W - 1] * LOG2E
    bound = qk + b_max
    flag = jnp.all(bound - (b_self - qk) <= NA_MAX_BOUND_GAP)
    return bound, flag.astype(jnp.int32).reshape(1)


def _na(u3, bias_rows, flag):
    B, S, _ = u3.shape
    assert S // GRID_W >= NA_WIN_H and NA_WIN_H % 2 == 0 and 2 * GRID_W == LANE
    nb = math.gcd(B, NA_BATCH_PER_STEP)
    assert (nb * (S // GRID_W)) % NA_ROWS_PER_STEP == 0
    return pl.pallas_call(
        _na_kernel,
        grid=(NA_HEADS // 2, B // nb),
        in_specs=[
            pl.BlockSpec(memory_space=pltpu.SMEM),
            pl.BlockSpec((nb, S, LANE), lambda h, b: (b, 0, U_NQ // LANE + h)),
            pl.BlockSpec((nb, S, LANE), lambda h, b: (b, 0, U_NK // LANE + h)),
            pl.BlockSpec((nb, S, LANE), lambda h, b: (b, 0, U_NV // LANE + h)),
            pl.BlockSpec((1, 2, 2 * NA_WIN_H, 2 * GRID_W), lambda h, b: (h, 0, 0, 0)),
        ],
        out_specs=pl.BlockSpec((nb, S, LANE), lambda h, b: (b, 0, h)),
        out_shape=jax.ShapeDtypeStruct((B, S, NA_W), BF16),
        scratch_shapes=[
            pltpu.VMEM((nb, S, 2 * LANE), BF16),
            pltpu.VMEM((2 * NA_WIN_H - 2, 2 * GRID_W, 2 * GRID_W), F32),
        ],
        compiler_params=_cparams(("parallel", "arbitrary")),
        name="na_attn",
    )(flag, u3, u3, u3, bias_rows)


def _merge_kernel(x_ref, ys_ref, yg_ref, yn_ref, gate_ref, ws_ref, wg_ref, wn_ref, wo_ref, o_ref):
    D = D_MODEL
    mixed = _sigmoid(gate_ref[:, 0:D].astype(F32)) * _dot(ys_ref[...], ws_ref[...])
    mixed += _sigmoid(gate_ref[:, D:2 * D].astype(F32)) * _dot(yg_ref[...], wg_ref[...])
    mixed += _sigmoid(gate_ref[:, 2 * D:3 * D].astype(F32)) * _dot(yn_ref[...], wn_ref[...])
    o_ref[...] = x_ref[...] + _dot(mixed.astype(BF16), wo_ref[...])


def _merge(x2, ys, yg, yn, u2, ws, wg, wn, wo, tm=512):
    T = x2.shape[0]
    D = D_MODEL
    row = lambda i: (i, 0)
    fixed = lambda i: (0, 0)
    return pl.pallas_call(
        _merge_kernel,
        grid=(T // tm,),
        in_specs=[
            pl.BlockSpec((tm, D), row),
            pl.BlockSpec((tm, D), row),
            pl.BlockSpec((tm, D), row),
            pl.BlockSpec((tm, D), row),
            pl.BlockSpec((tm, N_BRANCH * D), lambda i: (i, U_GATE // (N_BRANCH * D))),
            pl.BlockSpec((D, D), fixed),
            pl.BlockSpec((D, D), fixed),
            pl.BlockSpec((D, D), fixed),
            pl.BlockSpec((D, D), fixed),
        ],
        out_specs=pl.BlockSpec((tm, D), row),
        out_shape=jax.ShapeDtypeStruct((T, D), F32),
        compiler_params=_cparams(("parallel",)),
        name="merge",
    )(x2, ys, yg, yn, u2, ws, wg, wn, wo)


def _mlp_kernel(x_ref, nw_ref, w1_ref, w2_ref, o_ref, *, tf):
    x = x_ref[...]
    ms = jnp.mean(x * x, axis=-1, keepdims=True)
    h = (x * lax.rsqrt(ms + EPS) * nw_ref[...]).astype(BF16)
    acc = x
    for f in range(D_FF // tf):
        a = jnp.maximum(_dot(h, w1_ref[:, f * tf:(f + 1) * tf]), 0.0)
        acc = acc + _dot((a * a).astype(BF16), w2_ref[f * tf:(f + 1) * tf, :])
    o_ref[...] = acc


def _merge_mlp_kernel(x_ref, ys_ref, yg_ref, yn_ref, gate_ref, ws_ref, wg_ref, wn_ref, wo_ref, nw_ref, w1_ref, w2_ref,
                      o_ref, xm_ref, *, tf):
    _merge_kernel(x_ref, ys_ref, yg_ref, yn_ref, gate_ref, ws_ref, wg_ref, wn_ref, wo_ref, xm_ref)
    _mlp_kernel(xm_ref, nw_ref, w1_ref, w2_ref, o_ref, tf=tf)


def _merge_mlp(x2, ys, yg, yn, u2, ws, wg, wn, wo, nw, w1, w2, tm=512, tf=1024):
    T = x2.shape[0]
    D = D_MODEL
    row = lambda i: (i, 0)
    resident = pl.Buffered(1)
    weight = lambda shape: pl.BlockSpec(shape, lambda i: (0, 0), pipeline_mode=resident)
    return pl.pallas_call(
        functools.partial(_merge_mlp_kernel, tf=tf),
        grid=(T // tm,),
        in_specs=[
            pl.BlockSpec((tm, D), row),
            pl.BlockSpec((tm, D), row),
            pl.BlockSpec((tm, D), row),
            pl.BlockSpec((tm, D), row),
            pl.BlockSpec((tm, N_BRANCH * D), lambda i: (i, U_GATE // (N_BRANCH * D))),
            weight((D, D)), weight((D, D)), weight((D, D)), weight((D, D)),
            pl.BlockSpec((1, D), lambda i: (0, 0)),
            weight((D, D_FF)), weight((D_FF, D)),
        ],
        out_specs=pl.BlockSpec((tm, D), row),
        out_shape=jax.ShapeDtypeStruct((T, D), F32),
        scratch_shapes=[pltpu.VMEM((tm, D), F32)],
        compiler_params=_cparams(("parallel",)),
        name="merge_mlp",
    )(x2, ys, yg, yn, u2, ws, wg, wn, wo, nw, w1, w2)


def _mlp(x2, nw, w1, w2, tm=1024, tf=1024):
    T = x2.shape[0]
    D = D_MODEL
    resident = pl.Buffered(1)
    return pl.pallas_call(
        functools.partial(_mlp_kernel, tf=tf),
        grid=(T // tm,),
        in_specs=[
            pl.BlockSpec((tm, D), lambda i: (i, 0)),
            pl.BlockSpec((1, D), lambda i: (0, 0)),
            pl.BlockSpec((D, D_FF), lambda i: (0, 0), pipeline_mode=resident),
            pl.BlockSpec((D_FF, D), lambda i: (0, 0), pipeline_mode=resident),
        ],
        out_specs=pl.BlockSpec((tm, D), lambda i: (i, 0)),
        out_shape=jax.ShapeDtypeStruct((T, D), F32),
        compiler_params=_cparams(("parallel",)),
        name="mlp",
    )(x2, nw, w1, w2)


def _pad_rows(w, start, total):
    return jnp.zeros((total, w.shape[1]), w.dtype).at[start:start + w.shape[0]].set(w)


def kernel(x, norm_mix_w, w_in, ssd_conv_w, ssd_conv_b, ssd_dt_bias_f, ssd_dt_bias_b, ssd_a_log_f,
           ssd_a_log_b, ssd_d, ssd_norm_w, gla_a2_f, gla_a2_bias_f, gla_a2_b, gla_a2_bias_b,
           gla_norm_w, na_q_norm_w, na_k_norm_w, na_rpb, w_branch_ssd, w_branch_gla, w_branch_na,
           w_out, norm_mlp_w, w_ff1, w_ff2):
    B, S, D = x.shape
    T = B * S
    depth = w_in.shape[0]
    x2 = x.reshape(T, D)
    w_in_t = jnp.swapaxes(w_in, 1, 2)
    for l in range(depth):
        w_big, w_small = _permute_weight(w_in_t, l)
        hg = (SSD_GROUPS, SSD_HG)
        zeros_r = jnp.zeros((SSD_GROUPS, LANE - 2 * SSD_HG), F32)
        dt_bias = jnp.concatenate([ssd_dt_bias_f[l].reshape(hg), ssd_dt_bias_b[l].reshape(hg), zeros_r], axis=1)
        a_neg = jnp.concatenate([-jnp.exp(ssd_a_log_f[l]).reshape(hg), -jnp.exp(ssd_a_log_b[l]).reshape(hg),
                                 zeros_r], axis=1)
        prow3 = jnp.concatenate([dt_bias[:, None], a_neg[:, None], jnp.zeros((SSD_GROUPS, 6, LANE), F32)], axis=1)
        prow = prow3.reshape(SSD_GROUPS * 8, LANE)
        pcol = jnp.transpose(prow3[:, :, :SMALL_T_ROWS], (0, 2, 1))
        drow = jnp.repeat(ssd_d[l], SSD_HEAD_DIM)[None, :]
        hk = (GLA_HEADS, 1, GLA_DK)
        a2 = jnp.concatenate([_pad_rows(gla_a2_f[l], SM_GAF, LANE).reshape((LANE,) + hk),
                              _pad_rows(gla_a2_b[l], SM_GAB, LANE).reshape((LANE,) + hk)],
                             axis=2).reshape(LANE, -1).astype(BF16)
        a2_bias = jnp.concatenate([gla_a2_bias_f[l].reshape(hk), gla_a2_bias_b[l].reshape(hk)],
                                  axis=1).reshape(1, -1)
        na_bound, na_flag = _na_score_bound(na_rpb[l], na_q_norm_w[l], na_k_norm_w[l])
        table = _na_bias_rows(na_rpb[l] * LOG2E - na_bound[:, None, None])
        q_row = jnp.tile(na_q_norm_w[l] * (NA_HEAD_DIM ** -0.5 * LOG2E), NA_HEADS)
        k_row = jnp.tile(na_k_norm_w[l], NA_HEADS)
        qkw = jnp.concatenate([q_row[None], k_row[None], jnp.zeros((6, NA_W), F32)], axis=0)

        u2, us2, ust = _inproj(x2, norm_mix_w[l][None, :], w_big, w_small, qkw)
        u3 = u2.reshape(B, S, U_WIDTH)
        us3 = us2.reshape(B, S, SMALL_W)
        xbc = _conv(u3, ssd_conv_w[l], ssd_conv_b[l][None, :])
        y_ssd = _ssd(xbc, u3, ust, prow, pcol, drow, ssd_norm_w[l][None, :])
        y_gla = _gla(u3, us3, a2, a2_bias, gla_norm_w[l][None, :])
        y_na = _na(u3, table, na_flag)
        x2 = _merge_mlp(x2, y_ssd.reshape(T, -1), y_gla.reshape(T, -1), y_na.reshape(T, -1), u2,
                        w_branch_ssd[l].astype(BF16), w_branch_gla[l].astype(BF16),
                        w_branch_na[l].astype(BF16), w_out[l].astype(BF16),
                        norm_mlp_w[l][None, :], w_ff1[l].astype(BF16), w_ff2[l].astype(BF16))
    return x2.reshape(B, S, D)
```

```python
import functools
import math

import jax
import jax.numpy as jnp
import numpy as np
from jax import lax
from jax.experimental import pallas as pl
from jax.experimental.pallas import tpu as pltpu

F32 = jnp.float32
BF16 = jnp.bfloat16

EPS = 1e-6
D_MODEL = 1024
GRID_W = 64

SSD_HEADS = 16
SSD_HEAD_DIM = 64
SSD_D_INNER = 1024
SSD_GROUPS = 2
SSD_STATE = 128
SSD_CONV = 5
SSD_CONV_DIM = 1536
SSD_CHUNK = 128
CONV_IN_BLOCKS = 3
CONV_TILE = 256
CONV_ROWS = 128
SSD_CHUNKS_PER_STEP = 2
SSD_OUT_UNROLL = 16
SSD_GROUP_W = SSD_D_INNER // SSD_GROUPS
SSD_HG = SSD_HEADS // SSD_GROUPS

GLA_HEADS = 4
GLA_DK = 128
GLA_DV = 256
GLA_KEY_W = 512
GLA_VAL_W = 1024
GLA_GATE_RANK = 16
GLA_GATE_NORM = 16.0
GLA_CHUNK = 64
GLA_BLOCKS_PER_STEP = 2
GLA_HEADS_PER_STEP = 2
GLA_OUT_UNROLL = 16

NA_HEADS = 16
NA_HEAD_DIM = 64
NA_W = 1024
NA_WIN_H = 8
NA_WIN_W = 16
NA_BATCH_PER_STEP = 4
NA_ROWS_PER_STEP = 32
NA_ROWS_PER_STEP_EXACT = 2
LOG2E = 1.4426950408889634
NA_BOUND_SLACK = 0.02
NA_MAX_BOUND_GAP = 90.0

N_BRANCH = 3
D_FF = 4096

IN_SIZES = (SSD_D_INNER, SSD_CONV_DIM, SSD_HEADS, SSD_HEADS,
            GLA_KEY_W, GLA_KEY_W, GLA_VAL_W, GLA_VAL_W, GLA_GATE_RANK, GLA_GATE_RANK,
            NA_W, NA_W, NA_W, N_BRANCH * D_MODEL)
_IN_OFF = np.concatenate([[0], np.cumsum(IN_SIZES)])
(_O_Z, _O_XBC, _O_DTF, _O_DTB, _O_GQ, _O_GK, _O_GV, _O_GG, _O_GAF, _O_GAB,
 _O_NQ, _O_NK, _O_NV, _O_GATE) = [int(v) for v in _IN_OFF[:-1]]

U_NQ = 0
U_NK = 1024
U_NV = 2048
U_GATE = 3072
U_Z = 6144
U_XBC = 7168
U_GQ = 8704
U_GK = 9216
U_GV = 9728
U_GG = 10752
U_WIDTH = 11776
INPROJ_COL_CHUNK = 1024
LANE = 128
VMEM_LIMIT = 56 * 1024 * 1024

SMALL_W = SSD_GROUPS * LANE
SM_DTF, SM_DTB, SM_GAF, SM_GAB = 0, 8, 16, 32
SMALL_T_ROWS = 2 * SSD_HG

_BIG_SEGS = ((_O_NQ, 6144), (_O_Z, 2560), (_O_GQ, 3072))


def _cparams(sem, vmem=VMEM_LIMIT):
    return pltpu.CompilerParams(dimension_semantics=sem, vmem_limit_bytes=vmem)


def _sigmoid(x):
    return 1.0 / (1.0 + jnp.exp2(x * (-LOG2E)))


def _silu(x):
    return x * _sigmoid(x)


def _softplus(x):
    return jnp.maximum(x, 0.0) + jnp.log1p(jnp.exp(-jnp.abs(x)))


def _neg_log2_sigmoid(x):
    t = x * (-LOG2E)
    return jnp.maximum(t, 0.0) + jnp.log2(1.0 + jnp.exp2(-jnp.abs(t)))


def _nt(a, b):
    return lax.dot_general(a, b, (((1,), (1,)), ((), ())), preferred_element_type=F32)


def _tn(a, b):
    return lax.dot_general(a, b, (((0,), (0,)), ((), ())), preferred_element_type=F32)


def _dot(a, b):
    return jnp.dot(a, b, preferred_element_type=F32)


def _iota2(shape, dim):
    return lax.broadcasted_iota(jnp.int32, shape, dim)


def _wperm_kernel(wt_ref, o_ref, ws_ref):
    off = 0
    for a, n in _BIG_SEGS:
        o_ref[off:off + n, :] = wt_ref[0, a:a + n, :].astype(BF16)
        off += n
    ws_ref[...] = jnp.zeros_like(ws_ref)
    for g in range(SSD_GROUPS):
        base, h0 = g * LANE, g * SSD_HG
        ws_ref[base + SM_DTF:base + SM_DTF + SSD_HG, :] = wt_ref[0, _O_DTF + h0:_O_DTF + h0 + SSD_HG, :]
        ws_ref[base + SM_DTB:base + SM_DTB + SSD_HG, :] = wt_ref[0, _O_DTB + h0:_O_DTB + h0 + SSD_HG, :]
    ws_ref[SM_GAF:SM_GAF + GLA_GATE_RANK, :] = wt_ref[0, _O_GAF:_O_GAF + GLA_GATE_RANK, :]
    ws_ref[SM_GAB:SM_GAB + GLA_GATE_RANK, :] = wt_ref[0, _O_GAB:_O_GAB + GLA_GATE_RANK, :]


def _permute_weight(w_t_all, layer, tc=256):
    _, n_in, d = w_t_all.shape
    return pl.pallas_call(
        _wperm_kernel,
        grid=(d // tc,),
        in_specs=[pl.BlockSpec((1, n_in, tc), lambda i: (layer, 0, i))],
        out_specs=[
            pl.BlockSpec((U_WIDTH, tc), lambda i: (0, i)),
            pl.BlockSpec((SMALL_W, tc), lambda i: (0, i)),
        ],
        out_shape=[
            jax.ShapeDtypeStruct((U_WIDTH, d), BF16),
            jax.ShapeDtypeStruct((SMALL_W, d), F32),
        ],
        compiler_params=_cparams(("parallel",)),
        name="w_permute",
    )(w_t_all)


def _head_rms(r, w_row):
    G = 2 * LANE
    er = _iota2((G, G), 0) // NA_HEAD_DIM
    ec = _iota2((G, G), 1) // NA_HEAD_DIM
    e_blk = jnp.where(er == ec, 1.0 / NA_HEAD_DIM, 0.0).astype(BF16)
    outs = []
    for a in range(0, r.shape[1], G):
        x = r[:, a:a + G]
        ms = _dot((x * x).astype(BF16), e_blk)
        outs.append(x * lax.rsqrt(ms + EPS))
    return jnp.concatenate(outs, axis=1) * w_row


def _inproj_kernel(x_ref, nw_ref, w_ref, ws_ref, qkw_ref, u_ref, us_ref, ust_ref, h_ref):
    tn = u_ref.shape[1]
    chunks = [(a, min(a + INPROJ_COL_CHUNK, tn)) for a in range(0, tn, INPROJ_COL_CHUNK)]

    @pl.when(pl.program_id(1) == 0)
    def _():
        x = x_ref[...]
        ms = jnp.mean(x * x, axis=-1, keepdims=True)
        h = (x * lax.rsqrt(ms + EPS) * nw_ref[...]).astype(BF16)
        h_ref[...] = h
        ws = ws_ref[...].astype(BF16)
        us_ref[...] = _nt(h, ws)
        dt_rows = jnp.concatenate([ws[g * LANE:g * LANE + SMALL_T_ROWS] for g in range(SSD_GROUPS)], axis=0)
        ust_ref[...] = _nt(dt_rows, h)
        for c, (a, b) in enumerate(chunks):
            r = _nt(h, w_ref[a:b, :])
            if c < 2:
                r = _head_rms(r, qkw_ref[c:c + 1, :])
            u_ref[:, a:b] = r.astype(BF16)

    @pl.when(pl.program_id(1) != 0)
    def _():
        for a, b in chunks:
            u_ref[:, a:b] = _nt(h_ref[...], w_ref[a:b, :]).astype(BF16)


def _inproj(x2, nw, w_big, w_small, qkw, tm=1024, tn=U_WIDTH // 4):
    assert (U_NQ, U_NK) == (0, INPROJ_COL_CHUNK) and NA_W == INPROJ_COL_CHUNK
    T = x2.shape[0]
    return pl.pallas_call(
        _inproj_kernel,
        grid=(T // tm, U_WIDTH // tn),
        in_specs=[
            pl.BlockSpec((tm, D_MODEL), lambda i, j: (i, 0)),
            pl.BlockSpec((1, D_MODEL), lambda i, j: (0, 0)),
            pl.BlockSpec((tn, D_MODEL), lambda i, j: (j, 0)),
            pl.BlockSpec((SMALL_W, D_MODEL), lambda i, j: (0, 0)),
            pl.BlockSpec((8, NA_W), lambda i, j: (0, 0)),
        ],
        out_specs=[
            pl.BlockSpec((tm, tn), lambda i, j: (i, j)),
            pl.BlockSpec((tm, SMALL_W), lambda i, j: (i, 0)),
            pl.BlockSpec((SSD_GROUPS * SMALL_T_ROWS, tm), lambda i, j: (0, i)),
        ],
        out_shape=[
            jax.ShapeDtypeStruct((T, U_WIDTH), BF16),
            jax.ShapeDtypeStruct((T, SMALL_W), F32),
            jax.ShapeDtypeStruct((SSD_GROUPS * SMALL_T_ROWS, T), F32),
        ],
        scratch_shapes=[pltpu.VMEM((tm, D_MODEL), BF16)],
        compiler_params=_cparams(("parallel", "arbitrary")),
        name="inproj",
    )(x2, nw, w_big, w_small, qkw)


def _conv_kernel(*refs):
    u_refs, (w_ref, b_ref, o_ref, xp_ref) = refs[:CONV_IN_BLOCKS], refs[CONV_IN_BLOCKS:]
    S, wb = u_refs[0].shape[1], u_refs[0].shape[2]
    C = o_ref.shape[2]
    R = CONV_ROWS
    tc = CONV_TILE
    pad = SSD_CONV // 2
    xp_ref[0:R, :] = jnp.zeros((R, C), BF16)
    xp_ref[R + S:R + S + R, :] = jnp.zeros((R, C), BF16)
    for i, u_ref in enumerate(u_refs):
        xp_ref[R:R + S, i * wb:(i + 1) * wb] = u_ref[0]
    t_i = _iota2((R, 2 * R), 0)
    j_i = _iota2((R, 2 * R), 1)
    side = [k for k in range(SSD_CONV) if k != pad]
    shifts = jnp.concatenate([(j_i == t_i + R // 2 + (k - pad)).astype(BF16) for k in side], axis=0)
    for blk in range(S // R):
        for c0 in range(0, C, tc):
            cs = slice(c0, c0 + tc)
            win = xp_ref[blk * R + R // 2:blk * R + R // 2 + 2 * R, cs]
            sh = _dot(shifts, win)
            acc = b_ref[:, cs] + w_ref[pad:pad + 1, cs] * xp_ref[R + blk * R:R + (blk + 1) * R, cs].astype(F32)
            for n, k in enumerate(side):
                acc = acc + w_ref[k:k + 1, cs] * sh[n * R:(n + 1) * R]
            o_ref[0, blk * R:(blk + 1) * R, cs] = _silu(acc).astype(BF16)


def _conv(u3, conv_w, conv_b):
    B, S, _ = u3.shape
    wb = SSD_CONV_DIM // CONV_IN_BLOCKS
    assert U_XBC % wb == 0 and wb % CONV_TILE == 0
    off = U_XBC // wb
    in_blocks = [pl.BlockSpec((1, S, wb), functools.partial(lambda b, i: (b, 0, off + i), i=i))
                 for i in range(CONV_IN_BLOCKS)]
    return pl.pallas_call(
        _conv_kernel,
        grid=(B,),
        in_specs=in_blocks + [
            pl.BlockSpec((SSD_CONV, SSD_CONV_DIM), lambda b: (0, 0)),
            pl.BlockSpec((1, SSD_CONV_DIM), lambda b: (0, 0)),
        ],
        out_specs=pl.BlockSpec((1, S, SSD_CONV_DIM), lambda b: (b, 0, 0)),
        out_shape=jax.ShapeDtypeStruct((B, S, SSD_CONV_DIM), BF16),
        scratch_shapes=[pltpu.VMEM((S + 2 * CONV_ROWS, SSD_CONV_DIM), BF16)],
        compiler_params=_cparams(("parallel",)),
        name="ssd_conv",
    )(*([u3] * CONV_IN_BLOCKS), conv_w, conv_b)


def _split_hi_lo(v):
    hi = v.astype(BF16)
    lo = (v - hi.astype(F32)).astype(BF16)
    return jnp.concatenate([hi, lo], axis=1)


def _split3(v, axis):
    hi = v.astype(BF16)
    r1 = v - hi.astype(F32)
    mid = r1.astype(BF16)
    lo = (r1 - mid.astype(F32)).astype(BF16)
    return jnp.concatenate([hi, mid, lo], axis=axis)


def _ssd_kernel(x_ref, b_ref, c_ref, z_ref, ust_ref, prow_ref, pcol_ref, drow_ref, nw_ref,
                o_ref, acc_ref, cumc_ref, cumr_ref, wdt_ref, dec_ref, xs_ref, decx_ref, sst_ref, st_ref):
    L = SSD_CHUNK
    S = x_ref.shape[1]
    nc = S // L
    W = SSD_GROUP_W
    HG = SSD_HG
    R = 2 * HG

    ii = _iota2((L, L), 0)
    jj = _iota2((L, L), 1)
    tril = ii >= jj
    triu = jj >= ii
    tril_b = tril.astype(BF16)
    triu_b = triu.astype(BF16)

    er = _iota2((LANE, 2 * W), 0)
    ec = _iota2((LANE, 2 * W), 1)
    e = (er == jnp.where(ec < W, ec // SSD_HEAD_DIM, (ec - W) // SSD_HEAD_DIM + HG)).astype(BF16)
    e2 = jnp.concatenate([e, e], axis=0)

    a_row = prow_ref[1:2, :]
    bias_col = pcol_ref[0, :, 0:1]
    a_col = pcol_ref[0, :, 1:2]

    lane_fwd = _iota2((L, LANE), 1) < HG
    row_fwd = _iota2((R, L), 0) < HG
    lane_half = _iota2((L, LANE), 1) < SSD_HEAD_DIM

    def decay_sums(c):
        rs = pl.ds(pl.multiple_of(c * L, L), L)
        dt_r = _softplus(ust_ref[:, rs] + bias_col)
        a_r = dt_r * a_col
        dt_c = jnp.concatenate([dt_r, jnp.zeros((L - R, L), F32)], axis=0).T
        a_c = dt_c * a_row
        return dt_c, a_c, _dot(tril_b, _split3(a_c, 1)), dt_r, a_r, _dot(_split3(a_r, 0), triu_b)

    def decay_store(c, sums):
        rs = pl.ds(pl.multiple_of(c * L, L), L)
        dt_c, a_c, pp, dt_r, a_r, pr = sums
        p_c = pp[:, 0:LANE] + pp[:, LANE:2 * LANE] + pp[:, 2 * LANE:3 * LANE]
        tot_c = p_c[L - 1:L, :]
        cum_c = jnp.where(lane_fwd, p_c, tot_c - p_c + a_c)
        cumc_ref[rs, :] = cum_c * LOG2E
        wdt_ref[rs, :] = jnp.exp(tot_c - cum_c) * dt_c
        dec_ref[c] = jnp.broadcast_to(jnp.exp(tot_c), (16, LANE))
        p_r = pr[0:R] + pr[R:2 * R] + pr[2 * R:3 * R]
        tot_r = p_r[:, L - 1:L]
        cum_r = jnp.where(row_fwd, p_r, tot_r - p_r + a_r)
        cumr_ref[:, rs] = (cum_r - jnp.log(dt_r)) * LOG2E

    def intra_chunk(c):
        rs = pl.ds(pl.multiple_of(c * L, L), L)
        x_c = x_ref[0, rs, :]
        b_c = b_ref[0, rs, :]
        cb = _nt(c_ref[0, rs, :], b_c)
        ex = _dot(_split_hi_lo(jnp.concatenate([wdt_ref[rs, :], dec_ref[c]], axis=0)), e2)
        decx_ref[c] = ex[L:L + 8]
        xf = x_c.astype(F32)
        xw = jnp.concatenate([xf * ex[0:L, 0:W], xf * ex[0:L, W:2 * W]], axis=1).astype(BF16)
        xs_ref[c] = _tn(b_c, xw)
        cum_c = cumc_ref[rs, :]
        cum_r = cumr_ref[:, rs]
        for hp in range(HG // 2):
            ms = []
            for hh in range(2):
                hf = 2 * hp + hh
                hb = HG + 2 * hp + hh
                decf = jnp.exp2(jnp.where(tril, cum_c[:, hf:hf + 1] - cum_r[hf:hf + 1, :], -jnp.inf))
                decb = jnp.exp2(jnp.where(triu, cum_c[:, hb:hb + 1] - cum_r[hb:hb + 1, :], -jnp.inf))
                ms.append((cb * (decf + decb)).astype(BF16))
            m2 = jnp.concatenate(ms, axis=1)
            xp = x_c[:, hp * LANE:(hp + 1) * LANE]
            xz = jnp.zeros_like(xp)
            x2 = jnp.concatenate([jnp.where(lane_half, xp, xz), jnp.where(lane_half, xz, xp)], axis=0)
            acc_ref[rs, hp * LANE:(hp + 1) * LANE] = _dot(m2, x2)

    G = SSD_CHUNKS_PER_STEP
    n_steps = nc // G

    def fused_body(i, carry):
        sums = [decay_sums(G * (i + 1) + j) for j in range(G)]
        for j in range(G):
            intra_chunk(G * i + j)
        for j in range(G):
            decay_store(G * (i + 1) + j, sums[j])
        return carry

    for j in range(G):
        decay_store(j, decay_sums(j))
    lax.fori_loop(0, n_steps - 1, fused_body, 0, unroll=True)
    for j in range(G):
        intra_chunk(G * (n_steps - 1) + j)

    st_ref[...] = jnp.zeros_like(st_ref)

    def state_body(i, carry):
        for d, c in enumerate((i, nc - 1 - i)):
            ls = slice(d * W, (d + 1) * W)
            s_prev = st_ref[:, ls]
            sst_ref[c, :, ls] = s_prev.astype(BF16)
            st_ref[:, ls] = s_prev * decx_ref[c, 0:1, ls] + xs_ref[c, :, ls]
        return carry

    lax.fori_loop(0, nc, state_body, 0, unroll=True)

    def out_body(c, carry):
        rs = pl.ds(pl.multiple_of(c * L, L), L)
        c_c = c_ref[0, rs, :]
        ex = _split_hi_lo(jnp.exp2(cumc_ref[rs, :]))
        y = acc_ref[rs, :] + x_ref[0, rs, :].astype(F32) * drow_ref[...]
        for d in range(2):
            ls = slice(d * W, (d + 1) * W)
            y = y + _dot(c_c, sst_ref[c, :, ls]) * _dot(ex, e2[:, ls])
        y = y * _silu(z_ref[0, rs, :].astype(F32))
        ms = jnp.mean(y * y, axis=-1, keepdims=True)
        o_ref[0, rs, :] = (y * lax.rsqrt(ms + EPS) * nw_ref[...]).astype(BF16)
        return carry

    lax.fori_loop(0, nc, out_body, 0, unroll=SSD_OUT_UNROLL)


def _ssd(xbc, u3, ust, prow, pcol, drow, nw):
    B, S, _ = xbc.shape
    W = SSD_GROUP_W
    N = SSD_STATE
    nc = S // SSD_CHUNK
    return pl.pallas_call(
        _ssd_kernel,
        grid=(B, SSD_GROUPS),
        in_specs=[
            pl.BlockSpec((1, S, W), lambda b, g: (b, 0, g)),
            pl.BlockSpec((1, S, N), lambda b, g: (b, 0, SSD_D_INNER // N + g)),
            pl.BlockSpec((1, S, N), lambda b, g: (b, 0, SSD_D_INNER // N + SSD_GROUPS + g)),
            pl.BlockSpec((1, S, W), lambda b, g: (b, 0, U_Z // W + g)),
            pl.BlockSpec((SMALL_T_ROWS, S), lambda b, g: (g, b)),
            pl.BlockSpec((8, LANE), lambda b, g: (g, 0)),
            pl.BlockSpec((1, SMALL_T_ROWS, 8), lambda b, g: (g, 0, 0)),
            pl.BlockSpec((1, W), lambda b, g: (0, g)),
            pl.BlockSpec((1, W), lambda b, g: (0, g)),
        ],
        out_specs=pl.BlockSpec((1, S, W), lambda b, g: (b, 0, g)),
        out_shape=jax.ShapeDtypeStruct((B, S, SSD_D_INNER), BF16),
        scratch_shapes=[
            pltpu.VMEM((S, W), F32),
            pltpu.VMEM((S, LANE), F32),
            pltpu.VMEM((SMALL_T_ROWS, S), F32),
            pltpu.VMEM((S, LANE), F32),
            pltpu.VMEM((nc, 16, LANE), F32),
            pltpu.VMEM((nc, N, 2 * W), F32),
            pltpu.VMEM((nc, 8, 2 * W), F32),
            pltpu.VMEM((nc, N, 2 * W), BF16),
            pltpu.VMEM((N, 2 * W), F32),
        ],
        compiler_params=_cparams(("parallel", "parallel")),
        name="ssd_scan",
    )(xbc, xbc, xbc, u3, ust, prow, pcol, drow, nw)


def _gla_kernel(q_ref, k_ref, v_ref, gg_ref, us_ref, a2_ref, bias_ref, nw_ref,
                o_ref, acc_ref, g_ref, qd_ref, kd_ref, kdp_ref, qcat_ref, x_ref, dec_ref, sst_ref):
    L = GLA_CHUNK
    BL = 2 * L
    DK = GLA_DK
    S = q_ref.shape[1]
    nb = S // BL
    scale = DK ** -0.5

    ii = _iota2((BL, BL), 0)
    jj = _iota2((BL, BL), 1)
    same = (ii // L) == (jj // L)
    masks = (same & (ii >= jj), same & (jj >= ii))
    tri2 = masks[0].astype(BF16)
    par_row = _iota2((BL, DK), 0) // L

    ga = us_ref[0].astype(BF16)
    g_ref[...] = _neg_log2_sigmoid(_dot(ga, a2_ref[...]) + bias_ref[...]) * (-1.0 / GLA_GATE_NORM)

    def decay_sums(i):
        g = g_ref[pl.ds(pl.multiple_of(i * BL, BL), BL), :]
        return g, _dot(tri2, _split_hi_lo(g))

    def decay_block(i, sums):
        rs = pl.ds(pl.multiple_of(i * BL, BL), BL)
        g, pp = sums
        p = pp[:, 0:2 * DK] + pp[:, 2 * DK:4 * DK]
        q_c = q_ref[0, rs, :].astype(F32) * scale
        k_c = k_ref[0, rs, :].astype(F32)
        zero = jnp.zeros((BL, DK), BF16)
        for d in range(2):
            p_d = p[:, d * DK:(d + 1) * DK]
            tot = jnp.where(par_row == 0, p_d[L - 1:L, :], p_d[BL - 1:BL, :])
            b = p_d if d == 0 else tot - p_d + g[:, DK:]
            qd = (q_c * jnp.exp2(b)).astype(BF16)
            kdec = (k_c * jnp.exp2(tot - b)).astype(BF16)
            qd_ref[d, rs, :] = qd
            kd_ref[d, rs, :] = (k_c * jnp.exp2(-b)).astype(BF16)
            for par in range(2):
                sel = par_row == par
                kdp_ref[d, rs, par * DK:(par + 1) * DK] = jnp.where(sel, kdec, zero)
                qcat_ref[rs, (2 * d + par) * DK:(2 * d + par + 1) * DK] = jnp.where(sel, qd, zero)
                last = (par + 1) * L - 1
                dec_ref[d, 2 * i + par] = jnp.broadcast_to(jnp.exp2(p_d[last:last + 1, :]), (8, DK))

    G = GLA_BLOCKS_PER_STEP
    n_groups = nb // G

    def intra_group(i):
        blks = [G * i + j for j in range(G)]
        rss = [pl.ds(pl.multiple_of(b * BL, BL), BL) for b in blks]
        vs = [v_ref[0, rs, :] for rs in rss]
        atts = [[_nt(qd_ref[d, rs, :], kd_ref[d, rs, :]) for d in range(2)] for rs in rss]
        for j, b in enumerate(blks):
            for d in range(2):
                x_ref[d, b] = _tn(vs[j], kdp_ref[d, rss[j], :])
        for j in range(G):
            att = jnp.where(masks[0], atts[j][0], 0.0) + jnp.where(masks[1], atts[j][1], 0.0)
            acc_ref[rss[j], :] = _dot(att.astype(BF16), vs[j])

    def fused_body(i, carry):
        sums = [decay_sums(G * (i + 1) + j) for j in range(G)]
        intra_group(i)
        for j in range(G):
            decay_block(G * (i + 1) + j, sums[j])
        return carry

    for j in range(G):
        decay_block(j, decay_sums(j))
    lax.fori_loop(0, n_groups - 1, fused_body, 0, unroll=True)
    intra_group(n_groups - 1)

    def state_body(d):
        def body(i, s):
            b = i if d == 0 else nb - 1 - i
            for par in ((0, 1) if d == 0 else (1, 0)):
                lane0 = (2 * d + par) * DK
                sst_ref[b, :, lane0:lane0 + DK] = s.astype(BF16)
                s = s * dec_ref[d, 2 * b + par, 0:1, :] + x_ref[d, b, :, par * DK:(par + 1) * DK]
            return s
        return body

    for d in range(2):
        lax.fori_loop(0, nb, state_body(d), jnp.zeros((GLA_DV, DK), F32), unroll=True)

    def out_body(b, carry):
        rs = pl.ds(pl.multiple_of(b * BL, BL), BL)
        o = acc_ref[rs, :] + _nt(qcat_ref[rs, :], sst_ref[b])
        ms = jnp.mean(o * o, axis=-1, keepdims=True)
        o = o * lax.rsqrt(ms + EPS) * nw_ref[...]
        o_ref[0, rs, :] = (o * _silu(gg_ref[0, rs, :].astype(F32))).astype(BF16)
        return carry

    lax.fori_loop(0, nb, out_body, 0, unroll=GLA_OUT_UNROLL)


def _gla_heads_kernel(q_ref, k_ref, v_ref, gg_ref, us_ref, a2_ref, bias_ref, nw_ref, o_ref, *scratch):
    DK, DV = GLA_DK, GLA_DV
    for hh in range(GLA_HEADS_PER_STEP):
        ks, vs, gs = slice(hh * DK, (hh + 1) * DK), slice(hh * DV, (hh + 1) * DV), slice(hh * 2 * DK, (hh + 1) * 2 * DK)
        _gla_kernel(q_ref.at[:, :, ks], k_ref.at[:, :, ks], v_ref.at[:, :, vs], gg_ref.at[:, :, vs], us_ref,
                    a2_ref.at[:, gs], bias_ref.at[:, gs], nw_ref, o_ref.at[:, :, vs], *scratch)


def _gla(u3, us3, a2, bias, nw):
    B, S, _ = u3.shape
    HPS = GLA_HEADS_PER_STEP
    DK, DV = GLA_DK, GLA_DV
    nc = S // GLA_CHUNK
    nb = nc // 2
    return pl.pallas_call(
        _gla_heads_kernel,
        grid=(B, GLA_HEADS // HPS),
        in_specs=[
            pl.BlockSpec((1, S, HPS * DK), lambda b, h: (b, 0, U_GQ // (HPS * DK) + h)),
            pl.BlockSpec((1, S, HPS * DK), lambda b, h: (b, 0, U_GK // (HPS * DK) + h)),
            pl.BlockSpec((1, S, HPS * DV), lambda b, h: (b, 0, U_GV // (HPS * DV) + h)),
            pl.BlockSpec((1, S, HPS * DV), lambda b, h: (b, 0, U_GG // (HPS * DV) + h)),
            pl.BlockSpec((1, S, LANE), lambda b, h: (b, 0, 0)),
            pl.BlockSpec((LANE, HPS * 2 * DK), lambda b, h: (0, h)),
            pl.BlockSpec((1, HPS * 2 * DK), lambda b, h: (0, h)),
            pl.BlockSpec((1, DV), lambda b, h: (0, 0)),
        ],
        out_specs=pl.BlockSpec((1, S, HPS * DV), lambda b, h: (b, 0, h)),
        out_shape=jax.ShapeDtypeStruct((B, S, GLA_VAL_W), BF16),
        scratch_shapes=[
            pltpu.VMEM((S, DV), F32),
            pltpu.VMEM((S, 2 * DK), F32),
            pltpu.VMEM((2, S, DK), BF16),
            pltpu.VMEM((2, S, DK), BF16),
            pltpu.VMEM((2, S, 2 * DK), BF16),
            pltpu.VMEM((S, 4 * DK), BF16),
            pltpu.VMEM((2, nb, DV, 2 * DK), F32),
            pltpu.VMEM((2, nc, 8, DK), F32),
            pltpu.VMEM((nb, DV, 4 * DK), BF16),
        ],
        compiler_params=_cparams(("parallel", "parallel")),
        name="gla_scan",
    )(u3, u3, u3, u3, us3, a2, bias, nw)


def _na_bias_rows(rpb):
    H, R, C = rpb.shape
    n_pos = GRID_W - NA_WIN_W
    n_neg = GRID_W - NA_WIN_W + 2
    ext = jnp.concatenate([rpb[:, :, NA_WIN_W - 1:], jnp.repeat(rpb[:, :, C - 1:], n_pos, axis=2),
                           jnp.repeat(rpb[:, :, 0:1], n_neg, axis=2), rpb[:, :, 1:NA_WIN_W - 1]], axis=2)
    return jnp.pad(ext, ((0, 0), (0, 1), (0, 0))).reshape(H // 2, 2, R + 1, 2 * GRID_W)


def _na_kernel(flag_ref, q_ref, k_ref, v_ref, ext_ref, o_ref, vx_ref, tab_ref):
    NB, S, _ = q_ref.shape
    rows = S // GRID_W
    total = NB * rows
    win_h = NA_WIN_H
    nk = win_h * GRID_W

    vx_ref[:, :, 0:LANE] = v_ref[...]
    vx_ref[:, :, LANE:2 * LANE] = jnp.ones((NB, S, LANE), BF16)
    bound_ok = flag_ref[0] != 0

    first_q = _iota2((GRID_W, LANE), 1) < NA_HEAD_DIM

    @pl.when(pl.program_id(1) == 0)
    def _():
        q_col = _iota2((GRID_W, 2 * GRID_W), 0)
        k_col = _iota2((GRID_W, 2 * GRID_W), 1) % GRID_W
        w_start = jnp.clip(q_col - NA_WIN_W // 2, 0, GRID_W - NA_WIN_W)
        in_window = (k_col >= w_start) & (k_col < w_start + NA_WIN_W)
        low = _iota2((GRID_W, 2 * GRID_W), 1) < GRID_W
        for hd in range(2):
            def skewed(rr, shift):
                row = jnp.broadcast_to(ext_ref[0, hd, rr:rr + 1, :], (GRID_W, 2 * GRID_W))
                return pltpu.roll(row, shift, axis=1, stride=1, stride_axis=0)

            for e in range(2 * NA_WIN_H - 2):
                t = jnp.where(low, skewed(e, 0), skewed(e + 1, GRID_W))
                tab_ref[e, hd * GRID_W:(hd + 1) * GRID_W, :] = jnp.where(in_window, t, -jnp.inf)

    def locate(r):
        bi = r // rows
        rl = r - bi * rows
        r0 = jnp.clip(rl - win_h // 2, 0, rows - win_h)
        return bi, pl.multiple_of(rl * GRID_W, GRID_W), rl - r0, pl.multiple_of(r0 * GRID_W, GRID_W)

    def scores(r):
        bi, q0, delta, k0 = locate(r)
        q = q_ref[bi, pl.ds(q0, GRID_W), :]
        zero = jnp.zeros_like(q)
        qs = jnp.concatenate([jnp.where(first_q, q, zero), jnp.where(first_q, zero, q)], axis=0)
        rr0 = (win_h - 1) - delta
        bias = jnp.concatenate([tab_ref[rr0 + w] for w in range(0, win_h, 2)], axis=1)
        return _nt(qs, k_ref[bi, pl.ds(k0, nk), :]) + bias

    def probs_exact(s):
        return jnp.exp2(s - jnp.max(s, axis=-1, keepdims=True)).astype(BF16)

    def probs_bounded(r):
        return jnp.exp2(scores(r)).astype(BF16)

    def attend(r, p):
        bi, q0, _, k0 = locate(r)
        ox = _dot(p, vx_ref[bi, pl.ds(k0, nk), :])
        o = ox[:, 0:LANE] / ox[:, LANE:2 * LANE]
        o_ref[bi, pl.ds(q0, GRID_W), :] = jnp.where(first_q, o[:GRID_W], o[GRID_W:]).astype(BF16)

    @pl.when(bound_ok)
    def _():
        U = NA_ROWS_PER_STEP

        def row_body(i, p_prev):
            r = i * U
            for j in range(U):
                attend(r - U + j, p_prev[j])
            return tuple(probs_bounded(r + j) for j in range(U))

        p_last = lax.fori_loop(1, total // U, row_body, tuple(probs_bounded(j) for j in range(U)), unroll=True)
        for j in range(U):
            attend(total - U + j, p_last[j])

    @pl.when(jnp.logical_not(bound_ok))
    def _():
        U = NA_ROWS_PER_STEP_EXACT

        def row_body(i, carry):
            s_cur, p_prev = carry
            r = i * U
            for j in range(U):
                attend(r - U + j, p_prev[j])
            p = tuple(probs_exact(s) for s in s_cur)
            s_next = tuple(scores(jnp.minimum(r + U + j, total - 1)) for j in range(U))
            return s_next, p

        p0 = tuple(probs_exact(scores(j)) for j in range(U))
        s1 = tuple(scores(U + j) for j in range(U))
        _, p_last = lax.fori_loop(1, total // U, row_body, (s1, p0))
        for j in range(U):
            attend(total - U + j, p_last[j])


def _na_score_bound(rpb, q_norm_w, k_norm_w):
    H = rpb.shape[0]
    qk = (NA_HEAD_DIM ** 0.5 * LOG2E * (1.0 + NA_BOUND_SLACK)) * jnp.max(jnp.abs(q_norm_w)) * jnp.max(jnp.abs(k_norm_w))
    b_max = jnp.max(rpb.reshape(H, -1), axis=1) * LOG2E
    b_self = rpb[:, NA_WIN_H - 1, NA_WIN_W - 1] * LOG2E
    bound = qk + b_max
    flag = jnp.all(bound - (b_self - qk) <= NA_MAX_BOUND_GAP)
    return bound, flag.astype(jnp.int32).reshape(1)


def _na(u3, bias_rows, flag):
    B, S, _ = u3.shape
    assert S // GRID_W >= NA_WIN_H and NA_WIN_H % 2 == 0 and 2 * GRID_W == LANE
    nb = math.gcd(B, NA_BATCH_PER_STEP)
    assert (nb * (S // GRID_W)) % NA_ROWS_PER_STEP == 0
    return pl.pallas_call(
        _na_kernel,
        grid=(NA_HEADS // 2, B // nb),
        in_specs=[
            pl.BlockSpec(memory_space=pltpu.SMEM),
            pl.BlockSpec((nb, S, LANE), lambda h, b: (b, 0, U_NQ // LANE + h)),
            pl.BlockSpec((nb, S, LANE), lambda h, b: (b, 0, U_NK // LANE + h)),
            pl.BlockSpec((nb, S, LANE), lambda h, b: (b, 0, U_NV // LANE + h)),
            pl.BlockSpec((1, 2, 2 * NA_WIN_H, 2 * GRID_W), lambda h, b: (h, 0, 0, 0)),
        ],
        out_specs=pl.BlockSpec((nb, S, LANE), lambda h, b: (b, 0, h)),
        out_shape=jax.ShapeDtypeStruct((B, S, NA_W), BF16),
        scratch_shapes=[
            pltpu.VMEM((nb, S, 2 * LANE), BF16),
            pltpu.VMEM((2 * NA_WIN_H - 2, 2 * GRID_W, 2 * GRID_W), F32),
        ],
        compiler_params=_cparams(("parallel", "arbitrary")),
        name="na_attn",
    )(flag, u3, u3, u3, bias_rows)


def _merge_kernel(x_ref, ys_ref, yg_ref, yn_ref, gate_ref, ws_ref, wg_ref, wn_ref, wo_ref, o_ref):
    D = D_MODEL
    mixed = _sigmoid(gate_ref[:, 0:D].astype(F32)) * _dot(ys_ref[...], ws_ref[...])
    mixed += _sigmoid(gate_ref[:, D:2 * D].astype(F32)) * _dot(yg_ref[...], wg_ref[...])
    mixed += _sigmoid(gate_ref[:, 2 * D:3 * D].astype(F32)) * _dot(yn_ref[...], wn_ref[...])
    o_ref[...] = x_ref[...] + _dot(mixed.astype(BF16), wo_ref[...])


def _mlp_kernel(x_ref, nw_ref, w1_ref, w2_ref, o_ref, *, tf):
    x = x_ref[...]
    ms = jnp.mean(x * x, axis=-1, keepdims=True)
    h = (x * lax.rsqrt(ms + EPS) * nw_ref[...]).astype(BF16)
    acc = x
    for f in range(D_FF // tf):
        a = jnp.maximum(_dot(h, w1_ref[:, f * tf:(f + 1) * tf]), 0.0)
        acc = acc + _dot((a * a).astype(BF16), w2_ref[f * tf:(f + 1) * tf, :])
    o_ref[...] = acc


def _merge_mlp_kernel(x_ref, ys_ref, yg_ref, yn_ref, gate_ref, ws_ref, wg_ref, wn_ref, wo_ref, nw_ref, w1_ref, w2_ref,
                      o_ref, xm_ref, *, tf):
    _merge_kernel(x_ref, ys_ref, yg_ref, yn_ref, gate_ref, ws_ref, wg_ref, wn_ref, wo_ref, xm_ref)
    _mlp_kernel(xm_ref, nw_ref, w1_ref, w2_ref, o_ref, tf=tf)


def _merge_mlp(x2, ys, yg, yn, u2, ws, wg, wn, wo, nw, w1, w2, tm=512, tf=1024):
    T = x2.shape[0]
    D = D_MODEL
    row = lambda i: (i, 0)
    resident = pl.Buffered(1)
    weight = lambda shape: pl.BlockSpec(shape, lambda i: (0, 0), pipeline_mode=resident)
    return pl.pallas_call(
        functools.partial(_merge_mlp_kernel, tf=tf),
        grid=(T // tm,),
        in_specs=[
            pl.BlockSpec((tm, D), row),
            pl.BlockSpec((tm, D), row),
            pl.BlockSpec((tm, D), row),
            pl.BlockSpec((tm, D), row),
            pl.BlockSpec((tm, N_BRANCH * D), lambda i: (i, U_GATE // (N_BRANCH * D))),
            weight((D, D)), weight((D, D)), weight((D, D)), weight((D, D)),
            pl.BlockSpec((1, D), lambda i: (0, 0)),
            weight((D, D_FF)), weight((D_FF, D)),
        ],
        out_specs=pl.BlockSpec((tm, D), row),
        out_shape=jax.ShapeDtypeStruct((T, D), F32),
        scratch_shapes=[pltpu.VMEM((tm, D), F32)],
        compiler_params=_cparams(("parallel",)),
        name="merge_mlp",
    )(x2, ys, yg, yn, u2, ws, wg, wn, wo, nw, w1, w2)


def _pad_rows(w, start, total):
    return jnp.zeros((total, w.shape[1]), w.dtype).at[start:start + w.shape[0]].set(w)


def kernel(x, norm_mix_w, w_in, ssd_conv_w, ssd_conv_b, ssd_dt_bias_f, ssd_dt_bias_b, ssd_a_log_f,
           ssd_a_log_b, ssd_d, ssd_norm_w, gla_a2_f, gla_a2_bias_f, gla_a2_b, gla_a2_bias_b,
           gla_norm_w, na_q_norm_w, na_k_norm_w, na_rpb, w_branch_ssd, w_branch_gla, w_branch_na,
           w_out, norm_mlp_w, w_ff1, w_ff2):
    B, S, D = x.shape
    T = B * S
    depth = w_in.shape[0]
    x2 = x.reshape(T, D)
    w_in_t = jnp.swapaxes(w_in, 1, 2)
    for l in range(depth):
        w_big, w_small = _permute_weight(w_in_t, l)
        hg = (SSD_GROUPS, SSD_HG)
        zeros_r = jnp.zeros((SSD_GROUPS, LANE - 2 * SSD_HG), F32)
        dt_bias = jnp.concatenate([ssd_dt_bias_f[l].reshape(hg), ssd_dt_bias_b[l].reshape(hg), zeros_r], axis=1)
        a_neg = jnp.concatenate([-jnp.exp(ssd_a_log_f[l]).reshape(hg), -jnp.exp(ssd_a_log_b[l]).reshape(hg),
                                 zeros_r], axis=1)
        prow3 = jnp.concatenate([dt_bias[:, None], a_neg[:, None], jnp.zeros((SSD_GROUPS, 6, LANE), F32)], axis=1)
        prow = prow3.reshape(SSD_GROUPS * 8, LANE)
        pcol = jnp.transpose(prow3[:, :, :SMALL_T_ROWS], (0, 2, 1))
        drow = jnp.repeat(ssd_d[l], SSD_HEAD_DIM)[None, :]
        hk = (GLA_HEADS, 1, GLA_DK)
        a2 = jnp.concatenate([_pad_rows(gla_a2_f[l], SM_GAF, LANE).reshape((LANE,) + hk),
                              _pad_rows(gla_a2_b[l], SM_GAB, LANE).reshape((LANE,) + hk)],
                             axis=2).reshape(LANE, -1).astype(BF16)
        a2_bias = jnp.concatenate([gla_a2_bias_f[l].reshape(hk), gla_a2_bias_b[l].reshape(hk)],
                                  axis=1).reshape(1, -1)
        na_bound, na_flag = _na_score_bound(na_rpb[l], na_q_norm_w[l], na_k_norm_w[l])
        table = _na_bias_rows(na_rpb[l] * LOG2E - na_bound[:, None, None])
        q_row = jnp.tile(na_q_norm_w[l] * (NA_HEAD_DIM ** -0.5 * LOG2E), NA_HEADS)
        k_row = jnp.tile(na_k_norm_w[l], NA_HEADS)
        qkw = jnp.concatenate([q_row[None], k_row[None], jnp.zeros((6, NA_W), F32)], axis=0)

        u2, us2, ust = _inproj(x2, norm_mix_w[l][None, :], w_big, w_small, qkw)
        u3 = u2.reshape(B, S, U_WIDTH)
        us3 = us2.reshape(B, S, SMALL_W)
        xbc = _conv(u3, ssd_conv_w[l], ssd_conv_b[l][None, :])
        y_ssd = _ssd(xbc, u3, ust, prow, pcol, drow, ssd_norm_w[l][None, :])
        y_gla = _gla(u3, us3, a2, a2_bias, gla_norm_w[l][None, :])
        y_na = _na(u3, table, na_flag)
        x2 = _merge_mlp(x2, y_ssd.reshape(T, -1), y_gla.reshape(T, -1), y_na.reshape(T, -1), u2,
                        w_branch_ssd[l].astype(BF16), w_branch_gla[l].astype(BF16),
                        w_branch_na[l].astype(BF16), w_out[l].astype(BF16),
                        norm_mlp_w[l][None, :], w_ff1[l].astype(BF16), w_ff2[l].astype(BF16))
    return x2.reshape(B, S, D)
```
